```python
import jax, jax.numpy as jnp
from jax import lax
import numpy as np

D_MODEL = 2048
BATCH = 8
SEQ = 8192
DEPTH = 1

CHUNK = 128
A_HEADS = 8
A_WIDTH = D_MODEL
A_HEAD_DIM = A_WIDTH // A_HEADS
B_GROUPS = 16
B_WIDTH = D_MODEL
CONV_WIDTH = 3
N_BRANCH = 2
DN_ALPHA = (2.0 * DEPTH) ** 0.25
DN_BETA = (8.0 * DEPTH) ** -0.25
LN_EPS = 1e-5
IN_WIDTHS = (A_WIDTH, A_WIDTH, A_WIDTH, B_WIDTH, B_WIDTH, B_WIDTH, B_WIDTH, D_MODEL, D_MODEL)
IN_COLS = 3 * A_WIDTH + 4 * B_WIDTH + N_BRANCH * D_MODEL

kernel_name = "hybrid_gated_sgu_shortconv_deepnorm"


def _split_points():
    pts, acc = [], 0
    for w in IN_WIDTHS[:-1]:
        acc += w
        pts.append(acc)
    return pts


def layer_norm(x, g, b):
    xf = x.astype(jnp.float32)
    mu = jnp.mean(xf, axis=-1, keepdims=True)
    var = jnp.mean(jnp.square(xf - mu), axis=-1, keepdims=True)
    y = (xf - mu) * lax.rsqrt(var + LN_EPS)
    return (y * g.astype(jnp.float32) + b.astype(jnp.float32)).astype(x.dtype)


def chunked_sgu(u, v, ln_g, ln_b, w_s, b_s):
    bsz, s, _ = u.shape
    nc = s // CHUNK
    u = jax.nn.gelu(u).reshape(bsz, nc, CHUNK, A_HEADS, A_HEAD_DIM)
    v = jax.nn.gelu(v).reshape(bsz, nc, CHUNK, A_HEADS, A_HEAD_DIM)
    v = layer_norm(v, ln_g.reshape(A_HEADS, A_HEAD_DIM), ln_b.reshape(A_HEADS, A_HEAD_DIM))
    causal = jnp.tril(jnp.ones((CHUNK, CHUNK), dtype=bool))
    w = jnp.where(causal[None], w_s, jnp.zeros_like(w_s))
    mixed = jnp.einsum('hts,bcshd->bcthd', w, v)
    mixed = mixed + jnp.transpose(b_s)[None, None, :, :, None]
    return (u * mixed).reshape(bsz, s, A_WIDTH)


def short_gated_conv(xb, cb, bb, conv_w, conv_b):
    s = xb.shape[1]
    h = cb * xb
    hp = jnp.pad(h, ((0, 0), (CONV_WIDTH - 1, 0), (0, 0)))
    conv = conv_b + conv_w[0] * hp[:, 0:s, :]
    for k in range(1, CONV_WIDTH):
        conv = conv + conv_w[k] * hp[:, k:k + s, :]
    return bb * conv


def _fwd_setup_inputs(seed: int = 0) -> dict:
    key = jax.random.key(seed)
    ks = jax.random.split(key, 16)
    nrm = jax.random.normal
    x = nrm(ks[0], (BATCH, SEQ, D_MODEL), jnp.float32)
    w_in = nrm(ks[1], (DEPTH, D_MODEL, IN_COLS), jnp.float32) * D_MODEL ** -0.5
    b_gate = 0.02 * nrm(ks[2], (DEPTH, N_BRANCH * D_MODEL), jnp.float32)
    ln_v_g = 1.0 + 0.02 * nrm(ks[3], (DEPTH, A_WIDTH), jnp.float32)
    ln_v_b = 0.02 * nrm(ks[4], (DEPTH, A_WIDTH), jnp.float32)
    w_s = nrm(ks[5], (DEPTH, A_HEADS, CHUNK, CHUNK), jnp.float32) * (0.5 * CHUNK ** -0.5)
    b_s = 1.0 + 0.02 * nrm(ks[6], (DEPTH, A_HEADS, CHUNK), jnp.float32)
    conv_w = nrm(ks[7], (DEPTH, CONV_WIDTH, B_WIDTH), jnp.float32) * CONV_WIDTH ** -0.5
    conv_b = 0.02 * nrm(ks[8], (DEPTH, B_WIDTH), jnp.float32)
    w_oa = nrm(ks[9], (DEPTH, A_WIDTH, D_MODEL), jnp.float32) * (A_WIDTH ** -0.5 * DN_BETA)
    w_ob = nrm(ks[10], (DEPTH, B_WIDTH, D_MODEL), jnp.float32) * (B_WIDTH ** -0.5 * DN_BETA)
    w_out = nrm(ks[11], (DEPTH, D_MODEL, D_MODEL), jnp.float32) * (D_MODEL ** -0.5 * DN_BETA)
    ln_g = 1.0 + 0.02 * nrm(ks[12], (DEPTH, D_MODEL), jnp.float32)
    ln_b = 0.02 * nrm(ks[13], (DEPTH, D_MODEL), jnp.float32)
    return {"x": x, "w_in": w_in, "b_gate": b_gate, "ln_v_g": ln_v_g, "ln_v_b": ln_v_b,
            "w_s": w_s, "b_s": b_s, "conv_w": conv_w, "conv_b": conv_b,
            "w_oa": w_oa, "w_ob": w_ob, "w_out": w_out, "ln_g": ln_g, "ln_b": ln_b}


def _fwd_reference(x, w_in, b_gate, ln_v_g, ln_v_b, w_s, b_s, conv_w, conv_b,
              w_oa, w_ob, w_out, ln_g, ln_b):
    splits = _split_points()
    for l in range(DEPTH):
        p = jnp.einsum('bsd,dc->bsc', x, w_in[l])
        ua, va, za, xb, cb, bb, zb, ga, gb = jnp.split(p, splits, axis=-1)
        ya = chunked_sgu(ua, va, ln_v_g[l], ln_v_b[l], w_s[l], b_s[l]) * jax.nn.silu(za)
        yb = short_gated_conv(xb, cb, bb, conv_w[l], conv_b[l]) * jax.nn.silu(zb)
        gate_a = jax.nn.sigmoid(ga + b_gate[l, :D_MODEL])
        gate_b = jax.nn.sigmoid(gb + b_gate[l, D_MODEL:])
        merged = (gate_a * jnp.einsum('bse,ed->bsd', ya, w_oa[l])
                  + gate_b * jnp.einsum('bse,ed->bsd', yb, w_ob[l]))
        out = jnp.einsum('bsd,de->bse', merged, w_out[l])
        x = layer_norm(DN_ALPHA * x + out, ln_g[l], ln_b[l])
    return x


import jax as _jax
import jax.numpy as _jnp

TWIN_FORMAT = 'train_step'
FWD_PARAMS = ['x', 'w_in', 'b_gate', 'ln_v_g', 'ln_v_b', 'w_s', 'b_s', 'conv_w', 'conv_b', 'w_oa', 'w_ob', 'w_out', 'ln_g', 'ln_b']
TWIN_WEIGHTS = ['w_in', 'b_gate', 'ln_v_g', 'ln_v_b', 'w_s', 'b_s', 'conv_w', 'conv_b', 'w_oa', 'w_ob', 'w_out', 'ln_g', 'ln_b']
TWIN_DIFF_INPUT = 'x'
TWIN_INPUTS = ['x', 'w_in', 'b_gate', 'ln_v_g', 'ln_v_b', 'w_s', 'b_s', 'conv_w', 'conv_b', 'w_oa', 'w_ob', 'w_out', 'ln_g', 'ln_b', 'loss_target', 'm_w_in', 'm_b_gate', 'm_ln_v_g', 'm_ln_v_b', 'm_w_s', 'm_b_s', 'm_conv_w', 'm_conv_b', 'm_w_oa', 'm_w_ob', 'm_w_out', 'm_ln_g', 'm_ln_b', 'v_w_in', 'v_b_gate', 'v_ln_v_g', 'v_ln_v_b', 'v_w_s', 'v_b_s', 'v_conv_w', 'v_conv_b', 'v_w_oa', 'v_w_ob', 'v_w_out', 'v_ln_g', 'v_ln_b']
TWIN_OUTPUTS = ['loss', 'grad_x', 'grad_w_in', 'grad_b_gate', 'grad_ln_v_g', 'grad_ln_v_b', 'grad_w_s', 'grad_b_s', 'grad_conv_w', 'grad_conv_b', 'grad_w_oa', 'grad_w_ob', 'grad_w_out', 'grad_ln_g', 'grad_ln_b', 'delta_w_in', 'delta_b_gate', 'delta_ln_v_g', 'delta_ln_v_b', 'delta_w_s', 'delta_b_s', 'delta_conv_w', 'delta_conv_b', 'delta_w_oa', 'delta_w_ob', 'delta_w_out', 'delta_ln_g', 'delta_ln_b', 'new_m_w_in', 'new_m_b_gate', 'new_m_ln_v_g', 'new_m_ln_v_b', 'new_m_w_s', 'new_m_b_s', 'new_m_conv_w', 'new_m_conv_b', 'new_m_w_oa', 'new_m_w_ob', 'new_m_w_out', 'new_m_ln_g', 'new_m_ln_b', 'new_v_w_in', 'new_v_b_gate', 'new_v_ln_v_g', 'new_v_ln_v_b', 'new_v_w_s', 'new_v_b_s', 'new_v_conv_w', 'new_v_conv_b', 'new_v_w_oa', 'new_v_w_ob', 'new_v_w_out', 'new_v_ln_g', 'new_v_ln_b']
TWIN_LEAF_KINDS = {'loss': 'loss', 'grad_x': 'grad_x', 'grad_w_in': 'grad_w', 'grad_b_gate': 'grad_w', 'grad_ln_v_g': 'grad_w', 'grad_ln_v_b': 'grad_w', 'grad_w_s': 'grad_w', 'grad_b_s': 'grad_w', 'grad_conv_w': 'grad_w', 'grad_conv_b': 'grad_w', 'grad_w_oa': 'grad_w', 'grad_w_ob': 'grad_w', 'grad_w_out': 'grad_w', 'grad_ln_g': 'grad_w', 'grad_ln_b': 'grad_w', 'delta_w_in': 'delta_w', 'delta_b_gate': 'delta_w', 'delta_ln_v_g': 'delta_w', 'delta_ln_v_b': 'delta_w', 'delta_w_s': 'delta_w', 'delta_b_s': 'delta_w', 'delta_conv_w': 'delta_w', 'delta_conv_b': 'delta_w', 'delta_w_oa': 'delta_w', 'delta_w_ob': 'delta_w', 'delta_w_out': 'delta_w', 'delta_ln_g': 'delta_w', 'delta_ln_b': 'delta_w', 'new_m_w_in': 'new_m', 'new_m_b_gate': 'new_m', 'new_m_ln_v_g': 'new_m', 'new_m_ln_v_b': 'new_m', 'new_m_w_s': 'new_m', 'new_m_b_s': 'new_m', 'new_m_conv_w': 'new_m', 'new_m_conv_b': 'new_m', 'new_m_w_oa': 'new_m', 'new_m_w_ob': 'new_m', 'new_m_w_out': 'new_m', 'new_m_ln_g': 'new_m', 'new_m_ln_b': 'new_m', 'new_v_w_in': 'new_v', 'new_v_b_gate': 'new_v', 'new_v_ln_v_g': 'new_v', 'new_v_ln_v_b': 'new_v', 'new_v_w_s': 'new_v', 'new_v_b_s': 'new_v', 'new_v_conv_w': 'new_v', 'new_v_conv_b': 'new_v', 'new_v_w_oa': 'new_v', 'new_v_w_ob': 'new_v', 'new_v_w_out': 'new_v', 'new_v_ln_g': 'new_v', 'new_v_ln_b': 'new_v'}


def _forward(args):
    return _fwd_reference(*[args[k] for k in FWD_PARAMS])


def _output_shape():
    def fwd():
        inp = _fwd_setup_inputs(0)
        return _fwd_reference(*[inp[k] for k in FWD_PARAMS])
    out = _jax.eval_shape(fwd)
    return out.shape, out.dtype

N_MICROBATCH = 1
ADAM_LR = 0.001
ADAM_B1 = 0.9
ADAM_B2 = 0.999
ADAM_EPS = 1e-08
ADAM_WD = 0.01
ADAM_STEP = 10
PER_EXAMPLE_BATCH_AXIS = {'x': 0, 'loss_target': 0}
SHARED_INPUTS = []
_WEIGHT_DTYPES = {'w_in': _jnp.float32, 'b_gate': _jnp.float32, 'ln_v_g': _jnp.float32, 'ln_v_b': _jnp.float32, 'w_s': _jnp.float32, 'b_s': _jnp.float32, 'conv_w': _jnp.float32, 'conv_b': _jnp.float32, 'w_oa': _jnp.float32, 'w_ob': _jnp.float32, 'w_out': _jnp.float32, 'ln_g': _jnp.float32, 'ln_b': _jnp.float32}
MOMENT_SCALE = {'w_in': 9.191146e-03, 'b_gate': 4.056515e-03, 'ln_v_g': 2.831670e-03, 'ln_v_b': 2.651243e-03, 'w_s': 7.857487e-03, 'b_s': 1.113424e-02, 'conv_w': 1.193340e-02, 'conv_b': 1.206042e-02, 'w_oa': 1.480482e-02, 'w_ob': 1.997707e-02, 'w_out': 2.484942e-02, 'ln_g': 3.198203e+01, 'ln_b': 5.046319e-01}


def _to_microbatches(a, axis):
    t = _jnp.moveaxis(a, axis, 0)
    t = t.reshape((N_MICROBATCH, t.shape[0] // N_MICROBATCH) + t.shape[1:])
    return _jnp.moveaxis(t, 1, axis + 1)


def setup_inputs(seed: int = 0) -> dict:
    inp = _fwd_setup_inputs(seed)
    key = _jax.random.fold_in(_jax.random.key(seed), 7919)
    shape, _ = _output_shape()
    out = dict(inp)
    out["loss_target"] = _jax.random.normal(_jax.random.fold_in(key, 0), shape, _jnp.float32)
    for i, name in enumerate(TWIN_WEIGHTS):
        w = inp[name].astype(_jnp.float32)
        if MOMENT_SCALE is None:
            s = _jnp.sqrt(_jnp.mean(_jnp.square(w)) + 1e-30)
        else:
            s = MOMENT_SCALE[name]
        km, kv = _jax.random.split(_jax.random.fold_in(key, i + 1))
        out[name] = w
        out["m_" + name] = s * _jax.random.normal(km, w.shape, _jnp.float32)
        out["v_" + name] = (s * s) * _jax.random.uniform(kv, w.shape, _jnp.float32, 0.5, 1.5)
    if N_MICROBATCH > 1:
        for name, axis in PER_EXAMPLE_BATCH_AXIS.items():
            out[name] = _to_microbatches(out[name], axis)
    return {'x': out['x'], 'w_in': out['w_in'], 'b_gate': out['b_gate'], 'ln_v_g': out['ln_v_g'], 'ln_v_b': out['ln_v_b'], 'w_s': out['w_s'], 'b_s': out['b_s'], 'conv_w': out['conv_w'], 'conv_b': out['conv_b'], 'w_oa': out['w_oa'], 'w_ob': out['w_ob'], 'w_out': out['w_out'], 'ln_g': out['ln_g'], 'ln_b': out['ln_b'], 'loss_target': out['loss_target'], 'm_w_in': out['m_w_in'], 'm_b_gate': out['m_b_gate'], 'm_ln_v_g': out['m_ln_v_g'], 'm_ln_v_b': out['m_ln_v_b'], 'm_w_s': out['m_w_s'], 'm_b_s': out['m_b_s'], 'm_conv_w': out['m_conv_w'], 'm_conv_b': out['m_conv_b'], 'm_w_oa': out['m_w_oa'], 'm_w_ob': out['m_w_ob'], 'm_w_out': out['m_w_out'], 'm_ln_g': out['m_ln_g'], 'm_ln_b': out['m_ln_b'], 'v_w_in': out['v_w_in'], 'v_b_gate': out['v_b_gate'], 'v_ln_v_g': out['v_ln_v_g'], 'v_ln_v_b': out['v_ln_v_b'], 'v_w_s': out['v_w_s'], 'v_b_s': out['v_b_s'], 'v_conv_w': out['v_conv_w'], 'v_conv_b': out['v_conv_b'], 'v_w_oa': out['v_w_oa'], 'v_w_ob': out['v_w_ob'], 'v_w_out': out['v_w_out'], 'v_ln_g': out['v_ln_g'], 'v_ln_b': out['v_ln_b']}


def _loss(weights, diff, rest, loss_target):
    with _jax.named_scope("forward"):
        args = {**rest, TWIN_DIFF_INPUT: diff, **{k: w.astype(_WEIGHT_DTYPES[k]) for k, w in weights.items()}}
        y = _forward(args)
    with _jax.named_scope("loss_head"):
        err = _jnp.square(y.astype(_jnp.float32) - loss_target)
        return 0.5 * _jnp.sum(_jnp.mean(err, axis=-1)) if err.ndim else 0.5 * err


def _adamw(w, g, m, v):
    m = ADAM_B1 * m + (1.0 - ADAM_B1) * g
    v = ADAM_B2 * v + (1.0 - ADAM_B2) * _jnp.square(g)
    m_hat = m / (1.0 - ADAM_B1 ** ADAM_STEP)
    v_hat = v / (1.0 - ADAM_B2 ** ADAM_STEP)
    delta = -ADAM_LR * (m_hat / (_jnp.sqrt(v_hat) + ADAM_EPS) + ADAM_WD * w)
    return delta, m, v


def reference(x, w_in, b_gate, ln_v_g, ln_v_b, w_s, b_s, conv_w, conv_b, w_oa, w_ob, w_out, ln_g, ln_b, loss_target, m_w_in, m_b_gate, m_ln_v_g, m_ln_v_b, m_w_s, m_b_s, m_conv_w, m_conv_b, m_w_oa, m_w_ob, m_w_out, m_ln_g, m_ln_b, v_w_in, v_b_gate, v_ln_v_g, v_ln_v_b, v_w_s, v_b_s, v_conv_w, v_conv_b, v_w_oa, v_w_ob, v_w_out, v_ln_g, v_ln_b):
    given = dict(x=x, w_in=w_in, b_gate=b_gate, ln_v_g=ln_v_g, ln_v_b=ln_v_b, w_s=w_s, b_s=b_s, conv_w=conv_w, conv_b=conv_b, w_oa=w_oa, w_ob=w_ob, w_out=w_out, ln_g=ln_g, ln_b=ln_b, loss_target=loss_target, m_w_in=m_w_in, m_b_gate=m_b_gate, m_ln_v_g=m_ln_v_g, m_ln_v_b=m_ln_v_b, m_w_s=m_w_s, m_b_s=m_b_s, m_conv_w=m_conv_w, m_conv_b=m_conv_b, m_w_oa=m_w_oa, m_w_ob=m_w_ob, m_w_out=m_w_out, m_ln_g=m_ln_g, m_ln_b=m_ln_b, v_w_in=v_w_in, v_b_gate=v_b_gate, v_ln_v_g=v_ln_v_g, v_ln_v_b=v_ln_v_b, v_w_s=v_w_s, v_b_s=v_b_s, v_conv_w=v_conv_w, v_conv_b=v_conv_b, v_w_oa=v_w_oa, v_w_ob=v_w_ob, v_w_out=v_w_out, v_ln_g=v_ln_g, v_ln_b=v_ln_b)
    weights = {n: given[n] for n in TWIN_WEIGHTS}
    shared = {n: given[n] for n in SHARED_INPUTS}
    per_example = {n: given[n] for n in ['x']}
    grad_fn = _jax.value_and_grad(_loss, argnums=(0, 1))

    def one_microbatch(ex, loss_target):
        ex = dict(ex)
        diff = ex.pop(TWIN_DIFF_INPUT)
        return grad_fn(weights, diff, {**shared, **ex}, loss_target)

    if N_MICROBATCH == 1:
        loss, (grad_w, grad_x) = one_microbatch(per_example, given["loss_target"])
    else:
        def body(carry, xs):
            loss_sum, grad_sum = carry
            l_k, (gw_k, gx_k) = one_microbatch(xs[0], xs[1])
            with _jax.named_scope("update"):
                return (loss_sum + l_k, _jax.tree.map(_jnp.add, grad_sum, gw_k)), gx_k

        init = (_jnp.zeros((), _jnp.float32), _jax.tree.map(_jnp.zeros_like, weights))
        (loss, grad_w), grad_x = _jax.lax.scan(body, init, (per_example, given["loss_target"]))
    with _jax.named_scope("update"):
        delta_w, new_m, new_v = {}, {}, {}
        for n in TWIN_WEIGHTS:
            delta_w[n], new_m[n], new_v[n] = _adamw(weights[n], grad_w[n], given["m_" + n], given["v_" + n])
    return (loss, grad_x, *[grad_w[n] for n in TWIN_WEIGHTS], *[delta_w[n] for n in TWIN_WEIGHTS],
            *[new_m[n] for n in TWIN_WEIGHTS], *[new_v[n] for n in TWIN_WEIGHTS])
```

```python
import functools
import math

import jax
import jax.numpy as jnp
from jax import lax
from jax.experimental import pallas as pl
from jax.experimental.pallas import tpu as pltpu

F32 = jnp.float32
BF16 = jnp.bfloat16

D_MODEL = 2048
N_HEADS = 8
HEAD_DIM = D_MODEL // N_HEADS
CHUNK = 128
N_SEG = 9
N_CHIPS = 4
SHARD_COLS = N_SEG * D_MODEL // N_CHIPS
COL_BLOCK = 512
BLOCKS_PER_SHARD = SHARD_COLS // COL_BLOCK
BLOCKS_PER_SEG = D_MODEL // COL_BLOCK
SHARD_ROWS = D_MODEL // N_CHIPS
DN_ALPHA = 2.0 ** 0.25
LN_EPS = 1e-5
GELU_K = math.sqrt(2.0 / math.pi)
GELU_C = 0.044715

ADAM_LR = 0.001
ADAM_B1 = 0.9
ADAM_B2 = 0.999
ADAM_EPS = 1e-08
ADAM_WD = 0.01
ADAM_STEP = 10
ADAM_C1 = 1.0 / (1.0 - ADAM_B1 ** ADAM_STEP)
ADAM_C2 = 1.0 / (1.0 - ADAM_B2 ** ADAM_STEP)

VMEM_LIMIT = 56 * 1024 * 1024
MESH = pl.DeviceIdType.MESH
ANY = pl.BlockSpec(memory_space=pl.ANY)

SMALL_ROWS = (("ln_g", 16), ("ln_b", 16), ("b_gate", 32), ("ln_v_g", 16), ("ln_v_b", 16),
              ("w_s", 1024), ("b_s", 8), ("conv_w", 48), ("conv_b", 16), ("loss", 8))
SMALL_TOTAL = sum(r for _, r in SMALL_ROWS)


def _params(*sem):
    return pltpu.CompilerParams(dimension_semantics=sem, vmem_limit_bytes=VMEM_LIMIT)


def _sigmoid(x):
    return 1.0 / (1.0 + jnp.exp(-x))


def _gelu_and_grad(x):
    x2 = x * x
    th = jnp.tanh(GELU_K * (x + GELU_C * x * x2))
    g = 0.5 * x * (1.0 + th)
    dg = 0.5 * (1.0 + th) + 0.5 * x * (1.0 - th * th) * (GELU_K * (1.0 + 3.0 * GELU_C * x2))
    return g, dg


def _gelu(x):
    return 0.5 * x * (1.0 + jnp.tanh(GELU_K * (x + GELU_C * x * x * x)))


def _dot(a, b, ca=1, cb=0):
    return lax.dot_general(a, b, (((ca,), (cb,)), ((), ())), preferred_element_type=F32)


def _cast_rows(x, tm, name):
    r, c = x.shape

    def body(x_ref, o_ref):
        o_ref[...] = x_ref[...].astype(BF16)

    return pl.pallas_call(
        body, name=name, grid=(r // tm,),
        in_specs=[pl.BlockSpec((tm, c), lambda i: (i, 0))],
        out_specs=pl.BlockSpec((tm, c), lambda i: (i, 0)),
        out_shape=jax.ShapeDtypeStruct((r, c), BF16),
        compiler_params=_params("parallel"),
    )(x)


def _prep_small_weights(w_oa, w_ob, w_out, w_s):
    rows = w_oa.shape[0]

    def body(a_ref, b_ref, c_ref, ws_ref, wo_ref, wm_ref):
        wo_ref[0] = a_ref[...].astype(BF16)
        wo_ref[1] = b_ref[...].astype(BF16)
        wo_ref[2] = c_ref[...].astype(BF16)
        t = lax.broadcasted_iota(jnp.int32, (CHUNK, CHUNK), 0)
        s = lax.broadcasted_iota(jnp.int32, (CHUNK, CHUNK), 1)
        for h in range(N_HEADS):
            wm_ref[h] = jnp.where(s <= t, ws_ref[h], 0.0).astype(BF16)

    return pl.pallas_call(
        body, name="prep_small_weights",
        out_shape=(jax.ShapeDtypeStruct((3, rows, D_MODEL), BF16),
                   jax.ShapeDtypeStruct((N_HEADS, CHUNK, CHUNK), BF16)),
        compiler_params=_params(),
    )(w_oa, w_ob, w_out, w_s)


def _mesh_pos():
    x, y, c = lax.axis_index("x"), lax.axis_index("y"), lax.axis_index("c")
    chips = [(1 - x, y), (x, 1 - y), (1 - x, 1 - y)]
    return x, y, c, chips


def _all_gather_weights(win_b, wo_b, conv_w8):
    d, sc = win_b.shape
    rows = wo_b.shape[1]
    hd, hr = d // 2, rows // 2

    def body(win_ref, wo_ref, cw_ref, wf_ref, wof_ref, cwf_ref, send_sems, recv_sems, local_sems):
        x, y, c, chips = _mesh_pos()
        me = 2 * x + y
        sibling = (x, y, 1 - c)

        def half(ref_kind, k, cc):
            if ref_kind == 0:
                return wf_ref.at[k, pl.ds(cc * hd, hd), :]
            return wof_ref.at[:, k, pl.ds(cc * hr, hr), :]

        def src_half(ref_kind, cc):
            if ref_kind == 0:
                return win_ref.at[pl.ds(cc * hd, hd), :]
            return wo_ref.at[:, pl.ds(cc * hr, hr), :]

        def rcopy(sem, src, dst, to):
            return pltpu.make_async_remote_copy(src_ref=src, dst_ref=dst, send_sem=send_sems.at[sem],
                                                recv_sem=recv_sems.at[sem], device_id=to, device_id_type=MESH)

        locals_ = [pltpu.make_async_copy(win_ref, wf_ref.at[me], local_sems.at[0]),
                   pltpu.make_async_copy(wo_ref, wof_ref.at[:, me], local_sems.at[1]),
                   pltpu.make_async_copy(cw_ref, cwf_ref.at[me], local_sems.at[2])]
        for cp in locals_:
            cp.start()
        first = []
        for kind in (0, 1):
            for j, chip in enumerate(chips):
                first.append(rcopy(kind * 6 + j, src_half(kind, c), half(kind, me, c), (*chip, c)))
        for j, chip in enumerate(chips):
            first.append(rcopy(12 + j, cw_ref, cwf_ref.at[me], (*chip, c)))
        for cp in first:
            cp.start()
        passed = []
        for kind in (0, 1):
            for j, (cx, cy) in enumerate(chips):
                k = 2 * cx + cy
                rcopy(kind * 6 + j, src_half(kind, c), half(kind, k, c), sibling).wait_recv()
                fwd = rcopy(kind * 6 + 3 + j, half(kind, k, c), half(kind, k, c), sibling)
                fwd.start()
                passed.append(fwd)
        for j, (cx, cy) in enumerate(chips):
            rcopy(12 + j, cw_ref, cwf_ref.at[2 * cx + cy], sibling).wait_recv()
        for kind in (0, 1):
            for j, (cx, cy) in enumerate(chips):
                k = 2 * cx + cy
                rcopy(kind * 6 + 3 + j, half(kind, k, 1 - c), half(kind, k, 1 - c), sibling).wait_recv()
        for cp in first + passed:
            cp.wait_send()
        for cp in locals_:
            cp.wait()

    return pl.pallas_call(
        body, name="all_gather_weights",
        in_specs=[ANY, ANY, ANY], out_specs=(ANY, ANY, ANY),
        out_shape=(jax.ShapeDtypeStruct((N_CHIPS, d, sc), BF16),
                   jax.ShapeDtypeStruct((3, N_CHIPS, rows, D_MODEL), BF16),
                   jax.ShapeDtypeStruct((N_CHIPS,) + conv_w8.shape, F32)),
        scratch_shapes=[pltpu.SemaphoreType.DMA((15,)), pltpu.SemaphoreType.DMA((15,)),
                        pltpu.SemaphoreType.DMA((3,))],
        compiler_params=pltpu.CompilerParams(has_side_effects=True),
    )(win_b, wo_b, conv_w8)


def _gather_small(small):
    def body(s_ref, o_ref, send_sems, recv_sems, local_sem):
        x, y, c, _ = _mesh_pos()
        me = 4 * x + 2 * y + c
        mine = pltpu.make_async_copy(s_ref, o_ref.at[me], local_sem)
        mine.start()
        sends = []
        for k in range(1, 8):
            dx, dy, dc = (k >> 2) & 1, (k >> 1) & 1, k & 1
            to = (x ^ dx, y ^ dy, c ^ dc)
            cp = pltpu.make_async_remote_copy(src_ref=s_ref, dst_ref=o_ref.at[me], send_sem=send_sems.at[k - 1],
                                              recv_sem=recv_sems.at[k - 1], device_id=to, device_id_type=MESH)
            cp.start()
            sends.append(cp)
        for k in range(1, 8):
            dx, dy, dc = (k >> 2) & 1, (k >> 1) & 1, k & 1
            frm = 4 * (x ^ dx) + 2 * (y ^ dy) + (c ^ dc)
            pltpu.make_async_remote_copy(src_ref=s_ref, dst_ref=o_ref.at[frm], send_sem=send_sems.at[k - 1],
                                         recv_sem=recv_sems.at[k - 1], device_id=(x, y, c),
                                         device_id_type=MESH).wait_recv()
        for cp in sends:
            cp.wait_send()
        mine.wait()

    return pl.pallas_call(
        body, name="gather_small",
        in_specs=[ANY], out_specs=ANY,
        out_shape=jax.ShapeDtypeStruct((8,) + small.shape, small.dtype),
        scratch_shapes=[pltpu.SemaphoreType.DMA((7,)), pltpu.SemaphoreType.DMA((7,)), pltpu.SemaphoreType.DMA],
        compiler_params=pltpu.CompilerParams(has_side_effects=True),
    )(small)


def _exchange_halves(g_in, g_o):
    d, c9 = g_in.shape
    hd = d // 2
    hr = SHARD_ROWS // 2
    g_o4 = [g.reshape(N_CHIPS, 2, hr, D_MODEL) for g in g_o]

    def body(gi_ref, ga_ref, gb_ref, gc_ref, ri_ref, ra_ref, rb_ref, rc_ref, send_sems, recv_sems):
        x, y, c, _ = _mesh_pos()
        sibling = (x, y, 1 - c)
        cps = [pltpu.make_async_remote_copy(src_ref=gi_ref.at[pl.ds((1 - c) * hd, hd), :], dst_ref=ri_ref,
                                            send_sem=send_sems.at[0], recv_sem=recv_sems.at[0],
                                            device_id=sibling, device_id_type=MESH)]
        for n, (g_ref, r_ref) in enumerate(((ga_ref, ra_ref), (gb_ref, rb_ref), (gc_ref, rc_ref))):
            cps.append(pltpu.make_async_remote_copy(src_ref=g_ref.at[:, 1 - c], dst_ref=r_ref,
                                                    send_sem=send_sems.at[1 + n], recv_sem=recv_sems.at[1 + n],
                                                    device_id=sibling, device_id_type=MESH))
        for cp in cps:
            cp.start()
        for cp in cps:
            cp.wait()

    o_shape = jax.ShapeDtypeStruct((N_CHIPS, hr, D_MODEL), BF16)
    return pl.pallas_call(
        body, name="rs_exchange_halves",
        in_specs=[ANY] * 4, out_specs=(ANY,) * 4,
        out_shape=(jax.ShapeDtypeStruct((hd, c9), BF16), o_shape, o_shape, o_shape),
        scratch_shapes=[pltpu.SemaphoreType.DMA((4,)), pltpu.SemaphoreType.DMA((4,))],
        compiler_params=pltpu.CompilerParams(has_side_effects=True),
    )(g_in, *g_o4)


def _add_halves(g_in, g_o, r_in, r_o):
    d, c9 = g_in.shape
    hd = d // 2
    hr = SHARD_ROWS // 2
    core = lax.axis_index("c").astype(jnp.int32).reshape(1)
    tm = min(512, hd)
    nb = hd // tm

    def body_in(c_ref, g_ref, r_ref, o_ref):
        o_ref[...] = (g_ref[...].astype(F32) + r_ref[...].astype(F32)).astype(BF16)

    s_in = pl.pallas_call(
        body_in, name="rs_add_halves_in",
        grid_spec=pltpu.PrefetchScalarGridSpec(
            num_scalar_prefetch=1, grid=(N_CHIPS, nb),
            in_specs=[pl.BlockSpec((tm, SHARD_COLS), lambda k, i, c_ref: (c_ref[0] * nb + i, k)),
                      pl.BlockSpec((tm, SHARD_COLS), lambda k, i, c_ref: (i, k))],
            out_specs=pl.BlockSpec((None, tm, SHARD_COLS), lambda k, i, c_ref: (k, i, 0))),
        out_shape=jax.ShapeDtypeStruct((N_CHIPS, hd, SHARD_COLS), BF16),
        compiler_params=_params("parallel", "parallel"),
    )(core, g_in, r_in)

    g_o4 = [g.reshape(N_CHIPS, 2, hr, D_MODEL) for g in g_o]

    def body_o(c_ref, ga_ref, gb_ref, gc_ref, ra_ref, rb_ref, rc_ref, o_ref):
        for n, (g_ref, r_ref) in enumerate(((ga_ref, ra_ref), (gb_ref, rb_ref), (gc_ref, rc_ref))):
            o_ref[n] = (g_ref[...].astype(F32) + r_ref[...].astype(F32)).astype(BF16)

    gspec = pl.BlockSpec((None, None, hr, D_MODEL), lambda k, c_ref: (k, c_ref[0], 0, 0))
    rspec = pl.BlockSpec((None, hr, D_MODEL), lambda k, c_ref: (k, 0, 0))
    s_o = pl.pallas_call(
        body_o, name="rs_add_halves_o",
        grid_spec=pltpu.PrefetchScalarGridSpec(
            num_scalar_prefetch=1, grid=(N_CHIPS,),
            in_specs=[gspec] * 3 + [rspec] * 3,
            out_specs=pl.BlockSpec((3, None, hr, D_MODEL), lambda k, c_ref: (0, k, 0, 0))),
        out_shape=jax.ShapeDtypeStruct((3, N_CHIPS, hr, D_MODEL), BF16),
        compiler_params=_params("parallel"),
    )(core, *g_o4, *r_o)
    return s_in, s_o


def _scatter_to_chips(s_in, s_o):
    _, hd, sc = s_in.shape
    hr = s_o.shape[2]

    def body(si_ref, so_ref, ri_ref, ro_ref, send_sems, recv_sems, local_sems):
        x, y, c, chips = _mesh_pos()
        me = 2 * x + y
        locals_ = [pltpu.make_async_copy(si_ref.at[me], ri_ref.at[me], local_sems.at[0]),
                   pltpu.make_async_copy(so_ref.at[:, me], ro_ref.at[me], local_sems.at[1])]
        for cp in locals_:
            cp.start()
        sends = []
        for j, (cx, cy) in enumerate(chips):
            k = 2 * cx + cy
            sends.append(pltpu.make_async_remote_copy(src_ref=si_ref.at[k], dst_ref=ri_ref.at[me],
                                                      send_sem=send_sems.at[j], recv_sem=recv_sems.at[j],
                                                      device_id=(cx, cy, c), device_id_type=MESH))
            sends.append(pltpu.make_async_remote_copy(src_ref=so_ref.at[:, k], dst_ref=ro_ref.at[me],
                                                      send_sem=send_sems.at[3 + j], recv_sem=recv_sems.at[3 + j],
                                                      device_id=(cx, cy, c), device_id_type=MESH))
        for cp in sends:
            cp.start()
        for j, (cx, cy) in enumerate(chips):
            k = 2 * cx + cy
            pltpu.make_async_remote_copy(src_ref=si_ref.at[k], dst_ref=ri_ref.at[k], send_sem=send_sems.at[j],
                                         recv_sem=recv_sems.at[j], device_id=(x, y, c),
                                         device_id_type=MESH).wait_recv()
            pltpu.make_async_remote_copy(src_ref=so_ref.at[:, k], dst_ref=ro_ref.at[k], send_sem=send_sems.at[3 + j],
                                         recv_sem=recv_sems.at[3 + j], device_id=(x, y, c),
                                         device_id_type=MESH).wait_recv()
        for cp in sends:
            cp.wait_send()
        for cp in locals_:
            cp.wait()

    return pl.pallas_call(
        body, name="rs_scatter_to_chips",
        in_specs=[ANY, ANY], out_specs=(ANY, ANY),
        out_shape=(jax.ShapeDtypeStruct((N_CHIPS, hd, sc), BF16),
                   jax.ShapeDtypeStruct((N_CHIPS, 3, hr, D_MODEL), BF16)),
        scratch_shapes=[pltpu.SemaphoreType.DMA((6,)), pltpu.SemaphoreType.DMA((6,)), pltpu.SemaphoreType.DMA((2,))],
        compiler_params=pltpu.CompilerParams(has_side_effects=True),
    )(s_in, s_o)


def _sum_chips(r_in, r_o):
    _, hd, sc = r_in.shape
    hr = r_o.shape[2]
    tm = min(256, hd)

    def body_in(r_ref, o_ref):
        o_ref[...] = ((r_ref[0].astype(F32) + r_ref[1].astype(F32)) + r_ref[2].astype(F32)) + r_ref[3].astype(F32)

    f_in = pl.pallas_call(
        body_in, name="rs_sum_chips_in", grid=(hd // tm,),
        in_specs=[pl.BlockSpec((N_CHIPS, tm, sc), lambda i: (0, i, 0))],
        out_specs=pl.BlockSpec((tm, sc), lambda i: (i, 0)),
        out_shape=jax.ShapeDtypeStruct((hd, sc), F32),
        compiler_params=_params("parallel"),
    )(r_in)

    def body_o(r_ref, o_ref):
        o_ref[...] = ((r_ref[0].astype(F32) + r_ref[1].astype(F32)) + r_ref[2].astype(F32)) + r_ref[3].astype(F32)

    f_o = pl.pallas_call(
        body_o, name="rs_sum_chips_o", grid=(3,),
        in_specs=[pl.BlockSpec((N_CHIPS, None, hr, D_MODEL), lambda n: (0, n, 0, 0))],
        out_specs=pl.BlockSpec((None, hr, D_MODEL), lambda n: (n, 0, 0)),
        out_shape=jax.ShapeDtypeStruct((3, hr, D_MODEL), F32),
        compiler_params=_params("parallel"),
    )(r_o)
    return f_in, f_o


def _share_halves(f_in, f_o):
    hd, sc = f_in.shape
    hr = f_o.shape[1]

    def body(fi_ref, fo_ref, gi_ref, go_ref, send_sems, recv_sems, local_sems):
        x, y, c, _ = _mesh_pos()
        sibling = (x, y, 1 - c)

        def dst(cc):
            return gi_ref.at[pl.ds(cc * hd, hd), :], go_ref.at[:, pl.ds(cc * hr, hr), :]

        mine_i, mine_o = dst(c)
        locals_ = [pltpu.make_async_copy(fi_ref, mine_i, local_sems.at[0]),
                   pltpu.make_async_copy(fo_ref, mine_o, local_sems.at[1])]
        for cp in locals_:
            cp.start()
        sends = [pltpu.make_async_remote_copy(src_ref=fi_ref, dst_ref=mine_i, send_sem=send_sems.at[0],
                                              recv_sem=recv_sems.at[0], device_id=sibling, device_id_type=MESH),
                 pltpu.make_async_remote_copy(src_ref=fo_ref, dst_ref=mine_o, send_sem=send_sems.at[1],
                                              recv_sem=recv_sems.at[1], device_id=sibling, device_id_type=MESH)]
        for cp in sends:
            cp.start()
        other_i, other_o = dst(1 - c)
        pltpu.make_async_remote_copy(src_ref=fi_ref, dst_ref=other_i, send_sem=send_sems.at[0],
                                     recv_sem=recv_sems.at[0], device_id=sibling, device_id_type=MESH).wait_recv()
        pltpu.make_async_remote_copy(src_ref=fo_ref, dst_ref=other_o, send_sem=send_sems.at[1],
                                     recv_sem=recv_sems.at[1], device_id=sibling, device_id_type=MESH).wait_recv()
        for cp in sends:
            cp.wait_send()
        for cp in locals_:
            cp.wait()

    return pl.pallas_call(
        body, name="rs_share_halves",
        in_specs=[ANY, ANY], out_specs=(ANY, ANY),
        out_shape=(jax.ShapeDtypeStruct((2 * hd, sc), F32), jax.ShapeDtypeStruct((3, 2 * hr, D_MODEL), F32)),
        scratch_shapes=[pltpu.SemaphoreType.DMA((2,)), pltpu.SemaphoreType.DMA((2,)), pltpu.SemaphoreType.DMA((2,))],
        compiler_params=pltpu.CompilerParams(has_side_effects=True),
    )(f_in, f_o)


def _project_in(xb, w_full, first_block, n_seg, tm, name):
    t = xb.shape[0]
    nj = n_seg * BLOCKS_PER_SEG

    def body(x_ref, w_ref, o_ref):
        o_ref[...] = _dot(x_ref[...], w_ref[...]).astype(BF16)

    def w_map(i, j):
        g = first_block + j
        return (g // BLOCKS_PER_SHARD, 0, g % BLOCKS_PER_SHARD)

    return pl.pallas_call(
        body, name=name, grid=(t // tm, nj),
        in_specs=[pl.BlockSpec((tm, D_MODEL), lambda i, j: (i, 0)),
                  pl.BlockSpec((None, D_MODEL, COL_BLOCK), w_map)],
        out_specs=pl.BlockSpec((None, tm, COL_BLOCK), lambda i, j: (j // BLOCKS_PER_SEG, i, j % BLOCKS_PER_SEG)),
        out_shape=jax.ShapeDtypeStruct((n_seg, t, D_MODEL), BF16),
        compiler_params=_params("parallel", "parallel"),
    )(xb, w_full)


def _merge_forward(ya, yb, wo3, p_g, bg, tm, tn):
    t = ya.shape[0]

    def body(ya_ref, yb_ref, wa_ref, wb_ref, g_ref, bg_ref, m_ref, oab_ref):
        oa = _dot(ya_ref[...], wa_ref[...])
        ob = _dot(yb_ref[...], wb_ref[...])
        ga = _sigmoid(g_ref[0].astype(F32) + bg_ref[0])
        gb = _sigmoid(g_ref[1].astype(F32) + bg_ref[1])
        m_ref[...] = (ga * oa + gb * ob).astype(BF16)
        oab_ref[0] = oa.astype(BF16)
        oab_ref[1] = ob.astype(BF16)

    return pl.pallas_call(
        body, name="merge_forward", grid=(t // tm, D_MODEL // tn),
        in_specs=[pl.BlockSpec((tm, D_MODEL), lambda i, j: (i, 0)),
                  pl.BlockSpec((tm, D_MODEL), lambda i, j: (i, 0)),
                  pl.BlockSpec((None, D_MODEL, tn), lambda i, j: (0, 0, j)),
                  pl.BlockSpec((None, D_MODEL, tn), lambda i, j: (1, 0, j)),
                  pl.BlockSpec((2, tm, tn), lambda i, j: (0, i, j)),
                  pl.BlockSpec((2, 1, tn), lambda i, j: (0, 0, j))],
        out_specs=(pl.BlockSpec((tm, tn), lambda i, j: (i, j)),
                   pl.BlockSpec((2, tm, tn), lambda i, j: (0, i, j))),
        out_shape=(jax.ShapeDtypeStruct((t, D_MODEL), BF16), jax.ShapeDtypeStruct((2, t, D_MODEL), BF16)),
        compiler_params=_params("parallel", "parallel"),
    )(ya, yb, wo3, wo3, p_g, bg)


def _head(merged, wo3, x, target, ln_g, ln_b, tm):
    t = x.shape[0]
    inv_d = 1.0 / D_MODEL

    def body(m_ref, w_ref, x_ref, t_ref, g_ref, b_ref, dr_ref, gx_ref, dm_ref, dg_ref, db_ref, loss_ref):
        i = pl.program_id(0)

        @pl.when(i == 0)
        def _():
            dg_ref[...] = jnp.zeros_like(dg_ref)
            db_ref[...] = jnp.zeros_like(db_ref)
            loss_ref[...] = jnp.zeros_like(loss_ref)

        w = w_ref[...]
        out = _dot(m_ref[...], w)
        r = DN_ALPHA * x_ref[...] + out
        mu = jnp.mean(r, axis=-1, keepdims=True)
        xc = r - mu
        var = jnp.mean(xc * xc, axis=-1, keepdims=True)
        rstd = lax.rsqrt(var + LN_EPS)
        xhat = xc * rstd
        g = g_ref[...]
        e = xhat * g + b_ref[...] - t_ref[...]
        se = jnp.sum(jnp.sum(e * e, axis=1, keepdims=True), axis=0, keepdims=True)
        loss_ref[...] += jnp.broadcast_to((0.5 * inv_d) * se, loss_ref.shape)
        dy = e * inv_d
        db_ref[...] += jnp.sum(dy, axis=0, keepdims=True)
        dg_ref[...] += jnp.sum(dy * xhat, axis=0, keepdims=True)
        dxh = dy * g
        m1 = jnp.mean(dxh, axis=-1, keepdims=True)
        m2 = jnp.mean(dxh * xhat, axis=-1, keepdims=True)
        dr = rstd * (dxh - m1 - xhat * m2)
        gx_ref[...] = DN_ALPHA * dr
        drb = dr.astype(BF16)
        dr_ref[...] = drb
        dm_ref[...] = _dot(drb, w, 1, 1).astype(BF16)

    row = pl.BlockSpec((tm, D_MODEL), lambda i: (i, 0))
    vec = pl.BlockSpec((1, D_MODEL), lambda i: (0, 0))
    return pl.pallas_call(
        body, name="head", grid=(t // tm,),
        in_specs=[row, pl.BlockSpec((None, D_MODEL, D_MODEL), lambda i: (2, 0, 0)), row, row, vec, vec],
        out_specs=(row, row, row, vec, vec, pl.BlockSpec((1, 128), lambda i: (0, 0))),
        out_shape=(jax.ShapeDtypeStruct((t, D_MODEL), BF16), jax.ShapeDtypeStruct((t, D_MODEL), F32),
                   jax.ShapeDtypeStruct((t, D_MODEL), BF16), jax.ShapeDtypeStruct((1, D_MODEL), F32),
                   jax.ShapeDtypeStruct((1, D_MODEL), F32), jax.ShapeDtypeStruct((1, 128), F32)),
        compiler_params=_params("arbitrary"),
    )(merged, wo3, x, target, ln_g, ln_b)


def _gate_backward(dmerged, oab, p_g, bg, tm, tn):
    t = dmerged.shape[0]

    def body(dm_ref, oab_ref, g_ref, bg_ref, do_ref, dpg_ref, dbg_ref):
        @pl.when(pl.program_id(1) == 0)
        def _():
            dbg_ref[...] = jnp.zeros_like(dbg_ref)

        dm = dm_ref[...].astype(F32)
        for n in range(2):
            gate = _sigmoid(g_ref[n].astype(F32) + bg_ref[n])
            do_ref[n] = (dm * gate).astype(BF16)
            dgate = dm * oab_ref[n].astype(F32) * gate * (1.0 - gate)
            dpg_ref[n] = dgate.astype(BF16)
            dbg_ref[n] += jnp.sum(dgate, axis=0, keepdims=True)

    pair = pl.BlockSpec((2, tm, tn), lambda j, i: (0, i, j))
    vec = pl.BlockSpec((2, 1, tn), lambda j, i: (0, 0, j))
    return pl.pallas_call(
        body, name="gate_backward", grid=(D_MODEL // tn, t // tm),
        in_specs=[pl.BlockSpec((tm, tn), lambda j, i: (i, j)), pair, pair, vec],
        out_specs=(pair, pair, vec),
        out_shape=(jax.ShapeDtypeStruct((2, t, D_MODEL), BF16), jax.ShapeDtypeStruct((2, t, D_MODEL), BF16),
                   jax.ShapeDtypeStruct((2, 1, D_MODEL), F32)),
        compiler_params=_params("parallel", "arbitrary"),
    )(dmerged, oab, p_g, bg)


def _branch_backward(doab, wo3, tm, tn):
    t = doab.shape[1]

    def body(d_ref, wa_ref, wb_ref, o_ref):
        o_ref[0] = _dot(d_ref[0], wa_ref[...], 1, 1).astype(BF16)
        o_ref[1] = _dot(d_ref[1], wb_ref[...], 1, 1).astype(BF16)

    return pl.pallas_call(
        body, name="branch_backward", grid=(t // tm, D_MODEL // tn),
        in_specs=[pl.BlockSpec((2, tm, D_MODEL), lambda i, j: (0, i, 0)),
                  pl.BlockSpec((None, tn, D_MODEL), lambda i, j: (0, j, 0)),
                  pl.BlockSpec((None, tn, D_MODEL), lambda i, j: (1, j, 0))],
        out_specs=pl.BlockSpec((2, tm, tn), lambda i, j: (0, i, j)),
        out_shape=jax.ShapeDtypeStruct((2, t, D_MODEL), BF16),
        compiler_params=_params("parallel", "parallel"),
    )(doab, wo3, wo3)


def _weight_grad(a, a_sel, b, b_sel, tm, tn, tk, name):
    t = a.shape[-2]
    nk = t // tk

    def body(a_ref, b_ref, o_ref, acc_ref):
        k = pl.program_id(2)
        part = _dot(a_ref[...], b_ref[...], 0, 0)

        @pl.when(k == 0)
        def _():
            acc_ref[...] = part

        @pl.when(k > 0)
        def _():
            acc_ref[...] += part

        @pl.when(k == nk - 1)
        def _():
            o_ref[...] = acc_ref[...].astype(BF16)

    def spec(arr, sel, width, which):
        if arr.ndim == 2:
            return pl.BlockSpec((tk, width), lambda i, j, k: (k, (i, j)[which]))
        return pl.BlockSpec((None, tk, width), lambda i, j, k: (sel, k, (i, j)[which]))

    return pl.pallas_call(
        body, name=name, grid=(D_MODEL // tm, D_MODEL // tn, nk),
        in_specs=[spec(a, a_sel, tm, 0), spec(b, b_sel, tn, 1)],
        out_specs=pl.BlockSpec((tm, tn), lambda i, j, k: (i, j)),
        out_shape=jax.ShapeDtypeStruct((D_MODEL, D_MODEL), BF16),
        scratch_shapes=[pltpu.VMEM((tm, tn), F32)],
        compiler_params=_params("parallel", "parallel", "arbitrary"),
    )(a, b)


def _win_grad(xb, dp, first_seg, g_prev, tn, tk, name):
    n_seg, t, _ = dp.shape
    nk = t // tk
    per_seg = D_MODEL // tn
    nj = n_seg * per_seg

    def body(*refs):
        if g_prev is None:
            x_ref, dp_ref, o_ref, acc_ref = refs
        else:
            x_ref, dp_ref, _, o_ref, acc_ref = refs
        k = pl.program_id(1)
        part = _dot(x_ref[...], dp_ref[...], 0, 0)

        @pl.when(k == 0)
        def _():
            acc_ref[...] = part

        @pl.when(k > 0)
        def _():
            acc_ref[...] += part

        @pl.when(k == nk - 1)
        def _():
            o_ref[...] = acc_ref[...].astype(BF16)

    in_specs = [pl.BlockSpec((tk, D_MODEL), lambda j, k: (k, 0)),
                pl.BlockSpec((None, tk, tn), lambda j, k: (j // per_seg, k, j % per_seg))]
    args = [xb, dp]
    aliases = {}
    if g_prev is not None:
        in_specs.append(ANY)
        args.append(g_prev)
        aliases = {2: 0}
    return pl.pallas_call(
        body, name=name, grid=(nj, nk),
        in_specs=in_specs,
        out_specs=pl.BlockSpec((D_MODEL, tn), lambda j, k: (0, first_seg * per_seg + j)),
        out_shape=jax.ShapeDtypeStruct((D_MODEL, N_SEG * D_MODEL), BF16),
        scratch_shapes=[pltpu.VMEM((D_MODEL, tn), F32)],
        input_output_aliases=aliases,
        compiler_params=_params("parallel", "arbitrary"),
    )(*args)


def _input_grad(dp, w_full, first_block, gx, tm, name):
    n_seg, t, _ = dp.shape
    nk = n_seg * BLOCKS_PER_SEG

    def body(dp_ref, w_ref, gx_ref, o_ref):
        k = pl.program_id(1)
        part = _dot(dp_ref[...], w_ref[...], 1, 1)

        @pl.when(k == 0)
        def _():
            o_ref[...] = gx_ref[...] + part

        @pl.when(k > 0)
        def _():
            o_ref[...] += part

    def w_map(i, k):
        g = first_block + k
        return (g // BLOCKS_PER_SHARD, 0, g % BLOCKS_PER_SHARD)

    return pl.pallas_call(
        body, name=name, grid=(t // tm, nk),
        in_specs=[pl.BlockSpec((None, tm, COL_BLOCK), lambda i, k: (k // BLOCKS_PER_SEG, i, k % BLOCKS_PER_SEG)),
                  pl.BlockSpec((None, D_MODEL, COL_BLOCK), w_map),
                  pl.BlockSpec((tm, D_MODEL), lambda i, k: (i, 0))],
        out_specs=pl.BlockSpec((tm, D_MODEL), lambda i, k: (i, 0)),
        out_shape=jax.ShapeDtypeStruct((t, D_MODEL), F32),
        input_output_aliases={2: 0},
        compiler_params=_params("parallel", "arbitrary"),
    )(dp, w_full, gx)


def _sgu_chunk_forward(u, v, z, wm, bs, lng, lnb):
    ug, dug = _gelu_and_grad(u)
    vg, dvg = _gelu_and_grad(v)
    mu = jnp.mean(vg, axis=-1, keepdims=True)
    xc = vg - mu
    var = jnp.mean(xc * xc, axis=-1, keepdims=True)
    rstd = lax.rsqrt(var + LN_EPS)
    vhat = xc * rstd
    vln = (vhat * lng + lnb).astype(BF16)
    mixed = _dot(wm, vln) + bs
    sig = _sigmoid(z)
    return ug, dug, dvg, rstd, vhat, vln, mixed, sig


def _mixer_a_forward(p_a, wm, bs_col, ln_v_g, ln_v_b, tm):
    t = p_a.shape[1]

    def body(p_ref, wm_ref, bs_ref, g_ref, b_ref, o_ref):
        wm_v, bs_v, lng, lnb = wm_ref[...], bs_ref[...], g_ref[...], b_ref[...]

        def chunk(ci, carry):
            rows = pl.ds(pl.multiple_of(ci * CHUNK, CHUNK), CHUNK)
            u = p_ref[0, rows, :].astype(F32)
            v = p_ref[1, rows, :].astype(F32)
            z = p_ref[2, rows, :].astype(F32)
            ug, _, _, _, _, _, mixed, sig = _sgu_chunk_forward(u, v, z, wm_v, bs_v, lng, lnb)
            o_ref[rows, :] = (ug * mixed * (z * sig)).astype(BF16)
            return carry

        lax.fori_loop(0, tm // CHUNK, chunk, 0)

    return pl.pallas_call(
        body, name="mixer_a_forward", grid=(t // tm, N_HEADS),
        in_specs=[pl.BlockSpec((3, tm, HEAD_DIM), lambda i, h: (0, i, h)),
                  pl.BlockSpec((None, CHUNK, CHUNK), lambda i, h: (h, 0, 0)),
                  pl.BlockSpec((None, CHUNK, 1), lambda i, h: (h, 0, 0)),
                  pl.BlockSpec((1, HEAD_DIM), lambda i, h: (0, h)),
                  pl.BlockSpec((1, HEAD_DIM), lambda i, h: (0, h))],
        out_specs=pl.BlockSpec((tm, HEAD_DIM), lambda i, h: (i, h)),
        out_shape=jax.ShapeDtypeStruct((t, D_MODEL), BF16),
        compiler_params=_params("parallel", "parallel"),
    )(p_a, wm, bs_col, ln_v_g, ln_v_b)


def _mixer_a_backward(p_a, dyab, wm, bs_col, ln_v_g, ln_v_b, tm):
    t = p_a.shape[1]

    def body(p_ref, dy_ref, wm_ref, bs_ref, g_ref, b_ref, dp_ref, dws_ref, dbs_ref, dg_ref, db_ref):
        @pl.when(pl.program_id(1) == 0)
        def _():
            dws_ref[...] = jnp.zeros_like(dws_ref)
            dbs_ref[...] = jnp.zeros_like(dbs_ref)
            dg_ref[...] = jnp.zeros_like(dg_ref)
            db_ref[...] = jnp.zeros_like(db_ref)

        wm_v, bs_v, lng, lnb = wm_ref[...], bs_ref[...], g_ref[...], b_ref[...]
        causal = (lax.broadcasted_iota(jnp.int32, (CHUNK, CHUNK), 1)
                  <= lax.broadcasted_iota(jnp.int32, (CHUNK, CHUNK), 0))

        def chunk(ci, carry):
            rows = pl.ds(pl.multiple_of(ci * CHUNK, CHUNK), CHUNK)
            u = p_ref[0, rows, :].astype(F32)
            v = p_ref[1, rows, :].astype(F32)
            z = p_ref[2, rows, :].astype(F32)
            dy = dy_ref[rows, :].astype(F32)
            ug, dug, dvg, rstd, vhat, vln, mixed, sig = _sgu_chunk_forward(u, v, z, wm_v, bs_v, lng, lnb)
            sz = z * sig
            dmixed = dy * ug * sz
            dp_ref[0, rows, :] = (dy * mixed * sz * dug).astype(BF16)
            dp_ref[2, rows, :] = (dy * ug * mixed * (sig * (1.0 + z * (1.0 - sig)))).astype(BF16)
            dbs_ref[...] += jnp.sum(dmixed, axis=1, keepdims=True)
            dmb = dmixed.astype(BF16)
            dws_ref[...] += jnp.where(causal, _dot(dmb, vln, 1, 1), 0.0)
            dvln = _dot(wm_v, dmb, 0, 0)
            db_ref[...] += jnp.sum(dvln, axis=0, keepdims=True)
            dg_ref[...] += jnp.sum(dvln * vhat, axis=0, keepdims=True)
            dvh = dvln * lng
            m1 = jnp.mean(dvh, axis=-1, keepdims=True)
            m2 = jnp.mean(dvh * vhat, axis=-1, keepdims=True)
            dp_ref[1, rows, :] = (rstd * (dvh - m1 - vhat * m2) * dvg).astype(BF16)
            return carry

        lax.fori_loop(0, tm // CHUNK, chunk, 0)

    return pl.pallas_call(
        body, name="mixer_a_backward", grid=(N_HEADS, t // tm),
        in_specs=[pl.BlockSpec((3, tm, HEAD_DIM), lambda h, i: (0, i, h)),
                  pl.BlockSpec((None, tm, HEAD_DIM), lambda h, i: (0, i, h)),
                  pl.BlockSpec((None, CHUNK, CHUNK), lambda h, i: (h, 0, 0)),
                  pl.BlockSpec((None, CHUNK, 1), lambda h, i: (h, 0, 0)),
                  pl.BlockSpec((1, HEAD_DIM), lambda h, i: (0, h)),
                  pl.BlockSpec((1, HEAD_DIM), lambda h, i: (0, h))],
        out_specs=(pl.BlockSpec((3, tm, HEAD_DIM), lambda h, i: (0, i, h)),
                   pl.BlockSpec((None, CHUNK, CHUNK), lambda h, i: (h, 0, 0)),
                   pl.BlockSpec((None, CHUNK, 1), lambda h, i: (h, 0, 0)),
                   pl.BlockSpec((1, HEAD_DIM), lambda h, i: (0, h)),
                   pl.BlockSpec((1, HEAD_DIM), lambda h, i: (0, h))),
        out_shape=(jax.ShapeDtypeStruct((3, t, D_MODEL), BF16),
                   jax.ShapeDtypeStruct((N_HEADS, CHUNK, CHUNK), F32),
                   jax.ShapeDtypeStruct((N_HEADS, CHUNK, 1), F32),
                   jax.ShapeDtypeStruct((1, D_MODEL), F32), jax.ShapeDtypeStruct((1, D_MODEL), F32)),
        compiler_params=_params("parallel", "arbitrary"),
    )(p_a, dyab, wm, bs_col, ln_v_g, ln_v_b)


HALO = 16


def _conv_taps(h, halo_h, tm):
    row = lax.broadcasted_iota(jnp.int32, h.shape, 0)
    last1 = halo_h[HALO - 1:HALO, :]
    last2 = halo_h[HALO - 2:HALO - 1, :]
    h1 = jnp.where(row == 0, last1, pltpu.roll(h, 1, 0))
    h2 = jnp.where(row == 0, last2, jnp.where(row == 1, last1, pltpu.roll(h, 2, 0)))
    return h1, h2


def _mixer_b_forward(p_b, conv_w, conv_b, tm, tc):
    t = p_b.shape[1]

    def body(p_ref, halo_ref, w_ref, b_ref, o_ref):
        valid = (pl.program_id(1) > 0).astype(F32)
        h = p_ref[1].astype(F32) * p_ref[0].astype(F32)
        halo_h = halo_ref[1].astype(F32) * halo_ref[0].astype(F32) * valid
        h1, h2 = _conv_taps(h, halo_h, tm)
        w = w_ref[...]
        conv = b_ref[...] + w[0:1, :] * h2 + w[1:2, :] * h1 + w[2:3, :] * h
        z = p_ref[3].astype(F32)
        o_ref[...] = (p_ref[2].astype(F32) * conv * (z * _sigmoid(z))).astype(BF16)

    steps = tm // HALO
    return pl.pallas_call(
        body, name="mixer_b_forward", grid=(D_MODEL // tc, t // tm),
        in_specs=[pl.BlockSpec((4, tm, tc), lambda j, i: (0, i, j)),
                  pl.BlockSpec((4, HALO, tc), lambda j, i: (0, jnp.maximum(i * steps - 1, 0), j)),
                  pl.BlockSpec((3, tc), lambda j, i: (0, j)),
                  pl.BlockSpec((1, tc), lambda j, i: (0, j))],
        out_specs=pl.BlockSpec((tm, tc), lambda j, i: (i, j)),
        out_shape=jax.ShapeDtypeStruct((t, D_MODEL), BF16),
        compiler_params=_params("parallel", "parallel"),
    )(p_b, p_b, conv_w, conv_b)


def _mixer_b_backward(p_b, dyab, conv_w, conv_b, tm, tc):
    t = p_b.shape[1]
    n = t // tm

    def body(p_ref, halo_ref, dy_ref, w_ref, b_ref, dp_ref, dw_ref, db_ref, next_ref):
        ii = pl.program_id(1)

        @pl.when(ii == 0)
        def _():
            dw_ref[...] = jnp.zeros_like(dw_ref)
            db_ref[...] = jnp.zeros_like(db_ref)
            next_ref[...] = jnp.zeros_like(next_ref)

        valid = (ii < n - 1).astype(F32)
        xb = p_ref[0].astype(F32)
        cb = p_ref[1].astype(F32)
        bb = p_ref[2].astype(F32)
        z = p_ref[3].astype(F32)
        h = cb * xb
        halo_h = halo_ref[1].astype(F32) * halo_ref[0].astype(F32) * valid
        h1, h2 = _conv_taps(h, halo_h, tm)
        w = w_ref[...]
        w0, w1, w2 = w[0:1, :], w[1:2, :], w[2:3, :]
        conv = b_ref[...] + w0 * h2 + w1 * h1 + w2 * h
        sig = _sigmoid(z)
        sz = z * sig
        dy = dy_ref[...].astype(F32)
        dconv = dy * bb * sz
        dp_ref[2] = (dy * conv * sz).astype(BF16)
        dp_ref[3] = (dy * bb * conv * (sig * (1.0 + z * (1.0 - sig)))).astype(BF16)
        db_ref[...] += jnp.sum(dconv, axis=0, keepdims=True)
        dw_ref[0:1, :] += jnp.sum(dconv * h2, axis=0, keepdims=True)
        dw_ref[1:2, :] += jnp.sum(dconv * h1, axis=0, keepdims=True)
        dw_ref[2:3, :] += jnp.sum(dconv * h, axis=0, keepdims=True)
        row = lax.broadcasted_iota(jnp.int32, h.shape, 0)
        nxt = next_ref[...]
        n0, n1 = nxt[0:1, :], nxt[1:2, :]
        d1 = jnp.where(row == tm - 1, n0, pltpu.roll(dconv, tm - 1, 0))
        d2 = jnp.where(row == tm - 1, n1, jnp.where(row == tm - 2, n0, pltpu.roll(dconv, tm - 2, 0)))
        dh = w2 * dconv + w1 * d1 + w0 * d2
        dp_ref[0] = (dh * cb).astype(BF16)
        dp_ref[1] = (dh * xb).astype(BF16)
        next_ref[...] = dconv[0:8, :]

    steps = tm // HALO
    return pl.pallas_call(
        body, name="mixer_b_backward", grid=(D_MODEL // tc, n),
        in_specs=[pl.BlockSpec((4, tm, tc), lambda j, ii: (0, n - 1 - ii, j)),
                  pl.BlockSpec((4, HALO, tc), lambda j, ii: (0, jnp.maximum((n - 1 - ii) * steps - 1, 0), j)),
                  pl.BlockSpec((None, tm, tc), lambda j, ii: (1, n - 1 - ii, j)),
                  pl.BlockSpec((3, tc), lambda j, ii: (0, j)),
                  pl.BlockSpec((1, tc), lambda j, ii: (0, j))],
        out_specs=(pl.BlockSpec((4, tm, tc), lambda j, ii: (0, n - 1 - ii, j)),
                   pl.BlockSpec((3, tc), lambda j, ii: (0, j)),
                   pl.BlockSpec((1, tc), lambda j, ii: (0, j))),
        out_shape=(jax.ShapeDtypeStruct((4, t, D_MODEL), BF16),
                   jax.ShapeDtypeStruct((3, D_MODEL), F32), jax.ShapeDtypeStruct((1, D_MODEL), F32)),
        scratch_shapes=[pltpu.VMEM((8, tc), F32)],
        compiler_params=_params("parallel", "arbitrary"),
    )(p_b, p_b, dyab, conv_w, conv_b)


def _adam_math(w, g, m, v):
    m = ADAM_B1 * m + (1.0 - ADAM_B1) * g
    v = ADAM_B2 * v + (1.0 - ADAM_B2) * (g * g)
    delta = -ADAM_LR * ((m * ADAM_C1) / (jnp.sqrt(v * ADAM_C2) + ADAM_EPS) + ADAM_WD * w)
    return delta, m, v


def _adam_rows(w, g, m, v, tm, name):
    r, c = w.shape

    def body(w_ref, g_ref, m_ref, v_ref, go_ref, d_ref, mo_ref, vo_ref):
        g = g_ref[...]
        d, mn, vn = _adam_math(w_ref[...], g, m_ref[...], v_ref[...])
        go_ref[...] = g
        d_ref[...] = d
        mo_ref[...] = mn
        vo_ref[...] = vn

    spec = pl.BlockSpec((tm, c), lambda i: (i, 0))
    shape = jax.ShapeDtypeStruct((r, c), F32)
    return pl.pallas_call(
        body, name=name, grid=(r // tm,),
        in_specs=[spec] * 4, out_specs=(spec,) * 4, out_shape=(shape,) * 4,
        compiler_params=_params("parallel"),
    )(w, g, m, v)


def _adam_small(gathered, w, m, v):
    r = w.shape[0]

    def body(ga_ref, w_ref, m_ref, v_ref, g_ref, d_ref, mo_ref, vo_ref):
        g = ga_ref[0]
        for k in range(1, 8):
            g = g + ga_ref[k]
        d, mn, vn = _adam_math(w_ref[...], g, m_ref[...], v_ref[...])
        g_ref[...] = g
        d_ref[...] = d
        mo_ref[...] = mn
        vo_ref[...] = vn

    shape = jax.ShapeDtypeStruct((r, 128), F32)
    return pl.pallas_call(
        body, name="adam_small", out_shape=(shape,) * 4, compiler_params=_params(),
    )(gathered, w, m, v)


def _pack_small(parts):
    rows = []
    for name, r in SMALL_ROWS:
        a = parts[name].astype(F32).reshape(-1)
        pad = r * 128 - a.shape[0]
        if pad:
            a = jnp.concatenate([a, jnp.zeros((pad,), F32)])
        rows.append(a.reshape(r, 128))
    return jnp.concatenate(rows, axis=0)


def _unpack_small(buf, shapes):
    out, r0 = {}, 0
    for name, r in SMALL_ROWS:
        shp = shapes[name]
        n = math.prod(shp)
        out[name] = buf[r0:r0 + r].reshape(-1)[:n].reshape(shp)
        r0 += r
    return out


def kernel(x, w_in, b_gate, ln_v_g, ln_v_b, w_s, b_s, conv_w, conv_b, w_oa, w_ob, w_out, ln_g, ln_b, loss_target, m_w_in, m_b_gate, m_ln_v_g, m_ln_v_b, m_w_s, m_b_s, m_conv_w, m_conv_b, m_w_oa, m_w_ob, m_w_out, m_ln_g, m_ln_b, v_w_in, v_b_gate, v_ln_v_g, v_ln_v_b, v_w_s, v_b_s, v_conv_w, v_conv_b, v_w_oa, v_w_ob, v_w_out, v_ln_g, v_ln_b):
    t = x.shape[1]
    x2 = x[0]
    target = loss_target[0]
    chip = 2 * lax.axis_index("x") + lax.axis_index("y")
    conv_cols = conv_w.shape[2]

    win_b = _cast_rows(w_in[0], 256, "cast_w_in")
    wo_b, wm = _prep_small_weights(w_oa[0], w_ob[0], w_out[0], w_s[0])
    conv_w8 = jnp.concatenate([conv_w[0], jnp.zeros((5, conv_cols), F32)], axis=0)
    w_full, wo_full, cw_full = _all_gather_weights(win_b, wo_b, conv_w8)
    wo3 = wo_full.reshape(3, D_MODEL, D_MODEL)
    conv_w_all = jnp.transpose(cw_full[:, :3, :], (1, 0, 2)).reshape(3, D_MODEL)
    bs_col = b_s[0].reshape(N_HEADS, CHUNK, 1)
    bg = b_gate.reshape(2, 1, D_MODEL)

    xb = _cast_rows(x2, min(512, t), "cast_x")
    tm_p = min(1024, t)
    p_a = _project_in(xb, w_full, 0, 3, tm_p, "project_in_a")
    p_b = _project_in(xb, w_full, 3 * BLOCKS_PER_SEG, 4, tm_p, "project_in_b")
    p_g = _project_in(xb, w_full, 7 * BLOCKS_PER_SEG, 2, tm_p, "project_in_g")
    tm_a = min(512, t)
    ya = _mixer_a_forward(p_a, wm, bs_col, ln_v_g, ln_v_b, tm_a)
    tm_b = min(256, t)
    yb = _mixer_b_forward(p_b, conv_w_all, conv_b, tm_b, 256)
    tm_m = min(512, t)
    merged, oab = _merge_forward(ya, yb, wo3, p_g, bg, tm_m, 512)

    drb, gx, dmerged, d_ln_g, d_ln_b, loss_part = _head(merged, wo3, x2, target, ln_g, ln_b, min(256, t))
    doab, dp_g, d_bg = _gate_backward(dmerged, oab, p_g, bg, tm_m, 512)
    dyab = _branch_backward(doab, wo3, tm_m, 512)
    dp_a, d_ws, d_bs, d_lnv_g, d_lnv_b = _mixer_a_backward(p_a, dyab, wm, bs_col, ln_v_g, ln_v_b, tm_a)
    dp_b, d_cw, d_cb = _mixer_b_backward(p_b, dyab, conv_w_all, conv_b, tm_b, 256)

    tk = min(512, t)
    g_oa = _weight_grad(ya, 0, doab, 0, 1024, 1024, tk, "grad_w_oa")
    g_ob = _weight_grad(yb, 0, doab, 1, 1024, 1024, tk, "grad_w_ob")
    g_out = _weight_grad(merged, 0, drb, 0, 1024, 1024, tk, "grad_w_out")
    g_in = _win_grad(xb, dp_a, 0, None, 1024, tk, "grad_w_in_a")
    g_in = _win_grad(xb, dp_b, 3, g_in, 1024, tk, "grad_w_in_b")
    g_in = _win_grad(xb, dp_g, 7, g_in, 1024, tk, "grad_w_in_g")
    tm_x = min(1024, t)
    gx = _input_grad(dp_a, w_full, 0, gx, tm_x, "grad_x_a")
    gx = _input_grad(dp_b, w_full, 3 * BLOCKS_PER_SEG, gx, tm_x, "grad_x_b")
    grad_x = _input_grad(dp_g, w_full, 7 * BLOCKS_PER_SEG, gx, tm_x, "grad_x_g")

    r_in, *r_o = _exchange_halves(g_in, (g_oa, g_ob, g_out))
    s_in, s_o = _add_halves(g_in, (g_oa, g_ob, g_out), r_in, r_o)
    q_in, q_o = _scatter_to_chips(s_in, s_o)
    f_in, f_o = _sum_chips(q_in, q_o)
    gsum_in, gsum_o = _share_halves(f_in, f_o)

    big = {}
    big["w_in"] = _adam_rows(w_in[0], gsum_in, m_w_in[0], v_w_in[0], 128, "adam_w_in")
    for n, (name, w, m, v) in enumerate((("w_oa", w_oa, m_w_oa, v_w_oa), ("w_ob", w_ob, m_w_ob, v_w_ob),
                                         ("w_out", w_out, m_w_out, v_w_out))):
        big[name] = _adam_rows(w[0], gsum_o[n], m[0], v[0], 256, "adam_" + name)

    small_grads = {"ln_g": d_ln_g, "ln_b": d_ln_b, "b_gate": d_bg, "ln_v_g": d_lnv_g, "ln_v_b": d_lnv_b,
                   "w_s": d_ws, "b_s": d_bs, "conv_w": d_cw, "conv_b": d_cb, "loss": loss_part[:, :1]}
    gathered = _gather_small(_pack_small(small_grads))

    def placed(a):
        return lax.dynamic_update_slice(jnp.zeros((3, D_MODEL), F32), a[0], (0, chip * conv_cols))

    def packed(pre, cw):
        z1 = jnp.zeros((1,), F32)
        return _pack_small({"ln_g": pre["ln_g"], "ln_b": pre["ln_b"], "b_gate": pre["b_gate"],
                            "ln_v_g": pre["ln_v_g"], "ln_v_b": pre["ln_v_b"], "w_s": pre["w_s"], "b_s": pre["b_s"],
                            "conv_w": placed(cw), "conv_b": pre["conv_b"], "loss": z1})

    ws = dict(ln_g=ln_g, ln_b=ln_b, b_gate=b_gate, ln_v_g=ln_v_g, ln_v_b=ln_v_b, w_s=w_s, b_s=b_s, conv_b=conv_b)
    ms = dict(ln_g=m_ln_g, ln_b=m_ln_b, b_gate=m_b_gate, ln_v_g=m_ln_v_g, ln_v_b=m_ln_v_b, w_s=m_w_s, b_s=m_b_s,
              conv_b=m_conv_b)
    vs = dict(ln_g=v_ln_g, ln_b=v_ln_b, b_gate=v_b_gate, ln_v_g=v_ln_v_g, ln_v_b=v_ln_v_b, w_s=v_w_s, b_s=v_b_s,
              conv_b=v_conv_b)
    s_g, s_d, s_m, s_v = _adam_small(gathered, packed(ws, conv_w), packed(ms, m_conv_w), packed(vs, v_conv_w))
    shapes = {"ln_g": ln_g.shape, "ln_b": ln_b.shape, "b_gate": b_gate.shape, "ln_v_g": ln_v_g.shape,
              "ln_v_b": ln_v_b.shape, "w_s": w_s.shape, "b_s": b_s.shape, "conv_w": (1, 3, D_MODEL),
              "conv_b": conv_b.shape, "loss": (1,)}
    small = [_unpack_small(b, shapes) for b in (s_g, s_d, s_m, s_v)]
    for d in small:
        d["conv_w"] = lax.dynamic_slice(d["conv_w"], (0, 0, chip * conv_cols), (1, 3, conv_cols))
    loss = small[0]["loss"][0]

    order = ("w_in", "b_gate", "ln_v_g", "ln_v_b", "w_s", "b_s", "conv_w", "conv_b", "w_oa", "w_ob", "w_out",
             "ln_g", "ln_b")
    outs = [loss, grad_x[None]]
    for which in range(4):
        for name in order:
            outs.append(big[name][which][None] if name in big else small[which][name])
    return tuple(outs)
```

```python
import functools
import math

import jax
import jax.numpy as jnp
from jax import lax
from jax.experimental import pallas as pl
from jax.experimental.pallas import tpu as pltpu

F32 = jnp.float32
BF16 = jnp.bfloat16

D_MODEL = 2048
N_HEADS = 8
HEAD_DIM = D_MODEL // N_HEADS
CHUNK = 128
N_SEG = 9
N_CHIPS = 4
SHARD_COLS = N_SEG * D_MODEL // N_CHIPS
COL_BLOCK = 512
BLOCKS_PER_SHARD = SHARD_COLS // COL_BLOCK
BLOCKS_PER_SEG = D_MODEL // COL_BLOCK
SHARD_ROWS = D_MODEL // N_CHIPS
DN_ALPHA = 2.0 ** 0.25
LN_EPS = 1e-5
GELU_K = math.sqrt(2.0 / math.pi)
GELU_C = 0.044715

ADAM_LR = 0.001
ADAM_B1 = 0.9
ADAM_B2 = 0.999
ADAM_EPS = 1e-08
ADAM_WD = 0.01
ADAM_STEP = 10
ADAM_C1 = 1.0 / (1.0 - ADAM_B1 ** ADAM_STEP)
ADAM_C2 = 1.0 / (1.0 - ADAM_B2 ** ADAM_STEP)

VMEM_LIMIT = 56 * 1024 * 1024
MESH = pl.DeviceIdType.MESH
ANY = pl.BlockSpec(memory_space=pl.ANY)

SMALL_ROWS = (("ln_g", 16), ("ln_b", 16), ("b_gate", 32), ("ln_v_g", 16), ("ln_v_b", 16),
              ("w_s", 1024), ("b_s", 8), ("conv_w", 48), ("conv_b", 16), ("loss", 8))
SMALL_TOTAL = sum(r for _, r in SMALL_ROWS)


def _params(*sem):
    return pltpu.CompilerParams(dimension_semantics=sem, vmem_limit_bytes=VMEM_LIMIT)


def _sigmoid(x):
    return 1.0 / (1.0 + jnp.exp(-x))


def _gelu_and_grad(x):
    x2 = x * x
    th = jnp.tanh(GELU_K * (x + GELU_C * x * x2))
    g = 0.5 * x * (1.0 + th)
    dg = 0.5 * (1.0 + th) + 0.5 * x * (1.0 - th * th) * (GELU_K * (1.0 + 3.0 * GELU_C * x2))
    return g, dg


def _gelu(x):
    return 0.5 * x * (1.0 + jnp.tanh(GELU_K * (x + GELU_C * x * x * x)))


def _dot(a, b, ca=1, cb=0):
    return lax.dot_general(a, b, (((ca,), (cb,)), ((), ())), preferred_element_type=F32)


def _cast_rows(x, tm, name):
    r, c = x.shape

    def body(x_ref, o_ref):
        o_ref[...] = x_ref[...].astype(BF16)

    return pl.pallas_call(
        body, name=name, grid=(r // tm,),
        in_specs=[pl.BlockSpec((tm, c), lambda i: (i, 0))],
        out_specs=pl.BlockSpec((tm, c), lambda i: (i, 0)),
        out_shape=jax.ShapeDtypeStruct((r, c), BF16),
        compiler_params=_params("parallel"),
    )(x)


def _cast_into_slot(w, slot, n_slots, tm, name):
    r, c = w.shape

    def body(s_ref, w_ref, o_ref):
        o_ref[...] = w_ref[...].astype(BF16)

    return pl.pallas_call(
        body, name=name,
        grid_spec=pltpu.PrefetchScalarGridSpec(
            num_scalar_prefetch=1, grid=(r // tm,),
            in_specs=[pl.BlockSpec((tm, c), lambda i, s_ref: (i, 0))],
            out_specs=pl.BlockSpec((None, tm, c), lambda i, s_ref: (s_ref[0], i, 0))),
        out_shape=jax.ShapeDtypeStruct((n_slots, r, c), BF16),
        compiler_params=_params("parallel"),
    )(slot, w)


def _prep_small_weights(w_oa, w_ob, w_out, w_s):
    rows = w_oa.shape[0]

    def body(a_ref, b_ref, c_ref, ws_ref, wo_ref, wm_ref):
        wo_ref[0] = a_ref[...].astype(BF16)
        wo_ref[1] = b_ref[...].astype(BF16)
        wo_ref[2] = c_ref[...].astype(BF16)
        t = lax.broadcasted_iota(jnp.int32, (CHUNK, CHUNK), 0)
        s = lax.broadcasted_iota(jnp.int32, (CHUNK, CHUNK), 1)
        for h in range(N_HEADS):
            wm_ref[h] = jnp.where(s <= t, ws_ref[h], 0.0).astype(BF16)

    return pl.pallas_call(
        body, name="prep_small_weights",
        out_shape=(jax.ShapeDtypeStruct((3, rows, D_MODEL), BF16),
                   jax.ShapeDtypeStruct((N_HEADS, CHUNK, CHUNK), BF16)),
        compiler_params=_params(),
    )(w_oa, w_ob, w_out, w_s)


def _mesh_pos():
    x, y, c = lax.axis_index("x"), lax.axis_index("y"), lax.axis_index("c")
    chips = [(1 - x, y), (x, 1 - y), (1 - x, 1 - y)]
    return x, y, c, chips


def _all_gather_weights(w_pre, wo_b, conv_w8):
    _, d, sc = w_pre.shape
    rows = wo_b.shape[1]
    hd, hr = d // 2, rows // 2

    def body(win_ref, wo_ref, cw_ref, wf_ref, wof_ref, cwf_ref, send_sems, recv_sems, local_sems):
        x, y, c, chips = _mesh_pos()
        me = 2 * x + y
        sibling = (x, y, 1 - c)

        def half(ref_kind, k, cc):
            if ref_kind == 0:
                return wf_ref.at[k, pl.ds(cc * hd, hd), :]
            return wof_ref.at[:, k, pl.ds(cc * hr, hr), :]

        def src_half(ref_kind, cc):
            if ref_kind == 0:
                return win_ref.at[me, pl.ds(cc * hd, hd), :]
            return wo_ref.at[:, pl.ds(cc * hr, hr), :]

        def rcopy(sem, src, dst, to):
            return pltpu.make_async_remote_copy(src_ref=src, dst_ref=dst, send_sem=send_sems.at[sem],
                                                recv_sem=recv_sems.at[sem], device_id=to, device_id_type=MESH)

        locals_ = [pltpu.make_async_copy(wo_ref, wof_ref.at[:, me], local_sems.at[0]),
                   pltpu.make_async_copy(cw_ref, cwf_ref.at[me], local_sems.at[1])]
        for cp in locals_:
            cp.start()
        first = []
        for kind in (0, 1):
            for j, chip in enumerate(chips):
                first.append(rcopy(kind * 6 + j, src_half(kind, c), half(kind, me, c), (*chip, c)))
        for j, chip in enumerate(chips):
            first.append(rcopy(12 + j, cw_ref, cwf_ref.at[me], (*chip, c)))
        for cp in first:
            cp.start()
        passed = []
        for kind in (0, 1):
            for j, (cx, cy) in enumerate(chips):
                k = 2 * cx + cy
                rcopy(kind * 6 + j, src_half(kind, c), half(kind, k, c), sibling).wait_recv()
                fwd = rcopy(kind * 6 + 3 + j, half(kind, k, c), half(kind, k, c), sibling)
                fwd.start()
                passed.append(fwd)
        for j, (cx, cy) in enumerate(chips):
            rcopy(12 + j, cw_ref, cwf_ref.at[2 * cx + cy], sibling).wait_recv()
        for kind in (0, 1):
            for j, (cx, cy) in enumerate(chips):
                k = 2 * cx + cy
                rcopy(kind * 6 + 3 + j, half(kind, k, 1 - c), half(kind, k, 1 - c), sibling).wait_recv()
        for cp in first + passed:
            cp.wait_send()
        for cp in locals_:
            cp.wait()

    return pl.pallas_call(
        body, name="all_gather_weights",
        in_specs=[ANY, ANY, ANY], out_specs=(ANY, ANY, ANY),
        out_shape=(jax.ShapeDtypeStruct((N_CHIPS, d, sc), BF16),
                   jax.ShapeDtypeStruct((3, N_CHIPS, rows, D_MODEL), BF16),
                   jax.ShapeDtypeStruct((N_CHIPS,) + conv_w8.shape, F32)),
        scratch_shapes=[pltpu.SemaphoreType.DMA((15,)), pltpu.SemaphoreType.DMA((15,)),
                        pltpu.SemaphoreType.DMA((2,))],
        input_output_aliases={0: 0},
        compiler_params=pltpu.CompilerParams(has_side_effects=True),
    )(w_pre, wo_b, conv_w8)


def _gather_small(small):
    def body(s_ref, o_ref, send_sems, recv_sems, local_sem):
        x, y, c, _ = _mesh_pos()
        me = 4 * x + 2 * y + c
        mine = pltpu.make_async_copy(s_ref, o_ref.at[me], local_sem)
        mine.start()
        sends = []
        for k in range(1, 8):
            dx, dy, dc = (k >> 2) & 1, (k >> 1) & 1, k & 1
            to = (x ^ dx, y ^ dy, c ^ dc)
            cp = pltpu.make_async_remote_copy(src_ref=s_ref, dst_ref=o_ref.at[me], send_sem=send_sems.at[k - 1],
                                              recv_sem=recv_sems.at[k - 1], device_id=to, device_id_type=MESH)
            cp.start()
            sends.append(cp)
        for k in range(1, 8):
            dx, dy, dc = (k >> 2) & 1, (k >> 1) & 1, k & 1
            frm = 4 * (x ^ dx) + 2 * (y ^ dy) + (c ^ dc)
            pltpu.make_async_remote_copy(src_ref=s_ref, dst_ref=o_ref.at[frm], send_sem=send_sems.at[k - 1],
                                         recv_sem=recv_sems.at[k - 1], device_id=(x, y, c),
                                         device_id_type=MESH).wait_recv()
        for cp in sends:
            cp.wait_send()
        mine.wait()

    return pl.pallas_call(
        body, name="gather_small",
        in_specs=[ANY], out_specs=ANY,
        out_shape=jax.ShapeDtypeStruct((8,) + small.shape, small.dtype),
        scratch_shapes=[pltpu.SemaphoreType.DMA((7,)), pltpu.SemaphoreType.DMA((7,)), pltpu.SemaphoreType.DMA],
        compiler_params=pltpu.CompilerParams(has_side_effects=True),
    )(small)


def _exchange_halves(g_in, g_o):
    d, c9 = g_in.shape
    hd = d // 2
    hr = SHARD_ROWS // 2
    g_o4 = [g.reshape(N_CHIPS, 2, hr, D_MODEL) for g in g_o]

    def body(gi_ref, ga_ref, gb_ref, gc_ref, ri_ref, ra_ref, rb_ref, rc_ref, send_sems, recv_sems):
        x, y, c, _ = _mesh_pos()
        sibling = (x, y, 1 - c)
        cps = [pltpu.make_async_remote_copy(src_ref=gi_ref.at[pl.ds((1 - c) * hd, hd), :], dst_ref=ri_ref,
                                            send_sem=send_sems.at[0], recv_sem=recv_sems.at[0],
                                            device_id=sibling, device_id_type=MESH)]
        for n, (g_ref, r_ref) in enumerate(((ga_ref, ra_ref), (gb_ref, rb_ref), (gc_ref, rc_ref))):
            cps.append(pltpu.make_async_remote_copy(src_ref=g_ref.at[:, 1 - c], dst_ref=r_ref,
                                                    send_sem=send_sems.at[1 + n], recv_sem=recv_sems.at[1 + n],
                                                    device_id=sibling, device_id_type=MESH))
        for cp in cps:
            cp.start()
        for cp in cps:
            cp.wait()

    o_shape = jax.ShapeDtypeStruct((N_CHIPS, hr, D_MODEL), BF16)
    return pl.pallas_call(
        body, name="rs_exchange_halves",
        in_specs=[ANY] * 4, out_specs=(ANY,) * 4,
        out_shape=(jax.ShapeDtypeStruct((hd, c9), BF16), o_shape, o_shape, o_shape),
        scratch_shapes=[pltpu.SemaphoreType.DMA((4,)), pltpu.SemaphoreType.DMA((4,))],
        compiler_params=pltpu.CompilerParams(has_side_effects=True),
    )(g_in, *g_o4)


def _add_halves(g_in, g_o, r_in, r_o):
    d, c9 = g_in.shape
    hd = d // 2
    hr = SHARD_ROWS // 2
    core = lax.axis_index("c").astype(jnp.int32).reshape(1)
    tm = min(512, hd)
    nb = hd // tm

    def body_in(c_ref, g_ref, r_ref, o_ref):
        o_ref[...] = (g_ref[...].astype(F32) + r_ref[...].astype(F32)).astype(BF16)

    s_in = pl.pallas_call(
        body_in, name="rs_add_halves_in",
        grid_spec=pltpu.PrefetchScalarGridSpec(
            num_scalar_prefetch=1, grid=(N_CHIPS, nb),
            in_specs=[pl.BlockSpec((tm, SHARD_COLS), lambda k, i, c_ref: (c_ref[0] * nb + i, k)),
                      pl.BlockSpec((tm, SHARD_COLS), lambda k, i, c_ref: (i, k))],
            out_specs=pl.BlockSpec((None, tm, SHARD_COLS), lambda k, i, c_ref: (k, i, 0))),
        out_shape=jax.ShapeDtypeStruct((N_CHIPS, hd, SHARD_COLS), BF16),
        compiler_params=_params("parallel", "parallel"),
    )(core, g_in, r_in)

    g_o4 = [g.reshape(N_CHIPS, 2, hr, D_MODEL) for g in g_o]

    def body_o(c_ref, ga_ref, gb_ref, gc_ref, ra_ref, rb_ref, rc_ref, o_ref):
        for n, (g_ref, r_ref) in enumerate(((ga_ref, ra_ref), (gb_ref, rb_ref), (gc_ref, rc_ref))):
            o_ref[n] = (g_ref[...].astype(F32) + r_ref[...].astype(F32)).astype(BF16)

    gspec = pl.BlockSpec((None, None, hr, D_MODEL), lambda k, c_ref: (k, c_ref[0], 0, 0))
    rspec = pl.BlockSpec((None, hr, D_MODEL), lambda k, c_ref: (k, 0, 0))
    s_o = pl.pallas_call(
        body_o, name="rs_add_halves_o",
        grid_spec=pltpu.PrefetchScalarGridSpec(
            num_scalar_prefetch=1, grid=(N_CHIPS,),
            in_specs=[gspec] * 3 + [rspec] * 3,
            out_specs=pl.BlockSpec((3, None, hr, D_MODEL), lambda k, c_ref: (0, k, 0, 0))),
        out_shape=jax.ShapeDtypeStruct((3, N_CHIPS, hr, D_MODEL), BF16),
        compiler_params=_params("parallel"),
    )(core, *g_o4, *r_o)
    return s_in, s_o


def _scatter_to_chips(s_in, s_o):
    _, hd, sc = s_in.shape
    hr = s_o.shape[2]

    def body(si_ref, so_ref, ri_ref, ro_ref, send_sems, recv_sems):
        x, y, c, chips = _mesh_pos()
        me = 2 * x + y
        sends = []
        for j, (cx, cy) in enumerate(chips):
            k = 2 * cx + cy
            sends.append(pltpu.make_async_remote_copy(src_ref=si_ref.at[k], dst_ref=ri_ref.at[me],
                                                      send_sem=send_sems.at[j], recv_sem=recv_sems.at[j],
                                                      device_id=(cx, cy, c), device_id_type=MESH))
            sends.append(pltpu.make_async_remote_copy(src_ref=so_ref.at[:, k], dst_ref=ro_ref.at[me],
                                                      send_sem=send_sems.at[3 + j], recv_sem=recv_sems.at[3 + j],
                                                      device_id=(cx, cy, c), device_id_type=MESH))
        for cp in sends:
            cp.start()
        for j, (cx, cy) in enumerate(chips):
            k = 2 * cx + cy
            pltpu.make_async_remote_copy(src_ref=si_ref.at[k], dst_ref=ri_ref.at[k], send_sem=send_sems.at[j],
                                         recv_sem=recv_sems.at[j], device_id=(x, y, c),
                                         device_id_type=MESH).wait_recv()
            pltpu.make_async_remote_copy(src_ref=so_ref.at[:, k], dst_ref=ro_ref.at[k], send_sem=send_sems.at[3 + j],
                                         recv_sem=recv_sems.at[3 + j], device_id=(x, y, c),
                                         device_id_type=MESH).wait_recv()
        for cp in sends:
            cp.wait_send()

    return pl.pallas_call(
        body, name="rs_scatter_to_chips",
        in_specs=[ANY, ANY], out_specs=(ANY, ANY),
        out_shape=(jax.ShapeDtypeStruct((N_CHIPS, hd, sc), BF16),
                   jax.ShapeDtypeStruct((N_CHIPS, 3, hr, D_MODEL), BF16)),
        scratch_shapes=[pltpu.SemaphoreType.DMA((6,)), pltpu.SemaphoreType.DMA((6,))],
        compiler_params=pltpu.CompilerParams(has_side_effects=True),
    )(s_in, s_o)


def _sum_chips(r_in, r_o, s_in, s_o):
    _, hd, sc = r_in.shape
    hr = r_o.shape[2]
    tm = min(256, hd)
    nb = hd // tm
    pos = jnp.stack([2 * lax.axis_index("x") + lax.axis_index("y"), lax.axis_index("c")]).astype(jnp.int32)

    def chip_sum(pos_ref, r_ref, s_ref):
        acc = None
        for k in range(N_CHIPS):
            term = jnp.where(pos_ref[0] == k, s_ref[...], r_ref[k]).astype(F32)
            acc = term if acc is None else acc + term
        return acc

    def body_in(pos_ref, r_ref, s_ref, o_ref):
        o_ref[...] = chip_sum(pos_ref, r_ref, s_ref)

    f_in = pl.pallas_call(
        body_in, name="rs_sum_chips_in",
        grid_spec=pltpu.PrefetchScalarGridSpec(
            num_scalar_prefetch=1, grid=(nb,),
            in_specs=[pl.BlockSpec((N_CHIPS, tm, sc), lambda i, p: (0, i, 0)),
                      pl.BlockSpec((None, tm, sc), lambda i, p: (p[0], i, 0))],
            out_specs=pl.BlockSpec((tm, sc), lambda i, p: (p[1] * nb + i, 0))),
        out_shape=jax.ShapeDtypeStruct((2 * hd, sc), F32),
        compiler_params=_params("parallel"),
    )(pos, r_in, s_in)

    def body_o(pos_ref, r_ref, s_ref, o_ref):
        o_ref[...] = chip_sum(pos_ref, r_ref, s_ref)

    f_o = pl.pallas_call(
        body_o, name="rs_sum_chips_o",
        grid_spec=pltpu.PrefetchScalarGridSpec(
            num_scalar_prefetch=1, grid=(3,),
            in_specs=[pl.BlockSpec((N_CHIPS, None, hr, D_MODEL), lambda n, p: (0, n, 0, 0)),
                      pl.BlockSpec((None, None, hr, D_MODEL), lambda n, p: (n, p[0], 0, 0))],
            out_specs=pl.BlockSpec((None, hr, D_MODEL), lambda n, p: (n, p[1], 0))),
        out_shape=jax.ShapeDtypeStruct((3, 2 * hr, D_MODEL), F32),
        compiler_params=_params("parallel"),
    )(pos, r_o, s_o)
    return f_in, f_o


def _share_halves(f_in, f_o):
    hd, sc = f_in.shape[0] // 2, f_in.shape[1]
    hr = f_o.shape[1] // 2

    def body(fi_ref, fo_ref, gi_ref, go_ref, send_sems, recv_sems):
        x, y, c, _ = _mesh_pos()
        sibling = (x, y, 1 - c)

        def halves(cc):
            rows_i, rows_o = pl.ds(cc * hd, hd), pl.ds(cc * hr, hr)
            return (fi_ref.at[rows_i, :], gi_ref.at[rows_i, :]), (fo_ref.at[:, rows_o, :], go_ref.at[:, rows_o, :])

        def copies(cc):
            return [pltpu.make_async_remote_copy(src_ref=src, dst_ref=dst, send_sem=send_sems.at[n],
                                                 recv_sem=recv_sems.at[n], device_id=sibling, device_id_type=MESH)
                    for n, (src, dst) in enumerate(halves(cc))]

        sends = copies(c)
        for cp in sends:
            cp.start()
        for cp in copies(1 - c):
            cp.wait_recv()
        for cp in sends:
            cp.wait_send()

    return pl.pallas_call(
        body, name="rs_share_halves",
        in_specs=[ANY, ANY], out_specs=(ANY, ANY),
        out_shape=(jax.ShapeDtypeStruct(f_in.shape, F32), jax.ShapeDtypeStruct(f_o.shape, F32)),
        scratch_shapes=[pltpu.SemaphoreType.DMA((2,)), pltpu.SemaphoreType.DMA((2,))],
        input_output_aliases={0: 0, 1: 1},
        compiler_params=pltpu.CompilerParams(has_side_effects=True),
    )(f_in, f_o)


def _project_in(xb, w_full, first_block, n_seg, tm, name):
    t = xb.shape[0]
    nj = n_seg * BLOCKS_PER_SEG

    def body(x_ref, w_ref, o_ref):
        o_ref[...] = _dot(x_ref[...], w_ref[...]).astype(BF16)

    def w_map(i, j):
        g = first_block + j
        return (g // BLOCKS_PER_SHARD, 0, g % BLOCKS_PER_SHARD)

    return pl.pallas_call(
        body, name=name, grid=(t // tm, nj),
        in_specs=[pl.BlockSpec((tm, D_MODEL), lambda i, j: (i, 0)),
                  pl.BlockSpec((None, D_MODEL, COL_BLOCK), w_map)],
        out_specs=pl.BlockSpec((None, tm, COL_BLOCK), lambda i, j: (j // BLOCKS_PER_SEG, i, j % BLOCKS_PER_SEG)),
        out_shape=jax.ShapeDtypeStruct((n_seg, t, D_MODEL), BF16),
        compiler_params=_params("parallel", "parallel"),
    )(xb, w_full)


def _merge_forward(ya, yb, wo3, p_g, bg, tm, tn):
    t = ya.shape[0]

    def body(ya_ref, yb_ref, wa_ref, wb_ref, g_ref, bg_ref, m_ref, oab_ref):
        oa = _dot(ya_ref[...], wa_ref[...])
        ob = _dot(yb_ref[...], wb_ref[...])
        ga = _sigmoid(g_ref[0].astype(F32) + bg_ref[0])
        gb = _sigmoid(g_ref[1].astype(F32) + bg_ref[1])
        m_ref[...] = (ga * oa + gb * ob).astype(BF16)
        oab_ref[0] = oa.astype(BF16)
        oab_ref[1] = ob.astype(BF16)

    return pl.pallas_call(
        body, name="merge_forward", grid=(t // tm, D_MODEL // tn),
        in_specs=[pl.BlockSpec((tm, D_MODEL), lambda i, j: (i, 0)),
                  pl.BlockSpec((tm, D_MODEL), lambda i, j: (i, 0)),
                  pl.BlockSpec((None, D_MODEL, tn), lambda i, j: (0, 0, j)),
                  pl.BlockSpec((None, D_MODEL, tn), lambda i, j: (1, 0, j)),
                  pl.BlockSpec((2, tm, tn), lambda i, j: (0, i, j)),
                  pl.BlockSpec((2, 1, tn), lambda i, j: (0, 0, j))],
        out_specs=(pl.BlockSpec((tm, tn), lambda i, j: (i, j)),
                   pl.BlockSpec((2, tm, tn), lambda i, j: (0, i, j))),
        out_shape=(jax.ShapeDtypeStruct((t, D_MODEL), BF16), jax.ShapeDtypeStruct((2, t, D_MODEL), BF16)),
        compiler_params=_params("parallel", "parallel"),
    )(ya, yb, wo3, wo3, p_g, bg)


def _head(merged, wo3, x, target, ln_g, ln_b, tm):
    t = x.shape[0]
    inv_d = 1.0 / D_MODEL

    def body(m_ref, w_ref, x_ref, t_ref, g_ref, b_ref, dr_ref, gx_ref, dm_ref, dg_ref, db_ref, loss_ref):
        i = pl.program_id(0)

        @pl.when(i == 0)
        def _():
            dg_ref[...] = jnp.zeros_like(dg_ref)
            db_ref[...] = jnp.zeros_like(db_ref)
            loss_ref[...] = jnp.zeros_like(loss_ref)

        w = w_ref[...]
        out = _dot(m_ref[...], w)
        r = DN_ALPHA * x_ref[...] + out
        mu = jnp.mean(r, axis=-1, keepdims=True)
        xc = r - mu
        var = jnp.mean(xc * xc, axis=-1, keepdims=True)
        rstd = lax.rsqrt(var + LN_EPS)
        xhat = xc * rstd
        g = g_ref[...]
        e = xhat * g + b_ref[...] - t_ref[...]
        se = jnp.sum(jnp.sum(e * e, axis=1, keepdims=True), axis=0, keepdims=True)
        loss_ref[...] += jnp.broadcast_to((0.5 * inv_d) * se, loss_ref.shape)
        dy = e * inv_d
        db_ref[...] += jnp.sum(dy, axis=0, keepdims=True)
        dg_ref[...] += jnp.sum(dy * xhat, axis=0, keepdims=True)
        dxh = dy * g
        m1 = jnp.mean(dxh, axis=-1, keepdims=True)
        m2 = jnp.mean(dxh * xhat, axis=-1, keepdims=True)
        dr = rstd * (dxh - m1 - xhat * m2)
        gx_ref[...] = DN_ALPHA * dr
        drb = dr.astype(BF16)
        dr_ref[...] = drb
        dm_ref[...] = _dot(drb, w, 1, 1).astype(BF16)

    row = pl.BlockSpec((tm, D_MODEL), lambda i: (i, 0))
    vec = pl.BlockSpec((1, D_MODEL), lambda i: (0, 0))
    return pl.pallas_call(
        body, name="head", grid=(t // tm,),
        in_specs=[row, pl.BlockSpec((None, D_MODEL, D_MODEL), lambda i: (2, 0, 0)), row, row, vec, vec],
        out_specs=(row, row, row, vec, vec, pl.BlockSpec((1, 128), lambda i: (0, 0))),
        out_shape=(jax.ShapeDtypeStruct((t, D_MODEL), BF16), jax.ShapeDtypeStruct((t, D_MODEL), F32),
                   jax.ShapeDtypeStruct((t, D_MODEL), BF16), jax.ShapeDtypeStruct((1, D_MODEL), F32),
                   jax.ShapeDtypeStruct((1, D_MODEL), F32), jax.ShapeDtypeStruct((1, 128), F32)),
        compiler_params=_params("arbitrary"),
    )(merged, wo3, x, target, ln_g, ln_b)


def _gate_backward(dmerged, oab, p_g, bg, tm, tn):
    t = dmerged.shape[0]

    def body(dm_ref, oab_ref, g_ref, bg_ref, do_ref, dpg_ref, dbg_ref):
        @pl.when(pl.program_id(1) == 0)
        def _():
            dbg_ref[...] = jnp.zeros_like(dbg_ref)

        dm = dm_ref[...].astype(F32)
        for n in range(2):
            gate = _sigmoid(g_ref[n].astype(F32) + bg_ref[n])
            do_ref[n] = (dm * gate).astype(BF16)
            dgate = dm * oab_ref[n].astype(F32) * gate * (1.0 - gate)
            dpg_ref[n] = dgate.astype(BF16)
            dbg_ref[n] += jnp.sum(dgate, axis=0, keepdims=True)

    pair = pl.BlockSpec((2, tm, tn), lambda j, i: (0, i, j))
    vec = pl.BlockSpec((2, 1, tn), lambda j, i: (0, 0, j))
    return pl.pallas_call(
        body, name="gate_backward", grid=(D_MODEL // tn, t // tm),
        in_specs=[pl.BlockSpec((tm, tn), lambda j, i: (i, j)), pair, pair, vec],
        out_specs=(pair, pair, vec),
        out_shape=(jax.ShapeDtypeStruct((2, t, D_MODEL), BF16), jax.ShapeDtypeStruct((2, t, D_MODEL), BF16),
                   jax.ShapeDtypeStruct((2, 1, D_MODEL), F32)),
        compiler_params=_params("parallel", "arbitrary"),
    )(dmerged, oab, p_g, bg)


def _branch_backward(doab, wo3, tm, tn):
    t = doab.shape[1]

    def body(d_ref, wa_ref, wb_ref, o_ref):
        o_ref[0] = _dot(d_ref[0], wa_ref[...], 1, 1).astype(BF16)
        o_ref[1] = _dot(d_ref[1], wb_ref[...], 1, 1).astype(BF16)

    return pl.pallas_call(
        body, name="branch_backward", grid=(t // tm, D_MODEL // tn),
        in_specs=[pl.BlockSpec((2, tm, D_MODEL), lambda i, j: (0, i, 0)),
                  pl.BlockSpec((None, tn, D_MODEL), lambda i, j: (0, j, 0)),
                  pl.BlockSpec((None, tn, D_MODEL), lambda i, j: (1, j, 0))],
        out_specs=pl.BlockSpec((2, tm, tn), lambda i, j: (0, i, j)),
        out_shape=jax.ShapeDtypeStruct((2, t, D_MODEL), BF16),
        compiler_params=_params("parallel", "parallel"),
    )(doab, wo3, wo3)


def _weight_grad(a, a_sel, b, b_sel, tm, tn, tk, name):
    t = a.shape[-2]
    nk = t // tk

    def body(a_ref, b_ref, o_ref, acc_ref):
        k = pl.program_id(2)
        part = _dot(a_ref[...], b_ref[...], 0, 0)

        @pl.when(k == 0)
        def _():
            acc_ref[...] = part

        @pl.when(k > 0)
        def _():
            acc_ref[...] += part

        @pl.when(k == nk - 1)
        def _():
            o_ref[...] = acc_ref[...].astype(BF16)

    def spec(arr, sel, width, which):
        if arr.ndim == 2:
            return pl.BlockSpec((tk, width), lambda i, j, k: (k, (i, j)[which]))
        return pl.BlockSpec((None, tk, width), lambda i, j, k: (sel, k, (i, j)[which]))

    return pl.pallas_call(
        body, name=name, grid=(D_MODEL // tm, D_MODEL // tn, nk),
        in_specs=[spec(a, a_sel, tm, 0), spec(b, b_sel, tn, 1)],
        out_specs=pl.BlockSpec((tm, tn), lambda i, j, k: (i, j)),
        out_shape=jax.ShapeDtypeStruct((D_MODEL, D_MODEL), BF16),
        scratch_shapes=[pltpu.VMEM((tm, tn), F32)],
        compiler_params=_params("parallel", "parallel", "arbitrary"),
    )(a, b)


def _win_grad(xb, dp, first_seg, g_prev, tn, tk, name):
    n_seg, t, _ = dp.shape
    nk = t // tk
    per_seg = D_MODEL // tn
    nj = n_seg * per_seg

    def body(*refs):
        if g_prev is None:
            x_ref, dp_ref, o_ref, acc_ref = refs
        else:
            x_ref, dp_ref, _, o_ref, acc_ref = refs
        k = pl.program_id(1)
        part = _dot(x_ref[...], dp_ref[...], 0, 0)

        @pl.when(k == 0)
        def _():
            acc_ref[...] = part

        @pl.when(k > 0)
        def _():
            acc_ref[...] += part

        @pl.when(k == nk - 1)
        def _():
            o_ref[...] = acc_ref[...].astype(BF16)

    in_specs = [pl.BlockSpec((tk, D_MODEL), lambda j, k: (k, 0)),
                pl.BlockSpec((None, tk, tn), lambda j, k: (j // per_seg, k, j % per_seg))]
    args = [xb, dp]
    aliases = {}
    if g_prev is not None:
        in_specs.append(ANY)
        args.append(g_prev)
        aliases = {2: 0}
    return pl.pallas_call(
        body, name=name, grid=(nj, nk),
        in_specs=in_specs,
        out_specs=pl.BlockSpec((D_MODEL, tn), lambda j, k: (0, first_seg * per_seg + j)),
        out_shape=jax.ShapeDtypeStruct((D_MODEL, N_SEG * D_MODEL), BF16),
        scratch_shapes=[pltpu.VMEM((D_MODEL, tn), F32)],
        input_output_aliases=aliases,
        compiler_params=_params("parallel", "arbitrary"),
    )(*args)


def _input_grad(dp, w_full, first_block, gx, tm, name):
    n_seg, t, _ = dp.shape
    nk = n_seg * BLOCKS_PER_SEG

    def body(dp_ref, w_ref, gx_ref, o_ref):
        k = pl.program_id(1)
        part = _dot(dp_ref[...], w_ref[...], 1, 1)

        @pl.when(k == 0)
        def _():
            o_ref[...] = gx_ref[...] + part

        @pl.when(k > 0)
        def _():
            o_ref[...] += part

    def w_map(i, k):
        g = first_block + k
        return (g // BLOCKS_PER_SHARD, 0, g % BLOCKS_PER_SHARD)

    return pl.pallas_call(
        body, name=name, grid=(t // tm, nk),
        in_specs=[pl.BlockSpec((None, tm, COL_BLOCK), lambda i, k: (k // BLOCKS_PER_SEG, i, k % BLOCKS_PER_SEG)),
                  pl.BlockSpec((None, D_MODEL, COL_BLOCK), w_map),
                  pl.BlockSpec((tm, D_MODEL), lambda i, k: (i, 0))],
        out_specs=pl.BlockSpec((tm, D_MODEL), lambda i, k: (i, 0)),
        out_shape=jax.ShapeDtypeStruct((t, D_MODEL), F32),
        input_output_aliases={2: 0},
        compiler_params=_params("parallel", "arbitrary"),
    )(dp, w_full, gx)


def _sgu_chunk_forward(u, v, z, wm, bs, lng, lnb):
    ug, dug = _gelu_and_grad(u)
    vg, dvg = _gelu_and_grad(v)
    mu = jnp.mean(vg, axis=-1, keepdims=True)
    xc = vg - mu
    var = jnp.mean(xc * xc, axis=-1, keepdims=True)
    rstd = lax.rsqrt(var + LN_EPS)
    vhat = xc * rstd
    vln = (vhat * lng + lnb).astype(BF16)
    mixed = _dot(wm, vln) + bs
    sig = _sigmoid(z)
    return ug, dug, dvg, rstd, vhat, vln, mixed, sig


def _mixer_a_forward(p_a, wm, bs_col, ln_v_g, ln_v_b, tm):
    t = p_a.shape[1]

    def body(p_ref, wm_ref, bs_ref, g_ref, b_ref, o_ref):
        wm_v, bs_v, lng, lnb = wm_ref[...], bs_ref[...], g_ref[...], b_ref[...]

        def chunk(ci, carry):
            rows = pl.ds(pl.multiple_of(ci * CHUNK, CHUNK), CHUNK)
            u = p_ref[0, rows, :].astype(F32)
            v = p_ref[1, rows, :].astype(F32)
            z = p_ref[2, rows, :].astype(F32)
            ug, _, _, _, _, _, mixed, sig = _sgu_chunk_forward(u, v, z, wm_v, bs_v, lng, lnb)
            o_ref[rows, :] = (ug * mixed * (z * sig)).astype(BF16)
            return carry

        lax.fori_loop(0, tm // CHUNK, chunk, 0)

    return pl.pallas_call(
        body, name="mixer_a_forward", grid=(t // tm, N_HEADS),
        in_specs=[pl.BlockSpec((3, tm, HEAD_DIM), lambda i, h: (0, i, h)),
                  pl.BlockSpec((None, CHUNK, CHUNK), lambda i, h: (h, 0, 0)),
                  pl.BlockSpec((None, CHUNK, 1), lambda i, h: (h, 0, 0)),
                  pl.BlockSpec((1, HEAD_DIM), lambda i, h: (0, h)),
                  pl.BlockSpec((1, HEAD_DIM), lambda i, h: (0, h))],
        out_specs=pl.BlockSpec((tm, HEAD_DIM), lambda i, h: (i, h)),
        out_shape=jax.ShapeDtypeStruct((t, D_MODEL), BF16),
        compiler_params=_params("parallel", "parallel"),
    )(p_a, wm, bs_col, ln_v_g, ln_v_b)


def _mixer_a_backward(p_a, dyab, wm, bs_col, ln_v_g, ln_v_b, tm):
    t = p_a.shape[1]

    def body(p_ref, dy_ref, wm_ref, bs_ref, g_ref, b_ref, dp_ref, dws_ref, dbs_ref, dg_ref, db_ref):
        @pl.when(pl.program_id(1) == 0)
        def _():
            dws_ref[...] = jnp.zeros_like(dws_ref)
            dbs_ref[...] = jnp.zeros_like(dbs_ref)
            dg_ref[...] = jnp.zeros_like(dg_ref)
            db_ref[...] = jnp.zeros_like(db_ref)

        wm_v, bs_v, lng, lnb = wm_ref[...], bs_ref[...], g_ref[...], b_ref[...]
        causal = (lax.broadcasted_iota(jnp.int32, (CHUNK, CHUNK), 1)
                  <= lax.broadcasted_iota(jnp.int32, (CHUNK, CHUNK), 0))

        def chunk(ci, carry):
            rows = pl.ds(pl.multiple_of(ci * CHUNK, CHUNK), CHUNK)
            u = p_ref[0, rows, :].astype(F32)
            v = p_ref[1, rows, :].astype(F32)
            z = p_ref[2, rows, :].astype(F32)
            dy = dy_ref[rows, :].astype(F32)
            ug, dug, dvg, rstd, vhat, vln, mixed, sig = _sgu_chunk_forward(u, v, z, wm_v, bs_v, lng, lnb)
            sz = z * sig
            dmixed = dy * ug * sz
            dp_ref[0, rows, :] = (dy * mixed * sz * dug).astype(BF16)
            dp_ref[2, rows, :] = (dy * ug * mixed * (sig * (1.0 + z * (1.0 - sig)))).astype(BF16)
            dbs_ref[...] += jnp.sum(dmixed, axis=1, keepdims=True)
            dmb = dmixed.astype(BF16)
            dws_ref[...] += jnp.where(causal, _dot(dmb, vln, 1, 1), 0.0)
            dvln = _dot(wm_v, dmb, 0, 0)
            db_ref[...] += jnp.sum(dvln, axis=0, keepdims=True)
            dg_ref[...] += jnp.sum(dvln * vhat, axis=0, keepdims=True)
            dvh = dvln * lng
            m1 = jnp.mean(dvh, axis=-1, keepdims=True)
            m2 = jnp.mean(dvh * vhat, axis=-1, keepdims=True)
            dp_ref[1, rows, :] = (rstd * (dvh - m1 - vhat * m2) * dvg).astype(BF16)
            return carry

        lax.fori_loop(0, tm // CHUNK, chunk, 0)

    return pl.pallas_call(
        body, name="mixer_a_backward", grid=(N_HEADS, t // tm),
        in_specs=[pl.BlockSpec((3, tm, HEAD_DIM), lambda h, i: (0, i, h)),
                  pl.BlockSpec((None, tm, HEAD_DIM), lambda h, i: (0, i, h)),
                  pl.BlockSpec((None, CHUNK, CHUNK), lambda h, i: (h, 0, 0)),
                  pl.BlockSpec((None, CHUNK, 1), lambda h, i: (h, 0, 0)),
                  pl.BlockSpec((1, HEAD_DIM), lambda h, i: (0, h)),
                  pl.BlockSpec((1, HEAD_DIM), lambda h, i: (0, h))],
        out_specs=(pl.BlockSpec((3, tm, HEAD_DIM), lambda h, i: (0, i, h)),
                   pl.BlockSpec((None, CHUNK, CHUNK), lambda h, i: (h, 0, 0)),
                   pl.BlockSpec((None, CHUNK, 1), lambda h, i: (h, 0, 0)),
                   pl.BlockSpec((1, HEAD_DIM), lambda h, i: (0, h)),
                   pl.BlockSpec((1, HEAD_DIM), lambda h, i: (0, h))),
        out_shape=(jax.ShapeDtypeStruct((3, t, D_MODEL), BF16),
                   jax.ShapeDtypeStruct((N_HEADS, CHUNK, CHUNK), F32),
                   jax.ShapeDtypeStruct((N_HEADS, CHUNK, 1), F32),
                   jax.ShapeDtypeStruct((1, D_MODEL), F32), jax.ShapeDtypeStruct((1, D_MODEL), F32)),
        compiler_params=_params("parallel", "arbitrary"),
    )(p_a, dyab, wm, bs_col, ln_v_g, ln_v_b)


HALO = 16


def _conv_taps(h, halo_h, tm):
    row = lax.broadcasted_iota(jnp.int32, h.shape, 0)
    last1 = halo_h[HALO - 1:HALO, :]
    last2 = halo_h[HALO - 2:HALO - 1, :]
    h1 = jnp.where(row == 0, last1, pltpu.roll(h, 1, 0))
    h2 = jnp.where(row == 0, last2, jnp.where(row == 1, last1, pltpu.roll(h, 2, 0)))
    return h1, h2


def _mixer_b_forward(p_b, conv_w, conv_b, tm, tc):
    t = p_b.shape[1]

    def body(p_ref, halo_ref, w_ref, b_ref, o_ref):
        valid = (pl.program_id(1) > 0).astype(F32)
        h = p_ref[1].astype(F32) * p_ref[0].astype(F32)
        halo_h = halo_ref[1].astype(F32) * halo_ref[0].astype(F32) * valid
        h1, h2 = _conv_taps(h, halo_h, tm)
        w = w_ref[...]
        conv = b_ref[...] + w[0:1, :] * h2 + w[1:2, :] * h1 + w[2:3, :] * h
        z = p_ref[3].astype(F32)
        o_ref[...] = (p_ref[2].astype(F32) * conv * (z * _sigmoid(z))).astype(BF16)

    steps = tm // HALO
    return pl.pallas_call(
        body, name="mixer_b_forward", grid=(D_MODEL // tc, t // tm),
        in_specs=[pl.BlockSpec((4, tm, tc), lambda j, i: (0, i, j)),
                  pl.BlockSpec((4, HALO, tc), lambda j, i: (0, jnp.maximum(i * steps - 1, 0), j)),
                  pl.BlockSpec((3, tc), lambda j, i: (0, j)),
                  pl.BlockSpec((1, tc), lambda j, i: (0, j))],
        out_specs=pl.BlockSpec((tm, tc), lambda j, i: (i, j)),
        out_shape=jax.ShapeDtypeStruct((t, D_MODEL), BF16),
        compiler_params=_params("parallel", "parallel"),
    )(p_b, p_b, conv_w, conv_b)


def _mixer_b_backward(p_b, dyab, conv_w, conv_b, tm, tc):
    t = p_b.shape[1]
    n = t // tm

    def body(p_ref, halo_ref, dy_ref, w_ref, b_ref, dp_ref, dw_ref, db_ref, next_ref):
        ii = pl.program_id(1)

        @pl.when(ii == 0)
        def _():
            dw_ref[...] = jnp.zeros_like(dw_ref)
            db_ref[...] = jnp.zeros_like(db_ref)
            next_ref[...] = jnp.zeros_like(next_ref)

        valid = (ii < n - 1).astype(F32)
        xb = p_ref[0].astype(F32)
        cb = p_ref[1].astype(F32)
        bb = p_ref[2].astype(F32)
        z = p_ref[3].astype(F32)
        h = cb * xb
        halo_h = halo_ref[1].astype(F32) * halo_ref[0].astype(F32) * valid
        h1, h2 = _conv_taps(h, halo_h, tm)
        w = w_ref[...]
        w0, w1, w2 = w[0:1, :], w[1:2, :], w[2:3, :]
        conv = b_ref[...] + w0 * h2 + w1 * h1 + w2 * h
        sig = _sigmoid(z)
        sz = z * sig
        dy = dy_ref[...].astype(F32)
        dconv = dy * bb * sz
        dp_ref[2] = (dy * conv * sz).astype(BF16)
        dp_ref[3] = (dy * bb * conv * (sig * (1.0 + z * (1.0 - sig)))).astype(BF16)
        db_ref[...] += jnp.sum(dconv, axis=0, keepdims=True)
        dw_ref[0:1, :] += jnp.sum(dconv * h2, axis=0, keepdims=True)
        dw_ref[1:2, :] += jnp.sum(dconv * h1, axis=0, keepdims=True)
        dw_ref[2:3, :] += jnp.sum(dconv * h, axis=0, keepdims=True)
        row = lax.broadcasted_iota(jnp.int32, h.shape, 0)
        nxt = next_ref[...]
        n0, n1 = nxt[0:1, :], nxt[1:2, :]
        d1 = jnp.where(row == tm - 1, n0, pltpu.roll(dconv, tm - 1, 0))
        d2 = jnp.where(row == tm - 1, n1, jnp.where(row == tm - 2, n0, pltpu.roll(dconv, tm - 2, 0)))
        dh = w2 * dconv + w1 * d1 + w0 * d2
        dp_ref[0] = (dh * cb).astype(BF16)
        dp_ref[1] = (dh * xb).astype(BF16)
        next_ref[...] = dconv[0:8, :]

    steps = tm // HALO
    return pl.pallas_call(
        body, name="mixer_b_backward", grid=(D_MODEL // tc, n),
        in_specs=[pl.BlockSpec((4, tm, tc), lambda j, ii: (0, n - 1 - ii, j)),
                  pl.BlockSpec((4, HALO, tc), lambda j, ii: (0, jnp.maximum((n - 1 - ii) * steps - 1, 0), j)),
                  pl.BlockSpec((None, tm, tc), lambda j, ii: (1, n - 1 - ii, j)),
                  pl.BlockSpec((3, tc), lambda j, ii: (0, j)),
                  pl.BlockSpec((1, tc), lambda j, ii: (0, j))],
        out_specs=(pl.BlockSpec((4, tm, tc), lambda j, ii: (0, n - 1 - ii, j)),
                   pl.BlockSpec((3, tc), lambda j, ii: (0, j)),
                   pl.BlockSpec((1, tc), lambda j, ii: (0, j))),
        out_shape=(jax.ShapeDtypeStruct((4, t, D_MODEL), BF16),
                   jax.ShapeDtypeStruct((3, D_MODEL), F32), jax.ShapeDtypeStruct((1, D_MODEL), F32)),
        scratch_shapes=[pltpu.VMEM((8, tc), F32)],
        compiler_params=_params("parallel", "arbitrary"),
    )(p_b, p_b, dyab, conv_w, conv_b)


def _adam_math(w, g, m, v):
    m = ADAM_B1 * m + (1.0 - ADAM_B1) * g
    v = ADAM_B2 * v + (1.0 - ADAM_B2) * (g * g)
    delta = -ADAM_LR * ((m * ADAM_C1) / (jnp.sqrt(v * ADAM_C2) + ADAM_EPS) + ADAM_WD * w)
    return delta, m, v


def _adam_rows(w, g, m, v, tm, name):
    r, c = w.shape

    def body(w_ref, g_ref, m_ref, v_ref, go_ref, d_ref, mo_ref, vo_ref):
        g = g_ref[...]
        d, mn, vn = _adam_math(w_ref[...], g, m_ref[...], v_ref[...])
        go_ref[...] = g
        d_ref[...] = d
        mo_ref[...] = mn
        vo_ref[...] = vn

    spec = pl.BlockSpec((tm, c), lambda i: (i, 0))
    shape = jax.ShapeDtypeStruct((r, c), F32)
    return pl.pallas_call(
        body, name=name, grid=(r // tm,),
        in_specs=[spec] * 4, out_specs=(spec,) * 4, out_shape=(shape,) * 4,
        compiler_params=_params("parallel"),
    )(w, g, m, v)


def _adam_small(gathered, w, m, v):
    r = w.shape[0]

    def body(ga_ref, w_ref, m_ref, v_ref, g_ref, d_ref, mo_ref, vo_ref):
        g = ga_ref[0]
        for k in range(1, 8):
            g = g + ga_ref[k]
        d, mn, vn = _adam_math(w_ref[...], g, m_ref[...], v_ref[...])
        g_ref[...] = g
        d_ref[...] = d
        mo_ref[...] = mn
        vo_ref[...] = vn

    shape = jax.ShapeDtypeStruct((r, 128), F32)
    return pl.pallas_call(
        body, name="adam_small", out_shape=(shape,) * 4, compiler_params=_params(),
    )(gathered, w, m, v)


def _pack_small(parts):
    rows = []
    for name, r in SMALL_ROWS:
        a = parts[name].astype(F32).reshape(-1)
        pad = r * 128 - a.shape[0]
        if pad:
            a = jnp.concatenate([a, jnp.zeros((pad,), F32)])
        rows.append(a.reshape(r, 128))
    return jnp.concatenate(rows, axis=0)


def _unpack_small(buf, shapes):
    out, r0 = {}, 0
    for name, r in SMALL_ROWS:
        shp = shapes[name]
        n = math.prod(shp)
        out[name] = buf[r0:r0 + r].reshape(-1)[:n].reshape(shp)
        r0 += r
    return out


def kernel(x, w_in, b_gate, ln_v_g, ln_v_b, w_s, b_s, conv_w, conv_b, w_oa, w_ob, w_out, ln_g, ln_b, loss_target, m_w_in, m_b_gate, m_ln_v_g, m_ln_v_b, m_w_s, m_b_s, m_conv_w, m_conv_b, m_w_oa, m_w_ob, m_w_out, m_ln_g, m_ln_b, v_w_in, v_b_gate, v_ln_v_g, v_ln_v_b, v_w_s, v_b_s, v_conv_w, v_conv_b, v_w_oa, v_w_ob, v_w_out, v_ln_g, v_ln_b):
    t = x.shape[1]
    x2 = x[0]
    target = loss_target[0]
    chip = 2 * lax.axis_index("x") + lax.axis_index("y")
    conv_cols = conv_w.shape[2]

    w_pre = _cast_into_slot(w_in[0], chip.astype(jnp.int32).reshape(1), N_CHIPS, 256, "cast_w_in")
    wo_b, wm = _prep_small_weights(w_oa[0], w_ob[0], w_out[0], w_s[0])
    conv_w8 = jnp.concatenate([conv_w[0], jnp.zeros((5, conv_cols), F32)], axis=0)
    w_full, wo_full, cw_full = _all_gather_weights(w_pre, wo_b, conv_w8)
    wo3 = wo_full.reshape(3, D_MODEL, D_MODEL)
    conv_w_all = jnp.transpose(cw_full[:, :3, :], (1, 0, 2)).reshape(3, D_MODEL)
    bs_col = b_s[0].reshape(N_HEADS, CHUNK, 1)
    bg = b_gate.reshape(2, 1, D_MODEL)

    xb = _cast_rows(x2, min(512, t), "cast_x")
    tm_p = min(1024, t)
    p_a = _project_in(xb, w_full, 0, 3, tm_p, "project_in_a")
    p_b = _project_in(xb, w_full, 3 * BLOCKS_PER_SEG, 4, tm_p, "project_in_b")
    p_g = _project_in(xb, w_full, 7 * BLOCKS_PER_SEG, 2, tm_p, "project_in_g")
    tm_a = min(512, t)
    ya = _mixer_a_forward(p_a, wm, bs_col, ln_v_g, ln_v_b, tm_a)
    tm_b = min(256, t)
    yb = _mixer_b_forward(p_b, conv_w_all, conv_b, tm_b, 256)
    tm_m = min(512, t)
    merged, oab = _merge_forward(ya, yb, wo3, p_g, bg, tm_m, 512)

    drb, gx, dmerged, d_ln_g, d_ln_b, loss_part = _head(merged, wo3, x2, target, ln_g, ln_b, min(256, t))
    doab, dp_g, d_bg = _gate_backward(dmerged, oab, p_g, bg, tm_m, 512)
    dyab = _branch_backward(doab, wo3, tm_m, 512)
    dp_a, d_ws, d_bs, d_lnv_g, d_lnv_b = _mixer_a_backward(p_a, dyab, wm, bs_col, ln_v_g, ln_v_b, tm_a)
    dp_b, d_cw, d_cb = _mixer_b_backward(p_b, dyab, conv_w_all, conv_b, tm_b, 256)

    tk = min(512, t)
    g_oa = _weight_grad(ya, 0, doab, 0, 1024, 1024, tk, "grad_w_oa")
    g_ob = _weight_grad(yb, 0, doab, 1, 1024, 1024, tk, "grad_w_ob")
    g_out = _weight_grad(merged, 0, drb, 0, 1024, 1024, tk, "grad_w_out")
    g_in = _win_grad(xb, dp_a, 0, None, 1024, tk, "grad_w_in_a")
    g_in = _win_grad(xb, dp_b, 3, g_in, 1024, tk, "grad_w_in_b")
    g_in = _win_grad(xb, dp_g, 7, g_in, 1024, tk, "grad_w_in_g")
    tm_x = min(1024, t)
    gx = _input_grad(dp_a, w_full, 0, gx, tm_x, "grad_x_a")
    gx = _input_grad(dp_b, w_full, 3 * BLOCKS_PER_SEG, gx, tm_x, "grad_x_b")
    grad_x = _input_grad(dp_g, w_full, 7 * BLOCKS_PER_SEG, gx, tm_x, "grad_x_g")

    r_in, *r_o = _exchange_halves(g_in, (g_oa, g_ob, g_out))
    s_in, s_o = _add_halves(g_in, (g_oa, g_ob, g_out), r_in, r_o)
    q_in, q_o = _scatter_to_chips(s_in, s_o)
    f_in, f_o = _sum_chips(q_in, q_o, s_in, s_o)
    gsum_in, gsum_o = _share_halves(f_in, f_o)

    big = {}
    big["w_in"] = _adam_rows(w_in[0], gsum_in, m_w_in[0], v_w_in[0], 128, "adam_w_in")
    for n, (name, w, m, v) in enumerate((("w_oa", w_oa, m_w_oa, v_w_oa), ("w_ob", w_ob, m_w_ob, v_w_ob),
                                         ("w_out", w_out, m_w_out, v_w_out))):
        big[name] = _adam_rows(w[0], gsum_o[n], m[0], v[0], 256, "adam_" + name)

    small_grads = {"ln_g": d_ln_g, "ln_b": d_ln_b, "b_gate": d_bg, "ln_v_g": d_lnv_g, "ln_v_b": d_lnv_b,
                   "w_s": d_ws, "b_s": d_bs, "conv_w": d_cw, "conv_b": d_cb, "loss": loss_part[:, :1]}
    gathered = _gather_small(_pack_small(small_grads))

    def placed(a):
        return lax.dynamic_update_slice(jnp.zeros((3, D_MODEL), F32), a[0], (0, chip * conv_cols))

    def packed(pre, cw):
        z1 = jnp.zeros((1,), F32)
        return _pack_small({"ln_g": pre["ln_g"], "ln_b": pre["ln_b"], "b_gate": pre["b_gate"],
                            "ln_v_g": pre["ln_v_g"], "ln_v_b": pre["ln_v_b"], "w_s": pre["w_s"], "b_s": pre["b_s"],
                            "conv_w": placed(cw), "conv_b": pre["conv_b"], "loss": z1})

    ws = dict(ln_g=ln_g, ln_b=ln_b, b_gate=b_gate, ln_v_g=ln_v_g, ln_v_b=ln_v_b, w_s=w_s, b_s=b_s, conv_b=conv_b)
    ms = dict(ln_g=m_ln_g, ln_b=m_ln_b, b_gate=m_b_gate, ln_v_g=m_ln_v_g, ln_v_b=m_ln_v_b, w_s=m_w_s, b_s=m_b_s,
              conv_b=m_conv_b)
    vs = dict(ln_g=v_ln_g, ln_b=v_ln_b, b_gate=v_b_gate, ln_v_g=v_ln_v_g, ln_v_b=v_ln_v_b, w_s=v_w_s, b_s=v_b_s,
              conv_b=v_conv_b)
    s_g, s_d, s_m, s_v = _adam_small(gathered, packed(ws, conv_w), packed(ms, m_conv_w), packed(vs, v_conv_w))
    shapes = {"ln_g": ln_g.shape, "ln_b": ln_b.shape, "b_gate": b_gate.shape, "ln_v_g": ln_v_g.shape,
              "ln_v_b": ln_v_b.shape, "w_s": w_s.shape, "b_s": b_s.shape, "conv_w": (1, 3, D_MODEL),
              "conv_b": conv_b.shape, "loss": (1,)}
    small = [_unpack_small(b, shapes) for b in (s_g, s_d, s_m, s_v)]
    for d in small:
        d["conv_w"] = lax.dynamic_slice(d["conv_w"], (0, 0, chip * conv_cols), (1, 3, conv_cols))
    loss = small[0]["loss"][0]

    order = ("w_in", "b_gate", "ln_v_g", "ln_v_b", "w_s", "b_s", "conv_w", "conv_b", "w_oa", "w_ob", "w_out",
             "ln_g", "ln_b")
    outs = [loss, grad_x[None]]
    for which in range(4):
        for name in order:
            outs.append(big[name][which][None] if name in big else small[which][name])
    return tuple(outs)
```

```python
import functools
import math

import jax
import jax.numpy as jnp
from jax import lax
from jax.experimental import pallas as pl
from jax.experimental.pallas import tpu as pltpu

F32 = jnp.float32
BF16 = jnp.bfloat16

D_MODEL = 2048
N_HEADS = 8
HEAD_DIM = D_MODEL // N_HEADS
CHUNK = 128
N_SEG = 9
N_CHIPS = 4
SHARD_COLS = N_SEG * D_MODEL // N_CHIPS
COL_BLOCK = 512
BLOCKS_PER_SHARD = SHARD_COLS // COL_BLOCK
BLOCKS_PER_SEG = D_MODEL // COL_BLOCK
SHARD_ROWS = D_MODEL // N_CHIPS
N_SLOTS = 12
BLOCK_A, BLOCK_G, BLOCK_B = 0, 2, 2
DN_ALPHA = 2.0 ** 0.25
LN_EPS = 1e-5
GELU_K = math.sqrt(2.0 / math.pi)
GELU_C = 0.044715

ADAM_LR = 0.001
ADAM_B1 = 0.9
ADAM_B2 = 0.999
ADAM_EPS = 1e-08
ADAM_WD = 0.01
ADAM_STEP = 10
ADAM_C1 = 1.0 / (1.0 - ADAM_B1 ** ADAM_STEP)
ADAM_C2 = 1.0 / (1.0 - ADAM_B2 ** ADAM_STEP)

VMEM_LIMIT = 56 * 1024 * 1024
MESH = pl.DeviceIdType.MESH
ANY = pl.BlockSpec(memory_space=pl.ANY)

SMALL_ROWS = (("ln_g", 16), ("ln_b", 16), ("b_gate", 32), ("ln_v_g", 16), ("ln_v_b", 16),
              ("w_s", 1024), ("b_s", 8), ("conv_w", 48), ("conv_b", 16), ("loss", 8))
SMALL_TOTAL = sum(r for _, r in SMALL_ROWS)


def _params(*sem):
    return pltpu.CompilerParams(dimension_semantics=sem, vmem_limit_bytes=VMEM_LIMIT)


def _sigmoid(x):
    return 1.0 / (1.0 + jnp.exp(-x))


def _gelu_and_grad(x):
    x2 = x * x
    th = jnp.tanh(GELU_K * (x + GELU_C * x * x2))
    g = 0.5 * x * (1.0 + th)
    dg = 0.5 * (1.0 + th) + 0.5 * x * (1.0 - th * th) * (GELU_K * (1.0 + 3.0 * GELU_C * x2))
    return g, dg


def _gelu(x):
    return 0.5 * x * (1.0 + jnp.tanh(GELU_K * (x + GELU_C * x * x * x)))


def _dot(a, b, ca=1, cb=0):
    return lax.dot_general(a, b, (((ca,), (cb,)), ((), ())), preferred_element_type=F32)


def _cast_rows(x, tm, name):
    r, c = x.shape

    def body(x_ref, o_ref):
        o_ref[...] = x_ref[...].astype(BF16)

    return pl.pallas_call(
        body, name=name, grid=(r // tm,),
        in_specs=[pl.BlockSpec((tm, c), lambda i: (i, 0))],
        out_specs=pl.BlockSpec((tm, c), lambda i: (i, 0)),
        out_shape=jax.ShapeDtypeStruct((r, c), BF16),
        compiler_params=_params("parallel"),
    )(x)


def _cast_into_slot(w, slot, n_slots, tm, name):
    r, c = w.shape

    def body(s_ref, w_ref, o_ref):
        o_ref[...] = w_ref[...].astype(BF16)

    return pl.pallas_call(
        body, name=name,
        grid_spec=pltpu.PrefetchScalarGridSpec(
            num_scalar_prefetch=1, grid=(r // tm,),
            in_specs=[pl.BlockSpec((tm, c), lambda i, s_ref: (i, 0))],
            out_specs=pl.BlockSpec((None, tm, c), lambda i, s_ref: (s_ref[0], i, 0))),
        out_shape=jax.ShapeDtypeStruct((n_slots, r, c), BF16),
        compiler_params=_params("parallel"),
    )(slot, w)


def _prep_small_weights(w_oa, w_ob, w_out, w_s):
    rows = w_oa.shape[0]

    def body(a_ref, b_ref, c_ref, ws_ref, wo_ref, wm_ref):
        wo_ref[0] = a_ref[...].astype(BF16)
        wo_ref[1] = b_ref[...].astype(BF16)
        wo_ref[2] = c_ref[...].astype(BF16)
        t = lax.broadcasted_iota(jnp.int32, (CHUNK, CHUNK), 0)
        s = lax.broadcasted_iota(jnp.int32, (CHUNK, CHUNK), 1)
        for h in range(N_HEADS):
            wm_ref[h] = jnp.where(s <= t, ws_ref[h], 0.0).astype(BF16)

    return pl.pallas_call(
        body, name="prep_small_weights",
        out_shape=(jax.ShapeDtypeStruct((3, rows, D_MODEL), BF16),
                   jax.ShapeDtypeStruct((N_HEADS, CHUNK, CHUNK), BF16)),
        compiler_params=_params(),
    )(w_oa, w_ob, w_out, w_s)


def _mesh_pos():
    x, y, c = lax.axis_index("x"), lax.axis_index("y"), lax.axis_index("c")
    chips = [(1 - x, y), (x, 1 - y), (1 - x, 1 - y)]
    return x, y, c, chips


def _slot_of_seg(seg):
    return jnp.where(seg < 3, seg, jnp.where(seg < 7, seg + 5, seg - 3))


def _gather_and_project(xb, w_pre, wo_b, conv_w8, tm):
    t = xb.shape[0]
    _, d, sc = w_pre.shape
    rows = wo_b.shape[1]
    hd, hr = d // 2, rows // 2
    ni, nj = t // tm, BLOCKS_PER_SHARD
    total = N_CHIPS * ni * nj
    mx, my = lax.axis_index("x"), lax.axis_index("y")
    order = jnp.stack([2 * mx + my, 2 * (1 - mx) + my, 2 * mx + (1 - my),
                       2 * (1 - mx) + (1 - my)]).astype(jnp.int32)

    def body(order_ref, x_ref, wpre_ref, wo_ref, cw_ref, p_ref, wf_ref, wof_ref, cwf_ref,
             wbuf, wsem, send_sems, recv_sems, local_sems):
        x, y, c, chips = _mesh_pos()
        me = 2 * x + y
        sibling = (x, y, 1 - c)
        s, i, j = pl.program_id(0), pl.program_id(1), pl.program_id(2)
        n = (s * ni + i) * nj + j

        def half(kind, k, cc):
            if kind == 0:
                return wf_ref.at[k, pl.ds(cc * hd, hd), :]
            return wof_ref.at[:, k, pl.ds(cc * hr, hr), :]

        def src_half(kind, cc):
            if kind == 0:
                return wpre_ref.at[me, pl.ds(cc * hd, hd), :]
            return wo_ref.at[:, pl.ds(cc * hr, hr), :]

        def rcopy(sem, src, dst, to):
            return pltpu.make_async_remote_copy(src_ref=src, dst_ref=dst, send_sem=send_sems.at[sem],
                                                recv_sem=recv_sems.at[sem], device_id=to, device_id_type=MESH)

        def ici_send(kind, q):
            return rcopy(kind * 6 + q, src_half(kind, c), half(kind, me, c), (*chips[q], c))

        def ici_landed(kind, q):
            k = 2 * chips[q][0] + chips[q][1]
            return rcopy(kind * 6 + q, src_half(kind, c), half(kind, k, c), sibling)

        def forward(kind, q, cc):
            k = 2 * chips[q][0] + chips[q][1]
            return rcopy(kind * 6 + 3 + q, half(kind, k, cc), half(kind, k, cc), sibling)

        def conv_send(q):
            return rcopy(12 + q, cw_ref, cwf_ref.at[me], (*chips[q], c))

        def local_copies():
            return [pltpu.make_async_copy(wo_ref, wof_ref.at[:, me], local_sems.at[0]),
                    pltpu.make_async_copy(cw_ref, cwf_ref.at[me], local_sems.at[1])]

        def tile_start(s_, j_, slot):
            cols = pl.ds(pl.multiple_of(j_ * COL_BLOCK, COL_BLOCK), COL_BLOCK)

            @pl.when(s_ == 0)
            def _():
                pltpu.make_async_copy(wpre_ref.at[me, :, cols], wbuf.at[slot], wsem.at[slot]).start()

            @pl.when(s_ > 0)
            def _():
                pltpu.make_async_copy(wf_ref.at[order_ref[s_], :, cols], wbuf.at[slot], wsem.at[slot]).start()

        def at(s_, i_, j_):
            return jnp.logical_and(s == s_, jnp.logical_and(i == i_, j == j_))

        @pl.when(n == 0)
        def _():
            for cp in local_copies():
                cp.start()
            for kind in (0, 1):
                for q in range(3):
                    ici_send(kind, q).start()
            for q in range(3):
                conv_send(q).start()
            tile_start(0, 0, 0)

        @pl.when(at(0, ni - 1, nj - 1))
        def _():
            for q in (0, 1):
                ici_landed(0, q).wait_recv()
                forward(0, q, c).start()
            forward(0, 0, 1 - c).wait_recv()

        @pl.when(at(1, ni - 1, nj - 1))
        def _():
            forward(0, 1, 1 - c).wait_recv()

        @pl.when(at(2, (3 * ni) // 4, 0))
        def _():
            ici_landed(0, 2).wait_recv()
            forward(0, 2, c).start()

        @pl.when(at(2, ni - 1, nj - 1))
        def _():
            forward(0, 2, 1 - c).wait_recv()

        n1 = n + 1

        @pl.when(n1 < total)
        def _():
            tile_start(n1 // (ni * nj), lax.rem(n1, nj), lax.rem(n1, 2))

        slot = lax.rem(n, 2)
        pltpu.make_async_copy(wpre_ref.at[0, :, pl.ds(0, COL_BLOCK)], wbuf.at[slot], wsem.at[slot]).wait()
        p_ref[...] = _dot(x_ref[...], wbuf[slot]).astype(BF16)

        @pl.when(n == total - 1)
        def _():
            for q in range(3):
                ici_landed(1, q).wait_recv()
                forward(1, q, c).start()
            for q in range(3):
                forward(1, q, 1 - c).wait_recv()
                rcopy(12 + q, cw_ref, cwf_ref.at[2 * chips[q][0] + chips[q][1]], sibling).wait_recv()
            for kind in (0, 1):
                for q in range(3):
                    ici_send(kind, q).wait_send()
                    forward(kind, q, c).wait_send()
            for q in range(3):
                conv_send(q).wait_send()
            for cp in local_copies():
                cp.wait()

    def p_map(s, i, j, o):
        g = o[s] * BLOCKS_PER_SHARD + j
        return (_slot_of_seg(g // BLOCKS_PER_SEG), i, lax.rem(g, BLOCKS_PER_SEG))

    return pl.pallas_call(
        body, name="gather_and_project",
        grid_spec=pltpu.PrefetchScalarGridSpec(
            num_scalar_prefetch=1, grid=(N_CHIPS, ni, nj),
            in_specs=[pl.BlockSpec((tm, D_MODEL), lambda s, i, j, o: (i, 0)), ANY, ANY, ANY],
            out_specs=(pl.BlockSpec((None, tm, COL_BLOCK), p_map), ANY, ANY, ANY),
            scratch_shapes=[pltpu.VMEM((2, d, COL_BLOCK), BF16), pltpu.SemaphoreType.DMA((2,)),
                            pltpu.SemaphoreType.DMA((15,)), pltpu.SemaphoreType.DMA((15,)),
                            pltpu.SemaphoreType.DMA((2,))]),
        out_shape=(jax.ShapeDtypeStruct((N_SLOTS, t, D_MODEL), BF16),
                   jax.ShapeDtypeStruct((N_CHIPS, d, sc), BF16),
                   jax.ShapeDtypeStruct((3, N_CHIPS, rows, D_MODEL), BF16),
                   jax.ShapeDtypeStruct((N_CHIPS,) + conv_w8.shape, F32)),
        input_output_aliases={2: 1},
        compiler_params=pltpu.CompilerParams(dimension_semantics=("arbitrary",) * 3, vmem_limit_bytes=VMEM_LIMIT,
                                             has_side_effects=True),
    )(order, xb, w_pre, wo_b, conv_w8)


def _gather_small(small):
    def body(s_ref, o_ref, send_sems, recv_sems, local_sem):
        x, y, c, _ = _mesh_pos()
        me = 4 * x + 2 * y + c
        mine = pltpu.make_async_copy(s_ref, o_ref.at[me], local_sem)
        mine.start()
        sends = []
        for k in range(1, 8):
            dx, dy, dc = (k >> 2) & 1, (k >> 1) & 1, k & 1
            to = (x ^ dx, y ^ dy, c ^ dc)
            cp = pltpu.make_async_remote_copy(src_ref=s_ref, dst_ref=o_ref.at[me], send_sem=send_sems.at[k - 1],
                                              recv_sem=recv_sems.at[k - 1], device_id=to, device_id_type=MESH)
            cp.start()
            sends.append(cp)
        for k in range(1, 8):
            dx, dy, dc = (k >> 2) & 1, (k >> 1) & 1, k & 1
            frm = 4 * (x ^ dx) + 2 * (y ^ dy) + (c ^ dc)
            pltpu.make_async_remote_copy(src_ref=s_ref, dst_ref=o_ref.at[frm], send_sem=send_sems.at[k - 1],
                                         recv_sem=recv_sems.at[k - 1], device_id=(x, y, c),
                                         device_id_type=MESH).wait_recv()
        for cp in sends:
            cp.wait_send()
        mine.wait()

    return pl.pallas_call(
        body, name="gather_small",
        in_specs=[ANY], out_specs=ANY,
        out_shape=jax.ShapeDtypeStruct((8,) + small.shape, small.dtype),
        scratch_shapes=[pltpu.SemaphoreType.DMA((7,)), pltpu.SemaphoreType.DMA((7,)), pltpu.SemaphoreType.DMA],
        compiler_params=pltpu.CompilerParams(has_side_effects=True),
    )(small)


def _exchange_halves(g_in, g_o):
    d, c9 = g_in.shape
    hd = d // 2
    hr = SHARD_ROWS // 2
    g_o4 = [g.reshape(N_CHIPS, 2, hr, D_MODEL) for g in g_o]

    def body(gi_ref, ga_ref, gb_ref, gc_ref, ri_ref, ra_ref, rb_ref, rc_ref, send_sems, recv_sems):
        x, y, c, _ = _mesh_pos()
        sibling = (x, y, 1 - c)
        cps = [pltpu.make_async_remote_copy(src_ref=gi_ref.at[pl.ds((1 - c) * hd, hd), :], dst_ref=ri_ref,
                                            send_sem=send_sems.at[0], recv_sem=recv_sems.at[0],
                                            device_id=sibling, device_id_type=MESH)]
        for n, (g_ref, r_ref) in enumerate(((ga_ref, ra_ref), (gb_ref, rb_ref), (gc_ref, rc_ref))):
            cps.append(pltpu.make_async_remote_copy(src_ref=g_ref.at[:, 1 - c], dst_ref=r_ref,
                                                    send_sem=send_sems.at[1 + n], recv_sem=recv_sems.at[1 + n],
                                                    device_id=sibling, device_id_type=MESH))
        for cp in cps:
            cp.start()
        for cp in cps:
            cp.wait()

    o_shape = jax.ShapeDtypeStruct((N_CHIPS, hr, D_MODEL), BF16)
    return pl.pallas_call(
        body, name="rs_exchange_halves",
        in_specs=[ANY] * 4, out_specs=(ANY,) * 4,
        out_shape=(jax.ShapeDtypeStruct((hd, c9), BF16), o_shape, o_shape, o_shape),
        scratch_shapes=[pltpu.SemaphoreType.DMA((4,)), pltpu.SemaphoreType.DMA((4,))],
        compiler_params=pltpu.CompilerParams(has_side_effects=True),
    )(g_in, *g_o4)


def _add_halves(g_in, g_o, r_in, r_o):
    d, c9 = g_in.shape
    hd = d // 2
    hr = SHARD_ROWS // 2
    core = lax.axis_index("c").astype(jnp.int32).reshape(1)
    tm = min(512, hd)
    nb = hd // tm

    def body_in(c_ref, g_ref, r_ref, o_ref):
        o_ref[...] = (g_ref[...].astype(F32) + r_ref[...].astype(F32)).astype(BF16)

    s_in = pl.pallas_call(
        body_in, name="rs_add_halves_in",
        grid_spec=pltpu.PrefetchScalarGridSpec(
            num_scalar_prefetch=1, grid=(N_CHIPS, nb),
            in_specs=[pl.BlockSpec((tm, SHARD_COLS), lambda k, i, c_ref: (c_ref[0] * nb + i, k)),
                      pl.BlockSpec((tm, SHARD_COLS), lambda k, i, c_ref: (i, k))],
            out_specs=pl.BlockSpec((None, tm, SHARD_COLS), lambda k, i, c_ref: (k, i, 0))),
        out_shape=jax.ShapeDtypeStruct((N_CHIPS, hd, SHARD_COLS), BF16),
        compiler_params=_params("parallel", "parallel"),
    )(core, g_in, r_in)

    g_o4 = [g.reshape(N_CHIPS, 2, hr, D_MODEL) for g in g_o]

    def body_o(c_ref, ga_ref, gb_ref, gc_ref, ra_ref, rb_ref, rc_ref, o_ref):
        for n, (g_ref, r_ref) in enumerate(((ga_ref, ra_ref), (gb_ref, rb_ref), (gc_ref, rc_ref))):
            o_ref[n] = (g_ref[...].astype(F32) + r_ref[...].astype(F32)).astype(BF16)

    gspec = pl.BlockSpec((None, None, hr, D_MODEL), lambda k, c_ref: (k, c_ref[0], 0, 0))
    rspec = pl.BlockSpec((None, hr, D_MODEL), lambda k, c_ref: (k, 0, 0))
    s_o = pl.pallas_call(
        body_o, name="rs_add_halves_o",
        grid_spec=pltpu.PrefetchScalarGridSpec(
            num_scalar_prefetch=1, grid=(N_CHIPS,),
            in_specs=[gspec] * 3 + [rspec] * 3,
            out_specs=pl.BlockSpec((3, None, hr, D_MODEL), lambda k, c_ref: (0, k, 0, 0))),
        out_shape=jax.ShapeDtypeStruct((3, N_CHIPS, hr, D_MODEL), BF16),
        compiler_params=_params("parallel"),
    )(core, *g_o4, *r_o)
    return s_in, s_o


def _sum_chips(r_in, r_o, s_in, s_o):
    _, hd, sc = r_in.shape
    hr = r_o.shape[2]
    tm = min(256, hd)
    nb = hd // tm
    pos = jnp.stack([2 * lax.axis_index("x") + lax.axis_index("y"), lax.axis_index("c")]).astype(jnp.int32)

    def chip_sum(pos_ref, r_ref, s_ref):
        acc = None
        for k in range(N_CHIPS):
            term = jnp.where(pos_ref[0] == k, s_ref[...], r_ref[k]).astype(F32)
            acc = term if acc is None else acc + term
        return acc

    def body_in(pos_ref, r_ref, s_ref, o_ref):
        o_ref[...] = chip_sum(pos_ref, r_ref, s_ref)

    f_in = pl.pallas_call(
        body_in, name="rs_sum_chips_in",
        grid_spec=pltpu.PrefetchScalarGridSpec(
            num_scalar_prefetch=1, grid=(nb,),
            in_specs=[pl.BlockSpec((N_CHIPS, tm, sc), lambda i, p: (0, i, 0)),
                      pl.BlockSpec((None, tm, sc), lambda i, p: (p[0], i, 0))],
            out_specs=pl.BlockSpec((tm, sc), lambda i, p: (p[1] * nb + i, 0))),
        out_shape=jax.ShapeDtypeStruct((2 * hd, sc), F32),
        compiler_params=_params("parallel"),
    )(pos, r_in, s_in)

    def body_o(pos_ref, r_ref, s_ref, o_ref):
        o_ref[...] = chip_sum(pos_ref, r_ref, s_ref)

    f_o = pl.pallas_call(
        body_o, name="rs_sum_chips_o",
        grid_spec=pltpu.PrefetchScalarGridSpec(
            num_scalar_prefetch=1, grid=(3,),
            in_specs=[pl.BlockSpec((N_CHIPS, None, hr, D_MODEL), lambda n, p: (0, n, 0, 0)),
                      pl.BlockSpec((None, None, hr, D_MODEL), lambda n, p: (n, p[0], 0, 0))],
            out_specs=pl.BlockSpec((None, hr, D_MODEL), lambda n, p: (n, p[1], 0))),
        out_shape=jax.ShapeDtypeStruct((3, 2 * hr, D_MODEL), F32),
        compiler_params=_params("parallel"),
    )(pos, r_o, s_o)
    return f_in, f_o


def _share_halves(f_in, f_o):
    hd, sc = f_in.shape[0] // 2, f_in.shape[1]
    hr = f_o.shape[1] // 2

    def body(fi_ref, fo_ref, gi_ref, go_ref, send_sems, recv_sems):
        x, y, c, _ = _mesh_pos()
        sibling = (x, y, 1 - c)

        def halves(cc):
            rows_i, rows_o = pl.ds(cc * hd, hd), pl.ds(cc * hr, hr)
            return (fi_ref.at[rows_i, :], gi_ref.at[rows_i, :]), (fo_ref.at[:, rows_o, :], go_ref.at[:, rows_o, :])

        def copies(cc):
            return [pltpu.make_async_remote_copy(src_ref=src, dst_ref=dst, send_sem=send_sems.at[n],
                                                 recv_sem=recv_sems.at[n], device_id=sibling, device_id_type=MESH)
                    for n, (src, dst) in enumerate(halves(cc))]

        sends = copies(c)
        for cp in sends:
            cp.start()
        for cp in copies(1 - c):
            cp.wait_recv()
        for cp in sends:
            cp.wait_send()

    return pl.pallas_call(
        body, name="rs_share_halves",
        in_specs=[ANY, ANY], out_specs=(ANY, ANY),
        out_shape=(jax.ShapeDtypeStruct(f_in.shape, F32), jax.ShapeDtypeStruct(f_o.shape, F32)),
        scratch_shapes=[pltpu.SemaphoreType.DMA((2,)), pltpu.SemaphoreType.DMA((2,))],
        input_output_aliases={0: 0, 1: 1},
        compiler_params=pltpu.CompilerParams(has_side_effects=True),
    )(f_in, f_o)


def _merge_forward(ya, yb, wo3, p, bg, tm, tn):
    t = ya.shape[0]

    def body(ya_ref, yb_ref, wa_ref, wb_ref, g_ref, bg_ref, m_ref, oab_ref):
        oa = _dot(ya_ref[...], wa_ref[...])
        ob = _dot(yb_ref[...], wb_ref[...])
        ga = _sigmoid(g_ref[0].astype(F32) + bg_ref[0])
        gb = _sigmoid(g_ref[1].astype(F32) + bg_ref[1])
        m_ref[...] = (ga * oa + gb * ob).astype(BF16)
        oab_ref[0] = oa.astype(BF16)
        oab_ref[1] = ob.astype(BF16)

    return pl.pallas_call(
        body, name="merge_forward", grid=(t // tm, D_MODEL // tn),
        in_specs=[pl.BlockSpec((tm, D_MODEL), lambda i, j: (i, 0)),
                  pl.BlockSpec((tm, D_MODEL), lambda i, j: (i, 0)),
                  pl.BlockSpec((None, D_MODEL, tn), lambda i, j: (0, 0, j)),
                  pl.BlockSpec((None, D_MODEL, tn), lambda i, j: (1, 0, j)),
                  pl.BlockSpec((2, tm, tn), lambda i, j: (BLOCK_G, i, j)),
                  pl.BlockSpec((2, 1, tn), lambda i, j: (0, 0, j))],
        out_specs=(pl.BlockSpec((tm, tn), lambda i, j: (i, j)),
                   pl.BlockSpec((2, tm, tn), lambda i, j: (0, i, j))),
        out_shape=(jax.ShapeDtypeStruct((t, D_MODEL), BF16), jax.ShapeDtypeStruct((2, t, D_MODEL), BF16)),
        compiler_params=_params("parallel", "parallel"),
    )(ya, yb, wo3, wo3, p, bg)


def _head(merged, wo3, x, target, ln_g, ln_b, tm):
    t = x.shape[0]
    inv_d = 1.0 / D_MODEL

    def body(m_ref, w_ref, x_ref, t_ref, g_ref, b_ref, dr_ref, gx_ref, dm_ref, dg_ref, db_ref, loss_ref):
        i = pl.program_id(0)

        @pl.when(i == 0)
        def _():
            dg_ref[...] = jnp.zeros_like(dg_ref)
            db_ref[...] = jnp.zeros_like(db_ref)
            loss_ref[...] = jnp.zeros_like(loss_ref)

        w = w_ref[...]
        out = _dot(m_ref[...], w)
        r = DN_ALPHA * x_ref[...] + out
        mu = jnp.mean(r, axis=-1, keepdims=True)
        xc = r - mu
        var = jnp.mean(xc * xc, axis=-1, keepdims=True)
        rstd = lax.rsqrt(var + LN_EPS)
        xhat = xc * rstd
        g = g_ref[...]
        e = xhat * g + b_ref[...] - t_ref[...]
        se = jnp.sum(jnp.sum(e * e, axis=1, keepdims=True), axis=0, keepdims=True)
        loss_ref[...] += jnp.broadcast_to((0.5 * inv_d) * se, loss_ref.shape)
        dy = e * inv_d
        db_ref[...] += jnp.sum(dy, axis=0, keepdims=True)
        dg_ref[...] += jnp.sum(dy * xhat, axis=0, keepdims=True)
        dxh = dy * g
        m1 = jnp.mean(dxh, axis=-1, keepdims=True)
        m2 = jnp.mean(dxh * xhat, axis=-1, keepdims=True)
        dr = rstd * (dxh - m1 - xhat * m2)
        gx_ref[...] = DN_ALPHA * dr
        drb = dr.astype(BF16)
        dr_ref[...] = drb
        dm_ref[...] = _dot(drb, w, 1, 1).astype(BF16)

    row = pl.BlockSpec((tm, D_MODEL), lambda i: (i, 0))
    vec = pl.BlockSpec((1, D_MODEL), lambda i: (0, 0))
    return pl.pallas_call(
        body, name="head", grid=(t // tm,),
        in_specs=[row, pl.BlockSpec((None, D_MODEL, D_MODEL), lambda i: (2, 0, 0)), row, row, vec, vec],
        out_specs=(row, row, row, vec, vec, pl.BlockSpec((1, 128), lambda i: (0, 0))),
        out_shape=(jax.ShapeDtypeStruct((t, D_MODEL), BF16), jax.ShapeDtypeStruct((t, D_MODEL), F32),
                   jax.ShapeDtypeStruct((t, D_MODEL), BF16), jax.ShapeDtypeStruct((1, D_MODEL), F32),
                   jax.ShapeDtypeStruct((1, D_MODEL), F32), jax.ShapeDtypeStruct((1, 128), F32)),
        compiler_params=_params("arbitrary"),
    )(merged, wo3, x, target, ln_g, ln_b)


def _gate_backward(dmerged, oab, p, bg, tm, tn):
    t = dmerged.shape[0]

    def body(dm_ref, oab_ref, g_ref, bg_ref, do_ref, dpg_ref, dbg_ref):
        @pl.when(pl.program_id(1) == 0)
        def _():
            dbg_ref[...] = jnp.zeros_like(dbg_ref)

        dm = dm_ref[...].astype(F32)
        for n in range(2):
            gate = _sigmoid(g_ref[n].astype(F32) + bg_ref[n])
            do_ref[n] = (dm * gate).astype(BF16)
            dgate = dm * oab_ref[n].astype(F32) * gate * (1.0 - gate)
            dpg_ref[n] = dgate.astype(BF16)
            dbg_ref[n] += jnp.sum(dgate, axis=0, keepdims=True)

    pair = pl.BlockSpec((2, tm, tn), lambda j, i: (0, i, j))
    gates = pl.BlockSpec((2, tm, tn), lambda j, i: (BLOCK_G, i, j))
    vec = pl.BlockSpec((2, 1, tn), lambda j, i: (0, 0, j))
    return pl.pallas_call(
        body, name="gate_backward", grid=(D_MODEL // tn, t // tm),
        in_specs=[pl.BlockSpec((tm, tn), lambda j, i: (i, j)), pair, gates, vec],
        out_specs=(pair, gates, vec),
        out_shape=(jax.ShapeDtypeStruct((2, t, D_MODEL), BF16), jax.ShapeDtypeStruct((N_SLOTS, t, D_MODEL), BF16),
                   jax.ShapeDtypeStruct((2, 1, D_MODEL), F32)),
        compiler_params=_params("parallel", "arbitrary"),
    )(dmerged, oab, p, bg)


def _branch_backward(doab, wo3, tm, tn):
    t = doab.shape[1]

    def body(d_ref, wa_ref, wb_ref, o_ref):
        o_ref[0] = _dot(d_ref[0], wa_ref[...], 1, 1).astype(BF16)
        o_ref[1] = _dot(d_ref[1], wb_ref[...], 1, 1).astype(BF16)

    return pl.pallas_call(
        body, name="branch_backward", grid=(t // tm, D_MODEL // tn),
        in_specs=[pl.BlockSpec((2, tm, D_MODEL), lambda i, j: (0, i, 0)),
                  pl.BlockSpec((None, tn, D_MODEL), lambda i, j: (0, j, 0)),
                  pl.BlockSpec((None, tn, D_MODEL), lambda i, j: (1, j, 0))],
        out_specs=pl.BlockSpec((2, tm, tn), lambda i, j: (0, i, j)),
        out_shape=jax.ShapeDtypeStruct((2, t, D_MODEL), BF16),
        compiler_params=_params("parallel", "parallel"),
    )(doab, wo3, wo3)


def _weight_grad(a, a_sel, b, b_sel, tm, tn, tk, name):
    t = a.shape[-2]
    nk = t // tk

    def body(a_ref, b_ref, o_ref, acc_ref):
        k = pl.program_id(2)
        part = _dot(a_ref[...], b_ref[...], 0, 0)

        @pl.when(k == 0)
        def _():
            acc_ref[...] = part

        @pl.when(k > 0)
        def _():
            acc_ref[...] += part

        @pl.when(k == nk - 1)
        def _():
            o_ref[...] = acc_ref[...].astype(BF16)

    def spec(arr, sel, width, which):
        if arr.ndim == 2:
            return pl.BlockSpec((tk, width), lambda i, j, k: (k, (i, j)[which]))
        return pl.BlockSpec((None, tk, width), lambda i, j, k: (sel, k, (i, j)[which]))

    return pl.pallas_call(
        body, name=name, grid=(D_MODEL // tm, D_MODEL // tn, nk),
        in_specs=[spec(a, a_sel, tm, 0), spec(b, b_sel, tn, 1)],
        out_specs=pl.BlockSpec((tm, tn), lambda i, j, k: (i, j)),
        out_shape=jax.ShapeDtypeStruct((D_MODEL, D_MODEL), BF16),
        scratch_shapes=[pltpu.VMEM((tm, tn), F32)],
        compiler_params=_params("parallel", "parallel", "arbitrary"),
    )(a, b)


def _win_grad(xb, dp, tn, tk):
    _, t, _ = dp.shape
    nk = t // tk
    per_seg = D_MODEL // tn
    nj = N_SEG * per_seg

    def body(x_ref, dp_ref, o_ref, acc_ref):
        k = pl.program_id(1)
        part = _dot(x_ref[...], dp_ref[...], 0, 0)

        @pl.when(k == 0)
        def _():
            acc_ref[...] = part

        @pl.when(k > 0)
        def _():
            acc_ref[...] += part

        @pl.when(k == nk - 1)
        def _():
            o_ref[...] = acc_ref[...].astype(BF16)

    return pl.pallas_call(
        body, name="grad_w_in", grid=(nj, nk),
        in_specs=[pl.BlockSpec((tk, D_MODEL), lambda j, k: (k, 0)),
                  pl.BlockSpec((None, tk, tn), lambda j, k: (_slot_of_seg(j // per_seg), k, j % per_seg))],
        out_specs=pl.BlockSpec((D_MODEL, tn), lambda j, k: (0, j)),
        out_shape=jax.ShapeDtypeStruct((D_MODEL, N_SEG * D_MODEL), BF16),
        scratch_shapes=[pltpu.VMEM((D_MODEL, tn), F32)],
        compiler_params=_params("parallel", "arbitrary"),
    )(xb, dp)


def _input_grad_and_scatter(dp, w_full, gx, s_in, s_o, tm):
    _, t, _ = dp.shape
    ni, nk = t // tm, N_SEG * BLOCKS_PER_SEG
    _, hd, sc = s_in.shape
    hr = s_o.shape[2]

    def body(dp_ref, w_ref, gx_ref, si_ref, so_ref, o_ref, ri_ref, ro_ref, send_sems, recv_sems):
        i, k = pl.program_id(0), pl.program_id(1)
        x, y, c, chips = _mesh_pos()
        me = 2 * x + y

        def sends():
            cps = []
            for q, (cx, cy) in enumerate(chips):
                dest = 2 * cx + cy
                cps.append(pltpu.make_async_remote_copy(src_ref=si_ref.at[dest], dst_ref=ri_ref.at[me],
                                                        send_sem=send_sems.at[q], recv_sem=recv_sems.at[q],
                                                        device_id=(cx, cy, c), device_id_type=MESH))
                cps.append(pltpu.make_async_remote_copy(src_ref=so_ref.at[:, dest], dst_ref=ro_ref.at[me],
                                                        send_sem=send_sems.at[3 + q], recv_sem=recv_sems.at[3 + q],
                                                        device_id=(cx, cy, c), device_id_type=MESH))
            return cps

        @pl.when(jnp.logical_and(i == 0, k == 0))
        def _():
            for cp in sends():
                cp.start()

        part = _dot(dp_ref[...], w_ref[...], 1, 1)

        @pl.when(k == 0)
        def _():
            o_ref[...] = gx_ref[...] + part

        @pl.when(k > 0)
        def _():
            o_ref[...] += part

        @pl.when(jnp.logical_and(i == ni - 1, k == nk - 1))
        def _():
            for q, (cx, cy) in enumerate(chips):
                frm = 2 * cx + cy
                pltpu.make_async_remote_copy(src_ref=si_ref.at[frm], dst_ref=ri_ref.at[frm], send_sem=send_sems.at[q],
                                             recv_sem=recv_sems.at[q], device_id=(x, y, c),
                                             device_id_type=MESH).wait_recv()
                pltpu.make_async_remote_copy(src_ref=so_ref.at[:, frm], dst_ref=ro_ref.at[frm],
                                             send_sem=send_sems.at[3 + q], recv_sem=recv_sems.at[3 + q],
                                             device_id=(x, y, c), device_id_type=MESH).wait_recv()
            for cp in sends():
                cp.wait_send()

    return pl.pallas_call(
        body, name="grad_x_and_scatter", grid=(ni, nk),
        in_specs=[pl.BlockSpec((None, tm, COL_BLOCK),
                               lambda i, k: (_slot_of_seg(k // BLOCKS_PER_SEG), i, k % BLOCKS_PER_SEG)),
                  pl.BlockSpec((None, D_MODEL, COL_BLOCK),
                               lambda i, k: (k // BLOCKS_PER_SHARD, 0, k % BLOCKS_PER_SHARD)),
                  pl.BlockSpec((tm, D_MODEL), lambda i, k: (i, 0)), ANY, ANY],
        out_specs=(pl.BlockSpec((tm, D_MODEL), lambda i, k: (i, 0)), ANY, ANY),
        out_shape=(jax.ShapeDtypeStruct((t, D_MODEL), F32),
                   jax.ShapeDtypeStruct((N_CHIPS, hd, sc), BF16),
                   jax.ShapeDtypeStruct((N_CHIPS, 3, hr, D_MODEL), BF16)),
        scratch_shapes=[pltpu.SemaphoreType.DMA((6,)), pltpu.SemaphoreType.DMA((6,))],
        input_output_aliases={2: 0},
        compiler_params=pltpu.CompilerParams(dimension_semantics=("arbitrary", "arbitrary"),
                                             vmem_limit_bytes=VMEM_LIMIT, has_side_effects=True),
    )(dp, w_full, gx, s_in, s_o)


def _sgu_chunk_forward(u, v, z, wm, bs, lng, lnb):
    ug, dug = _gelu_and_grad(u)
    vg, dvg = _gelu_and_grad(v)
    mu = jnp.mean(vg, axis=-1, keepdims=True)
    xc = vg - mu
    var = jnp.mean(xc * xc, axis=-1, keepdims=True)
    rstd = lax.rsqrt(var + LN_EPS)
    vhat = xc * rstd
    vln = (vhat * lng + lnb).astype(BF16)
    mixed = _dot(wm, vln) + bs
    sig = _sigmoid(z)
    return ug, dug, dvg, rstd, vhat, vln, mixed, sig


def _mixer_a_forward(p_a, wm, bs_col, ln_v_g, ln_v_b, tm):
    t = p_a.shape[1]

    def body(p_ref, wm_ref, bs_ref, g_ref, b_ref, o_ref):
        wm_v, bs_v, lng, lnb = wm_ref[...], bs_ref[...], g_ref[...], b_ref[...]

        def chunk(ci, carry):
            rows = pl.ds(pl.multiple_of(ci * CHUNK, CHUNK), CHUNK)
            u = p_ref[0, rows, :].astype(F32)
            v = p_ref[1, rows, :].astype(F32)
            z = p_ref[2, rows, :].astype(F32)
            ug, _, _, _, _, _, mixed, sig = _sgu_chunk_forward(u, v, z, wm_v, bs_v, lng, lnb)
            o_ref[rows, :] = (ug * mixed * (z * sig)).astype(BF16)
            return carry

        lax.fori_loop(0, tm // CHUNK, chunk, 0)

    return pl.pallas_call(
        body, name="mixer_a_forward", grid=(t // tm, N_HEADS),
        in_specs=[pl.BlockSpec((3, tm, HEAD_DIM), lambda i, h: (0, i, h)),
                  pl.BlockSpec((None, CHUNK, CHUNK), lambda i, h: (h, 0, 0)),
                  pl.BlockSpec((None, CHUNK, 1), lambda i, h: (h, 0, 0)),
                  pl.BlockSpec((1, HEAD_DIM), lambda i, h: (0, h)),
                  pl.BlockSpec((1, HEAD_DIM), lambda i, h: (0, h))],
        out_specs=pl.BlockSpec((tm, HEAD_DIM), lambda i, h: (i, h)),
        out_shape=jax.ShapeDtypeStruct((t, D_MODEL), BF16),
        compiler_params=_params("parallel", "parallel"),
    )(p_a, wm, bs_col, ln_v_g, ln_v_b)


def _mixer_a_backward(p_a, dyab, wm, bs_col, ln_v_g, ln_v_b, dp, tm):
    t = p_a.shape[1]

    def body(p_ref, dy_ref, wm_ref, bs_ref, g_ref, b_ref, dp_in, dp_ref, dws_ref, dbs_ref, dg_ref, db_ref):
        @pl.when(pl.program_id(1) == 0)
        def _():
            dws_ref[...] = jnp.zeros_like(dws_ref)
            dbs_ref[...] = jnp.zeros_like(dbs_ref)
            dg_ref[...] = jnp.zeros_like(dg_ref)
            db_ref[...] = jnp.zeros_like(db_ref)

        wm_v, bs_v, lng, lnb = wm_ref[...], bs_ref[...], g_ref[...], b_ref[...]
        causal = (lax.broadcasted_iota(jnp.int32, (CHUNK, CHUNK), 1)
                  <= lax.broadcasted_iota(jnp.int32, (CHUNK, CHUNK), 0))

        def chunk(ci, carry):
            rows = pl.ds(pl.multiple_of(ci * CHUNK, CHUNK), CHUNK)
            u = p_ref[0, rows, :].astype(F32)
            v = p_ref[1, rows, :].astype(F32)
            z = p_ref[2, rows, :].astype(F32)
            dy = dy_ref[rows, :].astype(F32)
            ug, dug, dvg, rstd, vhat, vln, mixed, sig = _sgu_chunk_forward(u, v, z, wm_v, bs_v, lng, lnb)
            sz = z * sig
            dmixed = dy * ug * sz
            dp_ref[0, rows, :] = (dy * mixed * sz * dug).astype(BF16)
            dp_ref[2, rows, :] = (dy * ug * mixed * (sig * (1.0 + z * (1.0 - sig)))).astype(BF16)
            dbs_ref[...] += jnp.sum(dmixed, axis=1, keepdims=True)
            dmb = dmixed.astype(BF16)
            dws_ref[...] += jnp.where(causal, _dot(dmb, vln, 1, 1), 0.0)
            dvln = _dot(wm_v, dmb, 0, 0)
            db_ref[...] += jnp.sum(dvln, axis=0, keepdims=True)
            dg_ref[...] += jnp.sum(dvln * vhat, axis=0, keepdims=True)
            dvh = dvln * lng
            m1 = jnp.mean(dvh, axis=-1, keepdims=True)
            m2 = jnp.mean(dvh * vhat, axis=-1, keepdims=True)
            dp_ref[1, rows, :] = (rstd * (dvh - m1 - vhat * m2) * dvg).astype(BF16)
            return carry

        lax.fori_loop(0, tm // CHUNK, chunk, 0)

    return pl.pallas_call(
        body, name="mixer_a_backward", grid=(N_HEADS, t // tm),
        in_specs=[pl.BlockSpec((3, tm, HEAD_DIM), lambda h, i: (0, i, h)),
                  pl.BlockSpec((None, tm, HEAD_DIM), lambda h, i: (0, i, h)),
                  pl.BlockSpec((None, CHUNK, CHUNK), lambda h, i: (h, 0, 0)),
                  pl.BlockSpec((None, CHUNK, 1), lambda h, i: (h, 0, 0)),
                  pl.BlockSpec((1, HEAD_DIM), lambda h, i: (0, h)),
                  pl.BlockSpec((1, HEAD_DIM), lambda h, i: (0, h)), ANY],
        out_specs=(pl.BlockSpec((3, tm, HEAD_DIM), lambda h, i: (BLOCK_A, i, h)),
                   pl.BlockSpec((None, CHUNK, CHUNK), lambda h, i: (h, 0, 0)),
                   pl.BlockSpec((None, CHUNK, 1), lambda h, i: (h, 0, 0)),
                   pl.BlockSpec((1, HEAD_DIM), lambda h, i: (0, h)),
                   pl.BlockSpec((1, HEAD_DIM), lambda h, i: (0, h))),
        out_shape=(jax.ShapeDtypeStruct(dp.shape, BF16),
                   jax.ShapeDtypeStruct((N_HEADS, CHUNK, CHUNK), F32),
                   jax.ShapeDtypeStruct((N_HEADS, CHUNK, 1), F32),
                   jax.ShapeDtypeStruct((1, D_MODEL), F32), jax.ShapeDtypeStruct((1, D_MODEL), F32)),
        input_output_aliases={6: 0},
        compiler_params=_params("parallel", "arbitrary"),
    )(p_a, dyab, wm, bs_col, ln_v_g, ln_v_b, dp)


HALO = 16


def _conv_taps(h, halo_h, tm):
    row = lax.broadcasted_iota(jnp.int32, h.shape, 0)
    last1 = halo_h[HALO - 1:HALO, :]
    last2 = halo_h[HALO - 2:HALO - 1, :]
    h1 = jnp.where(row == 0, last1, pltpu.roll(h, 1, 0))
    h2 = jnp.where(row == 0, last2, jnp.where(row == 1, last1, pltpu.roll(h, 2, 0)))
    return h1, h2


def _mixer_b_forward(p_b, conv_w, conv_b, tm, tc):
    t = p_b.shape[1]

    def body(p_ref, halo_ref, w_ref, b_ref, o_ref):
        valid = (pl.program_id(1) > 0).astype(F32)
        h = p_ref[1].astype(F32) * p_ref[0].astype(F32)
        halo_h = halo_ref[1].astype(F32) * halo_ref[0].astype(F32) * valid
        h1, h2 = _conv_taps(h, halo_h, tm)
        w = w_ref[...]
        conv = b_ref[...] + w[0:1, :] * h2 + w[1:2, :] * h1 + w[2:3, :] * h
        z = p_ref[3].astype(F32)
        o_ref[...] = (p_ref[2].astype(F32) * conv * (z * _sigmoid(z))).astype(BF16)

    steps = tm // HALO
    return pl.pallas_call(
        body, name="mixer_b_forward", grid=(D_MODEL // tc, t // tm),
        in_specs=[pl.BlockSpec((4, tm, tc), lambda j, i: (BLOCK_B, i, j)),
                  pl.BlockSpec((4, HALO, tc), lambda j, i: (BLOCK_B, jnp.maximum(i * steps - 1, 0), j)),
                  pl.BlockSpec((3, tc), lambda j, i: (0, j)),
                  pl.BlockSpec((1, tc), lambda j, i: (0, j))],
        out_specs=pl.BlockSpec((tm, tc), lambda j, i: (i, j)),
        out_shape=jax.ShapeDtypeStruct((t, D_MODEL), BF16),
        compiler_params=_params("parallel", "parallel"),
    )(p_b, p_b, conv_w, conv_b)


def _mixer_b_backward(p_b, dyab, conv_w, conv_b, dp, tm, tc):
    t = p_b.shape[1]
    n = t // tm

    def body(p_ref, halo_ref, dy_ref, w_ref, b_ref, dp_in, dp_ref, dw_ref, db_ref, next_ref):
        ii = pl.program_id(1)

        @pl.when(ii == 0)
        def _():
            dw_ref[...] = jnp.zeros_like(dw_ref)
            db_ref[...] = jnp.zeros_like(db_ref)
            next_ref[...] = jnp.zeros_like(next_ref)

        valid = (ii < n - 1).astype(F32)
        xb = p_ref[0].astype(F32)
        cb = p_ref[1].astype(F32)
        bb = p_ref[2].astype(F32)
        z = p_ref[3].astype(F32)
        h = cb * xb
        halo_h = halo_ref[1].astype(F32) * halo_ref[0].astype(F32) * valid
        h1, h2 = _conv_taps(h, halo_h, tm)
        w = w_ref[...]
        w0, w1, w2 = w[0:1, :], w[1:2, :], w[2:3, :]
        conv = b_ref[...] + w0 * h2 + w1 * h1 + w2 * h
        sig = _sigmoid(z)
        sz = z * sig
        dy = dy_ref[...].astype(F32)
        dconv = dy * bb * sz
        dp_ref[2] = (dy * conv * sz).astype(BF16)
        dp_ref[3] = (dy * bb * conv * (sig * (1.0 + z * (1.0 - sig)))).astype(BF16)
        db_ref[...] += jnp.sum(dconv, axis=0, keepdims=True)
        dw_ref[0:1, :] += jnp.sum(dconv * h2, axis=0, keepdims=True)
        dw_ref[1:2, :] += jnp.sum(dconv * h1, axis=0, keepdims=True)
        dw_ref[2:3, :] += jnp.sum(dconv * h, axis=0, keepdims=True)
        row = lax.broadcasted_iota(jnp.int32, h.shape, 0)
        nxt = next_ref[...]
        n0, n1 = nxt[0:1, :], nxt[1:2, :]
        d1 = jnp.where(row == tm - 1, n0, pltpu.roll(dconv, tm - 1, 0))
        d2 = jnp.where(row == tm - 1, n1, jnp.where(row == tm - 2, n0, pltpu.roll(dconv, tm - 2, 0)))
        dh = w2 * dconv + w1 * d1 + w0 * d2
        dp_ref[0] = (dh * cb).astype(BF16)
        dp_ref[1] = (dh * xb).astype(BF16)
        next_ref[...] = dconv[0:8, :]

    steps = tm // HALO
    return pl.pallas_call(
        body, name="mixer_b_backward", grid=(D_MODEL // tc, n),
        in_specs=[pl.BlockSpec((4, tm, tc), lambda j, ii: (BLOCK_B, n - 1 - ii, j)),
                  pl.BlockSpec((4, HALO, tc), lambda j, ii: (BLOCK_B, jnp.maximum((n - 1 - ii) * steps - 1, 0), j)),
                  pl.BlockSpec((None, tm, tc), lambda j, ii: (1, n - 1 - ii, j)),
                  pl.BlockSpec((3, tc), lambda j, ii: (0, j)),
                  pl.BlockSpec((1, tc), lambda j, ii: (0, j)), ANY],
        out_specs=(pl.BlockSpec((4, tm, tc), lambda j, ii: (BLOCK_B, n - 1 - ii, j)),
                   pl.BlockSpec((3, tc), lambda j, ii: (0, j)),
                   pl.BlockSpec((1, tc), lambda j, ii: (0, j))),
        out_shape=(jax.ShapeDtypeStruct(dp.shape, BF16),
                   jax.ShapeDtypeStruct((3, D_MODEL), F32), jax.ShapeDtypeStruct((1, D_MODEL), F32)),
        scratch_shapes=[pltpu.VMEM((8, tc), F32)],
        input_output_aliases={5: 0},
        compiler_params=_params("parallel", "arbitrary"),
    )(p_b, p_b, dyab, conv_w, conv_b, dp)


def _adam_math(w, g, m, v):
    m = ADAM_B1 * m + (1.0 - ADAM_B1) * g
    v = ADAM_B2 * v + (1.0 - ADAM_B2) * (g * g)
    delta = -ADAM_LR * ((m * ADAM_C1) / (jnp.sqrt(v * ADAM_C2) + ADAM_EPS) + ADAM_WD * w)
    return delta, m, v


def _adam_rows(w, g, m, v, tm, name):
    r, c = w.shape

    def body(w_ref, g_ref, m_ref, v_ref, go_ref, d_ref, mo_ref, vo_ref):
        g = g_ref[...]
        d, mn, vn = _adam_math(w_ref[...], g, m_ref[...], v_ref[...])
        go_ref[...] = g
        d_ref[...] = d
        mo_ref[...] = mn
        vo_ref[...] = vn

    spec = pl.BlockSpec((tm, c), lambda i: (i, 0))
    shape = jax.ShapeDtypeStruct((r, c), F32)
    return pl.pallas_call(
        body, name=name, grid=(r // tm,),
        in_specs=[spec] * 4, out_specs=(spec,) * 4, out_shape=(shape,) * 4,
        compiler_params=_params("parallel"),
    )(w, g, m, v)


def _adam_small(gathered, w, m, v):
    r = w.shape[0]

    def body(ga_ref, w_ref, m_ref, v_ref, g_ref, d_ref, mo_ref, vo_ref):
        g = ga_ref[0]
        for k in range(1, 8):
            g = g + ga_ref[k]
        d, mn, vn = _adam_math(w_ref[...], g, m_ref[...], v_ref[...])
        g_ref[...] = g
        d_ref[...] = d
        mo_ref[...] = mn
        vo_ref[...] = vn

    shape = jax.ShapeDtypeStruct((r, 128), F32)
    return pl.pallas_call(
        body, name="adam_small", out_shape=(shape,) * 4, compiler_params=_params(),
    )(gathered, w, m, v)


def _pack_small(parts):
    rows = []
    for name, r in SMALL_ROWS:
        a = parts[name].astype(F32).reshape(-1)
        pad = r * 128 - a.shape[0]
        if pad:
            a = jnp.concatenate([a, jnp.zeros((pad,), F32)])
        rows.append(a.reshape(r, 128))
    return jnp.concatenate(rows, axis=0)


def _unpack_small(buf, shapes):
    out, r0 = {}, 0
    for name, r in SMALL_ROWS:
        shp = shapes[name]
        n = math.prod(shp)
        out[name] = buf[r0:r0 + r].reshape(-1)[:n].reshape(shp)
        r0 += r
    return out


def kernel(x, w_in, b_gate, ln_v_g, ln_v_b, w_s, b_s, conv_w, conv_b, w_oa, w_ob, w_out, ln_g, ln_b, loss_target, m_w_in, m_b_gate, m_ln_v_g, m_ln_v_b, m_w_s, m_b_s, m_conv_w, m_conv_b, m_w_oa, m_w_ob, m_w_out, m_ln_g, m_ln_b, v_w_in, v_b_gate, v_ln_v_g, v_ln_v_b, v_w_s, v_b_s, v_conv_w, v_conv_b, v_w_oa, v_w_ob, v_w_out, v_ln_g, v_ln_b):
    t = x.shape[1]
    x2 = x[0]
    target = loss_target[0]
    chip = 2 * lax.axis_index("x") + lax.axis_index("y")
    conv_cols = conv_w.shape[2]

    w_pre = _cast_into_slot(w_in[0], chip.astype(jnp.int32).reshape(1), N_CHIPS, 256, "cast_w_in")
    wo_b, wm = _prep_small_weights(w_oa[0], w_ob[0], w_out[0], w_s[0])
    conv_w8 = jnp.concatenate([conv_w[0], jnp.zeros((5, conv_cols), F32)], axis=0)
    bs_col = b_s[0].reshape(N_HEADS, CHUNK, 1)
    bg = b_gate.reshape(2, 1, D_MODEL)

    xb = _cast_rows(x2, min(512, t), "cast_x")
    p, w_full, wo_full, cw_full = _gather_and_project(xb, w_pre, wo_b, conv_w8, min(1024, t))
    wo3 = wo_full.reshape(3, D_MODEL, D_MODEL)
    conv_w_all = jnp.transpose(cw_full[:, :3, :], (1, 0, 2)).reshape(3, D_MODEL)
    tm_a = min(512, t)
    ya = _mixer_a_forward(p, wm, bs_col, ln_v_g, ln_v_b, tm_a)
    tm_b = min(256, t)
    yb = _mixer_b_forward(p, conv_w_all, conv_b, tm_b, 256)
    tm_m = min(512, t)
    merged, oab = _merge_forward(ya, yb, wo3, p, bg, tm_m, 512)

    drb, gx, dmerged, d_ln_g, d_ln_b, loss_part = _head(merged, wo3, x2, target, ln_g, ln_b, min(256, t))
    doab, dp, d_bg = _gate_backward(dmerged, oab, p, bg, tm_m, 512)
    dyab = _branch_backward(doab, wo3, tm_m, 512)
    dp, d_ws, d_bs, d_lnv_g, d_lnv_b = _mixer_a_backward(p, dyab, wm, bs_col, ln_v_g, ln_v_b, dp, tm_a)
    dp, d_cw, d_cb = _mixer_b_backward(p, dyab, conv_w_all, conv_b, dp, tm_b, 256)

    tk = min(512, t)
    g_oa = _weight_grad(ya, 0, doab, 0, 1024, 1024, tk, "grad_w_oa")
    g_ob = _weight_grad(yb, 0, doab, 1, 1024, 1024, tk, "grad_w_ob")
    g_out = _weight_grad(merged, 0, drb, 0, 1024, 1024, tk, "grad_w_out")
    g_in = _win_grad(xb, dp, 1024, tk)

    r_in, *r_o = _exchange_halves(g_in, (g_oa, g_ob, g_out))
    s_in, s_o = _add_halves(g_in, (g_oa, g_ob, g_out), r_in, r_o)
    grad_x, q_in, q_o = _input_grad_and_scatter(dp, w_full, gx, s_in, s_o, min(1024, t))
    f_in, f_o = _sum_chips(q_in, q_o, s_in, s_o)
    gsum_in, gsum_o = _share_halves(f_in, f_o)

    big = {}
    big["w_in"] = _adam_rows(w_in[0], gsum_in, m_w_in[0], v_w_in[0], 128, "adam_w_in")
    for n, (name, w, m, v) in enumerate((("w_oa", w_oa, m_w_oa, v_w_oa), ("w_ob", w_ob, m_w_ob, v_w_ob),
                                         ("w_out", w_out, m_w_out, v_w_out))):
        big[name] = _adam_rows(w[0], gsum_o[n], m[0], v[0], 256, "adam_" + name)

    small_grads = {"ln_g": d_ln_g, "ln_b": d_ln_b, "b_gate": d_bg, "ln_v_g": d_lnv_g, "ln_v_b": d_lnv_b,
                   "w_s": d_ws, "b_s": d_bs, "conv_w": d_cw, "conv_b": d_cb, "loss": loss_part[:, :1]}
    gathered = _gather_small(_pack_small(small_grads))

    def placed(a):
        return lax.dynamic_update_slice(jnp.zeros((3, D_MODEL), F32), a[0], (0, chip * conv_cols))

    def packed(pre, cw):
        z1 = jnp.zeros((1,), F32)
        return _pack_small({"ln_g": pre["ln_g"], "ln_b": pre["ln_b"], "b_gate": pre["b_gate"],
                            "ln_v_g": pre["ln_v_g"], "ln_v_b": pre["ln_v_b"], "w_s": pre["w_s"], "b_s": pre["b_s"],
                            "conv_w": placed(cw), "conv_b": pre["conv_b"], "loss": z1})

    ws = dict(ln_g=ln_g, ln_b=ln_b, b_gate=b_gate, ln_v_g=ln_v_g, ln_v_b=ln_v_b, w_s=w_s, b_s=b_s, conv_b=conv_b)
    ms = dict(ln_g=m_ln_g, ln_b=m_ln_b, b_gate=m_b_gate, ln_v_g=m_ln_v_g, ln_v_b=m_ln_v_b, w_s=m_w_s, b_s=m_b_s,
              conv_b=m_conv_b)
    vs = dict(ln_g=v_ln_g, ln_b=v_ln_b, b_gate=v_b_gate, ln_v_g=v_ln_v_g, ln_v_b=v_ln_v_b, w_s=v_w_s, b_s=v_b_s,
              conv_b=v_conv_b)
    s_g, s_d, s_m, s_v = _adam_small(gathered, packed(ws, conv_w), packed(ms, m_conv_w), packed(vs, v_conv_w))
    shapes = {"ln_g": ln_g.shape, "ln_b": ln_b.shape, "b_gate": b_gate.shape, "ln_v_g": ln_v_g.shape,
              "ln_v_b": ln_v_b.shape, "w_s": w_s.shape, "b_s": b_s.shape, "conv_w": (1, 3, D_MODEL),
              "conv_b": conv_b.shape, "loss": (1,)}
    small = [_unpack_small(b, shapes) for b in (s_g, s_d, s_m, s_v)]
    for d in small:
        d["conv_w"] = lax.dynamic_slice(d["conv_w"], (0, 0, chip * conv_cols), (1, 3, conv_cols))
    loss = small[0]["loss"][0]

    order = ("w_in", "b_gate", "ln_v_g", "ln_v_b", "w_s", "b_s", "conv_w", "conv_b", "w_oa", "w_ob", "w_out",
             "ln_g", "ln_b")
    outs = [loss, grad_x[None]]
    for which in range(4):
        for name in order:
            outs.append(big[name][which][None] if name in big else small[which][name])
    return tuple(outs)
```

```python
import functools
import math

import jax
import jax.numpy as jnp
from jax import lax
from jax.experimental import pallas as pl
from jax.experimental.pallas import tpu as pltpu

F32 = jnp.float32
BF16 = jnp.bfloat16

D_MODEL = 2048
N_HEADS = 8
HEAD_DIM = D_MODEL // N_HEADS
CHUNK = 128
N_SEG = 9
N_CHIPS = 4
SHARD_COLS = N_SEG * D_MODEL // N_CHIPS
COL_BLOCK = 512
BLOCKS_PER_SHARD = SHARD_COLS // COL_BLOCK
BLOCKS_PER_SEG = D_MODEL // COL_BLOCK
SHARD_ROWS = D_MODEL // N_CHIPS
N_SLOTS = 12
BLOCK_A, BLOCK_G, BLOCK_B = 0, 2, 2
DN_ALPHA = 2.0 ** 0.25
LN_EPS = 1e-5
GELU_K = math.sqrt(2.0 / math.pi)
GELU_C = 0.044715

ADAM_LR = 0.001
ADAM_B1 = 0.9
ADAM_B2 = 0.999
ADAM_EPS = 1e-08
ADAM_WD = 0.01
ADAM_STEP = 10
ADAM_C1 = 1.0 / (1.0 - ADAM_B1 ** ADAM_STEP)
ADAM_C2 = 1.0 / (1.0 - ADAM_B2 ** ADAM_STEP)

VMEM_LIMIT = 56 * 1024 * 1024
MESH = pl.DeviceIdType.MESH
ANY = pl.BlockSpec(memory_space=pl.ANY)

SMALL_ROWS = (("ln_g", 16), ("ln_b", 16), ("b_gate", 32), ("ln_v_g", 16), ("ln_v_b", 16),
              ("w_s", 1024), ("b_s", 8), ("conv_w", 48), ("conv_b", 16), ("loss", 8))
SMALL_TOTAL = sum(r for _, r in SMALL_ROWS)


def _params(*sem):
    return pltpu.CompilerParams(dimension_semantics=sem, vmem_limit_bytes=VMEM_LIMIT)


def _sigmoid(x):
    return 1.0 / (1.0 + jnp.exp(-x))


def _gelu_and_grad(x):
    x2 = x * x
    th = jnp.tanh(GELU_K * (x + GELU_C * x * x2))
    g = 0.5 * x * (1.0 + th)
    dg = 0.5 * (1.0 + th) + 0.5 * x * (1.0 - th * th) * (GELU_K * (1.0 + 3.0 * GELU_C * x2))
    return g, dg


def _gelu(x):
    return 0.5 * x * (1.0 + jnp.tanh(GELU_K * (x + GELU_C * x * x * x)))


def _dot(a, b, ca=1, cb=0):
    return lax.dot_general(a, b, (((ca,), (cb,)), ((), ())), preferred_element_type=F32)


def _cast_rows(x, tm, name):
    r, c = x.shape

    def body(x_ref, o_ref):
        o_ref[...] = x_ref[...].astype(BF16)

    return pl.pallas_call(
        body, name=name, grid=(r // tm,),
        in_specs=[pl.BlockSpec((tm, c), lambda i: (i, 0))],
        out_specs=pl.BlockSpec((tm, c), lambda i: (i, 0)),
        out_shape=jax.ShapeDtypeStruct((r, c), BF16),
        compiler_params=_params("parallel"),
    )(x)


def _cast_into_columns(w, slot, n_slots, tm, name):
    r, c = w.shape

    def body(s_ref, w_ref, o_ref):
        o_ref[...] = w_ref[...].astype(BF16)

    return pl.pallas_call(
        body, name=name,
        grid_spec=pltpu.PrefetchScalarGridSpec(
            num_scalar_prefetch=1, grid=(r // tm,),
            in_specs=[pl.BlockSpec((tm, c), lambda i, s_ref: (i, 0))],
            out_specs=pl.BlockSpec((tm, c), lambda i, s_ref: (i, s_ref[0]))),
        out_shape=jax.ShapeDtypeStruct((r, n_slots * c), BF16),
        compiler_params=_params("parallel"),
    )(slot, w)


def _prep_small_weights(w_oa, w_ob, w_out, w_s):
    rows = w_oa.shape[0]

    def body(a_ref, b_ref, c_ref, ws_ref, wo_ref, wm_ref):
        wo_ref[0] = a_ref[...].astype(BF16)
        wo_ref[1] = b_ref[...].astype(BF16)
        wo_ref[2] = c_ref[...].astype(BF16)
        t = lax.broadcasted_iota(jnp.int32, (CHUNK, CHUNK), 0)
        s = lax.broadcasted_iota(jnp.int32, (CHUNK, CHUNK), 1)
        for h in range(N_HEADS):
            wm_ref[h] = jnp.where(s <= t, ws_ref[h], 0.0).astype(BF16)

    return pl.pallas_call(
        body, name="prep_small_weights",
        out_shape=(jax.ShapeDtypeStruct((3, rows, D_MODEL), BF16),
                   jax.ShapeDtypeStruct((N_HEADS, CHUNK, CHUNK), BF16)),
        compiler_params=_params(),
    )(w_oa, w_ob, w_out, w_s)


def _mesh_pos():
    x, y, c = lax.axis_index("x"), lax.axis_index("y"), lax.axis_index("c")
    chips = [(1 - x, y), (x, 1 - y), (1 - x, 1 - y)]
    return x, y, c, chips


def _slot_of_seg(seg):
    return jnp.where(seg < 3, seg, jnp.where(seg < 7, seg + 5, seg - 3))


def _gather_and_project(xb, w_pre, wo_b, conv_w8, tm):
    t = xb.shape[0]
    d, sc = w_pre.shape[0], w_pre.shape[1] // N_CHIPS
    rows = wo_b.shape[1]
    hd, hr = d // 2, rows // 2
    ni, nj = t // tm, BLOCKS_PER_SHARD
    total = N_CHIPS * ni * nj
    mx, my = lax.axis_index("x"), lax.axis_index("y")
    order = jnp.stack([2 * mx + my, 2 * (1 - mx) + my, 2 * mx + (1 - my),
                       2 * (1 - mx) + (1 - my)]).astype(jnp.int32)

    def body(order_ref, x_ref, wpre_ref, wo_ref, cw_ref, p_ref, wf_ref, wof_ref, cwf_ref,
             wbuf, wsem, send_sems, recv_sems, local_sems):
        x, y, c, chips = _mesh_pos()
        me = 2 * x + y
        sibling = (x, y, 1 - c)
        s, i, j = pl.program_id(0), pl.program_id(1), pl.program_id(2)
        n = (s * ni + i) * nj + j

        def shard_cols(k):
            return pl.ds(pl.multiple_of(k * sc, COL_BLOCK), sc)

        def half(kind, k, cc):
            if kind == 0:
                return wf_ref.at[pl.ds(cc * hd, hd), shard_cols(k)]
            return wof_ref.at[:, k, pl.ds(cc * hr, hr), :]

        def src_half(kind, cc):
            if kind == 0:
                return wpre_ref.at[pl.ds(cc * hd, hd), shard_cols(me)]
            return wo_ref.at[:, pl.ds(cc * hr, hr), :]

        def rcopy(sem, src, dst, to):
            return pltpu.make_async_remote_copy(src_ref=src, dst_ref=dst, send_sem=send_sems.at[sem],
                                                recv_sem=recv_sems.at[sem], device_id=to, device_id_type=MESH)

        def ici_send(kind, q):
            return rcopy(kind * 6 + q, src_half(kind, c), half(kind, me, c), (*chips[q], c))

        def ici_landed(kind, q):
            k = 2 * chips[q][0] + chips[q][1]
            return rcopy(kind * 6 + q, src_half(kind, c), half(kind, k, c), sibling)

        def forward(kind, q, cc):
            k = 2 * chips[q][0] + chips[q][1]
            return rcopy(kind * 6 + 3 + q, half(kind, k, cc), half(kind, k, cc), sibling)

        def conv_send(q):
            return rcopy(12 + q, cw_ref, cwf_ref.at[me], (*chips[q], c))

        def local_copies():
            return [pltpu.make_async_copy(wo_ref, wof_ref.at[:, me], local_sems.at[0]),
                    pltpu.make_async_copy(cw_ref, cwf_ref.at[me], local_sems.at[1])]

        def tile_start(s_, j_, slot):
            g = order_ref[s_] * BLOCKS_PER_SHARD + j_
            cols = pl.ds(pl.multiple_of(g * COL_BLOCK, COL_BLOCK), COL_BLOCK)

            @pl.when(s_ == 0)
            def _():
                pltpu.make_async_copy(wpre_ref.at[:, cols], wbuf.at[slot], wsem.at[slot]).start()

            @pl.when(s_ > 0)
            def _():
                pltpu.make_async_copy(wf_ref.at[:, cols], wbuf.at[slot], wsem.at[slot]).start()

        def at(s_, i_, j_):
            return jnp.logical_and(s == s_, jnp.logical_and(i == i_, j == j_))

        @pl.when(n == 0)
        def _():
            for cp in local_copies():
                cp.start()
            for q in (0, 1):
                ici_send(0, q).start()
            for q in range(3):
                conv_send(q).start()
            tile_start(0, 0, 0)

        @pl.when(at(0, ni - 1, nj - 1))
        def _():
            for q in (0, 1):
                ici_landed(0, q).wait_recv()
                forward(0, q, c).start()
            for q in (0, 1):
                ici_send(0, q).wait_send()
            ici_send(0, 2).start()
            forward(0, 0, 1 - c).wait_recv()

        @pl.when(at(1, ni - 1, nj - 1))
        def _():
            forward(0, 1, 1 - c).wait_recv()

        @pl.when(at(2, (3 * ni) // 4, 0))
        def _():
            ici_landed(0, 2).wait_recv()
            forward(0, 2, c).start()
            ici_send(0, 2).wait_send()
            for q in range(3):
                ici_send(1, q).start()

        @pl.when(at(2, ni - 1, nj - 1))
        def _():
            forward(0, 2, 1 - c).wait_recv()

        n1 = n + 1

        @pl.when(n1 < total)
        def _():
            tile_start(n1 // (ni * nj), lax.rem(n1, nj), lax.rem(n1, 2))

        slot = lax.rem(n, 2)
        pltpu.make_async_copy(wpre_ref.at[:, pl.ds(0, COL_BLOCK)], wbuf.at[slot], wsem.at[slot]).wait()
        p_ref[...] = _dot(x_ref[...], wbuf[slot]).astype(BF16)

        @pl.when(n == total - 1)
        def _():
            for q in range(3):
                ici_landed(1, q).wait_recv()
                forward(1, q, c).start()
            for q in range(3):
                forward(1, q, 1 - c).wait_recv()
                rcopy(12 + q, cw_ref, cwf_ref.at[2 * chips[q][0] + chips[q][1]], sibling).wait_recv()
            for q in range(3):
                ici_send(1, q).wait_send()
                forward(0, q, c).wait_send()
                forward(1, q, c).wait_send()
                conv_send(q).wait_send()
            for cp in local_copies():
                cp.wait()

    def p_map(s, i, j, o):
        g = o[s] * BLOCKS_PER_SHARD + j
        return (_slot_of_seg(g // BLOCKS_PER_SEG), i, lax.rem(g, BLOCKS_PER_SEG))

    return pl.pallas_call(
        body, name="gather_and_project",
        grid_spec=pltpu.PrefetchScalarGridSpec(
            num_scalar_prefetch=1, grid=(N_CHIPS, ni, nj),
            in_specs=[pl.BlockSpec((tm, D_MODEL), lambda s, i, j, o: (i, 0)), ANY, ANY, ANY],
            out_specs=(pl.BlockSpec((None, tm, COL_BLOCK), p_map), ANY, ANY, ANY),
            scratch_shapes=[pltpu.VMEM((2, d, COL_BLOCK), BF16), pltpu.SemaphoreType.DMA((2,)),
                            pltpu.SemaphoreType.DMA((15,)), pltpu.SemaphoreType.DMA((15,)),
                            pltpu.SemaphoreType.DMA((2,))]),
        out_shape=(jax.ShapeDtypeStruct((N_SLOTS, t, D_MODEL), BF16),
                   jax.ShapeDtypeStruct((d, N_CHIPS * sc), BF16),
                   jax.ShapeDtypeStruct((3, N_CHIPS, rows, D_MODEL), BF16),
                   jax.ShapeDtypeStruct((N_CHIPS,) + conv_w8.shape, F32)),
        input_output_aliases={2: 1},
        compiler_params=pltpu.CompilerParams(dimension_semantics=("arbitrary",) * 3, vmem_limit_bytes=VMEM_LIMIT,
                                             has_side_effects=True),
    )(order, xb, w_pre, wo_b, conv_w8)


def _gather_small(small):
    def body(s_ref, o_ref, send_sems, recv_sems, local_sem):
        x, y, c, _ = _mesh_pos()
        me = 4 * x + 2 * y + c
        mine = pltpu.make_async_copy(s_ref, o_ref.at[me], local_sem)
        mine.start()
        sends = []
        for k in range(1, 8):
            dx, dy, dc = (k >> 2) & 1, (k >> 1) & 1, k & 1
            to = (x ^ dx, y ^ dy, c ^ dc)
            cp = pltpu.make_async_remote_copy(src_ref=s_ref, dst_ref=o_ref.at[me], send_sem=send_sems.at[k - 1],
                                              recv_sem=recv_sems.at[k - 1], device_id=to, device_id_type=MESH)
            cp.start()
            sends.append(cp)
        for k in range(1, 8):
            dx, dy, dc = (k >> 2) & 1, (k >> 1) & 1, k & 1
            frm = 4 * (x ^ dx) + 2 * (y ^ dy) + (c ^ dc)
            pltpu.make_async_remote_copy(src_ref=s_ref, dst_ref=o_ref.at[frm], send_sem=send_sems.at[k - 1],
                                         recv_sem=recv_sems.at[k - 1], device_id=(x, y, c),
                                         device_id_type=MESH).wait_recv()
        for cp in sends:
            cp.wait_send()
        mine.wait()

    return pl.pallas_call(
        body, name="gather_small",
        in_specs=[ANY], out_specs=ANY,
        out_shape=jax.ShapeDtypeStruct((8,) + small.shape, small.dtype),
        scratch_shapes=[pltpu.SemaphoreType.DMA((7,)), pltpu.SemaphoreType.DMA((7,)), pltpu.SemaphoreType.DMA],
        compiler_params=pltpu.CompilerParams(has_side_effects=True),
    )(small)


def _exchange_halves(g_in, g_o):
    d, c9 = g_in.shape
    hd = d // 2
    hr = SHARD_ROWS // 2
    g_o4 = [g.reshape(N_CHIPS, 2, hr, D_MODEL) for g in g_o]

    def body(gi_ref, ga_ref, gb_ref, gc_ref, ri_ref, ra_ref, rb_ref, rc_ref, send_sems, recv_sems):
        x, y, c, _ = _mesh_pos()
        sibling = (x, y, 1 - c)
        cps = [pltpu.make_async_remote_copy(src_ref=gi_ref.at[pl.ds((1 - c) * hd, hd), :], dst_ref=ri_ref,
                                            send_sem=send_sems.at[0], recv_sem=recv_sems.at[0],
                                            device_id=sibling, device_id_type=MESH)]
        for n, (g_ref, r_ref) in enumerate(((ga_ref, ra_ref), (gb_ref, rb_ref), (gc_ref, rc_ref))):
            cps.append(pltpu.make_async_remote_copy(src_ref=g_ref.at[:, 1 - c], dst_ref=r_ref,
                                                    send_sem=send_sems.at[1 + n], recv_sem=recv_sems.at[1 + n],
                                                    device_id=sibling, device_id_type=MESH))
        for cp in cps:
            cp.start()
        for cp in cps:
            cp.wait()

    o_shape = jax.ShapeDtypeStruct((N_CHIPS, hr, D_MODEL), BF16)
    return pl.pallas_call(
        body, name="rs_exchange_halves",
        in_specs=[ANY] * 4, out_specs=(ANY,) * 4,
        out_shape=(jax.ShapeDtypeStruct((hd, c9), BF16), o_shape, o_shape, o_shape),
        scratch_shapes=[pltpu.SemaphoreType.DMA((4,)), pltpu.SemaphoreType.DMA((4,))],
        compiler_params=pltpu.CompilerParams(has_side_effects=True),
    )(g_in, *g_o4)


def _add_halves(g_in, g_o, r_in, r_o):
    d, c9 = g_in.shape
    hd = d // 2
    hr = SHARD_ROWS // 2
    core = lax.axis_index("c").astype(jnp.int32).reshape(1)
    tm = min(512, hd)
    nb = hd // tm

    def body_in(c_ref, g_ref, r_ref, o_ref):
        o_ref[...] = (g_ref[...].astype(F32) + r_ref[...].astype(F32)).astype(BF16)

    s_in = pl.pallas_call(
        body_in, name="rs_add_halves_in",
        grid_spec=pltpu.PrefetchScalarGridSpec(
            num_scalar_prefetch=1, grid=(N_CHIPS, nb),
            in_specs=[pl.BlockSpec((tm, SHARD_COLS), lambda k, i, c_ref: (c_ref[0] * nb + i, k)),
                      pl.BlockSpec((tm, SHARD_COLS), lambda k, i, c_ref: (i, k))],
            out_specs=pl.BlockSpec((None, tm, SHARD_COLS), lambda k, i, c_ref: (k, i, 0))),
        out_shape=jax.ShapeDtypeStruct((N_CHIPS, hd, SHARD_COLS), BF16),
        compiler_params=_params("parallel", "parallel"),
    )(core, g_in, r_in)

    g_o4 = [g.reshape(N_CHIPS, 2, hr, D_MODEL) for g in g_o]

    def body_o(c_ref, ga_ref, gb_ref, gc_ref, ra_ref, rb_ref, rc_ref, o_ref):
        for n, (g_ref, r_ref) in enumerate(((ga_ref, ra_ref), (gb_ref, rb_ref), (gc_ref, rc_ref))):
            o_ref[n] = (g_ref[...].astype(F32) + r_ref[...].astype(F32)).astype(BF16)

    gspec = pl.BlockSpec((None, None, hr, D_MODEL), lambda k, c_ref: (k, c_ref[0], 0, 0))
    rspec = pl.BlockSpec((None, hr, D_MODEL), lambda k, c_ref: (k, 0, 0))
    s_o = pl.pallas_call(
        body_o, name="rs_add_halves_o",
        grid_spec=pltpu.PrefetchScalarGridSpec(
            num_scalar_prefetch=1, grid=(N_CHIPS,),
            in_specs=[gspec] * 3 + [rspec] * 3,
            out_specs=pl.BlockSpec((3, None, hr, D_MODEL), lambda k, c_ref: (0, k, 0, 0))),
        out_shape=jax.ShapeDtypeStruct((3, N_CHIPS, hr, D_MODEL), BF16),
        compiler_params=_params("parallel"),
    )(core, *g_o4, *r_o)
    return s_in, s_o


def _sum_chips(r_in, r_o, s_in, s_o):
    _, hd, sc = r_in.shape
    hr = r_o.shape[2]
    tm = min(256, hd)
    nb = hd // tm
    pos = jnp.stack([2 * lax.axis_index("x") + lax.axis_index("y"), lax.axis_index("c")]).astype(jnp.int32)

    def chip_sum(pos_ref, r_ref, s_ref):
        acc = None
        for k in range(N_CHIPS):
            term = jnp.where(pos_ref[0] == k, s_ref[...], r_ref[k]).astype(F32)
            acc = term if acc is None else acc + term
        return acc

    def body_in(pos_ref, r_ref, s_ref, o_ref):
        o_ref[...] = chip_sum(pos_ref, r_ref, s_ref)

    f_in = pl.pallas_call(
        body_in, name="rs_sum_chips_in",
        grid_spec=pltpu.PrefetchScalarGridSpec(
            num_scalar_prefetch=1, grid=(nb,),
            in_specs=[pl.BlockSpec((N_CHIPS, tm, sc), lambda i, p: (0, i, 0)),
                      pl.BlockSpec((None, tm, sc), lambda i, p: (p[0], i, 0))],
            out_specs=pl.BlockSpec((tm, sc), lambda i, p: (p[1] * nb + i, 0))),
        out_shape=jax.ShapeDtypeStruct((2 * hd, sc), F32),
        compiler_params=_params("parallel"),
    )(pos, r_in, s_in)

    def body_o(pos_ref, r_ref, s_ref, o_ref):
        o_ref[...] = chip_sum(pos_ref, r_ref, s_ref)

    f_o = pl.pallas_call(
        body_o, name="rs_sum_chips_o",
        grid_spec=pltpu.PrefetchScalarGridSpec(
            num_scalar_prefetch=1, grid=(3,),
            in_specs=[pl.BlockSpec((N_CHIPS, None, hr, D_MODEL), lambda n, p: (0, n, 0, 0)),
                      pl.BlockSpec((None, None, hr, D_MODEL), lambda n, p: (n, p[0], 0, 0))],
            out_specs=pl.BlockSpec((None, hr, D_MODEL), lambda n, p: (n, p[1], 0))),
        out_shape=jax.ShapeDtypeStruct((3, 2 * hr, D_MODEL), F32),
        compiler_params=_params("parallel"),
    )(pos, r_o, s_o)
    return f_in, f_o


def _share_halves(f_in, f_o):
    hd, sc = f_in.shape[0] // 2, f_in.shape[1]
    hr = f_o.shape[1] // 2

    def body(fi_ref, fo_ref, gi_ref, go_ref, send_sems, recv_sems):
        x, y, c, _ = _mesh_pos()
        sibling = (x, y, 1 - c)

        def halves(cc):
            rows_i, rows_o = pl.ds(cc * hd, hd), pl.ds(cc * hr, hr)
            return (fi_ref.at[rows_i, :], gi_ref.at[rows_i, :]), (fo_ref.at[:, rows_o, :], go_ref.at[:, rows_o, :])

        def copies(cc):
            return [pltpu.make_async_remote_copy(src_ref=src, dst_ref=dst, send_sem=send_sems.at[n],
                                                 recv_sem=recv_sems.at[n], device_id=sibling, device_id_type=MESH)
                    for n, (src, dst) in enumerate(halves(cc))]

        sends = copies(c)
        for cp in sends:
            cp.start()
        for cp in copies(1 - c):
            cp.wait_recv()
        for cp in sends:
            cp.wait_send()

    return pl.pallas_call(
        body, name="rs_share_halves",
        in_specs=[ANY, ANY], out_specs=(ANY, ANY),
        out_shape=(jax.ShapeDtypeStruct(f_in.shape, F32), jax.ShapeDtypeStruct(f_o.shape, F32)),
        scratch_shapes=[pltpu.SemaphoreType.DMA((2,)), pltpu.SemaphoreType.DMA((2,))],
        input_output_aliases={0: 0, 1: 1},
        compiler_params=pltpu.CompilerParams(has_side_effects=True),
    )(f_in, f_o)


def _merge_forward(ya, yb, wo3, p, bg, tm, tn):
    t = ya.shape[0]

    def body(ya_ref, yb_ref, wa_ref, wb_ref, g_ref, bg_ref, m_ref, oab_ref):
        oa = _dot(ya_ref[...], wa_ref[...])
        ob = _dot(yb_ref[...], wb_ref[...])
        ga = _sigmoid(g_ref[0].astype(F32) + bg_ref[0])
        gb = _sigmoid(g_ref[1].astype(F32) + bg_ref[1])
        m_ref[...] = (ga * oa + gb * ob).astype(BF16)
        oab_ref[0] = oa.astype(BF16)
        oab_ref[1] = ob.astype(BF16)

    return pl.pallas_call(
        body, name="merge_forward", grid=(t // tm, D_MODEL // tn),
        in_specs=[pl.BlockSpec((tm, D_MODEL), lambda i, j: (i, 0)),
                  pl.BlockSpec((tm, D_MODEL), lambda i, j: (i, 0)),
                  pl.BlockSpec((None, D_MODEL, tn), lambda i, j: (0, 0, j)),
                  pl.BlockSpec((None, D_MODEL, tn), lambda i, j: (1, 0, j)),
                  pl.BlockSpec((2, tm, tn), lambda i, j: (BLOCK_G, i, j)),
                  pl.BlockSpec((2, 1, tn), lambda i, j: (0, 0, j))],
        out_specs=(pl.BlockSpec((tm, tn), lambda i, j: (i, j)),
                   pl.BlockSpec((2, tm, tn), lambda i, j: (0, i, j))),
        out_shape=(jax.ShapeDtypeStruct((t, D_MODEL), BF16), jax.ShapeDtypeStruct((2, t, D_MODEL), BF16)),
        compiler_params=_params("parallel", "parallel"),
    )(ya, yb, wo3, wo3, p, bg)


def _head(merged, wo3, x, target, ln_g, ln_b, tm):
    t = x.shape[0]
    inv_d = 1.0 / D_MODEL

    def body(m_ref, w_ref, x_ref, t_ref, g_ref, b_ref, dr_ref, gx_ref, dm_ref, dg_ref, db_ref, loss_ref):
        i = pl.program_id(0)

        @pl.when(i == 0)
        def _():
            dg_ref[...] = jnp.zeros_like(dg_ref)
            db_ref[...] = jnp.zeros_like(db_ref)
            loss_ref[...] = jnp.zeros_like(loss_ref)

        w = w_ref[...]
        out = _dot(m_ref[...], w)
        r = DN_ALPHA * x_ref[...] + out
        mu = jnp.mean(r, axis=-1, keepdims=True)
        xc = r - mu
        var = jnp.mean(xc * xc, axis=-1, keepdims=True)
        rstd = lax.rsqrt(var + LN_EPS)
        xhat = xc * rstd
        g = g_ref[...]
        e = xhat * g + b_ref[...] - t_ref[...]
        se = jnp.sum(jnp.sum(e * e, axis=1, keepdims=True), axis=0, keepdims=True)
        loss_ref[...] += jnp.broadcast_to((0.5 * inv_d) * se, loss_ref.shape)
        dy = e * inv_d
        db_ref[...] += jnp.sum(dy, axis=0, keepdims=True)
        dg_ref[...] += jnp.sum(dy * xhat, axis=0, keepdims=True)
        dxh = dy * g
        m1 = jnp.mean(dxh, axis=-1, keepdims=True)
        m2 = jnp.mean(dxh * xhat, axis=-1, keepdims=True)
        dr = rstd * (dxh - m1 - xhat * m2)
        gx_ref[...] = DN_ALPHA * dr
        drb = dr.astype(BF16)
        dr_ref[...] = drb
        dm_ref[...] = _dot(drb, w, 1, 1).astype(BF16)

    row = pl.BlockSpec((tm, D_MODEL), lambda i: (i, 0))
    vec = pl.BlockSpec((1, D_MODEL), lambda i: (0, 0))
    return pl.pallas_call(
        body, name="head", grid=(t // tm,),
        in_specs=[row, pl.BlockSpec((None, D_MODEL, D_MODEL), lambda i: (2, 0, 0)), row, row, vec, vec],
        out_specs=(row, row, row, vec, vec, pl.BlockSpec((1, 128), lambda i: (0, 0))),
        out_shape=(jax.ShapeDtypeStruct((t, D_MODEL), BF16), jax.ShapeDtypeStruct((t, D_MODEL), F32),
                   jax.ShapeDtypeStruct((t, D_MODEL), BF16), jax.ShapeDtypeStruct((1, D_MODEL), F32),
                   jax.ShapeDtypeStruct((1, D_MODEL), F32), jax.ShapeDtypeStruct((1, 128), F32)),
        compiler_params=_params("arbitrary"),
    )(merged, wo3, x, target, ln_g, ln_b)


def _gate_backward(dmerged, oab, p, bg, tm, tn):
    t = dmerged.shape[0]

    def body(dm_ref, oab_ref, g_ref, bg_ref, do_ref, dpg_ref, dbg_ref):
        @pl.when(pl.program_id(1) == 0)
        def _():
            dbg_ref[...] = jnp.zeros_like(dbg_ref)

        dm = dm_ref[...].astype(F32)
        for n in range(2):
            gate = _sigmoid(g_ref[n].astype(F32) + bg_ref[n])
            do_ref[n] = (dm * gate).astype(BF16)
            dgate = dm * oab_ref[n].astype(F32) * gate * (1.0 - gate)
            dpg_ref[n] = dgate.astype(BF16)
            dbg_ref[n] += jnp.sum(dgate, axis=0, keepdims=True)

    pair = pl.BlockSpec((2, tm, tn), lambda j, i: (0, i, j))
    gates = pl.BlockSpec((2, tm, tn), lambda j, i: (BLOCK_G, i, j))
    vec = pl.BlockSpec((2, 1, tn), lambda j, i: (0, 0, j))
    return pl.pallas_call(
        body, name="gate_backward", grid=(D_MODEL // tn, t // tm),
        in_specs=[pl.BlockSpec((tm, tn), lambda j, i: (i, j)), pair, gates, vec],
        out_specs=(pair, gates, vec),
        out_shape=(jax.ShapeDtypeStruct((2, t, D_MODEL), BF16), jax.ShapeDtypeStruct((N_SLOTS, t, D_MODEL), BF16),
                   jax.ShapeDtypeStruct((2, 1, D_MODEL), F32)),
        compiler_params=_params("parallel", "arbitrary"),
    )(dmerged, oab, p, bg)


def _branch_backward(doab, wo3, tm, tn):
    t = doab.shape[1]

    def body(d_ref, wa_ref, wb_ref, o_ref):
        o_ref[0] = _dot(d_ref[0], wa_ref[...], 1, 1).astype(BF16)
        o_ref[1] = _dot(d_ref[1], wb_ref[...], 1, 1).astype(BF16)

    return pl.pallas_call(
        body, name="branch_backward", grid=(t // tm, D_MODEL // tn),
        in_specs=[pl.BlockSpec((2, tm, D_MODEL), lambda i, j: (0, i, 0)),
                  pl.BlockSpec((None, tn, D_MODEL), lambda i, j: (0, j, 0)),
                  pl.BlockSpec((None, tn, D_MODEL), lambda i, j: (1, j, 0))],
        out_specs=pl.BlockSpec((2, tm, tn), lambda i, j: (0, i, j)),
        out_shape=jax.ShapeDtypeStruct((2, t, D_MODEL), BF16),
        compiler_params=_params("parallel", "parallel"),
    )(doab, wo3, wo3)


def _weight_grad(a, a_sel, b, b_sel, tm, tn, tk, name):
    t = a.shape[-2]
    nk = t // tk

    def body(a_ref, b_ref, o_ref, acc_ref):
        k = pl.program_id(2)
        part = _dot(a_ref[...], b_ref[...], 0, 0)

        @pl.when(k == 0)
        def _():
            acc_ref[...] = part

        @pl.when(k > 0)
        def _():
            acc_ref[...] += part

        @pl.when(k == nk - 1)
        def _():
            o_ref[...] = acc_ref[...].astype(BF16)

    def spec(arr, sel, width, which):
        if arr.ndim == 2:
            return pl.BlockSpec((tk, width), lambda i, j, k: (k, (i, j)[which]))
        return pl.BlockSpec((None, tk, width), lambda i, j, k: (sel, k, (i, j)[which]))

    return pl.pallas_call(
        body, name=name, grid=(D_MODEL // tm, D_MODEL // tn, nk),
        in_specs=[spec(a, a_sel, tm, 0), spec(b, b_sel, tn, 1)],
        out_specs=pl.BlockSpec((tm, tn), lambda i, j, k: (i, j)),
        out_shape=jax.ShapeDtypeStruct((D_MODEL, D_MODEL), BF16),
        scratch_shapes=[pltpu.VMEM((tm, tn), F32)],
        compiler_params=_params("parallel", "parallel", "arbitrary"),
    )(a, b)


def _win_grad(xb, dp, tn, tk):
    _, t, _ = dp.shape
    nk = t // tk
    per_seg = D_MODEL // tn
    nj = N_SEG * per_seg

    def body(x_ref, dp_ref, o_ref, acc_ref):
        k = pl.program_id(1)
        part = _dot(x_ref[...], dp_ref[...], 0, 0)

        @pl.when(k == 0)
        def _():
            acc_ref[...] = part

        @pl.when(k > 0)
        def _():
            acc_ref[...] += part

        @pl.when(k == nk - 1)
        def _():
            o_ref[...] = acc_ref[...].astype(BF16)

    return pl.pallas_call(
        body, name="grad_w_in", grid=(nj, nk),
        in_specs=[pl.BlockSpec((tk, D_MODEL), lambda j, k: (k, 0)),
                  pl.BlockSpec((None, tk, tn), lambda j, k: (_slot_of_seg(j // per_seg), k, j % per_seg))],
        out_specs=pl.BlockSpec((D_MODEL, tn), lambda j, k: (0, j)),
        out_shape=jax.ShapeDtypeStruct((D_MODEL, N_SEG * D_MODEL), BF16),
        scratch_shapes=[pltpu.VMEM((D_MODEL, tn), F32)],
        compiler_params=_params("parallel", "arbitrary"),
    )(xb, dp)


def _input_grad_and_scatter(dp, w_full, gx, s_in, s_o, tm):
    _, t, _ = dp.shape
    ni, nk = t // tm - 1, N_SEG
    _, hd, sc = s_in.shape
    hr = s_o.shape[2]

    def body(dp_ref, w_ref, gx_ref, si_ref, so_ref, o_ref, ri_ref, ro_ref, send_sems, recv_sems):
        i, k = pl.program_id(0), pl.program_id(1)
        x, y, c, chips = _mesh_pos()
        me = 2 * x + y

        def sends():
            cps = []
            for q, (cx, cy) in enumerate(chips):
                dest = 2 * cx + cy
                cps.append(pltpu.make_async_remote_copy(src_ref=si_ref.at[dest], dst_ref=ri_ref.at[me],
                                                        send_sem=send_sems.at[q], recv_sem=recv_sems.at[q],
                                                        device_id=(cx, cy, c), device_id_type=MESH))
                cps.append(pltpu.make_async_remote_copy(src_ref=so_ref.at[:, dest], dst_ref=ro_ref.at[me],
                                                        send_sem=send_sems.at[3 + q], recv_sem=recv_sems.at[3 + q],
                                                        device_id=(cx, cy, c), device_id_type=MESH))
            return cps

        @pl.when(jnp.logical_and(i == 0, k == 0))
        def _():
            for cp in sends():
                cp.start()

        part = _dot(dp_ref[...], w_ref[...], 1, 1)

        @pl.when(k == 0)
        def _():
            o_ref[...] = gx_ref[...] + part

        @pl.when(k > 0)
        def _():
            o_ref[...] += part

        @pl.when(jnp.logical_and(i == ni - 1, k == nk - 1))
        def _():
            for q, (cx, cy) in enumerate(chips):
                frm = 2 * cx + cy
                pltpu.make_async_remote_copy(src_ref=si_ref.at[frm], dst_ref=ri_ref.at[frm], send_sem=send_sems.at[q],
                                             recv_sem=recv_sems.at[q], device_id=(x, y, c),
                                             device_id_type=MESH).wait_recv()
                pltpu.make_async_remote_copy(src_ref=so_ref.at[:, frm], dst_ref=ro_ref.at[frm],
                                             send_sem=send_sems.at[3 + q], recv_sem=recv_sems.at[3 + q],
                                             device_id=(x, y, c), device_id_type=MESH).wait_recv()
            for cp in sends():
                cp.wait_send()

    return pl.pallas_call(
        body, name="grad_x_and_scatter", grid=(ni, nk),
        in_specs=[pl.BlockSpec((None, tm, D_MODEL), lambda i, k: (_slot_of_seg(k), i, 0)),
                  pl.BlockSpec((D_MODEL, D_MODEL), lambda i, k: (0, k)),
                  pl.BlockSpec((tm, D_MODEL), lambda i, k: (i, 0)), ANY, ANY],
        out_specs=(pl.BlockSpec((tm, D_MODEL), lambda i, k: (i, 0)), ANY, ANY),
        out_shape=(jax.ShapeDtypeStruct((t, D_MODEL), F32),
                   jax.ShapeDtypeStruct((N_CHIPS, hd, sc), BF16),
                   jax.ShapeDtypeStruct((N_CHIPS, 3, hr, D_MODEL), BF16)),
        scratch_shapes=[pltpu.SemaphoreType.DMA((6,)), pltpu.SemaphoreType.DMA((6,))],
        input_output_aliases={2: 0},
        compiler_params=pltpu.CompilerParams(dimension_semantics=("arbitrary", "arbitrary"),
                                             vmem_limit_bytes=VMEM_LIMIT, has_side_effects=True),
    )(dp, w_full, gx, s_in, s_o)


def _input_grad_tail(dp, w_full, gx, tm):
    _, t, _ = dp.shape
    last = t // tm - 1

    def body(dp_ref, w_ref, gx_ref, o_ref):
        k = pl.program_id(0)
        part = _dot(dp_ref[...], w_ref[...], 1, 1)

        @pl.when(k == 0)
        def _():
            o_ref[...] = gx_ref[...] + part

        @pl.when(k > 0)
        def _():
            o_ref[...] += part

    return pl.pallas_call(
        body, name="grad_x_tail", grid=(N_SEG,),
        in_specs=[pl.BlockSpec((None, tm, D_MODEL), lambda k: (_slot_of_seg(k), last, 0)),
                  pl.BlockSpec((D_MODEL, D_MODEL), lambda k: (0, k)),
                  pl.BlockSpec((tm, D_MODEL), lambda k: (last, 0))],
        out_specs=pl.BlockSpec((tm, D_MODEL), lambda k: (last, 0)),
        out_shape=jax.ShapeDtypeStruct((t, D_MODEL), F32),
        input_output_aliases={2: 0},
        compiler_params=_params("arbitrary"),
    )(dp, w_full, gx)


def _sgu_chunk_forward(u, v, z, wm, bs, lng, lnb):
    ug, dug = _gelu_and_grad(u)
    vg, dvg = _gelu_and_grad(v)
    mu = jnp.mean(vg, axis=-1, keepdims=True)
    xc = vg - mu
    var = jnp.mean(xc * xc, axis=-1, keepdims=True)
    rstd = lax.rsqrt(var + LN_EPS)
    vhat = xc * rstd
    vln = (vhat * lng + lnb).astype(BF16)
    mixed = _dot(wm, vln) + bs
    sig = _sigmoid(z)
    return ug, dug, dvg, rstd, vhat, vln, mixed, sig


def _mixer_a_forward(p_a, wm, bs_col, ln_v_g, ln_v_b, tm):
    t = p_a.shape[1]

    def body(p_ref, wm_ref, bs_ref, g_ref, b_ref, o_ref):
        wm_v, bs_v, lng, lnb = wm_ref[...], bs_ref[...], g_ref[...], b_ref[...]

        def chunk(ci, carry):
            rows = pl.ds(pl.multiple_of(ci * CHUNK, CHUNK), CHUNK)
            u = p_ref[0, rows, :].astype(F32)
            v = p_ref[1, rows, :].astype(F32)
            z = p_ref[2, rows, :].astype(F32)
            ug, _, _, _, _, _, mixed, sig = _sgu_chunk_forward(u, v, z, wm_v, bs_v, lng, lnb)
            o_ref[rows, :] = (ug * mixed * (z * sig)).astype(BF16)
            return carry

        lax.fori_loop(0, tm // CHUNK, chunk, 0)

    return pl.pallas_call(
        body, name="mixer_a_forward", grid=(t // tm, N_HEADS),
        in_specs=[pl.BlockSpec((3, tm, HEAD_DIM), lambda i, h: (0, i, h)),
                  pl.BlockSpec((None, CHUNK, CHUNK), lambda i, h: (h, 0, 0)),
                  pl.BlockSpec((None, CHUNK, 1), lambda i, h: (h, 0, 0)),
                  pl.BlockSpec((1, HEAD_DIM), lambda i, h: (0, h)),
                  pl.BlockSpec((1, HEAD_DIM), lambda i, h: (0, h))],
        out_specs=pl.BlockSpec((tm, HEAD_DIM), lambda i, h: (i, h)),
        out_shape=jax.ShapeDtypeStruct((t, D_MODEL), BF16),
        compiler_params=_params("parallel", "parallel"),
    )(p_a, wm, bs_col, ln_v_g, ln_v_b)


def _mixer_a_backward(p_a, dyab, wm, bs_col, ln_v_g, ln_v_b, dp, tm):
    t = p_a.shape[1]

    def body(p_ref, dy_ref, wm_ref, bs_ref, g_ref, b_ref, dp_in, dp_ref, dws_ref, dbs_ref, dg_ref, db_ref):
        @pl.when(pl.program_id(1) == 0)
        def _():
            dws_ref[...] = jnp.zeros_like(dws_ref)
            dbs_ref[...] = jnp.zeros_like(dbs_ref)
            dg_ref[...] = jnp.zeros_like(dg_ref)
            db_ref[...] = jnp.zeros_like(db_ref)

        wm_v, bs_v, lng, lnb = wm_ref[...], bs_ref[...], g_ref[...], b_ref[...]
        causal = (lax.broadcasted_iota(jnp.int32, (CHUNK, CHUNK), 1)
                  <= lax.broadcasted_iota(jnp.int32, (CHUNK, CHUNK), 0))

        def chunk(ci, carry):
            rows = pl.ds(pl.multiple_of(ci * CHUNK, CHUNK), CHUNK)
            u = p_ref[0, rows, :].astype(F32)
            v = p_ref[1, rows, :].astype(F32)
            z = p_ref[2, rows, :].astype(F32)
            dy = dy_ref[rows, :].astype(F32)
            ug, dug, dvg, rstd, vhat, vln, mixed, sig = _sgu_chunk_forward(u, v, z, wm_v, bs_v, lng, lnb)
            sz = z * sig
            dmixed = dy * ug * sz
            dp_ref[0, rows, :] = (dy * mixed * sz * dug).astype(BF16)
            dp_ref[2, rows, :] = (dy * ug * mixed * (sig * (1.0 + z * (1.0 - sig)))).astype(BF16)
            dbs_ref[...] += jnp.sum(dmixed, axis=1, keepdims=True)
            dmb = dmixed.astype(BF16)
            dws_ref[...] += jnp.where(causal, _dot(dmb, vln, 1, 1), 0.0)
            dvln = _dot(wm_v, dmb, 0, 0)
            db_ref[...] += jnp.sum(dvln, axis=0, keepdims=True)
            dg_ref[...] += jnp.sum(dvln * vhat, axis=0, keepdims=True)
            dvh = dvln * lng
            m1 = jnp.mean(dvh, axis=-1, keepdims=True)
            m2 = jnp.mean(dvh * vhat, axis=-1, keepdims=True)
            dp_ref[1, rows, :] = (rstd * (dvh - m1 - vhat * m2) * dvg).astype(BF16)
            return carry

        lax.fori_loop(0, tm // CHUNK, chunk, 0)

    return pl.pallas_call(
        body, name="mixer_a_backward", grid=(N_HEADS, t // tm),
        in_specs=[pl.BlockSpec((3, tm, HEAD_DIM), lambda h, i: (0, i, h)),
                  pl.BlockSpec((None, tm, HEAD_DIM), lambda h, i: (0, i, h)),
                  pl.BlockSpec((None, CHUNK, CHUNK), lambda h, i: (h, 0, 0)),
                  pl.BlockSpec((None, CHUNK, 1), lambda h, i: (h, 0, 0)),
                  pl.BlockSpec((1, HEAD_DIM), lambda h, i: (0, h)),
                  pl.BlockSpec((1, HEAD_DIM), lambda h, i: (0, h)), ANY],
        out_specs=(pl.BlockSpec((3, tm, HEAD_DIM), lambda h, i: (BLOCK_A, i, h)),
                   pl.BlockSpec((None, CHUNK, CHUNK), lambda h, i: (h, 0, 0)),
                   pl.BlockSpec((None, CHUNK, 1), lambda h, i: (h, 0, 0)),
                   pl.BlockSpec((1, HEAD_DIM), lambda h, i: (0, h)),
                   pl.BlockSpec((1, HEAD_DIM), lambda h, i: (0, h))),
        out_shape=(jax.ShapeDtypeStruct(dp.shape, BF16),
                   jax.ShapeDtypeStruct((N_HEADS, CHUNK, CHUNK), F32),
                   jax.ShapeDtypeStruct((N_HEADS, CHUNK, 1), F32),
                   jax.ShapeDtypeStruct((1, D_MODEL), F32), jax.ShapeDtypeStruct((1, D_MODEL), F32)),
        input_output_aliases={6: 0},
        compiler_params=_params("parallel", "arbitrary"),
    )(p_a, dyab, wm, bs_col, ln_v_g, ln_v_b, dp)


HALO = 16


def _conv_taps(h, halo_h, tm):
    row = lax.broadcasted_iota(jnp.int32, h.shape, 0)
    last1 = halo_h[HALO - 1:HALO, :]
    last2 = halo_h[HALO - 2:HALO - 1, :]
    h1 = jnp.where(row == 0, last1, pltpu.roll(h, 1, 0))
    h2 = jnp.where(row == 0, last2, jnp.where(row == 1, last1, pltpu.roll(h, 2, 0)))
    return h1, h2


def _mixer_b_forward(p_b, conv_w, conv_b, tm, tc):
    t = p_b.shape[1]

    def body(p_ref, halo_ref, w_ref, b_ref, o_ref):
        valid = (pl.program_id(1) > 0).astype(F32)
        h = p_ref[1].astype(F32) * p_ref[0].astype(F32)
        halo_h = halo_ref[1].astype(F32) * halo_ref[0].astype(F32) * valid
        h1, h2 = _conv_taps(h, halo_h, tm)
        w = w_ref[...]
        conv = b_ref[...] + w[0:1, :] * h2 + w[1:2, :] * h1 + w[2:3, :] * h
        z = p_ref[3].astype(F32)
        o_ref[...] = (p_ref[2].astype(F32) * conv * (z * _sigmoid(z))).astype(BF16)

    steps = tm // HALO
    return pl.pallas_call(
        body, name="mixer_b_forward", grid=(D_MODEL // tc, t // tm),
        in_specs=[pl.BlockSpec((4, tm, tc), lambda j, i: (BLOCK_B, i, j)),
                  pl.BlockSpec((4, HALO, tc), lambda j, i: (BLOCK_B, jnp.maximum(i * steps - 1, 0), j)),
                  pl.BlockSpec((3, tc), lambda j, i: (0, j)),
                  pl.BlockSpec((1, tc), lambda j, i: (0, j))],
        out_specs=pl.BlockSpec((tm, tc), lambda j, i: (i, j)),
        out_shape=jax.ShapeDtypeStruct((t, D_MODEL), BF16),
        compiler_params=_params("parallel", "parallel"),
    )(p_b, p_b, conv_w, conv_b)


def _mixer_b_backward(p_b, dyab, conv_w, conv_b, dp, tm, tc):
    t = p_b.shape[1]
    n = t // tm

    def body(p_ref, halo_ref, dy_ref, w_ref, b_ref, dp_in, dp_ref, dw_ref, db_ref, next_ref):
        ii = pl.program_id(1)

        @pl.when(ii == 0)
        def _():
            dw_ref[...] = jnp.zeros_like(dw_ref)
            db_ref[...] = jnp.zeros_like(db_ref)
            next_ref[...] = jnp.zeros_like(next_ref)

        valid = (ii < n - 1).astype(F32)
        xb = p_ref[0].astype(F32)
        cb = p_ref[1].astype(F32)
        bb = p_ref[2].astype(F32)
        z = p_ref[3].astype(F32)
        h = cb * xb
        halo_h = halo_ref[1].astype(F32) * halo_ref[0].astype(F32) * valid
        h1, h2 = _conv_taps(h, halo_h, tm)
        w = w_ref[...]
        w0, w1, w2 = w[0:1, :], w[1:2, :], w[2:3, :]
        conv = b_ref[...] + w0 * h2 + w1 * h1 + w2 * h
        sig = _sigmoid(z)
        sz = z * sig
        dy = dy_ref[...].astype(F32)
        dconv = dy * bb * sz
        dp_ref[2] = (dy * conv * sz).astype(BF16)
        dp_ref[3] = (dy * bb * conv * (sig * (1.0 + z * (1.0 - sig)))).astype(BF16)
        db_ref[...] += jnp.sum(dconv, axis=0, keepdims=True)
        dw_ref[0:1, :] += jnp.sum(dconv * h2, axis=0, keepdims=True)
        dw_ref[1:2, :] += jnp.sum(dconv * h1, axis=0, keepdims=True)
        dw_ref[2:3, :] += jnp.sum(dconv * h, axis=0, keepdims=True)
        row = lax.broadcasted_iota(jnp.int32, h.shape, 0)
        nxt = next_ref[...]
        n0, n1 = nxt[0:1, :], nxt[1:2, :]
        d1 = jnp.where(row == tm - 1, n0, pltpu.roll(dconv, tm - 1, 0))
        d2 = jnp.where(row == tm - 1, n1, jnp.where(row == tm - 2, n0, pltpu.roll(dconv, tm - 2, 0)))
        dh = w2 * dconv + w1 * d1 + w0 * d2
        dp_ref[0] = (dh * cb).astype(BF16)
        dp_ref[1] = (dh * xb).astype(BF16)
        next_ref[...] = dconv[0:8, :]

    steps = tm // HALO
    return pl.pallas_call(
        body, name="mixer_b_backward", grid=(D_MODEL // tc, n),
        in_specs=[pl.BlockSpec((4, tm, tc), lambda j, ii: (BLOCK_B, n - 1 - ii, j)),
                  pl.BlockSpec((4, HALO, tc), lambda j, ii: (BLOCK_B, jnp.maximum((n - 1 - ii) * steps - 1, 0), j)),
                  pl.BlockSpec((None, tm, tc), lambda j, ii: (1, n - 1 - ii, j)),
                  pl.BlockSpec((3, tc), lambda j, ii: (0, j)),
                  pl.BlockSpec((1, tc), lambda j, ii: (0, j)), ANY],
        out_specs=(pl.BlockSpec((4, tm, tc), lambda j, ii: (BLOCK_B, n - 1 - ii, j)),
                   pl.BlockSpec((3, tc), lambda j, ii: (0, j)),
                   pl.BlockSpec((1, tc), lambda j, ii: (0, j))),
        out_shape=(jax.ShapeDtypeStruct(dp.shape, BF16),
                   jax.ShapeDtypeStruct((3, D_MODEL), F32), jax.ShapeDtypeStruct((1, D_MODEL), F32)),
        scratch_shapes=[pltpu.VMEM((8, tc), F32)],
        input_output_aliases={5: 0},
        compiler_params=_params("parallel", "arbitrary"),
    )(p_b, p_b, dyab, conv_w, conv_b, dp)


def _adam_math(w, g, m, v):
    m = ADAM_B1 * m + (1.0 - ADAM_B1) * g
    v = ADAM_B2 * v + (1.0 - ADAM_B2) * (g * g)
    delta = -ADAM_LR * ((m * ADAM_C1) / (jnp.sqrt(v * ADAM_C2) + ADAM_EPS) + ADAM_WD * w)
    return delta, m, v


def _adam_rows(w, g, m, v, tm, name):
    r, c = w.shape

    def body(w_ref, g_ref, m_ref, v_ref, go_ref, d_ref, mo_ref, vo_ref):
        g = g_ref[...]
        d, mn, vn = _adam_math(w_ref[...], g, m_ref[...], v_ref[...])
        go_ref[...] = g
        d_ref[...] = d
        mo_ref[...] = mn
        vo_ref[...] = vn

    spec = pl.BlockSpec((tm, c), lambda i: (i, 0))
    shape = jax.ShapeDtypeStruct((r, c), F32)
    return pl.pallas_call(
        body, name=name, grid=(r // tm,),
        in_specs=[spec] * 4, out_specs=(spec,) * 4, out_shape=(shape,) * 4,
        compiler_params=_params("parallel"),
    )(w, g, m, v)


def _adam_small(gathered, w, m, v):
    r = w.shape[0]

    def body(ga_ref, w_ref, m_ref, v_ref, g_ref, d_ref, mo_ref, vo_ref):
        g = ga_ref[0]
        for k in range(1, 8):
            g = g + ga_ref[k]
        d, mn, vn = _adam_math(w_ref[...], g, m_ref[...], v_ref[...])
        g_ref[...] = g
        d_ref[...] = d
        mo_ref[...] = mn
        vo_ref[...] = vn

    shape = jax.ShapeDtypeStruct((r, 128), F32)
    return pl.pallas_call(
        body, name="adam_small", out_shape=(shape,) * 4, compiler_params=_params(),
    )(gathered, w, m, v)


def _pack_small(parts):
    rows = []
    for name, r in SMALL_ROWS:
        a = parts[name].astype(F32).reshape(-1)
        pad = r * 128 - a.shape[0]
        if pad:
            a = jnp.concatenate([a, jnp.zeros((pad,), F32)])
        rows.append(a.reshape(r, 128))
    return jnp.concatenate(rows, axis=0)


def _unpack_small(buf, shapes):
    out, r0 = {}, 0
    for name, r in SMALL_ROWS:
        shp = shapes[name]
        n = math.prod(shp)
        out[name] = buf[r0:r0 + r].reshape(-1)[:n].reshape(shp)
        r0 += r
    return out


def kernel(x, w_in, b_gate, ln_v_g, ln_v_b, w_s, b_s, conv_w, conv_b, w_oa, w_ob, w_out, ln_g, ln_b, loss_target, m_w_in, m_b_gate, m_ln_v_g, m_ln_v_b, m_w_s, m_b_s, m_conv_w, m_conv_b, m_w_oa, m_w_ob, m_w_out, m_ln_g, m_ln_b, v_w_in, v_b_gate, v_ln_v_g, v_ln_v_b, v_w_s, v_b_s, v_conv_w, v_conv_b, v_w_oa, v_w_ob, v_w_out, v_ln_g, v_ln_b):
    t = x.shape[1]
    x2 = x[0]
    target = loss_target[0]
    chip = 2 * lax.axis_index("x") + lax.axis_index("y")
    conv_cols = conv_w.shape[2]

    w_pre = _cast_into_columns(w_in[0], chip.astype(jnp.int32).reshape(1), N_CHIPS, 256, "cast_w_in")
    wo_b, wm = _prep_small_weights(w_oa[0], w_ob[0], w_out[0], w_s[0])
    conv_w8 = jnp.concatenate([conv_w[0], jnp.zeros((5, conv_cols), F32)], axis=0)
    bs_col = b_s[0].reshape(N_HEADS, CHUNK, 1)
    bg = b_gate.reshape(2, 1, D_MODEL)

    xb = _cast_rows(x2, min(512, t), "cast_x")
    p, w_full, wo_full, cw_full = _gather_and_project(xb, w_pre, wo_b, conv_w8, min(1024, t))
    wo3 = wo_full.reshape(3, D_MODEL, D_MODEL)
    conv_w_all = jnp.transpose(cw_full[:, :3, :], (1, 0, 2)).reshape(3, D_MODEL)
    tm_a = min(512, t)
    ya = _mixer_a_forward(p, wm, bs_col, ln_v_g, ln_v_b, tm_a)
    tm_b = min(256, t)
    yb = _mixer_b_forward(p, conv_w_all, conv_b, tm_b, 256)
    tm_m = min(512, t)
    merged, oab = _merge_forward(ya, yb, wo3, p, bg, tm_m, 512)

    drb, gx, dmerged, d_ln_g, d_ln_b, loss_part = _head(merged, wo3, x2, target, ln_g, ln_b, min(256, t))
    doab, dp, d_bg = _gate_backward(dmerged, oab, p, bg, tm_m, 512)
    dyab = _branch_backward(doab, wo3, tm_m, 512)
    dp, d_ws, d_bs, d_lnv_g, d_lnv_b = _mixer_a_backward(p, dyab, wm, bs_col, ln_v_g, ln_v_b, dp, tm_a)
    dp, d_cw, d_cb = _mixer_b_backward(p, dyab, conv_w_all, conv_b, dp, tm_b, 256)

    tk = min(2048, t)
    g_oa = _weight_grad(ya, 0, doab, 0, 1024, 1024, tk, "grad_w_oa")
    g_ob = _weight_grad(yb, 0, doab, 1, 1024, 1024, tk, "grad_w_ob")
    g_out = _weight_grad(merged, 0, drb, 0, 1024, 1024, tk, "grad_w_out")
    g_in = _win_grad(xb, dp, 1024, min(1024, t))

    r_in, *r_o = _exchange_halves(g_in, (g_oa, g_ob, g_out))
    s_in, s_o = _add_halves(g_in, (g_oa, g_ob, g_out), r_in, r_o)
    tm_x = min(512, t // 2)
    gx, q_in, q_o = _input_grad_and_scatter(dp, w_full, gx, s_in, s_o, tm_x)
    grad_x = _input_grad_tail(dp, w_full, gx, tm_x)
    f_in, f_o = _sum_chips(q_in, q_o, s_in, s_o)
    gsum_in, gsum_o = _share_halves(f_in, f_o)

    big = {}
    big["w_in"] = _adam_rows(w_in[0], gsum_in, m_w_in[0], v_w_in[0], 128, "adam_w_in")
    for n, (name, w, m, v) in enumerate((("w_oa", w_oa, m_w_oa, v_w_oa), ("w_ob", w_ob, m_w_ob, v_w_ob),
                                         ("w_out", w_out, m_w_out, v_w_out))):
        big[name] = _adam_rows(w[0], gsum_o[n], m[0], v[0], 256, "adam_" + name)

    small_grads = {"ln_g": d_ln_g, "ln_b": d_ln_b, "b_gate": d_bg, "ln_v_g": d_lnv_g, "ln_v_b": d_lnv_b,
                   "w_s": d_ws, "b_s": d_bs, "conv_w": d_cw, "conv_b": d_cb, "loss": loss_part[:, :1]}
    gathered = _gather_small(_pack_small(small_grads))

    def placed(a):
        return lax.dynamic_update_slice(jnp.zeros((3, D_MODEL), F32), a[0], (0, chip * conv_cols))

    def packed(pre, cw):
        z1 = jnp.zeros((1,), F32)
        return _pack_small({"ln_g": pre["ln_g"], "ln_b": pre["ln_b"], "b_gate": pre["b_gate"],
                            "ln_v_g": pre["ln_v_g"], "ln_v_b": pre["ln_v_b"], "w_s": pre["w_s"], "b_s": pre["b_s"],
                            "conv_w": placed(cw), "conv_b": pre["conv_b"], "loss": z1})

    ws = dict(ln_g=ln_g, ln_b=ln_b, b_gate=b_gate, ln_v_g=ln_v_g, ln_v_b=ln_v_b, w_s=w_s, b_s=b_s, conv_b=conv_b)
    ms = dict(ln_g=m_ln_g, ln_b=m_ln_b, b_gate=m_b_gate, ln_v_g=m_ln_v_g, ln_v_b=m_ln_v_b, w_s=m_w_s, b_s=m_b_s,
              conv_b=m_conv_b)
    vs = dict(ln_g=v_ln_g, ln_b=v_ln_b, b_gate=v_b_gate, ln_v_g=v_ln_v_g, ln_v_b=v_ln_v_b, w_s=v_w_s, b_s=v_b_s,
              conv_b=v_conv_b)
    s_g, s_d, s_m, s_v = _adam_small(gathered, packed(ws, conv_w), packed(ms, m_conv_w), packed(vs, v_conv_w))
    shapes = {"ln_g": ln_g.shape, "ln_b": ln_b.shape, "b_gate": b_gate.shape, "ln_v_g": ln_v_g.shape,
              "ln_v_b": ln_v_b.shape, "w_s": w_s.shape, "b_s": b_s.shape, "conv_w": (1, 3, D_MODEL),
              "conv_b": conv_b.shape, "loss": (1,)}
    small = [_unpack_small(b, shapes) for b in (s_g, s_d, s_m, s_v)]
    for d in small:
        d["conv_w"] = lax.dynamic_slice(d["conv_w"], (0, 0, chip * conv_cols), (1, 3, conv_cols))
    loss = small[0]["loss"][0]

    order = ("w_in", "b_gate", "ln_v_g", "ln_v_b", "w_s", "b_s", "conv_w", "conv_b", "w_oa", "w_ob", "w_out",
             "ln_g", "ln_b")
    outs = [loss, grad_x[None]]
    for which in range(4):
        for name in order:
            outs.append(big[name][which][None] if name in big else small[which][name])
    return tuple(outs)
```

```python
import functools
import math

import jax
import jax.numpy as jnp
from jax import lax
from jax.experimental import pallas as pl
from jax.experimental.pallas import tpu as pltpu

F32 = jnp.float32
BF16 = jnp.bfloat16

D_MODEL = 2048
N_HEADS = 8
HEAD_DIM = D_MODEL // N_HEADS
CHUNK = 128
N_SEG = 9
N_CHIPS = 4
SHARD_COLS = N_SEG * D_MODEL // N_CHIPS
COL_BLOCK = 512
BLOCKS_PER_SHARD = SHARD_COLS // COL_BLOCK
BLOCKS_PER_SEG = D_MODEL // COL_BLOCK
SHARD_ROWS = D_MODEL // N_CHIPS
N_SLOTS = 12
BLOCK_A, BLOCK_G, BLOCK_B = 0, 2, 2
SLOT_OF_SEG = (0, 1, 2, 8, 9, 10, 11, 4, 5)
DN_ALPHA = 2.0 ** 0.25
LN_EPS = 1e-5
GELU_K = math.sqrt(2.0 / math.pi)
GELU_C = 0.044715

ADAM_LR = 0.001
ADAM_B1 = 0.9
ADAM_B2 = 0.999
ADAM_EPS = 1e-08
ADAM_WD = 0.01
ADAM_STEP = 10
ADAM_C1 = 1.0 / (1.0 - ADAM_B1 ** ADAM_STEP)
ADAM_C2 = 1.0 / (1.0 - ADAM_B2 ** ADAM_STEP)

VMEM_LIMIT = 56 * 1024 * 1024
MESH = pl.DeviceIdType.MESH
ANY = pl.BlockSpec(memory_space=pl.ANY)

SMALL_ROWS = (("ln_g", 16), ("ln_b", 16), ("b_gate", 32), ("ln_v_g", 16), ("ln_v_b", 16),
              ("w_s", 1024), ("b_s", 8), ("conv_w", 48), ("conv_b", 16), ("loss", 8))
SMALL_TOTAL = sum(r for _, r in SMALL_ROWS)


def _params(*sem):
    return pltpu.CompilerParams(dimension_semantics=sem, vmem_limit_bytes=VMEM_LIMIT)


def _sigmoid(x):
    return 1.0 / (1.0 + jnp.exp(-x))


def _gelu_and_grad(x):
    x2 = x * x
    th = jnp.tanh(GELU_K * (x + GELU_C * x * x2))
    g = 0.5 * x * (1.0 + th)
    dg = 0.5 * (1.0 + th) + 0.5 * x * (1.0 - th * th) * (GELU_K * (1.0 + 3.0 * GELU_C * x2))
    return g, dg


def _gelu(x):
    return 0.5 * x * (1.0 + jnp.tanh(GELU_K * (x + GELU_C * x * x * x)))


def _dot(a, b, ca=1, cb=0):
    return lax.dot_general(a, b, (((ca,), (cb,)), ((), ())), preferred_element_type=F32)


def _cast_rows(x, tm, name):
    r, c = x.shape

    def body(x_ref, o_ref):
        o_ref[...] = x_ref[...].astype(BF16)

    return pl.pallas_call(
        body, name=name, grid=(r // tm,),
        in_specs=[pl.BlockSpec((tm, c), lambda i: (i, 0))],
        out_specs=pl.BlockSpec((tm, c), lambda i: (i, 0)),
        out_shape=jax.ShapeDtypeStruct((r, c), BF16),
        compiler_params=_params("parallel"),
    )(x)


def _cast_into_columns(w, slot, n_slots, tm, name):
    r, c = w.shape

    def body(s_ref, w_ref, o_ref):
        o_ref[...] = w_ref[...].astype(BF16)

    return pl.pallas_call(
        body, name=name,
        grid_spec=pltpu.PrefetchScalarGridSpec(
            num_scalar_prefetch=1, grid=(r // tm,),
            in_specs=[pl.BlockSpec((tm, c), lambda i, s_ref: (i, 0))],
            out_specs=pl.BlockSpec((tm, c), lambda i, s_ref: (i, s_ref[0]))),
        out_shape=jax.ShapeDtypeStruct((r, n_slots * c), BF16),
        compiler_params=_params("parallel"),
    )(slot, w)


def _prep_small_weights(w_oa, w_ob, w_out, w_s):
    rows = w_oa.shape[0]

    def body(a_ref, b_ref, c_ref, ws_ref, wo_ref, wm_ref):
        wo_ref[0] = a_ref[...].astype(BF16)
        wo_ref[1] = b_ref[...].astype(BF16)
        wo_ref[2] = c_ref[...].astype(BF16)
        t = lax.broadcasted_iota(jnp.int32, (CHUNK, CHUNK), 0)
        s = lax.broadcasted_iota(jnp.int32, (CHUNK, CHUNK), 1)
        for h in range(N_HEADS):
            wm_ref[h] = jnp.where(s <= t, ws_ref[h], 0.0).astype(BF16)

    return pl.pallas_call(
        body, name="prep_small_weights",
        out_shape=(jax.ShapeDtypeStruct((3, rows, D_MODEL), BF16),
                   jax.ShapeDtypeStruct((N_HEADS, CHUNK, CHUNK), BF16)),
        compiler_params=_params(),
    )(w_oa, w_ob, w_out, w_s)


def _mesh_pos():
    x, y, c = lax.axis_index("x"), lax.axis_index("y"), lax.axis_index("c")
    chips = [(1 - x, y), (x, 1 - y), (1 - x, 1 - y)]
    return x, y, c, chips


def _slot_of_seg(seg):
    return jnp.where(seg < 3, seg, jnp.where(seg < 7, seg + 5, seg - 3))


def _gather_and_project(xb, w_pre, wo_b, conv_w8, tm):
    t = xb.shape[0]
    d, sc = w_pre.shape[0], w_pre.shape[1] // N_CHIPS
    rows = wo_b.shape[1]
    hd, hr = d // 2, rows // 2
    ni, nj = t // tm, BLOCKS_PER_SHARD
    total = N_CHIPS * ni * nj
    mx, my = lax.axis_index("x"), lax.axis_index("y")
    order = jnp.stack([2 * mx + my, 2 * (1 - mx) + my, 2 * mx + (1 - my),
                       2 * (1 - mx) + (1 - my)]).astype(jnp.int32)

    def body(order_ref, x_ref, wpre_ref, wo_ref, cw_ref, p_ref, wf_ref, wof_ref, cwf_ref,
             wbuf, wsem, send_sems, recv_sems, local_sems):
        x, y, c, chips = _mesh_pos()
        me = 2 * x + y
        sibling = (x, y, 1 - c)
        s, i, j = pl.program_id(0), pl.program_id(1), pl.program_id(2)
        n = (s * ni + i) * nj + j

        def shard_cols(k):
            return pl.ds(pl.multiple_of(k * sc, COL_BLOCK), sc)

        def half(kind, k, cc):
            if kind == 0:
                return wf_ref.at[pl.ds(cc * hd, hd), shard_cols(k)]
            return wof_ref.at[:, k, pl.ds(cc * hr, hr), :]

        def src_half(kind, cc):
            if kind == 0:
                return wpre_ref.at[pl.ds(cc * hd, hd), shard_cols(me)]
            return wo_ref.at[:, pl.ds(cc * hr, hr), :]

        def rcopy(sem, src, dst, to):
            return pltpu.make_async_remote_copy(src_ref=src, dst_ref=dst, send_sem=send_sems.at[sem],
                                                recv_sem=recv_sems.at[sem], device_id=to, device_id_type=MESH)

        def ici_send(kind, q):
            return rcopy(kind * 6 + q, src_half(kind, c), half(kind, me, c), (*chips[q], c))

        def ici_landed(kind, q):
            k = 2 * chips[q][0] + chips[q][1]
            return rcopy(kind * 6 + q, src_half(kind, c), half(kind, k, c), sibling)

        def forward(kind, q, cc):
            k = 2 * chips[q][0] + chips[q][1]
            return rcopy(kind * 6 + 3 + q, half(kind, k, cc), half(kind, k, cc), sibling)

        def conv_send(q):
            return rcopy(12 + q, cw_ref, cwf_ref.at[me], (*chips[q], c))

        def local_copies():
            return [pltpu.make_async_copy(wo_ref, wof_ref.at[:, me], local_sems.at[0]),
                    pltpu.make_async_copy(cw_ref, cwf_ref.at[me], local_sems.at[1])]

        def tile_start(s_, j_, slot):
            g = order_ref[s_] * BLOCKS_PER_SHARD + j_
            cols = pl.ds(pl.multiple_of(g * COL_BLOCK, COL_BLOCK), COL_BLOCK)

            @pl.when(s_ == 0)
            def _():
                pltpu.make_async_copy(wpre_ref.at[:, cols], wbuf.at[slot], wsem.at[slot]).start()

            @pl.when(s_ > 0)
            def _():
                pltpu.make_async_copy(wf_ref.at[:, cols], wbuf.at[slot], wsem.at[slot]).start()

        def at(s_, i_, j_):
            return jnp.logical_and(s == s_, jnp.logical_and(i == i_, j == j_))

        @pl.when(n == 0)
        def _():
            for cp in local_copies():
                cp.start()
            for q in (0, 1):
                ici_send(0, q).start()
            for q in range(3):
                conv_send(q).start()
            tile_start(0, 0, 0)

        @pl.when(at(0, ni - 1, nj - 1))
        def _():
            for q in (0, 1):
                ici_landed(0, q).wait_recv()
                forward(0, q, c).start()
            for q in (0, 1):
                ici_send(0, q).wait_send()
            ici_send(0, 2).start()
            forward(0, 0, 1 - c).wait_recv()

        @pl.when(at(1, ni - 1, nj - 1))
        def _():
            forward(0, 1, 1 - c).wait_recv()

        @pl.when(at(2, (3 * ni) // 4, 0))
        def _():
            ici_landed(0, 2).wait_recv()
            forward(0, 2, c).start()
            ici_send(0, 2).wait_send()
            for q in range(3):
                ici_send(1, q).start()

        @pl.when(at(2, ni - 1, nj - 1))
        def _():
            forward(0, 2, 1 - c).wait_recv()

        n1 = n + 1

        @pl.when(n1 < total)
        def _():
            tile_start(n1 // (ni * nj), lax.rem(n1, nj), lax.rem(n1, 2))

        slot = lax.rem(n, 2)
        pltpu.make_async_copy(wpre_ref.at[:, pl.ds(0, COL_BLOCK)], wbuf.at[slot], wsem.at[slot]).wait()
        p_ref[...] = _dot(x_ref[...], wbuf[slot]).astype(BF16)

        @pl.when(n == total - 1)
        def _():
            for q in range(3):
                ici_landed(1, q).wait_recv()
                forward(1, q, c).start()
            for q in range(3):
                forward(1, q, 1 - c).wait_recv()
                rcopy(12 + q, cw_ref, cwf_ref.at[2 * chips[q][0] + chips[q][1]], sibling).wait_recv()
            for q in range(3):
                ici_send(1, q).wait_send()
                forward(0, q, c).wait_send()
                forward(1, q, c).wait_send()
                conv_send(q).wait_send()
            for cp in local_copies():
                cp.wait()

    def p_map(s, i, j, o):
        g = o[s] * BLOCKS_PER_SHARD + j
        return (_slot_of_seg(g // BLOCKS_PER_SEG), i, lax.rem(g, BLOCKS_PER_SEG))

    return pl.pallas_call(
        body, name="gather_and_project",
        grid_spec=pltpu.PrefetchScalarGridSpec(
            num_scalar_prefetch=1, grid=(N_CHIPS, ni, nj),
            in_specs=[pl.BlockSpec((tm, D_MODEL), lambda s, i, j, o: (i, 0)), ANY, ANY, ANY],
            out_specs=(pl.BlockSpec((None, tm, COL_BLOCK), p_map), ANY, ANY, ANY),
            scratch_shapes=[pltpu.VMEM((2, d, COL_BLOCK), BF16), pltpu.SemaphoreType.DMA((2,)),
                            pltpu.SemaphoreType.DMA((15,)), pltpu.SemaphoreType.DMA((15,)),
                            pltpu.SemaphoreType.DMA((2,))]),
        out_shape=(jax.ShapeDtypeStruct((N_SLOTS, t, D_MODEL), BF16),
                   jax.ShapeDtypeStruct((d, N_CHIPS * sc), BF16),
                   jax.ShapeDtypeStruct((3, N_CHIPS, rows, D_MODEL), BF16),
                   jax.ShapeDtypeStruct((N_CHIPS,) + conv_w8.shape, F32)),
        input_output_aliases={2: 1},
        compiler_params=pltpu.CompilerParams(dimension_semantics=("arbitrary",) * 3, vmem_limit_bytes=VMEM_LIMIT,
                                             has_side_effects=True),
    )(order, xb, w_pre, wo_b, conv_w8)


def _exchange_halves(g_in, g_o):
    d, c9 = g_in.shape
    hd = d // 2
    hr = SHARD_ROWS // 2
    g_o4 = [g.reshape(N_CHIPS, 2, hr, D_MODEL) for g in g_o]

    def body(gi_ref, ga_ref, gb_ref, gc_ref, ri_ref, ra_ref, rb_ref, rc_ref, send_sems, recv_sems):
        x, y, c, _ = _mesh_pos()
        sibling = (x, y, 1 - c)
        cps = [pltpu.make_async_remote_copy(src_ref=gi_ref.at[pl.ds((1 - c) * hd, hd), :], dst_ref=ri_ref,
                                            send_sem=send_sems.at[0], recv_sem=recv_sems.at[0],
                                            device_id=sibling, device_id_type=MESH)]
        for n, (g_ref, r_ref) in enumerate(((ga_ref, ra_ref), (gb_ref, rb_ref), (gc_ref, rc_ref))):
            cps.append(pltpu.make_async_remote_copy(src_ref=g_ref.at[:, 1 - c], dst_ref=r_ref,
                                                    send_sem=send_sems.at[1 + n], recv_sem=recv_sems.at[1 + n],
                                                    device_id=sibling, device_id_type=MESH))
        for cp in cps:
            cp.start()
        for cp in cps:
            cp.wait()

    o_shape = jax.ShapeDtypeStruct((N_CHIPS, hr, D_MODEL), BF16)
    return pl.pallas_call(
        body, name="rs_exchange_halves",
        in_specs=[ANY] * 4, out_specs=(ANY,) * 4,
        out_shape=(jax.ShapeDtypeStruct((hd, c9), BF16), o_shape, o_shape, o_shape),
        scratch_shapes=[pltpu.SemaphoreType.DMA((4,)), pltpu.SemaphoreType.DMA((4,))],
        compiler_params=pltpu.CompilerParams(has_side_effects=True),
    )(g_in, *g_o4)


def _add_halves(g_in, g_o, r_in, r_o):
    d, c9 = g_in.shape
    hd = d // 2
    hr = SHARD_ROWS // 2
    core = lax.axis_index("c").astype(jnp.int32).reshape(1)
    tm = min(512, hd)
    nb = hd // tm

    def body_in(c_ref, g_ref, r_ref, o_ref):
        o_ref[...] = (g_ref[...].astype(F32) + r_ref[...].astype(F32)).astype(BF16)

    s_in = pl.pallas_call(
        body_in, name="rs_add_halves_in",
        grid_spec=pltpu.PrefetchScalarGridSpec(
            num_scalar_prefetch=1, grid=(N_CHIPS, nb),
            in_specs=[pl.BlockSpec((tm, SHARD_COLS), lambda k, i, c_ref: (c_ref[0] * nb + i, k)),
                      pl.BlockSpec((tm, SHARD_COLS), lambda k, i, c_ref: (i, k))],
            out_specs=pl.BlockSpec((None, tm, SHARD_COLS), lambda k, i, c_ref: (k, i, 0))),
        out_shape=jax.ShapeDtypeStruct((N_CHIPS, hd, SHARD_COLS), BF16),
        compiler_params=_params("parallel", "parallel"),
    )(core, g_in, r_in)

    g_o4 = [g.reshape(N_CHIPS, 2, hr, D_MODEL) for g in g_o]

    def body_o(c_ref, ga_ref, gb_ref, gc_ref, ra_ref, rb_ref, rc_ref, o_ref):
        for n, (g_ref, r_ref) in enumerate(((ga_ref, ra_ref), (gb_ref, rb_ref), (gc_ref, rc_ref))):
            o_ref[n] = (g_ref[...].astype(F32) + r_ref[...].astype(F32)).astype(BF16)

    gspec = pl.BlockSpec((None, None, hr, D_MODEL), lambda k, c_ref: (k, c_ref[0], 0, 0))
    rspec = pl.BlockSpec((None, hr, D_MODEL), lambda k, c_ref: (k, 0, 0))
    s_o = pl.pallas_call(
        body_o, name="rs_add_halves_o",
        grid_spec=pltpu.PrefetchScalarGridSpec(
            num_scalar_prefetch=1, grid=(N_CHIPS,),
            in_specs=[gspec] * 3 + [rspec] * 3,
            out_specs=pl.BlockSpec((3, None, hr, D_MODEL), lambda k, c_ref: (0, k, 0, 0))),
        out_shape=jax.ShapeDtypeStruct((3, N_CHIPS, hr, D_MODEL), BF16),
        compiler_params=_params("parallel"),
    )(core, *g_o4, *r_o)
    return s_in, s_o


def _sum_chips(r_in, r_o, s_in, s_o):
    _, hd, sc = r_in.shape
    hr = r_o.shape[2]
    tm = min(256, hd)
    nb = hd // tm
    pos = jnp.stack([2 * lax.axis_index("x") + lax.axis_index("y"), lax.axis_index("c")]).astype(jnp.int32)

    def chip_sum(pos_ref, r_ref, s_ref):
        acc = None
        for k in range(N_CHIPS):
            term = jnp.where(pos_ref[0] == k, s_ref[...], r_ref[k]).astype(F32)
            acc = term if acc is None else acc + term
        return acc

    def body_in(pos_ref, r_ref, s_ref, o_ref):
        o_ref[...] = chip_sum(pos_ref, r_ref, s_ref)

    f_in = pl.pallas_call(
        body_in, name="rs_sum_chips_in",
        grid_spec=pltpu.PrefetchScalarGridSpec(
            num_scalar_prefetch=1, grid=(nb,),
            in_specs=[pl.BlockSpec((N_CHIPS, tm, sc), lambda i, p: (0, i, 0)),
                      pl.BlockSpec((None, tm, sc), lambda i, p: (p[0], i, 0))],
            out_specs=pl.BlockSpec((tm, sc), lambda i, p: (p[1] * nb + i, 0))),
        out_shape=jax.ShapeDtypeStruct((2 * hd, sc), F32),
        compiler_params=_params("parallel"),
    )(pos, r_in, s_in)

    def body_o(pos_ref, r_ref, s_ref, o_ref):
        o_ref[...] = chip_sum(pos_ref, r_ref, s_ref)

    f_o = pl.pallas_call(
        body_o, name="rs_sum_chips_o",
        grid_spec=pltpu.PrefetchScalarGridSpec(
            num_scalar_prefetch=1, grid=(3,),
            in_specs=[pl.BlockSpec((N_CHIPS, None, hr, D_MODEL), lambda n, p: (0, n, 0, 0)),
                      pl.BlockSpec((None, None, hr, D_MODEL), lambda n, p: (n, p[0], 0, 0))],
            out_specs=pl.BlockSpec((None, hr, D_MODEL), lambda n, p: (n, p[1], 0))),
        out_shape=jax.ShapeDtypeStruct((3, 2 * hr, D_MODEL), F32),
        compiler_params=_params("parallel"),
    )(pos, r_o, s_o)
    return f_in, f_o


def _share_halves(f_in, f_o):
    hd, sc = f_in.shape[0] // 2, f_in.shape[1]
    hr = f_o.shape[1] // 2

    def body(fi_ref, fo_ref, gi_ref, go_ref, send_sems, recv_sems):
        x, y, c, _ = _mesh_pos()
        sibling = (x, y, 1 - c)

        def halves(cc):
            rows_i, rows_o = pl.ds(cc * hd, hd), pl.ds(cc * hr, hr)
            return (fi_ref.at[rows_i, :], gi_ref.at[rows_i, :]), (fo_ref.at[:, rows_o, :], go_ref.at[:, rows_o, :])

        def copies(cc):
            return [pltpu.make_async_remote_copy(src_ref=src, dst_ref=dst, send_sem=send_sems.at[n],
                                                 recv_sem=recv_sems.at[n], device_id=sibling, device_id_type=MESH)
                    for n, (src, dst) in enumerate(halves(cc))]

        sends = copies(c)
        for cp in sends:
            cp.start()
        for cp in copies(1 - c):
            cp.wait_recv()
        for cp in sends:
            cp.wait_send()

    return pl.pallas_call(
        body, name="rs_share_halves",
        in_specs=[ANY, ANY], out_specs=(ANY, ANY),
        out_shape=(jax.ShapeDtypeStruct(f_in.shape, F32), jax.ShapeDtypeStruct(f_o.shape, F32)),
        scratch_shapes=[pltpu.SemaphoreType.DMA((2,)), pltpu.SemaphoreType.DMA((2,))],
        input_output_aliases={0: 0, 1: 1},
        compiler_params=pltpu.CompilerParams(has_side_effects=True),
    )(f_in, f_o)


def _merge_forward(ya, yb, wo3, p, bg, tm, tn):
    t = ya.shape[0]

    def body(ya_ref, yb_ref, wa_ref, wb_ref, g_ref, bg_ref, m_ref, oab_ref):
        oa = _dot(ya_ref[...], wa_ref[...])
        ob = _dot(yb_ref[...], wb_ref[...])
        ga = _sigmoid(g_ref[0].astype(F32) + bg_ref[0])
        gb = _sigmoid(g_ref[1].astype(F32) + bg_ref[1])
        m_ref[...] = (ga * oa + gb * ob).astype(BF16)
        oab_ref[0] = oa.astype(BF16)
        oab_ref[1] = ob.astype(BF16)

    return pl.pallas_call(
        body, name="merge_forward", grid=(t // tm, D_MODEL // tn),
        in_specs=[pl.BlockSpec((tm, D_MODEL), lambda i, j: (i, 0)),
                  pl.BlockSpec((tm, D_MODEL), lambda i, j: (i, 0)),
                  pl.BlockSpec((None, D_MODEL, tn), lambda i, j: (0, 0, j)),
                  pl.BlockSpec((None, D_MODEL, tn), lambda i, j: (1, 0, j)),
                  pl.BlockSpec((2, tm, tn), lambda i, j: (BLOCK_G, i, j)),
                  pl.BlockSpec((2, 1, tn), lambda i, j: (0, 0, j))],
        out_specs=(pl.BlockSpec((tm, tn), lambda i, j: (i, j)),
                   pl.BlockSpec((2, tm, tn), lambda i, j: (0, i, j))),
        out_shape=(jax.ShapeDtypeStruct((t, D_MODEL), BF16), jax.ShapeDtypeStruct((2, t, D_MODEL), BF16)),
        compiler_params=_params("parallel", "parallel"),
    )(ya, yb, wo3, wo3, p, bg)


def _head(merged, wo3, x, target, ln_g, ln_b, tm):
    t = x.shape[0]
    inv_d = 1.0 / D_MODEL

    def body(m_ref, w_ref, x_ref, t_ref, g_ref, b_ref, dr_ref, gx_ref, dm_ref, dg_ref, db_ref, loss_ref):
        i = pl.program_id(0)

        @pl.when(i == 0)
        def _():
            dg_ref[...] = jnp.zeros_like(dg_ref)
            db_ref[...] = jnp.zeros_like(db_ref)
            loss_ref[...] = jnp.zeros_like(loss_ref)

        w = w_ref[...]
        out = _dot(m_ref[...], w)
        r = DN_ALPHA * x_ref[...] + out
        mu = jnp.mean(r, axis=-1, keepdims=True)
        xc = r - mu
        var = jnp.mean(xc * xc, axis=-1, keepdims=True)
        rstd = lax.rsqrt(var + LN_EPS)
        xhat = xc * rstd
        g = g_ref[...]
        e = xhat * g + b_ref[...] - t_ref[...]
        se = jnp.sum(jnp.sum(e * e, axis=1, keepdims=True), axis=0, keepdims=True)
        loss_ref[...] += jnp.broadcast_to((0.5 * inv_d) * se, loss_ref.shape)
        dy = e * inv_d
        db_ref[...] += jnp.sum(dy, axis=0, keepdims=True)
        dg_ref[...] += jnp.sum(dy * xhat, axis=0, keepdims=True)
        dxh = dy * g
        m1 = jnp.mean(dxh, axis=-1, keepdims=True)
        m2 = jnp.mean(dxh * xhat, axis=-1, keepdims=True)
        dr = rstd * (dxh - m1 - xhat * m2)
        gx_ref[...] = DN_ALPHA * dr
        drb = dr.astype(BF16)
        dr_ref[...] = drb
        dm_ref[...] = _dot(drb, w, 1, 1).astype(BF16)

    row = pl.BlockSpec((tm, D_MODEL), lambda i: (i, 0))
    vec = pl.BlockSpec((1, D_MODEL), lambda i: (0, 0))
    return pl.pallas_call(
        body, name="head", grid=(t // tm,),
        in_specs=[row, pl.BlockSpec((None, D_MODEL, D_MODEL), lambda i: (2, 0, 0)), row, row, vec, vec],
        out_specs=(row, row, row, vec, vec, pl.BlockSpec((1, 128), lambda i: (0, 0))),
        out_shape=(jax.ShapeDtypeStruct((t, D_MODEL), BF16), jax.ShapeDtypeStruct((t, D_MODEL), F32),
                   jax.ShapeDtypeStruct((t, D_MODEL), BF16), jax.ShapeDtypeStruct((1, D_MODEL), F32),
                   jax.ShapeDtypeStruct((1, D_MODEL), F32), jax.ShapeDtypeStruct((1, 128), F32)),
        compiler_params=_params("arbitrary"),
    )(merged, wo3, x, target, ln_g, ln_b)


def _gate_backward(dmerged, oab, p, bg, tm, tn):
    t = dmerged.shape[0]

    def body(dm_ref, oab_ref, g_ref, bg_ref, do_ref, dpg_ref, dbg_ref):
        @pl.when(pl.program_id(1) == 0)
        def _():
            dbg_ref[...] = jnp.zeros_like(dbg_ref)

        dm = dm_ref[...].astype(F32)
        for n in range(2):
            gate = _sigmoid(g_ref[n].astype(F32) + bg_ref[n])
            do_ref[n] = (dm * gate).astype(BF16)
            dgate = dm * oab_ref[n].astype(F32) * gate * (1.0 - gate)
            dpg_ref[n] = dgate.astype(BF16)
            dbg_ref[n] += jnp.sum(dgate, axis=0, keepdims=True)

    pair = pl.BlockSpec((2, tm, tn), lambda j, i: (0, i, j))
    gates = pl.BlockSpec((2, tm, tn), lambda j, i: (BLOCK_G, i, j))
    vec = pl.BlockSpec((2, 1, tn), lambda j, i: (0, 0, j))
    return pl.pallas_call(
        body, name="gate_backward", grid=(D_MODEL // tn, t // tm),
        in_specs=[pl.BlockSpec((tm, tn), lambda j, i: (i, j)), pair, gates, vec],
        out_specs=(pair, gates, vec),
        out_shape=(jax.ShapeDtypeStruct((2, t, D_MODEL), BF16), jax.ShapeDtypeStruct((N_SLOTS, t, D_MODEL), BF16),
                   jax.ShapeDtypeStruct((2, 1, D_MODEL), F32)),
        compiler_params=_params("parallel", "arbitrary"),
    )(dmerged, oab, p, bg)


def _branch_backward(doab, wo3, tm, tn):
    t = doab.shape[1]

    def body(d_ref, wa_ref, wb_ref, o_ref):
        o_ref[0] = _dot(d_ref[0], wa_ref[...], 1, 1).astype(BF16)
        o_ref[1] = _dot(d_ref[1], wb_ref[...], 1, 1).astype(BF16)

    return pl.pallas_call(
        body, name="branch_backward", grid=(t // tm, D_MODEL // tn),
        in_specs=[pl.BlockSpec((2, tm, D_MODEL), lambda i, j: (0, i, 0)),
                  pl.BlockSpec((None, tn, D_MODEL), lambda i, j: (0, j, 0)),
                  pl.BlockSpec((None, tn, D_MODEL), lambda i, j: (1, j, 0))],
        out_specs=pl.BlockSpec((2, tm, tn), lambda i, j: (0, i, j)),
        out_shape=jax.ShapeDtypeStruct((2, t, D_MODEL), BF16),
        compiler_params=_params("parallel", "parallel"),
    )(doab, wo3, wo3)


def _weight_grad(a, a_sel, b, b_sel, tm, tn, tk, name):
    t = a.shape[-2]
    nk = t // tk

    def body(a_ref, b_ref, o_ref, acc_ref):
        k = pl.program_id(2)
        part = _dot(a_ref[...], b_ref[...], 0, 0)

        @pl.when(k == 0)
        def _():
            acc_ref[...] = part

        @pl.when(k > 0)
        def _():
            acc_ref[...] += part

        @pl.when(k == nk - 1)
        def _():
            o_ref[...] = acc_ref[...].astype(BF16)

    def spec(arr, sel, width, which):
        if arr.ndim == 2:
            return pl.BlockSpec((tk, width), lambda i, j, k: (k, (i, j)[which]))
        return pl.BlockSpec((None, tk, width), lambda i, j, k: (sel, k, (i, j)[which]))

    return pl.pallas_call(
        body, name=name, grid=(D_MODEL // tm, D_MODEL // tn, nk),
        in_specs=[spec(a, a_sel, tm, 0), spec(b, b_sel, tn, 1)],
        out_specs=pl.BlockSpec((tm, tn), lambda i, j, k: (i, j)),
        out_shape=jax.ShapeDtypeStruct((D_MODEL, D_MODEL), BF16),
        scratch_shapes=[pltpu.VMEM((tm, tn), F32)],
        compiler_params=_params("parallel", "parallel", "arbitrary"),
    )(a, b)


def _win_grad(xb, dp, tn, tk):
    _, t, _ = dp.shape
    nk = t // tk
    per_seg = D_MODEL // tn
    nj = N_SEG * per_seg

    def body(x_ref, dp_ref, o_ref, acc_ref):
        k = pl.program_id(1)
        part = _dot(x_ref[...], dp_ref[...], 0, 0)

        @pl.when(k == 0)
        def _():
            acc_ref[...] = part

        @pl.when(k > 0)
        def _():
            acc_ref[...] += part

        @pl.when(k == nk - 1)
        def _():
            o_ref[...] = acc_ref[...].astype(BF16)

    return pl.pallas_call(
        body, name="grad_w_in", grid=(nj, nk),
        in_specs=[pl.BlockSpec((tk, D_MODEL), lambda j, k: (k, 0)),
                  pl.BlockSpec((None, tk, tn), lambda j, k: (_slot_of_seg(j // per_seg), k, j % per_seg))],
        out_specs=pl.BlockSpec((D_MODEL, tn), lambda j, k: (0, j)),
        out_shape=jax.ShapeDtypeStruct((D_MODEL, N_SEG * D_MODEL), BF16),
        scratch_shapes=[pltpu.VMEM((D_MODEL, tn), F32)],
        compiler_params=_params("parallel", "arbitrary"),
    )(xb, dp)


def _input_grad_and_scatter(dp, w_full, gx, s_in, s_o, small, tm):
    _, t, _ = dp.shape
    ni, nk = t // tm, N_SEG - 1
    _, hd, sc = s_in.shape
    hr = s_o.shape[2]

    def body(dp_ref, w_ref, gx_ref, si_ref, so_ref, sm_ref, o_ref, ri_ref, ro_ref, ga_ref,
             send_sems, recv_sems, local_sem):
        i, k = pl.program_id(0), pl.program_id(1)
        x, y, c, chips = _mesh_pos()
        me = 2 * x + y
        dev = 4 * x + 2 * y + c

        def peer(r):
            return (x ^ ((r >> 2) & 1), y ^ ((r >> 1) & 1), c ^ (r & 1))

        def sends():
            cps = []
            for q, (cx, cy) in enumerate(chips):
                dest = 2 * cx + cy
                cps.append(pltpu.make_async_remote_copy(src_ref=si_ref.at[dest], dst_ref=ri_ref.at[me],
                                                        send_sem=send_sems.at[q], recv_sem=recv_sems.at[q],
                                                        device_id=(cx, cy, c), device_id_type=MESH))
                cps.append(pltpu.make_async_remote_copy(src_ref=so_ref.at[:, dest], dst_ref=ro_ref.at[me],
                                                        send_sem=send_sems.at[3 + q], recv_sem=recv_sems.at[3 + q],
                                                        device_id=(cx, cy, c), device_id_type=MESH))
            for r in range(1, 8):
                cps.append(pltpu.make_async_remote_copy(src_ref=sm_ref, dst_ref=ga_ref.at[dev],
                                                        send_sem=send_sems.at[5 + r], recv_sem=recv_sems.at[5 + r],
                                                        device_id=peer(r), device_id_type=MESH))
            return cps

        own_small = pltpu.make_async_copy(sm_ref, ga_ref.at[dev], local_sem)

        @pl.when(jnp.logical_and(i == 0, k == 0))
        def _():
            for cp in sends():
                cp.start()
            own_small.start()

        part = _dot(dp_ref[...], w_ref[...], 1, 1)

        @pl.when(k == 0)
        def _():
            o_ref[...] = gx_ref[...] + part

        @pl.when(k > 0)
        def _():
            o_ref[...] += part

        @pl.when(jnp.logical_and(i == ni - 1, k == nk - 1))
        def _():
            for q, (cx, cy) in enumerate(chips):
                frm = 2 * cx + cy
                pltpu.make_async_remote_copy(src_ref=si_ref.at[frm], dst_ref=ri_ref.at[frm], send_sem=send_sems.at[q],
                                             recv_sem=recv_sems.at[q], device_id=(x, y, c),
                                             device_id_type=MESH).wait_recv()
                pltpu.make_async_remote_copy(src_ref=so_ref.at[:, frm], dst_ref=ro_ref.at[frm],
                                             send_sem=send_sems.at[3 + q], recv_sem=recv_sems.at[3 + q],
                                             device_id=(x, y, c), device_id_type=MESH).wait_recv()
            for r in range(1, 8):
                px, py, pc = peer(r)
                pltpu.make_async_remote_copy(src_ref=sm_ref, dst_ref=ga_ref.at[4 * px + 2 * py + pc],
                                             send_sem=send_sems.at[5 + r], recv_sem=recv_sems.at[5 + r],
                                             device_id=(x, y, c), device_id_type=MESH).wait_recv()
            for cp in sends():
                cp.wait_send()
            own_small.wait()

    return pl.pallas_call(
        body, name="grad_x_and_scatter", grid=(ni, nk),
        in_specs=[pl.BlockSpec((None, tm, D_MODEL), lambda i, k: (_slot_of_seg(k), i, 0)),
                  pl.BlockSpec((D_MODEL, D_MODEL), lambda i, k: (0, k)),
                  pl.BlockSpec((tm, D_MODEL), lambda i, k: (i, 0)), ANY, ANY, ANY],
        out_specs=(pl.BlockSpec((tm, D_MODEL), lambda i, k: (i, 0)), ANY, ANY, ANY),
        out_shape=(jax.ShapeDtypeStruct((t, D_MODEL), F32),
                   jax.ShapeDtypeStruct((N_CHIPS, hd, sc), BF16),
                   jax.ShapeDtypeStruct((N_CHIPS, 3, hr, D_MODEL), BF16),
                   jax.ShapeDtypeStruct((8,) + small.shape, small.dtype)),
        scratch_shapes=[pltpu.SemaphoreType.DMA((13,)), pltpu.SemaphoreType.DMA((13,)), pltpu.SemaphoreType.DMA],
        input_output_aliases={2: 0},
        compiler_params=pltpu.CompilerParams(dimension_semantics=("arbitrary", "arbitrary"),
                                             vmem_limit_bytes=VMEM_LIMIT, has_side_effects=True),
    )(dp, w_full, gx, s_in, s_o, small)


def _input_grad_tail(dp, w_full, partial, tm):
    _, t, _ = dp.shape
    seg = N_SEG - 1

    def body(dp_ref, w_ref, part_ref, o_ref):
        o_ref[...] = part_ref[...] + _dot(dp_ref[...], w_ref[...], 1, 1)

    row = pl.BlockSpec((tm, D_MODEL), lambda i: (i, 0))
    return pl.pallas_call(
        body, name="grad_x_tail", grid=(t // tm,),
        in_specs=[pl.BlockSpec((None, tm, D_MODEL), lambda i: (SLOT_OF_SEG[seg], i, 0)),
                  pl.BlockSpec((D_MODEL, D_MODEL), lambda i: (0, seg)), row],
        out_specs=row,
        out_shape=jax.ShapeDtypeStruct((t, D_MODEL), F32),
        compiler_params=_params("parallel"),
    )(dp, w_full, partial)


def _sgu_chunk_forward(u, v, z, wm, bs, lng, lnb):
    ug, dug = _gelu_and_grad(u)
    vg, dvg = _gelu_and_grad(v)
    mu = jnp.mean(vg, axis=-1, keepdims=True)
    xc = vg - mu
    var = jnp.mean(xc * xc, axis=-1, keepdims=True)
    rstd = lax.rsqrt(var + LN_EPS)
    vhat = xc * rstd
    vln = (vhat * lng + lnb).astype(BF16)
    mixed = _dot(wm, vln) + bs
    sig = _sigmoid(z)
    return ug, dug, dvg, rstd, vhat, vln, mixed, sig


def _mixer_a_forward(p_a, wm, bs_col, ln_v_g, ln_v_b, tm):
    t = p_a.shape[1]

    def body(p_ref, wm_ref, bs_ref, g_ref, b_ref, o_ref):
        wm_v, bs_v, lng, lnb = wm_ref[...], bs_ref[...], g_ref[...], b_ref[...]

        def chunk(ci, carry):
            rows = pl.ds(pl.multiple_of(ci * CHUNK, CHUNK), CHUNK)
            u = p_ref[0, rows, :].astype(F32)
            v = p_ref[1, rows, :].astype(F32)
            z = p_ref[2, rows, :].astype(F32)
            ug, _, _, _, _, _, mixed, sig = _sgu_chunk_forward(u, v, z, wm_v, bs_v, lng, lnb)
            o_ref[rows, :] = (ug * mixed * (z * sig)).astype(BF16)
            return carry

        lax.fori_loop(0, tm // CHUNK, chunk, 0, unroll=True)

    return pl.pallas_call(
        body, name="mixer_a_forward", grid=(t // tm, N_HEADS),
        in_specs=[pl.BlockSpec((3, tm, HEAD_DIM), lambda i, h: (0, i, h)),
                  pl.BlockSpec((None, CHUNK, CHUNK), lambda i, h: (h, 0, 0)),
                  pl.BlockSpec((None, CHUNK, 1), lambda i, h: (h, 0, 0)),
                  pl.BlockSpec((1, HEAD_DIM), lambda i, h: (0, h)),
                  pl.BlockSpec((1, HEAD_DIM), lambda i, h: (0, h))],
        out_specs=pl.BlockSpec((tm, HEAD_DIM), lambda i, h: (i, h)),
        out_shape=jax.ShapeDtypeStruct((t, D_MODEL), BF16),
        compiler_params=_params("parallel", "parallel"),
    )(p_a, wm, bs_col, ln_v_g, ln_v_b)


def _mixer_a_backward(p_a, dyab, wm, bs_col, ln_v_g, ln_v_b, dp, tm):
    t = p_a.shape[1]

    def body(p_ref, dy_ref, wm_ref, bs_ref, g_ref, b_ref, dp_in, dp_ref, dws_ref, dbs_ref, dg_ref, db_ref):
        @pl.when(pl.program_id(1) == 0)
        def _():
            dws_ref[...] = jnp.zeros_like(dws_ref)
            dbs_ref[...] = jnp.zeros_like(dbs_ref)
            dg_ref[...] = jnp.zeros_like(dg_ref)
            db_ref[...] = jnp.zeros_like(db_ref)

        wm_v, bs_v, lng, lnb = wm_ref[...], bs_ref[...], g_ref[...], b_ref[...]
        causal = (lax.broadcasted_iota(jnp.int32, (CHUNK, CHUNK), 1)
                  <= lax.broadcasted_iota(jnp.int32, (CHUNK, CHUNK), 0))

        def chunk(ci, carry):
            rows = pl.ds(pl.multiple_of(ci * CHUNK, CHUNK), CHUNK)
            u = p_ref[0, rows, :].astype(F32)
            v = p_ref[1, rows, :].astype(F32)
            z = p_ref[2, rows, :].astype(F32)
            dy = dy_ref[rows, :].astype(F32)
            ug, dug, dvg, rstd, vhat, vln, mixed, sig = _sgu_chunk_forward(u, v, z, wm_v, bs_v, lng, lnb)
            sz = z * sig
            dmixed = dy * ug * sz
            dp_ref[0, rows, :] = (dy * mixed * sz * dug).astype(BF16)
            dp_ref[2, rows, :] = (dy * ug * mixed * (sig * (1.0 + z * (1.0 - sig)))).astype(BF16)
            dbs_ref[...] += jnp.sum(dmixed, axis=1, keepdims=True)
            dmb = dmixed.astype(BF16)
            dws_ref[...] += jnp.where(causal, _dot(dmb, vln, 1, 1), 0.0)
            dvln = _dot(wm_v, dmb, 0, 0)
            db_ref[...] += jnp.sum(dvln, axis=0, keepdims=True)
            dg_ref[...] += jnp.sum(dvln * vhat, axis=0, keepdims=True)
            dvh = dvln * lng
            m1 = jnp.mean(dvh, axis=-1, keepdims=True)
            m2 = jnp.mean(dvh * vhat, axis=-1, keepdims=True)
            dp_ref[1, rows, :] = (rstd * (dvh - m1 - vhat * m2) * dvg).astype(BF16)
            return carry

        lax.fori_loop(0, tm // CHUNK, chunk, 0, unroll=True)

    return pl.pallas_call(
        body, name="mixer_a_backward", grid=(N_HEADS, t // tm),
        in_specs=[pl.BlockSpec((3, tm, HEAD_DIM), lambda h, i: (0, i, h)),
                  pl.BlockSpec((None, tm, HEAD_DIM), lambda h, i: (0, i, h)),
                  pl.BlockSpec((None, CHUNK, CHUNK), lambda h, i: (h, 0, 0)),
                  pl.BlockSpec((None, CHUNK, 1), lambda h, i: (h, 0, 0)),
                  pl.BlockSpec((1, HEAD_DIM), lambda h, i: (0, h)),
                  pl.BlockSpec((1, HEAD_DIM), lambda h, i: (0, h)), ANY],
        out_specs=(pl.BlockSpec((3, tm, HEAD_DIM), lambda h, i: (BLOCK_A, i, h)),
                   pl.BlockSpec((None, CHUNK, CHUNK), lambda h, i: (h, 0, 0)),
                   pl.BlockSpec((None, CHUNK, 1), lambda h, i: (h, 0, 0)),
                   pl.BlockSpec((1, HEAD_DIM), lambda h, i: (0, h)),
                   pl.BlockSpec((1, HEAD_DIM), lambda h, i: (0, h))),
        out_shape=(jax.ShapeDtypeStruct(dp.shape, BF16),
                   jax.ShapeDtypeStruct((N_HEADS, CHUNK, CHUNK), F32),
                   jax.ShapeDtypeStruct((N_HEADS, CHUNK, 1), F32),
                   jax.ShapeDtypeStruct((1, D_MODEL), F32), jax.ShapeDtypeStruct((1, D_MODEL), F32)),
        input_output_aliases={6: 0},
        compiler_params=_params("parallel", "arbitrary"),
    )(p_a, dyab, wm, bs_col, ln_v_g, ln_v_b, dp)


HALO = 16


def _conv_taps(h, halo_h, tm):
    row = lax.broadcasted_iota(jnp.int32, h.shape, 0)
    last1 = halo_h[HALO - 1:HALO, :]
    last2 = halo_h[HALO - 2:HALO - 1, :]
    h1 = jnp.where(row == 0, last1, pltpu.roll(h, 1, 0))
    h2 = jnp.where(row == 0, last2, jnp.where(row == 1, last1, pltpu.roll(h, 2, 0)))
    return h1, h2


def _mixer_b_forward(p_b, conv_w, conv_b, tm, tc):
    t = p_b.shape[1]

    def body(p_ref, halo_ref, w_ref, b_ref, o_ref):
        valid = (pl.program_id(1) > 0).astype(F32)
        h = p_ref[1].astype(F32) * p_ref[0].astype(F32)
        halo_h = halo_ref[1].astype(F32) * halo_ref[0].astype(F32) * valid
        h1, h2 = _conv_taps(h, halo_h, tm)
        w = w_ref[...]
        conv = b_ref[...] + w[0:1, :] * h2 + w[1:2, :] * h1 + w[2:3, :] * h
        z = p_ref[3].astype(F32)
        o_ref[...] = (p_ref[2].astype(F32) * conv * (z * _sigmoid(z))).astype(BF16)

    steps = tm // HALO
    return pl.pallas_call(
        body, name="mixer_b_forward", grid=(D_MODEL // tc, t // tm),
        in_specs=[pl.BlockSpec((4, tm, tc), lambda j, i: (BLOCK_B, i, j)),
                  pl.BlockSpec((4, HALO, tc), lambda j, i: (BLOCK_B, jnp.maximum(i * steps - 1, 0), j)),
                  pl.BlockSpec((3, tc), lambda j, i: (0, j)),
                  pl.BlockSpec((1, tc), lambda j, i: (0, j))],
        out_specs=pl.BlockSpec((tm, tc), lambda j, i: (i, j)),
        out_shape=jax.ShapeDtypeStruct((t, D_MODEL), BF16),
        compiler_params=_params("parallel", "parallel"),
    )(p_b, p_b, conv_w, conv_b)


def _mixer_b_backward(p_b, dyab, conv_w, conv_b, dp, tm, tc):
    t = p_b.shape[1]
    n = t // tm

    def body(p_ref, halo_ref, dy_ref, w_ref, b_ref, dp_in, dp_ref, dw_ref, db_ref, next_ref):
        ii = pl.program_id(1)

        @pl.when(ii == 0)
        def _():
            dw_ref[...] = jnp.zeros_like(dw_ref)
            db_ref[...] = jnp.zeros_like(db_ref)
            next_ref[...] = jnp.zeros_like(next_ref)

        valid = (ii < n - 1).astype(F32)
        xb = p_ref[0].astype(F32)
        cb = p_ref[1].astype(F32)
        bb = p_ref[2].astype(F32)
        z = p_ref[3].astype(F32)
        h = cb * xb
        halo_h = halo_ref[1].astype(F32) * halo_ref[0].astype(F32) * valid
        h1, h2 = _conv_taps(h, halo_h, tm)
        w = w_ref[...]
        w0, w1, w2 = w[0:1, :], w[1:2, :], w[2:3, :]
        conv = b_ref[...] + w0 * h2 + w1 * h1 + w2 * h
        sig = _sigmoid(z)
        sz = z * sig
        dy = dy_ref[...].astype(F32)
        dconv = dy * bb * sz
        dp_ref[2] = (dy * conv * sz).astype(BF16)
        dp_ref[3] = (dy * bb * conv * (sig * (1.0 + z * (1.0 - sig)))).astype(BF16)
        db_ref[...] += jnp.sum(dconv, axis=0, keepdims=True)
        dw_ref[0:1, :] += jnp.sum(dconv * h2, axis=0, keepdims=True)
        dw_ref[1:2, :] += jnp.sum(dconv * h1, axis=0, keepdims=True)
        dw_ref[2:3, :] += jnp.sum(dconv * h, axis=0, keepdims=True)
        row = lax.broadcasted_iota(jnp.int32, h.shape, 0)
        nxt = next_ref[...]
        n0, n1 = nxt[0:1, :], nxt[1:2, :]
        d1 = jnp.where(row == tm - 1, n0, pltpu.roll(dconv, tm - 1, 0))
        d2 = jnp.where(row == tm - 1, n1, jnp.where(row == tm - 2, n0, pltpu.roll(dconv, tm - 2, 0)))
        dh = w2 * dconv + w1 * d1 + w0 * d2
        dp_ref[0] = (dh * cb).astype(BF16)
        dp_ref[1] = (dh * xb).astype(BF16)
        next_ref[...] = dconv[0:8, :]

    steps = tm // HALO
    return pl.pallas_call(
        body, name="mixer_b_backward", grid=(D_MODEL // tc, n),
        in_specs=[pl.BlockSpec((4, tm, tc), lambda j, ii: (BLOCK_B, n - 1 - ii, j)),
                  pl.BlockSpec((4, HALO, tc), lambda j, ii: (BLOCK_B, jnp.maximum((n - 1 - ii) * steps - 1, 0), j)),
                  pl.BlockSpec((None, tm, tc), lambda j, ii: (1, n - 1 - ii, j)),
                  pl.BlockSpec((3, tc), lambda j, ii: (0, j)),
                  pl.BlockSpec((1, tc), lambda j, ii: (0, j)), ANY],
        out_specs=(pl.BlockSpec((4, tm, tc), lambda j, ii: (BLOCK_B, n - 1 - ii, j)),
                   pl.BlockSpec((3, tc), lambda j, ii: (0, j)),
                   pl.BlockSpec((1, tc), lambda j, ii: (0, j))),
        out_shape=(jax.ShapeDtypeStruct(dp.shape, BF16),
                   jax.ShapeDtypeStruct((3, D_MODEL), F32), jax.ShapeDtypeStruct((1, D_MODEL), F32)),
        scratch_shapes=[pltpu.VMEM((8, tc), F32)],
        input_output_aliases={5: 0},
        compiler_params=_params("parallel", "arbitrary"),
    )(p_b, p_b, dyab, conv_w, conv_b, dp)


def _adam_math(w, g, m, v):
    m = ADAM_B1 * m + (1.0 - ADAM_B1) * g
    v = ADAM_B2 * v + (1.0 - ADAM_B2) * (g * g)
    delta = -ADAM_LR * ((m * ADAM_C1) / (jnp.sqrt(v * ADAM_C2) + ADAM_EPS) + ADAM_WD * w)
    return delta, m, v


def _adam_rows(w, g, m, v, tm, name):
    r, c = w.shape

    def body(w_ref, g_ref, m_ref, v_ref, go_ref, d_ref, mo_ref, vo_ref):
        g = g_ref[...]
        d, mn, vn = _adam_math(w_ref[...], g, m_ref[...], v_ref[...])
        go_ref[...] = g
        d_ref[...] = d
        mo_ref[...] = mn
        vo_ref[...] = vn

    spec = pl.BlockSpec((tm, c), lambda i: (i, 0))
    shape = jax.ShapeDtypeStruct((r, c), F32)
    return pl.pallas_call(
        body, name=name, grid=(r // tm,),
        in_specs=[spec] * 4, out_specs=(spec,) * 4, out_shape=(shape,) * 4,
        compiler_params=_params("parallel"),
    )(w, g, m, v)


def _adam_small(gathered, w, m, v):
    r = w.shape[0]

    def body(ga_ref, w_ref, m_ref, v_ref, g_ref, d_ref, mo_ref, vo_ref):
        g = ga_ref[0]
        for k in range(1, 8):
            g = g + ga_ref[k]
        d, mn, vn = _adam_math(w_ref[...], g, m_ref[...], v_ref[...])
        g_ref[...] = g
        d_ref[...] = d
        mo_ref[...] = mn
        vo_ref[...] = vn

    shape = jax.ShapeDtypeStruct((r, 128), F32)
    return pl.pallas_call(
        body, name="adam_small", out_shape=(shape,) * 4, compiler_params=_params(),
    )(gathered, w, m, v)


def _pack_small(parts):
    rows = []
    for name, r in SMALL_ROWS:
        a = parts[name].astype(F32).reshape(-1)
        pad = r * 128 - a.shape[0]
        if pad:
            a = jnp.concatenate([a, jnp.zeros((pad,), F32)])
        rows.append(a.reshape(r, 128))
    return jnp.concatenate(rows, axis=0)


def _unpack_small(buf, shapes):
    out, r0 = {}, 0
    for name, r in SMALL_ROWS:
        shp = shapes[name]
        n = math.prod(shp)
        out[name] = buf[r0:r0 + r].reshape(-1)[:n].reshape(shp)
        r0 += r
    return out


def kernel(x, w_in, b_gate, ln_v_g, ln_v_b, w_s, b_s, conv_w, conv_b, w_oa, w_ob, w_out, ln_g, ln_b, loss_target, m_w_in, m_b_gate, m_ln_v_g, m_ln_v_b, m_w_s, m_b_s, m_conv_w, m_conv_b, m_w_oa, m_w_ob, m_w_out, m_ln_g, m_ln_b, v_w_in, v_b_gate, v_ln_v_g, v_ln_v_b, v_w_s, v_b_s, v_conv_w, v_conv_b, v_w_oa, v_w_ob, v_w_out, v_ln_g, v_ln_b):
    t = x.shape[1]
    x2 = x[0]
    target = loss_target[0]
    chip = 2 * lax.axis_index("x") + lax.axis_index("y")
    conv_cols = conv_w.shape[2]

    w_pre = _cast_into_columns(w_in[0], chip.astype(jnp.int32).reshape(1), N_CHIPS, 256, "cast_w_in")
    wo_b, wm = _prep_small_weights(w_oa[0], w_ob[0], w_out[0], w_s[0])
    conv_w8 = jnp.concatenate([conv_w[0], jnp.zeros((5, conv_cols), F32)], axis=0)
    bs_col = b_s[0].reshape(N_HEADS, CHUNK, 1)
    bg = b_gate.reshape(2, 1, D_MODEL)

    xb = _cast_rows(x2, min(512, t), "cast_x")
    p, w_full, wo_full, cw_full = _gather_and_project(xb, w_pre, wo_b, conv_w8, min(1024, t))
    wo3 = wo_full.reshape(3, D_MODEL, D_MODEL)
    conv_w_all = jnp.transpose(cw_full[:, :3, :], (1, 0, 2)).reshape(3, D_MODEL)
    tm_a = min(512, t)
    ya = _mixer_a_forward(p, wm, bs_col, ln_v_g, ln_v_b, tm_a)
    tm_b = min(512, t)
    yb = _mixer_b_forward(p, conv_w_all, conv_b, tm_b, 512)
    tm_m = min(512, t)
    merged, oab = _merge_forward(ya, yb, wo3, p, bg, tm_m, 512)

    drb, gx, dmerged, d_ln_g, d_ln_b, loss_part = _head(merged, wo3, x2, target, ln_g, ln_b, min(256, t))
    doab, dp, d_bg = _gate_backward(dmerged, oab, p, bg, tm_m, 512)
    dyab = _branch_backward(doab, wo3, tm_m, 512)
    dp, d_ws, d_bs, d_lnv_g, d_lnv_b = _mixer_a_backward(p, dyab, wm, bs_col, ln_v_g, ln_v_b, dp, tm_a)
    dp, d_cw, d_cb = _mixer_b_backward(p, dyab, conv_w_all, conv_b, dp, tm_b, 512)

    tk = min(2048, t)
    g_oa = _weight_grad(ya, 0, doab, 0, 1024, 1024, tk, "grad_w_oa")
    g_ob = _weight_grad(yb, 0, doab, 1, 1024, 1024, tk, "grad_w_ob")
    g_out = _weight_grad(merged, 0, drb, 0, 1024, 1024, tk, "grad_w_out")
    g_in = _win_grad(xb, dp, 1024, min(1024, t))

    r_in, *r_o = _exchange_halves(g_in, (g_oa, g_ob, g_out))
    s_in, s_o = _add_halves(g_in, (g_oa, g_ob, g_out), r_in, r_o)
    small_grads = {"ln_g": d_ln_g, "ln_b": d_ln_b, "b_gate": d_bg, "ln_v_g": d_lnv_g, "ln_v_b": d_lnv_b,
                   "w_s": d_ws, "b_s": d_bs, "conv_w": d_cw, "conv_b": d_cb, "loss": loss_part[:, :1]}
    tm_x = min(512, t)
    gx, q_in, q_o, gathered = _input_grad_and_scatter(dp, w_full, gx, s_in, s_o, _pack_small(small_grads), tm_x)
    grad_x = _input_grad_tail(dp, w_full, gx, tm_x)
    f_in, f_o = _sum_chips(q_in, q_o, s_in, s_o)
    gsum_in, gsum_o = _share_halves(f_in, f_o)

    big = {}
    big["w_in"] = _adam_rows(w_in[0], gsum_in, m_w_in[0], v_w_in[0], 128, "adam_w_in")
    for n, (name, w, m, v) in enumerate((("w_oa", w_oa, m_w_oa, v_w_oa), ("w_ob", w_ob, m_w_ob, v_w_ob),
                                         ("w_out", w_out, m_w_out, v_w_out))):
        big[name] = _adam_rows(w[0], gsum_o[n], m[0], v[0], 256, "adam_" + name)

    def placed(a):
        return lax.dynamic_update_slice(jnp.zeros((3, D_MODEL), F32), a[0], (0, chip * conv_cols))

    def packed(pre, cw):
        z1 = jnp.zeros((1,), F32)
        return _pack_small({"ln_g": pre["ln_g"], "ln_b": pre["ln_b"], "b_gate": pre["b_gate"],
                            "ln_v_g": pre["ln_v_g"], "ln_v_b": pre["ln_v_b"], "w_s": pre["w_s"], "b_s": pre["b_s"],
                            "conv_w": placed(cw), "conv_b": pre["conv_b"], "loss": z1})

    ws = dict(ln_g=ln_g, ln_b=ln_b, b_gate=b_gate, ln_v_g=ln_v_g, ln_v_b=ln_v_b, w_s=w_s, b_s=b_s, conv_b=conv_b)
    ms = dict(ln_g=m_ln_g, ln_b=m_ln_b, b_gate=m_b_gate, ln_v_g=m_ln_v_g, ln_v_b=m_ln_v_b, w_s=m_w_s, b_s=m_b_s,
              conv_b=m_conv_b)
    vs = dict(ln_g=v_ln_g, ln_b=v_ln_b, b_gate=v_b_gate, ln_v_g=v_ln_v_g, ln_v_b=v_ln_v_b, w_s=v_w_s, b_s=v_b_s,
              conv_b=v_conv_b)
    s_g, s_d, s_m, s_v = _adam_small(gathered, packed(ws, conv_w), packed(ms, m_conv_w), packed(vs, v_conv_w))
    shapes = {"ln_g": ln_g.shape, "ln_b": ln_b.shape, "b_gate": b_gate.shape, "ln_v_g": ln_v_g.shape,
              "ln_v_b": ln_v_b.shape, "w_s": w_s.shape, "b_s": b_s.shape, "conv_w": (1, 3, D_MODEL),
              "conv_b": conv_b.shape, "loss": (1,)}
    small = [_unpack_small(b, shapes) for b in (s_g, s_d, s_m, s_v)]
    for d in small:
        d["conv_w"] = lax.dynamic_slice(d["conv_w"], (0, 0, chip * conv_cols), (1, 3, conv_cols))
    loss = small[0]["loss"][0]

    order = ("w_in", "b_gate", "ln_v_g", "ln_v_b", "w_s", "b_s", "conv_w", "conv_b", "w_oa", "w_ob", "w_out",
             "ln_g", "ln_b")
    outs = [loss, grad_x[None]]
    for which in range(4):
        for name in order:
            outs.append(big[name][which][None] if name in big else small[which][name])
    return tuple(outs)
```

```python
import functools
import math

import jax
import jax.numpy as jnp
from jax import lax
from jax.experimental import pallas as pl
from jax.experimental.pallas import tpu as pltpu

F32 = jnp.float32
BF16 = jnp.bfloat16

D_MODEL = 2048
N_HEADS = 8
HEAD_DIM = D_MODEL // N_HEADS
CHUNK = 128
N_SEG = 9
N_CHIPS = 4
SHARD_COLS = N_SEG * D_MODEL // N_CHIPS
COL_BLOCK = 512
BLOCKS_PER_SHARD = SHARD_COLS // COL_BLOCK
BLOCKS_PER_SEG = D_MODEL // COL_BLOCK
SHARD_ROWS = D_MODEL // N_CHIPS
N_SLOTS = 12
BLOCK_A, BLOCK_G, BLOCK_B = 0, 2, 2
SLOT_OF_SEG = (0, 1, 2, 8, 9, 10, 11, 4, 5)
DN_ALPHA = 2.0 ** 0.25
LN_EPS = 1e-5
GELU_K = math.sqrt(2.0 / math.pi)
GELU_C = 0.044715

ADAM_LR = 0.001
ADAM_B1 = 0.9
ADAM_B2 = 0.999
ADAM_EPS = 1e-08
ADAM_WD = 0.01
ADAM_STEP = 10
ADAM_C1 = 1.0 / (1.0 - ADAM_B1 ** ADAM_STEP)
ADAM_C2 = 1.0 / (1.0 - ADAM_B2 ** ADAM_STEP)

VMEM_LIMIT = 56 * 1024 * 1024
MESH = pl.DeviceIdType.MESH
ANY = pl.BlockSpec(memory_space=pl.ANY)

SMALL_ROWS = (("ln_g", 16), ("ln_b", 16), ("b_gate", 32), ("ln_v_g", 16), ("ln_v_b", 16),
              ("w_s", 1024), ("b_s", 8), ("conv_w", 48), ("conv_b", 16), ("loss", 8))
SMALL_TOTAL = sum(r for _, r in SMALL_ROWS)


def _params(*sem):
    return pltpu.CompilerParams(dimension_semantics=sem, vmem_limit_bytes=VMEM_LIMIT)


def _sigmoid(x):
    return 1.0 / (1.0 + jnp.exp(-x))


def _gelu_and_grad(x):
    x2 = x * x
    th = jnp.tanh(GELU_K * (x + GELU_C * x * x2))
    g = 0.5 * x * (1.0 + th)
    dg = 0.5 * (1.0 + th) + 0.5 * x * (1.0 - th * th) * (GELU_K * (1.0 + 3.0 * GELU_C * x2))
    return g, dg


def _gelu(x):
    return 0.5 * x * (1.0 + jnp.tanh(GELU_K * (x + GELU_C * x * x * x)))


def _dot(a, b, ca=1, cb=0):
    return lax.dot_general(a, b, (((ca,), (cb,)), ((), ())), preferred_element_type=F32)


def _cast_and_transpose(x, tm):
    t, d = x.shape

    def body(x_ref, o_ref, ot_ref):
        v = x_ref[...]
        o_ref[...] = v.astype(BF16)
        ot_ref[...] = v.T.astype(BF16)

    return pl.pallas_call(
        body, name="cast_x", grid=(t // tm,),
        in_specs=[pl.BlockSpec((tm, d), lambda i: (i, 0))],
        out_specs=(pl.BlockSpec((tm, d), lambda i: (i, 0)), pl.BlockSpec((d, tm), lambda i: (0, i))),
        out_shape=(jax.ShapeDtypeStruct((t, d), BF16), jax.ShapeDtypeStruct((d, t), BF16)),
        compiler_params=_params("parallel"),
    )(x)


def _cast_into_columns(w, slot, n_slots, tm, name):
    r, c = w.shape

    def body(s_ref, w_ref, o_ref):
        o_ref[...] = w_ref[...].astype(BF16)

    return pl.pallas_call(
        body, name=name,
        grid_spec=pltpu.PrefetchScalarGridSpec(
            num_scalar_prefetch=1, grid=(r // tm,),
            in_specs=[pl.BlockSpec((tm, c), lambda i, s_ref: (i, 0))],
            out_specs=pl.BlockSpec((tm, c), lambda i, s_ref: (i, s_ref[0]))),
        out_shape=jax.ShapeDtypeStruct((r, n_slots * c), BF16),
        compiler_params=_params("parallel"),
    )(slot, w)


def _prep_small_weights(w_oa, w_ob, w_out, w_s):
    rows = w_oa.shape[0]

    def body(a_ref, b_ref, c_ref, ws_ref, wo_ref, wm_ref):
        wo_ref[0] = a_ref[...].astype(BF16)
        wo_ref[1] = b_ref[...].astype(BF16)
        wo_ref[2] = c_ref[...].astype(BF16)
        t = lax.broadcasted_iota(jnp.int32, (CHUNK, CHUNK), 0)
        s = lax.broadcasted_iota(jnp.int32, (CHUNK, CHUNK), 1)
        for h in range(N_HEADS):
            wm_ref[h] = jnp.where(s <= t, ws_ref[h], 0.0).astype(BF16)

    return pl.pallas_call(
        body, name="prep_small_weights",
        out_shape=(jax.ShapeDtypeStruct((3, rows, D_MODEL), BF16),
                   jax.ShapeDtypeStruct((N_HEADS, CHUNK, CHUNK), BF16)),
        compiler_params=_params(),
    )(w_oa, w_ob, w_out, w_s)


def _mesh_pos():
    x, y, c = lax.axis_index("x"), lax.axis_index("y"), lax.axis_index("c")
    chips = [(1 - x, y), (x, 1 - y), (1 - x, 1 - y)]
    return x, y, c, chips


def _slot_of_seg(seg):
    return jnp.where(seg < 3, seg, jnp.where(seg < 7, seg + 5, seg - 3))


def _gather_and_project(xb, w_pre, wo_b, conv_w8, tm):
    t = xb.shape[0]
    d, sc = w_pre.shape[0], w_pre.shape[1] // N_CHIPS
    rows = wo_b.shape[1]
    hd, hr = d // 2, rows // 2
    ni, nj = t // tm, BLOCKS_PER_SHARD
    total = N_CHIPS * ni * nj
    mx, my = lax.axis_index("x"), lax.axis_index("y")
    order = jnp.stack([2 * mx + my, 2 * (1 - mx) + my, 2 * mx + (1 - my),
                       2 * (1 - mx) + (1 - my)]).astype(jnp.int32)

    def body(order_ref, x_ref, wpre_ref, wo_ref, cw_ref, p_ref, wf_ref, wof_ref, cwf_ref,
             wbuf, wsem, send_sems, recv_sems, local_sems):
        x, y, c, chips = _mesh_pos()
        me = 2 * x + y
        sibling = (x, y, 1 - c)
        s, i, j = pl.program_id(0), pl.program_id(1), pl.program_id(2)
        n = (s * ni + i) * nj + j

        def shard_cols(k):
            return pl.ds(pl.multiple_of(k * sc, COL_BLOCK), sc)

        def half(kind, k, cc):
            if kind == 0:
                return wf_ref.at[pl.ds(cc * hd, hd), shard_cols(k)]
            return wof_ref.at[:, k, pl.ds(cc * hr, hr), :]

        def src_half(kind, cc):
            if kind == 0:
                return wpre_ref.at[pl.ds(cc * hd, hd), shard_cols(me)]
            return wo_ref.at[:, pl.ds(cc * hr, hr), :]

        def rcopy(sem, src, dst, to):
            return pltpu.make_async_remote_copy(src_ref=src, dst_ref=dst, send_sem=send_sems.at[sem],
                                                recv_sem=recv_sems.at[sem], device_id=to, device_id_type=MESH)

        def ici_send(kind, q):
            return rcopy(kind * 6 + q, src_half(kind, c), half(kind, me, c), (*chips[q], c))

        def ici_landed(kind, q):
            k = 2 * chips[q][0] + chips[q][1]
            return rcopy(kind * 6 + q, src_half(kind, c), half(kind, k, c), sibling)

        def forward(kind, q, cc):
            k = 2 * chips[q][0] + chips[q][1]
            return rcopy(kind * 6 + 3 + q, half(kind, k, cc), half(kind, k, cc), sibling)

        def conv_send(q):
            return rcopy(12 + q, cw_ref, cwf_ref.at[me], (*chips[q], c))

        def local_copies():
            return [pltpu.make_async_copy(wo_ref, wof_ref.at[:, me], local_sems.at[0]),
                    pltpu.make_async_copy(cw_ref, cwf_ref.at[me], local_sems.at[1])]

        def tile_start(s_, j_, slot):
            g = order_ref[s_] * BLOCKS_PER_SHARD + j_
            cols = pl.ds(pl.multiple_of(g * COL_BLOCK, COL_BLOCK), COL_BLOCK)

            @pl.when(s_ == 0)
            def _():
                pltpu.make_async_copy(wpre_ref.at[:, cols], wbuf.at[slot], wsem.at[slot]).start()

            @pl.when(s_ > 0)
            def _():
                pltpu.make_async_copy(wf_ref.at[:, cols], wbuf.at[slot], wsem.at[slot]).start()

        def at(s_, i_, j_):
            return jnp.logical_and(s == s_, jnp.logical_and(i == i_, j == j_))

        @pl.when(n == 0)
        def _():
            for cp in local_copies():
                cp.start()
            for q in (0, 1):
                ici_send(0, q).start()
            for q in range(3):
                conv_send(q).start()
            tile_start(0, 0, 0)

        @pl.when(at(0, ni - 1, nj - 1))
        def _():
            for q in (0, 1):
                ici_landed(0, q).wait_recv()
                forward(0, q, c).start()
            for q in (0, 1):
                ici_send(0, q).wait_send()
            ici_send(0, 2).start()
            forward(0, 0, 1 - c).wait_recv()

        @pl.when(at(1, ni - 1, nj - 1))
        def _():
            forward(0, 1, 1 - c).wait_recv()

        @pl.when(at(2, (3 * ni) // 4, 0))
        def _():
            ici_landed(0, 2).wait_recv()
            forward(0, 2, c).start()
            ici_send(0, 2).wait_send()
            for q in range(3):
                ici_send(1, q).start()

        @pl.when(at(2, ni - 1, nj - 1))
        def _():
            forward(0, 2, 1 - c).wait_recv()

        n1 = n + 1

        @pl.when(n1 < total)
        def _():
            tile_start(n1 // (ni * nj), lax.rem(n1, nj), lax.rem(n1, 2))

        slot = lax.rem(n, 2)
        pltpu.make_async_copy(wpre_ref.at[:, pl.ds(0, COL_BLOCK)], wbuf.at[slot], wsem.at[slot]).wait()
        p_ref[...] = _dot(x_ref[...], wbuf[slot]).astype(BF16)

        @pl.when(n == total - 1)
        def _():
            for q in range(3):
                ici_landed(1, q).wait_recv()
                forward(1, q, c).start()
            for q in range(3):
                forward(1, q, 1 - c).wait_recv()
                rcopy(12 + q, cw_ref, cwf_ref.at[2 * chips[q][0] + chips[q][1]], sibling).wait_recv()
            for q in range(3):
                ici_send(1, q).wait_send()
                forward(0, q, c).wait_send()
                forward(1, q, c).wait_send()
                conv_send(q).wait_send()
            for cp in local_copies():
                cp.wait()

    def p_map(s, i, j, o):
        g = o[s] * BLOCKS_PER_SHARD + j
        return (_slot_of_seg(g // BLOCKS_PER_SEG), i, lax.rem(g, BLOCKS_PER_SEG))

    return pl.pallas_call(
        body, name="gather_and_project",
        grid_spec=pltpu.PrefetchScalarGridSpec(
            num_scalar_prefetch=1, grid=(N_CHIPS, ni, nj),
            in_specs=[pl.BlockSpec((tm, D_MODEL), lambda s, i, j, o: (i, 0)), ANY, ANY, ANY],
            out_specs=(pl.BlockSpec((None, tm, COL_BLOCK), p_map), ANY, ANY, ANY),
            scratch_shapes=[pltpu.VMEM((2, d, COL_BLOCK), BF16), pltpu.SemaphoreType.DMA((2,)),
                            pltpu.SemaphoreType.DMA((15,)), pltpu.SemaphoreType.DMA((15,)),
                            pltpu.SemaphoreType.DMA((2,))]),
        out_shape=(jax.ShapeDtypeStruct((N_SLOTS, t, D_MODEL), BF16),
                   jax.ShapeDtypeStruct((d, N_CHIPS * sc), BF16),
                   jax.ShapeDtypeStruct((3, N_CHIPS, rows, D_MODEL), BF16),
                   jax.ShapeDtypeStruct((N_CHIPS,) + conv_w8.shape, F32)),
        input_output_aliases={2: 1},
        compiler_params=pltpu.CompilerParams(dimension_semantics=("arbitrary",) * 3, vmem_limit_bytes=VMEM_LIMIT,
                                             has_side_effects=True),
    )(order, xb, w_pre, wo_b, conv_w8)


def _exchange_halves(g_in, g_o):
    d, c9 = g_in.shape
    hd = d // 2
    hr = SHARD_ROWS // 2
    g_o4 = [g.reshape(N_CHIPS, 2, hr, D_MODEL) for g in g_o]

    def body(gi_ref, ga_ref, gb_ref, gc_ref, ri_ref, ra_ref, rb_ref, rc_ref, send_sems, recv_sems):
        x, y, c, _ = _mesh_pos()
        sibling = (x, y, 1 - c)
        cps = [pltpu.make_async_remote_copy(src_ref=gi_ref.at[pl.ds((1 - c) * hd, hd), :], dst_ref=ri_ref,
                                            send_sem=send_sems.at[0], recv_sem=recv_sems.at[0],
                                            device_id=sibling, device_id_type=MESH)]
        for n, (g_ref, r_ref) in enumerate(((ga_ref, ra_ref), (gb_ref, rb_ref), (gc_ref, rc_ref))):
            cps.append(pltpu.make_async_remote_copy(src_ref=g_ref.at[:, 1 - c], dst_ref=r_ref,
                                                    send_sem=send_sems.at[1 + n], recv_sem=recv_sems.at[1 + n],
                                                    device_id=sibling, device_id_type=MESH))
        for cp in cps:
            cp.start()
        for cp in cps:
            cp.wait()

    o_shape = jax.ShapeDtypeStruct((N_CHIPS, hr, D_MODEL), BF16)
    return pl.pallas_call(
        body, name="rs_exchange_halves",
        in_specs=[ANY] * 4, out_specs=(ANY,) * 4,
        out_shape=(jax.ShapeDtypeStruct((hd, c9), BF16), o_shape, o_shape, o_shape),
        scratch_shapes=[pltpu.SemaphoreType.DMA((4,)), pltpu.SemaphoreType.DMA((4,))],
        compiler_params=pltpu.CompilerParams(has_side_effects=True),
    )(g_in, *g_o4)


def _add_halves(g_in, g_o, r_in, r_o):
    d, c9 = g_in.shape
    hd = d // 2
    hr = SHARD_ROWS // 2
    core = lax.axis_index("c").astype(jnp.int32).reshape(1)
    tm = min(512, hd)
    nb = hd // tm

    def body_in(c_ref, g_ref, r_ref, o_ref):
        o_ref[...] = (g_ref[...].astype(F32) + r_ref[...].astype(F32)).astype(BF16)

    s_in = pl.pallas_call(
        body_in, name="rs_add_halves_in",
        grid_spec=pltpu.PrefetchScalarGridSpec(
            num_scalar_prefetch=1, grid=(N_CHIPS, nb),
            in_specs=[pl.BlockSpec((tm, SHARD_COLS), lambda k, i, c_ref: (c_ref[0] * nb + i, k)),
                      pl.BlockSpec((tm, SHARD_COLS), lambda k, i, c_ref: (i, k))],
            out_specs=pl.BlockSpec((None, tm, SHARD_COLS), lambda k, i, c_ref: (k, i, 0))),
        out_shape=jax.ShapeDtypeStruct((N_CHIPS, hd, SHARD_COLS), BF16),
        compiler_params=_params("parallel", "parallel"),
    )(core, g_in, r_in)

    g_o4 = [g.reshape(N_CHIPS, 2, hr, D_MODEL) for g in g_o]

    def body_o(c_ref, ga_ref, gb_ref, gc_ref, ra_ref, rb_ref, rc_ref, o_ref):
        for n, (g_ref, r_ref) in enumerate(((ga_ref, ra_ref), (gb_ref, rb_ref), (gc_ref, rc_ref))):
            o_ref[n] = (g_ref[...].astype(F32) + r_ref[...].astype(F32)).astype(BF16)

    gspec = pl.BlockSpec((None, None, hr, D_MODEL), lambda k, c_ref: (k, c_ref[0], 0, 0))
    rspec = pl.BlockSpec((None, hr, D_MODEL), lambda k, c_ref: (k, 0, 0))
    s_o = pl.pallas_call(
        body_o, name="rs_add_halves_o",
        grid_spec=pltpu.PrefetchScalarGridSpec(
            num_scalar_prefetch=1, grid=(N_CHIPS,),
            in_specs=[gspec] * 3 + [rspec] * 3,
            out_specs=pl.BlockSpec((3, None, hr, D_MODEL), lambda k, c_ref: (0, k, 0, 0))),
        out_shape=jax.ShapeDtypeStruct((3, N_CHIPS, hr, D_MODEL), BF16),
        compiler_params=_params("parallel"),
    )(core, *g_o4, *r_o)
    return s_in, s_o


def _sum_chips(r_in, r_o, s_in, s_o):
    _, hd, sc = r_in.shape
    hr = r_o.shape[2]
    tm = min(256, hd)
    nb = hd // tm
    pos = jnp.stack([2 * lax.axis_index("x") + lax.axis_index("y"), lax.axis_index("c")]).astype(jnp.int32)

    def chip_sum(pos_ref, r_ref, s_ref):
        acc = None
        for k in range(N_CHIPS):
            term = jnp.where(pos_ref[0] == k, s_ref[...], r_ref[k]).astype(F32)
            acc = term if acc is None else acc + term
        return acc

    def body_in(pos_ref, r_ref, s_ref, o_ref):
        o_ref[...] = chip_sum(pos_ref, r_ref, s_ref)

    f_in = pl.pallas_call(
        body_in, name="rs_sum_chips_in",
        grid_spec=pltpu.PrefetchScalarGridSpec(
            num_scalar_prefetch=1, grid=(nb,),
            in_specs=[pl.BlockSpec((N_CHIPS, tm, sc), lambda i, p: (0, i, 0)),
                      pl.BlockSpec((None, tm, sc), lambda i, p: (p[0], i, 0))],
            out_specs=pl.BlockSpec((tm, sc), lambda i, p: (p[1] * nb + i, 0))),
        out_shape=jax.ShapeDtypeStruct((2 * hd, sc), F32),
        compiler_params=_params("parallel"),
    )(pos, r_in, s_in)

    def body_o(pos_ref, r_ref, s_ref, o_ref):
        o_ref[...] = chip_sum(pos_ref, r_ref, s_ref)

    f_o = pl.pallas_call(
        body_o, name="rs_sum_chips_o",
        grid_spec=pltpu.PrefetchScalarGridSpec(
            num_scalar_prefetch=1, grid=(3,),
            in_specs=[pl.BlockSpec((N_CHIPS, None, hr, D_MODEL), lambda n, p: (0, n, 0, 0)),
                      pl.BlockSpec((None, None, hr, D_MODEL), lambda n, p: (n, p[0], 0, 0))],
            out_specs=pl.BlockSpec((None, hr, D_MODEL), lambda n, p: (n, p[1], 0))),
        out_shape=jax.ShapeDtypeStruct((3, 2 * hr, D_MODEL), F32),
        compiler_params=_params("parallel"),
    )(pos, r_o, s_o)
    return f_in, f_o


def _share_halves(f_in, f_o):
    hd, sc = f_in.shape[0] // 2, f_in.shape[1]
    hr = f_o.shape[1] // 2

    def body(fi_ref, fo_ref, gi_ref, go_ref, send_sems, recv_sems):
        x, y, c, _ = _mesh_pos()
        sibling = (x, y, 1 - c)

        def halves(cc):
            rows_i, rows_o = pl.ds(cc * hd, hd), pl.ds(cc * hr, hr)
            return (fi_ref.at[rows_i, :], gi_ref.at[rows_i, :]), (fo_ref.at[:, rows_o, :], go_ref.at[:, rows_o, :])

        def copies(cc):
            return [pltpu.make_async_remote_copy(src_ref=src, dst_ref=dst, send_sem=send_sems.at[n],
                                                 recv_sem=recv_sems.at[n], device_id=sibling, device_id_type=MESH)
                    for n, (src, dst) in enumerate(halves(cc))]

        sends = copies(c)
        for cp in sends:
            cp.start()
        for cp in copies(1 - c):
            cp.wait_recv()
        for cp in sends:
            cp.wait_send()

    return pl.pallas_call(
        body, name="rs_share_halves",
        in_specs=[ANY, ANY], out_specs=(ANY, ANY),
        out_shape=(jax.ShapeDtypeStruct(f_in.shape, F32), jax.ShapeDtypeStruct(f_o.shape, F32)),
        scratch_shapes=[pltpu.SemaphoreType.DMA((2,)), pltpu.SemaphoreType.DMA((2,))],
        input_output_aliases={0: 0, 1: 1},
        compiler_params=pltpu.CompilerParams(has_side_effects=True),
    )(f_in, f_o)


def _merge_forward(ya, yb, wo3, p, bg, tm, tn):
    t = ya.shape[0]

    def body(ya_ref, yb_ref, wa_ref, wb_ref, g_ref, bg_ref, m_ref, oab_ref):
        oa = _dot(ya_ref[...], wa_ref[...])
        ob = _dot(yb_ref[...], wb_ref[...])
        ga = _sigmoid(g_ref[0].astype(F32) + bg_ref[0])
        gb = _sigmoid(g_ref[1].astype(F32) + bg_ref[1])
        m_ref[...] = (ga * oa + gb * ob).astype(BF16)
        oab_ref[0] = oa.astype(BF16)
        oab_ref[1] = ob.astype(BF16)

    return pl.pallas_call(
        body, name="merge_forward", grid=(t // tm, D_MODEL // tn),
        in_specs=[pl.BlockSpec((tm, D_MODEL), lambda i, j: (i, 0)),
                  pl.BlockSpec((tm, D_MODEL), lambda i, j: (i, 0)),
                  pl.BlockSpec((None, D_MODEL, tn), lambda i, j: (0, 0, j)),
                  pl.BlockSpec((None, D_MODEL, tn), lambda i, j: (1, 0, j)),
                  pl.BlockSpec((2, tm, tn), lambda i, j: (BLOCK_G, i, j)),
                  pl.BlockSpec((2, 1, tn), lambda i, j: (0, 0, j))],
        out_specs=(pl.BlockSpec((tm, tn), lambda i, j: (i, j)),
                   pl.BlockSpec((2, tm, tn), lambda i, j: (0, i, j))),
        out_shape=(jax.ShapeDtypeStruct((t, D_MODEL), BF16), jax.ShapeDtypeStruct((2, t, D_MODEL), BF16)),
        compiler_params=_params("parallel", "parallel"),
    )(ya, yb, wo3, wo3, p, bg)


HEAD_ROWS = 128


def _head(merged, wo3, x, target, ln_g, ln_b, tm):
    t = x.shape[0]
    inv_d = 1.0 / D_MODEL

    def body(m_ref, w_ref, x_ref, t_ref, g_ref, b_ref, dr_ref, gx_ref, dm_ref, dg_ref, db_ref, loss_ref):
        i = pl.program_id(0)

        @pl.when(i == 0)
        def _():
            dg_ref[...] = jnp.zeros_like(dg_ref)
            db_ref[...] = jnp.zeros_like(db_ref)
            loss_ref[...] = jnp.zeros_like(loss_ref)

        w = w_ref[...]
        g = g_ref[...]
        for r0 in range(0, tm, HEAD_ROWS):
            rows = slice(r0, r0 + HEAD_ROWS)
            out = _dot(m_ref[rows, :], w)
            r = DN_ALPHA * x_ref[rows, :] + out
            mu = jnp.mean(r, axis=-1, keepdims=True)
            xc = r - mu
            var = jnp.mean(xc * xc, axis=-1, keepdims=True)
            rstd = lax.rsqrt(var + LN_EPS)
            xhat = xc * rstd
            e = xhat * g + b_ref[...] - t_ref[rows, :]
            se = jnp.sum(jnp.sum(e * e, axis=1, keepdims=True), axis=0, keepdims=True)
            loss_ref[...] += jnp.broadcast_to((0.5 * inv_d) * se, loss_ref.shape)
            dy = e * inv_d
            db_ref[...] += jnp.sum(dy, axis=0, keepdims=True)
            dg_ref[...] += jnp.sum(dy * xhat, axis=0, keepdims=True)
            dxh = dy * g
            m1 = jnp.mean(dxh, axis=-1, keepdims=True)
            m2 = jnp.mean(dxh * xhat, axis=-1, keepdims=True)
            dr = rstd * (dxh - m1 - xhat * m2)
            gx_ref[rows, :] = DN_ALPHA * dr
            drb = dr.astype(BF16)
            dr_ref[rows, :] = drb
            dm_ref[rows, :] = _dot(drb, w, 1, 1).astype(BF16)

    row = pl.BlockSpec((tm, D_MODEL), lambda i: (i, 0))
    vec = pl.BlockSpec((1, D_MODEL), lambda i: (0, 0))
    return pl.pallas_call(
        body, name="head", grid=(t // tm,),
        in_specs=[row, pl.BlockSpec((None, D_MODEL, D_MODEL), lambda i: (2, 0, 0)), row, row, vec, vec],
        out_specs=(row, row, row, vec, vec, pl.BlockSpec((1, 128), lambda i: (0, 0))),
        out_shape=(jax.ShapeDtypeStruct((t, D_MODEL), BF16), jax.ShapeDtypeStruct((t, D_MODEL), F32),
                   jax.ShapeDtypeStruct((t, D_MODEL), BF16), jax.ShapeDtypeStruct((1, D_MODEL), F32),
                   jax.ShapeDtypeStruct((1, D_MODEL), F32), jax.ShapeDtypeStruct((1, 128), F32)),
        compiler_params=_params("arbitrary"),
    )(merged, wo3, x, target, ln_g, ln_b)


def _gate_backward(dmerged, oab, p, bg, tm, tn):
    t = dmerged.shape[0]

    def body(dm_ref, oab_ref, g_ref, bg_ref, do_ref, dpg_ref, dbg_ref):
        @pl.when(pl.program_id(1) == 0)
        def _():
            dbg_ref[...] = jnp.zeros_like(dbg_ref)

        dm = dm_ref[...].astype(F32)
        for n in range(2):
            gate = _sigmoid(g_ref[n].astype(F32) + bg_ref[n])
            do_ref[n] = (dm * gate).astype(BF16)
            dgate = dm * oab_ref[n].astype(F32) * gate * (1.0 - gate)
            dpg_ref[n] = dgate.astype(BF16)
            dbg_ref[n] += jnp.sum(dgate, axis=0, keepdims=True)

    pair = pl.BlockSpec((2, tm, tn), lambda j, i: (0, i, j))
    gates = pl.BlockSpec((2, tm, tn), lambda j, i: (BLOCK_G, i, j))
    vec = pl.BlockSpec((2, 1, tn), lambda j, i: (0, 0, j))
    return pl.pallas_call(
        body, name="gate_backward", grid=(D_MODEL // tn, t // tm),
        in_specs=[pl.BlockSpec((tm, tn), lambda j, i: (i, j)), pair, gates, vec],
        out_specs=(pair, gates, vec),
        out_shape=(jax.ShapeDtypeStruct((2, t, D_MODEL), BF16), jax.ShapeDtypeStruct((N_SLOTS, t, D_MODEL), BF16),
                   jax.ShapeDtypeStruct((2, 1, D_MODEL), F32)),
        compiler_params=_params("parallel", "arbitrary"),
    )(dmerged, oab, p, bg)


def _branch_backward(doab, wo3, tm, tn):
    t = doab.shape[1]

    def body(d_ref, wa_ref, wb_ref, o_ref):
        o_ref[0] = _dot(d_ref[0], wa_ref[...], 1, 1).astype(BF16)
        o_ref[1] = _dot(d_ref[1], wb_ref[...], 1, 1).astype(BF16)

    return pl.pallas_call(
        body, name="branch_backward", grid=(t // tm, D_MODEL // tn),
        in_specs=[pl.BlockSpec((2, tm, D_MODEL), lambda i, j: (0, i, 0)),
                  pl.BlockSpec((None, tn, D_MODEL), lambda i, j: (0, j, 0)),
                  pl.BlockSpec((None, tn, D_MODEL), lambda i, j: (1, j, 0))],
        out_specs=pl.BlockSpec((2, tm, tn), lambda i, j: (0, i, j)),
        out_shape=jax.ShapeDtypeStruct((2, t, D_MODEL), BF16),
        compiler_params=_params("parallel", "parallel"),
    )(doab, wo3, wo3)


def _weight_grad(a, a_sel, b, b_sel, tm, tn, tk, name):
    t = a.shape[-2]
    nk = t // tk

    def body(a_ref, b_ref, o_ref, acc_ref):
        k = pl.program_id(2)
        part = _dot(a_ref[...], b_ref[...], 0, 0)

        @pl.when(k == 0)
        def _():
            acc_ref[...] = part

        @pl.when(k > 0)
        def _():
            acc_ref[...] += part

        @pl.when(k == nk - 1)
        def _():
            o_ref[...] = acc_ref[...].astype(BF16)

    def spec(arr, sel, width, which):
        if arr.ndim == 2:
            return pl.BlockSpec((tk, width), lambda i, j, k: (k, (i, j)[which]))
        return pl.BlockSpec((None, tk, width), lambda i, j, k: (sel, k, (i, j)[which]))

    return pl.pallas_call(
        body, name=name, grid=(D_MODEL // tm, D_MODEL // tn, nk),
        in_specs=[spec(a, a_sel, tm, 0), spec(b, b_sel, tn, 1)],
        out_specs=pl.BlockSpec((tm, tn), lambda i, j, k: (i, j)),
        out_shape=jax.ShapeDtypeStruct((D_MODEL, D_MODEL), BF16),
        scratch_shapes=[pltpu.VMEM((tm, tn), F32)],
        compiler_params=_params("parallel", "parallel", "arbitrary"),
    )(a, b)


def _win_grad(xt, dp, tn, tk):
    _, t, _ = dp.shape
    nk = t // tk
    per_seg = D_MODEL // tn
    nj = N_SEG * per_seg

    def body(x_ref, dp_ref, o_ref, acc_ref):
        k = pl.program_id(1)
        part = _dot(x_ref[...], dp_ref[...])

        @pl.when(k == 0)
        def _():
            acc_ref[...] = part

        @pl.when(k > 0)
        def _():
            acc_ref[...] += part

        @pl.when(k == nk - 1)
        def _():
            o_ref[...] = acc_ref[...].astype(BF16)

    return pl.pallas_call(
        body, name="grad_w_in", grid=(nj, nk),
        in_specs=[pl.BlockSpec((D_MODEL, tk), lambda j, k: (0, k)),
                  pl.BlockSpec((None, tk, tn), lambda j, k: (_slot_of_seg(j // per_seg), k, j % per_seg))],
        out_specs=pl.BlockSpec((D_MODEL, tn), lambda j, k: (0, j)),
        out_shape=jax.ShapeDtypeStruct((D_MODEL, N_SEG * D_MODEL), BF16),
        scratch_shapes=[pltpu.VMEM((D_MODEL, tn), F32)],
        compiler_params=_params("parallel", "arbitrary"),
    )(xt, dp)


def _input_grad_and_scatter(dp, w_full, gx, s_in, s_o, small, tm):
    _, t, _ = dp.shape
    ni, nk = t // tm, N_SEG - 1
    _, hd, sc = s_in.shape
    hr = s_o.shape[2]

    def body(dp_ref, w_ref, gx_ref, si_ref, so_ref, sm_ref, o_ref, ri_ref, ro_ref, ga_ref,
             send_sems, recv_sems, local_sem):
        i, k = pl.program_id(0), pl.program_id(1)
        x, y, c, chips = _mesh_pos()
        me = 2 * x + y
        dev = 4 * x + 2 * y + c

        def peer(r):
            return (x ^ ((r >> 2) & 1), y ^ ((r >> 1) & 1), c ^ (r & 1))

        def sends():
            cps = []
            for q, (cx, cy) in enumerate(chips):
                dest = 2 * cx + cy
                cps.append(pltpu.make_async_remote_copy(src_ref=si_ref.at[dest], dst_ref=ri_ref.at[me],
                                                        send_sem=send_sems.at[q], recv_sem=recv_sems.at[q],
                                                        device_id=(cx, cy, c), device_id_type=MESH))
                cps.append(pltpu.make_async_remote_copy(src_ref=so_ref.at[:, dest], dst_ref=ro_ref.at[me],
                                                        send_sem=send_sems.at[3 + q], recv_sem=recv_sems.at[3 + q],
                                                        device_id=(cx, cy, c), device_id_type=MESH))
            for r in range(1, 8):
                cps.append(pltpu.make_async_remote_copy(src_ref=sm_ref, dst_ref=ga_ref.at[dev],
                                                        send_sem=send_sems.at[5 + r], recv_sem=recv_sems.at[5 + r],
                                                        device_id=peer(r), device_id_type=MESH))
            return cps

        own_small = pltpu.make_async_copy(sm_ref, ga_ref.at[dev], local_sem)

        @pl.when(jnp.logical_and(i == 0, k == 0))
        def _():
            for cp in sends():
                cp.start()
            own_small.start()

        part = _dot(dp_ref[...], w_ref[...], 1, 1)

        @pl.when(k == 0)
        def _():
            o_ref[...] = gx_ref[...] + part

        @pl.when(k > 0)
        def _():
            o_ref[...] += part

        @pl.when(jnp.logical_and(i == ni - 1, k == nk - 1))
        def _():
            for q, (cx, cy) in enumerate(chips):
                frm = 2 * cx + cy
                pltpu.make_async_remote_copy(src_ref=si_ref.at[frm], dst_ref=ri_ref.at[frm], send_sem=send_sems.at[q],
                                             recv_sem=recv_sems.at[q], device_id=(x, y, c),
                                             device_id_type=MESH).wait_recv()
                pltpu.make_async_remote_copy(src_ref=so_ref.at[:, frm], dst_ref=ro_ref.at[frm],
                                             send_sem=send_sems.at[3 + q], recv_sem=recv_sems.at[3 + q],
                                             device_id=(x, y, c), device_id_type=MESH).wait_recv()
            for r in range(1, 8):
                px, py, pc = peer(r)
                pltpu.make_async_remote_copy(src_ref=sm_ref, dst_ref=ga_ref.at[4 * px + 2 * py + pc],
                                             send_sem=send_sems.at[5 + r], recv_sem=recv_sems.at[5 + r],
                                             device_id=(x, y, c), device_id_type=MESH).wait_recv()
            for cp in sends():
                cp.wait_send()
            own_small.wait()

    return pl.pallas_call(
        body, name="grad_x_and_scatter", grid=(ni, nk),
        in_specs=[pl.BlockSpec((None, tm, D_MODEL), lambda i, k: (_slot_of_seg(k), i, 0)),
                  pl.BlockSpec((D_MODEL, D_MODEL), lambda i, k: (0, k)),
                  pl.BlockSpec((tm, D_MODEL), lambda i, k: (i, 0)), ANY, ANY, ANY],
        out_specs=(pl.BlockSpec((tm, D_MODEL), lambda i, k: (i, 0)), ANY, ANY, ANY),
        out_shape=(jax.ShapeDtypeStruct((t, D_MODEL), F32),
                   jax.ShapeDtypeStruct((N_CHIPS, hd, sc), BF16),
                   jax.ShapeDtypeStruct((N_CHIPS, 3, hr, D_MODEL), BF16),
                   jax.ShapeDtypeStruct((8,) + small.shape, small.dtype)),
        scratch_shapes=[pltpu.SemaphoreType.DMA((13,)), pltpu.SemaphoreType.DMA((13,)), pltpu.SemaphoreType.DMA],
        input_output_aliases={2: 0},
        compiler_params=pltpu.CompilerParams(dimension_semantics=("arbitrary", "arbitrary"),
                                             vmem_limit_bytes=VMEM_LIMIT, has_side_effects=True),
    )(dp, w_full, gx, s_in, s_o, small)


def _input_grad_tail(dp, w_full, partial, tm):
    _, t, _ = dp.shape
    seg = N_SEG - 1

    def body(dp_ref, w_ref, part_ref, o_ref):
        o_ref[...] = part_ref[...] + _dot(dp_ref[...], w_ref[...], 1, 1)

    row = pl.BlockSpec((tm, D_MODEL), lambda i: (i, 0))
    return pl.pallas_call(
        body, name="grad_x_tail", grid=(t // tm,),
        in_specs=[pl.BlockSpec((None, tm, D_MODEL), lambda i: (SLOT_OF_SEG[seg], i, 0)),
                  pl.BlockSpec((D_MODEL, D_MODEL), lambda i: (0, seg)), row],
        out_specs=row,
        out_shape=jax.ShapeDtypeStruct((t, D_MODEL), F32),
        compiler_params=_params("parallel"),
    )(dp, w_full, partial)


def _sgu_chunk_forward(u, v, z, wm, bs, lng, lnb):
    ug, dug = _gelu_and_grad(u)
    vg, dvg = _gelu_and_grad(v)
    mu = jnp.mean(vg, axis=-1, keepdims=True)
    xc = vg - mu
    var = jnp.mean(xc * xc, axis=-1, keepdims=True)
    rstd = lax.rsqrt(var + LN_EPS)
    vhat = xc * rstd
    vln = (vhat * lng + lnb).astype(BF16)
    mixed = _dot(wm, vln) + bs
    sig = _sigmoid(z)
    return ug, dug, dvg, rstd, vhat, vln, mixed, sig


def _mixer_a_forward(p_a, wm, bs_col, ln_v_g, ln_v_b, tm):
    t = p_a.shape[1]

    def body(p_ref, wm_ref, bs_ref, g_ref, b_ref, o_ref):
        wm_v, bs_v, lng, lnb = wm_ref[...], bs_ref[...], g_ref[...], b_ref[...]

        def chunk(ci, carry):
            rows = pl.ds(pl.multiple_of(ci * CHUNK, CHUNK), CHUNK)
            u = p_ref[0, rows, :].astype(F32)
            v = p_ref[1, rows, :].astype(F32)
            z = p_ref[2, rows, :].astype(F32)
            ug, _, _, _, _, _, mixed, sig = _sgu_chunk_forward(u, v, z, wm_v, bs_v, lng, lnb)
            o_ref[rows, :] = (ug * mixed * (z * sig)).astype(BF16)
            return carry

        lax.fori_loop(0, tm // CHUNK, chunk, 0, unroll=True)

    return pl.pallas_call(
        body, name="mixer_a_forward", grid=(t // tm, N_HEADS),
        in_specs=[pl.BlockSpec((3, tm, HEAD_DIM), lambda i, h: (0, i, h)),
                  pl.BlockSpec((None, CHUNK, CHUNK), lambda i, h: (h, 0, 0)),
                  pl.BlockSpec((None, CHUNK, 1), lambda i, h: (h, 0, 0)),
                  pl.BlockSpec((1, HEAD_DIM), lambda i, h: (0, h)),
                  pl.BlockSpec((1, HEAD_DIM), lambda i, h: (0, h))],
        out_specs=pl.BlockSpec((tm, HEAD_DIM), lambda i, h: (i, h)),
        out_shape=jax.ShapeDtypeStruct((t, D_MODEL), BF16),
        compiler_params=_params("parallel", "parallel"),
    )(p_a, wm, bs_col, ln_v_g, ln_v_b)


def _mixer_a_backward(p_a, dyab, wm, bs_col, ln_v_g, ln_v_b, dp, tm):
    t = p_a.shape[1]

    def body(p_ref, dy_ref, wm_ref, bs_ref, g_ref, b_ref, dp_in, dp_ref, dws_ref, dbs_ref, dg_ref, db_ref):
        @pl.when(pl.program_id(1) == 0)
        def _():
            dws_ref[...] = jnp.zeros_like(dws_ref)
            dbs_ref[...] = jnp.zeros_like(dbs_ref)
            dg_ref[...] = jnp.zeros_like(dg_ref)
            db_ref[...] = jnp.zeros_like(db_ref)

        wm_v, bs_v, lng, lnb = wm_ref[...], bs_ref[...], g_ref[...], b_ref[...]
        causal = (lax.broadcasted_iota(jnp.int32, (CHUNK, CHUNK), 1)
                  <= lax.broadcasted_iota(jnp.int32, (CHUNK, CHUNK), 0))

        def chunk(ci, carry):
            rows = pl.ds(pl.multiple_of(ci * CHUNK, CHUNK), CHUNK)
            u = p_ref[0, rows, :].astype(F32)
            v = p_ref[1, rows, :].astype(F32)
            z = p_ref[2, rows, :].astype(F32)
            dy = dy_ref[rows, :].astype(F32)
            ug, dug, dvg, rstd, vhat, vln, mixed, sig = _sgu_chunk_forward(u, v, z, wm_v, bs_v, lng, lnb)
            sz = z * sig
            dmixed = dy * ug * sz
            dp_ref[0, rows, :] = (dy * mixed * sz * dug).astype(BF16)
            dp_ref[2, rows, :] = (dy * ug * mixed * (sig * (1.0 + z * (1.0 - sig)))).astype(BF16)
            dbs_ref[...] += jnp.sum(dmixed, axis=1, keepdims=True)
            dmb = dmixed.astype(BF16)
            dws_ref[...] += jnp.where(causal, _dot(dmb, vln, 1, 1), 0.0)
            dvln = _dot(wm_v, dmb, 0, 0)
            db_ref[...] += jnp.sum(dvln, axis=0, keepdims=True)
            dg_ref[...] += jnp.sum(dvln * vhat, axis=0, keepdims=True)
            dvh = dvln * lng
            m1 = jnp.mean(dvh, axis=-1, keepdims=True)
            m2 = jnp.mean(dvh * vhat, axis=-1, keepdims=True)
            dp_ref[1, rows, :] = (rstd * (dvh - m1 - vhat * m2) * dvg).astype(BF16)
            return carry

        lax.fori_loop(0, tm // CHUNK, chunk, 0, unroll=True)

    return pl.pallas_call(
        body, name="mixer_a_backward", grid=(N_HEADS, t // tm),
        in_specs=[pl.BlockSpec((3, tm, HEAD_DIM), lambda h, i: (0, i, h)),
                  pl.BlockSpec((None, tm, HEAD_DIM), lambda h, i: (0, i, h)),
                  pl.BlockSpec((None, CHUNK, CHUNK), lambda h, i: (h, 0, 0)),
                  pl.BlockSpec((None, CHUNK, 1), lambda h, i: (h, 0, 0)),
                  pl.BlockSpec((1, HEAD_DIM), lambda h, i: (0, h)),
                  pl.BlockSpec((1, HEAD_DIM), lambda h, i: (0, h)), ANY],
        out_specs=(pl.BlockSpec((3, tm, HEAD_DIM), lambda h, i: (BLOCK_A, i, h)),
                   pl.BlockSpec((None, CHUNK, CHUNK), lambda h, i: (h, 0, 0)),
                   pl.BlockSpec((None, CHUNK, 1), lambda h, i: (h, 0, 0)),
                   pl.BlockSpec((1, HEAD_DIM), lambda h, i: (0, h)),
                   pl.BlockSpec((1, HEAD_DIM), lambda h, i: (0, h))),
        out_shape=(jax.ShapeDtypeStruct(dp.shape, BF16),
                   jax.ShapeDtypeStruct((N_HEADS, CHUNK, CHUNK), F32),
                   jax.ShapeDtypeStruct((N_HEADS, CHUNK, 1), F32),
                   jax.ShapeDtypeStruct((1, D_MODEL), F32), jax.ShapeDtypeStruct((1, D_MODEL), F32)),
        input_output_aliases={6: 0},
        compiler_params=_params("parallel", "arbitrary"),
    )(p_a, dyab, wm, bs_col, ln_v_g, ln_v_b, dp)


HALO = 16


def _conv_taps(h, halo_h, tm):
    row = lax.broadcasted_iota(jnp.int32, h.shape, 0)
    last1 = halo_h[HALO - 1:HALO, :]
    last2 = halo_h[HALO - 2:HALO - 1, :]
    h1 = jnp.where(row == 0, last1, pltpu.roll(h, 1, 0))
    h2 = jnp.where(row == 0, last2, jnp.where(row == 1, last1, pltpu.roll(h, 2, 0)))
    return h1, h2


def _mixer_b_forward(p_b, conv_w, conv_b, tm, tc):
    t = p_b.shape[1]

    def body(p_ref, halo_ref, w_ref, b_ref, o_ref):
        valid = (pl.program_id(1) > 0).astype(F32)
        h = p_ref[1].astype(F32) * p_ref[0].astype(F32)
        halo_h = halo_ref[1].astype(F32) * halo_ref[0].astype(F32) * valid
        h1, h2 = _conv_taps(h, halo_h, tm)
        w = w_ref[...]
        conv = b_ref[...] + w[0:1, :] * h2 + w[1:2, :] * h1 + w[2:3, :] * h
        z = p_ref[3].astype(F32)
        o_ref[...] = (p_ref[2].astype(F32) * conv * (z * _sigmoid(z))).astype(BF16)

    steps = tm // HALO
    return pl.pallas_call(
        body, name="mixer_b_forward", grid=(D_MODEL // tc, t // tm),
        in_specs=[pl.BlockSpec((4, tm, tc), lambda j, i: (BLOCK_B, i, j)),
                  pl.BlockSpec((4, HALO, tc), lambda j, i: (BLOCK_B, jnp.maximum(i * steps - 1, 0), j)),
                  pl.BlockSpec((3, tc), lambda j, i: (0, j)),
                  pl.BlockSpec((1, tc), lambda j, i: (0, j))],
        out_specs=pl.BlockSpec((tm, tc), lambda j, i: (i, j)),
        out_shape=jax.ShapeDtypeStruct((t, D_MODEL), BF16),
        compiler_params=_params("parallel", "parallel"),
    )(p_b, p_b, conv_w, conv_b)


def _mixer_b_backward(p_b, dyab, conv_w, conv_b, dp, tm, tc):
    t = p_b.shape[1]
    n = t // tm

    def body(p_ref, halo_ref, dy_ref, w_ref, b_ref, dp_in, dp_ref, dw_ref, db_ref, next_ref):
        ii = pl.program_id(1)

        @pl.when(ii == 0)
        def _():
            dw_ref[...] = jnp.zeros_like(dw_ref)
            db_ref[...] = jnp.zeros_like(db_ref)
            next_ref[...] = jnp.zeros_like(next_ref)

        valid = (ii < n - 1).astype(F32)
        xb = p_ref[0].astype(F32)
        cb = p_ref[1].astype(F32)
        bb = p_ref[2].astype(F32)
        z = p_ref[3].astype(F32)
        h = cb * xb
        halo_h = halo_ref[1].astype(F32) * halo_ref[0].astype(F32) * valid
        h1, h2 = _conv_taps(h, halo_h, tm)
        w = w_ref[...]
        w0, w1, w2 = w[0:1, :], w[1:2, :], w[2:3, :]
        conv = b_ref[...] + w0 * h2 + w1 * h1 + w2 * h
        sig = _sigmoid(z)
        sz = z * sig
        dy = dy_ref[...].astype(F32)
        dconv = dy * bb * sz
        dp_ref[2] = (dy * conv * sz).astype(BF16)
        dp_ref[3] = (dy * bb * conv * (sig * (1.0 + z * (1.0 - sig)))).astype(BF16)
        db_ref[...] += jnp.sum(dconv, axis=0, keepdims=True)
        dw_ref[0:1, :] += jnp.sum(dconv * h2, axis=0, keepdims=True)
        dw_ref[1:2, :] += jnp.sum(dconv * h1, axis=0, keepdims=True)
        dw_ref[2:3, :] += jnp.sum(dconv * h, axis=0, keepdims=True)
        row = lax.broadcasted_iota(jnp.int32, h.shape, 0)
        nxt = next_ref[...]
        n0, n1 = nxt[0:1, :], nxt[1:2, :]
        d1 = jnp.where(row == tm - 1, n0, pltpu.roll(dconv, tm - 1, 0))
        d2 = jnp.where(row == tm - 1, n1, jnp.where(row == tm - 2, n0, pltpu.roll(dconv, tm - 2, 0)))
        dh = w2 * dconv + w1 * d1 + w0 * d2
        dp_ref[0] = (dh * cb).astype(BF16)
        dp_ref[1] = (dh * xb).astype(BF16)
        next_ref[...] = dconv[0:8, :]

    steps = tm // HALO
    return pl.pallas_call(
        body, name="mixer_b_backward", grid=(D_MODEL // tc, n),
        in_specs=[pl.BlockSpec((4, tm, tc), lambda j, ii: (BLOCK_B, n - 1 - ii, j)),
                  pl.BlockSpec((4, HALO, tc), lambda j, ii: (BLOCK_B, jnp.maximum((n - 1 - ii) * steps - 1, 0), j)),
                  pl.BlockSpec((None, tm, tc), lambda j, ii: (1, n - 1 - ii, j)),
                  pl.BlockSpec((3, tc), lambda j, ii: (0, j)),
                  pl.BlockSpec((1, tc), lambda j, ii: (0, j)), ANY],
        out_specs=(pl.BlockSpec((4, tm, tc), lambda j, ii: (BLOCK_B, n - 1 - ii, j)),
                   pl.BlockSpec((3, tc), lambda j, ii: (0, j)),
                   pl.BlockSpec((1, tc), lambda j, ii: (0, j))),
        out_shape=(jax.ShapeDtypeStruct(dp.shape, BF16),
                   jax.ShapeDtypeStruct((3, D_MODEL), F32), jax.ShapeDtypeStruct((1, D_MODEL), F32)),
        scratch_shapes=[pltpu.VMEM((8, tc), F32)],
        input_output_aliases={5: 0},
        compiler_params=_params("parallel", "arbitrary"),
    )(p_b, p_b, dyab, conv_w, conv_b, dp)


def _adam_math(w, g, m, v):
    m = ADAM_B1 * m + (1.0 - ADAM_B1) * g
    v = ADAM_B2 * v + (1.0 - ADAM_B2) * (g * g)
    delta = -ADAM_LR * ((m * ADAM_C1) / (jnp.sqrt(v * ADAM_C2) + ADAM_EPS) + ADAM_WD * w)
    return delta, m, v


def _adam_rows(w, g, m, v, tm, name):
    r, c = w.shape

    def body(w_ref, g_ref, m_ref, v_ref, go_ref, d_ref, mo_ref, vo_ref):
        g = g_ref[...]
        d, mn, vn = _adam_math(w_ref[...], g, m_ref[...], v_ref[...])
        go_ref[...] = g
        d_ref[...] = d
        mo_ref[...] = mn
        vo_ref[...] = vn

    spec = pl.BlockSpec((tm, c), lambda i: (i, 0))
    shape = jax.ShapeDtypeStruct((r, c), F32)
    return pl.pallas_call(
        body, name=name, grid=(r // tm,),
        in_specs=[spec] * 4, out_specs=(spec,) * 4, out_shape=(shape,) * 4,
        compiler_params=_params("parallel"),
    )(w, g, m, v)


def _adam_small(gathered, w, m, v):
    r = w.shape[0]

    def body(ga_ref, w_ref, m_ref, v_ref, g_ref, d_ref, mo_ref, vo_ref):
        g = ga_ref[0]
        for k in range(1, 8):
            g = g + ga_ref[k]
        d, mn, vn = _adam_math(w_ref[...], g, m_ref[...], v_ref[...])
        g_ref[...] = g
        d_ref[...] = d
        mo_ref[...] = mn
        vo_ref[...] = vn

    shape = jax.ShapeDtypeStruct((r, 128), F32)
    return pl.pallas_call(
        body, name="adam_small", out_shape=(shape,) * 4, compiler_params=_params(),
    )(gathered, w, m, v)


def _pack_small(parts):
    rows = []
    for name, r in SMALL_ROWS:
        a = parts[name].astype(F32).reshape(-1)
        pad = r * 128 - a.shape[0]
        if pad:
            a = jnp.concatenate([a, jnp.zeros((pad,), F32)])
        rows.append(a.reshape(r, 128))
    return jnp.concatenate(rows, axis=0)


def _unpack_small(buf, shapes):
    out, r0 = {}, 0
    for name, r in SMALL_ROWS:
        shp = shapes[name]
        n = math.prod(shp)
        out[name] = buf[r0:r0 + r].reshape(-1)[:n].reshape(shp)
        r0 += r
    return out


def kernel(x, w_in, b_gate, ln_v_g, ln_v_b, w_s, b_s, conv_w, conv_b, w_oa, w_ob, w_out, ln_g, ln_b, loss_target, m_w_in, m_b_gate, m_ln_v_g, m_ln_v_b, m_w_s, m_b_s, m_conv_w, m_conv_b, m_w_oa, m_w_ob, m_w_out, m_ln_g, m_ln_b, v_w_in, v_b_gate, v_ln_v_g, v_ln_v_b, v_w_s, v_b_s, v_conv_w, v_conv_b, v_w_oa, v_w_ob, v_w_out, v_ln_g, v_ln_b):
    t = x.shape[1]
    x2 = x[0]
    target = loss_target[0]
    chip = 2 * lax.axis_index("x") + lax.axis_index("y")
    conv_cols = conv_w.shape[2]

    w_pre = _cast_into_columns(w_in[0], chip.astype(jnp.int32).reshape(1), N_CHIPS, 256, "cast_w_in")
    wo_b, wm = _prep_small_weights(w_oa[0], w_ob[0], w_out[0], w_s[0])
    conv_w8 = jnp.concatenate([conv_w[0], jnp.zeros((5, conv_cols), F32)], axis=0)
    bs_col = b_s[0].reshape(N_HEADS, CHUNK, 1)
    bg = b_gate.reshape(2, 1, D_MODEL)

    xb, xt = _cast_and_transpose(x2, min(512, t))
    p, w_full, wo_full, cw_full = _gather_and_project(xb, w_pre, wo_b, conv_w8, min(2048, t))
    wo3 = wo_full.reshape(3, D_MODEL, D_MODEL)
    conv_w_all = jnp.transpose(cw_full[:, :3, :], (1, 0, 2)).reshape(3, D_MODEL)
    tm_a = min(512, t)
    ya = _mixer_a_forward(p, wm, bs_col, ln_v_g, ln_v_b, tm_a)
    tm_b = min(512, t)
    yb = _mixer_b_forward(p, conv_w_all, conv_b, tm_b, 512)
    tm_m = min(1024, t)
    merged, oab = _merge_forward(ya, yb, wo3, p, bg, tm_m, 512)

    drb, gx, dmerged, d_ln_g, d_ln_b, loss_part = _head(merged, wo3, x2, target, ln_g, ln_b, min(256, t))
    doab, dp, d_bg = _gate_backward(dmerged, oab, p, bg, tm_m, 512)
    dyab = _branch_backward(doab, wo3, tm_m, 512)
    dp, d_ws, d_bs, d_lnv_g, d_lnv_b = _mixer_a_backward(p, dyab, wm, bs_col, ln_v_g, ln_v_b, dp, tm_a)
    dp, d_cw, d_cb = _mixer_b_backward(p, dyab, conv_w_all, conv_b, dp, tm_b, 512)

    tk = min(2048, t)
    g_oa = _weight_grad(ya, 0, doab, 0, 1024, 1024, tk, "grad_w_oa")
    g_ob = _weight_grad(yb, 0, doab, 1, 1024, 1024, tk, "grad_w_ob")
    g_out = _weight_grad(merged, 0, drb, 0, 1024, 1024, tk, "grad_w_out")
    g_in = _win_grad(xt, dp, 1024, tk)

    r_in, *r_o = _exchange_halves(g_in, (g_oa, g_ob, g_out))
    s_in, s_o = _add_halves(g_in, (g_oa, g_ob, g_out), r_in, r_o)
    small_grads = {"ln_g": d_ln_g, "ln_b": d_ln_b, "b_gate": d_bg, "ln_v_g": d_lnv_g, "ln_v_b": d_lnv_b,
                   "w_s": d_ws, "b_s": d_bs, "conv_w": d_cw, "conv_b": d_cb, "loss": loss_part[:, :1]}
    tm_x = min(512, t)
    gx, q_in, q_o, gathered = _input_grad_and_scatter(dp, w_full, gx, s_in, s_o, _pack_small(small_grads), tm_x)
    grad_x = _input_grad_tail(dp, w_full, gx, tm_x)
    f_in, f_o = _sum_chips(q_in, q_o, s_in, s_o)
    gsum_in, gsum_o = _share_halves(f_in, f_o)

    big = {}
    big["w_in"] = _adam_rows(w_in[0], gsum_in, m_w_in[0], v_w_in[0], 128, "adam_w_in")
    for n, (name, w, m, v) in enumerate((("w_oa", w_oa, m_w_oa, v_w_oa), ("w_ob", w_ob, m_w_ob, v_w_ob),
                                         ("w_out", w_out, m_w_out, v_w_out))):
        big[name] = _adam_rows(w[0], gsum_o[n], m[0], v[0], 256, "adam_" + name)

    def placed(a):
        return lax.dynamic_update_slice(jnp.zeros((3, D_MODEL), F32), a[0], (0, chip * conv_cols))

    def packed(pre, cw):
        z1 = jnp.zeros((1,), F32)
        return _pack_small({"ln_g": pre["ln_g"], "ln_b": pre["ln_b"], "b_gate": pre["b_gate"],
                            "ln_v_g": pre["ln_v_g"], "ln_v_b": pre["ln_v_b"], "w_s": pre["w_s"], "b_s": pre["b_s"],
                            "conv_w": placed(cw), "conv_b": pre["conv_b"], "loss": z1})

    ws = dict(ln_g=ln_g, ln_b=ln_b, b_gate=b_gate, ln_v_g=ln_v_g, ln_v_b=ln_v_b, w_s=w_s, b_s=b_s, conv_b=conv_b)
    ms = dict(ln_g=m_ln_g, ln_b=m_ln_b, b_gate=m_b_gate, ln_v_g=m_ln_v_g, ln_v_b=m_ln_v_b, w_s=m_w_s, b_s=m_b_s,
              conv_b=m_conv_b)
    vs = dict(ln_g=v_ln_g, ln_b=v_ln_b, b_gate=v_b_gate, ln_v_g=v_ln_v_g, ln_v_b=v_ln_v_b, w_s=v_w_s, b_s=v_b_s,
              conv_b=v_conv_b)
    s_g, s_d, s_m, s_v = _adam_small(gathered, packed(ws, conv_w), packed(ms, m_conv_w), packed(vs, v_conv_w))
    shapes = {"ln_g": ln_g.shape, "ln_b": ln_b.shape, "b_gate": b_gate.shape, "ln_v_g": ln_v_g.shape,
              "ln_v_b": ln_v_b.shape, "w_s": w_s.shape, "b_s": b_s.shape, "conv_w": (1, 3, D_MODEL),
              "conv_b": conv_b.shape, "loss": (1,)}
    small = [_unpack_small(b, shapes) for b in (s_g, s_d, s_m, s_v)]
    for d in small:
        d["conv_w"] = lax.dynamic_slice(d["conv_w"], (0, 0, chip * conv_cols), (1, 3, conv_cols))
    loss = small[0]["loss"][0]

    order = ("w_in", "b_gate", "ln_v_g", "ln_v_b", "w_s", "b_s", "conv_w", "conv_b", "w_oa", "w_ob", "w_out",
             "ln_g", "ln_b")
    outs = [loss, grad_x[None]]
    for which in range(4):
        for name in order:
            outs.append(big[name][which][None] if name in big else small[which][name])
    return tuple(outs)
```

```python
import functools
import math

import jax
import jax.numpy as jnp
from jax import lax
from jax.experimental import pallas as pl
from jax.experimental.pallas import tpu as pltpu

F32 = jnp.float32
BF16 = jnp.bfloat16

D_MODEL = 2048
N_HEADS = 8
HEAD_DIM = D_MODEL // N_HEADS
CHUNK = 128
N_SEG = 9
N_CHIPS = 4
SHARD_COLS = N_SEG * D_MODEL // N_CHIPS
COL_BLOCK = 512
BLOCKS_PER_SHARD = SHARD_COLS // COL_BLOCK
BLOCKS_PER_SEG = D_MODEL // COL_BLOCK
SHARD_ROWS = D_MODEL // N_CHIPS
N_SLOTS = 12
BLOCK_A, BLOCK_G, BLOCK_B = 0, 2, 2
SLOT_OF_SEG = (0, 1, 2, 8, 9, 10, 11, 4, 5)
DN_ALPHA = 2.0 ** 0.25
LN_EPS = 1e-5
GELU_K = math.sqrt(2.0 / math.pi)
GELU_C = 0.044715

ADAM_LR = 0.001
ADAM_B1 = 0.9
ADAM_B2 = 0.999
ADAM_EPS = 1e-08
ADAM_WD = 0.01
ADAM_STEP = 10
ADAM_C1 = 1.0 / (1.0 - ADAM_B1 ** ADAM_STEP)
ADAM_C2 = 1.0 / (1.0 - ADAM_B2 ** ADAM_STEP)

VMEM_LIMIT = 56 * 1024 * 1024
MESH = pl.DeviceIdType.MESH
ANY = pl.BlockSpec(memory_space=pl.ANY)

SMALL_ROWS = (("ln_g", 16), ("ln_b", 16), ("b_gate", 32), ("ln_v_g", 16), ("ln_v_b", 16),
              ("w_s", 1024), ("b_s", 8), ("conv_w", 48), ("conv_b", 16), ("loss", 8))
SMALL_TOTAL = sum(r for _, r in SMALL_ROWS)


def _params(*sem):
    return pltpu.CompilerParams(dimension_semantics=sem, vmem_limit_bytes=VMEM_LIMIT)


def _sigmoid(x):
    return 1.0 / (1.0 + jnp.exp(-x))


def _gelu_and_grad(x):
    x2 = x * x
    th = jnp.tanh(GELU_K * (x + GELU_C * x * x2))
    g = 0.5 * x * (1.0 + th)
    dg = 0.5 * (1.0 + th) + 0.5 * x * (1.0 - th * th) * (GELU_K * (1.0 + 3.0 * GELU_C * x2))
    return g, dg


def _gelu(x):
    return 0.5 * x * (1.0 + jnp.tanh(GELU_K * (x + GELU_C * x * x * x)))


def _dot(a, b, ca=1, cb=0):
    return lax.dot_general(a, b, (((ca,), (cb,)), ((), ())), preferred_element_type=F32)


def _cast_and_transpose(x, tm):
    t, d = x.shape

    def body(x_ref, o_ref, ot_ref):
        v = x_ref[...]
        o_ref[...] = v.astype(BF16)
        ot_ref[...] = v.T.astype(BF16)

    return pl.pallas_call(
        body, name="cast_x", grid=(t // tm,),
        in_specs=[pl.BlockSpec((tm, d), lambda i: (i, 0))],
        out_specs=(pl.BlockSpec((tm, d), lambda i: (i, 0)), pl.BlockSpec((d, tm), lambda i: (0, i))),
        out_shape=(jax.ShapeDtypeStruct((t, d), BF16), jax.ShapeDtypeStruct((d, t), BF16)),
        compiler_params=_params("parallel"),
    )(x)


def _cast_into_columns(w, slot, n_slots, tm, name):
    r, c = w.shape

    def body(s_ref, w_ref, o_ref):
        o_ref[...] = w_ref[...].astype(BF16)

    return pl.pallas_call(
        body, name=name,
        grid_spec=pltpu.PrefetchScalarGridSpec(
            num_scalar_prefetch=1, grid=(r // tm,),
            in_specs=[pl.BlockSpec((tm, c), lambda i, s_ref: (i, 0))],
            out_specs=pl.BlockSpec((tm, c), lambda i, s_ref: (i, s_ref[0]))),
        out_shape=jax.ShapeDtypeStruct((r, n_slots * c), BF16),
        compiler_params=_params("parallel"),
    )(slot, w)


def _prep_small_weights(w_oa, w_ob, w_out, w_s):
    rows = w_oa.shape[0]

    def body(a_ref, b_ref, c_ref, ws_ref, wo_ref, wm_ref):
        wo_ref[0] = a_ref[...].astype(BF16)
        wo_ref[1] = b_ref[...].astype(BF16)
        wo_ref[2] = c_ref[...].astype(BF16)
        t = lax.broadcasted_iota(jnp.int32, (CHUNK, CHUNK), 0)
        s = lax.broadcasted_iota(jnp.int32, (CHUNK, CHUNK), 1)
        for h in range(N_HEADS):
            wm_ref[h] = jnp.where(s <= t, ws_ref[h], 0.0).astype(BF16)

    return pl.pallas_call(
        body, name="prep_small_weights",
        out_shape=(jax.ShapeDtypeStruct((3, rows, D_MODEL), BF16),
                   jax.ShapeDtypeStruct((N_HEADS, CHUNK, CHUNK), BF16)),
        compiler_params=_params(),
    )(w_oa, w_ob, w_out, w_s)


def _mesh_pos():
    x, y, c = lax.axis_index("x"), lax.axis_index("y"), lax.axis_index("c")
    chips = [(1 - x, y), (x, 1 - y), (1 - x, 1 - y)]
    return x, y, c, chips


def _slot_of_seg(seg):
    return jnp.where(seg < 3, seg, jnp.where(seg < 7, seg + 5, seg - 3))


def _gather_and_project(xb, w_pre, wo_b, conv_w8, tm):
    t = xb.shape[0]
    d, sc = w_pre.shape[0], w_pre.shape[1] // N_CHIPS
    rows = wo_b.shape[1]
    hd, hr = d // 2, rows // 2
    ni, nj = t // tm, BLOCKS_PER_SHARD
    total = N_CHIPS * ni * nj
    mx, my = lax.axis_index("x"), lax.axis_index("y")
    order = jnp.stack([2 * mx + my, 2 * (1 - mx) + my, 2 * mx + (1 - my),
                       2 * (1 - mx) + (1 - my)]).astype(jnp.int32)

    def body(order_ref, x_ref, wpre_ref, wo_ref, cw_ref, p_ref, wf_ref, wof_ref, cwf_ref,
             wbuf, wsem, send_sems, recv_sems, local_sems):
        x, y, c, chips = _mesh_pos()
        me = 2 * x + y
        sibling = (x, y, 1 - c)
        s, i, j = pl.program_id(0), pl.program_id(1), pl.program_id(2)
        n = (s * ni + i) * nj + j

        def shard_cols(k):
            return pl.ds(pl.multiple_of(k * sc, COL_BLOCK), sc)

        def half(kind, k, cc):
            if kind == 0:
                return wf_ref.at[pl.ds(cc * hd, hd), shard_cols(k)]
            return wof_ref.at[:, k, pl.ds(cc * hr, hr), :]

        def src_half(kind, cc):
            if kind == 0:
                return wpre_ref.at[pl.ds(cc * hd, hd), shard_cols(me)]
            return wo_ref.at[:, pl.ds(cc * hr, hr), :]

        def rcopy(sem, src, dst, to):
            return pltpu.make_async_remote_copy(src_ref=src, dst_ref=dst, send_sem=send_sems.at[sem],
                                                recv_sem=recv_sems.at[sem], device_id=to, device_id_type=MESH)

        def ici_send(kind, q):
            return rcopy(kind * 6 + q, src_half(kind, c), half(kind, me, c), (*chips[q], c))

        def ici_landed(kind, q):
            k = 2 * chips[q][0] + chips[q][1]
            return rcopy(kind * 6 + q, src_half(kind, c), half(kind, k, c), sibling)

        def forward(kind, q, cc):
            k = 2 * chips[q][0] + chips[q][1]
            return rcopy(kind * 6 + 3 + q, half(kind, k, cc), half(kind, k, cc), sibling)

        def conv_send(q):
            return rcopy(12 + q, cw_ref, cwf_ref.at[me], (*chips[q], c))

        def local_copies():
            return [pltpu.make_async_copy(wo_ref, wof_ref.at[:, me], local_sems.at[0]),
                    pltpu.make_async_copy(cw_ref, cwf_ref.at[me], local_sems.at[1])]

        def tile_start(s_, j_, slot):
            g = order_ref[s_] * BLOCKS_PER_SHARD + j_
            cols = pl.ds(pl.multiple_of(g * COL_BLOCK, COL_BLOCK), COL_BLOCK)

            @pl.when(s_ == 0)
            def _():
                pltpu.make_async_copy(wpre_ref.at[:, cols], wbuf.at[slot], wsem.at[slot]).start()

            @pl.when(s_ > 0)
            def _():
                pltpu.make_async_copy(wf_ref.at[:, cols], wbuf.at[slot], wsem.at[slot]).start()

        def at(s_, i_, j_):
            return jnp.logical_and(s == s_, jnp.logical_and(i == i_, j == j_))

        @pl.when(n == 0)
        def _():
            for cp in local_copies():
                cp.start()
            for q in (0, 1):
                ici_send(0, q).start()
            for q in range(3):
                conv_send(q).start()
            tile_start(0, 0, 0)

        @pl.when(at(0, ni - 1, nj - 1))
        def _():
            for q in (0, 1):
                ici_landed(0, q).wait_recv()
                forward(0, q, c).start()
            for q in (0, 1):
                ici_send(0, q).wait_send()
            ici_send(0, 2).start()
            forward(0, 0, 1 - c).wait_recv()

        @pl.when(at(1, ni - 1, nj - 1))
        def _():
            forward(0, 1, 1 - c).wait_recv()

        @pl.when(at(2, (3 * ni) // 4, 0))
        def _():
            ici_landed(0, 2).wait_recv()
            forward(0, 2, c).start()
            ici_send(0, 2).wait_send()
            for q in range(3):
                ici_send(1, q).start()

        @pl.when(at(2, ni - 1, nj - 1))
        def _():
            forward(0, 2, 1 - c).wait_recv()

        n1 = n + 1

        @pl.when(n1 < total)
        def _():
            tile_start(n1 // (ni * nj), lax.rem(n1, nj), lax.rem(n1, 2))

        slot = lax.rem(n, 2)
        pltpu.make_async_copy(wpre_ref.at[:, pl.ds(0, COL_BLOCK)], wbuf.at[slot], wsem.at[slot]).wait()
        p_ref[...] = _dot(x_ref[...], wbuf[slot]).astype(BF16)

        @pl.when(n == total - 1)
        def _():
            for q in range(3):
                ici_landed(1, q).wait_recv()
                forward(1, q, c).start()
            for q in range(3):
                forward(1, q, 1 - c).wait_recv()
                rcopy(12 + q, cw_ref, cwf_ref.at[2 * chips[q][0] + chips[q][1]], sibling).wait_recv()
            for q in range(3):
                ici_send(1, q).wait_send()
                forward(0, q, c).wait_send()
                forward(1, q, c).wait_send()
                conv_send(q).wait_send()
            for cp in local_copies():
                cp.wait()

    def p_map(s, i, j, o):
        g = o[s] * BLOCKS_PER_SHARD + j
        return (_slot_of_seg(g // BLOCKS_PER_SEG), i, lax.rem(g, BLOCKS_PER_SEG))

    return pl.pallas_call(
        body, name="gather_and_project",
        grid_spec=pltpu.PrefetchScalarGridSpec(
            num_scalar_prefetch=1, grid=(N_CHIPS, ni, nj),
            in_specs=[pl.BlockSpec((tm, D_MODEL), lambda s, i, j, o: (i, 0)), ANY, ANY, ANY],
            out_specs=(pl.BlockSpec((None, tm, COL_BLOCK), p_map), ANY, ANY, ANY),
            scratch_shapes=[pltpu.VMEM((2, d, COL_BLOCK), BF16), pltpu.SemaphoreType.DMA((2,)),
                            pltpu.SemaphoreType.DMA((15,)), pltpu.SemaphoreType.DMA((15,)),
                            pltpu.SemaphoreType.DMA((2,))]),
        out_shape=(jax.ShapeDtypeStruct((N_SLOTS, t, D_MODEL), BF16),
                   jax.ShapeDtypeStruct((d, N_CHIPS * sc), BF16),
                   jax.ShapeDtypeStruct((3, N_CHIPS, rows, D_MODEL), BF16),
                   jax.ShapeDtypeStruct((N_CHIPS,) + conv_w8.shape, F32)),
        input_output_aliases={2: 1},
        compiler_params=pltpu.CompilerParams(dimension_semantics=("arbitrary",) * 3, vmem_limit_bytes=VMEM_LIMIT,
                                             has_side_effects=True),
    )(order, xb, w_pre, wo_b, conv_w8)


def _exchange_halves(g_in, g_o):
    d, c9 = g_in.shape
    hd = d // 2
    hr = SHARD_ROWS // 2
    g_o4 = [g.reshape(N_CHIPS, 2, hr, D_MODEL) for g in g_o]

    def body(gi_ref, ga_ref, gb_ref, gc_ref, ri_ref, ra_ref, rb_ref, rc_ref, send_sems, recv_sems):
        x, y, c, _ = _mesh_pos()
        sibling = (x, y, 1 - c)
        cps = [pltpu.make_async_remote_copy(src_ref=gi_ref.at[pl.ds((1 - c) * hd, hd), :], dst_ref=ri_ref,
                                            send_sem=send_sems.at[0], recv_sem=recv_sems.at[0],
                                            device_id=sibling, device_id_type=MESH)]
        for n, (g_ref, r_ref) in enumerate(((ga_ref, ra_ref), (gb_ref, rb_ref), (gc_ref, rc_ref))):
            cps.append(pltpu.make_async_remote_copy(src_ref=g_ref.at[:, 1 - c], dst_ref=r_ref,
                                                    send_sem=send_sems.at[1 + n], recv_sem=recv_sems.at[1 + n],
                                                    device_id=sibling, device_id_type=MESH))
        for cp in cps:
            cp.start()
        for cp in cps:
            cp.wait()

    o_shape = jax.ShapeDtypeStruct((N_CHIPS, hr, D_MODEL), BF16)
    return pl.pallas_call(
        body, name="rs_exchange_halves",
        in_specs=[ANY] * 4, out_specs=(ANY,) * 4,
        out_shape=(jax.ShapeDtypeStruct((hd, c9), BF16), o_shape, o_shape, o_shape),
        scratch_shapes=[pltpu.SemaphoreType.DMA((4,)), pltpu.SemaphoreType.DMA((4,))],
        compiler_params=pltpu.CompilerParams(has_side_effects=True),
    )(g_in, *g_o4)


def _add_halves(g_in, g_o, r_in, r_o):
    d, c9 = g_in.shape
    hd = d // 2
    hr = SHARD_ROWS // 2
    core = lax.axis_index("c").astype(jnp.int32).reshape(1)
    tm = min(512, hd)
    nb = hd // tm

    def body_in(c_ref, g_ref, r_ref, o_ref):
        o_ref[...] = (g_ref[...].astype(F32) + r_ref[...].astype(F32)).astype(BF16)

    s_in = pl.pallas_call(
        body_in, name="rs_add_halves_in",
        grid_spec=pltpu.PrefetchScalarGridSpec(
            num_scalar_prefetch=1, grid=(N_CHIPS, nb),
            in_specs=[pl.BlockSpec((tm, SHARD_COLS), lambda k, i, c_ref: (c_ref[0] * nb + i, k)),
                      pl.BlockSpec((tm, SHARD_COLS), lambda k, i, c_ref: (i, k))],
            out_specs=pl.BlockSpec((None, tm, SHARD_COLS), lambda k, i, c_ref: (k, i, 0))),
        out_shape=jax.ShapeDtypeStruct((N_CHIPS, hd, SHARD_COLS), BF16),
        compiler_params=_params("parallel", "parallel"),
    )(core, g_in, r_in)

    g_o4 = [g.reshape(N_CHIPS, 2, hr, D_MODEL) for g in g_o]

    def body_o(c_ref, ga_ref, gb_ref, gc_ref, ra_ref, rb_ref, rc_ref, o_ref):
        for n, (g_ref, r_ref) in enumerate(((ga_ref, ra_ref), (gb_ref, rb_ref), (gc_ref, rc_ref))):
            o_ref[n] = (g_ref[...].astype(F32) + r_ref[...].astype(F32)).astype(BF16)

    gspec = pl.BlockSpec((None, None, hr, D_MODEL), lambda k, c_ref: (k, c_ref[0], 0, 0))
    rspec = pl.BlockSpec((None, hr, D_MODEL), lambda k, c_ref: (k, 0, 0))
    s_o = pl.pallas_call(
        body_o, name="rs_add_halves_o",
        grid_spec=pltpu.PrefetchScalarGridSpec(
            num_scalar_prefetch=1, grid=(N_CHIPS,),
            in_specs=[gspec] * 3 + [rspec] * 3,
            out_specs=pl.BlockSpec((3, None, hr, D_MODEL), lambda k, c_ref: (0, k, 0, 0))),
        out_shape=jax.ShapeDtypeStruct((3, N_CHIPS, hr, D_MODEL), BF16),
        compiler_params=_params("parallel"),
    )(core, *g_o4, *r_o)
    return s_in, s_o


def _sum_chips(r_in, r_o, s_in, s_o):
    _, hd, sc = r_in.shape
    hr = r_o.shape[2]
    tm = min(256, hd)
    nb = hd // tm
    pos = jnp.stack([2 * lax.axis_index("x") + lax.axis_index("y"), lax.axis_index("c")]).astype(jnp.int32)

    def chip_sum(pos_ref, r_ref, s_ref):
        acc = None
        for k in range(N_CHIPS):
            term = jnp.where(pos_ref[0] == k, s_ref[...], r_ref[k]).astype(F32)
            acc = term if acc is None else acc + term
        return acc

    def body_in(pos_ref, r_ref, s_ref, o_ref):
        o_ref[...] = chip_sum(pos_ref, r_ref, s_ref)

    f_in = pl.pallas_call(
        body_in, name="rs_sum_chips_in",
        grid_spec=pltpu.PrefetchScalarGridSpec(
            num_scalar_prefetch=1, grid=(nb,),
            in_specs=[pl.BlockSpec((N_CHIPS, tm, sc), lambda i, p: (0, i, 0)),
                      pl.BlockSpec((None, tm, sc), lambda i, p: (p[0], i, 0))],
            out_specs=pl.BlockSpec((tm, sc), lambda i, p: (p[1] * nb + i, 0))),
        out_shape=jax.ShapeDtypeStruct((2 * hd, sc), F32),
        compiler_params=_params("parallel"),
    )(pos, r_in, s_in)

    def body_o(pos_ref, r_ref, s_ref, o_ref):
        o_ref[...] = chip_sum(pos_ref, r_ref, s_ref)

    f_o = pl.pallas_call(
        body_o, name="rs_sum_chips_o",
        grid_spec=pltpu.PrefetchScalarGridSpec(
            num_scalar_prefetch=1, grid=(3,),
            in_specs=[pl.BlockSpec((N_CHIPS, None, hr, D_MODEL), lambda n, p: (0, n, 0, 0)),
                      pl.BlockSpec((None, None, hr, D_MODEL), lambda n, p: (n, p[0], 0, 0))],
            out_specs=pl.BlockSpec((None, hr, D_MODEL), lambda n, p: (n, p[1], 0))),
        out_shape=jax.ShapeDtypeStruct((3, 2 * hr, D_MODEL), F32),
        compiler_params=_params("parallel"),
    )(pos, r_o, s_o)
    return f_in, f_o


def _share_halves(f_in, f_o):
    hd, sc = f_in.shape[0] // 2, f_in.shape[1]
    hr = f_o.shape[1] // 2

    def body(fi_ref, fo_ref, gi_ref, go_ref, send_sems, recv_sems):
        x, y, c, _ = _mesh_pos()
        sibling = (x, y, 1 - c)

        def halves(cc):
            rows_i, rows_o = pl.ds(cc * hd, hd), pl.ds(cc * hr, hr)
            return (fi_ref.at[rows_i, :], gi_ref.at[rows_i, :]), (fo_ref.at[:, rows_o, :], go_ref.at[:, rows_o, :])

        def copies(cc):
            return [pltpu.make_async_remote_copy(src_ref=src, dst_ref=dst, send_sem=send_sems.at[n],
                                                 recv_sem=recv_sems.at[n], device_id=sibling, device_id_type=MESH)
                    for n, (src, dst) in enumerate(halves(cc))]

        sends = copies(c)
        for cp in sends:
            cp.start()
        for cp in copies(1 - c):
            cp.wait_recv()
        for cp in sends:
            cp.wait_send()

    return pl.pallas_call(
        body, name="rs_share_halves",
        in_specs=[ANY, ANY], out_specs=(ANY, ANY),
        out_shape=(jax.ShapeDtypeStruct(f_in.shape, F32), jax.ShapeDtypeStruct(f_o.shape, F32)),
        scratch_shapes=[pltpu.SemaphoreType.DMA((2,)), pltpu.SemaphoreType.DMA((2,))],
        input_output_aliases={0: 0, 1: 1},
        compiler_params=pltpu.CompilerParams(has_side_effects=True),
    )(f_in, f_o)


def _merge_forward(ya, yb, wo3, p, bg, tm, tn):
    t = ya.shape[0]

    def body(ya_ref, yb_ref, wa_ref, wb_ref, g_ref, bg_ref, m_ref, oab_ref):
        oa = _dot(ya_ref[...], wa_ref[...])
        ob = _dot(yb_ref[...], wb_ref[...])
        ga = _sigmoid(g_ref[0].astype(F32) + bg_ref[0])
        gb = _sigmoid(g_ref[1].astype(F32) + bg_ref[1])
        m_ref[...] = (ga * oa + gb * ob).astype(BF16)
        oab_ref[0] = oa.astype(BF16)
        oab_ref[1] = ob.astype(BF16)

    return pl.pallas_call(
        body, name="merge_forward", grid=(t // tm, D_MODEL // tn),
        in_specs=[pl.BlockSpec((tm, D_MODEL), lambda i, j: (i, 0)),
                  pl.BlockSpec((tm, D_MODEL), lambda i, j: (i, 0)),
                  pl.BlockSpec((None, D_MODEL, tn), lambda i, j: (0, 0, j)),
                  pl.BlockSpec((None, D_MODEL, tn), lambda i, j: (1, 0, j)),
                  pl.BlockSpec((2, tm, tn), lambda i, j: (BLOCK_G, i, j)),
                  pl.BlockSpec((2, 1, tn), lambda i, j: (0, 0, j))],
        out_specs=(pl.BlockSpec((tm, tn), lambda i, j: (i, j)),
                   pl.BlockSpec((2, tm, tn), lambda i, j: (0, i, j))),
        out_shape=(jax.ShapeDtypeStruct((t, D_MODEL), BF16), jax.ShapeDtypeStruct((2, t, D_MODEL), BF16)),
        compiler_params=_params("parallel", "parallel"),
    )(ya, yb, wo3, wo3, p, bg)


HEAD_ROWS = 256


def _head(merged, wo3, x, target, ln_g, ln_b, tm):
    t = x.shape[0]
    inv_d = 1.0 / D_MODEL

    def body(m_ref, w_ref, x_ref, t_ref, g_ref, b_ref, dr_ref, gx_ref, dm_ref, dg_ref, db_ref, loss_ref):
        i = pl.program_id(0)

        @pl.when(i == 0)
        def _():
            dg_ref[...] = jnp.zeros_like(dg_ref)
            db_ref[...] = jnp.zeros_like(db_ref)
            loss_ref[...] = jnp.zeros_like(loss_ref)

        w = w_ref[...]
        g = g_ref[...]
        for r0 in range(0, tm, HEAD_ROWS):
            rows = slice(r0, r0 + HEAD_ROWS)
            out = _dot(m_ref[rows, :], w)
            r = DN_ALPHA * x_ref[rows, :] + out
            mu = jnp.mean(r, axis=-1, keepdims=True)
            xc = r - mu
            var = jnp.mean(xc * xc, axis=-1, keepdims=True)
            rstd = lax.rsqrt(var + LN_EPS)
            xhat = xc * rstd
            e = xhat * g + b_ref[...] - t_ref[rows, :]
            se = jnp.sum(jnp.sum(e * e, axis=1, keepdims=True), axis=0, keepdims=True)
            loss_ref[...] += jnp.broadcast_to((0.5 * inv_d) * se, loss_ref.shape)
            dy = e * inv_d
            db_ref[...] += jnp.sum(dy, axis=0, keepdims=True)
            dg_ref[...] += jnp.sum(dy * xhat, axis=0, keepdims=True)
            dxh = dy * g
            m1 = jnp.mean(dxh, axis=-1, keepdims=True)
            m2 = jnp.mean(dxh * xhat, axis=-1, keepdims=True)
            dr = rstd * (dxh - m1 - xhat * m2)
            gx_ref[rows, :] = DN_ALPHA * dr
            drb = dr.astype(BF16)
            dr_ref[rows, :] = drb
            dm_ref[rows, :] = _dot(drb, w, 1, 1).astype(BF16)

    row = pl.BlockSpec((tm, D_MODEL), lambda i: (i, 0))
    vec = pl.BlockSpec((1, D_MODEL), lambda i: (0, 0))
    return pl.pallas_call(
        body, name="head", grid=(t // tm,),
        in_specs=[row, pl.BlockSpec((None, D_MODEL, D_MODEL), lambda i: (2, 0, 0)), row, row, vec, vec],
        out_specs=(row, row, row, vec, vec, pl.BlockSpec((1, 128), lambda i: (0, 0))),
        out_shape=(jax.ShapeDtypeStruct((t, D_MODEL), BF16), jax.ShapeDtypeStruct((t, D_MODEL), F32),
                   jax.ShapeDtypeStruct((t, D_MODEL), BF16), jax.ShapeDtypeStruct((1, D_MODEL), F32),
                   jax.ShapeDtypeStruct((1, D_MODEL), F32), jax.ShapeDtypeStruct((1, 128), F32)),
        compiler_params=_params("arbitrary"),
    )(merged, wo3, x, target, ln_g, ln_b)


def _gate_and_branch_backward(dmerged, oab, p, bg, wo3, tm):
    t = dmerged.shape[0]

    def body(dm_ref, oab_ref, g_ref, bg_ref, wa_ref, wb_ref, do_ref, dpg_ref, dbg_ref, dy_ref):
        @pl.when(pl.program_id(0) == 0)
        def _():
            dbg_ref[...] = jnp.zeros_like(dbg_ref)

        dm = dm_ref[...].astype(F32)
        for n, w_ref in enumerate((wa_ref, wb_ref)):
            gate = _sigmoid(g_ref[n].astype(F32) + bg_ref[n])
            d_o = (dm * gate).astype(BF16)
            do_ref[n] = d_o
            dgate = dm * oab_ref[n].astype(F32) * gate * (1.0 - gate)
            dpg_ref[n] = dgate.astype(BF16)
            dbg_ref[n] += jnp.sum(dgate, axis=0, keepdims=True)
            dy_ref[n] = _dot(d_o, w_ref[...], 1, 1).astype(BF16)

    pair = pl.BlockSpec((2, tm, D_MODEL), lambda i: (0, i, 0))
    gates = pl.BlockSpec((2, tm, D_MODEL), lambda i: (BLOCK_G, i, 0))
    vec = pl.BlockSpec((2, 1, D_MODEL), lambda i: (0, 0, 0))

    def weight(n):
        return pl.BlockSpec((None, D_MODEL, D_MODEL), lambda i: (n, 0, 0), pipeline_mode=pl.Buffered(1))

    pair_shape = jax.ShapeDtypeStruct((2, t, D_MODEL), BF16)
    return pl.pallas_call(
        body, name="gate_and_branch_backward", grid=(t // tm,),
        in_specs=[pl.BlockSpec((tm, D_MODEL), lambda i: (i, 0)), pair, gates, vec, weight(0), weight(1)],
        out_specs=(pair, gates, vec, pair),
        out_shape=(pair_shape, jax.ShapeDtypeStruct((N_SLOTS, t, D_MODEL), BF16),
                   jax.ShapeDtypeStruct((2, 1, D_MODEL), F32), pair_shape),
        compiler_params=_params("arbitrary"),
    )(dmerged, oab, p, bg, wo3, wo3)


def _weight_grad(a, a_sel, b, b_sel, tm, tn, tk, name):
    t = a.shape[-2]
    nk = t // tk

    def body(a_ref, b_ref, o_ref, acc_ref):
        k = pl.program_id(2)
        part = _dot(a_ref[...], b_ref[...], 0, 0)

        @pl.when(k == 0)
        def _():
            acc_ref[...] = part

        @pl.when(k > 0)
        def _():
            acc_ref[...] += part

        @pl.when(k == nk - 1)
        def _():
            o_ref[...] = acc_ref[...].astype(BF16)

    def spec(arr, sel, width, which):
        if arr.ndim == 2:
            return pl.BlockSpec((tk, width), lambda i, j, k: (k, (i, j)[which]))
        return pl.BlockSpec((None, tk, width), lambda i, j, k: (sel, k, (i, j)[which]))

    return pl.pallas_call(
        body, name=name, grid=(D_MODEL // tm, D_MODEL // tn, nk),
        in_specs=[spec(a, a_sel, tm, 0), spec(b, b_sel, tn, 1)],
        out_specs=pl.BlockSpec((tm, tn), lambda i, j, k: (i, j)),
        out_shape=jax.ShapeDtypeStruct((D_MODEL, D_MODEL), BF16),
        scratch_shapes=[pltpu.VMEM((tm, tn), F32)],
        compiler_params=_params("parallel", "parallel", "arbitrary"),
    )(a, b)


def _win_grad(xt, dp, tn, tk):
    _, t, _ = dp.shape
    nk = t // tk
    per_seg = D_MODEL // tn
    nj = N_SEG * per_seg

    def body(x_ref, dp_ref, o_ref, acc_ref):
        k = pl.program_id(1)
        part = _dot(x_ref[...], dp_ref[...])

        @pl.when(k == 0)
        def _():
            acc_ref[...] = part

        @pl.when(k > 0)
        def _():
            acc_ref[...] += part

        @pl.when(k == nk - 1)
        def _():
            o_ref[...] = acc_ref[...].astype(BF16)

    return pl.pallas_call(
        body, name="grad_w_in", grid=(nj, nk),
        in_specs=[pl.BlockSpec((D_MODEL, tk), lambda j, k: (0, k)),
                  pl.BlockSpec((None, tk, tn), lambda j, k: (_slot_of_seg(j // per_seg), k, j % per_seg))],
        out_specs=pl.BlockSpec((D_MODEL, tn), lambda j, k: (0, j)),
        out_shape=jax.ShapeDtypeStruct((D_MODEL, N_SEG * D_MODEL), BF16),
        scratch_shapes=[pltpu.VMEM((D_MODEL, tn), F32)],
        compiler_params=_params("parallel", "arbitrary"),
    )(xt, dp)


def _input_grad_and_scatter(dp, w_full, gx, s_in, s_o, small, tm):
    _, t, _ = dp.shape
    ni, nk = t // tm, N_SEG - 1
    _, hd, sc = s_in.shape
    hr = s_o.shape[2]

    def body(dp_ref, w_ref, gx_ref, si_ref, so_ref, sm_ref, o_ref, ri_ref, ro_ref, ga_ref,
             send_sems, recv_sems, local_sem):
        i, k = pl.program_id(0), pl.program_id(1)
        x, y, c, chips = _mesh_pos()
        me = 2 * x + y
        dev = 4 * x + 2 * y + c

        def peer(r):
            return (x ^ ((r >> 2) & 1), y ^ ((r >> 1) & 1), c ^ (r & 1))

        def sends():
            cps = []
            for q, (cx, cy) in enumerate(chips):
                dest = 2 * cx + cy
                cps.append(pltpu.make_async_remote_copy(src_ref=si_ref.at[dest], dst_ref=ri_ref.at[me],
                                                        send_sem=send_sems.at[q], recv_sem=recv_sems.at[q],
                                                        device_id=(cx, cy, c), device_id_type=MESH))
                cps.append(pltpu.make_async_remote_copy(src_ref=so_ref.at[:, dest], dst_ref=ro_ref.at[me],
                                                        send_sem=send_sems.at[3 + q], recv_sem=recv_sems.at[3 + q],
                                                        device_id=(cx, cy, c), device_id_type=MESH))
            for r in range(1, 8):
                cps.append(pltpu.make_async_remote_copy(src_ref=sm_ref, dst_ref=ga_ref.at[dev],
                                                        send_sem=send_sems.at[5 + r], recv_sem=recv_sems.at[5 + r],
                                                        device_id=peer(r), device_id_type=MESH))
            return cps

        own_small = pltpu.make_async_copy(sm_ref, ga_ref.at[dev], local_sem)

        @pl.when(jnp.logical_and(i == 0, k == 0))
        def _():
            for cp in sends():
                cp.start()
            own_small.start()

        part = _dot(dp_ref[...], w_ref[...], 1, 1)

        @pl.when(k == 0)
        def _():
            o_ref[...] = gx_ref[...] + part

        @pl.when(k > 0)
        def _():
            o_ref[...] += part

        @pl.when(jnp.logical_and(i == ni - 1, k == nk - 1))
        def _():
            for q, (cx, cy) in enumerate(chips):
                frm = 2 * cx + cy
                pltpu.make_async_remote_copy(src_ref=si_ref.at[frm], dst_ref=ri_ref.at[frm], send_sem=send_sems.at[q],
                                             recv_sem=recv_sems.at[q], device_id=(x, y, c),
                                             device_id_type=MESH).wait_recv()
                pltpu.make_async_remote_copy(src_ref=so_ref.at[:, frm], dst_ref=ro_ref.at[frm],
                                             send_sem=send_sems.at[3 + q], recv_sem=recv_sems.at[3 + q],
                                             device_id=(x, y, c), device_id_type=MESH).wait_recv()
            for r in range(1, 8):
                px, py, pc = peer(r)
                pltpu.make_async_remote_copy(src_ref=sm_ref, dst_ref=ga_ref.at[4 * px + 2 * py + pc],
                                             send_sem=send_sems.at[5 + r], recv_sem=recv_sems.at[5 + r],
                                             device_id=(x, y, c), device_id_type=MESH).wait_recv()
            for cp in sends():
                cp.wait_send()
            own_small.wait()

    return pl.pallas_call(
        body, name="grad_x_and_scatter", grid=(ni, nk),
        in_specs=[pl.BlockSpec((None, tm, D_MODEL), lambda i, k: (_slot_of_seg(k), i, 0)),
                  pl.BlockSpec((D_MODEL, D_MODEL), lambda i, k: (0, k)),
                  pl.BlockSpec((tm, D_MODEL), lambda i, k: (i, 0)), ANY, ANY, ANY],
        out_specs=(pl.BlockSpec((tm, D_MODEL), lambda i, k: (i, 0)), ANY, ANY, ANY),
        out_shape=(jax.ShapeDtypeStruct((t, D_MODEL), F32),
                   jax.ShapeDtypeStruct((N_CHIPS, hd, sc), BF16),
                   jax.ShapeDtypeStruct((N_CHIPS, 3, hr, D_MODEL), BF16),
                   jax.ShapeDtypeStruct((8,) + small.shape, small.dtype)),
        scratch_shapes=[pltpu.SemaphoreType.DMA((13,)), pltpu.SemaphoreType.DMA((13,)), pltpu.SemaphoreType.DMA],
        input_output_aliases={2: 0},
        compiler_params=pltpu.CompilerParams(dimension_semantics=("arbitrary", "arbitrary"),
                                             vmem_limit_bytes=VMEM_LIMIT, has_side_effects=True),
    )(dp, w_full, gx, s_in, s_o, small)


def _input_grad_tail(dp, w_full, partial, tm):
    _, t, _ = dp.shape
    seg = N_SEG - 1

    def body(dp_ref, w_ref, part_ref, o_ref):
        o_ref[...] = part_ref[...] + _dot(dp_ref[...], w_ref[...], 1, 1)

    row = pl.BlockSpec((tm, D_MODEL), lambda i: (i, 0))
    return pl.pallas_call(
        body, name="grad_x_tail", grid=(t // tm,),
        in_specs=[pl.BlockSpec((None, tm, D_MODEL), lambda i: (SLOT_OF_SEG[seg], i, 0)),
                  pl.BlockSpec((D_MODEL, D_MODEL), lambda i: (0, seg)), row],
        out_specs=row,
        out_shape=jax.ShapeDtypeStruct((t, D_MODEL), F32),
        compiler_params=_params("parallel"),
    )(dp, w_full, partial)


def _sgu_chunk_forward(u, v, z, wm, bs, lng, lnb):
    ug, dug = _gelu_and_grad(u)
    vg, dvg = _gelu_and_grad(v)
    mu = jnp.mean(vg, axis=-1, keepdims=True)
    xc = vg - mu
    var = jnp.mean(xc * xc, axis=-1, keepdims=True)
    rstd = lax.rsqrt(var + LN_EPS)
    vhat = xc * rstd
    vln = (vhat * lng + lnb).astype(BF16)
    mixed = _dot(wm, vln) + bs
    sig = _sigmoid(z)
    return ug, dug, dvg, rstd, vhat, vln, mixed, sig


def _mixer_a_forward(p_a, wm, bs_col, ln_v_g, ln_v_b, tm):
    t = p_a.shape[1]

    def body(p_ref, wm_ref, bs_ref, g_ref, b_ref, o_ref):
        wm_v, bs_v, lng, lnb = wm_ref[...], bs_ref[...], g_ref[...], b_ref[...]

        def chunk(ci, carry):
            rows = pl.ds(pl.multiple_of(ci * CHUNK, CHUNK), CHUNK)
            u = p_ref[0, rows, :].astype(F32)
            v = p_ref[1, rows, :].astype(F32)
            z = p_ref[2, rows, :].astype(F32)
            ug, _, _, _, _, _, mixed, sig = _sgu_chunk_forward(u, v, z, wm_v, bs_v, lng, lnb)
            o_ref[rows, :] = (ug * mixed * (z * sig)).astype(BF16)
            return carry

        lax.fori_loop(0, tm // CHUNK, chunk, 0, unroll=True)

    return pl.pallas_call(
        body, name="mixer_a_forward", grid=(t // tm, N_HEADS),
        in_specs=[pl.BlockSpec((3, tm, HEAD_DIM), lambda i, h: (0, i, h)),
                  pl.BlockSpec((None, CHUNK, CHUNK), lambda i, h: (h, 0, 0)),
                  pl.BlockSpec((None, CHUNK, 1), lambda i, h: (h, 0, 0)),
                  pl.BlockSpec((1, HEAD_DIM), lambda i, h: (0, h)),
                  pl.BlockSpec((1, HEAD_DIM), lambda i, h: (0, h))],
        out_specs=pl.BlockSpec((tm, HEAD_DIM), lambda i, h: (i, h)),
        out_shape=jax.ShapeDtypeStruct((t, D_MODEL), BF16),
        compiler_params=_params("parallel", "parallel"),
    )(p_a, wm, bs_col, ln_v_g, ln_v_b)


def _mixer_a_backward(p_a, dyab, wm, bs_col, ln_v_g, ln_v_b, dp, tm):
    t = p_a.shape[1]

    def body(p_ref, dy_ref, wm_ref, bs_ref, g_ref, b_ref, dp_in, dp_ref, dws_ref, dbs_ref, dg_ref, db_ref):
        @pl.when(pl.program_id(1) == 0)
        def _():
            dws_ref[...] = jnp.zeros_like(dws_ref)
            dbs_ref[...] = jnp.zeros_like(dbs_ref)
            dg_ref[...] = jnp.zeros_like(dg_ref)
            db_ref[...] = jnp.zeros_like(db_ref)

        wm_v, bs_v, lng, lnb = wm_ref[...], bs_ref[...], g_ref[...], b_ref[...]
        causal = (lax.broadcasted_iota(jnp.int32, (CHUNK, CHUNK), 1)
                  <= lax.broadcasted_iota(jnp.int32, (CHUNK, CHUNK), 0))

        def chunk(ci, carry):
            rows = pl.ds(pl.multiple_of(ci * CHUNK, CHUNK), CHUNK)
            u = p_ref[0, rows, :].astype(F32)
            v = p_ref[1, rows, :].astype(F32)
            z = p_ref[2, rows, :].astype(F32)
            dy = dy_ref[rows, :].astype(F32)
            ug, dug, dvg, rstd, vhat, vln, mixed, sig = _sgu_chunk_forward(u, v, z, wm_v, bs_v, lng, lnb)
            sz = z * sig
            dmixed = dy * ug * sz
            dp_ref[0, rows, :] = (dy * mixed * sz * dug).astype(BF16)
            dp_ref[2, rows, :] = (dy * ug * mixed * (sig * (1.0 + z * (1.0 - sig)))).astype(BF16)
            dbs_ref[...] += jnp.sum(dmixed, axis=1, keepdims=True)
            dmb = dmixed.astype(BF16)
            dws_ref[...] += jnp.where(causal, _dot(dmb, vln, 1, 1), 0.0)
            dvln = _dot(wm_v, dmb, 0, 0)
            db_ref[...] += jnp.sum(dvln, axis=0, keepdims=True)
            dg_ref[...] += jnp.sum(dvln * vhat, axis=0, keepdims=True)
            dvh = dvln * lng
            m1 = jnp.mean(dvh, axis=-1, keepdims=True)
            m2 = jnp.mean(dvh * vhat, axis=-1, keepdims=True)
            dp_ref[1, rows, :] = (rstd * (dvh - m1 - vhat * m2) * dvg).astype(BF16)
            return carry

        lax.fori_loop(0, tm // CHUNK, chunk, 0, unroll=True)

    return pl.pallas_call(
        body, name="mixer_a_backward", grid=(N_HEADS, t // tm),
        in_specs=[pl.BlockSpec((3, tm, HEAD_DIM), lambda h, i: (0, i, h)),
                  pl.BlockSpec((None, tm, HEAD_DIM), lambda h, i: (0, i, h)),
                  pl.BlockSpec((None, CHUNK, CHUNK), lambda h, i: (h, 0, 0)),
                  pl.BlockSpec((None, CHUNK, 1), lambda h, i: (h, 0, 0)),
                  pl.BlockSpec((1, HEAD_DIM), lambda h, i: (0, h)),
                  pl.BlockSpec((1, HEAD_DIM), lambda h, i: (0, h)), ANY],
        out_specs=(pl.BlockSpec((3, tm, HEAD_DIM), lambda h, i: (BLOCK_A, i, h)),
                   pl.BlockSpec((None, CHUNK, CHUNK), lambda h, i: (h, 0, 0)),
                   pl.BlockSpec((None, CHUNK, 1), lambda h, i: (h, 0, 0)),
                   pl.BlockSpec((1, HEAD_DIM), lambda h, i: (0, h)),
                   pl.BlockSpec((1, HEAD_DIM), lambda h, i: (0, h))),
        out_shape=(jax.ShapeDtypeStruct(dp.shape, BF16),
                   jax.ShapeDtypeStruct((N_HEADS, CHUNK, CHUNK), F32),
                   jax.ShapeDtypeStruct((N_HEADS, CHUNK, 1), F32),
                   jax.ShapeDtypeStruct((1, D_MODEL), F32), jax.ShapeDtypeStruct((1, D_MODEL), F32)),
        input_output_aliases={6: 0},
        compiler_params=_params("parallel", "arbitrary"),
    )(p_a, dyab, wm, bs_col, ln_v_g, ln_v_b, dp)


HALO = 16


def _conv_taps(h, halo_h, tm):
    row = lax.broadcasted_iota(jnp.int32, h.shape, 0)
    last1 = halo_h[HALO - 1:HALO, :]
    last2 = halo_h[HALO - 2:HALO - 1, :]
    h1 = jnp.where(row == 0, last1, pltpu.roll(h, 1, 0))
    h2 = jnp.where(row == 0, last2, jnp.where(row == 1, last1, pltpu.roll(h, 2, 0)))
    return h1, h2


def _mixer_b_forward(p_b, conv_w, conv_b, tm, tc):
    t = p_b.shape[1]

    def body(p_ref, halo_ref, w_ref, b_ref, o_ref):
        valid = (pl.program_id(1) > 0).astype(F32)
        h = p_ref[1].astype(F32) * p_ref[0].astype(F32)
        halo_h = halo_ref[1].astype(F32) * halo_ref[0].astype(F32) * valid
        h1, h2 = _conv_taps(h, halo_h, tm)
        w = w_ref[...]
        conv = b_ref[...] + w[0:1, :] * h2 + w[1:2, :] * h1 + w[2:3, :] * h
        z = p_ref[3].astype(F32)
        o_ref[...] = (p_ref[2].astype(F32) * conv * (z * _sigmoid(z))).astype(BF16)

    steps = tm // HALO
    return pl.pallas_call(
        body, name="mixer_b_forward", grid=(D_MODEL // tc, t // tm),
        in_specs=[pl.BlockSpec((4, tm, tc), lambda j, i: (BLOCK_B, i, j)),
                  pl.BlockSpec((4, HALO, tc), lambda j, i: (BLOCK_B, jnp.maximum(i * steps - 1, 0), j)),
                  pl.BlockSpec((3, tc), lambda j, i: (0, j)),
                  pl.BlockSpec((1, tc), lambda j, i: (0, j))],
        out_specs=pl.BlockSpec((tm, tc), lambda j, i: (i, j)),
        out_shape=jax.ShapeDtypeStruct((t, D_MODEL), BF16),
        compiler_params=_params("parallel", "parallel"),
    )(p_b, p_b, conv_w, conv_b)


def _mixer_b_backward(p_b, dyab, conv_w, conv_b, dp, tm, tc):
    t = p_b.shape[1]
    n = t // tm

    def body(p_ref, halo_ref, dy_ref, w_ref, b_ref, dp_in, dp_ref, dw_ref, db_ref, next_ref):
        ii = pl.program_id(1)

        @pl.when(ii == 0)
        def _():
            dw_ref[...] = jnp.zeros_like(dw_ref)
            db_ref[...] = jnp.zeros_like(db_ref)
            next_ref[...] = jnp.zeros_like(next_ref)

        valid = (ii < n - 1).astype(F32)
        xb = p_ref[0].astype(F32)
        cb = p_ref[1].astype(F32)
        bb = p_ref[2].astype(F32)
        z = p_ref[3].astype(F32)
        h = cb * xb
        halo_h = halo_ref[1].astype(F32) * halo_ref[0].astype(F32) * valid
        h1, h2 = _conv_taps(h, halo_h, tm)
        w = w_ref[...]
        w0, w1, w2 = w[0:1, :], w[1:2, :], w[2:3, :]
        conv = b_ref[...] + w0 * h2 + w1 * h1 + w2 * h
        sig = _sigmoid(z)
        sz = z * sig
        dy = dy_ref[...].astype(F32)
        dconv = dy * bb * sz
        dp_ref[2] = (dy * conv * sz).astype(BF16)
        dp_ref[3] = (dy * bb * conv * (sig * (1.0 + z * (1.0 - sig)))).astype(BF16)
        db_ref[...] += jnp.sum(dconv, axis=0, keepdims=True)
        dw_ref[0:1, :] += jnp.sum(dconv * h2, axis=0, keepdims=True)
        dw_ref[1:2, :] += jnp.sum(dconv * h1, axis=0, keepdims=True)
        dw_ref[2:3, :] += jnp.sum(dconv * h, axis=0, keepdims=True)
        row = lax.broadcasted_iota(jnp.int32, h.shape, 0)
        nxt = next_ref[...]
        n0, n1 = nxt[0:1, :], nxt[1:2, :]
        d1 = jnp.where(row == tm - 1, n0, pltpu.roll(dconv, tm - 1, 0))
        d2 = jnp.where(row == tm - 1, n1, jnp.where(row == tm - 2, n0, pltpu.roll(dconv, tm - 2, 0)))
        dh = w2 * dconv + w1 * d1 + w0 * d2
        dp_ref[0] = (dh * cb).astype(BF16)
        dp_ref[1] = (dh * xb).astype(BF16)
        next_ref[...] = dconv[0:8, :]

    steps = tm // HALO
    return pl.pallas_call(
        body, name="mixer_b_backward", grid=(D_MODEL // tc, n),
        in_specs=[pl.BlockSpec((4, tm, tc), lambda j, ii: (BLOCK_B, n - 1 - ii, j)),
                  pl.BlockSpec((4, HALO, tc), lambda j, ii: (BLOCK_B, jnp.maximum((n - 1 - ii) * steps - 1, 0), j)),
                  pl.BlockSpec((None, tm, tc), lambda j, ii: (1, n - 1 - ii, j)),
                  pl.BlockSpec((3, tc), lambda j, ii: (0, j)),
                  pl.BlockSpec((1, tc), lambda j, ii: (0, j)), ANY],
        out_specs=(pl.BlockSpec((4, tm, tc), lambda j, ii: (BLOCK_B, n - 1 - ii, j)),
                   pl.BlockSpec((3, tc), lambda j, ii: (0, j)),
                   pl.BlockSpec((1, tc), lambda j, ii: (0, j))),
        out_shape=(jax.ShapeDtypeStruct(dp.shape, BF16),
                   jax.ShapeDtypeStruct((3, D_MODEL), F32), jax.ShapeDtypeStruct((1, D_MODEL), F32)),
        scratch_shapes=[pltpu.VMEM((8, tc), F32)],
        input_output_aliases={5: 0},
        compiler_params=_params("parallel", "arbitrary"),
    )(p_b, p_b, dyab, conv_w, conv_b, dp)


def _adam_math(w, g, m, v):
    m = ADAM_B1 * m + (1.0 - ADAM_B1) * g
    v = ADAM_B2 * v + (1.0 - ADAM_B2) * (g * g)
    delta = -ADAM_LR * ((m * ADAM_C1) / (jnp.sqrt(v * ADAM_C2) + ADAM_EPS) + ADAM_WD * w)
    return delta, m, v


def _adam_rows(w, g, m, v, tm, name):
    r, c = w.shape

    def body(w_ref, g_ref, m_ref, v_ref, go_ref, d_ref, mo_ref, vo_ref):
        g = g_ref[...]
        d, mn, vn = _adam_math(w_ref[...], g, m_ref[...], v_ref[...])
        go_ref[...] = g
        d_ref[...] = d
        mo_ref[...] = mn
        vo_ref[...] = vn

    spec = pl.BlockSpec((tm, c), lambda i: (i, 0))
    shape = jax.ShapeDtypeStruct((r, c), F32)
    return pl.pallas_call(
        body, name=name, grid=(r // tm,),
        in_specs=[spec] * 4, out_specs=(spec,) * 4, out_shape=(shape,) * 4,
        compiler_params=_params("parallel"),
    )(w, g, m, v)


SMALL_ROW0 = {name: sum(r for _, r in SMALL_ROWS[:i]) for i, (name, _) in enumerate(SMALL_ROWS)}
LANE_MAJOR = ("ln_g", "ln_b", "b_gate", "ln_v_g", "ln_v_b", "conv_b")


def _lane_pieces(n):
    return [(q, slice(q * 128, (q + 1) * 128)) for q in range(n // 128)]


def _pack_small(d_ln_g, d_ln_b, d_bg, d_lnv_g, d_lnv_b, d_ws, d_bs, d_cw, d_cb, loss_part):
    def body(lg, lb, bg, vg, vb, ws, bs, cw, cb, loss, o_ref):
        def put(row0, vec):
            for q, cols in _lane_pieces(vec.shape[1]):
                o_ref[row0 + q:row0 + q + 1, :] = vec[:, cols]

        put(SMALL_ROW0["ln_g"], lg[...])
        put(SMALL_ROW0["ln_b"], lb[...])
        for n in range(2):
            put(SMALL_ROW0["b_gate"] + n * (D_MODEL // 128), bg[n])
        put(SMALL_ROW0["ln_v_g"], vg[...])
        put(SMALL_ROW0["ln_v_b"], vb[...])
        for h in range(N_HEADS):
            o_ref[SMALL_ROW0["w_s"] + h * CHUNK:SMALL_ROW0["w_s"] + (h + 1) * CHUNK, :] = ws[h]
        o_ref[SMALL_ROW0["b_s"]:SMALL_ROW0["b_s"] + N_HEADS, :] = bs[...]
        for c in range(3):
            put(SMALL_ROW0["conv_w"] + c * (D_MODEL // 128), cw[c:c + 1, :])
        put(SMALL_ROW0["conv_b"], cb[...])
        o_ref[SMALL_ROW0["loss"]:SMALL_ROW0["loss"] + 8, :] = jnp.broadcast_to(loss[...], (8, 128))

    return pl.pallas_call(
        body, name="pack_small", out_shape=jax.ShapeDtypeStruct((SMALL_TOTAL, 128), F32), compiler_params=_params(),
    )(d_ln_g, d_ln_b, d_bg, d_lnv_g, d_lnv_b, d_ws, d_bs.reshape(N_HEADS, CHUNK), d_cw, d_cb, loss_part)


def _adam_small(gathered, params):
    names = list(params)
    flat = [a for n in names for a in params[n]]

    def body(*refs):
        ga_ref = refs[0]
        ins = refs[1:1 + 3 * len(names)]
        outs = refs[1 + 3 * len(names):-1]
        gs_ref = refs[-1]
        g = ga_ref[0]
        for k in range(1, 8):
            g = g + ga_ref[k]
        gs_ref[...] = g
        for i, name in enumerate(names):
            w_ref, m_ref, v_ref = ins[3 * i:3 * i + 3]
            o_refs = outs[4 * i:4 * i + 4]
            row0 = SMALL_ROW0[name]
            if name in LANE_MAJOR:
                pieces = [((slice(None), cols), slice(row0 + q, row0 + q + 1))
                          for q, cols in _lane_pieces(w_ref.shape[1])]
            elif name == "w_s":
                pieces = [((0, h), slice(row0 + h * CHUNK, row0 + (h + 1) * CHUNK)) for h in range(N_HEADS)]
            else:
                pieces = [((0,), slice(row0, row0 + N_HEADS))]
            for idx, rows in pieces:
                gp = gs_ref[rows, :]
                res = (gp,) + _adam_math(w_ref[idx], gp, m_ref[idx], v_ref[idx])
                for o_ref, val in zip(o_refs, res):
                    o_ref[idx] = val
        gcw_ref, loss_ref = outs[-2:]
        for c in range(3):
            for q, cols in _lane_pieces(D_MODEL):
                r = SMALL_ROW0["conv_w"] + c * (D_MODEL // 128) + q
                gcw_ref[c:c + 1, cols] = gs_ref[r:r + 1, :]
        loss_ref[...] = gs_ref[SMALL_ROW0["loss"]:SMALL_ROW0["loss"] + 1, :]

    out_shape = [jax.ShapeDtypeStruct(params[n][0].shape, F32) for n in names for _ in range(4)]
    out_shape += [jax.ShapeDtypeStruct((3, D_MODEL), F32), jax.ShapeDtypeStruct((1, 128), F32)]
    res = pl.pallas_call(
        body, name="adam_small", out_shape=tuple(out_shape),
        scratch_shapes=[pltpu.VMEM((SMALL_TOTAL, 128), F32)], compiler_params=_params(),
    )(gathered, *flat)
    return {n: res[4 * i:4 * i + 4] for i, n in enumerate(names)}, res[-2], res[-1]


def _adam_conv_w(g_all, chip, w, m, v):
    cols = w.shape[2]

    def body(c_ref, g_ref, w_ref, m_ref, v_ref, go_ref, d_ref, mo_ref, vo_ref):
        g = g_ref[...]
        d, mn, vn = _adam_math(w_ref[...], g, m_ref[...], v_ref[...])
        go_ref[...] = g
        d_ref[...] = d
        mo_ref[...] = mn
        vo_ref[...] = vn

    own = pl.BlockSpec((None, 3, cols), lambda i, c_ref: (0, 0, 0))
    return pl.pallas_call(
        body, name="adam_conv_w",
        grid_spec=pltpu.PrefetchScalarGridSpec(
            num_scalar_prefetch=1, grid=(1,),
            in_specs=[pl.BlockSpec((3, cols), lambda i, c_ref: (0, c_ref[0])), own, own, own],
            out_specs=(own,) * 4),
        out_shape=(jax.ShapeDtypeStruct(w.shape, F32),) * 4,
        compiler_params=_params("arbitrary"),
    )(chip, g_all, w, m, v)


def kernel(x, w_in, b_gate, ln_v_g, ln_v_b, w_s, b_s, conv_w, conv_b, w_oa, w_ob, w_out, ln_g, ln_b, loss_target, m_w_in, m_b_gate, m_ln_v_g, m_ln_v_b, m_w_s, m_b_s, m_conv_w, m_conv_b, m_w_oa, m_w_ob, m_w_out, m_ln_g, m_ln_b, v_w_in, v_b_gate, v_ln_v_g, v_ln_v_b, v_w_s, v_b_s, v_conv_w, v_conv_b, v_w_oa, v_w_ob, v_w_out, v_ln_g, v_ln_b):
    t = x.shape[1]
    x2 = x[0]
    target = loss_target[0]
    chip = 2 * lax.axis_index("x") + lax.axis_index("y")
    conv_cols = conv_w.shape[2]

    chip1 = chip.astype(jnp.int32).reshape(1)
    w_pre = _cast_into_columns(w_in[0], chip1, N_CHIPS, 256, "cast_w_in")
    wo_b, wm = _prep_small_weights(w_oa[0], w_ob[0], w_out[0], w_s[0])
    conv_w8 = jnp.concatenate([conv_w[0], jnp.zeros((5, conv_cols), F32)], axis=0)
    bs_col = b_s[0].reshape(N_HEADS, CHUNK, 1)
    bg = b_gate.reshape(2, 1, D_MODEL)

    xb, xt = _cast_and_transpose(x2, min(512, t))
    p, w_full, wo_full, cw_full = _gather_and_project(xb, w_pre, wo_b, conv_w8, min(2048, t))
    wo3 = wo_full.reshape(3, D_MODEL, D_MODEL)
    conv_w_all = jnp.transpose(cw_full[:, :3, :], (1, 0, 2)).reshape(3, D_MODEL)
    tm_a = min(512, t)
    ya = _mixer_a_forward(p, wm, bs_col, ln_v_g, ln_v_b, tm_a)
    tm_b = min(512, t)
    yb = _mixer_b_forward(p, conv_w_all, conv_b, tm_b, 512)
    tm_m = min(1024, t)
    merged, oab = _merge_forward(ya, yb, wo3, p, bg, tm_m, 512)

    drb, gx, dmerged, d_ln_g, d_ln_b, loss_part = _head(merged, wo3, x2, target, ln_g, ln_b, min(256, t))
    doab, dp, d_bg, dyab = _gate_and_branch_backward(dmerged, oab, p, bg, wo3, min(256, t))
    dp, d_ws, d_bs, d_lnv_g, d_lnv_b = _mixer_a_backward(p, dyab, wm, bs_col, ln_v_g, ln_v_b, dp, tm_a)
    dp, d_cw, d_cb = _mixer_b_backward(p, dyab, conv_w_all, conv_b, dp, tm_b, 512)

    tk = min(2048, t)
    g_oa = _weight_grad(ya, 0, doab, 0, 1024, 1024, tk, "grad_w_oa")
    g_ob = _weight_grad(yb, 0, doab, 1, 1024, 1024, tk, "grad_w_ob")
    g_out = _weight_grad(merged, 0, drb, 0, 1024, 1024, tk, "grad_w_out")
    g_in = _win_grad(xt, dp, 1024, tk)

    r_in, *r_o = _exchange_halves(g_in, (g_oa, g_ob, g_out))
    s_in, s_o = _add_halves(g_in, (g_oa, g_ob, g_out), r_in, r_o)
    small_part = _pack_small(d_ln_g, d_ln_b, d_bg, d_lnv_g, d_lnv_b, d_ws, d_bs, d_cw, d_cb, loss_part)
    tm_x = min(512, t)
    gx, q_in, q_o, gathered = _input_grad_and_scatter(dp, w_full, gx, s_in, s_o, small_part, tm_x)
    grad_x = _input_grad_tail(dp, w_full, gx, tm_x)
    f_in, f_o = _sum_chips(q_in, q_o, s_in, s_o)
    gsum_in, gsum_o = _share_halves(f_in, f_o)

    big = {}
    big["w_in"] = _adam_rows(w_in[0], gsum_in, m_w_in[0], v_w_in[0], 128, "adam_w_in")
    for n, (name, w, m, v) in enumerate((("w_oa", w_oa, m_w_oa, v_w_oa), ("w_ob", w_ob, m_w_ob, v_w_ob),
                                         ("w_out", w_out, m_w_out, v_w_out))):
        big[name] = _adam_rows(w[0], gsum_o[n], m[0], v[0], 256, "adam_" + name)

    small, g_conv_w, loss_row = _adam_small(gathered, {
        "ln_g": (ln_g, m_ln_g, v_ln_g), "ln_b": (ln_b, m_ln_b, v_ln_b), "b_gate": (b_gate, m_b_gate, v_b_gate),
        "ln_v_g": (ln_v_g, m_ln_v_g, v_ln_v_g), "ln_v_b": (ln_v_b, m_ln_v_b, v_ln_v_b),
        "w_s": (w_s, m_w_s, v_w_s), "b_s": (b_s, m_b_s, v_b_s), "conv_b": (conv_b, m_conv_b, v_conv_b)})
    small["conv_w"] = _adam_conv_w(g_conv_w, chip1, conv_w, m_conv_w, v_conv_w)
    loss = loss_row[0, 0]

    order = ("w_in", "b_gate", "ln_v_g", "ln_v_b", "w_s", "b_s", "conv_w", "conv_b", "w_oa", "w_ob", "w_out",
             "ln_g", "ln_b")
    outs = [loss, grad_x[None]]
    for which in range(4):
        for name in order:
            outs.append(big[name][which][None] if name in big else small[name][which])
    return tuple(outs)
```

```python
import functools
import math

import jax
import jax.numpy as jnp
from jax import lax
from jax.experimental import pallas as pl
from jax.experimental.pallas import tpu as pltpu

F32 = jnp.float32
BF16 = jnp.bfloat16

D_MODEL = 2048
N_HEADS = 8
HEAD_DIM = D_MODEL // N_HEADS
CHUNK = 128
N_SEG = 9
N_CHIPS = 4
SHARD_COLS = N_SEG * D_MODEL // N_CHIPS
COL_BLOCK = 512
BLOCKS_PER_SHARD = SHARD_COLS // COL_BLOCK
BLOCKS_PER_SEG = D_MODEL // COL_BLOCK
SHARD_ROWS = D_MODEL // N_CHIPS
N_SLOTS = 12
BLOCK_A, BLOCK_G, BLOCK_B = 0, 2, 2
SLOT_OF_SEG = (0, 1, 2, 8, 9, 10, 11, 4, 5)
DN_ALPHA = 2.0 ** 0.25
LN_EPS = 1e-5
GELU_K = math.sqrt(2.0 / math.pi)
GELU_C = 0.044715

ADAM_LR = 0.001
ADAM_B1 = 0.9
ADAM_B2 = 0.999
ADAM_EPS = 1e-08
ADAM_WD = 0.01
ADAM_STEP = 10
ADAM_C1 = 1.0 / (1.0 - ADAM_B1 ** ADAM_STEP)
ADAM_C2 = 1.0 / (1.0 - ADAM_B2 ** ADAM_STEP)

VMEM_LIMIT = 60 * 1024 * 1024
MESH = pl.DeviceIdType.MESH
ANY = pl.BlockSpec(memory_space=pl.ANY)

SMALL_ROWS = (("ln_g", 16), ("ln_b", 16), ("b_gate", 32), ("ln_v_g", 16), ("ln_v_b", 16),
              ("w_s", 1024), ("b_s", 8), ("conv_w", 48), ("conv_b", 16), ("loss", 8))
SMALL_TOTAL = sum(r for _, r in SMALL_ROWS)


def _params(*sem):
    return pltpu.CompilerParams(dimension_semantics=sem, vmem_limit_bytes=VMEM_LIMIT)


def _sigmoid(x):
    return 1.0 / (1.0 + jnp.exp(-x))


def _gelu_and_grad(x):
    x2 = x * x
    th = jnp.tanh(GELU_K * (x + GELU_C * x * x2))
    g = 0.5 * x * (1.0 + th)
    dg = 0.5 * (1.0 + th) + 0.5 * x * (1.0 - th * th) * (GELU_K * (1.0 + 3.0 * GELU_C * x2))
    return g, dg


def _gelu(x):
    return 0.5 * x * (1.0 + jnp.tanh(GELU_K * (x + GELU_C * x * x * x)))


def _dot(a, b, ca=1, cb=0):
    return lax.dot_general(a, b, (((ca,), (cb,)), ((), ())), preferred_element_type=F32)


def _cast_and_transpose(x, tm):
    t, d = x.shape

    def body(x_ref, o_ref, ot_ref):
        v = x_ref[...]
        o_ref[...] = v.astype(BF16)
        ot_ref[...] = v.T.astype(BF16)

    return pl.pallas_call(
        body, name="cast_x", grid=(t // tm,),
        in_specs=[pl.BlockSpec((tm, d), lambda i: (i, 0))],
        out_specs=(pl.BlockSpec((tm, d), lambda i: (i, 0)), pl.BlockSpec((d, tm), lambda i: (0, i))),
        out_shape=(jax.ShapeDtypeStruct((t, d), BF16), jax.ShapeDtypeStruct((d, t), BF16)),
        compiler_params=_params("parallel"),
    )(x)


def _cast_into_columns(w, slot, n_slots, tm, name):
    r, c = w.shape

    def body(s_ref, w_ref, o_ref):
        o_ref[...] = w_ref[...].astype(BF16)

    return pl.pallas_call(
        body, name=name,
        grid_spec=pltpu.PrefetchScalarGridSpec(
            num_scalar_prefetch=1, grid=(r // tm,),
            in_specs=[pl.BlockSpec((tm, c), lambda i, s_ref: (i, 0))],
            out_specs=pl.BlockSpec((tm, c), lambda i, s_ref: (i, s_ref[0]))),
        out_shape=jax.ShapeDtypeStruct((r, n_slots * c), BF16),
        compiler_params=_params("parallel"),
    )(slot, w)


def _prep_small_weights(w_oa, w_ob, w_out, w_s):
    rows = w_oa.shape[0]

    def body(a_ref, b_ref, c_ref, ws_ref, wo_ref, wm_ref):
        wo_ref[0] = a_ref[...].astype(BF16)
        wo_ref[1] = b_ref[...].astype(BF16)
        wo_ref[2] = c_ref[...].astype(BF16)
        t = lax.broadcasted_iota(jnp.int32, (CHUNK, CHUNK), 0)
        s = lax.broadcasted_iota(jnp.int32, (CHUNK, CHUNK), 1)
        for h in range(N_HEADS):
            wm_ref[h] = jnp.where(s <= t, ws_ref[h], 0.0).astype(BF16)

    return pl.pallas_call(
        body, name="prep_small_weights",
        out_shape=(jax.ShapeDtypeStruct((3, rows, D_MODEL), BF16),
                   jax.ShapeDtypeStruct((N_HEADS, CHUNK, CHUNK), BF16)),
        compiler_params=_params(),
    )(w_oa, w_ob, w_out, w_s)


def _mesh_pos():
    x, y, c = lax.axis_index("x"), lax.axis_index("y"), lax.axis_index("c")
    chips = [(1 - x, y), (x, 1 - y), (1 - x, 1 - y)]
    return x, y, c, chips


def _slot_of_seg(seg):
    return jnp.where(seg < 3, seg, jnp.where(seg < 7, seg + 5, seg - 3))


def _gather_and_project(xb, w_pre, wo_b, conv_w8, tm):
    t = xb.shape[0]
    d, sc = w_pre.shape[0], w_pre.shape[1] // N_CHIPS
    rows = wo_b.shape[1]
    hd, hr = d // 2, rows // 2
    ni, nj = t // tm, BLOCKS_PER_SHARD
    total = N_CHIPS * ni * nj
    mx, my = lax.axis_index("x"), lax.axis_index("y")
    order = jnp.stack([2 * mx + my, 2 * (1 - mx) + my, 2 * mx + (1 - my),
                       2 * (1 - mx) + (1 - my)]).astype(jnp.int32)

    def body(order_ref, x_ref, wpre_ref, wo_ref, cw_ref, p_ref, wf_ref, wof_ref, cwf_ref,
             wbuf, wsem, send_sems, recv_sems, local_sems):
        x, y, c, chips = _mesh_pos()
        me = 2 * x + y
        sibling = (x, y, 1 - c)
        s, i, j = pl.program_id(0), pl.program_id(1), pl.program_id(2)
        n = (s * ni + i) * nj + j

        def shard_cols(k):
            return pl.ds(pl.multiple_of(k * sc, COL_BLOCK), sc)

        def half(kind, k, cc):
            if kind == 0:
                return wf_ref.at[pl.ds(cc * hd, hd), shard_cols(k)]
            return wof_ref.at[:, k, pl.ds(cc * hr, hr), :]

        def src_half(kind, cc):
            if kind == 0:
                return wpre_ref.at[pl.ds(cc * hd, hd), shard_cols(me)]
            return wo_ref.at[:, pl.ds(cc * hr, hr), :]

        def rcopy(sem, src, dst, to):
            return pltpu.make_async_remote_copy(src_ref=src, dst_ref=dst, send_sem=send_sems.at[sem],
                                                recv_sem=recv_sems.at[sem], device_id=to, device_id_type=MESH)

        def ici_send(kind, q):
            return rcopy(kind * 6 + q, src_half(kind, c), half(kind, me, c), (*chips[q], c))

        def ici_landed(kind, q):
            k = 2 * chips[q][0] + chips[q][1]
            return rcopy(kind * 6 + q, src_half(kind, c), half(kind, k, c), sibling)

        def forward(kind, q, cc):
            k = 2 * chips[q][0] + chips[q][1]
            return rcopy(kind * 6 + 3 + q, half(kind, k, cc), half(kind, k, cc), sibling)

        def conv_send(q):
            return rcopy(12 + q, cw_ref, cwf_ref.at[me], (*chips[q], c))

        def local_copies():
            return [pltpu.make_async_copy(wo_ref, wof_ref.at[:, me], local_sems.at[0]),
                    pltpu.make_async_copy(cw_ref, cwf_ref.at[me], local_sems.at[1])]

        def tile_start(s_, j_, slot):
            g = order_ref[s_] * BLOCKS_PER_SHARD + j_
            cols = pl.ds(pl.multiple_of(g * COL_BLOCK, COL_BLOCK), COL_BLOCK)

            @pl.when(s_ == 0)
            def _():
                pltpu.make_async_copy(wpre_ref.at[:, cols], wbuf.at[slot], wsem.at[slot]).start()

            @pl.when(s_ > 0)
            def _():
                pltpu.make_async_copy(wf_ref.at[:, cols], wbuf.at[slot], wsem.at[slot]).start()

        def at(s_, i_, j_):
            return jnp.logical_and(s == s_, jnp.logical_and(i == i_, j == j_))

        @pl.when(n == 0)
        def _():
            for cp in local_copies():
                cp.start()
            for q in (0, 1):
                ici_send(0, q).start()
            for q in range(3):
                conv_send(q).start()
            tile_start(0, 0, 0)

        @pl.when(at(0, ni - 1, nj - 1))
        def _():
            for q in (0, 1):
                ici_landed(0, q).wait_recv()
                forward(0, q, c).start()
            for q in (0, 1):
                ici_send(0, q).wait_send()
            ici_send(0, 2).start()
            forward(0, 0, 1 - c).wait_recv()

        @pl.when(at(1, ni - 1, nj - 1))
        def _():
            forward(0, 1, 1 - c).wait_recv()

        @pl.when(at(2, (3 * ni) // 4, 0))
        def _():
            ici_landed(0, 2).wait_recv()
            forward(0, 2, c).start()
            ici_send(0, 2).wait_send()
            for q in range(3):
                ici_send(1, q).start()

        @pl.when(at(2, ni - 1, nj - 1))
        def _():
            forward(0, 2, 1 - c).wait_recv()

        n1 = n + 1

        @pl.when(n1 < total)
        def _():
            tile_start(n1 // (ni * nj), lax.rem(n1, nj), lax.rem(n1, 2))

        slot = lax.rem(n, 2)
        pltpu.make_async_copy(wpre_ref.at[:, pl.ds(0, COL_BLOCK)], wbuf.at[slot], wsem.at[slot]).wait()
        p_ref[...] = _dot(x_ref[...], wbuf[slot]).astype(BF16)

        @pl.when(n == total - 1)
        def _():
            for q in range(3):
                ici_landed(1, q).wait_recv()
                forward(1, q, c).start()
            for q in range(3):
                forward(1, q, 1 - c).wait_recv()
                rcopy(12 + q, cw_ref, cwf_ref.at[2 * chips[q][0] + chips[q][1]], sibling).wait_recv()
            for q in range(3):
                ici_send(1, q).wait_send()
                forward(0, q, c).wait_send()
                forward(1, q, c).wait_send()
                conv_send(q).wait_send()
            for cp in local_copies():
                cp.wait()

    def p_map(s, i, j, o):
        g = o[s] * BLOCKS_PER_SHARD + j
        return (_slot_of_seg(g // BLOCKS_PER_SEG), i, lax.rem(g, BLOCKS_PER_SEG))

    return pl.pallas_call(
        body, name="gather_and_project",
        grid_spec=pltpu.PrefetchScalarGridSpec(
            num_scalar_prefetch=1, grid=(N_CHIPS, ni, nj),
            in_specs=[pl.BlockSpec((tm, D_MODEL), lambda s, i, j, o: (i, 0)), ANY, ANY, ANY],
            out_specs=(pl.BlockSpec((None, tm, COL_BLOCK), p_map), ANY, ANY, ANY),
            scratch_shapes=[pltpu.VMEM((2, d, COL_BLOCK), BF16), pltpu.SemaphoreType.DMA((2,)),
                            pltpu.SemaphoreType.DMA((15,)), pltpu.SemaphoreType.DMA((15,)),
                            pltpu.SemaphoreType.DMA((2,))]),
        out_shape=(jax.ShapeDtypeStruct((N_SLOTS, t, D_MODEL), BF16),
                   jax.ShapeDtypeStruct((d, N_CHIPS * sc), BF16),
                   jax.ShapeDtypeStruct((3, N_CHIPS, rows, D_MODEL), BF16),
                   jax.ShapeDtypeStruct((N_CHIPS,) + conv_w8.shape, F32)),
        input_output_aliases={2: 1},
        compiler_params=pltpu.CompilerParams(dimension_semantics=("arbitrary",) * 3, vmem_limit_bytes=VMEM_LIMIT,
                                             has_side_effects=True),
    )(order, xb, w_pre, wo_b, conv_w8)


def _exchange_halves(g_in, g_o):
    d, c9 = g_in.shape
    hd = d // 2
    hr = SHARD_ROWS // 2
    g_o4 = [g.reshape(N_CHIPS, 2, hr, D_MODEL) for g in g_o]

    def body(gi_ref, ga_ref, gb_ref, gc_ref, ri_ref, ra_ref, rb_ref, rc_ref, send_sems, recv_sems):
        x, y, c, _ = _mesh_pos()
        sibling = (x, y, 1 - c)
        cps = [pltpu.make_async_remote_copy(src_ref=gi_ref.at[pl.ds((1 - c) * hd, hd), :], dst_ref=ri_ref,
                                            send_sem=send_sems.at[0], recv_sem=recv_sems.at[0],
                                            device_id=sibling, device_id_type=MESH)]
        for n, (g_ref, r_ref) in enumerate(((ga_ref, ra_ref), (gb_ref, rb_ref), (gc_ref, rc_ref))):
            cps.append(pltpu.make_async_remote_copy(src_ref=g_ref.at[:, 1 - c], dst_ref=r_ref,
                                                    send_sem=send_sems.at[1 + n], recv_sem=recv_sems.at[1 + n],
                                                    device_id=sibling, device_id_type=MESH))
        for cp in cps:
            cp.start()
        for cp in cps:
            cp.wait()

    o_shape = jax.ShapeDtypeStruct((N_CHIPS, hr, D_MODEL), BF16)
    return pl.pallas_call(
        body, name="rs_exchange_halves",
        in_specs=[ANY] * 4, out_specs=(ANY,) * 4,
        out_shape=(jax.ShapeDtypeStruct((hd, c9), BF16), o_shape, o_shape, o_shape),
        scratch_shapes=[pltpu.SemaphoreType.DMA((4,)), pltpu.SemaphoreType.DMA((4,))],
        compiler_params=pltpu.CompilerParams(has_side_effects=True),
    )(g_in, *g_o4)


def _add_halves(g_in, g_o, r_in, r_o):
    d, c9 = g_in.shape
    hd = d // 2
    hr = SHARD_ROWS // 2
    core = lax.axis_index("c").astype(jnp.int32).reshape(1)
    tm = min(512, hd)
    nb = hd // tm

    def body_in(c_ref, g_ref, r_ref, o_ref):
        o_ref[...] = (g_ref[...].astype(F32) + r_ref[...].astype(F32)).astype(BF16)

    s_in = pl.pallas_call(
        body_in, name="rs_add_halves_in",
        grid_spec=pltpu.PrefetchScalarGridSpec(
            num_scalar_prefetch=1, grid=(N_CHIPS, nb),
            in_specs=[pl.BlockSpec((tm, SHARD_COLS), lambda k, i, c_ref: (c_ref[0] * nb + i, k)),
                      pl.BlockSpec((tm, SHARD_COLS), lambda k, i, c_ref: (i, k))],
            out_specs=pl.BlockSpec((None, tm, SHARD_COLS), lambda k, i, c_ref: (k, i, 0))),
        out_shape=jax.ShapeDtypeStruct((N_CHIPS, hd, SHARD_COLS), BF16),
        compiler_params=_params("parallel", "parallel"),
    )(core, g_in, r_in)

    g_o4 = [g.reshape(N_CHIPS, 2, hr, D_MODEL) for g in g_o]

    def body_o(c_ref, ga_ref, gb_ref, gc_ref, ra_ref, rb_ref, rc_ref, o_ref):
        for n, (g_ref, r_ref) in enumerate(((ga_ref, ra_ref), (gb_ref, rb_ref), (gc_ref, rc_ref))):
            o_ref[n] = (g_ref[...].astype(F32) + r_ref[...].astype(F32)).astype(BF16)

    gspec = pl.BlockSpec((None, None, hr, D_MODEL), lambda k, c_ref: (k, c_ref[0], 0, 0))
    rspec = pl.BlockSpec((None, hr, D_MODEL), lambda k, c_ref: (k, 0, 0))
    s_o = pl.pallas_call(
        body_o, name="rs_add_halves_o",
        grid_spec=pltpu.PrefetchScalarGridSpec(
            num_scalar_prefetch=1, grid=(N_CHIPS,),
            in_specs=[gspec] * 3 + [rspec] * 3,
            out_specs=pl.BlockSpec((3, None, hr, D_MODEL), lambda k, c_ref: (0, k, 0, 0))),
        out_shape=jax.ShapeDtypeStruct((3, N_CHIPS, hr, D_MODEL), BF16),
        compiler_params=_params("parallel"),
    )(core, *g_o4, *r_o)
    return s_in, s_o


def _sum_chips(r_in, r_o, s_in, s_o):
    _, hd, sc = r_in.shape
    hr = r_o.shape[2]
    tm = min(256, hd)
    nb = hd // tm
    pos = jnp.stack([2 * lax.axis_index("x") + lax.axis_index("y"), lax.axis_index("c")]).astype(jnp.int32)

    def chip_sum(pos_ref, r_ref, s_ref):
        acc = None
        for k in range(N_CHIPS):
            term = jnp.where(pos_ref[0] == k, s_ref[...], r_ref[k]).astype(F32)
            acc = term if acc is None else acc + term
        return acc

    def body_in(pos_ref, r_ref, s_ref, o_ref):
        o_ref[...] = chip_sum(pos_ref, r_ref, s_ref)

    f_in = pl.pallas_call(
        body_in, name="rs_sum_chips_in",
        grid_spec=pltpu.PrefetchScalarGridSpec(
            num_scalar_prefetch=1, grid=(nb,),
            in_specs=[pl.BlockSpec((N_CHIPS, tm, sc), lambda i, p: (0, i, 0)),
                      pl.BlockSpec((None, tm, sc), lambda i, p: (p[0], i, 0))],
            out_specs=pl.BlockSpec((tm, sc), lambda i, p: (p[1] * nb + i, 0))),
        out_shape=jax.ShapeDtypeStruct((2 * hd, sc), F32),
        compiler_params=_params("parallel"),
    )(pos, r_in, s_in)

    def body_o(pos_ref, r_ref, s_ref, o_ref):
        o_ref[...] = chip_sum(pos_ref, r_ref, s_ref)

    f_o = pl.pallas_call(
        body_o, name="rs_sum_chips_o",
        grid_spec=pltpu.PrefetchScalarGridSpec(
            num_scalar_prefetch=1, grid=(3,),
            in_specs=[pl.BlockSpec((N_CHIPS, None, hr, D_MODEL), lambda n, p: (0, n, 0, 0)),
                      pl.BlockSpec((None, None, hr, D_MODEL), lambda n, p: (n, p[0], 0, 0))],
            out_specs=pl.BlockSpec((None, hr, D_MODEL), lambda n, p: (n, p[1], 0))),
        out_shape=jax.ShapeDtypeStruct((3, 2 * hr, D_MODEL), F32),
        compiler_params=_params("parallel"),
    )(pos, r_o, s_o)
    return f_in, f_o


def _share_halves(f_in, f_o):
    hd, sc = f_in.shape[0] // 2, f_in.shape[1]
    hr = f_o.shape[1] // 2

    def body(fi_ref, fo_ref, gi_ref, go_ref, send_sems, recv_sems):
        x, y, c, _ = _mesh_pos()
        sibling = (x, y, 1 - c)

        def halves(cc):
            rows_i, rows_o = pl.ds(cc * hd, hd), pl.ds(cc * hr, hr)
            return (fi_ref.at[rows_i, :], gi_ref.at[rows_i, :]), (fo_ref.at[:, rows_o, :], go_ref.at[:, rows_o, :])

        def copies(cc):
            return [pltpu.make_async_remote_copy(src_ref=src, dst_ref=dst, send_sem=send_sems.at[n],
                                                 recv_sem=recv_sems.at[n], device_id=sibling, device_id_type=MESH)
                    for n, (src, dst) in enumerate(halves(cc))]

        sends = copies(c)
        for cp in sends:
            cp.start()
        for cp in copies(1 - c):
            cp.wait_recv()
        for cp in sends:
            cp.wait_send()

    return pl.pallas_call(
        body, name="rs_share_halves",
        in_specs=[ANY, ANY], out_specs=(ANY, ANY),
        out_shape=(jax.ShapeDtypeStruct(f_in.shape, F32), jax.ShapeDtypeStruct(f_o.shape, F32)),
        scratch_shapes=[pltpu.SemaphoreType.DMA((2,)), pltpu.SemaphoreType.DMA((2,))],
        input_output_aliases={0: 0, 1: 1},
        compiler_params=pltpu.CompilerParams(has_side_effects=True),
    )(f_in, f_o)


def _merge_forward(ya, yb, wo3, p, bg, tm, tn):
    t = ya.shape[0]

    def body(ya_ref, yb_ref, wa_ref, wb_ref, g_ref, bg_ref, m_ref, oab_ref):
        oa = _dot(ya_ref[...], wa_ref[...])
        ob = _dot(yb_ref[...], wb_ref[...])
        ga = _sigmoid(g_ref[0].astype(F32) + bg_ref[0])
        gb = _sigmoid(g_ref[1].astype(F32) + bg_ref[1])
        m_ref[...] = (ga * oa + gb * ob).astype(BF16)
        oab_ref[0] = oa.astype(BF16)
        oab_ref[1] = ob.astype(BF16)

    return pl.pallas_call(
        body, name="merge_forward", grid=(t // tm, D_MODEL // tn),
        in_specs=[pl.BlockSpec((tm, D_MODEL), lambda i, j: (i, 0)),
                  pl.BlockSpec((tm, D_MODEL), lambda i, j: (i, 0)),
                  pl.BlockSpec((None, D_MODEL, tn), lambda i, j: (0, 0, j)),
                  pl.BlockSpec((None, D_MODEL, tn), lambda i, j: (1, 0, j)),
                  pl.BlockSpec((2, tm, tn), lambda i, j: (BLOCK_G, i, j)),
                  pl.BlockSpec((2, 1, tn), lambda i, j: (0, 0, j))],
        out_specs=(pl.BlockSpec((tm, tn), lambda i, j: (i, j)),
                   pl.BlockSpec((2, tm, tn), lambda i, j: (0, i, j))),
        out_shape=(jax.ShapeDtypeStruct((t, D_MODEL), BF16), jax.ShapeDtypeStruct((2, t, D_MODEL), BF16)),
        compiler_params=_params("parallel", "parallel"),
    )(ya, yb, wo3, wo3, p, bg)


HEAD_ROWS = 256


def _head(merged, wo3, x, target, ln_g, ln_b, tm):
    t = x.shape[0]
    inv_d = 1.0 / D_MODEL

    def body(m_ref, w_ref, x_ref, t_ref, g_ref, b_ref, dr_ref, gx_ref, dm_ref, dg_ref, db_ref, loss_ref):
        i = pl.program_id(0)

        @pl.when(i == 0)
        def _():
            dg_ref[...] = jnp.zeros_like(dg_ref)
            db_ref[...] = jnp.zeros_like(db_ref)
            loss_ref[...] = jnp.zeros_like(loss_ref)

        w = w_ref[...]
        g = g_ref[...]
        tiles = [slice(r0, r0 + HEAD_ROWS) for r0 in range(0, tm, HEAD_ROWS)]
        firsts = [_dot(m_ref[rows, :], w) for rows in tiles]
        for rows, out in zip(tiles, firsts):
            r = DN_ALPHA * x_ref[rows, :] + out
            mu = jnp.mean(r, axis=-1, keepdims=True)
            xc = r - mu
            var = jnp.mean(xc * xc, axis=-1, keepdims=True)
            rstd = lax.rsqrt(var + LN_EPS)
            xhat = xc * rstd
            e = xhat * g + b_ref[...] - t_ref[rows, :]
            se = jnp.sum(jnp.sum(e * e, axis=1, keepdims=True), axis=0, keepdims=True)
            loss_ref[...] += jnp.broadcast_to((0.5 * inv_d) * se, loss_ref.shape)
            dy = e * inv_d
            db_ref[...] += jnp.sum(dy, axis=0, keepdims=True)
            dg_ref[...] += jnp.sum(dy * xhat, axis=0, keepdims=True)
            dxh = dy * g
            m1 = jnp.mean(dxh, axis=-1, keepdims=True)
            m2 = jnp.mean(dxh * xhat, axis=-1, keepdims=True)
            dr = rstd * (dxh - m1 - xhat * m2)
            gx_ref[rows, :] = DN_ALPHA * dr
            drb = dr.astype(BF16)
            dr_ref[rows, :] = drb
            dm_ref[rows, :] = _dot(drb, w, 1, 1).astype(BF16)

    row = pl.BlockSpec((tm, D_MODEL), lambda i: (i, 0))
    vec = pl.BlockSpec((1, D_MODEL), lambda i: (0, 0))
    return pl.pallas_call(
        body, name="head", grid=(t // tm,),
        in_specs=[row, pl.BlockSpec((None, D_MODEL, D_MODEL), lambda i: (2, 0, 0), pipeline_mode=pl.Buffered(1)),
                  row, row, vec, vec],
        out_specs=(row, row, row, vec, vec, pl.BlockSpec((1, 128), lambda i: (0, 0))),
        out_shape=(jax.ShapeDtypeStruct((t, D_MODEL), BF16), jax.ShapeDtypeStruct((t, D_MODEL), F32),
                   jax.ShapeDtypeStruct((t, D_MODEL), BF16), jax.ShapeDtypeStruct((1, D_MODEL), F32),
                   jax.ShapeDtypeStruct((1, D_MODEL), F32), jax.ShapeDtypeStruct((1, 128), F32)),
        compiler_params=_params("arbitrary"),
    )(merged, wo3, x, target, ln_g, ln_b)


def _gate_and_branch_backward(dmerged, oab, p, bg, wo3, tm):
    t = dmerged.shape[0]

    def body(dm_ref, oab_ref, g_ref, bg_ref, wa_ref, wb_ref, do_ref, dpg_ref, dbg_ref, dy_ref):
        @pl.when(pl.program_id(0) == 0)
        def _():
            dbg_ref[...] = jnp.zeros_like(dbg_ref)

        dm = dm_ref[...].astype(F32)
        for n, w_ref in enumerate((wa_ref, wb_ref)):
            gate = _sigmoid(g_ref[n].astype(F32) + bg_ref[n])
            d_o = (dm * gate).astype(BF16)
            do_ref[n] = d_o
            dgate = dm * oab_ref[n].astype(F32) * gate * (1.0 - gate)
            dpg_ref[n] = dgate.astype(BF16)
            dbg_ref[n] += jnp.sum(dgate, axis=0, keepdims=True)
            dy_ref[n] = _dot(d_o, w_ref[...], 1, 1).astype(BF16)

    pair = pl.BlockSpec((2, tm, D_MODEL), lambda i: (0, i, 0))
    gates = pl.BlockSpec((2, tm, D_MODEL), lambda i: (BLOCK_G, i, 0))
    vec = pl.BlockSpec((2, 1, D_MODEL), lambda i: (0, 0, 0))

    def weight(n):
        return pl.BlockSpec((None, D_MODEL, D_MODEL), lambda i: (n, 0, 0), pipeline_mode=pl.Buffered(1))

    pair_shape = jax.ShapeDtypeStruct((2, t, D_MODEL), BF16)
    return pl.pallas_call(
        body, name="gate_and_branch_backward", grid=(t // tm,),
        in_specs=[pl.BlockSpec((tm, D_MODEL), lambda i: (i, 0)), pair, gates, vec, weight(0), weight(1)],
        out_specs=(pair, gates, vec, pair),
        out_shape=(pair_shape, jax.ShapeDtypeStruct((N_SLOTS, t, D_MODEL), BF16),
                   jax.ShapeDtypeStruct((2, 1, D_MODEL), F32), pair_shape),
        compiler_params=_params("arbitrary"),
    )(dmerged, oab, p, bg, wo3, wo3)


def _weight_grad(a, a_sel, b, b_sel, tm, tn, tk, name):
    t = a.shape[-2]
    nk = t // tk

    def body(a_ref, b_ref, o_ref, acc_ref):
        k = pl.program_id(2)

        @pl.when(k == 0)
        def _():
            acc_ref[...] = jnp.zeros_like(acc_ref)

        acc_ref[...] += _dot(a_ref[...], b_ref[...], 0, 0)

        @pl.when(k == nk - 1)
        def _():
            o_ref[...] = acc_ref[...].astype(BF16)

    def spec(arr, sel, width, which):
        if arr.ndim == 2:
            return pl.BlockSpec((tk, width), lambda i, j, k: (k, (i, j)[which]))
        return pl.BlockSpec((None, tk, width), lambda i, j, k: (sel, k, (i, j)[which]))

    return pl.pallas_call(
        body, name=name, grid=(D_MODEL // tm, D_MODEL // tn, nk),
        in_specs=[spec(a, a_sel, tm, 0), spec(b, b_sel, tn, 1)],
        out_specs=pl.BlockSpec((tm, tn), lambda i, j, k: (i, j)),
        out_shape=jax.ShapeDtypeStruct((D_MODEL, D_MODEL), BF16),
        scratch_shapes=[pltpu.VMEM((tm, tn), F32)],
        compiler_params=_params("parallel", "parallel", "arbitrary"),
    )(a, b)


def _win_grad(xt, dp, tn, tk):
    _, t, _ = dp.shape
    nk = t // tk
    per_seg = D_MODEL // tn
    nj = N_SEG * per_seg

    def body(x_ref, dp_ref, o_ref, acc_ref):
        k = pl.program_id(1)

        @pl.when(k == 0)
        def _():
            acc_ref[...] = jnp.zeros_like(acc_ref)

        acc_ref[...] += _dot(x_ref[...], dp_ref[...])

        @pl.when(k == nk - 1)
        def _():
            o_ref[...] = acc_ref[...].astype(BF16)

    return pl.pallas_call(
        body, name="grad_w_in", grid=(nj, nk),
        in_specs=[pl.BlockSpec((D_MODEL, tk), lambda j, k: (0, k)),
                  pl.BlockSpec((None, tk, tn), lambda j, k: (_slot_of_seg(j // per_seg), k, j % per_seg))],
        out_specs=pl.BlockSpec((D_MODEL, tn), lambda j, k: (0, j)),
        out_shape=jax.ShapeDtypeStruct((D_MODEL, N_SEG * D_MODEL), BF16),
        scratch_shapes=[pltpu.VMEM((D_MODEL, tn), F32)],
        compiler_params=_params("parallel", "arbitrary"),
    )(xt, dp)


def _input_grad_and_scatter(dp, w_full, gx, s_in, s_o, small, tm):
    _, t, _ = dp.shape
    ni, nk = t // tm, N_SEG - 1
    _, hd, sc = s_in.shape
    hr = s_o.shape[2]

    def body(dp_ref, w_ref, gx_ref, si_ref, so_ref, sm_ref, o_ref, ri_ref, ro_ref, ga_ref,
             send_sems, recv_sems, local_sem):
        i, k = pl.program_id(0), pl.program_id(1)
        x, y, c, chips = _mesh_pos()
        me = 2 * x + y
        dev = 4 * x + 2 * y + c

        def peer(r):
            return (x ^ ((r >> 2) & 1), y ^ ((r >> 1) & 1), c ^ (r & 1))

        def sends():
            cps = []
            for q, (cx, cy) in enumerate(chips):
                dest = 2 * cx + cy
                cps.append(pltpu.make_async_remote_copy(src_ref=si_ref.at[dest], dst_ref=ri_ref.at[me],
                                                        send_sem=send_sems.at[q], recv_sem=recv_sems.at[q],
                                                        device_id=(cx, cy, c), device_id_type=MESH))
                cps.append(pltpu.make_async_remote_copy(src_ref=so_ref.at[:, dest], dst_ref=ro_ref.at[me],
                                                        send_sem=send_sems.at[3 + q], recv_sem=recv_sems.at[3 + q],
                                                        device_id=(cx, cy, c), device_id_type=MESH))
            for r in range(1, 8):
                cps.append(pltpu.make_async_remote_copy(src_ref=sm_ref, dst_ref=ga_ref.at[dev],
                                                        send_sem=send_sems.at[5 + r], recv_sem=recv_sems.at[5 + r],
                                                        device_id=peer(r), device_id_type=MESH))
            return cps

        own_small = pltpu.make_async_copy(sm_ref, ga_ref.at[dev], local_sem)

        @pl.when(jnp.logical_and(i == 0, k == 0))
        def _():
            for cp in sends():
                cp.start()
            own_small.start()

        @pl.when(k == 0)
        def _():
            o_ref[...] = gx_ref[...]

        o_ref[...] += _dot(dp_ref[...], w_ref[...], 1, 1)

        @pl.when(jnp.logical_and(i == ni - 1, k == nk - 1))
        def _():
            for q, (cx, cy) in enumerate(chips):
                frm = 2 * cx + cy
                pltpu.make_async_remote_copy(src_ref=si_ref.at[frm], dst_ref=ri_ref.at[frm], send_sem=send_sems.at[q],
                                             recv_sem=recv_sems.at[q], device_id=(x, y, c),
                                             device_id_type=MESH).wait_recv()
                pltpu.make_async_remote_copy(src_ref=so_ref.at[:, frm], dst_ref=ro_ref.at[frm],
                                             send_sem=send_sems.at[3 + q], recv_sem=recv_sems.at[3 + q],
                                             device_id=(x, y, c), device_id_type=MESH).wait_recv()
            for r in range(1, 8):
                px, py, pc = peer(r)
                pltpu.make_async_remote_copy(src_ref=sm_ref, dst_ref=ga_ref.at[4 * px + 2 * py + pc],
                                             send_sem=send_sems.at[5 + r], recv_sem=recv_sems.at[5 + r],
                                             device_id=(x, y, c), device_id_type=MESH).wait_recv()
            for cp in sends():
                cp.wait_send()
            own_small.wait()

    return pl.pallas_call(
        body, name="grad_x_and_scatter", grid=(ni, nk),
        in_specs=[pl.BlockSpec((None, tm, D_MODEL), lambda i, k: (_slot_of_seg(k), i, 0)),
                  pl.BlockSpec((D_MODEL, D_MODEL), lambda i, k: (0, k)),
                  pl.BlockSpec((tm, D_MODEL), lambda i, k: (i, 0)), ANY, ANY, ANY],
        out_specs=(pl.BlockSpec((tm, D_MODEL), lambda i, k: (i, 0)), ANY, ANY, ANY),
        out_shape=(jax.ShapeDtypeStruct((t, D_MODEL), F32),
                   jax.ShapeDtypeStruct((N_CHIPS, hd, sc), BF16),
                   jax.ShapeDtypeStruct((N_CHIPS, 3, hr, D_MODEL), BF16),
                   jax.ShapeDtypeStruct((8,) + small.shape, small.dtype)),
        scratch_shapes=[pltpu.SemaphoreType.DMA((13,)), pltpu.SemaphoreType.DMA((13,)), pltpu.SemaphoreType.DMA],
        input_output_aliases={2: 0},
        compiler_params=pltpu.CompilerParams(dimension_semantics=("arbitrary", "arbitrary"),
                                             vmem_limit_bytes=VMEM_LIMIT, has_side_effects=True),
    )(dp, w_full, gx, s_in, s_o, small)


def _input_grad_tail(dp, w_full, partial, tm):
    _, t, _ = dp.shape
    seg = N_SEG - 1

    def body(dp_ref, w_ref, part_ref, o_ref):
        o_ref[...] = part_ref[...] + _dot(dp_ref[...], w_ref[...], 1, 1)

    row = pl.BlockSpec((tm, D_MODEL), lambda i: (i, 0))
    return pl.pallas_call(
        body, name="grad_x_tail", grid=(t // tm,),
        in_specs=[pl.BlockSpec((None, tm, D_MODEL), lambda i: (SLOT_OF_SEG[seg], i, 0)),
                  pl.BlockSpec((D_MODEL, D_MODEL), lambda i: (0, seg)), row],
        out_specs=row,
        out_shape=jax.ShapeDtypeStruct((t, D_MODEL), F32),
        compiler_params=_params("parallel"),
    )(dp, w_full, partial)


def _sgu_chunk_forward(u, v, z, wm, bs, lng, lnb):
    ug, dug = _gelu_and_grad(u)
    vg, dvg = _gelu_and_grad(v)
    mu = jnp.mean(vg, axis=-1, keepdims=True)
    xc = vg - mu
    var = jnp.mean(xc * xc, axis=-1, keepdims=True)
    rstd = lax.rsqrt(var + LN_EPS)
    vhat = xc * rstd
    vln = (vhat * lng + lnb).astype(BF16)
    mixed = _dot(wm, vln) + bs
    sig = _sigmoid(z)
    return ug, dug, dvg, rstd, vhat, vln, mixed, sig


def _mixer_a_forward(p_a, wm, bs_col, ln_v_g, ln_v_b, tm):
    t = p_a.shape[1]

    def body(p_ref, wm_ref, bs_ref, g_ref, b_ref, o_ref):
        wm_v, bs_v, lng, lnb = wm_ref[...], bs_ref[...], g_ref[...], b_ref[...]

        def chunk(ci, carry):
            rows = pl.ds(pl.multiple_of(ci * CHUNK, CHUNK), CHUNK)
            u = p_ref[0, rows, :].astype(F32)
            v = p_ref[1, rows, :].astype(F32)
            z = p_ref[2, rows, :].astype(F32)
            ug, _, _, _, _, _, mixed, sig = _sgu_chunk_forward(u, v, z, wm_v, bs_v, lng, lnb)
            o_ref[rows, :] = (ug * mixed * (z * sig)).astype(BF16)
            return carry

        lax.fori_loop(0, tm // CHUNK, chunk, 0, unroll=True)

    return pl.pallas_call(
        body, name="mixer_a_forward", grid=(t // tm, N_HEADS),
        in_specs=[pl.BlockSpec((3, tm, HEAD_DIM), lambda i, h: (0, i, h)),
                  pl.BlockSpec((None, CHUNK, CHUNK), lambda i, h: (h, 0, 0)),
                  pl.BlockSpec((None, CHUNK, 1), lambda i, h: (h, 0, 0)),
                  pl.BlockSpec((1, HEAD_DIM), lambda i, h: (0, h)),
                  pl.BlockSpec((1, HEAD_DIM), lambda i, h: (0, h))],
        out_specs=pl.BlockSpec((tm, HEAD_DIM), lambda i, h: (i, h)),
        out_shape=jax.ShapeDtypeStruct((t, D_MODEL), BF16),
        compiler_params=_params("parallel", "parallel"),
    )(p_a, wm, bs_col, ln_v_g, ln_v_b)


def _mixer_a_backward(p_a, dyab, wm, bs_col, ln_v_g, ln_v_b, dp, tm):
    t = p_a.shape[1]

    def body(p_ref, dy_ref, wm_ref, bs_ref, g_ref, b_ref, dp_in, dp_ref, dws_ref, dbs_ref, dg_ref, db_ref):
        @pl.when(pl.program_id(1) == 0)
        def _():
            dws_ref[...] = jnp.zeros_like(dws_ref)
            dbs_ref[...] = jnp.zeros_like(dbs_ref)
            dg_ref[...] = jnp.zeros_like(dg_ref)
            db_ref[...] = jnp.zeros_like(db_ref)

        wm_v, bs_v, lng, lnb = wm_ref[...], bs_ref[...], g_ref[...], b_ref[...]
        causal = (lax.broadcasted_iota(jnp.int32, (CHUNK, CHUNK), 1)
                  <= lax.broadcasted_iota(jnp.int32, (CHUNK, CHUNK), 0))

        def chunk(ci, carry):
            rows = pl.ds(pl.multiple_of(ci * CHUNK, CHUNK), CHUNK)
            u = p_ref[0, rows, :].astype(F32)
            v = p_ref[1, rows, :].astype(F32)
            z = p_ref[2, rows, :].astype(F32)
            dy = dy_ref[rows, :].astype(F32)
            ug, dug, dvg, rstd, vhat, vln, mixed, sig = _sgu_chunk_forward(u, v, z, wm_v, bs_v, lng, lnb)
            sz = z * sig
            dmixed = dy * ug * sz
            dp_ref[0, rows, :] = (dy * mixed * sz * dug).astype(BF16)
            dp_ref[2, rows, :] = (dy * ug * mixed * (sig * (1.0 + z * (1.0 - sig)))).astype(BF16)
            dbs_ref[...] += jnp.sum(dmixed, axis=1, keepdims=True)
            dmb = dmixed.astype(BF16)
            dws_ref[...] += jnp.where(causal, _dot(dmb, vln, 1, 1), 0.0)
            dvln = _dot(wm_v, dmb, 0, 0)
            db_ref[...] += jnp.sum(dvln, axis=0, keepdims=True)
            dg_ref[...] += jnp.sum(dvln * vhat, axis=0, keepdims=True)
            dvh = dvln * lng
            m1 = jnp.mean(dvh, axis=-1, keepdims=True)
            m2 = jnp.mean(dvh * vhat, axis=-1, keepdims=True)
            dp_ref[1, rows, :] = (rstd * (dvh - m1 - vhat * m2) * dvg).astype(BF16)
            return carry

        lax.fori_loop(0, tm // CHUNK, chunk, 0, unroll=True)

    return pl.pallas_call(
        body, name="mixer_a_backward", grid=(N_HEADS, t // tm),
        in_specs=[pl.BlockSpec((3, tm, HEAD_DIM), lambda h, i: (0, i, h)),
                  pl.BlockSpec((None, tm, HEAD_DIM), lambda h, i: (0, i, h)),
                  pl.BlockSpec((None, CHUNK, CHUNK), lambda h, i: (h, 0, 0)),
                  pl.BlockSpec((None, CHUNK, 1), lambda h, i: (h, 0, 0)),
                  pl.BlockSpec((1, HEAD_DIM), lambda h, i: (0, h)),
                  pl.BlockSpec((1, HEAD_DIM), lambda h, i: (0, h)), ANY],
        out_specs=(pl.BlockSpec((3, tm, HEAD_DIM), lambda h, i: (BLOCK_A, i, h)),
                   pl.BlockSpec((None, CHUNK, CHUNK), lambda h, i: (h, 0, 0)),
                   pl.BlockSpec((None, CHUNK, 1), lambda h, i: (h, 0, 0)),
                   pl.BlockSpec((1, HEAD_DIM), lambda h, i: (0, h)),
                   pl.BlockSpec((1, HEAD_DIM), lambda h, i: (0, h))),
        out_shape=(jax.ShapeDtypeStruct(dp.shape, BF16),
                   jax.ShapeDtypeStruct((N_HEADS, CHUNK, CHUNK), F32),
                   jax.ShapeDtypeStruct((N_HEADS, CHUNK, 1), F32),
                   jax.ShapeDtypeStruct((1, D_MODEL), F32), jax.ShapeDtypeStruct((1, D_MODEL), F32)),
        input_output_aliases={6: 0},
        compiler_params=_params("parallel", "arbitrary"),
    )(p_a, dyab, wm, bs_col, ln_v_g, ln_v_b, dp)


HALO = 16


def _conv_taps(h, halo_h, tm):
    row = lax.broadcasted_iota(jnp.int32, h.shape, 0)
    last1 = halo_h[HALO - 1:HALO, :]
    last2 = halo_h[HALO - 2:HALO - 1, :]
    h1 = jnp.where(row == 0, last1, pltpu.roll(h, 1, 0))
    h2 = jnp.where(row == 0, last2, jnp.where(row == 1, last1, pltpu.roll(h, 2, 0)))
    return h1, h2


def _mixer_b_forward(p_b, conv_w, conv_b, tm, tc):
    t = p_b.shape[1]

    def body(p_ref, halo_ref, w_ref, b_ref, o_ref):
        valid = (pl.program_id(1) > 0).astype(F32)
        h = p_ref[1].astype(F32) * p_ref[0].astype(F32)
        halo_h = halo_ref[1].astype(F32) * halo_ref[0].astype(F32) * valid
        h1, h2 = _conv_taps(h, halo_h, tm)
        w = w_ref[...]
        conv = b_ref[...] + w[0:1, :] * h2 + w[1:2, :] * h1 + w[2:3, :] * h
        z = p_ref[3].astype(F32)
        o_ref[...] = (p_ref[2].astype(F32) * conv * (z * _sigmoid(z))).astype(BF16)

    steps = tm // HALO
    return pl.pallas_call(
        body, name="mixer_b_forward", grid=(D_MODEL // tc, t // tm),
        in_specs=[pl.BlockSpec((4, tm, tc), lambda j, i: (BLOCK_B, i, j)),
                  pl.BlockSpec((4, HALO, tc), lambda j, i: (BLOCK_B, jnp.maximum(i * steps - 1, 0), j)),
                  pl.BlockSpec((3, tc), lambda j, i: (0, j)),
                  pl.BlockSpec((1, tc), lambda j, i: (0, j))],
        out_specs=pl.BlockSpec((tm, tc), lambda j, i: (i, j)),
        out_shape=jax.ShapeDtypeStruct((t, D_MODEL), BF16),
        compiler_params=_params("parallel", "parallel"),
    )(p_b, p_b, conv_w, conv_b)


def _mixer_b_backward(p_b, dyab, conv_w, conv_b, dp, tm, tc):
    t = p_b.shape[1]
    n = t // tm

    def body(p_ref, halo_ref, dy_ref, w_ref, b_ref, dp_in, dp_ref, dw_ref, db_ref, next_ref):
        ii = pl.program_id(1)

        @pl.when(ii == 0)
        def _():
            dw_ref[...] = jnp.zeros_like(dw_ref)
            db_ref[...] = jnp.zeros_like(db_ref)
            next_ref[...] = jnp.zeros_like(next_ref)

        valid = (ii < n - 1).astype(F32)
        xb = p_ref[0].astype(F32)
        cb = p_ref[1].astype(F32)
        bb = p_ref[2].astype(F32)
        z = p_ref[3].astype(F32)
        h = cb * xb
        halo_h = halo_ref[1].astype(F32) * halo_ref[0].astype(F32) * valid
        h1, h2 = _conv_taps(h, halo_h, tm)
        w = w_ref[...]
        w0, w1, w2 = w[0:1, :], w[1:2, :], w[2:3, :]
        conv = b_ref[...] + w0 * h2 + w1 * h1 + w2 * h
        sig = _sigmoid(z)
        sz = z * sig
        dy = dy_ref[...].astype(F32)
        dconv = dy * bb * sz
        dp_ref[2] = (dy * conv * sz).astype(BF16)
        dp_ref[3] = (dy * bb * conv * (sig * (1.0 + z * (1.0 - sig)))).astype(BF16)
        db_ref[...] += jnp.sum(dconv, axis=0, keepdims=True)
        dw_ref[0:1, :] += jnp.sum(dconv * h2, axis=0, keepdims=True)
        dw_ref[1:2, :] += jnp.sum(dconv * h1, axis=0, keepdims=True)
        dw_ref[2:3, :] += jnp.sum(dconv * h, axis=0, keepdims=True)
        row = lax.broadcasted_iota(jnp.int32, h.shape, 0)
        nxt = next_ref[...]
        n0, n1 = nxt[0:1, :], nxt[1:2, :]
        d1 = jnp.where(row == tm - 1, n0, pltpu.roll(dconv, tm - 1, 0))
        d2 = jnp.where(row == tm - 1, n1, jnp.where(row == tm - 2, n0, pltpu.roll(dconv, tm - 2, 0)))
        dh = w2 * dconv + w1 * d1 + w0 * d2
        dp_ref[0] = (dh * cb).astype(BF16)
        dp_ref[1] = (dh * xb).astype(BF16)
        next_ref[...] = dconv[0:8, :]

    steps = tm // HALO
    return pl.pallas_call(
        body, name="mixer_b_backward", grid=(D_MODEL // tc, n),
        in_specs=[pl.BlockSpec((4, tm, tc), lambda j, ii: (BLOCK_B, n - 1 - ii, j)),
                  pl.BlockSpec((4, HALO, tc), lambda j, ii: (BLOCK_B, jnp.maximum((n - 1 - ii) * steps - 1, 0), j)),
                  pl.BlockSpec((None, tm, tc), lambda j, ii: (1, n - 1 - ii, j)),
                  pl.BlockSpec((3, tc), lambda j, ii: (0, j)),
                  pl.BlockSpec((1, tc), lambda j, ii: (0, j)), ANY],
        out_specs=(pl.BlockSpec((4, tm, tc), lambda j, ii: (BLOCK_B, n - 1 - ii, j)),
                   pl.BlockSpec((3, tc), lambda j, ii: (0, j)),
                   pl.BlockSpec((1, tc), lambda j, ii: (0, j))),
        out_shape=(jax.ShapeDtypeStruct(dp.shape, BF16),
                   jax.ShapeDtypeStruct((3, D_MODEL), F32), jax.ShapeDtypeStruct((1, D_MODEL), F32)),
        scratch_shapes=[pltpu.VMEM((8, tc), F32)],
        input_output_aliases={5: 0},
        compiler_params=_params("parallel", "arbitrary"),
    )(p_b, p_b, dyab, conv_w, conv_b, dp)


def _adam_math(w, g, m, v):
    m = ADAM_B1 * m + (1.0 - ADAM_B1) * g
    v = ADAM_B2 * v + (1.0 - ADAM_B2) * (g * g)
    delta = -ADAM_LR * ((m * ADAM_C1) / (jnp.sqrt(v * ADAM_C2) + ADAM_EPS) + ADAM_WD * w)
    return delta, m, v


def _adam_rows(w, g, m, v, tm, name):
    r, c = w.shape

    def body(w_ref, g_ref, m_ref, v_ref, go_ref, d_ref, mo_ref, vo_ref):
        g = g_ref[...]
        d, mn, vn = _adam_math(w_ref[...], g, m_ref[...], v_ref[...])
        go_ref[...] = g
        d_ref[...] = d
        mo_ref[...] = mn
        vo_ref[...] = vn

    spec = pl.BlockSpec((tm, c), lambda i: (i, 0))
    shape = jax.ShapeDtypeStruct((r, c), F32)
    return pl.pallas_call(
        body, name=name, grid=(r // tm,),
        in_specs=[spec] * 4, out_specs=(spec,) * 4, out_shape=(shape,) * 4,
        compiler_params=_params("parallel"),
    )(w, g, m, v)


SMALL_ROW0 = {name: sum(r for _, r in SMALL_ROWS[:i]) for i, (name, _) in enumerate(SMALL_ROWS)}
LANE_MAJOR = ("ln_g", "ln_b", "b_gate", "ln_v_g", "ln_v_b", "conv_b")


def _lane_pieces(n):
    return [(q, slice(q * 128, (q + 1) * 128)) for q in range(n // 128)]


def _pack_small(d_ln_g, d_ln_b, d_bg, d_lnv_g, d_lnv_b, d_ws, d_bs, d_cw, d_cb, loss_part):
    def body(lg, lb, bg, vg, vb, ws, bs, cw, cb, loss, o_ref):
        def put(row0, vec):
            for q, cols in _lane_pieces(vec.shape[1]):
                o_ref[row0 + q:row0 + q + 1, :] = vec[:, cols]

        put(SMALL_ROW0["ln_g"], lg[...])
        put(SMALL_ROW0["ln_b"], lb[...])
        for n in range(2):
            put(SMALL_ROW0["b_gate"] + n * (D_MODEL // 128), bg[n])
        put(SMALL_ROW0["ln_v_g"], vg[...])
        put(SMALL_ROW0["ln_v_b"], vb[...])
        for h in range(N_HEADS):
            o_ref[SMALL_ROW0["w_s"] + h * CHUNK:SMALL_ROW0["w_s"] + (h + 1) * CHUNK, :] = ws[h]
        o_ref[SMALL_ROW0["b_s"]:SMALL_ROW0["b_s"] + N_HEADS, :] = bs[...]
        for c in range(3):
            put(SMALL_ROW0["conv_w"] + c * (D_MODEL // 128), cw[c:c + 1, :])
        put(SMALL_ROW0["conv_b"], cb[...])
        o_ref[SMALL_ROW0["loss"]:SMALL_ROW0["loss"] + 8, :] = jnp.broadcast_to(loss[...], (8, 128))

    return pl.pallas_call(
        body, name="pack_small", out_shape=jax.ShapeDtypeStruct((SMALL_TOTAL, 128), F32), compiler_params=_params(),
    )(d_ln_g, d_ln_b, d_bg, d_lnv_g, d_lnv_b, d_ws, d_bs.reshape(N_HEADS, CHUNK), d_cw, d_cb, loss_part)


def _adam_small(gathered, params):
    names = list(params)
    flat = [a for n in names for a in params[n]]

    def body(*refs):
        ga_ref = refs[0]
        ins = refs[1:1 + 3 * len(names)]
        outs = refs[1 + 3 * len(names):-1]
        gs_ref = refs[-1]
        g = ga_ref[0]
        for k in range(1, 8):
            g = g + ga_ref[k]
        gs_ref[...] = g
        for i, name in enumerate(names):
            w_ref, m_ref, v_ref = ins[3 * i:3 * i + 3]
            o_refs = outs[4 * i:4 * i + 4]
            row0 = SMALL_ROW0[name]
            if name in LANE_MAJOR:
                pieces = [((slice(None), cols), slice(row0 + q, row0 + q + 1))
                          for q, cols in _lane_pieces(w_ref.shape[1])]
            elif name == "w_s":
                pieces = [((0, h), slice(row0 + h * CHUNK, row0 + (h + 1) * CHUNK)) for h in range(N_HEADS)]
            else:
                pieces = [((0,), slice(row0, row0 + N_HEADS))]
            for idx, rows in pieces:
                gp = gs_ref[rows, :]
                res = (gp,) + _adam_math(w_ref[idx], gp, m_ref[idx], v_ref[idx])
                for o_ref, val in zip(o_refs, res):
                    o_ref[idx] = val
        gcw_ref, loss_ref = outs[-2:]
        for c in range(3):
            for q, cols in _lane_pieces(D_MODEL):
                r = SMALL_ROW0["conv_w"] + c * (D_MODEL // 128) + q
                gcw_ref[c:c + 1, cols] = gs_ref[r:r + 1, :]
        loss_ref[...] = gs_ref[SMALL_ROW0["loss"]:SMALL_ROW0["loss"] + 1, :]

    out_shape = [jax.ShapeDtypeStruct(params[n][0].shape, F32) for n in names for _ in range(4)]
    out_shape += [jax.ShapeDtypeStruct((3, D_MODEL), F32), jax.ShapeDtypeStruct((1, 128), F32)]
    res = pl.pallas_call(
        body, name="adam_small", out_shape=tuple(out_shape),
        scratch_shapes=[pltpu.VMEM((SMALL_TOTAL, 128), F32)], compiler_params=_params(),
    )(gathered, *flat)
    return {n: res[4 * i:4 * i + 4] for i, n in enumerate(names)}, res[-2], res[-1]


def _adam_conv_w(g_all, chip, w, m, v):
    cols = w.shape[2]

    def body(c_ref, g_ref, w_ref, m_ref, v_ref, go_ref, d_ref, mo_ref, vo_ref):
        g = g_ref[...]
        d, mn, vn = _adam_math(w_ref[...], g, m_ref[...], v_ref[...])
        go_ref[...] = g
        d_ref[...] = d
        mo_ref[...] = mn
        vo_ref[...] = vn

    own = pl.BlockSpec((None, 3, cols), lambda i, c_ref: (0, 0, 0))
    return pl.pallas_call(
        body, name="adam_conv_w",
        grid_spec=pltpu.PrefetchScalarGridSpec(
            num_scalar_prefetch=1, grid=(1,),
            in_specs=[pl.BlockSpec((3, cols), lambda i, c_ref: (0, c_ref[0])), own, own, own],
            out_specs=(own,) * 4),
        out_shape=(jax.ShapeDtypeStruct(w.shape, F32),) * 4,
        compiler_params=_params("arbitrary"),
    )(chip, g_all, w, m, v)


def kernel(x, w_in, b_gate, ln_v_g, ln_v_b, w_s, b_s, conv_w, conv_b, w_oa, w_ob, w_out, ln_g, ln_b, loss_target, m_w_in, m_b_gate, m_ln_v_g, m_ln_v_b, m_w_s, m_b_s, m_conv_w, m_conv_b, m_w_oa, m_w_ob, m_w_out, m_ln_g, m_ln_b, v_w_in, v_b_gate, v_ln_v_g, v_ln_v_b, v_w_s, v_b_s, v_conv_w, v_conv_b, v_w_oa, v_w_ob, v_w_out, v_ln_g, v_ln_b):
    t = x.shape[1]
    x2 = x[0]
    target = loss_target[0]
    chip = 2 * lax.axis_index("x") + lax.axis_index("y")
    conv_cols = conv_w.shape[2]

    chip1 = chip.astype(jnp.int32).reshape(1)
    w_pre = _cast_into_columns(w_in[0], chip1, N_CHIPS, 256, "cast_w_in")
    wo_b, wm = _prep_small_weights(w_oa[0], w_ob[0], w_out[0], w_s[0])
    conv_w8 = jnp.concatenate([conv_w[0], jnp.zeros((5, conv_cols), F32)], axis=0)
    bs_col = b_s[0].reshape(N_HEADS, CHUNK, 1)
    bg = b_gate.reshape(2, 1, D_MODEL)

    xb, xt = _cast_and_transpose(x2, min(512, t))
    p, w_full, wo_full, cw_full = _gather_and_project(xb, w_pre, wo_b, conv_w8, min(2048, t))
    wo3 = wo_full.reshape(3, D_MODEL, D_MODEL)
    conv_w_all = jnp.transpose(cw_full[:, :3, :], (1, 0, 2)).reshape(3, D_MODEL)
    tm_a = min(512, t)
    ya = _mixer_a_forward(p, wm, bs_col, ln_v_g, ln_v_b, tm_a)
    tm_b = min(512, t)
    yb = _mixer_b_forward(p, conv_w_all, conv_b, tm_b, 512)
    tm_m = min(1024, t)
    merged, oab = _merge_forward(ya, yb, wo3, p, bg, tm_m, 512)

    drb, gx, dmerged, d_ln_g, d_ln_b, loss_part = _head(merged, wo3, x2, target, ln_g, ln_b, min(512, t))
    doab, dp, d_bg, dyab = _gate_and_branch_backward(dmerged, oab, p, bg, wo3, min(256, t))
    dp, d_ws, d_bs, d_lnv_g, d_lnv_b = _mixer_a_backward(p, dyab, wm, bs_col, ln_v_g, ln_v_b, dp, tm_a)
    dp, d_cw, d_cb = _mixer_b_backward(p, dyab, conv_w_all, conv_b, dp, tm_b, 512)

    tk = min(2048, t)
    g_oa = _weight_grad(ya, 0, doab, 0, 1024, 1024, tk, "grad_w_oa")
    g_ob = _weight_grad(yb, 0, doab, 1, 1024, 1024, tk, "grad_w_ob")
    g_out = _weight_grad(merged, 0, drb, 0, 1024, 1024, tk, "grad_w_out")
    g_in = _win_grad(xt, dp, 1024, tk)

    r_in, *r_o = _exchange_halves(g_in, (g_oa, g_ob, g_out))
    s_in, s_o = _add_halves(g_in, (g_oa, g_ob, g_out), r_in, r_o)
    small_part = _pack_small(d_ln_g, d_ln_b, d_bg, d_lnv_g, d_lnv_b, d_ws, d_bs, d_cw, d_cb, loss_part)
    tm_x = min(512, t)
    gx, q_in, q_o, gathered = _input_grad_and_scatter(dp, w_full, gx, s_in, s_o, small_part, tm_x)
    grad_x = _input_grad_tail(dp, w_full, gx, tm_x)
    f_in, f_o = _sum_chips(q_in, q_o, s_in, s_o)
    gsum_in, gsum_o = _share_halves(f_in, f_o)

    big = {}
    big["w_in"] = _adam_rows(w_in[0], gsum_in, m_w_in[0], v_w_in[0], 128, "adam_w_in")
    for n, (name, w, m, v) in enumerate((("w_oa", w_oa, m_w_oa, v_w_oa), ("w_ob", w_ob, m_w_ob, v_w_ob),
                                         ("w_out", w_out, m_w_out, v_w_out))):
        big[name] = _adam_rows(w[0], gsum_o[n], m[0], v[0], 256, "adam_" + name)

    small, g_conv_w, loss_row = _adam_small(gathered, {
        "ln_g": (ln_g, m_ln_g, v_ln_g), "ln_b": (ln_b, m_ln_b, v_ln_b), "b_gate": (b_gate, m_b_gate, v_b_gate),
        "ln_v_g": (ln_v_g, m_ln_v_g, v_ln_v_g), "ln_v_b": (ln_v_b, m_ln_v_b, v_ln_v_b),
        "w_s": (w_s, m_w_s, v_w_s), "b_s": (b_s, m_b_s, v_b_s), "conv_b": (conv_b, m_conv_b, v_conv_b)})
    small["conv_w"] = _adam_conv_w(g_conv_w, chip1, conv_w, m_conv_w, v_conv_w)
    loss = loss_row[0, 0]

    order = ("w_in", "b_gate", "ln_v_g", "ln_v_b", "w_s", "b_s", "conv_w", "conv_b", "w_oa", "w_ob", "w_out",
             "ln_g", "ln_b")
    outs = [loss, grad_x[None]]
    for which in range(4):
        for name in order:
            outs.append(big[name][which][None] if name in big else small[name][which])
    return tuple(outs)
```

```python
import functools
import math

import jax
import jax.numpy as jnp
from jax import lax
from jax.experimental import pallas as pl
from jax.experimental.pallas import tpu as pltpu

F32 = jnp.float32
BF16 = jnp.bfloat16

D_MODEL = 2048
N_HEADS = 8
HEAD_DIM = D_MODEL // N_HEADS
CHUNK = 128
N_SEG = 9
N_CHIPS = 4
SHARD_COLS = N_SEG * D_MODEL // N_CHIPS
COL_BLOCK = 512
BLOCKS_PER_SHARD = SHARD_COLS // COL_BLOCK
BLOCKS_PER_SEG = D_MODEL // COL_BLOCK
SUBS = 3
SHARD_ROWS = D_MODEL // N_CHIPS
N_SLOTS = 12
BLOCK_A, BLOCK_G, BLOCK_B = 0, 2, 2
SLOT_OF_SEG = (0, 1, 2, 8, 9, 10, 11, 4, 5)
DN_ALPHA = 2.0 ** 0.25
LN_EPS = 1e-5
GELU_K = math.sqrt(2.0 / math.pi)
GELU_C = 0.044715

ADAM_LR = 0.001
ADAM_B1 = 0.9
ADAM_B2 = 0.999
ADAM_EPS = 1e-08
ADAM_WD = 0.01
ADAM_STEP = 10
ADAM_C1 = 1.0 / (1.0 - ADAM_B1 ** ADAM_STEP)
ADAM_C2 = 1.0 / (1.0 - ADAM_B2 ** ADAM_STEP)

VMEM_LIMIT = 60 * 1024 * 1024
MESH = pl.DeviceIdType.MESH
ANY = pl.BlockSpec(memory_space=pl.ANY)

SMALL_ROWS = (("ln_g", 16), ("ln_b", 16), ("b_gate", 32), ("ln_v_g", 16), ("ln_v_b", 16),
              ("w_s", 1024), ("b_s", 8), ("conv_w", 48), ("conv_b", 16), ("loss", 8))
SMALL_TOTAL = sum(r for _, r in SMALL_ROWS)


def _params(*sem):
    return pltpu.CompilerParams(dimension_semantics=sem, vmem_limit_bytes=VMEM_LIMIT)


def _sigmoid(x):
    return 1.0 / (1.0 + jnp.exp(-x))


def _gelu_and_grad(x):
    x2 = x * x
    th = jnp.tanh(GELU_K * (x + GELU_C * x * x2))
    g = 0.5 * x * (1.0 + th)
    dg = 0.5 * (1.0 + th) + 0.5 * x * (1.0 - th * th) * (GELU_K * (1.0 + 3.0 * GELU_C * x2))
    return g, dg


def _gelu(x):
    return 0.5 * x * (1.0 + jnp.tanh(GELU_K * (x + GELU_C * x * x * x)))


def _dot(a, b, ca=1, cb=0):
    return lax.dot_general(a, b, (((ca,), (cb,)), ((), ())), preferred_element_type=F32)


def _cast_and_transpose(x, tm):
    t, d = x.shape

    def body(x_ref, o_ref, ot_ref):
        v = x_ref[...]
        o_ref[...] = v.astype(BF16)
        ot_ref[...] = v.T.astype(BF16)

    return pl.pallas_call(
        body, name="cast_x", grid=(t // tm,),
        in_specs=[pl.BlockSpec((tm, d), lambda i: (i, 0))],
        out_specs=(pl.BlockSpec((tm, d), lambda i: (i, 0)), pl.BlockSpec((d, tm), lambda i: (0, i))),
        out_shape=(jax.ShapeDtypeStruct((t, d), BF16), jax.ShapeDtypeStruct((d, t), BF16)),
        compiler_params=_params("parallel"),
    )(x)


def _cast_into_columns(w, slot, n_slots, tm, name):
    r, c = w.shape

    def body(s_ref, w_ref, o_ref):
        o_ref[...] = w_ref[...].astype(BF16)

    return pl.pallas_call(
        body, name=name,
        grid_spec=pltpu.PrefetchScalarGridSpec(
            num_scalar_prefetch=1, grid=(r // tm,),
            in_specs=[pl.BlockSpec((tm, c), lambda i, s_ref: (i, 0))],
            out_specs=pl.BlockSpec((tm, c), lambda i, s_ref: (i, s_ref[0]))),
        out_shape=jax.ShapeDtypeStruct((r, n_slots * c), BF16),
        compiler_params=_params("parallel"),
    )(slot, w)


def _prep_small_weights(w_oa, w_ob, w_out, w_s):
    rows = w_oa.shape[0]

    def body(a_ref, b_ref, c_ref, ws_ref, wo_ref, wm_ref):
        wo_ref[0] = a_ref[...].astype(BF16)
        wo_ref[1] = b_ref[...].astype(BF16)
        wo_ref[2] = c_ref[...].astype(BF16)
        t = lax.broadcasted_iota(jnp.int32, (CHUNK, CHUNK), 0)
        s = lax.broadcasted_iota(jnp.int32, (CHUNK, CHUNK), 1)
        for h in range(N_HEADS):
            wm_ref[h] = jnp.where(s <= t, ws_ref[h], 0.0).astype(BF16)

    return pl.pallas_call(
        body, name="prep_small_weights",
        out_shape=(jax.ShapeDtypeStruct((3, rows, D_MODEL), BF16),
                   jax.ShapeDtypeStruct((N_HEADS, CHUNK, CHUNK), BF16)),
        compiler_params=_params(),
    )(w_oa, w_ob, w_out, w_s)


def _mesh_pos():
    x, y, c = lax.axis_index("x"), lax.axis_index("y"), lax.axis_index("c")
    chips = [(1 - x, y), (x, 1 - y), (1 - x, 1 - y)]
    return x, y, c, chips


def _slot_of_seg(seg):
    return jnp.where(seg < 3, seg, jnp.where(seg < 7, seg + 5, seg - 3))


def _gather_and_project(xb, w_pre, wo_b, conv_w8, tm):
    t = xb.shape[0]
    d, sc = w_pre.shape[0], w_pre.shape[1] // N_CHIPS
    rows = wo_b.shape[1]
    hd, hr = d // 2, rows // 2
    nj = BLOCKS_PER_SHARD // SUBS
    pc = nj * COL_BLOCK
    units = N_CHIPS * SUBS
    ni = t // tm
    total = units * ni * nj
    mx, my = lax.axis_index("x"), lax.axis_index("y")
    order = jnp.stack([2 * mx + my, 2 * (1 - mx) + my, 2 * mx + (1 - my),
                       2 * (1 - mx) + (1 - my)]).astype(jnp.int32)

    def body(order_ref, x_ref, wpre_ref, wo_ref, cw_ref, p_ref, wf_ref, wof_ref, cwf_ref,
             wbuf, wsem, send_sems, recv_sems, local_sems):
        x, y, c, chips = _mesh_pos()
        me = 2 * x + y
        sibling = (x, y, 1 - c)
        u, i, j = pl.program_id(0), pl.program_id(1), pl.program_id(2)
        n = (u * ni + i) * nj + j

        def chip_of(q):
            return 2 * chips[q][0] + chips[q][1]

        def piece(ref, k, cc, r):
            return ref.at[pl.ds(cc * hd, hd), pl.ds(pl.multiple_of(k * sc + r * pc, COL_BLOCK), pc)]

        def wo_half(ref4, k, cc):
            return ref4.at[:, k, pl.ds(cc * hr, hr), :]

        def rcopy(sem, src, dst, to):
            return pltpu.make_async_remote_copy(src_ref=src, dst_ref=dst, send_sem=send_sems.at[sem],
                                                recv_sem=recv_sems.at[sem], device_id=to, device_id_type=MESH)

        def w_send(q, r):
            return rcopy(q * SUBS + r, piece(wpre_ref, me, c, r), piece(wf_ref, me, c, r), (*chips[q], c))

        def w_landed(q, r):
            return rcopy(q * SUBS + r, piece(wpre_ref, me, c, r), piece(wf_ref, chip_of(q), c, r), sibling)

        def w_forward(q, r, cc):
            ref = piece(wf_ref, chip_of(q), cc, r)
            return rcopy(9 + q * SUBS + r, ref, ref, sibling)

        def wo_send(q):
            return rcopy(18 + q, wo_ref.at[:, pl.ds(c * hr, hr), :], wo_half(wof_ref, me, c), (*chips[q], c))

        def wo_landed(q):
            return rcopy(18 + q, wo_ref.at[:, pl.ds(c * hr, hr), :], wo_half(wof_ref, chip_of(q), c), sibling)

        def wo_forward(q, cc):
            ref = wo_half(wof_ref, chip_of(q), cc)
            return rcopy(21 + q, ref, ref, sibling)

        def conv_send(q):
            return rcopy(24 + q, cw_ref, cwf_ref.at[me], (*chips[q], c))

        def local_copies():
            return [pltpu.make_async_copy(wo_ref, wof_ref.at[:, me], local_sems.at[0]),
                    pltpu.make_async_copy(cw_ref, cwf_ref.at[me], local_sems.at[1])]

        def tile_start(u_, j_, slot):
            g = order_ref[u_ // SUBS] * BLOCKS_PER_SHARD + lax.rem(u_, SUBS) * nj + j_
            cols = pl.ds(pl.multiple_of(g * COL_BLOCK, COL_BLOCK), COL_BLOCK)

            @pl.when(u_ < SUBS)
            def _():
                pltpu.make_async_copy(wpre_ref.at[:, cols], wbuf.at[slot], wsem.at[slot]).start()

            @pl.when(u_ >= SUBS)
            def _():
                pltpu.make_async_copy(wf_ref.at[:, cols], wbuf.at[slot], wsem.at[slot]).start()

        def end_of(u_):
            return jnp.logical_and(u == u_, jnp.logical_and(i == ni - 1, j == nj - 1))

        @pl.when(n == 0)
        def _():
            for cp in local_copies():
                cp.start()
            for r in range(SUBS):
                for q in (0, 1):
                    w_send(q, r).start()
            for q in range(3):
                conv_send(q).start()
            tile_start(0, 0, 0)

        def pass_on(q, r):
            w_landed(q, r).wait_recv()
            w_forward(q, r, c).start()

        for u_ in range(1, units - 1):
            @pl.when(end_of(u_))
            def _(u_=u_):
                if u_ <= SUBS:
                    for q in (0, 1):
                        pass_on(q, u_ - 1)
                if u_ == SUBS:
                    for r in range(SUBS):
                        for q in (0, 1):
                            w_send(q, r).wait_send()
                    for r in range(SUBS):
                        w_send(2, r).start()
                if 2 * SUBS - 1 <= u_ <= 3 * SUBS - 2:
                    pass_on(2, u_ - (2 * SUBS - 1))
                if u_ == 3 * SUBS - 2:
                    for r in range(SUBS):
                        w_send(2, r).wait_send()
                    for q in range(3):
                        wo_send(q).start()
                nxt = u_ + 1
                if nxt >= SUBS:
                    w_forward(nxt // SUBS - 1, nxt % SUBS, 1 - c).wait_recv()

        n1 = n + 1

        @pl.when(n1 < total)
        def _():
            tile_start(n1 // (ni * nj), lax.rem(n1, nj), lax.rem(n1, 2))

        slot = lax.rem(n, 2)
        pltpu.make_async_copy(wpre_ref.at[:, pl.ds(0, COL_BLOCK)], wbuf.at[slot], wsem.at[slot]).wait()
        p_ref[...] = _dot(x_ref[...], wbuf[slot]).astype(BF16)

        @pl.when(n == total - 1)
        def _():
            for q in range(3):
                wo_landed(q).wait_recv()
                wo_forward(q, c).start()
            for q in range(3):
                wo_forward(q, 1 - c).wait_recv()
                rcopy(24 + q, cw_ref, cwf_ref.at[chip_of(q)], sibling).wait_recv()
            for q in range(3):
                for r in range(SUBS):
                    w_forward(q, r, c).wait_send()
                wo_send(q).wait_send()
                wo_forward(q, c).wait_send()
                conv_send(q).wait_send()
            for cp in local_copies():
                cp.wait()

    def p_map(u, i, j, o):
        g = o[u // SUBS] * BLOCKS_PER_SHARD + lax.rem(u, SUBS) * nj + j
        return (_slot_of_seg(g // BLOCKS_PER_SEG), i, lax.rem(g, BLOCKS_PER_SEG))

    return pl.pallas_call(
        body, name="gather_and_project",
        grid_spec=pltpu.PrefetchScalarGridSpec(
            num_scalar_prefetch=1, grid=(units, ni, nj),
            in_specs=[pl.BlockSpec((tm, D_MODEL), lambda u, i, j, o: (i, 0)), ANY, ANY, ANY],
            out_specs=(pl.BlockSpec((None, tm, COL_BLOCK), p_map), ANY, ANY, ANY),
            scratch_shapes=[pltpu.VMEM((2, d, COL_BLOCK), BF16), pltpu.SemaphoreType.DMA((2,)),
                            pltpu.SemaphoreType.DMA((27,)), pltpu.SemaphoreType.DMA((27,)),
                            pltpu.SemaphoreType.DMA((2,))]),
        out_shape=(jax.ShapeDtypeStruct((N_SLOTS, t, D_MODEL), BF16),
                   jax.ShapeDtypeStruct((d, N_CHIPS * sc), BF16),
                   jax.ShapeDtypeStruct((3, N_CHIPS, rows, D_MODEL), BF16),
                   jax.ShapeDtypeStruct((N_CHIPS,) + conv_w8.shape, F32)),
        input_output_aliases={2: 1},
        compiler_params=pltpu.CompilerParams(dimension_semantics=("arbitrary",) * 3, vmem_limit_bytes=VMEM_LIMIT,
                                             has_side_effects=True),
    )(order, xb, w_pre, wo_b, conv_w8)


def _exchange_halves(g_in, g_o):
    d, c9 = g_in.shape
    hd = d // 2
    hr = SHARD_ROWS // 2
    g_o4 = [g.reshape(N_CHIPS, 2, hr, D_MODEL) for g in g_o]

    def body(gi_ref, ga_ref, gb_ref, gc_ref, ri_ref, ra_ref, rb_ref, rc_ref, send_sems, recv_sems):
        x, y, c, _ = _mesh_pos()
        sibling = (x, y, 1 - c)
        cps = [pltpu.make_async_remote_copy(src_ref=gi_ref.at[pl.ds((1 - c) * hd, hd), :], dst_ref=ri_ref,
                                            send_sem=send_sems.at[0], recv_sem=recv_sems.at[0],
                                            device_id=sibling, device_id_type=MESH)]
        for n, (g_ref, r_ref) in enumerate(((ga_ref, ra_ref), (gb_ref, rb_ref), (gc_ref, rc_ref))):
            cps.append(pltpu.make_async_remote_copy(src_ref=g_ref.at[:, 1 - c], dst_ref=r_ref,
                                                    send_sem=send_sems.at[1 + n], recv_sem=recv_sems.at[1 + n],
                                                    device_id=sibling, device_id_type=MESH))
        for cp in cps:
            cp.start()
        for cp in cps:
            cp.wait()

    o_shape = jax.ShapeDtypeStruct((N_CHIPS, hr, D_MODEL), BF16)
    return pl.pallas_call(
        body, name="rs_exchange_halves",
        in_specs=[ANY] * 4, out_specs=(ANY,) * 4,
        out_shape=(jax.ShapeDtypeStruct((hd, c9), BF16), o_shape, o_shape, o_shape),
        scratch_shapes=[pltpu.SemaphoreType.DMA((4,)), pltpu.SemaphoreType.DMA((4,))],
        compiler_params=pltpu.CompilerParams(has_side_effects=True),
    )(g_in, *g_o4)


def _add_halves(g_in, g_o, r_in, r_o):
    d, c9 = g_in.shape
    hd = d // 2
    hr = SHARD_ROWS // 2
    core = lax.axis_index("c").astype(jnp.int32).reshape(1)
    tm = min(512, hd)
    nb = hd // tm

    def body_in(c_ref, g_ref, r_ref, o_ref):
        o_ref[...] = (g_ref[...].astype(F32) + r_ref[...].astype(F32)).astype(BF16)

    s_in = pl.pallas_call(
        body_in, name="rs_add_halves_in",
        grid_spec=pltpu.PrefetchScalarGridSpec(
            num_scalar_prefetch=1, grid=(N_CHIPS, nb),
            in_specs=[pl.BlockSpec((tm, SHARD_COLS), lambda k, i, c_ref: (c_ref[0] * nb + i, k)),
                      pl.BlockSpec((tm, SHARD_COLS), lambda k, i, c_ref: (i, k))],
            out_specs=pl.BlockSpec((None, tm, SHARD_COLS), lambda k, i, c_ref: (k, i, 0))),
        out_shape=jax.ShapeDtypeStruct((N_CHIPS, hd, SHARD_COLS), BF16),
        compiler_params=_params("parallel", "parallel"),
    )(core, g_in, r_in)

    g_o4 = [g.reshape(N_CHIPS, 2, hr, D_MODEL) for g in g_o]

    def body_o(c_ref, ga_ref, gb_ref, gc_ref, ra_ref, rb_ref, rc_ref, o_ref):
        for n, (g_ref, r_ref) in enumerate(((ga_ref, ra_ref), (gb_ref, rb_ref), (gc_ref, rc_ref))):
            o_ref[n] = (g_ref[...].astype(F32) + r_ref[...].astype(F32)).astype(BF16)

    gspec = pl.BlockSpec((None, None, hr, D_MODEL), lambda k, c_ref: (k, c_ref[0], 0, 0))
    rspec = pl.BlockSpec((None, hr, D_MODEL), lambda k, c_ref: (k, 0, 0))
    s_o = pl.pallas_call(
        body_o, name="rs_add_halves_o",
        grid_spec=pltpu.PrefetchScalarGridSpec(
            num_scalar_prefetch=1, grid=(N_CHIPS,),
            in_specs=[gspec] * 3 + [rspec] * 3,
            out_specs=pl.BlockSpec((3, None, hr, D_MODEL), lambda k, c_ref: (0, k, 0, 0))),
        out_shape=jax.ShapeDtypeStruct((3, N_CHIPS, hr, D_MODEL), BF16),
        compiler_params=_params("parallel"),
    )(core, *g_o4, *r_o)
    return s_in, s_o


def _sum_chips(r_in, r_o, s_in, s_o):
    _, hd, sc = r_in.shape
    hr = r_o.shape[2]
    tm = min(256, hd)
    nb = hd // tm
    pos = jnp.stack([2 * lax.axis_index("x") + lax.axis_index("y"), lax.axis_index("c")]).astype(jnp.int32)

    def chip_sum(pos_ref, r_ref, s_ref):
        acc = None
        for k in range(N_CHIPS):
            term = jnp.where(pos_ref[0] == k, s_ref[...], r_ref[k]).astype(F32)
            acc = term if acc is None else acc + term
        return acc

    def body_in(pos_ref, r_ref, s_ref, o_ref):
        o_ref[...] = chip_sum(pos_ref, r_ref, s_ref)

    f_in = pl.pallas_call(
        body_in, name="rs_sum_chips_in",
        grid_spec=pltpu.PrefetchScalarGridSpec(
            num_scalar_prefetch=1, grid=(nb,),
            in_specs=[pl.BlockSpec((N_CHIPS, tm, sc), lambda i, p: (0, i, 0)),
                      pl.BlockSpec((None, tm, sc), lambda i, p: (p[0], i, 0))],
            out_specs=pl.BlockSpec((tm, sc), lambda i, p: (p[1] * nb + i, 0))),
        out_shape=jax.ShapeDtypeStruct((2 * hd, sc), F32),
        compiler_params=_params("parallel"),
    )(pos, r_in, s_in)

    def body_o(pos_ref, r_ref, s_ref, o_ref):
        o_ref[...] = chip_sum(pos_ref, r_ref, s_ref)

    f_o = pl.pallas_call(
        body_o, name="rs_sum_chips_o",
        grid_spec=pltpu.PrefetchScalarGridSpec(
            num_scalar_prefetch=1, grid=(3,),
            in_specs=[pl.BlockSpec((N_CHIPS, None, hr, D_MODEL), lambda n, p: (0, n, 0, 0)),
                      pl.BlockSpec((None, None, hr, D_MODEL), lambda n, p: (n, p[0], 0, 0))],
            out_specs=pl.BlockSpec((None, hr, D_MODEL), lambda n, p: (n, p[1], 0))),
        out_shape=jax.ShapeDtypeStruct((3, 2 * hr, D_MODEL), F32),
        compiler_params=_params("parallel"),
    )(pos, r_o, s_o)
    return f_in, f_o


def _share_halves(f_in, f_o):
    hd, sc = f_in.shape[0] // 2, f_in.shape[1]
    hr = f_o.shape[1] // 2

    def body(fi_ref, fo_ref, gi_ref, go_ref, send_sems, recv_sems):
        x, y, c, _ = _mesh_pos()
        sibling = (x, y, 1 - c)

        def halves(cc):
            rows_i, rows_o = pl.ds(cc * hd, hd), pl.ds(cc * hr, hr)
            return (fi_ref.at[rows_i, :], gi_ref.at[rows_i, :]), (fo_ref.at[:, rows_o, :], go_ref.at[:, rows_o, :])

        def copies(cc):
            return [pltpu.make_async_remote_copy(src_ref=src, dst_ref=dst, send_sem=send_sems.at[n],
                                                 recv_sem=recv_sems.at[n], device_id=sibling, device_id_type=MESH)
                    for n, (src, dst) in enumerate(halves(cc))]

        sends = copies(c)
        for cp in sends:
            cp.start()
        for cp in copies(1 - c):
            cp.wait_recv()
        for cp in sends:
            cp.wait_send()

    return pl.pallas_call(
        body, name="rs_share_halves",
        in_specs=[ANY, ANY], out_specs=(ANY, ANY),
        out_shape=(jax.ShapeDtypeStruct(f_in.shape, F32), jax.ShapeDtypeStruct(f_o.shape, F32)),
        scratch_shapes=[pltpu.SemaphoreType.DMA((2,)), pltpu.SemaphoreType.DMA((2,))],
        input_output_aliases={0: 0, 1: 1},
        compiler_params=pltpu.CompilerParams(has_side_effects=True),
    )(f_in, f_o)


def _merge_forward(ya, yb, wo3, p, bg, tm, tn):
    t = ya.shape[0]

    def body(ya_ref, yb_ref, wa_ref, wb_ref, g_ref, bg_ref, m_ref, oab_ref):
        oa = _dot(ya_ref[...], wa_ref[...])
        ob = _dot(yb_ref[...], wb_ref[...])
        ga = _sigmoid(g_ref[0].astype(F32) + bg_ref[0])
        gb = _sigmoid(g_ref[1].astype(F32) + bg_ref[1])
        m_ref[...] = (ga * oa + gb * ob).astype(BF16)
        oab_ref[0] = oa.astype(BF16)
        oab_ref[1] = ob.astype(BF16)

    return pl.pallas_call(
        body, name="merge_forward", grid=(t // tm, D_MODEL // tn),
        in_specs=[pl.BlockSpec((tm, D_MODEL), lambda i, j: (i, 0)),
                  pl.BlockSpec((tm, D_MODEL), lambda i, j: (i, 0)),
                  pl.BlockSpec((None, D_MODEL, tn), lambda i, j: (0, 0, j)),
                  pl.BlockSpec((None, D_MODEL, tn), lambda i, j: (1, 0, j)),
                  pl.BlockSpec((2, tm, tn), lambda i, j: (BLOCK_G, i, j)),
                  pl.BlockSpec((2, 1, tn), lambda i, j: (0, 0, j))],
        out_specs=(pl.BlockSpec((tm, tn), lambda i, j: (i, j)),
                   pl.BlockSpec((2, tm, tn), lambda i, j: (0, i, j))),
        out_shape=(jax.ShapeDtypeStruct((t, D_MODEL), BF16), jax.ShapeDtypeStruct((2, t, D_MODEL), BF16)),
        compiler_params=_params("parallel", "parallel"),
    )(ya, yb, wo3, wo3, p, bg)


HEAD_ROWS = 256


def _head(merged, wo3, x, target, ln_g, ln_b, tm):
    t = x.shape[0]
    inv_d = 1.0 / D_MODEL

    def body(m_ref, w_ref, x_ref, t_ref, g_ref, b_ref, dr_ref, gx_ref, dm_ref, dg_ref, db_ref, loss_ref):
        i = pl.program_id(0)

        @pl.when(i == 0)
        def _():
            dg_ref[...] = jnp.zeros_like(dg_ref)
            db_ref[...] = jnp.zeros_like(db_ref)
            loss_ref[...] = jnp.zeros_like(loss_ref)

        w = w_ref[...]
        g = g_ref[...]
        tiles = [slice(r0, r0 + HEAD_ROWS) for r0 in range(0, tm, HEAD_ROWS)]
        firsts = [_dot(m_ref[rows, :], w) for rows in tiles]
        for rows, out in zip(tiles, firsts):
            r = DN_ALPHA * x_ref[rows, :] + out
            mu = jnp.mean(r, axis=-1, keepdims=True)
            xc = r - mu
            var = jnp.mean(xc * xc, axis=-1, keepdims=True)
            rstd = lax.rsqrt(var + LN_EPS)
            xhat = xc * rstd
            e = xhat * g + b_ref[...] - t_ref[rows, :]
            se = jnp.sum(jnp.sum(e * e, axis=1, keepdims=True), axis=0, keepdims=True)
            loss_ref[...] += jnp.broadcast_to((0.5 * inv_d) * se, loss_ref.shape)
            dy = e * inv_d
            db_ref[...] += jnp.sum(dy, axis=0, keepdims=True)
            dg_ref[...] += jnp.sum(dy * xhat, axis=0, keepdims=True)
            dxh = dy * g
            m1 = jnp.mean(dxh, axis=-1, keepdims=True)
            m2 = jnp.mean(dxh * xhat, axis=-1, keepdims=True)
            dr = rstd * (dxh - m1 - xhat * m2)
            gx_ref[rows, :] = DN_ALPHA * dr
            drb = dr.astype(BF16)
            dr_ref[rows, :] = drb
            dm_ref[rows, :] = _dot(drb, w, 1, 1).astype(BF16)

    row = pl.BlockSpec((tm, D_MODEL), lambda i: (i, 0))
    vec = pl.BlockSpec((1, D_MODEL), lambda i: (0, 0))
    return pl.pallas_call(
        body, name="head", grid=(t // tm,),
        in_specs=[row, pl.BlockSpec((None, D_MODEL, D_MODEL), lambda i: (2, 0, 0), pipeline_mode=pl.Buffered(1)),
                  row, row, vec, vec],
        out_specs=(row, row, row, vec, vec, pl.BlockSpec((1, 128), lambda i: (0, 0))),
        out_shape=(jax.ShapeDtypeStruct((t, D_MODEL), BF16), jax.ShapeDtypeStruct((t, D_MODEL), F32),
                   jax.ShapeDtypeStruct((t, D_MODEL), BF16), jax.ShapeDtypeStruct((1, D_MODEL), F32),
                   jax.ShapeDtypeStruct((1, D_MODEL), F32), jax.ShapeDtypeStruct((1, 128), F32)),
        compiler_params=_params("arbitrary"),
    )(merged, wo3, x, target, ln_g, ln_b)


def _gate_and_branch_backward(dmerged, oab, p, bg, wo3, tm):
    t = dmerged.shape[0]

    def body(dm_ref, oab_ref, g_ref, bg_ref, wa_ref, wb_ref, do_ref, dpg_ref, dbg_ref, dy_ref):
        @pl.when(pl.program_id(0) == 0)
        def _():
            dbg_ref[...] = jnp.zeros_like(dbg_ref)

        dm = dm_ref[...].astype(F32)
        for n, w_ref in enumerate((wa_ref, wb_ref)):
            gate = _sigmoid(g_ref[n].astype(F32) + bg_ref[n])
            d_o = (dm * gate).astype(BF16)
            do_ref[n] = d_o
            dgate = dm * oab_ref[n].astype(F32) * gate * (1.0 - gate)
            dpg_ref[n] = dgate.astype(BF16)
            dbg_ref[n] += jnp.sum(dgate, axis=0, keepdims=True)
            dy_ref[n] = _dot(d_o, w_ref[...], 1, 1).astype(BF16)

    pair = pl.BlockSpec((2, tm, D_MODEL), lambda i: (0, i, 0))
    gates = pl.BlockSpec((2, tm, D_MODEL), lambda i: (BLOCK_G, i, 0))
    vec = pl.BlockSpec((2, 1, D_MODEL), lambda i: (0, 0, 0))

    def weight(n):
        return pl.BlockSpec((None, D_MODEL, D_MODEL), lambda i: (n, 0, 0), pipeline_mode=pl.Buffered(1))

    pair_shape = jax.ShapeDtypeStruct((2, t, D_MODEL), BF16)
    return pl.pallas_call(
        body, name="gate_and_branch_backward", grid=(t // tm,),
        in_specs=[pl.BlockSpec((tm, D_MODEL), lambda i: (i, 0)), pair, gates, vec, weight(0), weight(1)],
        out_specs=(pair, gates, vec, pair),
        out_shape=(pair_shape, jax.ShapeDtypeStruct((N_SLOTS, t, D_MODEL), BF16),
                   jax.ShapeDtypeStruct((2, 1, D_MODEL), F32), pair_shape),
        compiler_params=_params("arbitrary"),
    )(dmerged, oab, p, bg, wo3, wo3)


def _weight_grad(a, a_sel, b, b_sel, tm, tn, tk, name):
    t = a.shape[-2]
    nk = t // tk

    def body(a_ref, b_ref, o_ref, acc_ref):
        k = pl.program_id(2)

        @pl.when(k == 0)
        def _():
            acc_ref[...] = jnp.zeros_like(acc_ref)

        acc_ref[...] += _dot(a_ref[...], b_ref[...], 0, 0)

        @pl.when(k == nk - 1)
        def _():
            o_ref[...] = acc_ref[...].astype(BF16)

    def spec(arr, sel, width, which):
        if arr.ndim == 2:
            return pl.BlockSpec((tk, width), lambda i, j, k: (k, (i, j)[which]))
        return pl.BlockSpec((None, tk, width), lambda i, j, k: (sel, k, (i, j)[which]))

    return pl.pallas_call(
        body, name=name, grid=(D_MODEL // tm, D_MODEL // tn, nk),
        in_specs=[spec(a, a_sel, tm, 0), spec(b, b_sel, tn, 1)],
        out_specs=pl.BlockSpec((tm, tn), lambda i, j, k: (i, j)),
        out_shape=jax.ShapeDtypeStruct((D_MODEL, D_MODEL), BF16),
        scratch_shapes=[pltpu.VMEM((tm, tn), F32)],
        compiler_params=_params("parallel", "parallel", "arbitrary"),
    )(a, b)


def _win_grad(xt, dp, tn, tk):
    _, t, _ = dp.shape
    nk = t // tk
    per_seg = D_MODEL // tn
    nj = N_SEG * per_seg

    def body(x_ref, dp_ref, o_ref, acc_ref):
        k = pl.program_id(1)

        @pl.when(k == 0)
        def _():
            acc_ref[...] = jnp.zeros_like(acc_ref)

        acc_ref[...] += _dot(x_ref[...], dp_ref[...])

        @pl.when(k == nk - 1)
        def _():
            o_ref[...] = acc_ref[...].astype(BF16)

    return pl.pallas_call(
        body, name="grad_w_in", grid=(nj, nk),
        in_specs=[pl.BlockSpec((D_MODEL, tk), lambda j, k: (0, k)),
                  pl.BlockSpec((None, tk, tn), lambda j, k: (_slot_of_seg(j // per_seg), k, j % per_seg))],
        out_specs=pl.BlockSpec((D_MODEL, tn), lambda j, k: (0, j)),
        out_shape=jax.ShapeDtypeStruct((D_MODEL, N_SEG * D_MODEL), BF16),
        scratch_shapes=[pltpu.VMEM((D_MODEL, tn), F32)],
        compiler_params=_params("parallel", "arbitrary"),
    )(xt, dp)


def _input_grad_and_scatter(dp, w_full, gx, s_in, s_o, small, tm):
    _, t, _ = dp.shape
    ni, nk = t // tm, N_SEG - 1
    _, hd, sc = s_in.shape
    hr = s_o.shape[2]

    def body(dp_ref, w_ref, gx_ref, si_ref, so_ref, sm_ref, o_ref, ri_ref, ro_ref, ga_ref,
             send_sems, recv_sems, local_sem):
        i, k = pl.program_id(0), pl.program_id(1)
        x, y, c, chips = _mesh_pos()
        me = 2 * x + y
        dev = 4 * x + 2 * y + c

        def peer(r):
            return (x ^ ((r >> 2) & 1), y ^ ((r >> 1) & 1), c ^ (r & 1))

        def sends():
            cps = []
            for q, (cx, cy) in enumerate(chips):
                dest = 2 * cx + cy
                cps.append(pltpu.make_async_remote_copy(src_ref=si_ref.at[dest], dst_ref=ri_ref.at[me],
                                                        send_sem=send_sems.at[q], recv_sem=recv_sems.at[q],
                                                        device_id=(cx, cy, c), device_id_type=MESH))
                cps.append(pltpu.make_async_remote_copy(src_ref=so_ref.at[:, dest], dst_ref=ro_ref.at[me],
                                                        send_sem=send_sems.at[3 + q], recv_sem=recv_sems.at[3 + q],
                                                        device_id=(cx, cy, c), device_id_type=MESH))
            for r in range(1, 8):
                cps.append(pltpu.make_async_remote_copy(src_ref=sm_ref, dst_ref=ga_ref.at[dev],
                                                        send_sem=send_sems.at[5 + r], recv_sem=recv_sems.at[5 + r],
                                                        device_id=peer(r), device_id_type=MESH))
            return cps

        own_small = pltpu.make_async_copy(sm_ref, ga_ref.at[dev], local_sem)

        @pl.when(jnp.logical_and(i == 0, k == 0))
        def _():
            for cp in sends():
                cp.start()
            own_small.start()

        @pl.when(k == 0)
        def _():
            o_ref[...] = gx_ref[...]

        o_ref[...] += _dot(dp_ref[...], w_ref[...], 1, 1)

        @pl.when(jnp.logical_and(i == ni - 1, k == nk - 1))
        def _():
            for q, (cx, cy) in enumerate(chips):
                frm = 2 * cx + cy
                pltpu.make_async_remote_copy(src_ref=si_ref.at[frm], dst_ref=ri_ref.at[frm], send_sem=send_sems.at[q],
                                             recv_sem=recv_sems.at[q], device_id=(x, y, c),
                                             device_id_type=MESH).wait_recv()
                pltpu.make_async_remote_copy(src_ref=so_ref.at[:, frm], dst_ref=ro_ref.at[frm],
                                             send_sem=send_sems.at[3 + q], recv_sem=recv_sems.at[3 + q],
                                             device_id=(x, y, c), device_id_type=MESH).wait_recv()
            for r in range(1, 8):
                px, py, pc = peer(r)
                pltpu.make_async_remote_copy(src_ref=sm_ref, dst_ref=ga_ref.at[4 * px + 2 * py + pc],
                                             send_sem=send_sems.at[5 + r], recv_sem=recv_sems.at[5 + r],
                                             device_id=(x, y, c), device_id_type=MESH).wait_recv()
            for cp in sends():
                cp.wait_send()
            own_small.wait()

    return pl.pallas_call(
        body, name="grad_x_and_scatter", grid=(ni, nk),
        in_specs=[pl.BlockSpec((None, tm, D_MODEL), lambda i, k: (_slot_of_seg(k), i, 0)),
                  pl.BlockSpec((D_MODEL, D_MODEL), lambda i, k: (0, k)),
                  pl.BlockSpec((tm, D_MODEL), lambda i, k: (i, 0)), ANY, ANY, ANY],
        out_specs=(pl.BlockSpec((tm, D_MODEL), lambda i, k: (i, 0)), ANY, ANY, ANY),
        out_shape=(jax.ShapeDtypeStruct((t, D_MODEL), F32),
                   jax.ShapeDtypeStruct((N_CHIPS, hd, sc), BF16),
                   jax.ShapeDtypeStruct((N_CHIPS, 3, hr, D_MODEL), BF16),
                   jax.ShapeDtypeStruct((8,) + small.shape, small.dtype)),
        scratch_shapes=[pltpu.SemaphoreType.DMA((13,)), pltpu.SemaphoreType.DMA((13,)), pltpu.SemaphoreType.DMA],
        input_output_aliases={2: 0},
        compiler_params=pltpu.CompilerParams(dimension_semantics=("arbitrary", "arbitrary"),
                                             vmem_limit_bytes=VMEM_LIMIT, has_side_effects=True),
    )(dp, w_full, gx, s_in, s_o, small)


def _input_grad_tail(dp, w_full, partial, tm):
    _, t, _ = dp.shape
    seg = N_SEG - 1

    def body(dp_ref, w_ref, part_ref, o_ref):
        o_ref[...] = part_ref[...] + _dot(dp_ref[...], w_ref[...], 1, 1)

    row = pl.BlockSpec((tm, D_MODEL), lambda i: (i, 0))
    return pl.pallas_call(
        body, name="grad_x_tail", grid=(t // tm,),
        in_specs=[pl.BlockSpec((None, tm, D_MODEL), lambda i: (SLOT_OF_SEG[seg], i, 0)),
                  pl.BlockSpec((D_MODEL, D_MODEL), lambda i: (0, seg)), row],
        out_specs=row,
        out_shape=jax.ShapeDtypeStruct((t, D_MODEL), F32),
        compiler_params=_params("parallel"),
    )(dp, w_full, partial)


def _sgu_chunk_forward(u, v, z, wm, bs, lng, lnb):
    ug, dug = _gelu_and_grad(u)
    vg, dvg = _gelu_and_grad(v)
    mu = jnp.mean(vg, axis=-1, keepdims=True)
    xc = vg - mu
    var = jnp.mean(xc * xc, axis=-1, keepdims=True)
    rstd = lax.rsqrt(var + LN_EPS)
    vhat = xc * rstd
    vln = (vhat * lng + lnb).astype(BF16)
    mixed = _dot(wm, vln) + bs
    sig = _sigmoid(z)
    return ug, dug, dvg, rstd, vhat, vln, mixed, sig


def _mixer_a_forward(p_a, wm, bs_col, ln_v_g, ln_v_b, tm):
    t = p_a.shape[1]

    def body(p_ref, wm_ref, bs_ref, g_ref, b_ref, o_ref):
        wm_v, bs_v, lng, lnb = wm_ref[...], bs_ref[...], g_ref[...], b_ref[...]

        def chunk(ci, carry):
            rows = pl.ds(pl.multiple_of(ci * CHUNK, CHUNK), CHUNK)
            u = p_ref[0, rows, :].astype(F32)
            v = p_ref[1, rows, :].astype(F32)
            z = p_ref[2, rows, :].astype(F32)
            ug, _, _, _, _, _, mixed, sig = _sgu_chunk_forward(u, v, z, wm_v, bs_v, lng, lnb)
            o_ref[rows, :] = (ug * mixed * (z * sig)).astype(BF16)
            return carry

        lax.fori_loop(0, tm // CHUNK, chunk, 0, unroll=True)

    return pl.pallas_call(
        body, name="mixer_a_forward", grid=(t // tm, N_HEADS),
        in_specs=[pl.BlockSpec((3, tm, HEAD_DIM), lambda i, h: (0, i, h)),
                  pl.BlockSpec((None, CHUNK, CHUNK), lambda i, h: (h, 0, 0)),
                  pl.BlockSpec((None, CHUNK, 1), lambda i, h: (h, 0, 0)),
                  pl.BlockSpec((1, HEAD_DIM), lambda i, h: (0, h)),
                  pl.BlockSpec((1, HEAD_DIM), lambda i, h: (0, h))],
        out_specs=pl.BlockSpec((tm, HEAD_DIM), lambda i, h: (i, h)),
        out_shape=jax.ShapeDtypeStruct((t, D_MODEL), BF16),
        compiler_params=_params("parallel", "parallel"),
    )(p_a, wm, bs_col, ln_v_g, ln_v_b)


def _mixer_a_backward(p_a, dyab, wm, bs_col, ln_v_g, ln_v_b, dp, tm):
    t = p_a.shape[1]

    def body(p_ref, dy_ref, wm_ref, bs_ref, g_ref, b_ref, dp_in, dp_ref, dws_ref, dbs_ref, dg_ref, db_ref):
        @pl.when(pl.program_id(1) == 0)
        def _():
            dws_ref[...] = jnp.zeros_like(dws_ref)
            dbs_ref[...] = jnp.zeros_like(dbs_ref)
            dg_ref[...] = jnp.zeros_like(dg_ref)
            db_ref[...] = jnp.zeros_like(db_ref)

        wm_v, bs_v, lng, lnb = wm_ref[...], bs_ref[...], g_ref[...], b_ref[...]
        causal = (lax.broadcasted_iota(jnp.int32, (CHUNK, CHUNK), 1)
                  <= lax.broadcasted_iota(jnp.int32, (CHUNK, CHUNK), 0))

        def chunk(ci, carry):
            rows = pl.ds(pl.multiple_of(ci * CHUNK, CHUNK), CHUNK)
            u = p_ref[0, rows, :].astype(F32)
            v = p_ref[1, rows, :].astype(F32)
            z = p_ref[2, rows, :].astype(F32)
            dy = dy_ref[rows, :].astype(F32)
            ug, dug, dvg, rstd, vhat, vln, mixed, sig = _sgu_chunk_forward(u, v, z, wm_v, bs_v, lng, lnb)
            sz = z * sig
            dmixed = dy * ug * sz
            dp_ref[0, rows, :] = (dy * mixed * sz * dug).astype(BF16)
            dp_ref[2, rows, :] = (dy * ug * mixed * (sig * (1.0 + z * (1.0 - sig)))).astype(BF16)
            dbs_ref[...] += jnp.sum(dmixed, axis=1, keepdims=True)
            dmb = dmixed.astype(BF16)
            dws_ref[...] += jnp.where(causal, _dot(dmb, vln, 1, 1), 0.0)
            dvln = _dot(wm_v, dmb, 0, 0)
            db_ref[...] += jnp.sum(dvln, axis=0, keepdims=True)
            dg_ref[...] += jnp.sum(dvln * vhat, axis=0, keepdims=True)
            dvh = dvln * lng
            m1 = jnp.mean(dvh, axis=-1, keepdims=True)
            m2 = jnp.mean(dvh * vhat, axis=-1, keepdims=True)
            dp_ref[1, rows, :] = (rstd * (dvh - m1 - vhat * m2) * dvg).astype(BF16)
            return carry

        lax.fori_loop(0, tm // CHUNK, chunk, 0, unroll=True)

    return pl.pallas_call(
        body, name="mixer_a_backward", grid=(N_HEADS, t // tm),
        in_specs=[pl.BlockSpec((3, tm, HEAD_DIM), lambda h, i: (0, i, h)),
                  pl.BlockSpec((None, tm, HEAD_DIM), lambda h, i: (0, i, h)),
                  pl.BlockSpec((None, CHUNK, CHUNK), lambda h, i: (h, 0, 0)),
                  pl.BlockSpec((None, CHUNK, 1), lambda h, i: (h, 0, 0)),
                  pl.BlockSpec((1, HEAD_DIM), lambda h, i: (0, h)),
                  pl.BlockSpec((1, HEAD_DIM), lambda h, i: (0, h)), ANY],
        out_specs=(pl.BlockSpec((3, tm, HEAD_DIM), lambda h, i: (BLOCK_A, i, h)),
                   pl.BlockSpec((None, CHUNK, CHUNK), lambda h, i: (h, 0, 0)),
                   pl.BlockSpec((None, CHUNK, 1), lambda h, i: (h, 0, 0)),
                   pl.BlockSpec((1, HEAD_DIM), lambda h, i: (0, h)),
                   pl.BlockSpec((1, HEAD_DIM), lambda h, i: (0, h))),
        out_shape=(jax.ShapeDtypeStruct(dp.shape, BF16),
                   jax.ShapeDtypeStruct((N_HEADS, CHUNK, CHUNK), F32),
                   jax.ShapeDtypeStruct((N_HEADS, CHUNK, 1), F32),
                   jax.ShapeDtypeStruct((1, D_MODEL), F32), jax.ShapeDtypeStruct((1, D_MODEL), F32)),
        input_output_aliases={6: 0},
        compiler_params=_params("parallel", "arbitrary"),
    )(p_a, dyab, wm, bs_col, ln_v_g, ln_v_b, dp)


HALO = 16


def _conv_taps(h, halo_h, tm):
    row = lax.broadcasted_iota(jnp.int32, h.shape, 0)
    last1 = halo_h[HALO - 1:HALO, :]
    last2 = halo_h[HALO - 2:HALO - 1, :]
    h1 = jnp.where(row == 0, last1, pltpu.roll(h, 1, 0))
    h2 = jnp.where(row == 0, last2, jnp.where(row == 1, last1, pltpu.roll(h, 2, 0)))
    return h1, h2


def _mixer_b_forward(p_b, conv_w, conv_b, tm, tc):
    t = p_b.shape[1]

    def body(p_ref, halo_ref, w_ref, b_ref, o_ref):
        valid = (pl.program_id(1) > 0).astype(F32)
        h = p_ref[1].astype(F32) * p_ref[0].astype(F32)
        halo_h = halo_ref[1].astype(F32) * halo_ref[0].astype(F32) * valid
        h1, h2 = _conv_taps(h, halo_h, tm)
        w = w_ref[...]
        conv = b_ref[...] + w[0:1, :] * h2 + w[1:2, :] * h1 + w[2:3, :] * h
        z = p_ref[3].astype(F32)
        o_ref[...] = (p_ref[2].astype(F32) * conv * (z * _sigmoid(z))).astype(BF16)

    steps = tm // HALO
    return pl.pallas_call(
        body, name="mixer_b_forward", grid=(D_MODEL // tc, t // tm),
        in_specs=[pl.BlockSpec((4, tm, tc), lambda j, i: (BLOCK_B, i, j)),
                  pl.BlockSpec((4, HALO, tc), lambda j, i: (BLOCK_B, jnp.maximum(i * steps - 1, 0), j)),
                  pl.BlockSpec((3, tc), lambda j, i: (0, j)),
                  pl.BlockSpec((1, tc), lambda j, i: (0, j))],
        out_specs=pl.BlockSpec((tm, tc), lambda j, i: (i, j)),
        out_shape=jax.ShapeDtypeStruct((t, D_MODEL), BF16),
        compiler_params=_params("parallel", "parallel"),
    )(p_b, p_b, conv_w, conv_b)


def _mixer_b_backward(p_b, dyab, conv_w, conv_b, dp, tm, tc):
    t = p_b.shape[1]
    n = t // tm

    def body(p_ref, halo_ref, dy_ref, w_ref, b_ref, dp_in, dp_ref, dw_ref, db_ref, next_ref):
        ii = pl.program_id(1)

        @pl.when(ii == 0)
        def _():
            dw_ref[...] = jnp.zeros_like(dw_ref)
            db_ref[...] = jnp.zeros_like(db_ref)
            next_ref[...] = jnp.zeros_like(next_ref)

        valid = (ii < n - 1).astype(F32)
        xb = p_ref[0].astype(F32)
        cb = p_ref[1].astype(F32)
        bb = p_ref[2].astype(F32)
        z = p_ref[3].astype(F32)
        h = cb * xb
        halo_h = halo_ref[1].astype(F32) * halo_ref[0].astype(F32) * valid
        h1, h2 = _conv_taps(h, halo_h, tm)
        w = w_ref[...]
        w0, w1, w2 = w[0:1, :], w[1:2, :], w[2:3, :]
        conv = b_ref[...] + w0 * h2 + w1 * h1 + w2 * h
        sig = _sigmoid(z)
        sz = z * sig
        dy = dy_ref[...].astype(F32)
        dconv = dy * bb * sz
        dp_ref[2] = (dy * conv * sz).astype(BF16)
        dp_ref[3] = (dy * bb * conv * (sig * (1.0 + z * (1.0 - sig)))).astype(BF16)
        db_ref[...] += jnp.sum(dconv, axis=0, keepdims=True)
        dw_ref[0:1, :] += jnp.sum(dconv * h2, axis=0, keepdims=True)
        dw_ref[1:2, :] += jnp.sum(dconv * h1, axis=0, keepdims=True)
        dw_ref[2:3, :] += jnp.sum(dconv * h, axis=0, keepdims=True)
        row = lax.broadcasted_iota(jnp.int32, h.shape, 0)
        nxt = next_ref[...]
        n0, n1 = nxt[0:1, :], nxt[1:2, :]
        d1 = jnp.where(row == tm - 1, n0, pltpu.roll(dconv, tm - 1, 0))
        d2 = jnp.where(row == tm - 1, n1, jnp.where(row == tm - 2, n0, pltpu.roll(dconv, tm - 2, 0)))
        dh = w2 * dconv + w1 * d1 + w0 * d2
        dp_ref[0] = (dh * cb).astype(BF16)
        dp_ref[1] = (dh * xb).astype(BF16)
        next_ref[...] = dconv[0:8, :]

    steps = tm // HALO
    return pl.pallas_call(
        body, name="mixer_b_backward", grid=(D_MODEL // tc, n),
        in_specs=[pl.BlockSpec((4, tm, tc), lambda j, ii: (BLOCK_B, n - 1 - ii, j)),
                  pl.BlockSpec((4, HALO, tc), lambda j, ii: (BLOCK_B, jnp.maximum((n - 1 - ii) * steps - 1, 0), j)),
                  pl.BlockSpec((None, tm, tc), lambda j, ii: (1, n - 1 - ii, j)),
                  pl.BlockSpec((3, tc), lambda j, ii: (0, j)),
                  pl.BlockSpec((1, tc), lambda j, ii: (0, j)), ANY],
        out_specs=(pl.BlockSpec((4, tm, tc), lambda j, ii: (BLOCK_B, n - 1 - ii, j)),
                   pl.BlockSpec((3, tc), lambda j, ii: (0, j)),
                   pl.BlockSpec((1, tc), lambda j, ii: (0, j))),
        out_shape=(jax.ShapeDtypeStruct(dp.shape, BF16),
                   jax.ShapeDtypeStruct((3, D_MODEL), F32), jax.ShapeDtypeStruct((1, D_MODEL), F32)),
        scratch_shapes=[pltpu.VMEM((8, tc), F32)],
        input_output_aliases={5: 0},
        compiler_params=_params("parallel", "arbitrary"),
    )(p_b, p_b, dyab, conv_w, conv_b, dp)


def _adam_math(w, g, m, v):
    m = ADAM_B1 * m + (1.0 - ADAM_B1) * g
    v = ADAM_B2 * v + (1.0 - ADAM_B2) * (g * g)
    delta = -ADAM_LR * ((m * ADAM_C1) / (jnp.sqrt(v * ADAM_C2) + ADAM_EPS) + ADAM_WD * w)
    return delta, m, v


def _adam_rows(w, g, m, v, tm, name):
    r, c = w.shape

    def body(w_ref, g_ref, m_ref, v_ref, go_ref, d_ref, mo_ref, vo_ref):
        g = g_ref[...]
        d, mn, vn = _adam_math(w_ref[...], g, m_ref[...], v_ref[...])
        go_ref[...] = g
        d_ref[...] = d
        mo_ref[...] = mn
        vo_ref[...] = vn

    spec = pl.BlockSpec((tm, c), lambda i: (i, 0))
    shape = jax.ShapeDtypeStruct((r, c), F32)
    return pl.pallas_call(
        body, name=name, grid=(r // tm,),
        in_specs=[spec] * 4, out_specs=(spec,) * 4, out_shape=(shape,) * 4,
        compiler_params=_params("parallel"),
    )(w, g, m, v)


SMALL_ROW0 = {name: sum(r for _, r in SMALL_ROWS[:i]) for i, (name, _) in enumerate(SMALL_ROWS)}
LANE_MAJOR = ("ln_g", "ln_b", "b_gate", "ln_v_g", "ln_v_b", "conv_b")


def _lane_pieces(n):
    return [(q, slice(q * 128, (q + 1) * 128)) for q in range(n // 128)]


def _pack_small(d_ln_g, d_ln_b, d_bg, d_lnv_g, d_lnv_b, d_ws, d_bs, d_cw, d_cb, loss_part):
    def body(lg, lb, bg, vg, vb, ws, bs, cw, cb, loss, o_ref):
        def put(row0, vec):
            for q, cols in _lane_pieces(vec.shape[1]):
                o_ref[row0 + q:row0 + q + 1, :] = vec[:, cols]

        put(SMALL_ROW0["ln_g"], lg[...])
        put(SMALL_ROW0["ln_b"], lb[...])
        for n in range(2):
            put(SMALL_ROW0["b_gate"] + n * (D_MODEL // 128), bg[n])
        put(SMALL_ROW0["ln_v_g"], vg[...])
        put(SMALL_ROW0["ln_v_b"], vb[...])
        for h in range(N_HEADS):
            o_ref[SMALL_ROW0["w_s"] + h * CHUNK:SMALL_ROW0["w_s"] + (h + 1) * CHUNK, :] = ws[h]
        o_ref[SMALL_ROW0["b_s"]:SMALL_ROW0["b_s"] + N_HEADS, :] = bs[...]
        for c in range(3):
            put(SMALL_ROW0["conv_w"] + c * (D_MODEL // 128), cw[c:c + 1, :])
        put(SMALL_ROW0["conv_b"], cb[...])
        o_ref[SMALL_ROW0["loss"]:SMALL_ROW0["loss"] + 8, :] = jnp.broadcast_to(loss[...], (8, 128))

    return pl.pallas_call(
        body, name="pack_small", out_shape=jax.ShapeDtypeStruct((SMALL_TOTAL, 128), F32), compiler_params=_params(),
    )(d_ln_g, d_ln_b, d_bg, d_lnv_g, d_lnv_b, d_ws, d_bs.reshape(N_HEADS, CHUNK), d_cw, d_cb, loss_part)


def _adam_small(gathered, params):
    names = list(params)
    flat = [a for n in names for a in params[n]]

    def body(*refs):
        ga_ref = refs[0]
        ins = refs[1:1 + 3 * len(names)]
        outs = refs[1 + 3 * len(names):-1]
        gs_ref = refs[-1]
        g = ga_ref[0]
        for k in range(1, 8):
            g = g + ga_ref[k]
        gs_ref[...] = g
        for i, name in enumerate(names):
            w_ref, m_ref, v_ref = ins[3 * i:3 * i + 3]
            o_refs = outs[4 * i:4 * i + 4]
            row0 = SMALL_ROW0[name]
            if name in LANE_MAJOR:
                pieces = [((slice(None), cols), slice(row0 + q, row0 + q + 1))
                          for q, cols in _lane_pieces(w_ref.shape[1])]
            elif name == "w_s":
                pieces = [((0, h), slice(row0 + h * CHUNK, row0 + (h + 1) * CHUNK)) for h in range(N_HEADS)]
            else:
                pieces = [((0,), slice(row0, row0 + N_HEADS))]
            for idx, rows in pieces:
                gp = gs_ref[rows, :]
                res = (gp,) + _adam_math(w_ref[idx], gp, m_ref[idx], v_ref[idx])
                for o_ref, val in zip(o_refs, res):
                    o_ref[idx] = val
        gcw_ref, loss_ref = outs[-2:]
        for c in range(3):
            for q, cols in _lane_pieces(D_MODEL):
                r = SMALL_ROW0["conv_w"] + c * (D_MODEL // 128) + q
                gcw_ref[c:c + 1, cols] = gs_ref[r:r + 1, :]
        loss_ref[...] = gs_ref[SMALL_ROW0["loss"]:SMALL_ROW0["loss"] + 1, :]

    out_shape = [jax.ShapeDtypeStruct(params[n][0].shape, F32) for n in names for _ in range(4)]
    out_shape += [jax.ShapeDtypeStruct((3, D_MODEL), F32), jax.ShapeDtypeStruct((1, 128), F32)]
    res = pl.pallas_call(
        body, name="adam_small", out_shape=tuple(out_shape),
        scratch_shapes=[pltpu.VMEM((SMALL_TOTAL, 128), F32)], compiler_params=_params(),
    )(gathered, *flat)
    return {n: res[4 * i:4 * i + 4] for i, n in enumerate(names)}, res[-2], res[-1]


def _adam_conv_w(g_all, chip, w, m, v):
    cols = w.shape[2]

    def body(c_ref, g_ref, w_ref, m_ref, v_ref, go_ref, d_ref, mo_ref, vo_ref):
        g = g_ref[...]
        d, mn, vn = _adam_math(w_ref[...], g, m_ref[...], v_ref[...])
        go_ref[...] = g
        d_ref[...] = d
        mo_ref[...] = mn
        vo_ref[...] = vn

    own = pl.BlockSpec((None, 3, cols), lambda i, c_ref: (0, 0, 0))
    return pl.pallas_call(
        body, name="adam_conv_w",
        grid_spec=pltpu.PrefetchScalarGridSpec(
            num_scalar_prefetch=1, grid=(1,),
            in_specs=[pl.BlockSpec((3, cols), lambda i, c_ref: (0, c_ref[0])), own, own, own],
            out_specs=(own,) * 4),
        out_shape=(jax.ShapeDtypeStruct(w.shape, F32),) * 4,
        compiler_params=_params("arbitrary"),
    )(chip, g_all, w, m, v)


def kernel(x, w_in, b_gate, ln_v_g, ln_v_b, w_s, b_s, conv_w, conv_b, w_oa, w_ob, w_out, ln_g, ln_b, loss_target, m_w_in, m_b_gate, m_ln_v_g, m_ln_v_b, m_w_s, m_b_s, m_conv_w, m_conv_b, m_w_oa, m_w_ob, m_w_out, m_ln_g, m_ln_b, v_w_in, v_b_gate, v_ln_v_g, v_ln_v_b, v_w_s, v_b_s, v_conv_w, v_conv_b, v_w_oa, v_w_ob, v_w_out, v_ln_g, v_ln_b):
    t = x.shape[1]
    x2 = x[0]
    target = loss_target[0]
    chip = 2 * lax.axis_index("x") + lax.axis_index("y")
    conv_cols = conv_w.shape[2]

    chip1 = chip.astype(jnp.int32).reshape(1)
    w_pre = _cast_into_columns(w_in[0], chip1, N_CHIPS, 256, "cast_w_in")
    wo_b, wm = _prep_small_weights(w_oa[0], w_ob[0], w_out[0], w_s[0])
    conv_w8 = jnp.concatenate([conv_w[0], jnp.zeros((5, conv_cols), F32)], axis=0)
    bs_col = b_s[0].reshape(N_HEADS, CHUNK, 1)
    bg = b_gate.reshape(2, 1, D_MODEL)

    xb, xt = _cast_and_transpose(x2, min(512, t))
    p, w_full, wo_full, cw_full = _gather_and_project(xb, w_pre, wo_b, conv_w8, min(2048, t))
    wo3 = wo_full.reshape(3, D_MODEL, D_MODEL)
    conv_w_all = jnp.transpose(cw_full[:, :3, :], (1, 0, 2)).reshape(3, D_MODEL)
    tm_a = min(512, t)
    ya = _mixer_a_forward(p, wm, bs_col, ln_v_g, ln_v_b, tm_a)
    tm_b = min(512, t)
    yb = _mixer_b_forward(p, conv_w_all, conv_b, tm_b, 512)
    tm_m = min(1024, t)
    merged, oab = _merge_forward(ya, yb, wo3, p, bg, tm_m, 512)

    drb, gx, dmerged, d_ln_g, d_ln_b, loss_part = _head(merged, wo3, x2, target, ln_g, ln_b, min(512, t))
    doab, dp, d_bg, dyab = _gate_and_branch_backward(dmerged, oab, p, bg, wo3, min(256, t))
    dp, d_ws, d_bs, d_lnv_g, d_lnv_b = _mixer_a_backward(p, dyab, wm, bs_col, ln_v_g, ln_v_b, dp, tm_a)
    dp, d_cw, d_cb = _mixer_b_backward(p, dyab, conv_w_all, conv_b, dp, tm_b, 512)

    tk = min(2048, t)
    g_oa = _weight_grad(ya, 0, doab, 0, 1024, 1024, tk, "grad_w_oa")
    g_ob = _weight_grad(yb, 0, doab, 1, 1024, 1024, tk, "grad_w_ob")
    g_out = _weight_grad(merged, 0, drb, 0, 1024, 1024, tk, "grad_w_out")
    g_in = _win_grad(xt, dp, 1024, tk)

    r_in, *r_o = _exchange_halves(g_in, (g_oa, g_ob, g_out))
    s_in, s_o = _add_halves(g_in, (g_oa, g_ob, g_out), r_in, r_o)
    small_part = _pack_small(d_ln_g, d_ln_b, d_bg, d_lnv_g, d_lnv_b, d_ws, d_bs, d_cw, d_cb, loss_part)
    tm_x = min(512, t)
    gx, q_in, q_o, gathered = _input_grad_and_scatter(dp, w_full, gx, s_in, s_o, small_part, tm_x)
    grad_x = _input_grad_tail(dp, w_full, gx, tm_x)
    f_in, f_o = _sum_chips(q_in, q_o, s_in, s_o)
    gsum_in, gsum_o = _share_halves(f_in, f_o)

    big = {}
    big["w_in"] = _adam_rows(w_in[0], gsum_in, m_w_in[0], v_w_in[0], 128, "adam_w_in")
    for n, (name, w, m, v) in enumerate((("w_oa", w_oa, m_w_oa, v_w_oa), ("w_ob", w_ob, m_w_ob, v_w_ob),
                                         ("w_out", w_out, m_w_out, v_w_out))):
        big[name] = _adam_rows(w[0], gsum_o[n], m[0], v[0], 256, "adam_" + name)

    small, g_conv_w, loss_row = _adam_small(gathered, {
        "ln_g": (ln_g, m_ln_g, v_ln_g), "ln_b": (ln_b, m_ln_b, v_ln_b), "b_gate": (b_gate, m_b_gate, v_b_gate),
        "ln_v_g": (ln_v_g, m_ln_v_g, v_ln_v_g), "ln_v_b": (ln_v_b, m_ln_v_b, v_ln_v_b),
        "w_s": (w_s, m_w_s, v_w_s), "b_s": (b_s, m_b_s, v_b_s), "conv_b": (conv_b, m_conv_b, v_conv_b)})
    small["conv_w"] = _adam_conv_w(g_conv_w, chip1, conv_w, m_conv_w, v_conv_w)
    loss = loss_row[0, 0]

    order = ("w_in", "b_gate", "ln_v_g", "ln_v_b", "w_s", "b_s", "conv_w", "conv_b", "w_oa", "w_ob", "w_out",
             "ln_g", "ln_b")
    outs = [loss, grad_x[None]]
    for which in range(4):
        for name in order:
            outs.append(big[name][which][None] if name in big else small[name][which])
    return tuple(outs)
```

```python
import functools
import math

import jax
import jax.numpy as jnp
from jax import lax
from jax.experimental import pallas as pl
from jax.experimental.pallas import tpu as pltpu

F32 = jnp.float32
BF16 = jnp.bfloat16

D_MODEL = 2048
N_HEADS = 8
HEAD_DIM = D_MODEL // N_HEADS
CHUNK = 128
N_SEG = 9
N_CHIPS = 4
SHARD_COLS = N_SEG * D_MODEL // N_CHIPS
COL_BLOCK = 512
BLOCKS_PER_SHARD = SHARD_COLS // COL_BLOCK
BLOCKS_PER_SEG = D_MODEL // COL_BLOCK
SUBS = 3
SHARD_ROWS = D_MODEL // N_CHIPS
N_SLOTS = 12
BLOCK_A, BLOCK_G, BLOCK_B = 0, 2, 2
SLOT_OF_SEG = (0, 1, 2, 8, 9, 10, 11, 4, 5)
DN_ALPHA = 2.0 ** 0.25
LN_EPS = 1e-5
GELU_K = math.sqrt(2.0 / math.pi)
GELU_C = 0.044715

ADAM_LR = 0.001
ADAM_B1 = 0.9
ADAM_B2 = 0.999
ADAM_EPS = 1e-08
ADAM_WD = 0.01
ADAM_STEP = 10
ADAM_C1 = 1.0 / (1.0 - ADAM_B1 ** ADAM_STEP)
ADAM_C2 = 1.0 / (1.0 - ADAM_B2 ** ADAM_STEP)

VMEM_LIMIT = 60 * 1024 * 1024
MESH = pl.DeviceIdType.MESH
ANY = pl.BlockSpec(memory_space=pl.ANY)

SMALL_ROWS = (("ln_g", 16), ("ln_b", 16), ("b_gate", 32), ("ln_v_g", 16), ("ln_v_b", 16),
              ("w_s", 1024), ("b_s", 8), ("conv_w", 48), ("conv_b", 16), ("loss", 8))
SMALL_TOTAL = sum(r for _, r in SMALL_ROWS)


def _params(*sem):
    return pltpu.CompilerParams(dimension_semantics=sem, vmem_limit_bytes=VMEM_LIMIT)


def _sigmoid(x):
    return 1.0 / (1.0 + jnp.exp(-x))


def _gelu_and_grad(x):
    x2 = x * x
    th = jnp.tanh(GELU_K * (x + GELU_C * x * x2))
    g = 0.5 * x * (1.0 + th)
    dg = 0.5 * (1.0 + th) + 0.5 * x * (1.0 - th * th) * (GELU_K * (1.0 + 3.0 * GELU_C * x2))
    return g, dg


def _gelu(x):
    return 0.5 * x * (1.0 + jnp.tanh(GELU_K * (x + GELU_C * x * x * x)))


def _dot(a, b, ca=1, cb=0):
    return lax.dot_general(a, b, (((ca,), (cb,)), ((), ())), preferred_element_type=F32)


def _cast_and_transpose(x, tm):
    t, d = x.shape

    def body(x_ref, o_ref, ot_ref):
        v = x_ref[...]
        o_ref[...] = v.astype(BF16)
        ot_ref[...] = v.T.astype(BF16)

    return pl.pallas_call(
        body, name="cast_x", grid=(t // tm,),
        in_specs=[pl.BlockSpec((tm, d), lambda i: (i, 0))],
        out_specs=(pl.BlockSpec((tm, d), lambda i: (i, 0)), pl.BlockSpec((d, tm), lambda i: (0, i))),
        out_shape=(jax.ShapeDtypeStruct((t, d), BF16), jax.ShapeDtypeStruct((d, t), BF16)),
        compiler_params=_params("parallel"),
    )(x)


def _cast_into_columns(w, slot, n_slots, tm, name):
    r, c = w.shape

    def body(s_ref, w_ref, o_ref):
        o_ref[...] = w_ref[...].astype(BF16)

    return pl.pallas_call(
        body, name=name,
        grid_spec=pltpu.PrefetchScalarGridSpec(
            num_scalar_prefetch=1, grid=(r // tm,),
            in_specs=[pl.BlockSpec((tm, c), lambda i, s_ref: (i, 0))],
            out_specs=pl.BlockSpec((tm, c), lambda i, s_ref: (i, s_ref[0]))),
        out_shape=jax.ShapeDtypeStruct((r, n_slots * c), BF16),
        compiler_params=_params("parallel"),
    )(slot, w)


def _prep_small_weights(w_oa, w_ob, w_out, w_s):
    rows = w_oa.shape[0]

    def body(a_ref, b_ref, c_ref, ws_ref, wo_ref, wm_ref):
        wo_ref[0] = a_ref[...].astype(BF16)
        wo_ref[1] = b_ref[...].astype(BF16)
        wo_ref[2] = c_ref[...].astype(BF16)
        t = lax.broadcasted_iota(jnp.int32, (CHUNK, CHUNK), 0)
        s = lax.broadcasted_iota(jnp.int32, (CHUNK, CHUNK), 1)
        for h in range(N_HEADS):
            wm_ref[h] = jnp.where(s <= t, ws_ref[h], 0.0).astype(BF16)

    return pl.pallas_call(
        body, name="prep_small_weights",
        out_shape=(jax.ShapeDtypeStruct((3, rows, D_MODEL), BF16),
                   jax.ShapeDtypeStruct((N_HEADS, CHUNK, CHUNK), BF16)),
        compiler_params=_params(),
    )(w_oa, w_ob, w_out, w_s)


def _mesh_pos():
    x, y, c = lax.axis_index("x"), lax.axis_index("y"), lax.axis_index("c")
    chips = [(1 - x, y), (x, 1 - y), (1 - x, 1 - y)]
    return x, y, c, chips


def _slot_of_seg(seg):
    return jnp.where(seg < 3, seg, jnp.where(seg < 7, seg + 5, seg - 3))


def _gather_and_project(xb, w_pre, wo_b, conv_w8, tm):
    t = xb.shape[0]
    d, sc = w_pre.shape[0], w_pre.shape[1] // N_CHIPS
    rows = wo_b.shape[1]
    hd, hr = d // 2, rows // 2
    nj = BLOCKS_PER_SHARD // SUBS
    pc = nj * COL_BLOCK
    units = N_CHIPS * SUBS
    ni = t // tm
    total = units * ni * nj
    mx, my = lax.axis_index("x"), lax.axis_index("y")
    order = jnp.stack([2 * mx + my, 2 * (1 - mx) + my, 2 * mx + (1 - my),
                       2 * (1 - mx) + (1 - my)]).astype(jnp.int32)

    def body(order_ref, x_ref, wpre_ref, wo_ref, cw_ref, p_ref, wf_ref, wof_ref, cwf_ref,
             wbuf, wsem, xbuf, xsem, send_sems, recv_sems, local_sems):
        x, y, c, chips = _mesh_pos()
        me = 2 * x + y
        sibling = (x, y, 1 - c)
        u, i, j = pl.program_id(0), pl.program_id(1), pl.program_id(2)
        n = (u * ni + i) * nj + j
        m = u * ni + i

        def rows_start(m_):
            pltpu.make_async_copy(x_ref.at[pl.ds(pl.multiple_of(lax.rem(m_, ni) * tm, tm), tm), :],
                                  xbuf.at[lax.rem(m_, 2)], xsem.at[lax.rem(m_, 2)]).start()

        def chip_of(q):
            return 2 * chips[q][0] + chips[q][1]

        def piece(ref, k, cc, r):
            return ref.at[pl.ds(cc * hd, hd), pl.ds(pl.multiple_of(k * sc + r * pc, COL_BLOCK), pc)]

        def wo_half(ref4, k, cc):
            return ref4.at[:, k, pl.ds(cc * hr, hr), :]

        def rcopy(sem, src, dst, to):
            return pltpu.make_async_remote_copy(src_ref=src, dst_ref=dst, send_sem=send_sems.at[sem],
                                                recv_sem=recv_sems.at[sem], device_id=to, device_id_type=MESH)

        def w_send(q, r):
            return rcopy(q * SUBS + r, piece(wpre_ref, me, c, r), piece(wf_ref, me, c, r), (*chips[q], c))

        def w_landed(q, r):
            return rcopy(q * SUBS + r, piece(wpre_ref, me, c, r), piece(wf_ref, chip_of(q), c, r), sibling)

        def w_forward(q, r, cc):
            ref = piece(wf_ref, chip_of(q), cc, r)
            return rcopy(9 + q * SUBS + r, ref, ref, sibling)

        def wo_send(q):
            return rcopy(18 + q, wo_ref.at[:, pl.ds(c * hr, hr), :], wo_half(wof_ref, me, c), (*chips[q], c))

        def wo_landed(q):
            return rcopy(18 + q, wo_ref.at[:, pl.ds(c * hr, hr), :], wo_half(wof_ref, chip_of(q), c), sibling)

        def wo_forward(q, cc):
            ref = wo_half(wof_ref, chip_of(q), cc)
            return rcopy(21 + q, ref, ref, sibling)

        def conv_send(q):
            return rcopy(24 + q, cw_ref, cwf_ref.at[me], (*chips[q], c))

        def local_copies():
            return [pltpu.make_async_copy(wo_ref, wof_ref.at[:, me], local_sems.at[0]),
                    pltpu.make_async_copy(cw_ref, cwf_ref.at[me], local_sems.at[1])]

        def tile_start(u_, j_, slot):
            g = order_ref[u_ // SUBS] * BLOCKS_PER_SHARD + lax.rem(u_, SUBS) * nj + j_
            cols = pl.ds(pl.multiple_of(g * COL_BLOCK, COL_BLOCK), COL_BLOCK)

            @pl.when(u_ < SUBS)
            def _():
                pltpu.make_async_copy(wpre_ref.at[:, cols], wbuf.at[slot], wsem.at[slot]).start()

            @pl.when(u_ >= SUBS)
            def _():
                pltpu.make_async_copy(wf_ref.at[:, cols], wbuf.at[slot], wsem.at[slot]).start()

        def end_of(u_):
            return jnp.logical_and(u == u_, jnp.logical_and(i == ni - 1, j == nj - 1))

        @pl.when(n == 0)
        def _():
            for cp in local_copies():
                cp.start()
            for r in range(SUBS):
                for q in (0, 1):
                    w_send(q, r).start()
            for q in range(3):
                conv_send(q).start()
            tile_start(0, 0, 0)
            rows_start(0)

        def pass_on(q, r):
            w_landed(q, r).wait_recv()
            w_forward(q, r, c).start()

        for u_ in range(1, units - 1):
            @pl.when(end_of(u_))
            def _(u_=u_):
                if u_ <= SUBS:
                    for q in (0, 1):
                        pass_on(q, u_ - 1)
                if u_ == SUBS:
                    for r in range(SUBS):
                        for q in (0, 1):
                            w_send(q, r).wait_send()
                    for r in range(SUBS):
                        w_send(2, r).start()
                if 2 * SUBS - 1 <= u_ <= 3 * SUBS - 2:
                    pass_on(2, u_ - (2 * SUBS - 1))
                if u_ == 3 * SUBS - 2:
                    for r in range(SUBS):
                        w_send(2, r).wait_send()
                    for q in range(3):
                        wo_send(q).start()
                nxt = u_ + 1
                if nxt >= SUBS:
                    w_forward(nxt // SUBS - 1, nxt % SUBS, 1 - c).wait_recv()

        n1 = n + 1

        @pl.when(n1 < total)
        def _():
            tile_start(n1 // (ni * nj), lax.rem(n1, nj), lax.rem(n1, 2))

        xslot = lax.rem(m, 2)

        @pl.when(j == 0)
        def _():
            @pl.when(m + 1 < units * ni)
            def _():
                rows_start(m + 1)

            pltpu.make_async_copy(x_ref.at[pl.ds(0, tm), :], xbuf.at[xslot], xsem.at[xslot]).wait()

        slot = lax.rem(n, 2)
        pltpu.make_async_copy(wpre_ref.at[:, pl.ds(0, COL_BLOCK)], wbuf.at[slot], wsem.at[slot]).wait()
        p_ref[...] = _dot(xbuf[xslot], wbuf[slot]).astype(BF16)

        @pl.when(n == total - 1)
        def _():
            for q in range(3):
                wo_landed(q).wait_recv()
                wo_forward(q, c).start()
            for q in range(3):
                wo_forward(q, 1 - c).wait_recv()
                rcopy(24 + q, cw_ref, cwf_ref.at[chip_of(q)], sibling).wait_recv()
            for q in range(3):
                for r in range(SUBS):
                    w_forward(q, r, c).wait_send()
                wo_send(q).wait_send()
                wo_forward(q, c).wait_send()
                conv_send(q).wait_send()
            for cp in local_copies():
                cp.wait()

    def p_map(u, i, j, o):
        g = o[u // SUBS] * BLOCKS_PER_SHARD + lax.rem(u, SUBS) * nj + j
        return (_slot_of_seg(g // BLOCKS_PER_SEG), i, lax.rem(g, BLOCKS_PER_SEG))

    return pl.pallas_call(
        body, name="gather_and_project",
        grid_spec=pltpu.PrefetchScalarGridSpec(
            num_scalar_prefetch=1, grid=(units, ni, nj),
            in_specs=[ANY, ANY, ANY, ANY],
            out_specs=(pl.BlockSpec((None, tm, COL_BLOCK), p_map), ANY, ANY, ANY),
            scratch_shapes=[pltpu.VMEM((2, d, COL_BLOCK), BF16), pltpu.SemaphoreType.DMA((2,)),
                            pltpu.VMEM((2, tm, D_MODEL), BF16), pltpu.SemaphoreType.DMA((2,)),
                            pltpu.SemaphoreType.DMA((27,)), pltpu.SemaphoreType.DMA((27,)),
                            pltpu.SemaphoreType.DMA((2,))]),
        out_shape=(jax.ShapeDtypeStruct((N_SLOTS, t, D_MODEL), BF16),
                   jax.ShapeDtypeStruct((d, N_CHIPS * sc), BF16),
                   jax.ShapeDtypeStruct((3, N_CHIPS, rows, D_MODEL), BF16),
                   jax.ShapeDtypeStruct((N_CHIPS,) + conv_w8.shape, F32)),
        input_output_aliases={2: 1},
        compiler_params=pltpu.CompilerParams(dimension_semantics=("arbitrary",) * 3, vmem_limit_bytes=VMEM_LIMIT,
                                             has_side_effects=True),
    )(order, xb, w_pre, wo_b, conv_w8)


def _exchange_halves(g_o):
    hr = SHARD_ROWS // 2
    g_o4 = [g.reshape(N_CHIPS, 2, hr, D_MODEL) for g in g_o]

    def body(ga_ref, gb_ref, gc_ref, ra_ref, rb_ref, rc_ref, send_sems, recv_sems):
        x, y, c, _ = _mesh_pos()
        sibling = (x, y, 1 - c)
        cps = []
        for n, (g_ref, r_ref) in enumerate(((ga_ref, ra_ref), (gb_ref, rb_ref), (gc_ref, rc_ref))):
            cps.append(pltpu.make_async_remote_copy(src_ref=g_ref.at[:, 1 - c], dst_ref=r_ref,
                                                    send_sem=send_sems.at[n], recv_sem=recv_sems.at[n],
                                                    device_id=sibling, device_id_type=MESH))
        for cp in cps:
            cp.start()
        for cp in cps:
            cp.wait()

    o_shape = jax.ShapeDtypeStruct((N_CHIPS, hr, D_MODEL), BF16)
    return pl.pallas_call(
        body, name="rs_exchange_halves",
        in_specs=[ANY] * 3, out_specs=(ANY,) * 3,
        out_shape=(o_shape, o_shape, o_shape),
        scratch_shapes=[pltpu.SemaphoreType.DMA((3,)), pltpu.SemaphoreType.DMA((3,))],
        compiler_params=pltpu.CompilerParams(has_side_effects=True),
    )(*g_o4)


def _add_halves(g_in, g_o, r_in, r_o):
    d, c9 = g_in.shape
    hd = d // 2
    hr = SHARD_ROWS // 2
    core = lax.axis_index("c").astype(jnp.int32).reshape(1)
    tm = min(512, hd)
    nb = hd // tm

    def body_in(c_ref, g_ref, r_ref, o_ref):
        o_ref[...] = (g_ref[...].astype(F32) + r_ref[...].astype(F32)).astype(BF16)

    s_in = pl.pallas_call(
        body_in, name="rs_add_halves_in",
        grid_spec=pltpu.PrefetchScalarGridSpec(
            num_scalar_prefetch=1, grid=(N_CHIPS, nb),
            in_specs=[pl.BlockSpec((tm, SHARD_COLS), lambda k, i, c_ref: (c_ref[0] * nb + i, k)),
                      pl.BlockSpec((tm, SHARD_COLS), lambda k, i, c_ref: (i, k))],
            out_specs=pl.BlockSpec((None, tm, SHARD_COLS), lambda k, i, c_ref: (k, i, 0))),
        out_shape=jax.ShapeDtypeStruct((N_CHIPS, hd, SHARD_COLS), BF16),
        compiler_params=_params("parallel", "parallel"),
    )(core, g_in, r_in)

    g_o4 = [g.reshape(N_CHIPS, 2, hr, D_MODEL) for g in g_o]

    def body_o(c_ref, ga_ref, gb_ref, gc_ref, ra_ref, rb_ref, rc_ref, o_ref):
        for n, (g_ref, r_ref) in enumerate(((ga_ref, ra_ref), (gb_ref, rb_ref), (gc_ref, rc_ref))):
            o_ref[n] = (g_ref[...].astype(F32) + r_ref[...].astype(F32)).astype(BF16)

    gspec = pl.BlockSpec((None, None, hr, D_MODEL), lambda k, c_ref: (k, c_ref[0], 0, 0))
    rspec = pl.BlockSpec((None, hr, D_MODEL), lambda k, c_ref: (k, 0, 0))
    s_o = pl.pallas_call(
        body_o, name="rs_add_halves_o",
        grid_spec=pltpu.PrefetchScalarGridSpec(
            num_scalar_prefetch=1, grid=(N_CHIPS,),
            in_specs=[gspec] * 3 + [rspec] * 3,
            out_specs=pl.BlockSpec((3, None, hr, D_MODEL), lambda k, c_ref: (0, k, 0, 0))),
        out_shape=jax.ShapeDtypeStruct((3, N_CHIPS, hr, D_MODEL), BF16),
        compiler_params=_params("parallel"),
    )(core, *g_o4, *r_o)
    return s_in, s_o


def _sum_chips(r_in, r_o, s_in, s_o):
    _, hd, sc = r_in.shape
    hr = r_o.shape[2]
    tm = min(256, hd)
    nb = hd // tm
    pos = jnp.stack([2 * lax.axis_index("x") + lax.axis_index("y"), lax.axis_index("c")]).astype(jnp.int32)

    def chip_sum(pos_ref, r_ref, s_ref):
        acc = None
        for k in range(N_CHIPS):
            term = jnp.where(pos_ref[0] == k, s_ref[...], r_ref[k]).astype(F32)
            acc = term if acc is None else acc + term
        return acc

    def body_in(pos_ref, r_ref, s_ref, o_ref):
        o_ref[...] = chip_sum(pos_ref, r_ref, s_ref)

    f_in = pl.pallas_call(
        body_in, name="rs_sum_chips_in",
        grid_spec=pltpu.PrefetchScalarGridSpec(
            num_scalar_prefetch=1, grid=(nb,),
            in_specs=[pl.BlockSpec((N_CHIPS, tm, sc), lambda i, p: (0, i, 0)),
                      pl.BlockSpec((None, tm, sc), lambda i, p: (p[0], i, 0))],
            out_specs=pl.BlockSpec((tm, sc), lambda i, p: (p[1] * nb + i, 0))),
        out_shape=jax.ShapeDtypeStruct((2 * hd, sc), F32),
        compiler_params=_params("parallel"),
    )(pos, r_in, s_in)

    def body_o(pos_ref, r_ref, s_ref, o_ref):
        o_ref[...] = chip_sum(pos_ref, r_ref, s_ref)

    f_o = pl.pallas_call(
        body_o, name="rs_sum_chips_o",
        grid_spec=pltpu.PrefetchScalarGridSpec(
            num_scalar_prefetch=1, grid=(3,),
            in_specs=[pl.BlockSpec((N_CHIPS, None, hr, D_MODEL), lambda n, p: (0, n, 0, 0)),
                      pl.BlockSpec((None, None, hr, D_MODEL), lambda n, p: (n, p[0], 0, 0))],
            out_specs=pl.BlockSpec((None, hr, D_MODEL), lambda n, p: (n, p[1], 0))),
        out_shape=jax.ShapeDtypeStruct((3, 2 * hr, D_MODEL), F32),
        compiler_params=_params("parallel"),
    )(pos, r_o, s_o)
    return f_in, f_o


def _share_halves(f_in, f_o):
    hd, sc = f_in.shape[0] // 2, f_in.shape[1]
    hr = f_o.shape[1] // 2

    def body(fi_ref, fo_ref, gi_ref, go_ref, send_sems, recv_sems):
        x, y, c, _ = _mesh_pos()
        sibling = (x, y, 1 - c)

        def halves(cc):
            rows_i, rows_o = pl.ds(cc * hd, hd), pl.ds(cc * hr, hr)
            return (fi_ref.at[rows_i, :], gi_ref.at[rows_i, :]), (fo_ref.at[:, rows_o, :], go_ref.at[:, rows_o, :])

        def copies(cc):
            return [pltpu.make_async_remote_copy(src_ref=src, dst_ref=dst, send_sem=send_sems.at[n],
                                                 recv_sem=recv_sems.at[n], device_id=sibling, device_id_type=MESH)
                    for n, (src, dst) in enumerate(halves(cc))]

        sends = copies(c)
        for cp in sends:
            cp.start()
        for cp in copies(1 - c):
            cp.wait_recv()
        for cp in sends:
            cp.wait_send()

    return pl.pallas_call(
        body, name="rs_share_halves",
        in_specs=[ANY, ANY], out_specs=(ANY, ANY),
        out_shape=(jax.ShapeDtypeStruct(f_in.shape, F32), jax.ShapeDtypeStruct(f_o.shape, F32)),
        scratch_shapes=[pltpu.SemaphoreType.DMA((2,)), pltpu.SemaphoreType.DMA((2,))],
        input_output_aliases={0: 0, 1: 1},
        compiler_params=pltpu.CompilerParams(has_side_effects=True),
    )(f_in, f_o)


def _merge_forward(ya, yb, wo3, p, bg, tm, tn):
    t = ya.shape[0]

    def body(ya_ref, yb_ref, wa_ref, wb_ref, g_ref, bg_ref, m_ref, oab_ref):
        oa = _dot(ya_ref[...], wa_ref[...])
        ob = _dot(yb_ref[...], wb_ref[...])
        ga = _sigmoid(g_ref[0].astype(F32) + bg_ref[0])
        gb = _sigmoid(g_ref[1].astype(F32) + bg_ref[1])
        m_ref[...] = (ga * oa + gb * ob).astype(BF16)
        oab_ref[0] = oa.astype(BF16)
        oab_ref[1] = ob.astype(BF16)

    return pl.pallas_call(
        body, name="merge_forward", grid=(t // tm, D_MODEL // tn),
        in_specs=[pl.BlockSpec((tm, D_MODEL), lambda i, j: (i, 0)),
                  pl.BlockSpec((tm, D_MODEL), lambda i, j: (i, 0)),
                  pl.BlockSpec((None, D_MODEL, tn), lambda i, j: (0, 0, j)),
                  pl.BlockSpec((None, D_MODEL, tn), lambda i, j: (1, 0, j)),
                  pl.BlockSpec((2, tm, tn), lambda i, j: (BLOCK_G, i, j)),
                  pl.BlockSpec((2, 1, tn), lambda i, j: (0, 0, j))],
        out_specs=(pl.BlockSpec((tm, tn), lambda i, j: (i, j)),
                   pl.BlockSpec((2, tm, tn), lambda i, j: (0, i, j))),
        out_shape=(jax.ShapeDtypeStruct((t, D_MODEL), BF16), jax.ShapeDtypeStruct((2, t, D_MODEL), BF16)),
        compiler_params=_params("parallel", "parallel"),
    )(ya, yb, wo3, wo3, p, bg)


HEAD_ROWS = 256


def _head(merged, wo3, x, target, ln_g, ln_b, tm):
    t = x.shape[0]
    inv_d = 1.0 / D_MODEL

    def body(m_ref, w_ref, x_ref, t_ref, g_ref, b_ref, dr_ref, gx_ref, dm_ref, dg_ref, db_ref, loss_ref):
        i = pl.program_id(0)

        @pl.when(i == 0)
        def _():
            dg_ref[...] = jnp.zeros_like(dg_ref)
            db_ref[...] = jnp.zeros_like(db_ref)
            loss_ref[...] = jnp.zeros_like(loss_ref)

        w = w_ref[...]
        g = g_ref[...]
        tiles = [slice(r0, r0 + HEAD_ROWS) for r0 in range(0, tm, HEAD_ROWS)]
        firsts = [_dot(m_ref[rows, :], w) for rows in tiles]
        for rows, out in zip(tiles, firsts):
            r = DN_ALPHA * x_ref[rows, :] + out
            mu = jnp.mean(r, axis=-1, keepdims=True)
            xc = r - mu
            var = jnp.mean(xc * xc, axis=-1, keepdims=True)
            rstd = lax.rsqrt(var + LN_EPS)
            xhat = xc * rstd
            e = xhat * g + b_ref[...] - t_ref[rows, :]
            se = jnp.sum(jnp.sum(e * e, axis=1, keepdims=True), axis=0, keepdims=True)
            loss_ref[...] += jnp.broadcast_to((0.5 * inv_d) * se, loss_ref.shape)
            dy = e * inv_d
            db_ref[...] += jnp.sum(dy, axis=0, keepdims=True)
            dg_ref[...] += jnp.sum(dy * xhat, axis=0, keepdims=True)
            dxh = dy * g
            m1 = jnp.mean(dxh, axis=-1, keepdims=True)
            m2 = jnp.mean(dxh * xhat, axis=-1, keepdims=True)
            dr = rstd * (dxh - m1 - xhat * m2)
            gx_ref[rows, :] = DN_ALPHA * dr
            drb = dr.astype(BF16)
            dr_ref[rows, :] = drb
            dm_ref[rows, :] = _dot(drb, w, 1, 1).astype(BF16)

    row = pl.BlockSpec((tm, D_MODEL), lambda i: (i, 0))
    vec = pl.BlockSpec((1, D_MODEL), lambda i: (0, 0))
    return pl.pallas_call(
        body, name="head", grid=(t // tm,),
        in_specs=[row, pl.BlockSpec((None, D_MODEL, D_MODEL), lambda i: (2, 0, 0), pipeline_mode=pl.Buffered(1)),
                  row, row, vec, vec],
        out_specs=(row, row, row, vec, vec, pl.BlockSpec((1, 128), lambda i: (0, 0))),
        out_shape=(jax.ShapeDtypeStruct((t, D_MODEL), BF16), jax.ShapeDtypeStruct((t, D_MODEL), F32),
                   jax.ShapeDtypeStruct((t, D_MODEL), BF16), jax.ShapeDtypeStruct((1, D_MODEL), F32),
                   jax.ShapeDtypeStruct((1, D_MODEL), F32), jax.ShapeDtypeStruct((1, 128), F32)),
        compiler_params=_params("arbitrary"),
    )(merged, wo3, x, target, ln_g, ln_b)


def _gate_and_branch_backward(dmerged, oab, p, bg, wo3, tm):
    t = dmerged.shape[0]

    def body(dm_ref, oab_ref, g_ref, bg_ref, wa_ref, wb_ref, do_ref, dpg_ref, dbg_ref, dy_ref):
        @pl.when(pl.program_id(0) == 0)
        def _():
            dbg_ref[...] = jnp.zeros_like(dbg_ref)

        dm = dm_ref[...].astype(F32)
        for n, w_ref in enumerate((wa_ref, wb_ref)):
            gate = _sigmoid(g_ref[n].astype(F32) + bg_ref[n])
            d_o = (dm * gate).astype(BF16)
            do_ref[n] = d_o
            dgate = dm * oab_ref[n].astype(F32) * gate * (1.0 - gate)
            dpg_ref[n] = dgate.astype(BF16)
            dbg_ref[n] += jnp.sum(dgate, axis=0, keepdims=True)
            dy_ref[n] = _dot(d_o, w_ref[...], 1, 1).astype(BF16)

    pair = pl.BlockSpec((2, tm, D_MODEL), lambda i: (0, i, 0))
    gates = pl.BlockSpec((2, tm, D_MODEL), lambda i: (BLOCK_G, i, 0))
    vec = pl.BlockSpec((2, 1, D_MODEL), lambda i: (0, 0, 0))

    def weight(n):
        return pl.BlockSpec((None, D_MODEL, D_MODEL), lambda i: (n, 0, 0), pipeline_mode=pl.Buffered(1))

    pair_shape = jax.ShapeDtypeStruct((2, t, D_MODEL), BF16)
    return pl.pallas_call(
        body, name="gate_and_branch_backward", grid=(t // tm,),
        in_specs=[pl.BlockSpec((tm, D_MODEL), lambda i: (i, 0)), pair, gates, vec, weight(0), weight(1)],
        out_specs=(pair, gates, vec, pair),
        out_shape=(pair_shape, jax.ShapeDtypeStruct((N_SLOTS, t, D_MODEL), BF16),
                   jax.ShapeDtypeStruct((2, 1, D_MODEL), F32), pair_shape),
        compiler_params=_params("arbitrary"),
    )(dmerged, oab, p, bg, wo3, wo3)


def _weight_grad(a, a_sel, b, b_sel, tm, tn, tk, name, exchange=None):
    t = a.shape[-2]
    nk = t // tk
    ni, nj = D_MODEL // tm, D_MODEL // tn

    def body(a_ref, b_ref, *rest):
        if exchange is None:
            o_ref, acc_ref = rest
        else:
            g_ref, o_ref, r_ref, acc_ref, send_sem, recv_sem = rest
        i, j, k = pl.program_id(0), pl.program_id(1), pl.program_id(2)

        if exchange is not None:
            x, y, c, _ = _mesh_pos()
            hd = exchange.shape[0] // 2
            swap = pltpu.make_async_remote_copy(src_ref=g_ref.at[pl.ds((1 - c) * hd, hd), :], dst_ref=r_ref,
                                                send_sem=send_sem, recv_sem=recv_sem,
                                                device_id=(x, y, 1 - c), device_id_type=MESH)

            @pl.when(jnp.logical_and(i == 0, jnp.logical_and(j == 0, k == 0)))
            def _():
                swap.start()

        @pl.when(k == 0)
        def _():
            acc_ref[...] = jnp.zeros_like(acc_ref)

        acc_ref[...] += _dot(a_ref[...], b_ref[...], 0, 0)

        @pl.when(k == nk - 1)
        def _():
            o_ref[...] = acc_ref[...].astype(BF16)

        if exchange is not None:
            @pl.when(jnp.logical_and(i == ni - 1, jnp.logical_and(j == nj - 1, k == nk - 1)))
            def _():
                swap.wait()

    def spec(arr, sel, width, which):
        if arr.ndim == 2:
            return pl.BlockSpec((tk, width), lambda i, j, k: (k, (i, j)[which]))
        return pl.BlockSpec((None, tk, width), lambda i, j, k: (sel, k, (i, j)[which]))

    o_spec = pl.BlockSpec((tm, tn), lambda i, j, k: (i, j))
    o_shape = jax.ShapeDtypeStruct((D_MODEL, D_MODEL), BF16)
    if exchange is None:
        return pl.pallas_call(
            body, name=name, grid=(ni, nj, nk),
            in_specs=[spec(a, a_sel, tm, 0), spec(b, b_sel, tn, 1)],
            out_specs=o_spec, out_shape=o_shape,
            scratch_shapes=[pltpu.VMEM((tm, tn), F32)],
            compiler_params=_params("parallel", "parallel", "arbitrary"),
        )(a, b)
    return pl.pallas_call(
        body, name=name, grid=(ni, nj, nk),
        in_specs=[spec(a, a_sel, tm, 0), spec(b, b_sel, tn, 1), ANY],
        out_specs=(o_spec, ANY),
        out_shape=(o_shape, jax.ShapeDtypeStruct((exchange.shape[0] // 2, exchange.shape[1]), BF16)),
        scratch_shapes=[pltpu.VMEM((tm, tn), F32), pltpu.SemaphoreType.DMA, pltpu.SemaphoreType.DMA],
        compiler_params=pltpu.CompilerParams(dimension_semantics=("arbitrary",) * 3, vmem_limit_bytes=VMEM_LIMIT,
                                             has_side_effects=True),
    )(a, b, exchange)


def _win_grad(xt, dp, tn, tk):
    _, t, _ = dp.shape
    nk = t // tk
    per_seg = D_MODEL // tn
    nj = N_SEG * per_seg

    def body(x_ref, dp_ref, o_ref, acc_ref):
        k = pl.program_id(1)

        @pl.when(k == 0)
        def _():
            acc_ref[...] = jnp.zeros_like(acc_ref)

        acc_ref[...] += _dot(x_ref[...], dp_ref[...])

        @pl.when(k == nk - 1)
        def _():
            o_ref[...] = acc_ref[...].astype(BF16)

    return pl.pallas_call(
        body, name="grad_w_in", grid=(nj, nk),
        in_specs=[pl.BlockSpec((D_MODEL, tk), lambda j, k: (0, k)),
                  pl.BlockSpec((None, tk, tn), lambda j, k: (_slot_of_seg(j // per_seg), k, j % per_seg))],
        out_specs=pl.BlockSpec((D_MODEL, tn), lambda j, k: (0, j)),
        out_shape=jax.ShapeDtypeStruct((D_MODEL, N_SEG * D_MODEL), BF16),
        scratch_shapes=[pltpu.VMEM((D_MODEL, tn), F32)],
        compiler_params=_params("parallel", "arbitrary"),
    )(xt, dp)


def _input_grad_and_scatter(dp, w_full, gx, s_in, s_o, small, tm):
    _, t, _ = dp.shape
    ni, nk = t // tm, N_SEG - 1
    _, hd, sc = s_in.shape
    hr = s_o.shape[2]

    def body(dp_ref, w_ref, gx_ref, si_ref, so_ref, sm_ref, o_ref, ri_ref, ro_ref, ga_ref,
             send_sems, recv_sems, local_sem):
        i, k = pl.program_id(0), pl.program_id(1)
        x, y, c, chips = _mesh_pos()
        me = 2 * x + y
        dev = 4 * x + 2 * y + c

        def peer(r):
            return (x ^ ((r >> 2) & 1), y ^ ((r >> 1) & 1), c ^ (r & 1))

        def sends():
            cps = []
            for q, (cx, cy) in enumerate(chips):
                dest = 2 * cx + cy
                cps.append(pltpu.make_async_remote_copy(src_ref=si_ref.at[dest], dst_ref=ri_ref.at[me],
                                                        send_sem=send_sems.at[q], recv_sem=recv_sems.at[q],
                                                        device_id=(cx, cy, c), device_id_type=MESH))
                cps.append(pltpu.make_async_remote_copy(src_ref=so_ref.at[:, dest], dst_ref=ro_ref.at[me],
                                                        send_sem=send_sems.at[3 + q], recv_sem=recv_sems.at[3 + q],
                                                        device_id=(cx, cy, c), device_id_type=MESH))
            for r in range(1, 8):
                cps.append(pltpu.make_async_remote_copy(src_ref=sm_ref, dst_ref=ga_ref.at[dev],
                                                        send_sem=send_sems.at[5 + r], recv_sem=recv_sems.at[5 + r],
                                                        device_id=peer(r), device_id_type=MESH))
            return cps

        own_small = pltpu.make_async_copy(sm_ref, ga_ref.at[dev], local_sem)

        @pl.when(jnp.logical_and(i == 0, k == 0))
        def _():
            for cp in sends():
                cp.start()
            own_small.start()

        @pl.when(k == 0)
        def _():
            o_ref[...] = gx_ref[...]

        o_ref[...] += _dot(dp_ref[...], w_ref[...], 1, 1)

        @pl.when(jnp.logical_and(i == ni - 1, k == nk - 1))
        def _():
            for q, (cx, cy) in enumerate(chips):
                frm = 2 * cx + cy
                pltpu.make_async_remote_copy(src_ref=si_ref.at[frm], dst_ref=ri_ref.at[frm], send_sem=send_sems.at[q],
                                             recv_sem=recv_sems.at[q], device_id=(x, y, c),
                                             device_id_type=MESH).wait_recv()
                pltpu.make_async_remote_copy(src_ref=so_ref.at[:, frm], dst_ref=ro_ref.at[frm],
                                             send_sem=send_sems.at[3 + q], recv_sem=recv_sems.at[3 + q],
                                             device_id=(x, y, c), device_id_type=MESH).wait_recv()
            for r in range(1, 8):
                px, py, pc = peer(r)
                pltpu.make_async_remote_copy(src_ref=sm_ref, dst_ref=ga_ref.at[4 * px + 2 * py + pc],
                                             send_sem=send_sems.at[5 + r], recv_sem=recv_sems.at[5 + r],
                                             device_id=(x, y, c), device_id_type=MESH).wait_recv()
            for cp in sends():
                cp.wait_send()
            own_small.wait()

    return pl.pallas_call(
        body, name="grad_x_and_scatter", grid=(ni, nk),
        in_specs=[pl.BlockSpec((None, tm, D_MODEL), lambda i, k: (_slot_of_seg(k), i, 0)),
                  pl.BlockSpec((D_MODEL, D_MODEL), lambda i, k: (0, k)),
                  pl.BlockSpec((tm, D_MODEL), lambda i, k: (i, 0)), ANY, ANY, ANY],
        out_specs=(pl.BlockSpec((tm, D_MODEL), lambda i, k: (i, 0)), ANY, ANY, ANY),
        out_shape=(jax.ShapeDtypeStruct((t, D_MODEL), F32),
                   jax.ShapeDtypeStruct((N_CHIPS, hd, sc), BF16),
                   jax.ShapeDtypeStruct((N_CHIPS, 3, hr, D_MODEL), BF16),
                   jax.ShapeDtypeStruct((8,) + small.shape, small.dtype)),
        scratch_shapes=[pltpu.SemaphoreType.DMA((13,)), pltpu.SemaphoreType.DMA((13,)), pltpu.SemaphoreType.DMA],
        input_output_aliases={2: 0},
        compiler_params=pltpu.CompilerParams(dimension_semantics=("arbitrary", "arbitrary"),
                                             vmem_limit_bytes=VMEM_LIMIT, has_side_effects=True),
    )(dp, w_full, gx, s_in, s_o, small)


def _input_grad_tail(dp, w_full, partial, tm):
    _, t, _ = dp.shape
    seg = N_SEG - 1

    def body(dp_ref, w_ref, part_ref, o_ref):
        o_ref[...] = part_ref[...] + _dot(dp_ref[...], w_ref[...], 1, 1)

    row = pl.BlockSpec((tm, D_MODEL), lambda i: (i, 0))
    return pl.pallas_call(
        body, name="grad_x_tail", grid=(t // tm,),
        in_specs=[pl.BlockSpec((None, tm, D_MODEL), lambda i: (SLOT_OF_SEG[seg], i, 0)),
                  pl.BlockSpec((D_MODEL, D_MODEL), lambda i: (0, seg)), row],
        out_specs=row,
        out_shape=jax.ShapeDtypeStruct((t, D_MODEL), F32),
        compiler_params=_params("parallel"),
    )(dp, w_full, partial)


def _sgu_chunk_forward(u, v, z, wm, bs, lng, lnb):
    ug, dug = _gelu_and_grad(u)
    vg, dvg = _gelu_and_grad(v)
    mu = jnp.mean(vg, axis=-1, keepdims=True)
    xc = vg - mu
    var = jnp.mean(xc * xc, axis=-1, keepdims=True)
    rstd = lax.rsqrt(var + LN_EPS)
    vhat = xc * rstd
    vln = (vhat * lng + lnb).astype(BF16)
    mixed = _dot(wm, vln) + bs
    sig = _sigmoid(z)
    return ug, dug, dvg, rstd, vhat, vln, mixed, sig


def _mixer_a_forward(p_a, wm, bs_col, ln_v_g, ln_v_b, tm):
    t = p_a.shape[1]

    def body(p_ref, wm_ref, bs_ref, g_ref, b_ref, o_ref):
        wm_v, bs_v, lng, lnb = wm_ref[...], bs_ref[...], g_ref[...], b_ref[...]

        def chunk(ci, carry):
            rows = pl.ds(pl.multiple_of(ci * CHUNK, CHUNK), CHUNK)
            u = p_ref[0, rows, :].astype(F32)
            v = p_ref[1, rows, :].astype(F32)
            z = p_ref[2, rows, :].astype(F32)
            ug, _, _, _, _, _, mixed, sig = _sgu_chunk_forward(u, v, z, wm_v, bs_v, lng, lnb)
            o_ref[rows, :] = (ug * mixed * (z * sig)).astype(BF16)
            return carry

        lax.fori_loop(0, tm // CHUNK, chunk, 0, unroll=True)

    return pl.pallas_call(
        body, name="mixer_a_forward", grid=(t // tm, N_HEADS),
        in_specs=[pl.BlockSpec((3, tm, HEAD_DIM), lambda i, h: (0, i, h)),
                  pl.BlockSpec((None, CHUNK, CHUNK), lambda i, h: (h, 0, 0)),
                  pl.BlockSpec((None, CHUNK, 1), lambda i, h: (h, 0, 0)),
                  pl.BlockSpec((1, HEAD_DIM), lambda i, h: (0, h)),
                  pl.BlockSpec((1, HEAD_DIM), lambda i, h: (0, h))],
        out_specs=pl.BlockSpec((tm, HEAD_DIM), lambda i, h: (i, h)),
        out_shape=jax.ShapeDtypeStruct((t, D_MODEL), BF16),
        compiler_params=_params("parallel", "parallel"),
    )(p_a, wm, bs_col, ln_v_g, ln_v_b)


def _mixer_a_backward(p_a, dyab, wm, bs_col, ln_v_g, ln_v_b, dp, tm):
    t = p_a.shape[1]

    def body(p_ref, dy_ref, wm_ref, bs_ref, g_ref, b_ref, dp_in, dp_ref, dws_ref, dbs_ref, dg_ref, db_ref):
        @pl.when(pl.program_id(1) == 0)
        def _():
            dws_ref[...] = jnp.zeros_like(dws_ref)
            dbs_ref[...] = jnp.zeros_like(dbs_ref)
            dg_ref[...] = jnp.zeros_like(dg_ref)
            db_ref[...] = jnp.zeros_like(db_ref)

        wm_v, bs_v, lng, lnb = wm_ref[...], bs_ref[...], g_ref[...], b_ref[...]
        causal = (lax.broadcasted_iota(jnp.int32, (CHUNK, CHUNK), 1)
                  <= lax.broadcasted_iota(jnp.int32, (CHUNK, CHUNK), 0))

        def chunk(ci, carry):
            rows = pl.ds(pl.multiple_of(ci * CHUNK, CHUNK), CHUNK)
            u = p_ref[0, rows, :].astype(F32)
            v = p_ref[1, rows, :].astype(F32)
            z = p_ref[2, rows, :].astype(F32)
            dy = dy_ref[rows, :].astype(F32)
            ug, dug, dvg, rstd, vhat, vln, mixed, sig = _sgu_chunk_forward(u, v, z, wm_v, bs_v, lng, lnb)
            sz = z * sig
            dmixed = dy * ug * sz
            dp_ref[0, rows, :] = (dy * mixed * sz * dug).astype(BF16)
            dp_ref[2, rows, :] = (dy * ug * mixed * (sig * (1.0 + z * (1.0 - sig)))).astype(BF16)
            dbs_ref[...] += jnp.sum(dmixed, axis=1, keepdims=True)
            dmb = dmixed.astype(BF16)
            dws_ref[...] += jnp.where(causal, _dot(dmb, vln, 1, 1), 0.0)
            dvln = _dot(wm_v, dmb, 0, 0)
            db_ref[...] += jnp.sum(dvln, axis=0, keepdims=True)
            dg_ref[...] += jnp.sum(dvln * vhat, axis=0, keepdims=True)
            dvh = dvln * lng
            m1 = jnp.mean(dvh, axis=-1, keepdims=True)
            m2 = jnp.mean(dvh * vhat, axis=-1, keepdims=True)
            dp_ref[1, rows, :] = (rstd * (dvh - m1 - vhat * m2) * dvg).astype(BF16)
            return carry

        lax.fori_loop(0, tm // CHUNK, chunk, 0, unroll=True)

    return pl.pallas_call(
        body, name="mixer_a_backward", grid=(N_HEADS, t // tm),
        in_specs=[pl.BlockSpec((3, tm, HEAD_DIM), lambda h, i: (0, i, h)),
                  pl.BlockSpec((None, tm, HEAD_DIM), lambda h, i: (0, i, h)),
                  pl.BlockSpec((None, CHUNK, CHUNK), lambda h, i: (h, 0, 0)),
                  pl.BlockSpec((None, CHUNK, 1), lambda h, i: (h, 0, 0)),
                  pl.BlockSpec((1, HEAD_DIM), lambda h, i: (0, h)),
                  pl.BlockSpec((1, HEAD_DIM), lambda h, i: (0, h)), ANY],
        out_specs=(pl.BlockSpec((3, tm, HEAD_DIM), lambda h, i: (BLOCK_A, i, h)),
                   pl.BlockSpec((None, CHUNK, CHUNK), lambda h, i: (h, 0, 0)),
                   pl.BlockSpec((None, CHUNK, 1), lambda h, i: (h, 0, 0)),
                   pl.BlockSpec((1, HEAD_DIM), lambda h, i: (0, h)),
                   pl.BlockSpec((1, HEAD_DIM), lambda h, i: (0, h))),
        out_shape=(jax.ShapeDtypeStruct(dp.shape, BF16),
                   jax.ShapeDtypeStruct((N_HEADS, CHUNK, CHUNK), F32),
                   jax.ShapeDtypeStruct((N_HEADS, CHUNK, 1), F32),
                   jax.ShapeDtypeStruct((1, D_MODEL), F32), jax.ShapeDtypeStruct((1, D_MODEL), F32)),
        input_output_aliases={6: 0},
        compiler_params=_params("parallel", "arbitrary"),
    )(p_a, dyab, wm, bs_col, ln_v_g, ln_v_b, dp)


HALO = 16


def _conv_taps(h, halo_h, tm):
    row = lax.broadcasted_iota(jnp.int32, h.shape, 0)
    last1 = halo_h[HALO - 1:HALO, :]
    last2 = halo_h[HALO - 2:HALO - 1, :]
    h1 = jnp.where(row == 0, last1, pltpu.roll(h, 1, 0))
    h2 = jnp.where(row == 0, last2, jnp.where(row == 1, last1, pltpu.roll(h, 2, 0)))
    return h1, h2


def _mixer_b_forward(p_b, conv_w, conv_b, tm, tc):
    t = p_b.shape[1]

    def body(p_ref, halo_ref, w_ref, b_ref, o_ref):
        valid = (pl.program_id(1) > 0).astype(F32)
        h = p_ref[1].astype(F32) * p_ref[0].astype(F32)
        halo_h = halo_ref[1].astype(F32) * halo_ref[0].astype(F32) * valid
        h1, h2 = _conv_taps(h, halo_h, tm)
        w = w_ref[...]
        conv = b_ref[...] + w[0:1, :] * h2 + w[1:2, :] * h1 + w[2:3, :] * h
        z = p_ref[3].astype(F32)
        o_ref[...] = (p_ref[2].astype(F32) * conv * (z * _sigmoid(z))).astype(BF16)

    steps = tm // HALO
    return pl.pallas_call(
        body, name="mixer_b_forward", grid=(D_MODEL // tc, t // tm),
        in_specs=[pl.BlockSpec((4, tm, tc), lambda j, i: (BLOCK_B, i, j)),
                  pl.BlockSpec((4, HALO, tc), lambda j, i: (BLOCK_B, jnp.maximum(i * steps - 1, 0), j)),
                  pl.BlockSpec((3, tc), lambda j, i: (0, j)),
                  pl.BlockSpec((1, tc), lambda j, i: (0, j))],
        out_specs=pl.BlockSpec((tm, tc), lambda j, i: (i, j)),
        out_shape=jax.ShapeDtypeStruct((t, D_MODEL), BF16),
        compiler_params=_params("parallel", "parallel"),
    )(p_b, p_b, conv_w, conv_b)


def _mixer_b_backward(p_b, dyab, conv_w, conv_b, dp, tm, tc):
    t = p_b.shape[1]
    n = t // tm

    def body(p_ref, halo_ref, dy_ref, w_ref, b_ref, dp_in, dp_ref, dw_ref, db_ref, next_ref):
        ii = pl.program_id(1)

        @pl.when(ii == 0)
        def _():
            dw_ref[...] = jnp.zeros_like(dw_ref)
            db_ref[...] = jnp.zeros_like(db_ref)
            next_ref[...] = jnp.zeros_like(next_ref)

        valid = (ii < n - 1).astype(F32)
        xb = p_ref[0].astype(F32)
        cb = p_ref[1].astype(F32)
        bb = p_ref[2].astype(F32)
        z = p_ref[3].astype(F32)
        h = cb * xb
        halo_h = halo_ref[1].astype(F32) * halo_ref[0].astype(F32) * valid
        h1, h2 = _conv_taps(h, halo_h, tm)
        w = w_ref[...]
        w0, w1, w2 = w[0:1, :], w[1:2, :], w[2:3, :]
        conv = b_ref[...] + w0 * h2 + w1 * h1 + w2 * h
        sig = _sigmoid(z)
        sz = z * sig
        dy = dy_ref[...].astype(F32)
        dconv = dy * bb * sz
        dp_ref[2] = (dy * conv * sz).astype(BF16)
        dp_ref[3] = (dy * bb * conv * (sig * (1.0 + z * (1.0 - sig)))).astype(BF16)
        db_ref[...] += jnp.sum(dconv, axis=0, keepdims=True)
        dw_ref[0:1, :] += jnp.sum(dconv * h2, axis=0, keepdims=True)
        dw_ref[1:2, :] += jnp.sum(dconv * h1, axis=0, keepdims=True)
        dw_ref[2:3, :] += jnp.sum(dconv * h, axis=0, keepdims=True)
        row = lax.broadcasted_iota(jnp.int32, h.shape, 0)
        nxt = next_ref[...]
        n0, n1 = nxt[0:1, :], nxt[1:2, :]
        d1 = jnp.where(row == tm - 1, n0, pltpu.roll(dconv, tm - 1, 0))
        d2 = jnp.where(row == tm - 1, n1, jnp.where(row == tm - 2, n0, pltpu.roll(dconv, tm - 2, 0)))
        dh = w2 * dconv + w1 * d1 + w0 * d2
        dp_ref[0] = (dh * cb).astype(BF16)
        dp_ref[1] = (dh * xb).astype(BF16)
        next_ref[...] = dconv[0:8, :]

    steps = tm // HALO
    return pl.pallas_call(
        body, name="mixer_b_backward", grid=(D_MODEL // tc, n),
        in_specs=[pl.BlockSpec((4, tm, tc), lambda j, ii: (BLOCK_B, n - 1 - ii, j)),
                  pl.BlockSpec((4, HALO, tc), lambda j, ii: (BLOCK_B, jnp.maximum((n - 1 - ii) * steps - 1, 0), j)),
                  pl.BlockSpec((None, tm, tc), lambda j, ii: (1, n - 1 - ii, j)),
                  pl.BlockSpec((3, tc), lambda j, ii: (0, j)),
                  pl.BlockSpec((1, tc), lambda j, ii: (0, j)), ANY],
        out_specs=(pl.BlockSpec((4, tm, tc), lambda j, ii: (BLOCK_B, n - 1 - ii, j)),
                   pl.BlockSpec((3, tc), lambda j, ii: (0, j)),
                   pl.BlockSpec((1, tc), lambda j, ii: (0, j))),
        out_shape=(jax.ShapeDtypeStruct(dp.shape, BF16),
                   jax.ShapeDtypeStruct((3, D_MODEL), F32), jax.ShapeDtypeStruct((1, D_MODEL), F32)),
        scratch_shapes=[pltpu.VMEM((8, tc), F32)],
        input_output_aliases={5: 0},
        compiler_params=_params("parallel", "arbitrary"),
    )(p_b, p_b, dyab, conv_w, conv_b, dp)


def _adam_math(w, g, m, v):
    m = ADAM_B1 * m + (1.0 - ADAM_B1) * g
    v = ADAM_B2 * v + (1.0 - ADAM_B2) * (g * g)
    delta = -ADAM_LR * ((m * ADAM_C1) / (jnp.sqrt(v * ADAM_C2) + ADAM_EPS) + ADAM_WD * w)
    return delta, m, v


def _adam_rows(w, g, m, v, tm, name):
    r, c = w.shape

    def body(w_ref, g_ref, m_ref, v_ref, go_ref, d_ref, mo_ref, vo_ref):
        g = g_ref[...]
        d, mn, vn = _adam_math(w_ref[...], g, m_ref[...], v_ref[...])
        go_ref[...] = g
        d_ref[...] = d
        mo_ref[...] = mn
        vo_ref[...] = vn

    spec = pl.BlockSpec((tm, c), lambda i: (i, 0))
    shape = jax.ShapeDtypeStruct((r, c), F32)
    return pl.pallas_call(
        body, name=name, grid=(r // tm,),
        in_specs=[spec] * 4, out_specs=(spec,) * 4, out_shape=(shape,) * 4,
        compiler_params=_params("parallel"),
    )(w, g, m, v)


SMALL_ROW0 = {name: sum(r for _, r in SMALL_ROWS[:i]) for i, (name, _) in enumerate(SMALL_ROWS)}
LANE_MAJOR = ("ln_g", "ln_b", "b_gate", "ln_v_g", "ln_v_b", "conv_b")


def _lane_pieces(n):
    return [(q, slice(q * 128, (q + 1) * 128)) for q in range(n // 128)]


def _pack_small(d_ln_g, d_ln_b, d_bg, d_lnv_g, d_lnv_b, d_ws, d_bs, d_cw, d_cb, loss_part):
    def body(lg, lb, bg, vg, vb, ws, bs, cw, cb, loss, o_ref):
        def put(row0, vec):
            for q, cols in _lane_pieces(vec.shape[1]):
                o_ref[row0 + q:row0 + q + 1, :] = vec[:, cols]

        put(SMALL_ROW0["ln_g"], lg[...])
        put(SMALL_ROW0["ln_b"], lb[...])
        for n in range(2):
            put(SMALL_ROW0["b_gate"] + n * (D_MODEL // 128), bg[n])
        put(SMALL_ROW0["ln_v_g"], vg[...])
        put(SMALL_ROW0["ln_v_b"], vb[...])
        for h in range(N_HEADS):
            o_ref[SMALL_ROW0["w_s"] + h * CHUNK:SMALL_ROW0["w_s"] + (h + 1) * CHUNK, :] = ws[h]
        o_ref[SMALL_ROW0["b_s"]:SMALL_ROW0["b_s"] + N_HEADS, :] = bs[...]
        for c in range(3):
            put(SMALL_ROW0["conv_w"] + c * (D_MODEL // 128), cw[c:c + 1, :])
        put(SMALL_ROW0["conv_b"], cb[...])
        o_ref[SMALL_ROW0["loss"]:SMALL_ROW0["loss"] + 8, :] = jnp.broadcast_to(loss[...], (8, 128))

    return pl.pallas_call(
        body, name="pack_small", out_shape=jax.ShapeDtypeStruct((SMALL_TOTAL, 128), F32), compiler_params=_params(),
    )(d_ln_g, d_ln_b, d_bg, d_lnv_g, d_lnv_b, d_ws, d_bs.reshape(N_HEADS, CHUNK), d_cw, d_cb, loss_part)


def _adam_small(gathered, params):
    names = list(params)
    flat = [a for n in names for a in params[n]]

    def body(*refs):
        ga_ref = refs[0]
        ins = refs[1:1 + 3 * len(names)]
        outs = refs[1 + 3 * len(names):-1]
        gs_ref = refs[-1]
        g = ga_ref[0]
        for k in range(1, 8):
            g = g + ga_ref[k]
        gs_ref[...] = g
        for i, name in enumerate(names):
            w_ref, m_ref, v_ref = ins[3 * i:3 * i + 3]
            o_refs = outs[4 * i:4 * i + 4]
            row0 = SMALL_ROW0[name]
            if name in LANE_MAJOR:
                pieces = [((slice(None), cols), slice(row0 + q, row0 + q + 1))
                          for q, cols in _lane_pieces(w_ref.shape[1])]
            elif name == "w_s":
                pieces = [((0, h), slice(row0 + h * CHUNK, row0 + (h + 1) * CHUNK)) for h in range(N_HEADS)]
            else:
                pieces = [((0,), slice(row0, row0 + N_HEADS))]
            for idx, rows in pieces:
                gp = gs_ref[rows, :]
                res = (gp,) + _adam_math(w_ref[idx], gp, m_ref[idx], v_ref[idx])
                for o_ref, val in zip(o_refs, res):
                    o_ref[idx] = val
        gcw_ref, loss_ref = outs[-2:]
        for c in range(3):
            for q, cols in _lane_pieces(D_MODEL):
                r = SMALL_ROW0["conv_w"] + c * (D_MODEL // 128) + q
                gcw_ref[c:c + 1, cols] = gs_ref[r:r + 1, :]
        loss_ref[...] = gs_ref[SMALL_ROW0["loss"]:SMALL_ROW0["loss"] + 1, :]

    out_shape = [jax.ShapeDtypeStruct(params[n][0].shape, F32) for n in names for _ in range(4)]
    out_shape += [jax.ShapeDtypeStruct((3, D_MODEL), F32), jax.ShapeDtypeStruct((1, 128), F32)]
    res = pl.pallas_call(
        body, name="adam_small", out_shape=tuple(out_shape),
        scratch_shapes=[pltpu.VMEM((SMALL_TOTAL, 128), F32)], compiler_params=_params(),
    )(gathered, *flat)
    return {n: res[4 * i:4 * i + 4] for i, n in enumerate(names)}, res[-2], res[-1]


def _adam_conv_w(g_all, chip, w, m, v):
    cols = w.shape[2]

    def body(c_ref, g_ref, w_ref, m_ref, v_ref, go_ref, d_ref, mo_ref, vo_ref):
        g = g_ref[...]
        d, mn, vn = _adam_math(w_ref[...], g, m_ref[...], v_ref[...])
        go_ref[...] = g
        d_ref[...] = d
        mo_ref[...] = mn
        vo_ref[...] = vn

    own = pl.BlockSpec((None, 3, cols), lambda i, c_ref: (0, 0, 0))
    return pl.pallas_call(
        body, name="adam_conv_w",
        grid_spec=pltpu.PrefetchScalarGridSpec(
            num_scalar_prefetch=1, grid=(1,),
            in_specs=[pl.BlockSpec((3, cols), lambda i, c_ref: (0, c_ref[0])), own, own, own],
            out_specs=(own,) * 4),
        out_shape=(jax.ShapeDtypeStruct(w.shape, F32),) * 4,
        compiler_params=_params("arbitrary"),
    )(chip, g_all, w, m, v)


def kernel(x, w_in, b_gate, ln_v_g, ln_v_b, w_s, b_s, conv_w, conv_b, w_oa, w_ob, w_out, ln_g, ln_b, loss_target, m_w_in, m_b_gate, m_ln_v_g, m_ln_v_b, m_w_s, m_b_s, m_conv_w, m_conv_b, m_w_oa, m_w_ob, m_w_out, m_ln_g, m_ln_b, v_w_in, v_b_gate, v_ln_v_g, v_ln_v_b, v_w_s, v_b_s, v_conv_w, v_conv_b, v_w_oa, v_w_ob, v_w_out, v_ln_g, v_ln_b):
    t = x.shape[1]
    x2 = x[0]
    target = loss_target[0]
    chip = 2 * lax.axis_index("x") + lax.axis_index("y")
    conv_cols = conv_w.shape[2]

    chip1 = chip.astype(jnp.int32).reshape(1)
    w_pre = _cast_into_columns(w_in[0], chip1, N_CHIPS, 256, "cast_w_in")
    wo_b, wm = _prep_small_weights(w_oa[0], w_ob[0], w_out[0], w_s[0])
    conv_w8 = jnp.concatenate([conv_w[0], jnp.zeros((5, conv_cols), F32)], axis=0)
    bs_col = b_s[0].reshape(N_HEADS, CHUNK, 1)
    bg = b_gate.reshape(2, 1, D_MODEL)

    xb, xt = _cast_and_transpose(x2, min(512, t))
    p, w_full, wo_full, cw_full = _gather_and_project(xb, w_pre, wo_b, conv_w8, min(2048, t))
    wo3 = wo_full.reshape(3, D_MODEL, D_MODEL)
    conv_w_all = jnp.transpose(cw_full[:, :3, :], (1, 0, 2)).reshape(3, D_MODEL)
    tm_a = min(512, t)
    ya = _mixer_a_forward(p, wm, bs_col, ln_v_g, ln_v_b, tm_a)
    tm_b = min(512, t)
    yb = _mixer_b_forward(p, conv_w_all, conv_b, tm_b, 512)
    tm_m = min(1024, t)
    merged, oab = _merge_forward(ya, yb, wo3, p, bg, tm_m, 512)

    drb, gx, dmerged, d_ln_g, d_ln_b, loss_part = _head(merged, wo3, x2, target, ln_g, ln_b, min(512, t))
    doab, dp, d_bg, dyab = _gate_and_branch_backward(dmerged, oab, p, bg, wo3, min(256, t))
    dp, d_ws, d_bs, d_lnv_g, d_lnv_b = _mixer_a_backward(p, dyab, wm, bs_col, ln_v_g, ln_v_b, dp, tm_a)
    dp, d_cw, d_cb = _mixer_b_backward(p, dyab, conv_w_all, conv_b, dp, tm_b, 512)

    tk = min(2048, t)
    g_in = _win_grad(xt, dp, 1024, tk)
    g_oa = _weight_grad(ya, 0, doab, 0, 1024, 1024, tk, "grad_w_oa")
    g_ob = _weight_grad(yb, 0, doab, 1, 1024, 1024, tk, "grad_w_ob")
    g_out, r_in = _weight_grad(merged, 0, drb, 0, 1024, 1024, tk, "grad_w_out", exchange=g_in)
    r_o = _exchange_halves((g_oa, g_ob, g_out))
    s_in, s_o = _add_halves(g_in, (g_oa, g_ob, g_out), r_in, r_o)
    small_part = _pack_small(d_ln_g, d_ln_b, d_bg, d_lnv_g, d_lnv_b, d_ws, d_bs, d_cw, d_cb, loss_part)
    tm_x = min(512, t)
    gx, q_in, q_o, gathered = _input_grad_and_scatter(dp, w_full, gx, s_in, s_o, small_part, tm_x)
    grad_x = _input_grad_tail(dp, w_full, gx, tm_x)
    f_in, f_o = _sum_chips(q_in, q_o, s_in, s_o)
    gsum_in, gsum_o = _share_halves(f_in, f_o)

    big = {}
    big["w_in"] = _adam_rows(w_in[0], gsum_in, m_w_in[0], v_w_in[0], 128, "adam_w_in")
    for n, (name, w, m, v) in enumerate((("w_oa", w_oa, m_w_oa, v_w_oa), ("w_ob", w_ob, m_w_ob, v_w_ob),
                                         ("w_out", w_out, m_w_out, v_w_out))):
        big[name] = _adam_rows(w[0], gsum_o[n], m[0], v[0], 256, "adam_" + name)

    small, g_conv_w, loss_row = _adam_small(gathered, {
        "ln_g": (ln_g, m_ln_g, v_ln_g), "ln_b": (ln_b, m_ln_b, v_ln_b), "b_gate": (b_gate, m_b_gate, v_b_gate),
        "ln_v_g": (ln_v_g, m_ln_v_g, v_ln_v_g), "ln_v_b": (ln_v_b, m_ln_v_b, v_ln_v_b),
        "w_s": (w_s, m_w_s, v_w_s), "b_s": (b_s, m_b_s, v_b_s), "conv_b": (conv_b, m_conv_b, v_conv_b)})
    small["conv_w"] = _adam_conv_w(g_conv_w, chip1, conv_w, m_conv_w, v_conv_w)
    loss = loss_row[0, 0]

    order = ("w_in", "b_gate", "ln_v_g", "ln_v_b", "w_s", "b_s", "conv_w", "conv_b", "w_oa", "w_ob", "w_out",
             "ln_g", "ln_b")
    outs = [loss, grad_x[None]]
    for which in range(4):
        for name in order:
            outs.append(big[name][which][None] if name in big else small[name][which])
    return tuple(outs)
```

```python
import functools
import math

import jax
import jax.numpy as jnp
from jax import lax
from jax.experimental import pallas as pl
from jax.experimental.pallas import tpu as pltpu

F32 = jnp.float32
BF16 = jnp.bfloat16

D_MODEL = 2048
N_HEADS = 8
HEAD_DIM = D_MODEL // N_HEADS
CHUNK = 128
N_SEG = 9
N_CHIPS = 4
SHARD_COLS = N_SEG * D_MODEL // N_CHIPS
COL_BLOCK = 512
BLOCKS_PER_SHARD = SHARD_COLS // COL_BLOCK
BLOCKS_PER_SEG = D_MODEL // COL_BLOCK
SUBS = 3
SHARD_ROWS = D_MODEL // N_CHIPS
N_SLOTS = 12
BLOCK_A, BLOCK_G, BLOCK_B = 0, 2, 2
SLOT_OF_SEG = (0, 1, 2, 8, 9, 10, 11, 4, 5)
DN_ALPHA = 2.0 ** 0.25
LN_EPS = 1e-5
GELU_K = math.sqrt(2.0 / math.pi)
GELU_C = 0.044715

ADAM_LR = 0.001
ADAM_B1 = 0.9
ADAM_B2 = 0.999
ADAM_EPS = 1e-08
ADAM_WD = 0.01
ADAM_STEP = 10
ADAM_C1 = 1.0 / (1.0 - ADAM_B1 ** ADAM_STEP)
ADAM_C2 = 1.0 / (1.0 - ADAM_B2 ** ADAM_STEP)

VMEM_LIMIT = 60 * 1024 * 1024
MESH = pl.DeviceIdType.MESH
ANY = pl.BlockSpec(memory_space=pl.ANY)

SMALL_ROWS = (("ln_g", 16), ("ln_b", 16), ("b_gate", 32), ("ln_v_g", 16), ("ln_v_b", 16),
              ("w_s", 1024), ("b_s", 8), ("conv_w", 48), ("conv_b", 16), ("loss", 8))
SMALL_TOTAL = sum(r for _, r in SMALL_ROWS)


def _params(*sem):
    return pltpu.CompilerParams(dimension_semantics=sem, vmem_limit_bytes=VMEM_LIMIT)


def _sigmoid(x):
    return 1.0 / (1.0 + jnp.exp(-x))


def _gelu_and_grad(x):
    x2 = x * x
    th = jnp.tanh(GELU_K * (x + GELU_C * x * x2))
    g = 0.5 * x * (1.0 + th)
    dg = 0.5 * (1.0 + th) + 0.5 * x * (1.0 - th * th) * (GELU_K * (1.0 + 3.0 * GELU_C * x2))
    return g, dg


def _gelu(x):
    return 0.5 * x * (1.0 + jnp.tanh(GELU_K * (x + GELU_C * x * x * x)))


def _dot(a, b, ca=1, cb=0):
    return lax.dot_general(a, b, (((ca,), (cb,)), ((), ())), preferred_element_type=F32)


def _cast_and_transpose(x, tm):
    t, d = x.shape

    def body(x_ref, o_ref, ot_ref):
        v = x_ref[...]
        o_ref[...] = v.astype(BF16)
        ot_ref[...] = v.T.astype(BF16)

    return pl.pallas_call(
        body, name="cast_x", grid=(t // tm,),
        in_specs=[pl.BlockSpec((tm, d), lambda i: (i, 0))],
        out_specs=(pl.BlockSpec((tm, d), lambda i: (i, 0)), pl.BlockSpec((d, tm), lambda i: (0, i))),
        out_shape=(jax.ShapeDtypeStruct((t, d), BF16), jax.ShapeDtypeStruct((d, t), BF16)),
        compiler_params=_params("parallel"),
    )(x)


def _cast_into_columns(w, slot, n_slots, tm, name):
    r, c = w.shape

    def body(s_ref, w_ref, o_ref):
        o_ref[...] = w_ref[...].astype(BF16)

    return pl.pallas_call(
        body, name=name,
        grid_spec=pltpu.PrefetchScalarGridSpec(
            num_scalar_prefetch=1, grid=(r // tm,),
            in_specs=[pl.BlockSpec((tm, c), lambda i, s_ref: (i, 0))],
            out_specs=pl.BlockSpec((tm, c), lambda i, s_ref: (i, s_ref[0]))),
        out_shape=jax.ShapeDtypeStruct((r, n_slots * c), BF16),
        compiler_params=_params("parallel"),
    )(slot, w)


def _prep_small_weights(w_oa, w_ob, w_out, w_s):
    rows = w_oa.shape[0]

    def body(a_ref, b_ref, c_ref, ws_ref, wo_ref, wm_ref):
        wo_ref[0] = a_ref[...].astype(BF16)
        wo_ref[1] = b_ref[...].astype(BF16)
        wo_ref[2] = c_ref[...].astype(BF16)
        t = lax.broadcasted_iota(jnp.int32, (CHUNK, CHUNK), 0)
        s = lax.broadcasted_iota(jnp.int32, (CHUNK, CHUNK), 1)
        for h in range(N_HEADS):
            wm_ref[h] = jnp.where(s <= t, ws_ref[h], 0.0).astype(BF16)

    return pl.pallas_call(
        body, name="prep_small_weights",
        out_shape=(jax.ShapeDtypeStruct((3, rows, D_MODEL), BF16),
                   jax.ShapeDtypeStruct((N_HEADS, CHUNK, CHUNK), BF16)),
        compiler_params=_params(),
    )(w_oa, w_ob, w_out, w_s)


def _mesh_pos():
    x, y, c = lax.axis_index("x"), lax.axis_index("y"), lax.axis_index("c")
    chips = [(1 - x, y), (x, 1 - y), (1 - x, 1 - y)]
    return x, y, c, chips


def _slot_of_seg(seg):
    return jnp.where(seg < 3, seg, jnp.where(seg < 7, seg + 5, seg - 3))


def _gather_and_project(xb, w_pre, wo_b, conv_w8, tm):
    t = xb.shape[0]
    d, sc = w_pre.shape[0], w_pre.shape[1] // N_CHIPS
    rows = wo_b.shape[1]
    hd, hr = d // 2, rows // 2
    nj = BLOCKS_PER_SHARD // SUBS
    pc = nj * COL_BLOCK
    units = N_CHIPS * SUBS
    ni = t // tm
    total = units * ni * nj
    mx, my = lax.axis_index("x"), lax.axis_index("y")
    order = jnp.stack([2 * mx + my, 2 * (1 - mx) + my, 2 * mx + (1 - my),
                       2 * (1 - mx) + (1 - my)]).astype(jnp.int32)

    def body(order_ref, x_ref, wpre_ref, wo_ref, cw_ref, p_ref, wf_ref, wof_ref, cwf_ref,
             wbuf, wsem, xbuf, xsem, send_sems, recv_sems, local_sems):
        x, y, c, chips = _mesh_pos()
        me = 2 * x + y
        sibling = (x, y, 1 - c)
        u, i, j = pl.program_id(0), pl.program_id(1), pl.program_id(2)
        n = (u * ni + i) * nj + j
        m = u * ni + i

        def rows_start(m_):
            pltpu.make_async_copy(x_ref.at[pl.ds(pl.multiple_of(lax.rem(m_, ni) * tm, tm), tm), :],
                                  xbuf.at[lax.rem(m_, 2)], xsem.at[lax.rem(m_, 2)]).start()

        def chip_of(q):
            return 2 * chips[q][0] + chips[q][1]

        def piece(ref, k, cc, r):
            return ref.at[pl.ds(cc * hd, hd), pl.ds(pl.multiple_of(k * sc + r * pc, COL_BLOCK), pc)]

        def wo_half(ref4, k, cc):
            return ref4.at[:, k, pl.ds(cc * hr, hr), :]

        def rcopy(sem, src, dst, to):
            return pltpu.make_async_remote_copy(src_ref=src, dst_ref=dst, send_sem=send_sems.at[sem],
                                                recv_sem=recv_sems.at[sem], device_id=to, device_id_type=MESH)

        def w_send(q, r):
            return rcopy(q * SUBS + r, piece(wpre_ref, me, c, r), piece(wf_ref, me, c, r), (*chips[q], c))

        def w_landed(q, r):
            return rcopy(q * SUBS + r, piece(wpre_ref, me, c, r), piece(wf_ref, chip_of(q), c, r), sibling)

        def w_forward(q, r, cc):
            ref = piece(wf_ref, chip_of(q), cc, r)
            return rcopy(9 + q * SUBS + r, ref, ref, sibling)

        def wo_send(q):
            return rcopy(18 + q, wo_ref.at[:, pl.ds(c * hr, hr), :], wo_half(wof_ref, me, c), (*chips[q], c))

        def wo_landed(q):
            return rcopy(18 + q, wo_ref.at[:, pl.ds(c * hr, hr), :], wo_half(wof_ref, chip_of(q), c), sibling)

        def wo_forward(q, cc):
            ref = wo_half(wof_ref, chip_of(q), cc)
            return rcopy(21 + q, ref, ref, sibling)

        def conv_send(q):
            return rcopy(24 + q, cw_ref, cwf_ref.at[me], (*chips[q], c))

        def local_copies():
            return [pltpu.make_async_copy(wo_ref, wof_ref.at[:, me], local_sems.at[0]),
                    pltpu.make_async_copy(cw_ref, cwf_ref.at[me], local_sems.at[1])]

        def tile_start(u_, j_, slot):
            g = order_ref[u_ // SUBS] * BLOCKS_PER_SHARD + lax.rem(u_, SUBS) * nj + j_
            cols = pl.ds(pl.multiple_of(g * COL_BLOCK, COL_BLOCK), COL_BLOCK)

            @pl.when(u_ < SUBS)
            def _():
                pltpu.make_async_copy(wpre_ref.at[:, cols], wbuf.at[slot], wsem.at[slot]).start()

            @pl.when(u_ >= SUBS)
            def _():
                pltpu.make_async_copy(wf_ref.at[:, cols], wbuf.at[slot], wsem.at[slot]).start()

        def end_of(u_):
            return jnp.logical_and(u == u_, jnp.logical_and(i == ni - 1, j == nj - 1))

        @pl.when(n == 0)
        def _():
            for cp in local_copies():
                cp.start()
            for r in range(SUBS):
                for q in (0, 1):
                    w_send(q, r).start()
            for q in range(3):
                conv_send(q).start()
            tile_start(0, 0, 0)
            rows_start(0)

        def pass_on(q, r):
            w_landed(q, r).wait_recv()
            w_forward(q, r, c).start()

        for u_ in range(1, units - 1):
            @pl.when(end_of(u_))
            def _(u_=u_):
                if u_ <= SUBS:
                    for q in (0, 1):
                        pass_on(q, u_ - 1)
                if u_ == SUBS:
                    for r in range(SUBS):
                        for q in (0, 1):
                            w_send(q, r).wait_send()
                    for r in range(SUBS):
                        w_send(2, r).start()
                if 2 * SUBS - 1 <= u_ <= 3 * SUBS - 2:
                    pass_on(2, u_ - (2 * SUBS - 1))
                if u_ == 3 * SUBS - 2:
                    for r in range(SUBS):
                        w_send(2, r).wait_send()
                    for q in range(3):
                        wo_send(q).start()
                nxt = u_ + 1
                if nxt >= SUBS:
                    w_forward(nxt // SUBS - 1, nxt % SUBS, 1 - c).wait_recv()

        n1 = n + 1

        @pl.when(n1 < total)
        def _():
            tile_start(n1 // (ni * nj), lax.rem(n1, nj), lax.rem(n1, 2))

        xslot = lax.rem(m, 2)

        @pl.when(j == 0)
        def _():
            @pl.when(m + 1 < units * ni)
            def _():
                rows_start(m + 1)

            pltpu.make_async_copy(x_ref.at[pl.ds(0, tm), :], xbuf.at[xslot], xsem.at[xslot]).wait()

        slot = lax.rem(n, 2)
        pltpu.make_async_copy(wpre_ref.at[:, pl.ds(0, COL_BLOCK)], wbuf.at[slot], wsem.at[slot]).wait()
        p_ref[...] = _dot(xbuf[xslot], wbuf[slot]).astype(BF16)

        @pl.when(n == total - 1)
        def _():
            for q in range(3):
                wo_landed(q).wait_recv()
                wo_forward(q, c).start()
            for q in range(3):
                wo_forward(q, 1 - c).wait_recv()
                rcopy(24 + q, cw_ref, cwf_ref.at[chip_of(q)], sibling).wait_recv()
            for q in range(3):
                for r in range(SUBS):
                    w_forward(q, r, c).wait_send()
                wo_send(q).wait_send()
                wo_forward(q, c).wait_send()
                conv_send(q).wait_send()
            for cp in local_copies():
                cp.wait()

    def p_map(u, i, j, o):
        g = o[u // SUBS] * BLOCKS_PER_SHARD + lax.rem(u, SUBS) * nj + j
        return (_slot_of_seg(g // BLOCKS_PER_SEG), i, lax.rem(g, BLOCKS_PER_SEG))

    return pl.pallas_call(
        body, name="gather_and_project",
        grid_spec=pltpu.PrefetchScalarGridSpec(
            num_scalar_prefetch=1, grid=(units, ni, nj),
            in_specs=[ANY, ANY, ANY, ANY],
            out_specs=(pl.BlockSpec((None, tm, COL_BLOCK), p_map), ANY, ANY, ANY),
            scratch_shapes=[pltpu.VMEM((2, d, COL_BLOCK), BF16), pltpu.SemaphoreType.DMA((2,)),
                            pltpu.VMEM((2, tm, D_MODEL), BF16), pltpu.SemaphoreType.DMA((2,)),
                            pltpu.SemaphoreType.DMA((27,)), pltpu.SemaphoreType.DMA((27,)),
                            pltpu.SemaphoreType.DMA((2,))]),
        out_shape=(jax.ShapeDtypeStruct((N_SLOTS, t, D_MODEL), BF16),
                   jax.ShapeDtypeStruct((d, N_CHIPS * sc), BF16),
                   jax.ShapeDtypeStruct((3, N_CHIPS, rows, D_MODEL), BF16),
                   jax.ShapeDtypeStruct((N_CHIPS,) + conv_w8.shape, F32)),
        input_output_aliases={2: 1},
        compiler_params=pltpu.CompilerParams(dimension_semantics=("arbitrary",) * 3, vmem_limit_bytes=VMEM_LIMIT,
                                             has_side_effects=True),
    )(order, xb, w_pre, wo_b, conv_w8)


def _exchange_halves(g_o):
    hr = SHARD_ROWS // 2
    g_o4 = [g.reshape(N_CHIPS, 2, hr, D_MODEL) for g in g_o]

    def body(ga_ref, gb_ref, gc_ref, ra_ref, rb_ref, rc_ref, send_sems, recv_sems):
        x, y, c, _ = _mesh_pos()
        sibling = (x, y, 1 - c)
        cps = []
        for n, (g_ref, r_ref) in enumerate(((ga_ref, ra_ref), (gb_ref, rb_ref), (gc_ref, rc_ref))):
            cps.append(pltpu.make_async_remote_copy(src_ref=g_ref.at[:, 1 - c], dst_ref=r_ref,
                                                    send_sem=send_sems.at[n], recv_sem=recv_sems.at[n],
                                                    device_id=sibling, device_id_type=MESH))
        for cp in cps:
            cp.start()
        for cp in cps:
            cp.wait()

    o_shape = jax.ShapeDtypeStruct((N_CHIPS, hr, D_MODEL), BF16)
    return pl.pallas_call(
        body, name="rs_exchange_halves",
        in_specs=[ANY] * 3, out_specs=(ANY,) * 3,
        out_shape=(o_shape, o_shape, o_shape),
        scratch_shapes=[pltpu.SemaphoreType.DMA((3,)), pltpu.SemaphoreType.DMA((3,))],
        compiler_params=pltpu.CompilerParams(has_side_effects=True),
    )(*g_o4)


def _add_halves(g_in, g_o, r_in, r_o):
    d, c9 = g_in.shape
    hd = d // 2
    hr = SHARD_ROWS // 2
    core = lax.axis_index("c").astype(jnp.int32).reshape(1)
    tm = min(512, hd)
    nb = hd // tm

    def body_in(c_ref, g_ref, r_ref, o_ref):
        o_ref[...] = (g_ref[...].astype(F32) + r_ref[...].astype(F32)).astype(BF16)

    s_in = pl.pallas_call(
        body_in, name="rs_add_halves_in",
        grid_spec=pltpu.PrefetchScalarGridSpec(
            num_scalar_prefetch=1, grid=(N_CHIPS, nb),
            in_specs=[pl.BlockSpec((tm, SHARD_COLS), lambda k, i, c_ref: (c_ref[0] * nb + i, k)),
                      pl.BlockSpec((tm, SHARD_COLS), lambda k, i, c_ref: (i, k))],
            out_specs=pl.BlockSpec((None, tm, SHARD_COLS), lambda k, i, c_ref: (k, i, 0))),
        out_shape=jax.ShapeDtypeStruct((N_CHIPS, hd, SHARD_COLS), BF16),
        compiler_params=_params("parallel", "parallel"),
    )(core, g_in, r_in)

    g_o4 = [g.reshape(N_CHIPS, 2, hr, D_MODEL) for g in g_o]

    def body_o(c_ref, ga_ref, gb_ref, gc_ref, ra_ref, rb_ref, rc_ref, o_ref):
        for n, (g_ref, r_ref) in enumerate(((ga_ref, ra_ref), (gb_ref, rb_ref), (gc_ref, rc_ref))):
            o_ref[n] = (g_ref[...].astype(F32) + r_ref[...].astype(F32)).astype(BF16)

    gspec = pl.BlockSpec((None, None, hr, D_MODEL), lambda k, c_ref: (k, c_ref[0], 0, 0))
    rspec = pl.BlockSpec((None, hr, D_MODEL), lambda k, c_ref: (k, 0, 0))
    s_o = pl.pallas_call(
        body_o, name="rs_add_halves_o",
        grid_spec=pltpu.PrefetchScalarGridSpec(
            num_scalar_prefetch=1, grid=(N_CHIPS,),
            in_specs=[gspec] * 3 + [rspec] * 3,
            out_specs=pl.BlockSpec((3, None, hr, D_MODEL), lambda k, c_ref: (0, k, 0, 0))),
        out_shape=jax.ShapeDtypeStruct((3, N_CHIPS, hr, D_MODEL), BF16),
        compiler_params=_params("parallel"),
    )(core, *g_o4, *r_o)
    return s_in, s_o


def _sum_chips(r_in, r_o, s_in, s_o):
    _, hd, sc = r_in.shape
    hr = r_o.shape[2]
    tm = min(256, hd)
    nb = hd // tm
    pos = jnp.stack([2 * lax.axis_index("x") + lax.axis_index("y"), lax.axis_index("c")]).astype(jnp.int32)

    def chip_sum(pos_ref, r_ref, s_ref):
        acc = None
        for k in range(N_CHIPS):
            term = jnp.where(pos_ref[0] == k, s_ref[...], r_ref[k]).astype(F32)
            acc = term if acc is None else acc + term
        return acc

    def body_in(pos_ref, r_ref, s_ref, o_ref):
        o_ref[...] = chip_sum(pos_ref, r_ref, s_ref)

    f_in = pl.pallas_call(
        body_in, name="rs_sum_chips_in",
        grid_spec=pltpu.PrefetchScalarGridSpec(
            num_scalar_prefetch=1, grid=(nb,),
            in_specs=[pl.BlockSpec((N_CHIPS, tm, sc), lambda i, p: (0, i, 0)),
                      pl.BlockSpec((None, tm, sc), lambda i, p: (p[0], i, 0))],
            out_specs=pl.BlockSpec((tm, sc), lambda i, p: (p[1] * nb + i, 0))),
        out_shape=jax.ShapeDtypeStruct((2 * hd, sc), F32),
        compiler_params=_params("parallel"),
    )(pos, r_in, s_in)

    def body_o(pos_ref, r_ref, s_ref, o_ref):
        o_ref[...] = chip_sum(pos_ref, r_ref, s_ref)

    f_o = pl.pallas_call(
        body_o, name="rs_sum_chips_o",
        grid_spec=pltpu.PrefetchScalarGridSpec(
            num_scalar_prefetch=1, grid=(3,),
            in_specs=[pl.BlockSpec((N_CHIPS, None, hr, D_MODEL), lambda n, p: (0, n, 0, 0)),
                      pl.BlockSpec((None, None, hr, D_MODEL), lambda n, p: (n, p[0], 0, 0))],
            out_specs=pl.BlockSpec((None, hr, D_MODEL), lambda n, p: (n, p[1], 0))),
        out_shape=jax.ShapeDtypeStruct((3, 2 * hr, D_MODEL), F32),
        compiler_params=_params("parallel"),
    )(pos, r_o, s_o)
    return f_in, f_o


def _share_halves(f_in, f_o):
    hd, sc = f_in.shape[0] // 2, f_in.shape[1]
    hr = f_o.shape[1] // 2

    def body(fi_ref, fo_ref, gi_ref, go_ref, send_sems, recv_sems):
        x, y, c, _ = _mesh_pos()
        sibling = (x, y, 1 - c)

        def halves(cc):
            rows_i, rows_o = pl.ds(cc * hd, hd), pl.ds(cc * hr, hr)
            return (fi_ref.at[rows_i, :], gi_ref.at[rows_i, :]), (fo_ref.at[:, rows_o, :], go_ref.at[:, rows_o, :])

        def copies(cc):
            return [pltpu.make_async_remote_copy(src_ref=src, dst_ref=dst, send_sem=send_sems.at[n],
                                                 recv_sem=recv_sems.at[n], device_id=sibling, device_id_type=MESH)
                    for n, (src, dst) in enumerate(halves(cc))]

        sends = copies(c)
        for cp in sends:
            cp.start()
        for cp in copies(1 - c):
            cp.wait_recv()
        for cp in sends:
            cp.wait_send()

    return pl.pallas_call(
        body, name="rs_share_halves",
        in_specs=[ANY, ANY], out_specs=(ANY, ANY),
        out_shape=(jax.ShapeDtypeStruct(f_in.shape, F32), jax.ShapeDtypeStruct(f_o.shape, F32)),
        scratch_shapes=[pltpu.SemaphoreType.DMA((2,)), pltpu.SemaphoreType.DMA((2,))],
        input_output_aliases={0: 0, 1: 1},
        compiler_params=pltpu.CompilerParams(has_side_effects=True),
    )(f_in, f_o)


def _merge_forward(ya, yb, wo3, p, bg, tm, tn):
    t = ya.shape[0]

    def body(ya_ref, yb_ref, wa_ref, wb_ref, g_ref, bg_ref, m_ref, oab_ref):
        oa = _dot(ya_ref[...], wa_ref[...])
        ob = _dot(yb_ref[...], wb_ref[...])
        ga = _sigmoid(g_ref[0].astype(F32) + bg_ref[0])
        gb = _sigmoid(g_ref[1].astype(F32) + bg_ref[1])
        m_ref[...] = (ga * oa + gb * ob).astype(BF16)
        oab_ref[0] = oa.astype(BF16)
        oab_ref[1] = ob.astype(BF16)

    return pl.pallas_call(
        body, name="merge_forward", grid=(t // tm, D_MODEL // tn),
        in_specs=[pl.BlockSpec((tm, D_MODEL), lambda i, j: (i, 0)),
                  pl.BlockSpec((tm, D_MODEL), lambda i, j: (i, 0)),
                  pl.BlockSpec((None, D_MODEL, tn), lambda i, j: (0, 0, j)),
                  pl.BlockSpec((None, D_MODEL, tn), lambda i, j: (1, 0, j)),
                  pl.BlockSpec((2, tm, tn), lambda i, j: (BLOCK_G, i, j)),
                  pl.BlockSpec((2, 1, tn), lambda i, j: (0, 0, j))],
        out_specs=(pl.BlockSpec((tm, tn), lambda i, j: (i, j)),
                   pl.BlockSpec((2, tm, tn), lambda i, j: (0, i, j))),
        out_shape=(jax.ShapeDtypeStruct((t, D_MODEL), BF16), jax.ShapeDtypeStruct((2, t, D_MODEL), BF16)),
        compiler_params=_params("parallel", "parallel"),
    )(ya, yb, wo3, wo3, p, bg)


HEAD_ROWS = 256


def _head(merged, wo3, x, target, ln_g, ln_b, tm):
    t = x.shape[0]
    inv_d = 1.0 / D_MODEL

    def body(m_ref, w_ref, x_ref, t_ref, g_ref, b_ref, dr_ref, gx_ref, dm_ref, dg_ref, db_ref, loss_ref):
        i = pl.program_id(0)

        @pl.when(i == 0)
        def _():
            dg_ref[...] = jnp.zeros_like(dg_ref)
            db_ref[...] = jnp.zeros_like(db_ref)
            loss_ref[...] = jnp.zeros_like(loss_ref)

        w = w_ref[...]
        g = g_ref[...]
        tiles = [slice(r0, r0 + HEAD_ROWS) for r0 in range(0, tm, HEAD_ROWS)]
        firsts = [_dot(m_ref[rows, :], w) for rows in tiles]
        for rows, out in zip(tiles, firsts):
            r = DN_ALPHA * x_ref[rows, :] + out
            mu = jnp.mean(r, axis=-1, keepdims=True)
            xc = r - mu
            var = jnp.mean(xc * xc, axis=-1, keepdims=True)
            rstd = lax.rsqrt(var + LN_EPS)
            xhat = xc * rstd
            e = xhat * g + b_ref[...] - t_ref[rows, :]
            se = jnp.sum(jnp.sum(e * e, axis=1, keepdims=True), axis=0, keepdims=True)
            loss_ref[...] += jnp.broadcast_to((0.5 * inv_d) * se, loss_ref.shape)
            dy = e * inv_d
            db_ref[...] += jnp.sum(dy, axis=0, keepdims=True)
            dg_ref[...] += jnp.sum(dy * xhat, axis=0, keepdims=True)
            dxh = dy * g
            m1 = jnp.mean(dxh, axis=-1, keepdims=True)
            m2 = jnp.mean(dxh * xhat, axis=-1, keepdims=True)
            dr = rstd * (dxh - m1 - xhat * m2)
            gx_ref[rows, :] = DN_ALPHA * dr
            drb = dr.astype(BF16)
            dr_ref[rows, :] = drb
            dm_ref[rows, :] = _dot(drb, w, 1, 1).astype(BF16)

    row = pl.BlockSpec((tm, D_MODEL), lambda i: (i, 0))
    vec = pl.BlockSpec((1, D_MODEL), lambda i: (0, 0))
    return pl.pallas_call(
        body, name="head", grid=(t // tm,),
        in_specs=[row, pl.BlockSpec((None, D_MODEL, D_MODEL), lambda i: (2, 0, 0), pipeline_mode=pl.Buffered(1)),
                  row, row, vec, vec],
        out_specs=(row, row, row, vec, vec, pl.BlockSpec((1, 128), lambda i: (0, 0))),
        out_shape=(jax.ShapeDtypeStruct((t, D_MODEL), BF16), jax.ShapeDtypeStruct((t, D_MODEL), F32),
                   jax.ShapeDtypeStruct((t, D_MODEL), BF16), jax.ShapeDtypeStruct((1, D_MODEL), F32),
                   jax.ShapeDtypeStruct((1, D_MODEL), F32), jax.ShapeDtypeStruct((1, 128), F32)),
        compiler_params=_params("arbitrary"),
    )(merged, wo3, x, target, ln_g, ln_b)


def _gate_and_branch_backward(dmerged, oab, p, bg, wo3, tm):
    t = dmerged.shape[0]
    nt = t // tm

    def body(dm_ref, oab_ref, g_ref, bg_ref, wa_ref, wb_ref, do_ref, dpg_ref, dbg_ref, dy_ref, stash_a, stash_b):
        i = pl.program_id(0)

        @pl.when(i == 0)
        def _():
            dbg_ref[...] = jnp.zeros_like(dbg_ref)
            stash_b[...] = jnp.zeros_like(stash_b)

        live = (i < nt).astype(F32)

        def step(read, write):
            dy_ref[0] = _dot(read[0], wa_ref[...], 1, 1).astype(BF16)
            dy_ref[1] = _dot(read[1], wb_ref[...], 1, 1).astype(BF16)
            dm = dm_ref[...].astype(F32)
            for n in range(2):
                gate = _sigmoid(g_ref[n].astype(F32) + bg_ref[n])
                d_o = (dm * gate).astype(BF16)
                do_ref[n] = d_o
                write[n] = d_o
                dgate = dm * oab_ref[n].astype(F32) * gate * (1.0 - gate)
                dpg_ref[n] = dgate.astype(BF16)
                dbg_ref[n] += live * jnp.sum(dgate, axis=0, keepdims=True)

        even = lax.rem(i, 2) == 0

        @pl.when(even)
        def _():
            step(stash_b, stash_a)

        @pl.when(jnp.logical_not(even))
        def _():
            step(stash_a, stash_b)

    def cur(i):
        return jnp.minimum(i, nt - 1)

    pair = pl.BlockSpec((2, tm, D_MODEL), lambda i: (0, cur(i), 0))
    gates = pl.BlockSpec((2, tm, D_MODEL), lambda i: (BLOCK_G, cur(i), 0))
    vec = pl.BlockSpec((2, 1, D_MODEL), lambda i: (0, 0, 0))

    def weight(n):
        return pl.BlockSpec((None, D_MODEL, D_MODEL), lambda i: (n, 0, 0), pipeline_mode=pl.Buffered(1))

    pair_shape = jax.ShapeDtypeStruct((2, t, D_MODEL), BF16)
    return pl.pallas_call(
        body, name="gate_and_branch_backward", grid=(nt + 1,),
        in_specs=[pl.BlockSpec((tm, D_MODEL), lambda i: (cur(i), 0)), pair, gates, vec, weight(0), weight(1)],
        out_specs=(pair, gates, vec, pl.BlockSpec((2, tm, D_MODEL), lambda i: (0, jnp.maximum(i - 1, 0), 0))),
        out_shape=(pair_shape, jax.ShapeDtypeStruct((N_SLOTS, t, D_MODEL), BF16),
                   jax.ShapeDtypeStruct((2, 1, D_MODEL), F32), pair_shape),
        scratch_shapes=[pltpu.VMEM((2, tm, D_MODEL), BF16), pltpu.VMEM((2, tm, D_MODEL), BF16)],
        compiler_params=_params("arbitrary"),
    )(dmerged, oab, p, bg, wo3, wo3)


def _weight_grad(a, a_sel, b, b_sel, tm, tn, tk, name, exchange=None):
    t = a.shape[-2]
    nk = t // tk
    ni, nj = D_MODEL // tm, D_MODEL // tn

    def body(a_ref, b_ref, *rest):
        if exchange is None:
            o_ref, acc_ref = rest
        else:
            g_ref, o_ref, r_ref, acc_ref, send_sem, recv_sem = rest
        i, j, k = pl.program_id(0), pl.program_id(1), pl.program_id(2)

        if exchange is not None:
            x, y, c, _ = _mesh_pos()
            hd = exchange.shape[0] // 2
            swap = pltpu.make_async_remote_copy(src_ref=g_ref.at[pl.ds((1 - c) * hd, hd), :], dst_ref=r_ref,
                                                send_sem=send_sem, recv_sem=recv_sem,
                                                device_id=(x, y, 1 - c), device_id_type=MESH)

            @pl.when(jnp.logical_and(i == 0, jnp.logical_and(j == 0, k == 0)))
            def _():
                swap.start()

        @pl.when(k == 0)
        def _():
            acc_ref[...] = jnp.zeros_like(acc_ref)

        acc_ref[...] += _dot(a_ref[...], b_ref[...], 0, 0)

        @pl.when(k == nk - 1)
        def _():
            o_ref[...] = acc_ref[...].astype(BF16)

        if exchange is not None:
            @pl.when(jnp.logical_and(i == ni - 1, jnp.logical_and(j == nj - 1, k == nk - 1)))
            def _():
                swap.wait()

    def spec(arr, sel, width, which):
        if arr.ndim == 2:
            return pl.BlockSpec((tk, width), lambda i, j, k: (k, (i, j)[which]))
        return pl.BlockSpec((None, tk, width), lambda i, j, k: (sel, k, (i, j)[which]))

    o_spec = pl.BlockSpec((tm, tn), lambda i, j, k: (i, j))
    o_shape = jax.ShapeDtypeStruct((D_MODEL, D_MODEL), BF16)
    if exchange is None:
        return pl.pallas_call(
            body, name=name, grid=(ni, nj, nk),
            in_specs=[spec(a, a_sel, tm, 0), spec(b, b_sel, tn, 1)],
            out_specs=o_spec, out_shape=o_shape,
            scratch_shapes=[pltpu.VMEM((tm, tn), F32)],
            compiler_params=_params("parallel", "parallel", "arbitrary"),
        )(a, b)
    return pl.pallas_call(
        body, name=name, grid=(ni, nj, nk),
        in_specs=[spec(a, a_sel, tm, 0), spec(b, b_sel, tn, 1), ANY],
        out_specs=(o_spec, ANY),
        out_shape=(o_shape, jax.ShapeDtypeStruct((exchange.shape[0] // 2, exchange.shape[1]), BF16)),
        scratch_shapes=[pltpu.VMEM((tm, tn), F32), pltpu.SemaphoreType.DMA, pltpu.SemaphoreType.DMA],
        compiler_params=pltpu.CompilerParams(dimension_semantics=("arbitrary",) * 3, vmem_limit_bytes=VMEM_LIMIT,
                                             has_side_effects=True),
    )(a, b, exchange)


def _win_grad(xt, dp, tn, tk):
    _, t, _ = dp.shape
    nk = t // tk
    per_seg = D_MODEL // tn
    nj = N_SEG * per_seg

    def body(x_ref, dp_ref, o_ref, acc_ref):
        k = pl.program_id(1)

        @pl.when(k == 0)
        def _():
            acc_ref[...] = jnp.zeros_like(acc_ref)

        acc_ref[...] += _dot(x_ref[...], dp_ref[...])

        @pl.when(k == nk - 1)
        def _():
            o_ref[...] = acc_ref[...].astype(BF16)

    return pl.pallas_call(
        body, name="grad_w_in", grid=(nj, nk),
        in_specs=[pl.BlockSpec((D_MODEL, tk), lambda j, k: (0, k)),
                  pl.BlockSpec((None, tk, tn), lambda j, k: (_slot_of_seg(j // per_seg), k, j % per_seg))],
        out_specs=pl.BlockSpec((D_MODEL, tn), lambda j, k: (0, j)),
        out_shape=jax.ShapeDtypeStruct((D_MODEL, N_SEG * D_MODEL), BF16),
        scratch_shapes=[pltpu.VMEM((D_MODEL, tn), F32)],
        compiler_params=_params("parallel", "arbitrary"),
    )(xt, dp)


def _input_grad_and_scatter(dp, w_full, gx, s_in, s_o, small, tm):
    _, t, _ = dp.shape
    ni, nk = t // tm, N_SEG - 1
    _, hd, sc = s_in.shape
    hr = s_o.shape[2]

    def body(dp_ref, w_ref, gx_ref, si_ref, so_ref, sm_ref, o_ref, ri_ref, ro_ref, ga_ref,
             send_sems, recv_sems, local_sem):
        i, k = pl.program_id(0), pl.program_id(1)
        x, y, c, chips = _mesh_pos()
        me = 2 * x + y
        dev = 4 * x + 2 * y + c

        def peer(r):
            return (x ^ ((r >> 2) & 1), y ^ ((r >> 1) & 1), c ^ (r & 1))

        def sends():
            cps = []
            for q, (cx, cy) in enumerate(chips):
                dest = 2 * cx + cy
                cps.append(pltpu.make_async_remote_copy(src_ref=si_ref.at[dest], dst_ref=ri_ref.at[me],
                                                        send_sem=send_sems.at[q], recv_sem=recv_sems.at[q],
                                                        device_id=(cx, cy, c), device_id_type=MESH))
                cps.append(pltpu.make_async_remote_copy(src_ref=so_ref.at[:, dest], dst_ref=ro_ref.at[me],
                                                        send_sem=send_sems.at[3 + q], recv_sem=recv_sems.at[3 + q],
                                                        device_id=(cx, cy, c), device_id_type=MESH))
            for r in range(1, 8):
                cps.append(pltpu.make_async_remote_copy(src_ref=sm_ref, dst_ref=ga_ref.at[dev],
                                                        send_sem=send_sems.at[5 + r], recv_sem=recv_sems.at[5 + r],
                                                        device_id=peer(r), device_id_type=MESH))
            return cps

        own_small = pltpu.make_async_copy(sm_ref, ga_ref.at[dev], local_sem)

        @pl.when(jnp.logical_and(i == 0, k == 0))
        def _():
            for cp in sends():
                cp.start()
            own_small.start()

        @pl.when(k == 0)
        def _():
            o_ref[...] = gx_ref[...]

        o_ref[...] += _dot(dp_ref[...], w_ref[...], 1, 1)

        @pl.when(jnp.logical_and(i == ni - 1, k == nk - 1))
        def _():
            for q, (cx, cy) in enumerate(chips):
                frm = 2 * cx + cy
                pltpu.make_async_remote_copy(src_ref=si_ref.at[frm], dst_ref=ri_ref.at[frm], send_sem=send_sems.at[q],
                                             recv_sem=recv_sems.at[q], device_id=(x, y, c),
                                             device_id_type=MESH).wait_recv()
                pltpu.make_async_remote_copy(src_ref=so_ref.at[:, frm], dst_ref=ro_ref.at[frm],
                                             send_sem=send_sems.at[3 + q], recv_sem=recv_sems.at[3 + q],
                                             device_id=(x, y, c), device_id_type=MESH).wait_recv()
            for r in range(1, 8):
                px, py, pc = peer(r)
                pltpu.make_async_remote_copy(src_ref=sm_ref, dst_ref=ga_ref.at[4 * px + 2 * py + pc],
                                             send_sem=send_sems.at[5 + r], recv_sem=recv_sems.at[5 + r],
                                             device_id=(x, y, c), device_id_type=MESH).wait_recv()
            for cp in sends():
                cp.wait_send()
            own_small.wait()

    return pl.pallas_call(
        body, name="grad_x_and_scatter", grid=(ni, nk),
        in_specs=[pl.BlockSpec((None, tm, D_MODEL), lambda i, k: (_slot_of_seg(k), i, 0)),
                  pl.BlockSpec((D_MODEL, D_MODEL), lambda i, k: (0, k)),
                  pl.BlockSpec((tm, D_MODEL), lambda i, k: (i, 0)), ANY, ANY, ANY],
        out_specs=(pl.BlockSpec((tm, D_MODEL), lambda i, k: (i, 0)), ANY, ANY, ANY),
        out_shape=(jax.ShapeDtypeStruct((t, D_MODEL), F32),
                   jax.ShapeDtypeStruct((N_CHIPS, hd, sc), BF16),
                   jax.ShapeDtypeStruct((N_CHIPS, 3, hr, D_MODEL), BF16),
                   jax.ShapeDtypeStruct((8,) + small.shape, small.dtype)),
        scratch_shapes=[pltpu.SemaphoreType.DMA((13,)), pltpu.SemaphoreType.DMA((13,)), pltpu.SemaphoreType.DMA],
        input_output_aliases={2: 0},
        compiler_params=pltpu.CompilerParams(dimension_semantics=("arbitrary", "arbitrary"),
                                             vmem_limit_bytes=VMEM_LIMIT, has_side_effects=True),
    )(dp, w_full, gx, s_in, s_o, small)


def _input_grad_tail(dp, w_full, partial, tm):
    _, t, _ = dp.shape
    seg = N_SEG - 1

    def body(dp_ref, w_ref, part_ref, o_ref):
        o_ref[...] = part_ref[...] + _dot(dp_ref[...], w_ref[...], 1, 1)

    row = pl.BlockSpec((tm, D_MODEL), lambda i: (i, 0))
    return pl.pallas_call(
        body, name="grad_x_tail", grid=(t // tm,),
        in_specs=[pl.BlockSpec((None, tm, D_MODEL), lambda i: (SLOT_OF_SEG[seg], i, 0)),
                  pl.BlockSpec((D_MODEL, D_MODEL), lambda i: (0, seg)), row],
        out_specs=row,
        out_shape=jax.ShapeDtypeStruct((t, D_MODEL), F32),
        compiler_params=_params("parallel"),
    )(dp, w_full, partial)


def _sgu_chunk_forward(u, v, z, wm, bs, lng, lnb):
    ug, dug = _gelu_and_grad(u)
    vg, dvg = _gelu_and_grad(v)
    mu = jnp.mean(vg, axis=-1, keepdims=True)
    xc = vg - mu
    var = jnp.mean(xc * xc, axis=-1, keepdims=True)
    rstd = lax.rsqrt(var + LN_EPS)
    vhat = xc * rstd
    vln = (vhat * lng + lnb).astype(BF16)
    mixed = _dot(wm, vln) + bs
    sig = _sigmoid(z)
    return ug, dug, dvg, rstd, vhat, vln, mixed, sig


def _mixer_a_forward(p_a, wm, bs_col, ln_v_g, ln_v_b, tm):
    t = p_a.shape[1]

    def body(p_ref, wm_ref, bs_ref, g_ref, b_ref, o_ref):
        wm_v, bs_v, lng, lnb = wm_ref[...], bs_ref[...], g_ref[...], b_ref[...]

        def chunk(ci, carry):
            rows = pl.ds(pl.multiple_of(ci * CHUNK, CHUNK), CHUNK)
            u = p_ref[0, rows, :].astype(F32)
            v = p_ref[1, rows, :].astype(F32)
            z = p_ref[2, rows, :].astype(F32)
            ug, _, _, _, _, _, mixed, sig = _sgu_chunk_forward(u, v, z, wm_v, bs_v, lng, lnb)
            o_ref[rows, :] = (ug * mixed * (z * sig)).astype(BF16)
            return carry

        lax.fori_loop(0, tm // CHUNK, chunk, 0, unroll=True)

    return pl.pallas_call(
        body, name="mixer_a_forward", grid=(t // tm, N_HEADS),
        in_specs=[pl.BlockSpec((3, tm, HEAD_DIM), lambda i, h: (0, i, h)),
                  pl.BlockSpec((None, CHUNK, CHUNK), lambda i, h: (h, 0, 0)),
                  pl.BlockSpec((None, CHUNK, 1), lambda i, h: (h, 0, 0)),
                  pl.BlockSpec((1, HEAD_DIM), lambda i, h: (0, h)),
                  pl.BlockSpec((1, HEAD_DIM), lambda i, h: (0, h))],
        out_specs=pl.BlockSpec((tm, HEAD_DIM), lambda i, h: (i, h)),
        out_shape=jax.ShapeDtypeStruct((t, D_MODEL), BF16),
        compiler_params=_params("parallel", "parallel"),
    )(p_a, wm, bs_col, ln_v_g, ln_v_b)


def _mixer_a_backward(p_a, dyab, wm, bs_col, ln_v_g, ln_v_b, dp, tm):
    t = p_a.shape[1]

    def body(p_ref, dy_ref, wm_ref, bs_ref, g_ref, b_ref, dp_in, dp_ref, dws_ref, dbs_ref, dg_ref, db_ref):
        @pl.when(pl.program_id(1) == 0)
        def _():
            dws_ref[...] = jnp.zeros_like(dws_ref)
            dbs_ref[...] = jnp.zeros_like(dbs_ref)
            dg_ref[...] = jnp.zeros_like(dg_ref)
            db_ref[...] = jnp.zeros_like(db_ref)

        wm_v, bs_v, lng, lnb = wm_ref[...], bs_ref[...], g_ref[...], b_ref[...]
        causal = (lax.broadcasted_iota(jnp.int32, (CHUNK, CHUNK), 1)
                  <= lax.broadcasted_iota(jnp.int32, (CHUNK, CHUNK), 0))

        def chunk(ci, carry):
            rows = pl.ds(pl.multiple_of(ci * CHUNK, CHUNK), CHUNK)
            u = p_ref[0, rows, :].astype(F32)
            v = p_ref[1, rows, :].astype(F32)
            z = p_ref[2, rows, :].astype(F32)
            dy = dy_ref[rows, :].astype(F32)
            ug, dug, dvg, rstd, vhat, vln, mixed, sig = _sgu_chunk_forward(u, v, z, wm_v, bs_v, lng, lnb)
            sz = z * sig
            dmixed = dy * ug * sz
            dp_ref[0, rows, :] = (dy * mixed * sz * dug).astype(BF16)
            dp_ref[2, rows, :] = (dy * ug * mixed * (sig * (1.0 + z * (1.0 - sig)))).astype(BF16)
            dbs_ref[...] += jnp.sum(dmixed, axis=1, keepdims=True)
            dmb = dmixed.astype(BF16)
            dws_ref[...] += jnp.where(causal, _dot(dmb, vln, 1, 1), 0.0)
            dvln = _dot(wm_v, dmb, 0, 0)
            db_ref[...] += jnp.sum(dvln, axis=0, keepdims=True)
            dg_ref[...] += jnp.sum(dvln * vhat, axis=0, keepdims=True)
            dvh = dvln * lng
            m1 = jnp.mean(dvh, axis=-1, keepdims=True)
            m2 = jnp.mean(dvh * vhat, axis=-1, keepdims=True)
            dp_ref[1, rows, :] = (rstd * (dvh - m1 - vhat * m2) * dvg).astype(BF16)
            return carry

        lax.fori_loop(0, tm // CHUNK, chunk, 0, unroll=True)

    return pl.pallas_call(
        body, name="mixer_a_backward", grid=(N_HEADS, t // tm),
        in_specs=[pl.BlockSpec((3, tm, HEAD_DIM), lambda h, i: (0, i, h)),
                  pl.BlockSpec((None, tm, HEAD_DIM), lambda h, i: (0, i, h)),
                  pl.BlockSpec((None, CHUNK, CHUNK), lambda h, i: (h, 0, 0)),
                  pl.BlockSpec((None, CHUNK, 1), lambda h, i: (h, 0, 0)),
                  pl.BlockSpec((1, HEAD_DIM), lambda h, i: (0, h)),
                  pl.BlockSpec((1, HEAD_DIM), lambda h, i: (0, h)), ANY],
        out_specs=(pl.BlockSpec((3, tm, HEAD_DIM), lambda h, i: (BLOCK_A, i, h)),
                   pl.BlockSpec((None, CHUNK, CHUNK), lambda h, i: (h, 0, 0)),
                   pl.BlockSpec((None, CHUNK, 1), lambda h, i: (h, 0, 0)),
                   pl.BlockSpec((1, HEAD_DIM), lambda h, i: (0, h)),
                   pl.BlockSpec((1, HEAD_DIM), lambda h, i: (0, h))),
        out_shape=(jax.ShapeDtypeStruct(dp.shape, BF16),
                   jax.ShapeDtypeStruct((N_HEADS, CHUNK, CHUNK), F32),
                   jax.ShapeDtypeStruct((N_HEADS, CHUNK, 1), F32),
                   jax.ShapeDtypeStruct((1, D_MODEL), F32), jax.ShapeDtypeStruct((1, D_MODEL), F32)),
        input_output_aliases={6: 0},
        compiler_params=_params("parallel", "arbitrary"),
    )(p_a, dyab, wm, bs_col, ln_v_g, ln_v_b, dp)


HALO = 16


def _conv_taps(h, halo_h, tm):
    row = lax.broadcasted_iota(jnp.int32, h.shape, 0)
    last1 = halo_h[HALO - 1:HALO, :]
    last2 = halo_h[HALO - 2:HALO - 1, :]
    h1 = jnp.where(row == 0, last1, pltpu.roll(h, 1, 0))
    h2 = jnp.where(row == 0, last2, jnp.where(row == 1, last1, pltpu.roll(h, 2, 0)))
    return h1, h2


def _mixer_b_forward(p_b, conv_w, conv_b, tm, tc):
    t = p_b.shape[1]

    def body(p_ref, halo_ref, w_ref, b_ref, o_ref):
        valid = (pl.program_id(1) > 0).astype(F32)
        h = p_ref[1].astype(F32) * p_ref[0].astype(F32)
        halo_h = halo_ref[1].astype(F32) * halo_ref[0].astype(F32) * valid
        h1, h2 = _conv_taps(h, halo_h, tm)
        w = w_ref[...]
        conv = b_ref[...] + w[0:1, :] * h2 + w[1:2, :] * h1 + w[2:3, :] * h
        z = p_ref[3].astype(F32)
        o_ref[...] = (p_ref[2].astype(F32) * conv * (z * _sigmoid(z))).astype(BF16)

    steps = tm // HALO
    return pl.pallas_call(
        body, name="mixer_b_forward", grid=(D_MODEL // tc, t // tm),
        in_specs=[pl.BlockSpec((4, tm, tc), lambda j, i: (BLOCK_B, i, j)),
                  pl.BlockSpec((4, HALO, tc), lambda j, i: (BLOCK_B, jnp.maximum(i * steps - 1, 0), j)),
                  pl.BlockSpec((3, tc), lambda j, i: (0, j)),
                  pl.BlockSpec((1, tc), lambda j, i: (0, j))],
        out_specs=pl.BlockSpec((tm, tc), lambda j, i: (i, j)),
        out_shape=jax.ShapeDtypeStruct((t, D_MODEL), BF16),
        compiler_params=_params("parallel", "parallel"),
    )(p_b, p_b, conv_w, conv_b)


def _mixer_b_backward(p_b, dyab, conv_w, conv_b, dp, tm, tc):
    t = p_b.shape[1]
    n = t // tm

    def body(p_ref, halo_ref, dy_ref, w_ref, b_ref, dp_in, dp_ref, dw_ref, db_ref, next_ref):
        ii = pl.program_id(1)

        @pl.when(ii == 0)
        def _():
            dw_ref[...] = jnp.zeros_like(dw_ref)
            db_ref[...] = jnp.zeros_like(db_ref)
            next_ref[...] = jnp.zeros_like(next_ref)

        valid = (ii < n - 1).astype(F32)
        xb = p_ref[0].astype(F32)
        cb = p_ref[1].astype(F32)
        bb = p_ref[2].astype(F32)
        z = p_ref[3].astype(F32)
        h = cb * xb
        halo_h = halo_ref[1].astype(F32) * halo_ref[0].astype(F32) * valid
        h1, h2 = _conv_taps(h, halo_h, tm)
        w = w_ref[...]
        w0, w1, w2 = w[0:1, :], w[1:2, :], w[2:3, :]
        conv = b_ref[...] + w0 * h2 + w1 * h1 + w2 * h
        sig = _sigmoid(z)
        sz = z * sig
        dy = dy_ref[...].astype(F32)
        dconv = dy * bb * sz
        dp_ref[2] = (dy * conv * sz).astype(BF16)
        dp_ref[3] = (dy * bb * conv * (sig * (1.0 + z * (1.0 - sig)))).astype(BF16)
        db_ref[...] += jnp.sum(dconv, axis=0, keepdims=True)
        dw_ref[0:1, :] += jnp.sum(dconv * h2, axis=0, keepdims=True)
        dw_ref[1:2, :] += jnp.sum(dconv * h1, axis=0, keepdims=True)
        dw_ref[2:3, :] += jnp.sum(dconv * h, axis=0, keepdims=True)
        row = lax.broadcasted_iota(jnp.int32, h.shape, 0)
        nxt = next_ref[...]
        n0, n1 = nxt[0:1, :], nxt[1:2, :]
        d1 = jnp.where(row == tm - 1, n0, pltpu.roll(dconv, tm - 1, 0))
        d2 = jnp.where(row == tm - 1, n1, jnp.where(row == tm - 2, n0, pltpu.roll(dconv, tm - 2, 0)))
        dh = w2 * dconv + w1 * d1 + w0 * d2
        dp_ref[0] = (dh * cb).astype(BF16)
        dp_ref[1] = (dh * xb).astype(BF16)
        next_ref[...] = dconv[0:8, :]

    steps = tm // HALO
    return pl.pallas_call(
        body, name="mixer_b_backward", grid=(D_MODEL // tc, n),
        in_specs=[pl.BlockSpec((4, tm, tc), lambda j, ii: (BLOCK_B, n - 1 - ii, j)),
                  pl.BlockSpec((4, HALO, tc), lambda j, ii: (BLOCK_B, jnp.maximum((n - 1 - ii) * steps - 1, 0), j)),
                  pl.BlockSpec((None, tm, tc), lambda j, ii: (1, n - 1 - ii, j)),
                  pl.BlockSpec((3, tc), lambda j, ii: (0, j)),
                  pl.BlockSpec((1, tc), lambda j, ii: (0, j)), ANY],
        out_specs=(pl.BlockSpec((4, tm, tc), lambda j, ii: (BLOCK_B, n - 1 - ii, j)),
                   pl.BlockSpec((3, tc), lambda j, ii: (0, j)),
                   pl.BlockSpec((1, tc), lambda j, ii: (0, j))),
        out_shape=(jax.ShapeDtypeStruct(dp.shape, BF16),
                   jax.ShapeDtypeStruct((3, D_MODEL), F32), jax.ShapeDtypeStruct((1, D_MODEL), F32)),
        scratch_shapes=[pltpu.VMEM((8, tc), F32)],
        input_output_aliases={5: 0},
        compiler_params=_params("parallel", "arbitrary"),
    )(p_b, p_b, dyab, conv_w, conv_b, dp)


def _adam_math(w, g, m, v):
    m = ADAM_B1 * m + (1.0 - ADAM_B1) * g
    v = ADAM_B2 * v + (1.0 - ADAM_B2) * (g * g)
    delta = -ADAM_LR * ((m * ADAM_C1) / (jnp.sqrt(v * ADAM_C2) + ADAM_EPS) + ADAM_WD * w)
    return delta, m, v


def _adam_rows(w, g, m, v, tm, name, g_sel=None):
    r, c = w.shape

    def body(w_ref, g_ref, m_ref, v_ref, go_ref, d_ref, mo_ref, vo_ref):
        g = g_ref[...]
        d, mn, vn = _adam_math(w_ref[...], g, m_ref[...], v_ref[...])
        go_ref[...] = g
        d_ref[...] = d
        mo_ref[...] = mn
        vo_ref[...] = vn

    spec = pl.BlockSpec((tm, c), lambda i: (i, 0))
    g_spec = spec if g_sel is None else pl.BlockSpec((None, tm, c), lambda i: (g_sel, i, 0))
    shape = jax.ShapeDtypeStruct((r, c), F32)
    return pl.pallas_call(
        body, name=name, grid=(r // tm,),
        in_specs=[spec, g_spec, spec, spec], out_specs=(spec,) * 4, out_shape=(shape,) * 4,
        compiler_params=_params("parallel"),
    )(w, g, m, v)


SMALL_ROW0 = {name: sum(r for _, r in SMALL_ROWS[:i]) for i, (name, _) in enumerate(SMALL_ROWS)}
LANE_MAJOR = ("ln_g", "ln_b", "b_gate", "ln_v_g", "ln_v_b", "conv_b")


def _lane_pieces(n):
    return [(q, slice(q * 128, (q + 1) * 128)) for q in range(n // 128)]


def _pack_small(d_ln_g, d_ln_b, d_bg, d_lnv_g, d_lnv_b, d_ws, d_bs, d_cw, d_cb, loss_part):
    def body(lg, lb, bg, vg, vb, ws, bs, cw, cb, loss, o_ref):
        def put(row0, vec):
            for q, cols in _lane_pieces(vec.shape[1]):
                o_ref[row0 + q:row0 + q + 1, :] = vec[:, cols]

        put(SMALL_ROW0["ln_g"], lg[...])
        put(SMALL_ROW0["ln_b"], lb[...])
        for n in range(2):
            put(SMALL_ROW0["b_gate"] + n * (D_MODEL // 128), bg[n])
        put(SMALL_ROW0["ln_v_g"], vg[...])
        put(SMALL_ROW0["ln_v_b"], vb[...])
        for h in range(N_HEADS):
            o_ref[SMALL_ROW0["w_s"] + h * CHUNK:SMALL_ROW0["w_s"] + (h + 1) * CHUNK, :] = ws[h]
        o_ref[SMALL_ROW0["b_s"]:SMALL_ROW0["b_s"] + N_HEADS, :] = bs[...]
        for c in range(3):
            put(SMALL_ROW0["conv_w"] + c * (D_MODEL // 128), cw[c:c + 1, :])
        put(SMALL_ROW0["conv_b"], cb[...])
        o_ref[SMALL_ROW0["loss"]:SMALL_ROW0["loss"] + 8, :] = jnp.broadcast_to(loss[...], (8, 128))

    return pl.pallas_call(
        body, name="pack_small", out_shape=jax.ShapeDtypeStruct((SMALL_TOTAL, 128), F32), compiler_params=_params(),
    )(d_ln_g, d_ln_b, d_bg, d_lnv_g, d_lnv_b, d_ws, d_bs.reshape(N_HEADS, CHUNK), d_cw, d_cb, loss_part)


def _adam_small(gathered, params):
    names = list(params)
    flat = [a for n in names for a in params[n]]

    def body(*refs):
        ga_ref = refs[0]
        ins = refs[1:1 + 3 * len(names)]
        outs = refs[1 + 3 * len(names):-1]
        gs_ref = refs[-1]
        g = ga_ref[0]
        for k in range(1, 8):
            g = g + ga_ref[k]
        gs_ref[...] = g
        for i, name in enumerate(names):
            w_ref, m_ref, v_ref = ins[3 * i:3 * i + 3]
            o_refs = outs[4 * i:4 * i + 4]
            row0 = SMALL_ROW0[name]
            if name in LANE_MAJOR:
                pieces = [((slice(None), cols), slice(row0 + q, row0 + q + 1))
                          for q, cols in _lane_pieces(w_ref.shape[1])]
            elif name == "w_s":
                pieces = [((0, h), slice(row0 + h * CHUNK, row0 + (h + 1) * CHUNK)) for h in range(N_HEADS)]
            else:
                pieces = [((0,), slice(row0, row0 + N_HEADS))]
            for idx, rows in pieces:
                gp = gs_ref[rows, :]
                res = (gp,) + _adam_math(w_ref[idx], gp, m_ref[idx], v_ref[idx])
                for o_ref, val in zip(o_refs, res):
                    o_ref[idx] = val
        gcw_ref, loss_ref = outs[-2:]
        for c in range(3):
            for q, cols in _lane_pieces(D_MODEL):
                r = SMALL_ROW0["conv_w"] + c * (D_MODEL // 128) + q
                gcw_ref[c:c + 1, cols] = gs_ref[r:r + 1, :]
        loss_ref[...] = gs_ref[SMALL_ROW0["loss"]:SMALL_ROW0["loss"] + 1, :]

    out_shape = [jax.ShapeDtypeStruct(params[n][0].shape, F32) for n in names for _ in range(4)]
    out_shape += [jax.ShapeDtypeStruct((3, D_MODEL), F32), jax.ShapeDtypeStruct((1, 128), F32)]
    res = pl.pallas_call(
        body, name="adam_small", out_shape=tuple(out_shape),
        scratch_shapes=[pltpu.VMEM((SMALL_TOTAL, 128), F32)], compiler_params=_params(),
    )(gathered, *flat)
    return {n: res[4 * i:4 * i + 4] for i, n in enumerate(names)}, res[-2], res[-1]


def _adam_conv_w(g_all, chip, w, m, v):
    cols = w.shape[2]

    def body(c_ref, g_ref, w_ref, m_ref, v_ref, go_ref, d_ref, mo_ref, vo_ref):
        g = g_ref[...]
        d, mn, vn = _adam_math(w_ref[...], g, m_ref[...], v_ref[...])
        go_ref[...] = g
        d_ref[...] = d
        mo_ref[...] = mn
        vo_ref[...] = vn

    own = pl.BlockSpec((None, 3, cols), lambda i, c_ref: (0, 0, 0))
    return pl.pallas_call(
        body, name="adam_conv_w",
        grid_spec=pltpu.PrefetchScalarGridSpec(
            num_scalar_prefetch=1, grid=(1,),
            in_specs=[pl.BlockSpec((3, cols), lambda i, c_ref: (0, c_ref[0])), own, own, own],
            out_specs=(own,) * 4),
        out_shape=(jax.ShapeDtypeStruct(w.shape, F32),) * 4,
        compiler_params=_params("arbitrary"),
    )(chip, g_all, w, m, v)


def kernel(x, w_in, b_gate, ln_v_g, ln_v_b, w_s, b_s, conv_w, conv_b, w_oa, w_ob, w_out, ln_g, ln_b, loss_target, m_w_in, m_b_gate, m_ln_v_g, m_ln_v_b, m_w_s, m_b_s, m_conv_w, m_conv_b, m_w_oa, m_w_ob, m_w_out, m_ln_g, m_ln_b, v_w_in, v_b_gate, v_ln_v_g, v_ln_v_b, v_w_s, v_b_s, v_conv_w, v_conv_b, v_w_oa, v_w_ob, v_w_out, v_ln_g, v_ln_b):
    t = x.shape[1]
    x2 = x[0]
    target = loss_target[0]
    chip = 2 * lax.axis_index("x") + lax.axis_index("y")
    conv_cols = conv_w.shape[2]

    chip1 = chip.astype(jnp.int32).reshape(1)
    w_pre = _cast_into_columns(w_in[0], chip1, N_CHIPS, 256, "cast_w_in")
    wo_b, wm = _prep_small_weights(w_oa[0], w_ob[0], w_out[0], w_s[0])
    conv_w8 = jnp.concatenate([conv_w[0], jnp.zeros((5, conv_cols), F32)], axis=0)
    bs_col = b_s[0].reshape(N_HEADS, CHUNK, 1)
    bg = b_gate.reshape(2, 1, D_MODEL)

    xb, xt = _cast_and_transpose(x2, min(512, t))
    p, w_full, wo_full, cw_full = _gather_and_project(xb, w_pre, wo_b, conv_w8, min(4096, t))
    wo3 = wo_full.reshape(3, D_MODEL, D_MODEL)
    conv_w_all = jnp.transpose(cw_full[:, :3, :], (1, 0, 2)).reshape(3, D_MODEL)
    tm_a = min(512, t)
    ya = _mixer_a_forward(p, wm, bs_col, ln_v_g, ln_v_b, tm_a)
    tm_b = min(512, t)
    yb = _mixer_b_forward(p, conv_w_all, conv_b, tm_b, 512)
    tm_m = min(1024, t)
    merged, oab = _merge_forward(ya, yb, wo3, p, bg, tm_m, 512)

    drb, gx, dmerged, d_ln_g, d_ln_b, loss_part = _head(merged, wo3, x2, target, ln_g, ln_b, min(512, t))
    doab, dp, d_bg, dyab = _gate_and_branch_backward(dmerged, oab, p, bg, wo3, min(256, t))
    dp, d_ws, d_bs, d_lnv_g, d_lnv_b = _mixer_a_backward(p, dyab, wm, bs_col, ln_v_g, ln_v_b, dp, tm_a)
    dp, d_cw, d_cb = _mixer_b_backward(p, dyab, conv_w_all, conv_b, dp, tm_b, 512)

    tk = min(2048, t)
    g_in = _win_grad(xt, dp, 1024, tk)
    g_oa = _weight_grad(ya, 0, doab, 0, 1024, 1024, tk, "grad_w_oa")
    g_ob = _weight_grad(yb, 0, doab, 1, 1024, 1024, tk, "grad_w_ob")
    g_out, r_in = _weight_grad(merged, 0, drb, 0, 1024, 1024, tk, "grad_w_out", exchange=g_in)
    r_o = _exchange_halves((g_oa, g_ob, g_out))
    s_in, s_o = _add_halves(g_in, (g_oa, g_ob, g_out), r_in, r_o)
    small_part = _pack_small(d_ln_g, d_ln_b, d_bg, d_lnv_g, d_lnv_b, d_ws, d_bs, d_cw, d_cb, loss_part)
    tm_x = min(512, t)
    gx, q_in, q_o, gathered = _input_grad_and_scatter(dp, w_full, gx, s_in, s_o, small_part, tm_x)
    grad_x = _input_grad_tail(dp, w_full, gx, tm_x)
    f_in, f_o = _sum_chips(q_in, q_o, s_in, s_o)
    gsum_in, gsum_o = _share_halves(f_in, f_o)

    big = {}
    big["w_in"] = _adam_rows(w_in[0], gsum_in, m_w_in[0], v_w_in[0], 128, "adam_w_in")
    for n, (name, w, m, v) in enumerate((("w_oa", w_oa, m_w_oa, v_w_oa), ("w_ob", w_ob, m_w_ob, v_w_ob),
                                         ("w_out", w_out, m_w_out, v_w_out))):
        big[name] = _adam_rows(w[0], gsum_o, m[0], v[0], 256, "adam_" + name, g_sel=n)

    small, g_conv_w, loss_row = _adam_small(gathered, {
        "ln_g": (ln_g, m_ln_g, v_ln_g), "ln_b": (ln_b, m_ln_b, v_ln_b), "b_gate": (b_gate, m_b_gate, v_b_gate),
        "ln_v_g": (ln_v_g, m_ln_v_g, v_ln_v_g), "ln_v_b": (ln_v_b, m_ln_v_b, v_ln_v_b),
        "w_s": (w_s, m_w_s, v_w_s), "b_s": (b_s, m_b_s, v_b_s), "conv_b": (conv_b, m_conv_b, v_conv_b)})
    small["conv_w"] = _adam_conv_w(g_conv_w, chip1, conv_w, m_conv_w, v_conv_w)
    loss = loss_row[0, 0]

    order = ("w_in", "b_gate", "ln_v_g", "ln_v_b", "w_s", "b_s", "conv_w", "conv_b", "w_oa", "w_ob", "w_out",
             "ln_g", "ln_b")
    outs = [loss, grad_x[None]]
    for which in range(4):
        for name in order:
            outs.append(big[name][which][None] if name in big else small[name][which])
    return tuple(outs)
```

```python
import functools
import math

import jax
import jax.numpy as jnp
from jax import lax
from jax.experimental import pallas as pl
from jax.experimental.pallas import tpu as pltpu

F32 = jnp.float32
BF16 = jnp.bfloat16

D_MODEL = 2048
N_HEADS = 8
HEAD_DIM = D_MODEL // N_HEADS
CHUNK = 128
N_SEG = 9
N_CHIPS = 4
SHARD_COLS = N_SEG * D_MODEL // N_CHIPS
COL_BLOCK = 512
BLOCKS_PER_SHARD = SHARD_COLS // COL_BLOCK
BLOCKS_PER_SEG = D_MODEL // COL_BLOCK
SUBS = 3
SHARD_ROWS = D_MODEL // N_CHIPS
N_SLOTS = 12
BLOCK_A, BLOCK_G, BLOCK_B = 0, 2, 2
SLOT_OF_SEG = (0, 1, 2, 8, 9, 10, 11, 4, 5)
DN_ALPHA = 2.0 ** 0.25
LN_EPS = 1e-5
GELU_K = math.sqrt(2.0 / math.pi)
GELU_C = 0.044715

ADAM_LR = 0.001
ADAM_B1 = 0.9
ADAM_B2 = 0.999
ADAM_EPS = 1e-08
ADAM_WD = 0.01
ADAM_STEP = 10
ADAM_C1 = 1.0 / (1.0 - ADAM_B1 ** ADAM_STEP)
ADAM_C2 = 1.0 / (1.0 - ADAM_B2 ** ADAM_STEP)

VMEM_LIMIT = 60 * 1024 * 1024
MESH = pl.DeviceIdType.MESH
ANY = pl.BlockSpec(memory_space=pl.ANY)

SMALL_ROWS = (("ln_g", 16), ("ln_b", 16), ("b_gate", 32), ("ln_v_g", 16), ("ln_v_b", 16),
              ("w_s", 1024), ("b_s", 8), ("conv_w", 48), ("conv_b", 16), ("loss", 8))
SMALL_TOTAL = sum(r for _, r in SMALL_ROWS)


def _params(*sem):
    return pltpu.CompilerParams(dimension_semantics=sem, vmem_limit_bytes=VMEM_LIMIT)


def _sigmoid(x):
    return 1.0 / (1.0 + jnp.exp(-x))


def _gelu_and_grad(x):
    x2 = x * x
    th = jnp.tanh(GELU_K * (x + GELU_C * x * x2))
    g = 0.5 * x * (1.0 + th)
    dg = 0.5 * (1.0 + th) + 0.5 * x * (1.0 - th * th) * (GELU_K * (1.0 + 3.0 * GELU_C * x2))
    return g, dg


def _gelu(x):
    return 0.5 * x * (1.0 + jnp.tanh(GELU_K * (x + GELU_C * x * x * x)))


def _dot(a, b, ca=1, cb=0):
    return lax.dot_general(a, b, (((ca,), (cb,)), ((), ())), preferred_element_type=F32)


def _cast_and_transpose(x, tm):
    t, d = x.shape

    def body(x_ref, o_ref, ot_ref):
        v = x_ref[...]
        o_ref[...] = v.astype(BF16)
        ot_ref[...] = v.T.astype(BF16)

    return pl.pallas_call(
        body, name="cast_x", grid=(t // tm,),
        in_specs=[pl.BlockSpec((tm, d), lambda i: (i, 0))],
        out_specs=(pl.BlockSpec((tm, d), lambda i: (i, 0)), pl.BlockSpec((d, tm), lambda i: (0, i))),
        out_shape=(jax.ShapeDtypeStruct((t, d), BF16), jax.ShapeDtypeStruct((d, t), BF16)),
        compiler_params=_params("parallel"),
    )(x)


def _cast_into_columns(w, slot, n_slots, tm, name):
    r, c = w.shape

    def body(s_ref, w_ref, o_ref):
        o_ref[...] = w_ref[...].astype(BF16)

    return pl.pallas_call(
        body, name=name,
        grid_spec=pltpu.PrefetchScalarGridSpec(
            num_scalar_prefetch=1, grid=(r // tm,),
            in_specs=[pl.BlockSpec((tm, c), lambda i, s_ref: (i, 0))],
            out_specs=pl.BlockSpec((tm, c), lambda i, s_ref: (i, s_ref[0]))),
        out_shape=jax.ShapeDtypeStruct((r, n_slots * c), BF16),
        compiler_params=_params("parallel"),
    )(slot, w)


def _prep_small_weights(w_oa, w_ob, w_out, w_s):
    rows = w_oa.shape[0]

    def body(a_ref, b_ref, c_ref, ws_ref, wo_ref, wm_ref):
        wo_ref[0] = a_ref[...].astype(BF16)
        wo_ref[1] = b_ref[...].astype(BF16)
        wo_ref[2] = c_ref[...].astype(BF16)
        t = lax.broadcasted_iota(jnp.int32, (CHUNK, CHUNK), 0)
        s = lax.broadcasted_iota(jnp.int32, (CHUNK, CHUNK), 1)
        for h in range(N_HEADS):
            wm_ref[h] = jnp.where(s <= t, ws_ref[h], 0.0).astype(BF16)

    return pl.pallas_call(
        body, name="prep_small_weights",
        out_shape=(jax.ShapeDtypeStruct((3, rows, D_MODEL), BF16),
                   jax.ShapeDtypeStruct((N_HEADS, CHUNK, CHUNK), BF16)),
        compiler_params=_params(),
    )(w_oa, w_ob, w_out, w_s)


def _mesh_pos():
    x, y, c = lax.axis_index("x"), lax.axis_index("y"), lax.axis_index("c")
    chips = [(1 - x, y), (x, 1 - y), (1 - x, 1 - y)]
    return x, y, c, chips


def _slot_of_seg(seg):
    return jnp.where(seg < 3, seg, jnp.where(seg < 7, seg + 5, seg - 3))


def _gather_and_project(xb, w_pre, wo_b, conv_w8, tm):
    t = xb.shape[0]
    d, sc = w_pre.shape[0], w_pre.shape[1] // N_CHIPS
    rows = wo_b.shape[1]
    hd, hr = d // 2, rows // 2
    nj = BLOCKS_PER_SHARD // SUBS
    pc = nj * COL_BLOCK
    units = N_CHIPS * SUBS
    ni = t // tm
    total = units * ni * nj
    mx, my = lax.axis_index("x"), lax.axis_index("y")
    order = jnp.stack([2 * mx + my, 2 * (1 - mx) + my, 2 * mx + (1 - my),
                       2 * (1 - mx) + (1 - my)]).astype(jnp.int32)

    def body(order_ref, x_ref, wpre_ref, wo_ref, cw_ref, p_ref, wf_ref, wof_ref, cwf_ref,
             wbuf, wsem, xbuf, xsem, send_sems, recv_sems, local_sems):
        x, y, c, chips = _mesh_pos()
        me = 2 * x + y
        sibling = (x, y, 1 - c)
        u, i, j = pl.program_id(0), pl.program_id(1), pl.program_id(2)
        n = (u * ni + i) * nj + j
        m = u * ni + i

        def rows_start(m_):
            pltpu.make_async_copy(x_ref.at[pl.ds(pl.multiple_of(lax.rem(m_, ni) * tm, tm), tm), :],
                                  xbuf.at[lax.rem(m_, 2)], xsem.at[lax.rem(m_, 2)]).start()

        def chip_of(q):
            return 2 * chips[q][0] + chips[q][1]

        def piece(ref, k, cc, r):
            return ref.at[pl.ds(cc * hd, hd), pl.ds(pl.multiple_of(k * sc + r * pc, COL_BLOCK), pc)]

        def wo_half(ref4, k, cc):
            return ref4.at[:, k, pl.ds(cc * hr, hr), :]

        def rcopy(sem, src, dst, to):
            return pltpu.make_async_remote_copy(src_ref=src, dst_ref=dst, send_sem=send_sems.at[sem],
                                                recv_sem=recv_sems.at[sem], device_id=to, device_id_type=MESH)

        def w_send(q, r):
            return rcopy(q * SUBS + r, piece(wpre_ref, me, c, r), piece(wf_ref, me, c, r), (*chips[q], c))

        def w_landed(q, r):
            return rcopy(q * SUBS + r, piece(wpre_ref, me, c, r), piece(wf_ref, chip_of(q), c, r), sibling)

        def w_forward(q, r, cc):
            ref = piece(wf_ref, chip_of(q), cc, r)
            return rcopy(9 + q * SUBS + r, ref, ref, sibling)

        def wo_send(q):
            return rcopy(18 + q, wo_ref.at[:, pl.ds(c * hr, hr), :], wo_half(wof_ref, me, c), (*chips[q], c))

        def wo_landed(q):
            return rcopy(18 + q, wo_ref.at[:, pl.ds(c * hr, hr), :], wo_half(wof_ref, chip_of(q), c), sibling)

        def wo_forward(q, cc):
            ref = wo_half(wof_ref, chip_of(q), cc)
            return rcopy(21 + q, ref, ref, sibling)

        def conv_send(q):
            return rcopy(24 + q, cw_ref, cwf_ref.at[me], (*chips[q], c))

        def local_copies():
            return [pltpu.make_async_copy(wo_ref, wof_ref.at[:, me], local_sems.at[0]),
                    pltpu.make_async_copy(cw_ref, cwf_ref.at[me], local_sems.at[1])]

        def tile_start(u_, j_, slot):
            g = order_ref[u_ // SUBS] * BLOCKS_PER_SHARD + lax.rem(u_, SUBS) * nj + j_
            cols = pl.ds(pl.multiple_of(g * COL_BLOCK, COL_BLOCK), COL_BLOCK)

            @pl.when(u_ < SUBS)
            def _():
                pltpu.make_async_copy(wpre_ref.at[:, cols], wbuf.at[slot], wsem.at[slot]).start()

            @pl.when(u_ >= SUBS)
            def _():
                pltpu.make_async_copy(wf_ref.at[:, cols], wbuf.at[slot], wsem.at[slot]).start()

        def end_of(u_):
            return jnp.logical_and(u == u_, jnp.logical_and(i == ni - 1, j == nj - 1))

        @pl.when(n == 0)
        def _():
            for cp in local_copies():
                cp.start()
            for r in range(SUBS):
                for q in (0, 1):
                    w_send(q, r).start()
            for q in range(3):
                conv_send(q).start()
            tile_start(0, 0, 0)
            rows_start(0)

        def pass_on(q, r):
            w_landed(q, r).wait_recv()
            w_forward(q, r, c).start()

        for u_ in range(1, units - 1):
            @pl.when(end_of(u_))
            def _(u_=u_):
                if u_ <= SUBS:
                    for q in (0, 1):
                        pass_on(q, u_ - 1)
                if u_ == SUBS:
                    for r in range(SUBS):
                        for q in (0, 1):
                            w_send(q, r).wait_send()
                    for r in range(SUBS):
                        w_send(2, r).start()
                if 2 * SUBS - 1 <= u_ <= 3 * SUBS - 2:
                    pass_on(2, u_ - (2 * SUBS - 1))
                if u_ == 3 * SUBS - 2:
                    for r in range(SUBS):
                        w_send(2, r).wait_send()
                    for q in range(3):
                        wo_send(q).start()
                nxt = u_ + 1
                if nxt >= SUBS:
                    w_forward(nxt // SUBS - 1, nxt % SUBS, 1 - c).wait_recv()

        n1 = n + 1

        @pl.when(n1 < total)
        def _():
            tile_start(n1 // (ni * nj), lax.rem(n1, nj), lax.rem(n1, 2))

        xslot = lax.rem(m, 2)

        @pl.when(j == 0)
        def _():
            @pl.when(m + 1 < units * ni)
            def _():
                rows_start(m + 1)

            pltpu.make_async_copy(x_ref.at[pl.ds(0, tm), :], xbuf.at[xslot], xsem.at[xslot]).wait()

        slot = lax.rem(n, 2)
        pltpu.make_async_copy(wpre_ref.at[:, pl.ds(0, COL_BLOCK)], wbuf.at[slot], wsem.at[slot]).wait()
        p_ref[...] = _dot(xbuf[xslot], wbuf[slot]).astype(BF16)

        @pl.when(n == total - 1)
        def _():
            for q in range(3):
                wo_landed(q).wait_recv()
                wo_forward(q, c).start()
            for q in range(3):
                wo_forward(q, 1 - c).wait_recv()
                rcopy(24 + q, cw_ref, cwf_ref.at[chip_of(q)], sibling).wait_recv()
            for q in range(3):
                for r in range(SUBS):
                    w_forward(q, r, c).wait_send()
                wo_send(q).wait_send()
                wo_forward(q, c).wait_send()
                conv_send(q).wait_send()
            for cp in local_copies():
                cp.wait()

    def p_map(u, i, j, o):
        g = o[u // SUBS] * BLOCKS_PER_SHARD + lax.rem(u, SUBS) * nj + j
        return (_slot_of_seg(g // BLOCKS_PER_SEG), i, lax.rem(g, BLOCKS_PER_SEG))

    return pl.pallas_call(
        body, name="gather_and_project",
        grid_spec=pltpu.PrefetchScalarGridSpec(
            num_scalar_prefetch=1, grid=(units, ni, nj),
            in_specs=[ANY, ANY, ANY, ANY],
            out_specs=(pl.BlockSpec((None, tm, COL_BLOCK), p_map), ANY, ANY, ANY),
            scratch_shapes=[pltpu.VMEM((2, d, COL_BLOCK), BF16), pltpu.SemaphoreType.DMA((2,)),
                            pltpu.VMEM((2, tm, D_MODEL), BF16), pltpu.SemaphoreType.DMA((2,)),
                            pltpu.SemaphoreType.DMA((27,)), pltpu.SemaphoreType.DMA((27,)),
                            pltpu.SemaphoreType.DMA((2,))]),
        out_shape=(jax.ShapeDtypeStruct((N_SLOTS, t, D_MODEL), BF16),
                   jax.ShapeDtypeStruct((d, N_CHIPS * sc), BF16),
                   jax.ShapeDtypeStruct((3, N_CHIPS, rows, D_MODEL), BF16),
                   jax.ShapeDtypeStruct((N_CHIPS,) + conv_w8.shape, F32)),
        input_output_aliases={2: 1},
        compiler_params=pltpu.CompilerParams(dimension_semantics=("arbitrary",) * 3, vmem_limit_bytes=VMEM_LIMIT,
                                             has_side_effects=True),
    )(order, xb, w_pre, wo_b, conv_w8)


def _exchange_halves(g_o):
    hr = SHARD_ROWS // 2
    g_o4 = [g.reshape(N_CHIPS, 2, hr, D_MODEL) for g in g_o]

    def body(ga_ref, gb_ref, gc_ref, ra_ref, rb_ref, rc_ref, send_sems, recv_sems):
        x, y, c, _ = _mesh_pos()
        sibling = (x, y, 1 - c)
        cps = []
        for n, (g_ref, r_ref) in enumerate(((ga_ref, ra_ref), (gb_ref, rb_ref), (gc_ref, rc_ref))):
            cps.append(pltpu.make_async_remote_copy(src_ref=g_ref.at[:, 1 - c], dst_ref=r_ref,
                                                    send_sem=send_sems.at[n], recv_sem=recv_sems.at[n],
                                                    device_id=sibling, device_id_type=MESH))
        for cp in cps:
            cp.start()
        for cp in cps:
            cp.wait()

    o_shape = jax.ShapeDtypeStruct((N_CHIPS, hr, D_MODEL), BF16)
    return pl.pallas_call(
        body, name="rs_exchange_halves",
        in_specs=[ANY] * 3, out_specs=(ANY,) * 3,
        out_shape=(o_shape, o_shape, o_shape),
        scratch_shapes=[pltpu.SemaphoreType.DMA((3,)), pltpu.SemaphoreType.DMA((3,))],
        compiler_params=pltpu.CompilerParams(has_side_effects=True),
    )(*g_o4)


def _add_halves(g_in, g_o, r_in, r_o):
    d, c9 = g_in.shape
    hd = d // 2
    hr = SHARD_ROWS // 2
    core = lax.axis_index("c").astype(jnp.int32).reshape(1)
    tm = min(512, hd)
    nb = hd // tm

    def body_in(c_ref, g_ref, r_ref, o_ref):
        o_ref[...] = (g_ref[...].astype(F32) + r_ref[...].astype(F32)).astype(BF16)

    s_in = pl.pallas_call(
        body_in, name="rs_add_halves_in",
        grid_spec=pltpu.PrefetchScalarGridSpec(
            num_scalar_prefetch=1, grid=(N_CHIPS, nb),
            in_specs=[pl.BlockSpec((tm, SHARD_COLS), lambda k, i, c_ref: (c_ref[0] * nb + i, k)),
                      pl.BlockSpec((tm, SHARD_COLS), lambda k, i, c_ref: (i, k))],
            out_specs=pl.BlockSpec((None, tm, SHARD_COLS), lambda k, i, c_ref: (k, i, 0))),
        out_shape=jax.ShapeDtypeStruct((N_CHIPS, hd, SHARD_COLS), BF16),
        compiler_params=_params("parallel", "parallel"),
    )(core, g_in, r_in)

    g_o4 = [g.reshape(N_CHIPS, 2, hr, D_MODEL) for g in g_o]

    def body_o(c_ref, ga_ref, gb_ref, gc_ref, ra_ref, rb_ref, rc_ref, o_ref):
        for n, (g_ref, r_ref) in enumerate(((ga_ref, ra_ref), (gb_ref, rb_ref), (gc_ref, rc_ref))):
            o_ref[n] = (g_ref[...].astype(F32) + r_ref[...].astype(F32)).astype(BF16)

    gspec = pl.BlockSpec((None, None, hr, D_MODEL), lambda k, c_ref: (k, c_ref[0], 0, 0))
    rspec = pl.BlockSpec((None, hr, D_MODEL), lambda k, c_ref: (k, 0, 0))
    s_o = pl.pallas_call(
        body_o, name="rs_add_halves_o",
        grid_spec=pltpu.PrefetchScalarGridSpec(
            num_scalar_prefetch=1, grid=(N_CHIPS,),
            in_specs=[gspec] * 3 + [rspec] * 3,
            out_specs=pl.BlockSpec((3, None, hr, D_MODEL), lambda k, c_ref: (0, k, 0, 0))),
        out_shape=jax.ShapeDtypeStruct((3, N_CHIPS, hr, D_MODEL), BF16),
        compiler_params=_params("parallel"),
    )(core, *g_o4, *r_o)
    return s_in, s_o


def _sum_chips(r_in, r_o, s_in, s_o):
    _, hd, sc = r_in.shape
    hr = r_o.shape[2]
    tm = min(256, hd)
    nb = hd // tm
    pos = jnp.stack([2 * lax.axis_index("x") + lax.axis_index("y"), lax.axis_index("c")]).astype(jnp.int32)

    def chip_sum(pos_ref, r_ref, s_ref):
        acc = None
        for k in range(N_CHIPS):
            term = jnp.where(pos_ref[0] == k, s_ref[...], r_ref[k]).astype(F32)
            acc = term if acc is None else acc + term
        return acc

    def body_in(pos_ref, r_ref, s_ref, o_ref):
        o_ref[...] = chip_sum(pos_ref, r_ref, s_ref)

    f_in = pl.pallas_call(
        body_in, name="rs_sum_chips_in",
        grid_spec=pltpu.PrefetchScalarGridSpec(
            num_scalar_prefetch=1, grid=(nb,),
            in_specs=[pl.BlockSpec((N_CHIPS, tm, sc), lambda i, p: (0, i, 0)),
                      pl.BlockSpec((None, tm, sc), lambda i, p: (p[0], i, 0))],
            out_specs=pl.BlockSpec((tm, sc), lambda i, p: (p[1] * nb + i, 0))),
        out_shape=jax.ShapeDtypeStruct((2 * hd, sc), F32),
        compiler_params=_params("parallel"),
    )(pos, r_in, s_in)

    def body_o(pos_ref, r_ref, s_ref, o_ref):
        o_ref[...] = chip_sum(pos_ref, r_ref, s_ref)

    f_o = pl.pallas_call(
        body_o, name="rs_sum_chips_o",
        grid_spec=pltpu.PrefetchScalarGridSpec(
            num_scalar_prefetch=1, grid=(3,),
            in_specs=[pl.BlockSpec((N_CHIPS, None, hr, D_MODEL), lambda n, p: (0, n, 0, 0)),
                      pl.BlockSpec((None, None, hr, D_MODEL), lambda n, p: (n, p[0], 0, 0))],
            out_specs=pl.BlockSpec((None, hr, D_MODEL), lambda n, p: (n, p[1], 0))),
        out_shape=jax.ShapeDtypeStruct((3, 2 * hr, D_MODEL), F32),
        compiler_params=_params("parallel"),
    )(pos, r_o, s_o)
    return f_in, f_o


def _share_halves(f_in, f_o):
    hd, sc = f_in.shape[0] // 2, f_in.shape[1]
    hr = f_o.shape[1] // 2

    def body(fi_ref, fo_ref, gi_ref, go_ref, send_sems, recv_sems):
        x, y, c, _ = _mesh_pos()
        sibling = (x, y, 1 - c)

        def halves(cc):
            rows_i, rows_o = pl.ds(cc * hd, hd), pl.ds(cc * hr, hr)
            return (fi_ref.at[rows_i, :], gi_ref.at[rows_i, :]), (fo_ref.at[:, rows_o, :], go_ref.at[:, rows_o, :])

        def copies(cc):
            return [pltpu.make_async_remote_copy(src_ref=src, dst_ref=dst, send_sem=send_sems.at[n],
                                                 recv_sem=recv_sems.at[n], device_id=sibling, device_id_type=MESH)
                    for n, (src, dst) in enumerate(halves(cc))]

        sends = copies(c)
        for cp in sends:
            cp.start()
        for cp in copies(1 - c):
            cp.wait_recv()
        for cp in sends:
            cp.wait_send()

    return pl.pallas_call(
        body, name="rs_share_halves",
        in_specs=[ANY, ANY], out_specs=(ANY, ANY),
        out_shape=(jax.ShapeDtypeStruct(f_in.shape, F32), jax.ShapeDtypeStruct(f_o.shape, F32)),
        scratch_shapes=[pltpu.SemaphoreType.DMA((2,)), pltpu.SemaphoreType.DMA((2,))],
        input_output_aliases={0: 0, 1: 1},
        compiler_params=pltpu.CompilerParams(has_side_effects=True),
    )(f_in, f_o)


def _merge_forward(ya, yb, wo3, p, bg, tm, tn):
    t = ya.shape[0]

    def body(ya_ref, yb_ref, wa_ref, wb_ref, g_ref, bg_ref, m_ref, oab_ref):
        oa = _dot(ya_ref[...], wa_ref[...])
        ob = _dot(yb_ref[...], wb_ref[...])
        ga = _sigmoid(g_ref[0].astype(F32) + bg_ref[0])
        gb = _sigmoid(g_ref[1].astype(F32) + bg_ref[1])
        m_ref[...] = (ga * oa + gb * ob).astype(BF16)
        oab_ref[0] = oa.astype(BF16)
        oab_ref[1] = ob.astype(BF16)

    return pl.pallas_call(
        body, name="merge_forward", grid=(t // tm, D_MODEL // tn),
        in_specs=[pl.BlockSpec((tm, D_MODEL), lambda i, j: (i, 0)),
                  pl.BlockSpec((tm, D_MODEL), lambda i, j: (i, 0)),
                  pl.BlockSpec((None, D_MODEL, tn), lambda i, j: (0, 0, j)),
                  pl.BlockSpec((None, D_MODEL, tn), lambda i, j: (1, 0, j)),
                  pl.BlockSpec((2, tm, tn), lambda i, j: (BLOCK_G, i, j)),
                  pl.BlockSpec((2, 1, tn), lambda i, j: (0, 0, j))],
        out_specs=(pl.BlockSpec((tm, tn), lambda i, j: (i, j)),
                   pl.BlockSpec((2, tm, tn), lambda i, j: (0, i, j))),
        out_shape=(jax.ShapeDtypeStruct((t, D_MODEL), BF16), jax.ShapeDtypeStruct((2, t, D_MODEL), BF16)),
        compiler_params=_params("parallel", "parallel"),
    )(ya, yb, wo3, wo3, p, bg)


HEAD_ROWS = 256


def _head(merged, wo3, x, target, ln_g, ln_b, tm):
    t = x.shape[0]
    inv_d = 1.0 / D_MODEL

    def body(m_ref, w_ref, x_ref, t_ref, g_ref, b_ref, dr_ref, gx_ref, dm_ref, dg_ref, db_ref, loss_ref):
        i = pl.program_id(0)

        @pl.when(i == 0)
        def _():
            dg_ref[...] = jnp.zeros_like(dg_ref)
            db_ref[...] = jnp.zeros_like(db_ref)
            loss_ref[...] = jnp.zeros_like(loss_ref)

        w = w_ref[...]
        g = g_ref[...]
        tiles = [slice(r0, r0 + HEAD_ROWS) for r0 in range(0, tm, HEAD_ROWS)]
        firsts = [_dot(m_ref[rows, :], w) for rows in tiles]
        for rows, out in zip(tiles, firsts):
            r = DN_ALPHA * x_ref[rows, :] + out
            mu = jnp.mean(r, axis=-1, keepdims=True)
            xc = r - mu
            var = jnp.mean(xc * xc, axis=-1, keepdims=True)
            rstd = lax.rsqrt(var + LN_EPS)
            xhat = xc * rstd
            e = xhat * g + b_ref[...] - t_ref[rows, :]
            se = jnp.sum(jnp.sum(e * e, axis=1, keepdims=True), axis=0, keepdims=True)
            loss_ref[...] += jnp.broadcast_to((0.5 * inv_d) * se, loss_ref.shape)
            dy = e * inv_d
            db_ref[...] += jnp.sum(dy, axis=0, keepdims=True)
            dg_ref[...] += jnp.sum(dy * xhat, axis=0, keepdims=True)
            dxh = dy * g
            m1 = jnp.mean(dxh, axis=-1, keepdims=True)
            m2 = jnp.mean(dxh * xhat, axis=-1, keepdims=True)
            dr = rstd * (dxh - m1 - xhat * m2)
            gx_ref[rows, :] = DN_ALPHA * dr
            drb = dr.astype(BF16)
            dr_ref[rows, :] = drb
            dm_ref[rows, :] = _dot(drb, w, 1, 1).astype(BF16)

    row = pl.BlockSpec((tm, D_MODEL), lambda i: (i, 0))
    vec = pl.BlockSpec((1, D_MODEL), lambda i: (0, 0))
    return pl.pallas_call(
        body, name="head", grid=(t // tm,),
        in_specs=[row, pl.BlockSpec((None, D_MODEL, D_MODEL), lambda i: (2, 0, 0), pipeline_mode=pl.Buffered(1)),
                  row, row, vec, vec],
        out_specs=(row, row, row, vec, vec, pl.BlockSpec((1, 128), lambda i: (0, 0))),
        out_shape=(jax.ShapeDtypeStruct((t, D_MODEL), BF16), jax.ShapeDtypeStruct((t, D_MODEL), F32),
                   jax.ShapeDtypeStruct((t, D_MODEL), BF16), jax.ShapeDtypeStruct((1, D_MODEL), F32),
                   jax.ShapeDtypeStruct((1, D_MODEL), F32), jax.ShapeDtypeStruct((1, 128), F32)),
        compiler_params=_params("arbitrary"),
    )(merged, wo3, x, target, ln_g, ln_b)


def _gate_and_branch_backward(dmerged, oab, p, bg, wo3, tm):
    t = dmerged.shape[0]

    def body(dm_ref, oab_ref, g_ref, bg_ref, wa_ref, wb_ref, do_ref, dpg_ref, dbg_ref, dy_ref):
        @pl.when(pl.program_id(0) == 0)
        def _():
            dbg_ref[...] = jnp.zeros_like(dbg_ref)

        dm = dm_ref[...].astype(F32)
        for n, w_ref in enumerate((wa_ref, wb_ref)):
            gate = _sigmoid(g_ref[n].astype(F32) + bg_ref[n])
            d_o = (dm * gate).astype(BF16)
            do_ref[n] = d_o
            dgate = dm * oab_ref[n].astype(F32) * gate * (1.0 - gate)
            dpg_ref[n] = dgate.astype(BF16)
            dbg_ref[n] += jnp.sum(dgate, axis=0, keepdims=True)
            dy_ref[n] = _dot(d_o, w_ref[...], 1, 1).astype(BF16)

    pair = pl.BlockSpec((2, tm, D_MODEL), lambda i: (0, i, 0))
    gates = pl.BlockSpec((2, tm, D_MODEL), lambda i: (BLOCK_G, i, 0))
    vec = pl.BlockSpec((2, 1, D_MODEL), lambda i: (0, 0, 0))

    def weight(n):
        return pl.BlockSpec((None, D_MODEL, D_MODEL), lambda i: (n, 0, 0), pipeline_mode=pl.Buffered(1))

    pair_shape = jax.ShapeDtypeStruct((2, t, D_MODEL), BF16)
    return pl.pallas_call(
        body, name="gate_and_branch_backward", grid=(t // tm,),
        in_specs=[pl.BlockSpec((tm, D_MODEL), lambda i: (i, 0)), pair, gates, vec, weight(0), weight(1)],
        out_specs=(pair, gates, vec, pair),
        out_shape=(pair_shape, jax.ShapeDtypeStruct((N_SLOTS, t, D_MODEL), BF16),
                   jax.ShapeDtypeStruct((2, 1, D_MODEL), F32), pair_shape),
        compiler_params=_params("arbitrary"),
    )(dmerged, oab, p, bg, wo3, wo3)


def _weight_grad(a, a_sel, b, b_sel, tm, tn, tk, name, exchange=None):
    t = a.shape[-2]
    nk = t // tk
    ni, nj = D_MODEL // tm, D_MODEL // tn

    def body(a_ref, b_ref, *rest):
        if exchange is None:
            o_ref, acc_ref = rest
        else:
            g_ref, o_ref, r_ref, acc_ref, send_sem, recv_sem = rest
        i, j, k = pl.program_id(0), pl.program_id(1), pl.program_id(2)

        if exchange is not None:
            x, y, c, _ = _mesh_pos()
            hd = exchange.shape[0] // 2
            swap = pltpu.make_async_remote_copy(src_ref=g_ref.at[pl.ds((1 - c) * hd, hd), :], dst_ref=r_ref,
                                                send_sem=send_sem, recv_sem=recv_sem,
                                                device_id=(x, y, 1 - c), device_id_type=MESH)

            @pl.when(jnp.logical_and(i == 0, jnp.logical_and(j == 0, k == 0)))
            def _():
                swap.start()

        @pl.when(k == 0)
        def _():
            acc_ref[...] = jnp.zeros_like(acc_ref)

        acc_ref[...] += _dot(a_ref[...], b_ref[...], 0, 0)

        @pl.when(k == nk - 1)
        def _():
            o_ref[...] = acc_ref[...].astype(BF16)

        if exchange is not None:
            @pl.when(jnp.logical_and(i == ni - 1, jnp.logical_and(j == nj - 1, k == nk - 1)))
            def _():
                swap.wait()

    def spec(arr, sel, width, which):
        if arr.ndim == 2:
            return pl.BlockSpec((tk, width), lambda i, j, k: (k, (i, j)[which]))
        return pl.BlockSpec((None, tk, width), lambda i, j, k: (sel, k, (i, j)[which]))

    o_spec = pl.BlockSpec((tm, tn), lambda i, j, k: (i, j))
    o_shape = jax.ShapeDtypeStruct((D_MODEL, D_MODEL), BF16)
    if exchange is None:
        return pl.pallas_call(
            body, name=name, grid=(ni, nj, nk),
            in_specs=[spec(a, a_sel, tm, 0), spec(b, b_sel, tn, 1)],
            out_specs=o_spec, out_shape=o_shape,
            scratch_shapes=[pltpu.VMEM((tm, tn), F32)],
            compiler_params=_params("parallel", "parallel", "arbitrary"),
        )(a, b)
    return pl.pallas_call(
        body, name=name, grid=(ni, nj, nk),
        in_specs=[spec(a, a_sel, tm, 0), spec(b, b_sel, tn, 1), ANY],
        out_specs=(o_spec, ANY),
        out_shape=(o_shape, jax.ShapeDtypeStruct((exchange.shape[0] // 2, exchange.shape[1]), BF16)),
        scratch_shapes=[pltpu.VMEM((tm, tn), F32), pltpu.SemaphoreType.DMA, pltpu.SemaphoreType.DMA],
        compiler_params=pltpu.CompilerParams(dimension_semantics=("arbitrary",) * 3, vmem_limit_bytes=VMEM_LIMIT,
                                             has_side_effects=True),
    )(a, b, exchange)


def _win_grad(xt, dp, tn, tk):
    _, t, _ = dp.shape
    nk = t // tk
    per_seg = D_MODEL // tn
    nj = N_SEG * per_seg

    def body(x_ref, dp_ref, o_ref, acc_ref):
        k = pl.program_id(1)

        @pl.when(k == 0)
        def _():
            acc_ref[...] = jnp.zeros_like(acc_ref)

        acc_ref[...] += _dot(x_ref[...], dp_ref[...])

        @pl.when(k == nk - 1)
        def _():
            o_ref[...] = acc_ref[...].astype(BF16)

    return pl.pallas_call(
        body, name="grad_w_in", grid=(nj, nk),
        in_specs=[pl.BlockSpec((D_MODEL, tk), lambda j, k: (0, k)),
                  pl.BlockSpec((None, tk, tn), lambda j, k: (_slot_of_seg(j // per_seg), k, j % per_seg))],
        out_specs=pl.BlockSpec((D_MODEL, tn), lambda j, k: (0, j)),
        out_shape=jax.ShapeDtypeStruct((D_MODEL, N_SEG * D_MODEL), BF16),
        scratch_shapes=[pltpu.VMEM((D_MODEL, tn), F32)],
        compiler_params=_params("parallel", "arbitrary"),
    )(xt, dp)


def _input_grad_and_scatter(dp, w_full, gx, s_in, s_o, small, tm):
    _, t, _ = dp.shape
    ni, nk = t // tm, N_SEG - 1
    _, hd, sc = s_in.shape
    hr = s_o.shape[2]

    def body(dp_ref, w_ref, gx_ref, si_ref, so_ref, sm_ref, o_ref, ri_ref, ro_ref, ga_ref,
             send_sems, recv_sems, local_sem):
        i, k = pl.program_id(0), pl.program_id(1)
        x, y, c, chips = _mesh_pos()
        me = 2 * x + y
        dev = 4 * x + 2 * y + c

        def peer(r):
            return (x ^ ((r >> 2) & 1), y ^ ((r >> 1) & 1), c ^ (r & 1))

        def sends():
            cps = []
            for q, (cx, cy) in enumerate(chips):
                dest = 2 * cx + cy
                cps.append(pltpu.make_async_remote_copy(src_ref=si_ref.at[dest], dst_ref=ri_ref.at[me],
                                                        send_sem=send_sems.at[q], recv_sem=recv_sems.at[q],
                                                        device_id=(cx, cy, c), device_id_type=MESH))
                cps.append(pltpu.make_async_remote_copy(src_ref=so_ref.at[:, dest], dst_ref=ro_ref.at[me],
                                                        send_sem=send_sems.at[3 + q], recv_sem=recv_sems.at[3 + q],
                                                        device_id=(cx, cy, c), device_id_type=MESH))
            for r in range(1, 8):
                cps.append(pltpu.make_async_remote_copy(src_ref=sm_ref, dst_ref=ga_ref.at[dev],
                                                        send_sem=send_sems.at[5 + r], recv_sem=recv_sems.at[5 + r],
                                                        device_id=peer(r), device_id_type=MESH))
            return cps

        own_small = pltpu.make_async_copy(sm_ref, ga_ref.at[dev], local_sem)

        @pl.when(jnp.logical_and(i == 0, k == 0))
        def _():
            for cp in sends():
                cp.start()
            own_small.start()

        @pl.when(k == 0)
        def _():
            o_ref[...] = gx_ref[...]

        o_ref[...] += _dot(dp_ref[...], w_ref[...], 1, 1)

        @pl.when(jnp.logical_and(i == ni - 1, k == nk - 1))
        def _():
            for q, (cx, cy) in enumerate(chips):
                frm = 2 * cx + cy
                pltpu.make_async_remote_copy(src_ref=si_ref.at[frm], dst_ref=ri_ref.at[frm], send_sem=send_sems.at[q],
                                             recv_sem=recv_sems.at[q], device_id=(x, y, c),
                                             device_id_type=MESH).wait_recv()
                pltpu.make_async_remote_copy(src_ref=so_ref.at[:, frm], dst_ref=ro_ref.at[frm],
                                             send_sem=send_sems.at[3 + q], recv_sem=recv_sems.at[3 + q],
                                             device_id=(x, y, c), device_id_type=MESH).wait_recv()
            for r in range(1, 8):
                px, py, pc = peer(r)
                pltpu.make_async_remote_copy(src_ref=sm_ref, dst_ref=ga_ref.at[4 * px + 2 * py + pc],
                                             send_sem=send_sems.at[5 + r], recv_sem=recv_sems.at[5 + r],
                                             device_id=(x, y, c), device_id_type=MESH).wait_recv()
            for cp in sends():
                cp.wait_send()
            own_small.wait()

    return pl.pallas_call(
        body, name="grad_x_and_scatter", grid=(ni, nk),
        in_specs=[pl.BlockSpec((None, tm, D_MODEL), lambda i, k: (_slot_of_seg(k), i, 0)),
                  pl.BlockSpec((D_MODEL, D_MODEL), lambda i, k: (0, k)),
                  pl.BlockSpec((tm, D_MODEL), lambda i, k: (i, 0)), ANY, ANY, ANY],
        out_specs=(pl.BlockSpec((tm, D_MODEL), lambda i, k: (i, 0)), ANY, ANY, ANY),
        out_shape=(jax.ShapeDtypeStruct((t, D_MODEL), F32),
                   jax.ShapeDtypeStruct((N_CHIPS, hd, sc), BF16),
                   jax.ShapeDtypeStruct((N_CHIPS, 3, hr, D_MODEL), BF16),
                   jax.ShapeDtypeStruct((8,) + small.shape, small.dtype)),
        scratch_shapes=[pltpu.SemaphoreType.DMA((13,)), pltpu.SemaphoreType.DMA((13,)), pltpu.SemaphoreType.DMA],
        input_output_aliases={2: 0},
        compiler_params=pltpu.CompilerParams(dimension_semantics=("arbitrary", "arbitrary"),
                                             vmem_limit_bytes=VMEM_LIMIT, has_side_effects=True),
    )(dp, w_full, gx, s_in, s_o, small)


def _input_grad_tail(dp, w_full, partial, tm):
    _, t, _ = dp.shape
    seg = N_SEG - 1

    def body(dp_ref, w_ref, part_ref, o_ref):
        o_ref[...] = part_ref[...] + _dot(dp_ref[...], w_ref[...], 1, 1)

    row = pl.BlockSpec((tm, D_MODEL), lambda i: (i, 0))
    return pl.pallas_call(
        body, name="grad_x_tail", grid=(t // tm,),
        in_specs=[pl.BlockSpec((None, tm, D_MODEL), lambda i: (SLOT_OF_SEG[seg], i, 0)),
                  pl.BlockSpec((D_MODEL, D_MODEL), lambda i: (0, seg)), row],
        out_specs=row,
        out_shape=jax.ShapeDtypeStruct((t, D_MODEL), F32),
        compiler_params=_params("parallel"),
    )(dp, w_full, partial)


def _sgu_chunk_forward(u, v, z, wm, bs, lng, lnb):
    ug, dug = _gelu_and_grad(u)
    vg, dvg = _gelu_and_grad(v)
    mu = jnp.mean(vg, axis=-1, keepdims=True)
    xc = vg - mu
    var = jnp.mean(xc * xc, axis=-1, keepdims=True)
    rstd = lax.rsqrt(var + LN_EPS)
    vhat = xc * rstd
    vln = (vhat * lng + lnb).astype(BF16)
    mixed = _dot(wm, vln) + bs
    sig = _sigmoid(z)
    return ug, dug, dvg, rstd, vhat, vln, mixed, sig


def _mixer_a_forward(p_a, wm, bs_col, ln_v_g, ln_v_b, tm):
    t = p_a.shape[1]

    def body(p_ref, wm_ref, bs_ref, g_ref, b_ref, o_ref):
        wm_v, bs_v, lng, lnb = wm_ref[...], bs_ref[...], g_ref[...], b_ref[...]

        def chunk(ci, carry):
            rows = pl.ds(pl.multiple_of(ci * CHUNK, CHUNK), CHUNK)
            u = p_ref[0, rows, :].astype(F32)
            v = p_ref[1, rows, :].astype(F32)
            z = p_ref[2, rows, :].astype(F32)
            ug, _, _, _, _, _, mixed, sig = _sgu_chunk_forward(u, v, z, wm_v, bs_v, lng, lnb)
            o_ref[rows, :] = (ug * mixed * (z * sig)).astype(BF16)
            return carry

        lax.fori_loop(0, tm // CHUNK, chunk, 0, unroll=True)

    return pl.pallas_call(
        body, name="mixer_a_forward", grid=(t // tm, N_HEADS),
        in_specs=[pl.BlockSpec((3, tm, HEAD_DIM), lambda i, h: (0, i, h)),
                  pl.BlockSpec((None, CHUNK, CHUNK), lambda i, h: (h, 0, 0)),
                  pl.BlockSpec((None, CHUNK, 1), lambda i, h: (h, 0, 0)),
                  pl.BlockSpec((1, HEAD_DIM), lambda i, h: (0, h)),
                  pl.BlockSpec((1, HEAD_DIM), lambda i, h: (0, h))],
        out_specs=pl.BlockSpec((tm, HEAD_DIM), lambda i, h: (i, h)),
        out_shape=jax.ShapeDtypeStruct((t, D_MODEL), BF16),
        compiler_params=_params("parallel", "parallel"),
    )(p_a, wm, bs_col, ln_v_g, ln_v_b)


def _mixer_a_backward(p_a, dyab, wm, bs_col, ln_v_g, ln_v_b, dp, tm):
    t = p_a.shape[1]

    def body(p_ref, dy_ref, wm_ref, bs_ref, g_ref, b_ref, dp_in, dp_ref, dws_ref, dbs_ref, dg_ref, db_ref):
        @pl.when(pl.program_id(1) == 0)
        def _():
            dws_ref[...] = jnp.zeros_like(dws_ref)
            dbs_ref[...] = jnp.zeros_like(dbs_ref)
            dg_ref[...] = jnp.zeros_like(dg_ref)
            db_ref[...] = jnp.zeros_like(db_ref)

        wm_v, bs_v, lng, lnb = wm_ref[...], bs_ref[...], g_ref[...], b_ref[...]
        causal = (lax.broadcasted_iota(jnp.int32, (CHUNK, CHUNK), 1)
                  <= lax.broadcasted_iota(jnp.int32, (CHUNK, CHUNK), 0))

        def chunk(ci, carry):
            rows = pl.ds(pl.multiple_of(ci * CHUNK, CHUNK), CHUNK)
            u = p_ref[0, rows, :].astype(F32)
            v = p_ref[1, rows, :].astype(F32)
            z = p_ref[2, rows, :].astype(F32)
            dy = dy_ref[rows, :].astype(F32)
            ug, dug, dvg, rstd, vhat, vln, mixed, sig = _sgu_chunk_forward(u, v, z, wm_v, bs_v, lng, lnb)
            sz = z * sig
            dmixed = dy * ug * sz
            dp_ref[0, rows, :] = (dy * mixed * sz * dug).astype(BF16)
            dp_ref[2, rows, :] = (dy * ug * mixed * (sig * (1.0 + z * (1.0 - sig)))).astype(BF16)
            dbs_ref[...] += jnp.sum(dmixed, axis=1, keepdims=True)
            dmb = dmixed.astype(BF16)
            dws_ref[...] += jnp.where(causal, _dot(dmb, vln, 1, 1), 0.0)
            dvln = _dot(wm_v, dmb, 0, 0)
            db_ref[...] += jnp.sum(dvln, axis=0, keepdims=True)
            dg_ref[...] += jnp.sum(dvln * vhat, axis=0, keepdims=True)
            dvh = dvln * lng
            m1 = jnp.mean(dvh, axis=-1, keepdims=True)
            m2 = jnp.mean(dvh * vhat, axis=-1, keepdims=True)
            dp_ref[1, rows, :] = (rstd * (dvh - m1 - vhat * m2) * dvg).astype(BF16)
            return carry

        lax.fori_loop(0, tm // CHUNK, chunk, 0, unroll=True)

    return pl.pallas_call(
        body, name="mixer_a_backward", grid=(N_HEADS, t // tm),
        in_specs=[pl.BlockSpec((3, tm, HEAD_DIM), lambda h, i: (0, i, h)),
                  pl.BlockSpec((None, tm, HEAD_DIM), lambda h, i: (0, i, h)),
                  pl.BlockSpec((None, CHUNK, CHUNK), lambda h, i: (h, 0, 0)),
                  pl.BlockSpec((None, CHUNK, 1), lambda h, i: (h, 0, 0)),
                  pl.BlockSpec((1, HEAD_DIM), lambda h, i: (0, h)),
                  pl.BlockSpec((1, HEAD_DIM), lambda h, i: (0, h)), ANY],
        out_specs=(pl.BlockSpec((3, tm, HEAD_DIM), lambda h, i: (BLOCK_A, i, h)),
                   pl.BlockSpec((None, CHUNK, CHUNK), lambda h, i: (h, 0, 0)),
                   pl.BlockSpec((None, CHUNK, 1), lambda h, i: (h, 0, 0)),
                   pl.BlockSpec((1, HEAD_DIM), lambda h, i: (0, h)),
                   pl.BlockSpec((1, HEAD_DIM), lambda h, i: (0, h))),
        out_shape=(jax.ShapeDtypeStruct(dp.shape, BF16),
                   jax.ShapeDtypeStruct((N_HEADS, CHUNK, CHUNK), F32),
                   jax.ShapeDtypeStruct((N_HEADS, CHUNK, 1), F32),
                   jax.ShapeDtypeStruct((1, D_MODEL), F32), jax.ShapeDtypeStruct((1, D_MODEL), F32)),
        input_output_aliases={6: 0},
        compiler_params=_params("parallel", "arbitrary"),
    )(p_a, dyab, wm, bs_col, ln_v_g, ln_v_b, dp)


HALO = 16


def _conv_taps(h, halo_h, tm):
    row = lax.broadcasted_iota(jnp.int32, h.shape, 0)
    last1 = halo_h[HALO - 1:HALO, :]
    last2 = halo_h[HALO - 2:HALO - 1, :]
    h1 = jnp.where(row == 0, last1, pltpu.roll(h, 1, 0))
    h2 = jnp.where(row == 0, last2, jnp.where(row == 1, last1, pltpu.roll(h, 2, 0)))
    return h1, h2


def _mixer_b_forward(p_b, conv_w, conv_b, tm, tc):
    t = p_b.shape[1]

    def body(p_ref, halo_ref, w_ref, b_ref, o_ref):
        valid = (pl.program_id(1) > 0).astype(F32)
        h = p_ref[1].astype(F32) * p_ref[0].astype(F32)
        halo_h = halo_ref[1].astype(F32) * halo_ref[0].astype(F32) * valid
        h1, h2 = _conv_taps(h, halo_h, tm)
        w = w_ref[...]
        conv = b_ref[...] + w[0:1, :] * h2 + w[1:2, :] * h1 + w[2:3, :] * h
        z = p_ref[3].astype(F32)
        o_ref[...] = (p_ref[2].astype(F32) * conv * (z * _sigmoid(z))).astype(BF16)

    steps = tm // HALO
    return pl.pallas_call(
        body, name="mixer_b_forward", grid=(D_MODEL // tc, t // tm),
        in_specs=[pl.BlockSpec((4, tm, tc), lambda j, i: (BLOCK_B, i, j)),
                  pl.BlockSpec((4, HALO, tc), lambda j, i: (BLOCK_B, jnp.maximum(i * steps - 1, 0), j)),
                  pl.BlockSpec((3, tc), lambda j, i: (0, j)),
                  pl.BlockSpec((1, tc), lambda j, i: (0, j))],
        out_specs=pl.BlockSpec((tm, tc), lambda j, i: (i, j)),
        out_shape=jax.ShapeDtypeStruct((t, D_MODEL), BF16),
        compiler_params=_params("parallel", "parallel"),
    )(p_b, p_b, conv_w, conv_b)


def _mixer_b_backward(p_b, dyab, conv_w, conv_b, dp, tm, tc):
    t = p_b.shape[1]
    n = t // tm

    def body(p_ref, halo_ref, dy_ref, w_ref, b_ref, dp_in, dp_ref, dw_ref, db_ref, next_ref):
        ii = pl.program_id(1)

        @pl.when(ii == 0)
        def _():
            dw_ref[...] = jnp.zeros_like(dw_ref)
            db_ref[...] = jnp.zeros_like(db_ref)
            next_ref[...] = jnp.zeros_like(next_ref)

        valid = (ii < n - 1).astype(F32)
        xb = p_ref[0].astype(F32)
        cb = p_ref[1].astype(F32)
        bb = p_ref[2].astype(F32)
        z = p_ref[3].astype(F32)
        h = cb * xb
        halo_h = halo_ref[1].astype(F32) * halo_ref[0].astype(F32) * valid
        h1, h2 = _conv_taps(h, halo_h, tm)
        w = w_ref[...]
        w0, w1, w2 = w[0:1, :], w[1:2, :], w[2:3, :]
        conv = b_ref[...] + w0 * h2 + w1 * h1 + w2 * h
        sig = _sigmoid(z)
        sz = z * sig
        dy = dy_ref[...].astype(F32)
        dconv = dy * bb * sz
        dp_ref[2] = (dy * conv * sz).astype(BF16)
        dp_ref[3] = (dy * bb * conv * (sig * (1.0 + z * (1.0 - sig)))).astype(BF16)
        db_ref[...] += jnp.sum(dconv, axis=0, keepdims=True)
        dw_ref[0:1, :] += jnp.sum(dconv * h2, axis=0, keepdims=True)
        dw_ref[1:2, :] += jnp.sum(dconv * h1, axis=0, keepdims=True)
        dw_ref[2:3, :] += jnp.sum(dconv * h, axis=0, keepdims=True)
        row = lax.broadcasted_iota(jnp.int32, h.shape, 0)
        nxt = next_ref[...]
        n0, n1 = nxt[0:1, :], nxt[1:2, :]
        d1 = jnp.where(row == tm - 1, n0, pltpu.roll(dconv, tm - 1, 0))
        d2 = jnp.where(row == tm - 1, n1, jnp.where(row == tm - 2, n0, pltpu.roll(dconv, tm - 2, 0)))
        dh = w2 * dconv + w1 * d1 + w0 * d2
        dp_ref[0] = (dh * cb).astype(BF16)
        dp_ref[1] = (dh * xb).astype(BF16)
        next_ref[...] = dconv[0:8, :]

    steps = tm // HALO
    return pl.pallas_call(
        body, name="mixer_b_backward", grid=(D_MODEL // tc, n),
        in_specs=[pl.BlockSpec((4, tm, tc), lambda j, ii: (BLOCK_B, n - 1 - ii, j)),
                  pl.BlockSpec((4, HALO, tc), lambda j, ii: (BLOCK_B, jnp.maximum((n - 1 - ii) * steps - 1, 0), j)),
                  pl.BlockSpec((None, tm, tc), lambda j, ii: (1, n - 1 - ii, j)),
                  pl.BlockSpec((3, tc), lambda j, ii: (0, j)),
                  pl.BlockSpec((1, tc), lambda j, ii: (0, j)), ANY],
        out_specs=(pl.BlockSpec((4, tm, tc), lambda j, ii: (BLOCK_B, n - 1 - ii, j)),
                   pl.BlockSpec((3, tc), lambda j, ii: (0, j)),
                   pl.BlockSpec((1, tc), lambda j, ii: (0, j))),
        out_shape=(jax.ShapeDtypeStruct(dp.shape, BF16),
                   jax.ShapeDtypeStruct((3, D_MODEL), F32), jax.ShapeDtypeStruct((1, D_MODEL), F32)),
        scratch_shapes=[pltpu.VMEM((8, tc), F32)],
        input_output_aliases={5: 0},
        compiler_params=_params("parallel", "arbitrary"),
    )(p_b, p_b, dyab, conv_w, conv_b, dp)


def _adam_math(w, g, m, v):
    m = ADAM_B1 * m + (1.0 - ADAM_B1) * g
    v = ADAM_B2 * v + (1.0 - ADAM_B2) * (g * g)
    delta = -ADAM_LR * ((m * ADAM_C1) / (jnp.sqrt(v * ADAM_C2) + ADAM_EPS) + ADAM_WD * w)
    return delta, m, v


def _adam_rows(w, g, m, v, tm, name, g_sel=None):
    r, c = w.shape

    def body(w_ref, g_ref, m_ref, v_ref, go_ref, d_ref, mo_ref, vo_ref):
        g = g_ref[...]
        d, mn, vn = _adam_math(w_ref[...], g, m_ref[...], v_ref[...])
        go_ref[...] = g
        d_ref[...] = d
        mo_ref[...] = mn
        vo_ref[...] = vn

    spec = pl.BlockSpec((tm, c), lambda i: (i, 0))
    g_spec = spec if g_sel is None else pl.BlockSpec((None, tm, c), lambda i: (g_sel, i, 0))
    shape = jax.ShapeDtypeStruct((r, c), F32)
    return pl.pallas_call(
        body, name=name, grid=(r // tm,),
        in_specs=[spec, g_spec, spec, spec], out_specs=(spec,) * 4, out_shape=(shape,) * 4,
        compiler_params=_params("parallel"),
    )(w, g, m, v)


SMALL_ROW0 = {name: sum(r for _, r in SMALL_ROWS[:i]) for i, (name, _) in enumerate(SMALL_ROWS)}
LANE_MAJOR = ("ln_g", "ln_b", "b_gate", "ln_v_g", "ln_v_b", "conv_b")


def _lane_pieces(n):
    return [(q, slice(q * 128, (q + 1) * 128)) for q in range(n // 128)]


def _pack_small(d_ln_g, d_ln_b, d_bg, d_lnv_g, d_lnv_b, d_ws, d_bs, d_cw, d_cb, loss_part):
    def body(lg, lb, bg, vg, vb, ws, bs, cw, cb, loss, o_ref):
        def put(row0, vec):
            for q, cols in _lane_pieces(vec.shape[1]):
                o_ref[row0 + q:row0 + q + 1, :] = vec[:, cols]

        put(SMALL_ROW0["ln_g"], lg[...])
        put(SMALL_ROW0["ln_b"], lb[...])
        for n in range(2):
            put(SMALL_ROW0["b_gate"] + n * (D_MODEL // 128), bg[n])
        put(SMALL_ROW0["ln_v_g"], vg[...])
        put(SMALL_ROW0["ln_v_b"], vb[...])
        for h in range(N_HEADS):
            o_ref[SMALL_ROW0["w_s"] + h * CHUNK:SMALL_ROW0["w_s"] + (h + 1) * CHUNK, :] = ws[h]
        o_ref[SMALL_ROW0["b_s"]:SMALL_ROW0["b_s"] + N_HEADS, :] = bs[...]
        for c in range(3):
            put(SMALL_ROW0["conv_w"] + c * (D_MODEL // 128), cw[c:c + 1, :])
        put(SMALL_ROW0["conv_b"], cb[...])
        o_ref[SMALL_ROW0["loss"]:SMALL_ROW0["loss"] + 8, :] = jnp.broadcast_to(loss[...], (8, 128))

    return pl.pallas_call(
        body, name="pack_small", out_shape=jax.ShapeDtypeStruct((SMALL_TOTAL, 128), F32), compiler_params=_params(),
    )(d_ln_g, d_ln_b, d_bg, d_lnv_g, d_lnv_b, d_ws, d_bs.reshape(N_HEADS, CHUNK), d_cw, d_cb, loss_part)


def _adam_small(gathered, params):
    names = list(params)
    flat = [a for n in names for a in params[n]]

    def body(*refs):
        ga_ref = refs[0]
        ins = refs[1:1 + 3 * len(names)]
        outs = refs[1 + 3 * len(names):-1]
        gs_ref = refs[-1]
        g = ga_ref[0]
        for k in range(1, 8):
            g = g + ga_ref[k]
        gs_ref[...] = g
        for i, name in enumerate(names):
            w_ref, m_ref, v_ref = ins[3 * i:3 * i + 3]
            o_refs = outs[4 * i:4 * i + 4]
            row0 = SMALL_ROW0[name]
            if name in LANE_MAJOR:
                pieces = [((slice(None), cols), slice(row0 + q, row0 + q + 1))
                          for q, cols in _lane_pieces(w_ref.shape[1])]
            elif name == "w_s":
                pieces = [((0, h), slice(row0 + h * CHUNK, row0 + (h + 1) * CHUNK)) for h in range(N_HEADS)]
            else:
                pieces = [((0,), slice(row0, row0 + N_HEADS))]
            for idx, rows in pieces:
                gp = gs_ref[rows, :]
                res = (gp,) + _adam_math(w_ref[idx], gp, m_ref[idx], v_ref[idx])
                for o_ref, val in zip(o_refs, res):
                    o_ref[idx] = val
        gcw_ref, loss_ref = outs[-2:]
        for c in range(3):
            for q, cols in _lane_pieces(D_MODEL):
                r = SMALL_ROW0["conv_w"] + c * (D_MODEL // 128) + q
                gcw_ref[c:c + 1, cols] = gs_ref[r:r + 1, :]
        loss_ref[...] = gs_ref[SMALL_ROW0["loss"]:SMALL_ROW0["loss"] + 1, :]

    out_shape = [jax.ShapeDtypeStruct(params[n][0].shape, F32) for n in names for _ in range(4)]
    out_shape += [jax.ShapeDtypeStruct((3, D_MODEL), F32), jax.ShapeDtypeStruct((1, 128), F32)]
    res = pl.pallas_call(
        body, name="adam_small", out_shape=tuple(out_shape),
        scratch_shapes=[pltpu.VMEM((SMALL_TOTAL, 128), F32)], compiler_params=_params(),
    )(gathered, *flat)
    return {n: res[4 * i:4 * i + 4] for i, n in enumerate(names)}, res[-2], res[-1]


def _adam_conv_w(g_all, chip, w, m, v):
    cols = w.shape[2]

    def body(c_ref, g_ref, w_ref, m_ref, v_ref, go_ref, d_ref, mo_ref, vo_ref):
        g = g_ref[...]
        d, mn, vn = _adam_math(w_ref[...], g, m_ref[...], v_ref[...])
        go_ref[...] = g
        d_ref[...] = d
        mo_ref[...] = mn
        vo_ref[...] = vn

    own = pl.BlockSpec((None, 3, cols), lambda i, c_ref: (0, 0, 0))
    return pl.pallas_call(
        body, name="adam_conv_w",
        grid_spec=pltpu.PrefetchScalarGridSpec(
            num_scalar_prefetch=1, grid=(1,),
            in_specs=[pl.BlockSpec((3, cols), lambda i, c_ref: (0, c_ref[0])), own, own, own],
            out_specs=(own,) * 4),
        out_shape=(jax.ShapeDtypeStruct(w.shape, F32),) * 4,
        compiler_params=_params("arbitrary"),
    )(chip, g_all, w, m, v)


def kernel(x, w_in, b_gate, ln_v_g, ln_v_b, w_s, b_s, conv_w, conv_b, w_oa, w_ob, w_out, ln_g, ln_b, loss_target, m_w_in, m_b_gate, m_ln_v_g, m_ln_v_b, m_w_s, m_b_s, m_conv_w, m_conv_b, m_w_oa, m_w_ob, m_w_out, m_ln_g, m_ln_b, v_w_in, v_b_gate, v_ln_v_g, v_ln_v_b, v_w_s, v_b_s, v_conv_w, v_conv_b, v_w_oa, v_w_ob, v_w_out, v_ln_g, v_ln_b):
    t = x.shape[1]
    x2 = x[0]
    target = loss_target[0]
    chip = 2 * lax.axis_index("x") + lax.axis_index("y")
    conv_cols = conv_w.shape[2]

    chip1 = chip.astype(jnp.int32).reshape(1)
    w_pre = _cast_into_columns(w_in[0], chip1, N_CHIPS, 256, "cast_w_in")
    wo_b, wm = _prep_small_weights(w_oa[0], w_ob[0], w_out[0], w_s[0])
    conv_w8 = jnp.concatenate([conv_w[0], jnp.zeros((5, conv_cols), F32)], axis=0)
    bs_col = b_s[0].reshape(N_HEADS, CHUNK, 1)
    bg = b_gate.reshape(2, 1, D_MODEL)

    xb, xt = _cast_and_transpose(x2, min(512, t))
    p, w_full, wo_full, cw_full = _gather_and_project(xb, w_pre, wo_b, conv_w8, min(4096, t))
    wo3 = wo_full.reshape(3, D_MODEL, D_MODEL)
    conv_w_all = jnp.transpose(cw_full[:, :3, :], (1, 0, 2)).reshape(3, D_MODEL)
    tm_a = min(512, t)
    ya = _mixer_a_forward(p, wm, bs_col, ln_v_g, ln_v_b, tm_a)
    tm_b = min(512, t)
    yb = _mixer_b_forward(p, conv_w_all, conv_b, tm_b, 512)
    tm_m = min(1024, t)
    merged, oab = _merge_forward(ya, yb, wo3, p, bg, tm_m, 512)

    drb, gx, dmerged, d_ln_g, d_ln_b, loss_part = _head(merged, wo3, x2, target, ln_g, ln_b, min(512, t))
    doab, dp, d_bg, dyab = _gate_and_branch_backward(dmerged, oab, p, bg, wo3, min(256, t))
    dp, d_ws, d_bs, d_lnv_g, d_lnv_b = _mixer_a_backward(p, dyab, wm, bs_col, ln_v_g, ln_v_b, dp, tm_a)
    dp, d_cw, d_cb = _mixer_b_backward(p, dyab, conv_w_all, conv_b, dp, tm_b, 512)

    tk = min(2048, t)
    g_in = _win_grad(xt, dp, 1024, tk)
    g_oa = _weight_grad(ya, 0, doab, 0, 1024, 1024, tk, "grad_w_oa")
    g_ob = _weight_grad(yb, 0, doab, 1, 1024, 1024, tk, "grad_w_ob")
    g_out, r_in = _weight_grad(merged, 0, drb, 0, 1024, 1024, tk, "grad_w_out", exchange=g_in)
    r_o = _exchange_halves((g_oa, g_ob, g_out))
    s_in, s_o = _add_halves(g_in, (g_oa, g_ob, g_out), r_in, r_o)
    small_part = _pack_small(d_ln_g, d_ln_b, d_bg, d_lnv_g, d_lnv_b, d_ws, d_bs, d_cw, d_cb, loss_part)
    tm_x = min(512, t)
    gx, q_in, q_o, gathered = _input_grad_and_scatter(dp, w_full, gx, s_in, s_o, small_part, tm_x)
    grad_x = _input_grad_tail(dp, w_full, gx, tm_x)
    f_in, f_o = _sum_chips(q_in, q_o, s_in, s_o)
    gsum_in, gsum_o = _share_halves(f_in, f_o)

    big = {}
    big["w_in"] = _adam_rows(w_in[0], gsum_in, m_w_in[0], v_w_in[0], 128, "adam_w_in")
    for n, (name, w, m, v) in enumerate((("w_oa", w_oa, m_w_oa, v_w_oa), ("w_ob", w_ob, m_w_ob, v_w_ob),
                                         ("w_out", w_out, m_w_out, v_w_out))):
        big[name] = _adam_rows(w[0], gsum_o, m[0], v[0], 256, "adam_" + name, g_sel=n)

    small, g_conv_w, loss_row = _adam_small(gathered, {
        "ln_g": (ln_g, m_ln_g, v_ln_g), "ln_b": (ln_b, m_ln_b, v_ln_b), "b_gate": (b_gate, m_b_gate, v_b_gate),
        "ln_v_g": (ln_v_g, m_ln_v_g, v_ln_v_g), "ln_v_b": (ln_v_b, m_ln_v_b, v_ln_v_b),
        "w_s": (w_s, m_w_s, v_w_s), "b_s": (b_s, m_b_s, v_b_s), "conv_b": (conv_b, m_conv_b, v_conv_b)})
    small["conv_w"] = _adam_conv_w(g_conv_w, chip1, conv_w, m_conv_w, v_conv_w)
    loss = loss_row[0, 0]

    order = ("w_in", "b_gate", "ln_v_g", "ln_v_b", "w_s", "b_s", "conv_w", "conv_b", "w_oa", "w_ob", "w_out",
             "ln_g", "ln_b")
    outs = [loss, grad_x[None]]
    for which in range(4):
        for name in order:
            outs.append(big[name][which][None] if name in big else small[name][which])
    return tuple(outs)
```

```python
import functools
import math

import jax
import jax.numpy as jnp
from jax import lax
from jax.experimental import pallas as pl
from jax.experimental.pallas import tpu as pltpu

F32 = jnp.float32
BF16 = jnp.bfloat16

D_MODEL = 2048
N_HEADS = 8
HEAD_DIM = D_MODEL // N_HEADS
CHUNK = 128
N_SEG = 9
N_CHIPS = 4
SHARD_COLS = N_SEG * D_MODEL // N_CHIPS
COL_BLOCK = 512
BLOCKS_PER_SHARD = SHARD_COLS // COL_BLOCK
BLOCKS_PER_SEG = D_MODEL // COL_BLOCK
SUBS = 3
SHARD_ROWS = D_MODEL // N_CHIPS
N_SLOTS = 12
BLOCK_A, BLOCK_G, BLOCK_B = 0, 2, 2
SLOT_OF_SEG = (0, 1, 2, 8, 9, 10, 11, 4, 5)
DN_ALPHA = 2.0 ** 0.25
LN_EPS = 1e-5
GELU_K = math.sqrt(2.0 / math.pi)
GELU_C = 0.044715

ADAM_LR = 0.001
ADAM_B1 = 0.9
ADAM_B2 = 0.999
ADAM_EPS = 1e-08
ADAM_WD = 0.01
ADAM_STEP = 10
ADAM_C1 = 1.0 / (1.0 - ADAM_B1 ** ADAM_STEP)
ADAM_C2 = 1.0 / (1.0 - ADAM_B2 ** ADAM_STEP)

VMEM_LIMIT = 60 * 1024 * 1024
MESH = pl.DeviceIdType.MESH
ANY = pl.BlockSpec(memory_space=pl.ANY)

SMALL_ROWS = (("ln_g", 16), ("ln_b", 16), ("b_gate", 32), ("ln_v_g", 16), ("ln_v_b", 16),
              ("w_s", 1024), ("b_s", 8), ("conv_w", 48), ("conv_b", 16), ("loss", 8))
SMALL_TOTAL = sum(r for _, r in SMALL_ROWS)


def _params(*sem):
    return pltpu.CompilerParams(dimension_semantics=sem, vmem_limit_bytes=VMEM_LIMIT)


def _sigmoid(x):
    return 1.0 / (1.0 + jnp.exp(-x))


def _gelu_gate(x, x2):
    return 1.0 / (1.0 + jnp.exp(x * ((-2.0 * GELU_K) + (-2.0 * GELU_K * GELU_C) * x2)))


def _gelu_and_grad(x):
    x2 = x * x
    s = _gelu_gate(x, x2)
    g = x * s
    dg = s + g * (1.0 - s) * ((2.0 * GELU_K) + (6.0 * GELU_K * GELU_C) * x2)
    return g, dg


def _silu_grad(sig, sz):
    return sig + sz * (1.0 - sig)


def _dot(a, b, ca=1, cb=0):
    return lax.dot_general(a, b, (((ca,), (cb,)), ((), ())), preferred_element_type=F32)


def _cast_and_transpose(x, tm):
    t, d = x.shape

    def body(x_ref, o_ref, ot_ref):
        v = x_ref[...]
        o_ref[...] = v.astype(BF16)
        ot_ref[...] = v.T.astype(BF16)

    return pl.pallas_call(
        body, name="cast_x", grid=(t // tm,),
        in_specs=[pl.BlockSpec((tm, d), lambda i: (i, 0))],
        out_specs=(pl.BlockSpec((tm, d), lambda i: (i, 0)), pl.BlockSpec((d, tm), lambda i: (0, i))),
        out_shape=(jax.ShapeDtypeStruct((t, d), BF16), jax.ShapeDtypeStruct((d, t), BF16)),
        compiler_params=_params("parallel"),
    )(x)


def _cast_into_columns(w, slot, n_slots, tm, name):
    r, c = w.shape

    def body(s_ref, w_ref, o_ref):
        o_ref[...] = w_ref[...].astype(BF16)

    return pl.pallas_call(
        body, name=name,
        grid_spec=pltpu.PrefetchScalarGridSpec(
            num_scalar_prefetch=1, grid=(r // tm,),
            in_specs=[pl.BlockSpec((tm, c), lambda i, s_ref: (i, 0))],
            out_specs=pl.BlockSpec((tm, c), lambda i, s_ref: (i, s_ref[0]))),
        out_shape=jax.ShapeDtypeStruct((r, n_slots * c), BF16),
        compiler_params=_params("parallel"),
    )(slot, w)


def _prep_small_weights(w_oa, w_ob, w_out, w_s):
    rows = w_oa.shape[0]

    def body(a_ref, b_ref, c_ref, ws_ref, wo_ref, wm_ref):
        wo_ref[0] = a_ref[...].astype(BF16)
        wo_ref[1] = b_ref[...].astype(BF16)
        wo_ref[2] = c_ref[...].astype(BF16)
        t = lax.broadcasted_iota(jnp.int32, (CHUNK, CHUNK), 0)
        s = lax.broadcasted_iota(jnp.int32, (CHUNK, CHUNK), 1)
        for h in range(N_HEADS):
            wm_ref[h] = jnp.where(s <= t, ws_ref[h], 0.0).astype(BF16)

    return pl.pallas_call(
        body, name="prep_small_weights",
        out_shape=(jax.ShapeDtypeStruct((3, rows, D_MODEL), BF16),
                   jax.ShapeDtypeStruct((N_HEADS, CHUNK, CHUNK), BF16)),
        compiler_params=_params(),
    )(w_oa, w_ob, w_out, w_s)


def _mesh_pos():
    x, y, c = lax.axis_index("x"), lax.axis_index("y"), lax.axis_index("c")
    chips = [(1 - x, y), (x, 1 - y), (1 - x, 1 - y)]
    return x, y, c, chips


def _slot_of_seg(seg):
    return jnp.where(seg < 3, seg, jnp.where(seg < 7, seg + 5, seg - 3))


def _gather_and_project(xb, w_pre, wo_b, conv_w8, tm):
    t = xb.shape[0]
    d, sc = w_pre.shape[0], w_pre.shape[1] // N_CHIPS
    rows = wo_b.shape[1]
    hd, hr = d // 2, rows // 2
    nj = BLOCKS_PER_SHARD // SUBS
    pc = nj * COL_BLOCK
    units = N_CHIPS * SUBS
    ni = t // tm
    total = units * ni * nj
    mx, my = lax.axis_index("x"), lax.axis_index("y")
    order = jnp.stack([2 * mx + my, 2 * (1 - mx) + my, 2 * mx + (1 - my),
                       2 * (1 - mx) + (1 - my)]).astype(jnp.int32)

    def body(order_ref, x_ref, wpre_ref, wo_ref, cw_ref, p_ref, wf_ref, wof_ref, cwf_ref,
             wbuf, wsem, xbuf, xsem, send_sems, recv_sems, local_sems):
        x, y, c, chips = _mesh_pos()
        me = 2 * x + y
        sibling = (x, y, 1 - c)
        u, i, j = pl.program_id(0), pl.program_id(1), pl.program_id(2)
        n = (u * ni + i) * nj + j
        m = u * ni + i

        def rows_start(m_):
            pltpu.make_async_copy(x_ref.at[pl.ds(pl.multiple_of(lax.rem(m_, ni) * tm, tm), tm), :],
                                  xbuf.at[lax.rem(m_, 2)], xsem.at[lax.rem(m_, 2)]).start()

        def chip_of(q):
            return 2 * chips[q][0] + chips[q][1]

        def piece(ref, k, cc, r):
            return ref.at[pl.ds(cc * hd, hd), pl.ds(pl.multiple_of(k * sc + r * pc, COL_BLOCK), pc)]

        def wo_half(ref4, k, cc):
            return ref4.at[:, k, pl.ds(cc * hr, hr), :]

        def rcopy(sem, src, dst, to):
            return pltpu.make_async_remote_copy(src_ref=src, dst_ref=dst, send_sem=send_sems.at[sem],
                                                recv_sem=recv_sems.at[sem], device_id=to, device_id_type=MESH)

        def w_send(q, r):
            return rcopy(q * SUBS + r, piece(wpre_ref, me, c, r), piece(wf_ref, me, c, r), (*chips[q], c))

        def w_landed(q, r):
            return rcopy(q * SUBS + r, piece(wpre_ref, me, c, r), piece(wf_ref, chip_of(q), c, r), sibling)

        def w_forward(q, r, cc):
            ref = piece(wf_ref, chip_of(q), cc, r)
            return rcopy(9 + q * SUBS + r, ref, ref, sibling)

        def wo_send(q):
            return rcopy(18 + q, wo_ref.at[:, pl.ds(c * hr, hr), :], wo_half(wof_ref, me, c), (*chips[q], c))

        def wo_landed(q):
            return rcopy(18 + q, wo_ref.at[:, pl.ds(c * hr, hr), :], wo_half(wof_ref, chip_of(q), c), sibling)

        def wo_forward(q, cc):
            ref = wo_half(wof_ref, chip_of(q), cc)
            return rcopy(21 + q, ref, ref, sibling)

        def conv_send(q):
            return rcopy(24 + q, cw_ref, cwf_ref.at[me], (*chips[q], c))

        def local_copies():
            return [pltpu.make_async_copy(wo_ref, wof_ref.at[:, me], local_sems.at[0]),
                    pltpu.make_async_copy(cw_ref, cwf_ref.at[me], local_sems.at[1])]

        def tile_start(u_, j_, slot):
            g = order_ref[u_ // SUBS] * BLOCKS_PER_SHARD + lax.rem(u_, SUBS) * nj + j_
            cols = pl.ds(pl.multiple_of(g * COL_BLOCK, COL_BLOCK), COL_BLOCK)

            @pl.when(u_ < SUBS)
            def _():
                pltpu.make_async_copy(wpre_ref.at[:, cols], wbuf.at[slot], wsem.at[slot]).start()

            @pl.when(u_ >= SUBS)
            def _():
                pltpu.make_async_copy(wf_ref.at[:, cols], wbuf.at[slot], wsem.at[slot]).start()

        def end_of(u_):
            return jnp.logical_and(u == u_, jnp.logical_and(i == ni - 1, j == nj - 1))

        @pl.when(n == 0)
        def _():
            for cp in local_copies():
                cp.start()
            for r in range(SUBS):
                for q in (0, 1):
                    w_send(q, r).start()
            for q in range(3):
                conv_send(q).start()
            tile_start(0, 0, 0)
            rows_start(0)

        def pass_on(q, r):
            w_landed(q, r).wait_recv()
            w_forward(q, r, c).start()

        for u_ in range(1, units - 1):
            @pl.when(end_of(u_))
            def _(u_=u_):
                if u_ <= SUBS:
                    for q in (0, 1):
                        pass_on(q, u_ - 1)
                if u_ == SUBS:
                    for r in range(SUBS):
                        for q in (0, 1):
                            w_send(q, r).wait_send()
                    for r in range(SUBS):
                        w_send(2, r).start()
                if 2 * SUBS - 1 <= u_ <= 3 * SUBS - 2:
                    pass_on(2, u_ - (2 * SUBS - 1))
                if u_ == 3 * SUBS - 2:
                    for r in range(SUBS):
                        w_send(2, r).wait_send()
                    for q in range(3):
                        wo_send(q).start()
                nxt = u_ + 1
                if nxt >= SUBS:
                    w_forward(nxt // SUBS - 1, nxt % SUBS, 1 - c).wait_recv()

        n1 = n + 1

        @pl.when(n1 < total)
        def _():
            tile_start(n1 // (ni * nj), lax.rem(n1, nj), lax.rem(n1, 2))

        xslot = lax.rem(m, 2)

        @pl.when(j == 0)
        def _():
            @pl.when(m + 1 < units * ni)
            def _():
                rows_start(m + 1)

            pltpu.make_async_copy(x_ref.at[pl.ds(0, tm), :], xbuf.at[xslot], xsem.at[xslot]).wait()

        slot = lax.rem(n, 2)
        pltpu.make_async_copy(wpre_ref.at[:, pl.ds(0, COL_BLOCK)], wbuf.at[slot], wsem.at[slot]).wait()
        p_ref[...] = _dot(xbuf[xslot], wbuf[slot]).astype(BF16)

        @pl.when(n == total - 1)
        def _():
            for q in range(3):
                wo_landed(q).wait_recv()
                wo_forward(q, c).start()
            for q in range(3):
                wo_forward(q, 1 - c).wait_recv()
                rcopy(24 + q, cw_ref, cwf_ref.at[chip_of(q)], sibling).wait_recv()
            for q in range(3):
                for r in range(SUBS):
                    w_forward(q, r, c).wait_send()
                wo_send(q).wait_send()
                wo_forward(q, c).wait_send()
                conv_send(q).wait_send()
            for cp in local_copies():
                cp.wait()

    def p_map(u, i, j, o):
        g = o[u // SUBS] * BLOCKS_PER_SHARD + lax.rem(u, SUBS) * nj + j
        return (_slot_of_seg(g // BLOCKS_PER_SEG), i, lax.rem(g, BLOCKS_PER_SEG))

    return pl.pallas_call(
        body, name="gather_and_project",
        grid_spec=pltpu.PrefetchScalarGridSpec(
            num_scalar_prefetch=1, grid=(units, ni, nj),
            in_specs=[ANY, ANY, ANY, ANY],
            out_specs=(pl.BlockSpec((None, tm, COL_BLOCK), p_map), ANY, ANY, ANY),
            scratch_shapes=[pltpu.VMEM((2, d, COL_BLOCK), BF16), pltpu.SemaphoreType.DMA((2,)),
                            pltpu.VMEM((2, tm, D_MODEL), BF16), pltpu.SemaphoreType.DMA((2,)),
                            pltpu.SemaphoreType.DMA((27,)), pltpu.SemaphoreType.DMA((27,)),
                            pltpu.SemaphoreType.DMA((2,))]),
        out_shape=(jax.ShapeDtypeStruct((N_SLOTS, t, D_MODEL), BF16),
                   jax.ShapeDtypeStruct((d, N_CHIPS * sc), BF16),
                   jax.ShapeDtypeStruct((3, N_CHIPS, rows, D_MODEL), BF16),
                   jax.ShapeDtypeStruct((N_CHIPS,) + conv_w8.shape, F32)),
        input_output_aliases={2: 1},
        compiler_params=pltpu.CompilerParams(dimension_semantics=("arbitrary",) * 3, vmem_limit_bytes=VMEM_LIMIT,
                                             has_side_effects=True),
    )(order, xb, w_pre, wo_b, conv_w8)


def _exchange_halves(g_o):
    hr = SHARD_ROWS // 2
    g_o4 = [g.reshape(N_CHIPS, 2, hr, D_MODEL) for g in g_o]

    def body(ga_ref, gb_ref, gc_ref, ra_ref, rb_ref, rc_ref, send_sems, recv_sems):
        x, y, c, _ = _mesh_pos()
        sibling = (x, y, 1 - c)
        cps = []
        for n, (g_ref, r_ref) in enumerate(((ga_ref, ra_ref), (gb_ref, rb_ref), (gc_ref, rc_ref))):
            cps.append(pltpu.make_async_remote_copy(src_ref=g_ref.at[:, 1 - c], dst_ref=r_ref,
                                                    send_sem=send_sems.at[n], recv_sem=recv_sems.at[n],
                                                    device_id=sibling, device_id_type=MESH))
        for cp in cps:
            cp.start()
        for cp in cps:
            cp.wait()

    o_shape = jax.ShapeDtypeStruct((N_CHIPS, hr, D_MODEL), BF16)
    return pl.pallas_call(
        body, name="rs_exchange_halves",
        in_specs=[ANY] * 3, out_specs=(ANY,) * 3,
        out_shape=(o_shape, o_shape, o_shape),
        scratch_shapes=[pltpu.SemaphoreType.DMA((3,)), pltpu.SemaphoreType.DMA((3,))],
        compiler_params=pltpu.CompilerParams(has_side_effects=True),
    )(*g_o4)


def _add_halves(g_in, g_o, r_in, r_o):
    d, c9 = g_in.shape
    hd = d // 2
    hr = SHARD_ROWS // 2
    core = lax.axis_index("c").astype(jnp.int32).reshape(1)
    tm = min(512, hd)
    nb = hd // tm

    def body_in(c_ref, g_ref, r_ref, o_ref):
        o_ref[...] = (g_ref[...].astype(F32) + r_ref[...].astype(F32)).astype(BF16)

    s_in = pl.pallas_call(
        body_in, name="rs_add_halves_in",
        grid_spec=pltpu.PrefetchScalarGridSpec(
            num_scalar_prefetch=1, grid=(N_CHIPS, nb),
            in_specs=[pl.BlockSpec((tm, SHARD_COLS), lambda k, i, c_ref: (c_ref[0] * nb + i, k)),
                      pl.BlockSpec((tm, SHARD_COLS), lambda k, i, c_ref: (i, k))],
            out_specs=pl.BlockSpec((None, tm, SHARD_COLS), lambda k, i, c_ref: (k, i, 0))),
        out_shape=jax.ShapeDtypeStruct((N_CHIPS, hd, SHARD_COLS), BF16),
        compiler_params=_params("parallel", "parallel"),
    )(core, g_in, r_in)

    g_o4 = [g.reshape(N_CHIPS, 2, hr, D_MODEL) for g in g_o]

    def body_o(c_ref, ga_ref, gb_ref, gc_ref, ra_ref, rb_ref, rc_ref, o_ref):
        for n, (g_ref, r_ref) in enumerate(((ga_ref, ra_ref), (gb_ref, rb_ref), (gc_ref, rc_ref))):
            o_ref[n] = (g_ref[...].astype(F32) + r_ref[...].astype(F32)).astype(BF16)

    gspec = pl.BlockSpec((None, None, hr, D_MODEL), lambda k, c_ref: (k, c_ref[0], 0, 0))
    rspec = pl.BlockSpec((None, hr, D_MODEL), lambda k, c_ref: (k, 0, 0))
    s_o = pl.pallas_call(
        body_o, name="rs_add_halves_o",
        grid_spec=pltpu.PrefetchScalarGridSpec(
            num_scalar_prefetch=1, grid=(N_CHIPS,),
            in_specs=[gspec] * 3 + [rspec] * 3,
            out_specs=pl.BlockSpec((3, None, hr, D_MODEL), lambda k, c_ref: (0, k, 0, 0))),
        out_shape=jax.ShapeDtypeStruct((3, N_CHIPS, hr, D_MODEL), BF16),
        compiler_params=_params("parallel"),
    )(core, *g_o4, *r_o)
    return s_in, s_o


def _sum_chips(r_in, r_o, s_in, s_o):
    _, hd, sc = r_in.shape
    hr = r_o.shape[2]
    tm = min(256, hd)
    nb = hd // tm
    pos = jnp.stack([2 * lax.axis_index("x") + lax.axis_index("y"), lax.axis_index("c")]).astype(jnp.int32)

    def chip_sum(pos_ref, r_ref, s_ref):
        acc = None
        for k in range(N_CHIPS):
            term = jnp.where(pos_ref[0] == k, s_ref[...], r_ref[k]).astype(F32)
            acc = term if acc is None else acc + term
        return acc

    def body_in(pos_ref, r_ref, s_ref, o_ref):
        o_ref[...] = chip_sum(pos_ref, r_ref, s_ref)

    f_in = pl.pallas_call(
        body_in, name="rs_sum_chips_in",
        grid_spec=pltpu.PrefetchScalarGridSpec(
            num_scalar_prefetch=1, grid=(nb,),
            in_specs=[pl.BlockSpec((N_CHIPS, tm, sc), lambda i, p: (0, i, 0)),
                      pl.BlockSpec((None, tm, sc), lambda i, p: (p[0], i, 0))],
            out_specs=pl.BlockSpec((tm, sc), lambda i, p: (p[1] * nb + i, 0))),
        out_shape=jax.ShapeDtypeStruct((2 * hd, sc), F32),
        compiler_params=_params("parallel"),
    )(pos, r_in, s_in)

    def body_o(pos_ref, r_ref, s_ref, o_ref):
        o_ref[...] = chip_sum(pos_ref, r_ref, s_ref)

    f_o = pl.pallas_call(
        body_o, name="rs_sum_chips_o",
        grid_spec=pltpu.PrefetchScalarGridSpec(
            num_scalar_prefetch=1, grid=(3,),
            in_specs=[pl.BlockSpec((N_CHIPS, None, hr, D_MODEL), lambda n, p: (0, n, 0, 0)),
                      pl.BlockSpec((None, None, hr, D_MODEL), lambda n, p: (n, p[0], 0, 0))],
            out_specs=pl.BlockSpec((None, hr, D_MODEL), lambda n, p: (n, p[1], 0))),
        out_shape=jax.ShapeDtypeStruct((3, 2 * hr, D_MODEL), F32),
        compiler_params=_params("parallel"),
    )(pos, r_o, s_o)
    return f_in, f_o


def _share_halves(f_in, f_o):
    hd, sc = f_in.shape[0] // 2, f_in.shape[1]
    hr = f_o.shape[1] // 2

    def body(fi_ref, fo_ref, gi_ref, go_ref, send_sems, recv_sems):
        x, y, c, _ = _mesh_pos()
        sibling = (x, y, 1 - c)

        def halves(cc):
            rows_i, rows_o = pl.ds(cc * hd, hd), pl.ds(cc * hr, hr)
            return (fi_ref.at[rows_i, :], gi_ref.at[rows_i, :]), (fo_ref.at[:, rows_o, :], go_ref.at[:, rows_o, :])

        def copies(cc):
            return [pltpu.make_async_remote_copy(src_ref=src, dst_ref=dst, send_sem=send_sems.at[n],
                                                 recv_sem=recv_sems.at[n], device_id=sibling, device_id_type=MESH)
                    for n, (src, dst) in enumerate(halves(cc))]

        sends = copies(c)
        for cp in sends:
            cp.start()
        for cp in copies(1 - c):
            cp.wait_recv()
        for cp in sends:
            cp.wait_send()

    return pl.pallas_call(
        body, name="rs_share_halves",
        in_specs=[ANY, ANY], out_specs=(ANY, ANY),
        out_shape=(jax.ShapeDtypeStruct(f_in.shape, F32), jax.ShapeDtypeStruct(f_o.shape, F32)),
        scratch_shapes=[pltpu.SemaphoreType.DMA((2,)), pltpu.SemaphoreType.DMA((2,))],
        input_output_aliases={0: 0, 1: 1},
        compiler_params=pltpu.CompilerParams(has_side_effects=True),
    )(f_in, f_o)


def _merge_forward(ya, yb, wo3, p, bg, tm, tn):
    t = ya.shape[0]

    def body(ya_ref, yb_ref, wa_ref, wb_ref, g_ref, bg_ref, m_ref, oab_ref):
        oa = _dot(ya_ref[...], wa_ref[...])
        ob = _dot(yb_ref[...], wb_ref[...])
        ga = _sigmoid(g_ref[0].astype(F32) + bg_ref[0])
        gb = _sigmoid(g_ref[1].astype(F32) + bg_ref[1])
        m_ref[...] = (ga * oa + gb * ob).astype(BF16)
        oab_ref[0] = oa.astype(BF16)
        oab_ref[1] = ob.astype(BF16)

    return pl.pallas_call(
        body, name="merge_forward", grid=(t // tm, D_MODEL // tn),
        in_specs=[pl.BlockSpec((tm, D_MODEL), lambda i, j: (i, 0)),
                  pl.BlockSpec((tm, D_MODEL), lambda i, j: (i, 0)),
                  pl.BlockSpec((None, D_MODEL, tn), lambda i, j: (0, 0, j)),
                  pl.BlockSpec((None, D_MODEL, tn), lambda i, j: (1, 0, j)),
                  pl.BlockSpec((2, tm, tn), lambda i, j: (BLOCK_G, i, j)),
                  pl.BlockSpec((2, 1, tn), lambda i, j: (0, 0, j))],
        out_specs=(pl.BlockSpec((tm, tn), lambda i, j: (i, j)),
                   pl.BlockSpec((2, tm, tn), lambda i, j: (0, i, j))),
        out_shape=(jax.ShapeDtypeStruct((t, D_MODEL), BF16), jax.ShapeDtypeStruct((2, t, D_MODEL), BF16)),
        compiler_params=_params("parallel", "parallel"),
    )(ya, yb, wo3, wo3, p, bg)


HEAD_ROWS = 256


def _head(merged, wo3, x, target, ln_g, ln_b, tm):
    t = x.shape[0]
    inv_d = 1.0 / D_MODEL

    def body(m_ref, w_ref, x_ref, t_ref, g_ref, b_ref, dr_ref, gx_ref, dm_ref, dg_ref, db_ref, loss_ref):
        i = pl.program_id(0)

        @pl.when(i == 0)
        def _():
            dg_ref[...] = jnp.zeros_like(dg_ref)
            db_ref[...] = jnp.zeros_like(db_ref)
            loss_ref[...] = jnp.zeros_like(loss_ref)

        w = w_ref[...]
        g = g_ref[...]
        tiles = [slice(r0, r0 + HEAD_ROWS) for r0 in range(0, tm, HEAD_ROWS)]
        firsts = [_dot(m_ref[rows, :], w) for rows in tiles]
        for rows, out in zip(tiles, firsts):
            r = DN_ALPHA * x_ref[rows, :] + out
            mu = jnp.mean(r, axis=-1, keepdims=True)
            xc = r - mu
            var = jnp.mean(xc * xc, axis=-1, keepdims=True)
            rstd = lax.rsqrt(var + LN_EPS)
            xhat = xc * rstd
            e = xhat * g + b_ref[...] - t_ref[rows, :]
            se = jnp.sum(jnp.sum(e * e, axis=1, keepdims=True), axis=0, keepdims=True)
            loss_ref[...] += jnp.broadcast_to((0.5 * inv_d) * se, loss_ref.shape)
            dy = e * inv_d
            db_ref[...] += jnp.sum(dy, axis=0, keepdims=True)
            dg_ref[...] += jnp.sum(dy * xhat, axis=0, keepdims=True)
            dxh = dy * g
            m1 = jnp.mean(dxh, axis=-1, keepdims=True)
            m2 = jnp.mean(dxh * xhat, axis=-1, keepdims=True)
            dr = rstd * (dxh - m1 - xhat * m2)
            gx_ref[rows, :] = DN_ALPHA * dr
            drb = dr.astype(BF16)
            dr_ref[rows, :] = drb
            dm_ref[rows, :] = _dot(drb, w, 1, 1).astype(BF16)

    row = pl.BlockSpec((tm, D_MODEL), lambda i: (i, 0))
    vec = pl.BlockSpec((1, D_MODEL), lambda i: (0, 0))
    return pl.pallas_call(
        body, name="head", grid=(t // tm,),
        in_specs=[row, pl.BlockSpec((None, D_MODEL, D_MODEL), lambda i: (2, 0, 0), pipeline_mode=pl.Buffered(1)),
                  row, row, vec, vec],
        out_specs=(row, row, row, vec, vec, pl.BlockSpec((1, 128), lambda i: (0, 0))),
        out_shape=(jax.ShapeDtypeStruct((t, D_MODEL), BF16), jax.ShapeDtypeStruct((t, D_MODEL), F32),
                   jax.ShapeDtypeStruct((t, D_MODEL), BF16), jax.ShapeDtypeStruct((1, D_MODEL), F32),
                   jax.ShapeDtypeStruct((1, D_MODEL), F32), jax.ShapeDtypeStruct((1, 128), F32)),
        compiler_params=_params("arbitrary"),
    )(merged, wo3, x, target, ln_g, ln_b)


def _gate_and_branch_backward(dmerged, oab, p, bg, wo3, tm):
    t = dmerged.shape[0]

    def body(dm_ref, oab_ref, g_ref, bg_ref, wa_ref, wb_ref, do_ref, dpg_ref, dbg_ref, dy_ref):
        @pl.when(pl.program_id(0) == 0)
        def _():
            dbg_ref[...] = jnp.zeros_like(dbg_ref)

        dm = dm_ref[...].astype(F32)
        for n, w_ref in enumerate((wa_ref, wb_ref)):
            gate = _sigmoid(g_ref[n].astype(F32) + bg_ref[n])
            d_o = (dm * gate).astype(BF16)
            do_ref[n] = d_o
            dgate = dm * oab_ref[n].astype(F32) * gate * (1.0 - gate)
            dpg_ref[n] = dgate.astype(BF16)
            dbg_ref[n] += jnp.sum(dgate, axis=0, keepdims=True)
            dy_ref[n] = _dot(d_o, w_ref[...], 1, 1).astype(BF16)

    pair = pl.BlockSpec((2, tm, D_MODEL), lambda i: (0, i, 0))
    gates = pl.BlockSpec((2, tm, D_MODEL), lambda i: (BLOCK_G, i, 0))
    vec = pl.BlockSpec((2, 1, D_MODEL), lambda i: (0, 0, 0))

    def weight(n):
        return pl.BlockSpec((None, D_MODEL, D_MODEL), lambda i: (n, 0, 0), pipeline_mode=pl.Buffered(1))

    pair_shape = jax.ShapeDtypeStruct((2, t, D_MODEL), BF16)
    return pl.pallas_call(
        body, name="gate_and_branch_backward", grid=(t // tm,),
        in_specs=[pl.BlockSpec((tm, D_MODEL), lambda i: (i, 0)), pair, gates, vec, weight(0), weight(1)],
        out_specs=(pair, gates, vec, pair),
        out_shape=(pair_shape, jax.ShapeDtypeStruct((N_SLOTS, t, D_MODEL), BF16),
                   jax.ShapeDtypeStruct((2, 1, D_MODEL), F32), pair_shape),
        compiler_params=_params("arbitrary"),
    )(dmerged, oab, p, bg, wo3, wo3)


def _weight_grad(a, a_sel, b, b_sel, tm, tn, tk, name, exchange=None):
    t = a.shape[-2]
    nk = t // tk
    ni, nj = D_MODEL // tm, D_MODEL // tn

    def body(a_ref, b_ref, *rest):
        if exchange is None:
            o_ref, acc_ref = rest
        else:
            g_ref, o_ref, r_ref, acc_ref, send_sem, recv_sem = rest
        i, j, k = pl.program_id(0), pl.program_id(1), pl.program_id(2)

        if exchange is not None:
            x, y, c, _ = _mesh_pos()
            hd = exchange.shape[0] // 2
            swap = pltpu.make_async_remote_copy(src_ref=g_ref.at[pl.ds((1 - c) * hd, hd), :], dst_ref=r_ref,
                                                send_sem=send_sem, recv_sem=recv_sem,
                                                device_id=(x, y, 1 - c), device_id_type=MESH)

            @pl.when(jnp.logical_and(i == 0, jnp.logical_and(j == 0, k == 0)))
            def _():
                swap.start()

        @pl.when(k == 0)
        def _():
            acc_ref[...] = jnp.zeros_like(acc_ref)

        acc_ref[...] += _dot(a_ref[...], b_ref[...], 0, 0)

        @pl.when(k == nk - 1)
        def _():
            o_ref[...] = acc_ref[...].astype(BF16)

        if exchange is not None:
            @pl.when(jnp.logical_and(i == ni - 1, jnp.logical_and(j == nj - 1, k == nk - 1)))
            def _():
                swap.wait()

    def spec(arr, sel, width, which):
        if arr.ndim == 2:
            return pl.BlockSpec((tk, width), lambda i, j, k: (k, (i, j)[which]))
        return pl.BlockSpec((None, tk, width), lambda i, j, k: (sel, k, (i, j)[which]))

    o_spec = pl.BlockSpec((tm, tn), lambda i, j, k: (i, j))
    o_shape = jax.ShapeDtypeStruct((D_MODEL, D_MODEL), BF16)
    if exchange is None:
        return pl.pallas_call(
            body, name=name, grid=(ni, nj, nk),
            in_specs=[spec(a, a_sel, tm, 0), spec(b, b_sel, tn, 1)],
            out_specs=o_spec, out_shape=o_shape,
            scratch_shapes=[pltpu.VMEM((tm, tn), F32)],
            compiler_params=_params("parallel", "parallel", "arbitrary"),
        )(a, b)
    return pl.pallas_call(
        body, name=name, grid=(ni, nj, nk),
        in_specs=[spec(a, a_sel, tm, 0), spec(b, b_sel, tn, 1), ANY],
        out_specs=(o_spec, ANY),
        out_shape=(o_shape, jax.ShapeDtypeStruct((exchange.shape[0] // 2, exchange.shape[1]), BF16)),
        scratch_shapes=[pltpu.VMEM((tm, tn), F32), pltpu.SemaphoreType.DMA, pltpu.SemaphoreType.DMA],
        compiler_params=pltpu.CompilerParams(dimension_semantics=("arbitrary",) * 3, vmem_limit_bytes=VMEM_LIMIT,
                                             has_side_effects=True),
    )(a, b, exchange)


def _win_grad(xt, dp, tn, tk):
    _, t, _ = dp.shape
    nk = t // tk
    per_seg = D_MODEL // tn
    nj = N_SEG * per_seg

    def body(x_ref, dp_ref, o_ref, acc_ref):
        k = pl.program_id(1)

        @pl.when(k == 0)
        def _():
            acc_ref[...] = jnp.zeros_like(acc_ref)

        acc_ref[...] += _dot(x_ref[...], dp_ref[...])

        @pl.when(k == nk - 1)
        def _():
            o_ref[...] = acc_ref[...].astype(BF16)

    return pl.pallas_call(
        body, name="grad_w_in", grid=(nj, nk),
        in_specs=[pl.BlockSpec((D_MODEL, tk), lambda j, k: (0, k)),
                  pl.BlockSpec((None, tk, tn), lambda j, k: (_slot_of_seg(j // per_seg), k, j % per_seg))],
        out_specs=pl.BlockSpec((D_MODEL, tn), lambda j, k: (0, j)),
        out_shape=jax.ShapeDtypeStruct((D_MODEL, N_SEG * D_MODEL), BF16),
        scratch_shapes=[pltpu.VMEM((D_MODEL, tn), F32)],
        compiler_params=_params("parallel", "arbitrary"),
    )(xt, dp)


def _input_grad_and_scatter(dp, w_full, gx, s_in, s_o, small, tm):
    _, t, _ = dp.shape
    ni, nk = t // tm, N_SEG - 1
    _, hd, sc = s_in.shape
    hr = s_o.shape[2]

    def body(dp_ref, w_ref, gx_ref, si_ref, so_ref, sm_ref, o_ref, ri_ref, ro_ref, ga_ref,
             send_sems, recv_sems, local_sem):
        i, k = pl.program_id(0), pl.program_id(1)
        x, y, c, chips = _mesh_pos()
        me = 2 * x + y
        dev = 4 * x + 2 * y + c

        def peer(r):
            return (x ^ ((r >> 2) & 1), y ^ ((r >> 1) & 1), c ^ (r & 1))

        def sends():
            cps = []
            for q, (cx, cy) in enumerate(chips):
                dest = 2 * cx + cy
                cps.append(pltpu.make_async_remote_copy(src_ref=si_ref.at[dest], dst_ref=ri_ref.at[me],
                                                        send_sem=send_sems.at[q], recv_sem=recv_sems.at[q],
                                                        device_id=(cx, cy, c), device_id_type=MESH))
                cps.append(pltpu.make_async_remote_copy(src_ref=so_ref.at[:, dest], dst_ref=ro_ref.at[me],
                                                        send_sem=send_sems.at[3 + q], recv_sem=recv_sems.at[3 + q],
                                                        device_id=(cx, cy, c), device_id_type=MESH))
            for r in range(1, 8):
                cps.append(pltpu.make_async_remote_copy(src_ref=sm_ref, dst_ref=ga_ref.at[dev],
                                                        send_sem=send_sems.at[5 + r], recv_sem=recv_sems.at[5 + r],
                                                        device_id=peer(r), device_id_type=MESH))
            return cps

        own_small = pltpu.make_async_copy(sm_ref, ga_ref.at[dev], local_sem)

        @pl.when(jnp.logical_and(i == 0, k == 0))
        def _():
            for cp in sends():
                cp.start()
            own_small.start()

        @pl.when(k == 0)
        def _():
            o_ref[...] = gx_ref[...]

        o_ref[...] += _dot(dp_ref[...], w_ref[...], 1, 1)

        @pl.when(jnp.logical_and(i == ni - 1, k == nk - 1))
        def _():
            for q, (cx, cy) in enumerate(chips):
                frm = 2 * cx + cy
                pltpu.make_async_remote_copy(src_ref=si_ref.at[frm], dst_ref=ri_ref.at[frm], send_sem=send_sems.at[q],
                                             recv_sem=recv_sems.at[q], device_id=(x, y, c),
                                             device_id_type=MESH).wait_recv()
                pltpu.make_async_remote_copy(src_ref=so_ref.at[:, frm], dst_ref=ro_ref.at[frm],
                                             send_sem=send_sems.at[3 + q], recv_sem=recv_sems.at[3 + q],
                                             device_id=(x, y, c), device_id_type=MESH).wait_recv()
            for r in range(1, 8):
                px, py, pc = peer(r)
                pltpu.make_async_remote_copy(src_ref=sm_ref, dst_ref=ga_ref.at[4 * px + 2 * py + pc],
                                             send_sem=send_sems.at[5 + r], recv_sem=recv_sems.at[5 + r],
                                             device_id=(x, y, c), device_id_type=MESH).wait_recv()
            for cp in sends():
                cp.wait_send()
            own_small.wait()

    return pl.pallas_call(
        body, name="grad_x_and_scatter", grid=(ni, nk),
        in_specs=[pl.BlockSpec((None, tm, D_MODEL), lambda i, k: (_slot_of_seg(k), i, 0)),
                  pl.BlockSpec((D_MODEL, D_MODEL), lambda i, k: (0, k)),
                  pl.BlockSpec((tm, D_MODEL), lambda i, k: (i, 0)), ANY, ANY, ANY],
        out_specs=(pl.BlockSpec((tm, D_MODEL), lambda i, k: (i, 0)), ANY, ANY, ANY),
        out_shape=(jax.ShapeDtypeStruct((t, D_MODEL), F32),
                   jax.ShapeDtypeStruct((N_CHIPS, hd, sc), BF16),
                   jax.ShapeDtypeStruct((N_CHIPS, 3, hr, D_MODEL), BF16),
                   jax.ShapeDtypeStruct((8,) + small.shape, small.dtype)),
        scratch_shapes=[pltpu.SemaphoreType.DMA((13,)), pltpu.SemaphoreType.DMA((13,)), pltpu.SemaphoreType.DMA],
        input_output_aliases={2: 0},
        compiler_params=pltpu.CompilerParams(dimension_semantics=("arbitrary", "arbitrary"),
                                             vmem_limit_bytes=VMEM_LIMIT, has_side_effects=True),
    )(dp, w_full, gx, s_in, s_o, small)


def _input_grad_tail(dp, w_full, partial, tm):
    _, t, _ = dp.shape
    seg = N_SEG - 1

    def body(dp_ref, w_ref, part_ref, o_ref):
        o_ref[...] = part_ref[...] + _dot(dp_ref[...], w_ref[...], 1, 1)

    row = pl.BlockSpec((tm, D_MODEL), lambda i: (i, 0))
    return pl.pallas_call(
        body, name="grad_x_tail", grid=(t // tm,),
        in_specs=[pl.BlockSpec((None, tm, D_MODEL), lambda i: (SLOT_OF_SEG[seg], i, 0)),
                  pl.BlockSpec((D_MODEL, D_MODEL), lambda i: (0, seg)), row],
        out_specs=row,
        out_shape=jax.ShapeDtypeStruct((t, D_MODEL), F32),
        compiler_params=_params("parallel"),
    )(dp, w_full, partial)


def _sgu_chunk_forward(u, v, z, wm, bs, lng, lnb):
    ug, dug = _gelu_and_grad(u)
    vg, dvg = _gelu_and_grad(v)
    mu = jnp.mean(vg, axis=-1, keepdims=True)
    xc = vg - mu
    var = jnp.mean(xc * xc, axis=-1, keepdims=True)
    rstd = lax.rsqrt(var + LN_EPS)
    vhat = xc * rstd
    vln = (vhat * lng + lnb).astype(BF16)
    mixed = _dot(wm, vln) + bs
    sig = _sigmoid(z)
    return ug, dug, dvg, rstd, vhat, vln, mixed, sig


def _mixer_a_forward(p_a, wm, bs_col, ln_v_g, ln_v_b, tm):
    t = p_a.shape[1]

    def body(p_ref, wm_ref, bs_ref, g_ref, b_ref, o_ref):
        wm_v, bs_v, lng, lnb = wm_ref[...], bs_ref[...], g_ref[...], b_ref[...]

        def chunk(ci, carry):
            rows = pl.ds(pl.multiple_of(ci * CHUNK, CHUNK), CHUNK)
            u = p_ref[0, rows, :].astype(F32)
            v = p_ref[1, rows, :].astype(F32)
            z = p_ref[2, rows, :].astype(F32)
            ug, _, _, _, _, _, mixed, sig = _sgu_chunk_forward(u, v, z, wm_v, bs_v, lng, lnb)
            o_ref[rows, :] = (ug * mixed * (z * sig)).astype(BF16)
            return carry

        lax.fori_loop(0, tm // CHUNK, chunk, 0, unroll=True)

    return pl.pallas_call(
        body, name="mixer_a_forward", grid=(t // tm, N_HEADS),
        in_specs=[pl.BlockSpec((3, tm, HEAD_DIM), lambda i, h: (0, i, h)),
                  pl.BlockSpec((None, CHUNK, CHUNK), lambda i, h: (h, 0, 0)),
                  pl.BlockSpec((None, CHUNK, 1), lambda i, h: (h, 0, 0)),
                  pl.BlockSpec((1, HEAD_DIM), lambda i, h: (0, h)),
                  pl.BlockSpec((1, HEAD_DIM), lambda i, h: (0, h))],
        out_specs=pl.BlockSpec((tm, HEAD_DIM), lambda i, h: (i, h)),
        out_shape=jax.ShapeDtypeStruct((t, D_MODEL), BF16),
        compiler_params=_params("parallel", "parallel"),
    )(p_a, wm, bs_col, ln_v_g, ln_v_b)


def _mixer_a_backward(p_a, dyab, wm, bs_col, ln_v_g, ln_v_b, dp, tm):
    t = p_a.shape[1]

    def body(p_ref, dy_ref, wm_ref, bs_ref, g_ref, b_ref, dp_in, dp_ref, dws_ref, dbs_ref, dg_ref, db_ref):
        @pl.when(pl.program_id(1) == 0)
        def _():
            dws_ref[...] = jnp.zeros_like(dws_ref)
            dbs_ref[...] = jnp.zeros_like(dbs_ref)
            dg_ref[...] = jnp.zeros_like(dg_ref)
            db_ref[...] = jnp.zeros_like(db_ref)

        wm_v, bs_v, lng, lnb = wm_ref[...], bs_ref[...], g_ref[...], b_ref[...]
        causal = (lax.broadcasted_iota(jnp.int32, (CHUNK, CHUNK), 1)
                  <= lax.broadcasted_iota(jnp.int32, (CHUNK, CHUNK), 0))

        def chunk(ci, carry):
            rows = pl.ds(pl.multiple_of(ci * CHUNK, CHUNK), CHUNK)
            u = p_ref[0, rows, :].astype(F32)
            v = p_ref[1, rows, :].astype(F32)
            z = p_ref[2, rows, :].astype(F32)
            dy = dy_ref[rows, :].astype(F32)
            ug, dug, dvg, rstd, vhat, vln, mixed, sig = _sgu_chunk_forward(u, v, z, wm_v, bs_v, lng, lnb)
            sz = z * sig
            dmixed = dy * ug * sz
            dp_ref[0, rows, :] = (dy * mixed * sz * dug).astype(BF16)
            dp_ref[2, rows, :] = (dy * ug * mixed * _silu_grad(sig, sz)).astype(BF16)
            dbs_ref[...] += jnp.sum(dmixed, axis=1, keepdims=True)
            dmb = dmixed.astype(BF16)
            dws_ref[...] += jnp.where(causal, _dot(dmb, vln, 1, 1), 0.0)
            dvln = _dot(wm_v, dmb, 0, 0)
            db_ref[...] += jnp.sum(dvln, axis=0, keepdims=True)
            dg_ref[...] += jnp.sum(dvln * vhat, axis=0, keepdims=True)
            dvh = dvln * lng
            m1 = jnp.mean(dvh, axis=-1, keepdims=True)
            m2 = jnp.mean(dvh * vhat, axis=-1, keepdims=True)
            dp_ref[1, rows, :] = (rstd * (dvh - m1 - vhat * m2) * dvg).astype(BF16)
            return carry

        lax.fori_loop(0, tm // CHUNK, chunk, 0, unroll=True)

    return pl.pallas_call(
        body, name="mixer_a_backward", grid=(N_HEADS, t // tm),
        in_specs=[pl.BlockSpec((3, tm, HEAD_DIM), lambda h, i: (0, i, h)),
                  pl.BlockSpec((None, tm, HEAD_DIM), lambda h, i: (0, i, h)),
                  pl.BlockSpec((None, CHUNK, CHUNK), lambda h, i: (h, 0, 0)),
                  pl.BlockSpec((None, CHUNK, 1), lambda h, i: (h, 0, 0)),
                  pl.BlockSpec((1, HEAD_DIM), lambda h, i: (0, h)),
                  pl.BlockSpec((1, HEAD_DIM), lambda h, i: (0, h)), ANY],
        out_specs=(pl.BlockSpec((3, tm, HEAD_DIM), lambda h, i: (BLOCK_A, i, h)),
                   pl.BlockSpec((None, CHUNK, CHUNK), lambda h, i: (h, 0, 0)),
                   pl.BlockSpec((None, CHUNK, 1), lambda h, i: (h, 0, 0)),
                   pl.BlockSpec((1, HEAD_DIM), lambda h, i: (0, h)),
                   pl.BlockSpec((1, HEAD_DIM), lambda h, i: (0, h))),
        out_shape=(jax.ShapeDtypeStruct(dp.shape, BF16),
                   jax.ShapeDtypeStruct((N_HEADS, CHUNK, CHUNK), F32),
                   jax.ShapeDtypeStruct((N_HEADS, CHUNK, 1), F32),
                   jax.ShapeDtypeStruct((1, D_MODEL), F32), jax.ShapeDtypeStruct((1, D_MODEL), F32)),
        input_output_aliases={6: 0},
        compiler_params=_params("parallel", "arbitrary"),
    )(p_a, dyab, wm, bs_col, ln_v_g, ln_v_b, dp)


HALO = 16


def _conv_taps(h, halo_h, tm):
    row = lax.broadcasted_iota(jnp.int32, h.shape, 0)
    last1 = halo_h[HALO - 1:HALO, :]
    last2 = halo_h[HALO - 2:HALO - 1, :]
    h1 = jnp.where(row == 0, last1, pltpu.roll(h, 1, 0))
    h2 = jnp.where(row == 0, last2, jnp.where(row == 1, last1, pltpu.roll(h, 2, 0)))
    return h1, h2


def _mixer_b_forward(p_b, conv_w, conv_b, tm, tc):
    t = p_b.shape[1]

    def body(p_ref, halo_ref, w_ref, b_ref, o_ref):
        valid = (pl.program_id(1) > 0).astype(F32)
        h = p_ref[1].astype(F32) * p_ref[0].astype(F32)
        halo_h = halo_ref[1].astype(F32) * halo_ref[0].astype(F32) * valid
        h1, h2 = _conv_taps(h, halo_h, tm)
        w = w_ref[...]
        conv = b_ref[...] + w[0:1, :] * h2 + w[1:2, :] * h1 + w[2:3, :] * h
        z = p_ref[3].astype(F32)
        o_ref[...] = (p_ref[2].astype(F32) * conv * (z * _sigmoid(z))).astype(BF16)

    steps = tm // HALO
    return pl.pallas_call(
        body, name="mixer_b_forward", grid=(D_MODEL // tc, t // tm),
        in_specs=[pl.BlockSpec((4, tm, tc), lambda j, i: (BLOCK_B, i, j)),
                  pl.BlockSpec((4, HALO, tc), lambda j, i: (BLOCK_B, jnp.maximum(i * steps - 1, 0), j)),
                  pl.BlockSpec((3, tc), lambda j, i: (0, j)),
                  pl.BlockSpec((1, tc), lambda j, i: (0, j))],
        out_specs=pl.BlockSpec((tm, tc), lambda j, i: (i, j)),
        out_shape=jax.ShapeDtypeStruct((t, D_MODEL), BF16),
        compiler_params=_params("parallel", "parallel"),
    )(p_b, p_b, conv_w, conv_b)


def _mixer_b_backward(p_b, dyab, conv_w, conv_b, dp, tm, tc):
    t = p_b.shape[1]
    n = t // tm

    def body(p_ref, halo_ref, dy_ref, w_ref, b_ref, dp_in, dp_ref, dw_ref, db_ref, next_ref):
        ii = pl.program_id(1)

        @pl.when(ii == 0)
        def _():
            dw_ref[...] = jnp.zeros_like(dw_ref)
            db_ref[...] = jnp.zeros_like(db_ref)
            next_ref[...] = jnp.zeros_like(next_ref)

        valid = (ii < n - 1).astype(F32)
        xb = p_ref[0].astype(F32)
        cb = p_ref[1].astype(F32)
        bb = p_ref[2].astype(F32)
        z = p_ref[3].astype(F32)
        h = cb * xb
        halo_h = halo_ref[1].astype(F32) * halo_ref[0].astype(F32) * valid
        h1, h2 = _conv_taps(h, halo_h, tm)
        w = w_ref[...]
        w0, w1, w2 = w[0:1, :], w[1:2, :], w[2:3, :]
        conv = b_ref[...] + w0 * h2 + w1 * h1 + w2 * h
        sig = _sigmoid(z)
        sz = z * sig
        dy = dy_ref[...].astype(F32)
        dconv = dy * bb * sz
        dp_ref[2] = (dy * conv * sz).astype(BF16)
        dp_ref[3] = (dy * bb * conv * _silu_grad(sig, sz)).astype(BF16)
        db_ref[...] += jnp.sum(dconv, axis=0, keepdims=True)
        dw_ref[0:1, :] += jnp.sum(dconv * h2, axis=0, keepdims=True)
        dw_ref[1:2, :] += jnp.sum(dconv * h1, axis=0, keepdims=True)
        dw_ref[2:3, :] += jnp.sum(dconv * h, axis=0, keepdims=True)
        row = lax.broadcasted_iota(jnp.int32, h.shape, 0)
        nxt = next_ref[...]
        n0, n1 = nxt[0:1, :], nxt[1:2, :]
        d1 = jnp.where(row == tm - 1, n0, pltpu.roll(dconv, tm - 1, 0))
        d2 = jnp.where(row == tm - 1, n1, jnp.where(row == tm - 2, n0, pltpu.roll(dconv, tm - 2, 0)))
        dh = w2 * dconv + w1 * d1 + w0 * d2
        dp_ref[0] = (dh * cb).astype(BF16)
        dp_ref[1] = (dh * xb).astype(BF16)
        next_ref[...] = dconv[0:8, :]

    steps = tm // HALO
    return pl.pallas_call(
        body, name="mixer_b_backward", grid=(D_MODEL // tc, n),
        in_specs=[pl.BlockSpec((4, tm, tc), lambda j, ii: (BLOCK_B, n - 1 - ii, j)),
                  pl.BlockSpec((4, HALO, tc), lambda j, ii: (BLOCK_B, jnp.maximum((n - 1 - ii) * steps - 1, 0), j)),
                  pl.BlockSpec((None, tm, tc), lambda j, ii: (1, n - 1 - ii, j)),
                  pl.BlockSpec((3, tc), lambda j, ii: (0, j)),
                  pl.BlockSpec((1, tc), lambda j, ii: (0, j)), ANY],
        out_specs=(pl.BlockSpec((4, tm, tc), lambda j, ii: (BLOCK_B, n - 1 - ii, j)),
                   pl.BlockSpec((3, tc), lambda j, ii: (0, j)),
                   pl.BlockSpec((1, tc), lambda j, ii: (0, j))),
        out_shape=(jax.ShapeDtypeStruct(dp.shape, BF16),
                   jax.ShapeDtypeStruct((3, D_MODEL), F32), jax.ShapeDtypeStruct((1, D_MODEL), F32)),
        scratch_shapes=[pltpu.VMEM((8, tc), F32)],
        input_output_aliases={5: 0},
        compiler_params=_params("parallel", "arbitrary"),
    )(p_b, p_b, dyab, conv_w, conv_b, dp)


def _adam_math(w, g, m, v):
    m = ADAM_B1 * m + (1.0 - ADAM_B1) * g
    v = ADAM_B2 * v + (1.0 - ADAM_B2) * (g * g)
    delta = -ADAM_LR * ((m * ADAM_C1) / (jnp.sqrt(v * ADAM_C2) + ADAM_EPS) + ADAM_WD * w)
    return delta, m, v


def _adam_rows(w, g, m, v, tm, name, g_sel=None):
    r, c = w.shape

    def body(w_ref, g_ref, m_ref, v_ref, go_ref, d_ref, mo_ref, vo_ref):
        g = g_ref[...]
        d, mn, vn = _adam_math(w_ref[...], g, m_ref[...], v_ref[...])
        go_ref[...] = g
        d_ref[...] = d
        mo_ref[...] = mn
        vo_ref[...] = vn

    spec = pl.BlockSpec((tm, c), lambda i: (i, 0))
    g_spec = spec if g_sel is None else pl.BlockSpec((None, tm, c), lambda i: (g_sel, i, 0))
    shape = jax.ShapeDtypeStruct((r, c), F32)
    return pl.pallas_call(
        body, name=name, grid=(r // tm,),
        in_specs=[spec, g_spec, spec, spec], out_specs=(spec,) * 4, out_shape=(shape,) * 4,
        compiler_params=_params("parallel"),
    )(w, g, m, v)


SMALL_ROW0 = {name: sum(r for _, r in SMALL_ROWS[:i]) for i, (name, _) in enumerate(SMALL_ROWS)}
LANE_MAJOR = ("ln_g", "ln_b", "b_gate", "ln_v_g", "ln_v_b", "conv_b")


def _lane_pieces(n):
    return [(q, slice(q * 128, (q + 1) * 128)) for q in range(n // 128)]


def _pack_small(d_ln_g, d_ln_b, d_bg, d_lnv_g, d_lnv_b, d_ws, d_bs, d_cw, d_cb, loss_part):
    def body(lg, lb, bg, vg, vb, ws, bs, cw, cb, loss, o_ref):
        def put(row0, vec):
            for q, cols in _lane_pieces(vec.shape[1]):
                o_ref[row0 + q:row0 + q + 1, :] = vec[:, cols]

        put(SMALL_ROW0["ln_g"], lg[...])
        put(SMALL_ROW0["ln_b"], lb[...])
        for n in range(2):
            put(SMALL_ROW0["b_gate"] + n * (D_MODEL // 128), bg[n])
        put(SMALL_ROW0["ln_v_g"], vg[...])
        put(SMALL_ROW0["ln_v_b"], vb[...])
        for h in range(N_HEADS):
            o_ref[SMALL_ROW0["w_s"] + h * CHUNK:SMALL_ROW0["w_s"] + (h + 1) * CHUNK, :] = ws[h]
        o_ref[SMALL_ROW0["b_s"]:SMALL_ROW0["b_s"] + N_HEADS, :] = bs[...]
        for c in range(3):
            put(SMALL_ROW0["conv_w"] + c * (D_MODEL // 128), cw[c:c + 1, :])
        put(SMALL_ROW0["conv_b"], cb[...])
        o_ref[SMALL_ROW0["loss"]:SMALL_ROW0["loss"] + 8, :] = jnp.broadcast_to(loss[...], (8, 128))

    return pl.pallas_call(
        body, name="pack_small", out_shape=jax.ShapeDtypeStruct((SMALL_TOTAL, 128), F32), compiler_params=_params(),
    )(d_ln_g, d_ln_b, d_bg, d_lnv_g, d_lnv_b, d_ws, d_bs.reshape(N_HEADS, CHUNK), d_cw, d_cb, loss_part)


def _adam_small(gathered, params):
    names = list(params)
    flat = [a for n in names for a in params[n]]

    def body(*refs):
        ga_ref = refs[0]
        ins = refs[1:1 + 3 * len(names)]
        outs = refs[1 + 3 * len(names):-1]
        gs_ref = refs[-1]
        g = ga_ref[0]
        for k in range(1, 8):
            g = g + ga_ref[k]
        gs_ref[...] = g
        for i, name in enumerate(names):
            w_ref, m_ref, v_ref = ins[3 * i:3 * i + 3]
            o_refs = outs[4 * i:4 * i + 4]
            row0 = SMALL_ROW0[name]
            if name in LANE_MAJOR:
                pieces = [((slice(None), cols), slice(row0 + q, row0 + q + 1))
                          for q, cols in _lane_pieces(w_ref.shape[1])]
            elif name == "w_s":
                pieces = [((0, h), slice(row0 + h * CHUNK, row0 + (h + 1) * CHUNK)) for h in range(N_HEADS)]
            else:
                pieces = [((0,), slice(row0, row0 + N_HEADS))]
            for idx, rows in pieces:
                gp = gs_ref[rows, :]
                res = (gp,) + _adam_math(w_ref[idx], gp, m_ref[idx], v_ref[idx])
                for o_ref, val in zip(o_refs, res):
                    o_ref[idx] = val
        gcw_ref, loss_ref = outs[-2:]
        for c in range(3):
            for q, cols in _lane_pieces(D_MODEL):
                r = SMALL_ROW0["conv_w"] + c * (D_MODEL // 128) + q
                gcw_ref[c:c + 1, cols] = gs_ref[r:r + 1, :]
        loss_ref[...] = gs_ref[SMALL_ROW0["loss"]:SMALL_ROW0["loss"] + 1, :]

    out_shape = [jax.ShapeDtypeStruct(params[n][0].shape, F32) for n in names for _ in range(4)]
    out_shape += [jax.ShapeDtypeStruct((3, D_MODEL), F32), jax.ShapeDtypeStruct((1, 128), F32)]
    res = pl.pallas_call(
        body, name="adam_small", out_shape=tuple(out_shape),
        scratch_shapes=[pltpu.VMEM((SMALL_TOTAL, 128), F32)], compiler_params=_params(),
    )(gathered, *flat)
    return {n: res[4 * i:4 * i + 4] for i, n in enumerate(names)}, res[-2], res[-1]


def _adam_conv_w(g_all, chip, w, m, v):
    cols = w.shape[2]

    def body(c_ref, g_ref, w_ref, m_ref, v_ref, go_ref, d_ref, mo_ref, vo_ref):
        g = g_ref[...]
        d, mn, vn = _adam_math(w_ref[...], g, m_ref[...], v_ref[...])
        go_ref[...] = g
        d_ref[...] = d
        mo_ref[...] = mn
        vo_ref[...] = vn

    own = pl.BlockSpec((None, 3, cols), lambda i, c_ref: (0, 0, 0))
    return pl.pallas_call(
        body, name="adam_conv_w",
        grid_spec=pltpu.PrefetchScalarGridSpec(
            num_scalar_prefetch=1, grid=(1,),
            in_specs=[pl.BlockSpec((3, cols), lambda i, c_ref: (0, c_ref[0])), own, own, own],
            out_specs=(own,) * 4),
        out_shape=(jax.ShapeDtypeStruct(w.shape, F32),) * 4,
        compiler_params=_params("arbitrary"),
    )(chip, g_all, w, m, v)


def kernel(x, w_in, b_gate, ln_v_g, ln_v_b, w_s, b_s, conv_w, conv_b, w_oa, w_ob, w_out, ln_g, ln_b, loss_target, m_w_in, m_b_gate, m_ln_v_g, m_ln_v_b, m_w_s, m_b_s, m_conv_w, m_conv_b, m_w_oa, m_w_ob, m_w_out, m_ln_g, m_ln_b, v_w_in, v_b_gate, v_ln_v_g, v_ln_v_b, v_w_s, v_b_s, v_conv_w, v_conv_b, v_w_oa, v_w_ob, v_w_out, v_ln_g, v_ln_b):
    t = x.shape[1]
    x2 = x[0]
    target = loss_target[0]
    chip = 2 * lax.axis_index("x") + lax.axis_index("y")
    conv_cols = conv_w.shape[2]

    chip1 = chip.astype(jnp.int32).reshape(1)
    w_pre = _cast_into_columns(w_in[0], chip1, N_CHIPS, 256, "cast_w_in")
    wo_b, wm = _prep_small_weights(w_oa[0], w_ob[0], w_out[0], w_s[0])
    conv_w8 = jnp.concatenate([conv_w[0], jnp.zeros((5, conv_cols), F32)], axis=0)
    bs_col = b_s[0].reshape(N_HEADS, CHUNK, 1)
    bg = b_gate.reshape(2, 1, D_MODEL)

    xb, xt = _cast_and_transpose(x2, min(512, t))
    p, w_full, wo_full, cw_full = _gather_and_project(xb, w_pre, wo_b, conv_w8, min(4096, t))
    wo3 = wo_full.reshape(3, D_MODEL, D_MODEL)
    conv_w_all = jnp.transpose(cw_full[:, :3, :], (1, 0, 2)).reshape(3, D_MODEL)
    tm_a = min(512, t)
    ya = _mixer_a_forward(p, wm, bs_col, ln_v_g, ln_v_b, tm_a)
    tm_b = min(512, t)
    yb = _mixer_b_forward(p, conv_w_all, conv_b, tm_b, 512)
    tm_m = min(1024, t)
    merged, oab = _merge_forward(ya, yb, wo3, p, bg, tm_m, 512)

    drb, gx, dmerged, d_ln_g, d_ln_b, loss_part = _head(merged, wo3, x2, target, ln_g, ln_b, min(512, t))
    doab, dp, d_bg, dyab = _gate_and_branch_backward(dmerged, oab, p, bg, wo3, min(256, t))
    dp, d_ws, d_bs, d_lnv_g, d_lnv_b = _mixer_a_backward(p, dyab, wm, bs_col, ln_v_g, ln_v_b, dp, tm_a)
    dp, d_cw, d_cb = _mixer_b_backward(p, dyab, conv_w_all, conv_b, dp, tm_b, 512)

    tk = min(2048, t)
    g_in = _win_grad(xt, dp, 1024, tk)
    g_oa = _weight_grad(ya, 0, doab, 0, 1024, 1024, tk, "grad_w_oa")
    g_ob = _weight_grad(yb, 0, doab, 1, 1024, 1024, tk, "grad_w_ob")
    g_out, r_in = _weight_grad(merged, 0, drb, 0, 1024, 1024, tk, "grad_w_out", exchange=g_in)
    r_o = _exchange_halves((g_oa, g_ob, g_out))
    s_in, s_o = _add_halves(g_in, (g_oa, g_ob, g_out), r_in, r_o)
    small_part = _pack_small(d_ln_g, d_ln_b, d_bg, d_lnv_g, d_lnv_b, d_ws, d_bs, d_cw, d_cb, loss_part)
    tm_x = min(512, t)
    gx, q_in, q_o, gathered = _input_grad_and_scatter(dp, w_full, gx, s_in, s_o, small_part, tm_x)
    grad_x = _input_grad_tail(dp, w_full, gx, tm_x)
    f_in, f_o = _sum_chips(q_in, q_o, s_in, s_o)
    gsum_in, gsum_o = _share_halves(f_in, f_o)

    big = {}
    big["w_in"] = _adam_rows(w_in[0], gsum_in, m_w_in[0], v_w_in[0], 128, "adam_w_in")
    for n, (name, w, m, v) in enumerate((("w_oa", w_oa, m_w_oa, v_w_oa), ("w_ob", w_ob, m_w_ob, v_w_ob),
                                         ("w_out", w_out, m_w_out, v_w_out))):
        big[name] = _adam_rows(w[0], gsum_o, m[0], v[0], 256, "adam_" + name, g_sel=n)

    small, g_conv_w, loss_row = _adam_small(gathered, {
        "ln_g": (ln_g, m_ln_g, v_ln_g), "ln_b": (ln_b, m_ln_b, v_ln_b), "b_gate": (b_gate, m_b_gate, v_b_gate),
        "ln_v_g": (ln_v_g, m_ln_v_g, v_ln_v_g), "ln_v_b": (ln_v_b, m_ln_v_b, v_ln_v_b),
        "w_s": (w_s, m_w_s, v_w_s), "b_s": (b_s, m_b_s, v_b_s), "conv_b": (conv_b, m_conv_b, v_conv_b)})
    small["conv_w"] = _adam_conv_w(g_conv_w, chip1, conv_w, m_conv_w, v_conv_w)
    loss = loss_row[0, 0]

    order = ("w_in", "b_gate", "ln_v_g", "ln_v_b", "w_s", "b_s", "conv_w", "conv_b", "w_oa", "w_ob", "w_out",
             "ln_g", "ln_b")
    outs = [loss, grad_x[None]]
    for which in range(4):
        for name in order:
            outs.append(big[name][which][None] if name in big else small[name][which])
    return tuple(outs)
```

```python
import functools
import math

import jax
import jax.numpy as jnp
from jax import lax
from jax.experimental import pallas as pl
from jax.experimental.pallas import tpu as pltpu

F32 = jnp.float32
BF16 = jnp.bfloat16

D_MODEL = 2048
N_HEADS = 8
HEAD_DIM = D_MODEL // N_HEADS
CHUNK = 128
N_SEG = 9
N_CHIPS = 4
SHARD_COLS = N_SEG * D_MODEL // N_CHIPS
COL_BLOCK = 512
BLOCKS_PER_SHARD = SHARD_COLS // COL_BLOCK
BLOCKS_PER_SEG = D_MODEL // COL_BLOCK
SUBS = 3
SHARD_ROWS = D_MODEL // N_CHIPS
N_SLOTS = 12
BLOCK_A, BLOCK_G, BLOCK_B = 0, 2, 2
SLOT_OF_SEG = (0, 1, 2, 8, 9, 10, 11, 4, 5)
DN_ALPHA = 2.0 ** 0.25
LN_EPS = 1e-5
GELU_K = math.sqrt(2.0 / math.pi)
GELU_C = 0.044715

ADAM_LR = 0.001
ADAM_B1 = 0.9
ADAM_B2 = 0.999
ADAM_EPS = 1e-08
ADAM_WD = 0.01
ADAM_STEP = 10
ADAM_C1 = 1.0 / (1.0 - ADAM_B1 ** ADAM_STEP)
ADAM_C2 = 1.0 / (1.0 - ADAM_B2 ** ADAM_STEP)

VMEM_LIMIT = 60 * 1024 * 1024
MESH = pl.DeviceIdType.MESH
ANY = pl.BlockSpec(memory_space=pl.ANY)

SMALL_ROWS = (("ln_g", 16), ("ln_b", 16), ("b_gate", 32), ("ln_v_g", 16), ("ln_v_b", 16),
              ("w_s", 1024), ("b_s", 8), ("conv_w", 48), ("conv_b", 16), ("loss", 8))
SMALL_TOTAL = sum(r for _, r in SMALL_ROWS)


def _params(*sem):
    return pltpu.CompilerParams(dimension_semantics=sem, vmem_limit_bytes=VMEM_LIMIT)


def _sigmoid(x):
    return 1.0 / (1.0 + jnp.exp(-x))


def _gelu_gate(x, x2):
    return 1.0 / (1.0 + jnp.exp(x * ((-2.0 * GELU_K) + (-2.0 * GELU_K * GELU_C) * x2)))


def _gelu_and_grad(x):
    x2 = x * x
    s = _gelu_gate(x, x2)
    g = x * s
    dg = s + g * (1.0 - s) * ((2.0 * GELU_K) + (6.0 * GELU_K * GELU_C) * x2)
    return g, dg


def _silu_grad(sig, sz):
    return sig + sz * (1.0 - sig)


def _dot(a, b, ca=1, cb=0):
    return lax.dot_general(a, b, (((ca,), (cb,)), ((), ())), preferred_element_type=F32)


def _cast_and_transpose(x, tm):
    t, d = x.shape

    def body(x_ref, o_ref, ot_ref):
        v = x_ref[...]
        o_ref[...] = v.astype(BF16)
        ot_ref[...] = v.T.astype(BF16)

    return pl.pallas_call(
        body, name="cast_x", grid=(t // tm,),
        in_specs=[pl.BlockSpec((tm, d), lambda i: (i, 0))],
        out_specs=(pl.BlockSpec((tm, d), lambda i: (i, 0)), pl.BlockSpec((d, tm), lambda i: (0, i))),
        out_shape=(jax.ShapeDtypeStruct((t, d), BF16), jax.ShapeDtypeStruct((d, t), BF16)),
        compiler_params=_params("parallel"),
    )(x)


def _cast_into_columns(w, slot, n_slots, tm, name):
    r, c = w.shape

    def body(s_ref, w_ref, o_ref):
        o_ref[...] = w_ref[...].astype(BF16)

    return pl.pallas_call(
        body, name=name,
        grid_spec=pltpu.PrefetchScalarGridSpec(
            num_scalar_prefetch=1, grid=(r // tm,),
            in_specs=[pl.BlockSpec((tm, c), lambda i, s_ref: (i, 0))],
            out_specs=pl.BlockSpec((tm, c), lambda i, s_ref: (i, s_ref[0]))),
        out_shape=jax.ShapeDtypeStruct((r, n_slots * c), BF16),
        compiler_params=_params("parallel"),
    )(slot, w)


def _prep_small_weights(w_oa, w_ob, w_out, w_s):
    rows = w_oa.shape[0]

    def body(a_ref, b_ref, c_ref, ws_ref, wo_ref, wm_ref):
        wo_ref[0] = a_ref[...].astype(BF16)
        wo_ref[1] = b_ref[...].astype(BF16)
        wo_ref[2] = c_ref[...].astype(BF16)
        t = lax.broadcasted_iota(jnp.int32, (CHUNK, CHUNK), 0)
        s = lax.broadcasted_iota(jnp.int32, (CHUNK, CHUNK), 1)
        for h in range(N_HEADS):
            wm_ref[h] = jnp.where(s <= t, ws_ref[h], 0.0).astype(BF16)

    return pl.pallas_call(
        body, name="prep_small_weights",
        out_shape=(jax.ShapeDtypeStruct((3, rows, D_MODEL), BF16),
                   jax.ShapeDtypeStruct((N_HEADS, CHUNK, CHUNK), BF16)),
        compiler_params=_params(),
    )(w_oa, w_ob, w_out, w_s)


def _mesh_pos():
    x, y, c = lax.axis_index("x"), lax.axis_index("y"), lax.axis_index("c")
    chips = [(1 - x, y), (x, 1 - y), (1 - x, 1 - y)]
    return x, y, c, chips


def _slot_of_seg(seg):
    return jnp.where(seg < 3, seg, jnp.where(seg < 7, seg + 5, seg - 3))


def _gather_and_project(xb, w_pre, wo_b, conv_w8, tm):
    t = xb.shape[0]
    d, sc = w_pre.shape[0], w_pre.shape[1] // N_CHIPS
    rows = wo_b.shape[1]
    hd, hr = d // 2, rows // 2
    nj = BLOCKS_PER_SHARD // SUBS
    pc = nj * COL_BLOCK
    units = N_CHIPS * SUBS
    ni = t // tm
    total = units * ni * nj
    mx, my = lax.axis_index("x"), lax.axis_index("y")
    order = jnp.stack([2 * mx + my, 2 * (1 - mx) + my, 2 * mx + (1 - my),
                       2 * (1 - mx) + (1 - my)]).astype(jnp.int32)

    def body(order_ref, x_ref, wpre_ref, wo_ref, cw_ref, p_ref, wf_ref, wof_ref, cwf_ref,
             wbuf, wsem, xbuf, xsem, send_sems, recv_sems, local_sems):
        x, y, c, chips = _mesh_pos()
        me = 2 * x + y
        sibling = (x, y, 1 - c)
        u, i, j = pl.program_id(0), pl.program_id(1), pl.program_id(2)
        n = (u * ni + i) * nj + j
        m = u * ni + i

        def rows_start(m_):
            pltpu.make_async_copy(x_ref.at[pl.ds(pl.multiple_of(lax.rem(m_, ni) * tm, tm), tm), :],
                                  xbuf.at[lax.rem(m_, 2)], xsem.at[lax.rem(m_, 2)]).start()

        def chip_of(q):
            return 2 * chips[q][0] + chips[q][1]

        def piece(ref, k, cc, r):
            return ref.at[pl.ds(cc * hd, hd), pl.ds(pl.multiple_of(k * sc + r * pc, COL_BLOCK), pc)]

        def wo_half(ref4, k, cc):
            return ref4.at[:, k, pl.ds(cc * hr, hr), :]

        def rcopy(sem, src, dst, to):
            return pltpu.make_async_remote_copy(src_ref=src, dst_ref=dst, send_sem=send_sems.at[sem],
                                                recv_sem=recv_sems.at[sem], device_id=to, device_id_type=MESH)

        def w_send(q, r):
            return rcopy(q * SUBS + r, piece(wpre_ref, me, c, r), piece(wf_ref, me, c, r), (*chips[q], c))

        def w_landed(q, r):
            return rcopy(q * SUBS + r, piece(wpre_ref, me, c, r), piece(wf_ref, chip_of(q), c, r), sibling)

        def w_forward(q, r, cc):
            ref = piece(wf_ref, chip_of(q), cc, r)
            return rcopy(9 + q * SUBS + r, ref, ref, sibling)

        def wo_send(q):
            return rcopy(18 + q, wo_ref.at[:, pl.ds(c * hr, hr), :], wo_half(wof_ref, me, c), (*chips[q], c))

        def wo_landed(q):
            return rcopy(18 + q, wo_ref.at[:, pl.ds(c * hr, hr), :], wo_half(wof_ref, chip_of(q), c), sibling)

        def wo_forward(q, cc):
            ref = wo_half(wof_ref, chip_of(q), cc)
            return rcopy(21 + q, ref, ref, sibling)

        def conv_send(q):
            return rcopy(24 + q, cw_ref, cwf_ref.at[me], (*chips[q], c))

        def local_copies():
            return [pltpu.make_async_copy(wo_ref, wof_ref.at[:, me], local_sems.at[0]),
                    pltpu.make_async_copy(cw_ref, cwf_ref.at[me], local_sems.at[1])]

        def tile_start(u_, j_, slot):
            g = order_ref[u_ // SUBS] * BLOCKS_PER_SHARD + lax.rem(u_, SUBS) * nj + j_
            cols = pl.ds(pl.multiple_of(g * COL_BLOCK, COL_BLOCK), COL_BLOCK)

            @pl.when(u_ < SUBS)
            def _():
                pltpu.make_async_copy(wpre_ref.at[:, cols], wbuf.at[slot], wsem.at[slot]).start()

            @pl.when(u_ >= SUBS)
            def _():
                pltpu.make_async_copy(wf_ref.at[:, cols], wbuf.at[slot], wsem.at[slot]).start()

        def end_of(u_):
            return jnp.logical_and(u == u_, jnp.logical_and(i == ni - 1, j == nj - 1))

        @pl.when(n == 0)
        def _():
            for cp in local_copies():
                cp.start()
            for r in range(SUBS):
                for q in (0, 1):
                    w_send(q, r).start()
            for q in range(3):
                conv_send(q).start()
            tile_start(0, 0, 0)
            rows_start(0)

        def pass_on(q, r):
            w_landed(q, r).wait_recv()
            w_forward(q, r, c).start()

        for u_ in range(1, units - 1):
            @pl.when(end_of(u_))
            def _(u_=u_):
                if u_ <= SUBS:
                    for q in (0, 1):
                        pass_on(q, u_ - 1)
                if u_ == SUBS:
                    for r in range(SUBS):
                        for q in (0, 1):
                            w_send(q, r).wait_send()
                    for r in range(SUBS):
                        w_send(2, r).start()
                if 2 * SUBS - 1 <= u_ <= 3 * SUBS - 2:
                    pass_on(2, u_ - (2 * SUBS - 1))
                if u_ == 3 * SUBS - 2:
                    for r in range(SUBS):
                        w_send(2, r).wait_send()
                    for q in range(3):
                        wo_send(q).start()
                nxt = u_ + 1
                if nxt >= SUBS:
                    w_forward(nxt // SUBS - 1, nxt % SUBS, 1 - c).wait_recv()

        n1 = n + 1

        @pl.when(n1 < total)
        def _():
            tile_start(n1 // (ni * nj), lax.rem(n1, nj), lax.rem(n1, 2))

        xslot = lax.rem(m, 2)

        @pl.when(j == 0)
        def _():
            @pl.when(m + 1 < units * ni)
            def _():
                rows_start(m + 1)

            pltpu.make_async_copy(x_ref.at[pl.ds(0, tm), :], xbuf.at[xslot], xsem.at[xslot]).wait()

        slot = lax.rem(n, 2)
        pltpu.make_async_copy(wpre_ref.at[:, pl.ds(0, COL_BLOCK)], wbuf.at[slot], wsem.at[slot]).wait()
        p_ref[...] = _dot(xbuf[xslot], wbuf[slot]).astype(BF16)

        @pl.when(n == total - 1)
        def _():
            for q in range(3):
                wo_landed(q).wait_recv()
                wo_forward(q, c).start()
            for q in range(3):
                wo_forward(q, 1 - c).wait_recv()
                rcopy(24 + q, cw_ref, cwf_ref.at[chip_of(q)], sibling).wait_recv()
            for q in range(3):
                for r in range(SUBS):
                    w_forward(q, r, c).wait_send()
                wo_send(q).wait_send()
                wo_forward(q, c).wait_send()
                conv_send(q).wait_send()
            for cp in local_copies():
                cp.wait()

    def p_map(u, i, j, o):
        g = o[u // SUBS] * BLOCKS_PER_SHARD + lax.rem(u, SUBS) * nj + j
        return (_slot_of_seg(g // BLOCKS_PER_SEG), i, lax.rem(g, BLOCKS_PER_SEG))

    return pl.pallas_call(
        body, name="gather_and_project",
        grid_spec=pltpu.PrefetchScalarGridSpec(
            num_scalar_prefetch=1, grid=(units, ni, nj),
            in_specs=[ANY, ANY, ANY, ANY],
            out_specs=(pl.BlockSpec((None, tm, COL_BLOCK), p_map), ANY, ANY, ANY),
            scratch_shapes=[pltpu.VMEM((2, d, COL_BLOCK), BF16), pltpu.SemaphoreType.DMA((2,)),
                            pltpu.VMEM((2, tm, D_MODEL), BF16), pltpu.SemaphoreType.DMA((2,)),
                            pltpu.SemaphoreType.DMA((27,)), pltpu.SemaphoreType.DMA((27,)),
                            pltpu.SemaphoreType.DMA((2,))]),
        out_shape=(jax.ShapeDtypeStruct((N_SLOTS, t, D_MODEL), BF16),
                   jax.ShapeDtypeStruct((d, N_CHIPS * sc), BF16),
                   jax.ShapeDtypeStruct((3, N_CHIPS, rows, D_MODEL), BF16),
                   jax.ShapeDtypeStruct((N_CHIPS,) + conv_w8.shape, F32)),
        input_output_aliases={2: 1},
        compiler_params=pltpu.CompilerParams(dimension_semantics=("arbitrary",) * 3, vmem_limit_bytes=VMEM_LIMIT,
                                             has_side_effects=True),
    )(order, xb, w_pre, wo_b, conv_w8)


def _exchange_halves(g_o):
    hr = SHARD_ROWS // 2
    g_o4 = [g.reshape(N_CHIPS, 2, hr, D_MODEL) for g in g_o]

    def body(ga_ref, gb_ref, gc_ref, ra_ref, rb_ref, rc_ref, send_sems, recv_sems):
        x, y, c, _ = _mesh_pos()
        sibling = (x, y, 1 - c)
        cps = []
        for n, (g_ref, r_ref) in enumerate(((ga_ref, ra_ref), (gb_ref, rb_ref), (gc_ref, rc_ref))):
            cps.append(pltpu.make_async_remote_copy(src_ref=g_ref.at[:, 1 - c], dst_ref=r_ref,
                                                    send_sem=send_sems.at[n], recv_sem=recv_sems.at[n],
                                                    device_id=sibling, device_id_type=MESH))
        for cp in cps:
            cp.start()
        for cp in cps:
            cp.wait()

    o_shape = jax.ShapeDtypeStruct((N_CHIPS, hr, D_MODEL), BF16)
    return pl.pallas_call(
        body, name="rs_exchange_halves",
        in_specs=[ANY] * 3, out_specs=(ANY,) * 3,
        out_shape=(o_shape, o_shape, o_shape),
        scratch_shapes=[pltpu.SemaphoreType.DMA((3,)), pltpu.SemaphoreType.DMA((3,))],
        compiler_params=pltpu.CompilerParams(has_side_effects=True),
    )(*g_o4)


def _add_halves(g_in, g_o, r_in, r_o):
    d, c9 = g_in.shape
    hd = d // 2
    hr = SHARD_ROWS // 2
    core = lax.axis_index("c").astype(jnp.int32).reshape(1)
    tm = min(512, hd)
    nb = hd // tm

    def body_in(c_ref, g_ref, r_ref, o_ref):
        o_ref[...] = (g_ref[...].astype(F32) + r_ref[...].astype(F32)).astype(BF16)

    s_in = pl.pallas_call(
        body_in, name="rs_add_halves_in",
        grid_spec=pltpu.PrefetchScalarGridSpec(
            num_scalar_prefetch=1, grid=(N_CHIPS, nb),
            in_specs=[pl.BlockSpec((tm, SHARD_COLS), lambda k, i, c_ref: (c_ref[0] * nb + i, k)),
                      pl.BlockSpec((tm, SHARD_COLS), lambda k, i, c_ref: (i, k))],
            out_specs=pl.BlockSpec((None, tm, SHARD_COLS), lambda k, i, c_ref: (k, i, 0))),
        out_shape=jax.ShapeDtypeStruct((N_CHIPS, hd, SHARD_COLS), BF16),
        compiler_params=_params("parallel", "parallel"),
    )(core, g_in, r_in)

    g_o4 = [g.reshape(N_CHIPS, 2, hr, D_MODEL) for g in g_o]

    def body_o(c_ref, ga_ref, gb_ref, gc_ref, ra_ref, rb_ref, rc_ref, o_ref):
        for n, (g_ref, r_ref) in enumerate(((ga_ref, ra_ref), (gb_ref, rb_ref), (gc_ref, rc_ref))):
            o_ref[n] = (g_ref[...].astype(F32) + r_ref[...].astype(F32)).astype(BF16)

    gspec = pl.BlockSpec((None, None, hr, D_MODEL), lambda k, c_ref: (k, c_ref[0], 0, 0))
    rspec = pl.BlockSpec((None, hr, D_MODEL), lambda k, c_ref: (k, 0, 0))
    s_o = pl.pallas_call(
        body_o, name="rs_add_halves_o",
        grid_spec=pltpu.PrefetchScalarGridSpec(
            num_scalar_prefetch=1, grid=(N_CHIPS,),
            in_specs=[gspec] * 3 + [rspec] * 3,
            out_specs=pl.BlockSpec((3, None, hr, D_MODEL), lambda k, c_ref: (0, k, 0, 0))),
        out_shape=jax.ShapeDtypeStruct((3, N_CHIPS, hr, D_MODEL), BF16),
        compiler_params=_params("parallel"),
    )(core, *g_o4, *r_o)
    return s_in, s_o


def _sum_chips(r_in, r_o, s_in, s_o):
    _, hd, sc = r_in.shape
    hr = r_o.shape[2]
    tm = min(256, hd)
    nb = hd // tm
    pos = jnp.stack([2 * lax.axis_index("x") + lax.axis_index("y"), lax.axis_index("c")]).astype(jnp.int32)

    def chip_sum(pos_ref, r_ref, s_ref):
        acc = None
        for k in range(N_CHIPS):
            term = jnp.where(pos_ref[0] == k, s_ref[...], r_ref[k]).astype(F32)
            acc = term if acc is None else acc + term
        return acc

    def body_in(pos_ref, r_ref, s_ref, o_ref):
        o_ref[...] = chip_sum(pos_ref, r_ref, s_ref)

    f_in = pl.pallas_call(
        body_in, name="rs_sum_chips_in",
        grid_spec=pltpu.PrefetchScalarGridSpec(
            num_scalar_prefetch=1, grid=(nb,),
            in_specs=[pl.BlockSpec((N_CHIPS, tm, sc), lambda i, p: (0, i, 0)),
                      pl.BlockSpec((None, tm, sc), lambda i, p: (p[0], i, 0))],
            out_specs=pl.BlockSpec((tm, sc), lambda i, p: (p[1] * nb + i, 0))),
        out_shape=jax.ShapeDtypeStruct((2 * hd, sc), F32),
        compiler_params=_params("parallel"),
    )(pos, r_in, s_in)

    def body_o(pos_ref, r_ref, s_ref, o_ref):
        o_ref[...] = chip_sum(pos_ref, r_ref, s_ref)

    f_o = pl.pallas_call(
        body_o, name="rs_sum_chips_o",
        grid_spec=pltpu.PrefetchScalarGridSpec(
            num_scalar_prefetch=1, grid=(3,),
            in_specs=[pl.BlockSpec((N_CHIPS, None, hr, D_MODEL), lambda n, p: (0, n, 0, 0)),
                      pl.BlockSpec((None, None, hr, D_MODEL), lambda n, p: (n, p[0], 0, 0))],
            out_specs=pl.BlockSpec((None, hr, D_MODEL), lambda n, p: (n, p[1], 0))),
        out_shape=jax.ShapeDtypeStruct((3, 2 * hr, D_MODEL), F32),
        compiler_params=_params("parallel"),
    )(pos, r_o, s_o)
    return f_in, f_o


def _share_halves(f_in, f_o):
    hd, sc = f_in.shape[0] // 2, f_in.shape[1]
    hr = f_o.shape[1] // 2

    def body(fi_ref, fo_ref, gi_ref, go_ref, send_sems, recv_sems):
        x, y, c, _ = _mesh_pos()
        sibling = (x, y, 1 - c)

        def halves(cc):
            rows_i, rows_o = pl.ds(cc * hd, hd), pl.ds(cc * hr, hr)
            return (fi_ref.at[rows_i, :], gi_ref.at[rows_i, :]), (fo_ref.at[:, rows_o, :], go_ref.at[:, rows_o, :])

        def copies(cc):
            return [pltpu.make_async_remote_copy(src_ref=src, dst_ref=dst, send_sem=send_sems.at[n],
                                                 recv_sem=recv_sems.at[n], device_id=sibling, device_id_type=MESH)
                    for n, (src, dst) in enumerate(halves(cc))]

        sends = copies(c)
        for cp in sends:
            cp.start()
        for cp in copies(1 - c):
            cp.wait_recv()
        for cp in sends:
            cp.wait_send()

    return pl.pallas_call(
        body, name="rs_share_halves",
        in_specs=[ANY, ANY], out_specs=(ANY, ANY),
        out_shape=(jax.ShapeDtypeStruct(f_in.shape, F32), jax.ShapeDtypeStruct(f_o.shape, F32)),
        scratch_shapes=[pltpu.SemaphoreType.DMA((2,)), pltpu.SemaphoreType.DMA((2,))],
        input_output_aliases={0: 0, 1: 1},
        compiler_params=pltpu.CompilerParams(has_side_effects=True),
    )(f_in, f_o)


def _merge_forward(ya, yb, wo3, p, bg, tm, tn):
    t = ya.shape[0]

    def body(ya_ref, yb_ref, wa_ref, wb_ref, g_ref, bg_ref, m_ref, oab_ref):
        oa = _dot(ya_ref[...], wa_ref[...])
        ob = _dot(yb_ref[...], wb_ref[...])
        ga = _sigmoid(g_ref[0].astype(F32) + bg_ref[0])
        gb = _sigmoid(g_ref[1].astype(F32) + bg_ref[1])
        m_ref[...] = (ga * oa + gb * ob).astype(BF16)
        oab_ref[0] = oa.astype(BF16)
        oab_ref[1] = ob.astype(BF16)

    return pl.pallas_call(
        body, name="merge_forward", grid=(t // tm, D_MODEL // tn),
        in_specs=[pl.BlockSpec((tm, D_MODEL), lambda i, j: (i, 0)),
                  pl.BlockSpec((tm, D_MODEL), lambda i, j: (i, 0)),
                  pl.BlockSpec((None, D_MODEL, tn), lambda i, j: (0, 0, j)),
                  pl.BlockSpec((None, D_MODEL, tn), lambda i, j: (1, 0, j)),
                  pl.BlockSpec((2, tm, tn), lambda i, j: (BLOCK_G, i, j)),
                  pl.BlockSpec((2, 1, tn), lambda i, j: (0, 0, j))],
        out_specs=(pl.BlockSpec((tm, tn), lambda i, j: (i, j)),
                   pl.BlockSpec((2, tm, tn), lambda i, j: (0, i, j))),
        out_shape=(jax.ShapeDtypeStruct((t, D_MODEL), BF16), jax.ShapeDtypeStruct((2, t, D_MODEL), BF16)),
        compiler_params=_params("parallel", "parallel"),
    )(ya, yb, wo3, wo3, p, bg)


HEAD_ROWS = 256


def _head(merged, wo3, x, target, ln_g, ln_b, tm):
    t = x.shape[0]
    inv_d = 1.0 / D_MODEL

    def body(m_ref, w_ref, x_ref, t_ref, g_ref, b_ref, dr_ref, gx_ref, dm_ref, dg_ref, db_ref, loss_ref):
        i = pl.program_id(0)

        @pl.when(i == 0)
        def _():
            dg_ref[...] = jnp.zeros_like(dg_ref)
            db_ref[...] = jnp.zeros_like(db_ref)
            loss_ref[...] = jnp.zeros_like(loss_ref)

        w = w_ref[...]
        g = g_ref[...]
        tiles = [slice(r0, r0 + HEAD_ROWS) for r0 in range(0, tm, HEAD_ROWS)]
        firsts = [_dot(m_ref[rows, :], w) for rows in tiles]
        for rows, out in zip(tiles, firsts):
            r = DN_ALPHA * x_ref[rows, :] + out
            mu = jnp.mean(r, axis=-1, keepdims=True)
            xc = r - mu
            var = jnp.mean(xc * xc, axis=-1, keepdims=True)
            rstd = lax.rsqrt(var + LN_EPS)
            xhat = xc * rstd
            e = xhat * g + b_ref[...] - t_ref[rows, :]
            se = jnp.sum(jnp.sum(e * e, axis=1, keepdims=True), axis=0, keepdims=True)
            loss_ref[...] += jnp.broadcast_to((0.5 * inv_d) * se, loss_ref.shape)
            dy = e * inv_d
            db_ref[...] += jnp.sum(dy, axis=0, keepdims=True)
            dg_ref[...] += jnp.sum(dy * xhat, axis=0, keepdims=True)
            dxh = dy * g
            m1 = jnp.mean(dxh, axis=-1, keepdims=True)
            m2 = jnp.mean(dxh * xhat, axis=-1, keepdims=True)
            dr = rstd * (dxh - m1 - xhat * m2)
            gx_ref[rows, :] = DN_ALPHA * dr
            drb = dr.astype(BF16)
            dr_ref[rows, :] = drb
            dm_ref[rows, :] = _dot(drb, w, 1, 1).astype(BF16)

    row = pl.BlockSpec((tm, D_MODEL), lambda i: (i, 0))
    vec = pl.BlockSpec((1, D_MODEL), lambda i: (0, 0))
    return pl.pallas_call(
        body, name="head", grid=(t // tm,),
        in_specs=[row, pl.BlockSpec((None, D_MODEL, D_MODEL), lambda i: (2, 0, 0), pipeline_mode=pl.Buffered(1)),
                  row, row, vec, vec],
        out_specs=(row, row, row, vec, vec, pl.BlockSpec((1, 128), lambda i: (0, 0))),
        out_shape=(jax.ShapeDtypeStruct((t, D_MODEL), BF16), jax.ShapeDtypeStruct((t, D_MODEL), F32),
                   jax.ShapeDtypeStruct((t, D_MODEL), BF16), jax.ShapeDtypeStruct((1, D_MODEL), F32),
                   jax.ShapeDtypeStruct((1, D_MODEL), F32), jax.ShapeDtypeStruct((1, 128), F32)),
        compiler_params=_params("arbitrary"),
    )(merged, wo3, x, target, ln_g, ln_b)


def _gate_and_branch_backward(dmerged, oab, p, bg, wo3, tm):
    t = dmerged.shape[0]

    def body(dm_ref, oab_ref, g_ref, bg_ref, wa_ref, wb_ref, do_ref, dpg_ref, dbg_ref, dy_ref):
        @pl.when(pl.program_id(0) == 0)
        def _():
            dbg_ref[...] = jnp.zeros_like(dbg_ref)

        dm = dm_ref[...].astype(F32)
        for n, w_ref in enumerate((wa_ref, wb_ref)):
            gate = _sigmoid(g_ref[n].astype(F32) + bg_ref[n])
            d_o = (dm * gate).astype(BF16)
            do_ref[n] = d_o
            dgate = dm * oab_ref[n].astype(F32) * gate * (1.0 - gate)
            dpg_ref[n] = dgate.astype(BF16)
            dbg_ref[n] += jnp.sum(dgate, axis=0, keepdims=True)
            dy_ref[n] = _dot(d_o, w_ref[...], 1, 1).astype(BF16)

    pair = pl.BlockSpec((2, tm, D_MODEL), lambda i: (0, i, 0))
    gates = pl.BlockSpec((2, tm, D_MODEL), lambda i: (BLOCK_G, i, 0))
    vec = pl.BlockSpec((2, 1, D_MODEL), lambda i: (0, 0, 0))

    def weight(n):
        return pl.BlockSpec((None, D_MODEL, D_MODEL), lambda i: (n, 0, 0), pipeline_mode=pl.Buffered(1))

    pair_shape = jax.ShapeDtypeStruct((2, t, D_MODEL), BF16)
    return pl.pallas_call(
        body, name="gate_and_branch_backward", grid=(t // tm,),
        in_specs=[pl.BlockSpec((tm, D_MODEL), lambda i: (i, 0)), pair, gates, vec, weight(0), weight(1)],
        out_specs=(pair, gates, vec, pair),
        out_shape=(pair_shape, jax.ShapeDtypeStruct((N_SLOTS, t, D_MODEL), BF16),
                   jax.ShapeDtypeStruct((2, 1, D_MODEL), F32), pair_shape),
        compiler_params=_params("arbitrary"),
    )(dmerged, oab, p, bg, wo3, wo3)


def _weight_grad(a, a_sel, b, b_sel, tm, tn, tk, name, exchange=None):
    t = a.shape[-2]
    nk = t // tk
    ni, nj = D_MODEL // tm, D_MODEL // tn

    def body(a_ref, b_ref, *rest):
        if exchange is None:
            o_ref, acc_ref = rest
        else:
            g_ref, o_ref, r_ref, acc_ref, send_sem, recv_sem = rest
        i, j, k = pl.program_id(0), pl.program_id(1), pl.program_id(2)

        if exchange is not None:
            x, y, c, _ = _mesh_pos()
            hd = exchange.shape[0] // 2
            swap = pltpu.make_async_remote_copy(src_ref=g_ref.at[pl.ds((1 - c) * hd, hd), :], dst_ref=r_ref,
                                                send_sem=send_sem, recv_sem=recv_sem,
                                                device_id=(x, y, 1 - c), device_id_type=MESH)

            @pl.when(jnp.logical_and(i == 0, jnp.logical_and(j == 0, k == 0)))
            def _():
                swap.start()

        @pl.when(k == 0)
        def _():
            acc_ref[...] = jnp.zeros_like(acc_ref)

        acc_ref[...] += _dot(a_ref[...], b_ref[...], 0, 0)

        @pl.when(k == nk - 1)
        def _():
            o_ref[...] = acc_ref[...].astype(BF16)

        if exchange is not None:
            @pl.when(jnp.logical_and(i == ni - 1, jnp.logical_and(j == nj - 1, k == nk - 1)))
            def _():
                swap.wait()

    def spec(arr, sel, width, which):
        if arr.ndim == 2:
            return pl.BlockSpec((tk, width), lambda i, j, k: (k, (i, j)[which]))
        return pl.BlockSpec((None, tk, width), lambda i, j, k: (sel, k, (i, j)[which]))

    o_spec = pl.BlockSpec((tm, tn), lambda i, j, k: (i, j))
    o_shape = jax.ShapeDtypeStruct((D_MODEL, D_MODEL), BF16)
    if exchange is None:
        return pl.pallas_call(
            body, name=name, grid=(ni, nj, nk),
            in_specs=[spec(a, a_sel, tm, 0), spec(b, b_sel, tn, 1)],
            out_specs=o_spec, out_shape=o_shape,
            scratch_shapes=[pltpu.VMEM((tm, tn), F32)],
            compiler_params=_params("parallel", "parallel", "arbitrary"),
        )(a, b)
    return pl.pallas_call(
        body, name=name, grid=(ni, nj, nk),
        in_specs=[spec(a, a_sel, tm, 0), spec(b, b_sel, tn, 1), ANY],
        out_specs=(o_spec, ANY),
        out_shape=(o_shape, jax.ShapeDtypeStruct((exchange.shape[0] // 2, exchange.shape[1]), BF16)),
        scratch_shapes=[pltpu.VMEM((tm, tn), F32), pltpu.SemaphoreType.DMA, pltpu.SemaphoreType.DMA],
        compiler_params=pltpu.CompilerParams(dimension_semantics=("arbitrary",) * 3, vmem_limit_bytes=VMEM_LIMIT,
                                             has_side_effects=True),
    )(a, b, exchange)


def _win_grad(xt, dp, tn, tk):
    _, t, _ = dp.shape
    nk = t // tk
    per_seg = D_MODEL // tn
    nj = N_SEG * per_seg

    def body(x_ref, dp_ref, o_ref, acc_ref):
        k = pl.program_id(1)

        @pl.when(k == 0)
        def _():
            acc_ref[...] = jnp.zeros_like(acc_ref)

        acc_ref[...] += _dot(x_ref[...], dp_ref[...])

        @pl.when(k == nk - 1)
        def _():
            o_ref[...] = acc_ref[...].astype(BF16)

    return pl.pallas_call(
        body, name="grad_w_in", grid=(nj, nk),
        in_specs=[pl.BlockSpec((D_MODEL, tk), lambda j, k: (0, k)),
                  pl.BlockSpec((None, tk, tn), lambda j, k: (_slot_of_seg(j // per_seg), k, j % per_seg))],
        out_specs=pl.BlockSpec((D_MODEL, tn), lambda j, k: (0, j)),
        out_shape=jax.ShapeDtypeStruct((D_MODEL, N_SEG * D_MODEL), BF16),
        scratch_shapes=[pltpu.VMEM((D_MODEL, tn), F32)],
        compiler_params=_params("parallel", "arbitrary"),
    )(xt, dp)


def _input_grad_and_scatter(dp, w_full, gx, s_in, s_o, small, tm):
    _, t, _ = dp.shape
    ni, nk = t // tm, N_SEG - TAIL_SEGS
    _, hd, sc = s_in.shape
    hr = s_o.shape[2]

    def body(dp_ref, w_ref, gx_ref, si_ref, so_ref, sm_ref, o_ref, ri_ref, ro_ref, ga_ref,
             send_sems, recv_sems, local_sem):
        i, k = pl.program_id(0), pl.program_id(1)
        x, y, c, chips = _mesh_pos()
        me = 2 * x + y
        dev = 4 * x + 2 * y + c

        def peer(r):
            return (x ^ ((r >> 2) & 1), y ^ ((r >> 1) & 1), c ^ (r & 1))

        def sends():
            cps = []
            for q, (cx, cy) in enumerate(chips):
                dest = 2 * cx + cy
                cps.append(pltpu.make_async_remote_copy(src_ref=si_ref.at[dest], dst_ref=ri_ref.at[me],
                                                        send_sem=send_sems.at[q], recv_sem=recv_sems.at[q],
                                                        device_id=(cx, cy, c), device_id_type=MESH))
                cps.append(pltpu.make_async_remote_copy(src_ref=so_ref.at[:, dest], dst_ref=ro_ref.at[me],
                                                        send_sem=send_sems.at[3 + q], recv_sem=recv_sems.at[3 + q],
                                                        device_id=(cx, cy, c), device_id_type=MESH))
            for r in range(1, 8):
                cps.append(pltpu.make_async_remote_copy(src_ref=sm_ref, dst_ref=ga_ref.at[dev],
                                                        send_sem=send_sems.at[5 + r], recv_sem=recv_sems.at[5 + r],
                                                        device_id=peer(r), device_id_type=MESH))
            return cps

        own_small = pltpu.make_async_copy(sm_ref, ga_ref.at[dev], local_sem)

        @pl.when(jnp.logical_and(i == 0, k == 0))
        def _():
            for cp in sends():
                cp.start()
            own_small.start()

        @pl.when(k == 0)
        def _():
            o_ref[...] = gx_ref[...]

        o_ref[...] += _dot(dp_ref[...], w_ref[...], 1, 1)

        @pl.when(jnp.logical_and(i == ni - 1, k == nk - 1))
        def _():
            for q, (cx, cy) in enumerate(chips):
                frm = 2 * cx + cy
                pltpu.make_async_remote_copy(src_ref=si_ref.at[frm], dst_ref=ri_ref.at[frm], send_sem=send_sems.at[q],
                                             recv_sem=recv_sems.at[q], device_id=(x, y, c),
                                             device_id_type=MESH).wait_recv()
                pltpu.make_async_remote_copy(src_ref=so_ref.at[:, frm], dst_ref=ro_ref.at[frm],
                                             send_sem=send_sems.at[3 + q], recv_sem=recv_sems.at[3 + q],
                                             device_id=(x, y, c), device_id_type=MESH).wait_recv()
            for r in range(1, 8):
                px, py, pc = peer(r)
                pltpu.make_async_remote_copy(src_ref=sm_ref, dst_ref=ga_ref.at[4 * px + 2 * py + pc],
                                             send_sem=send_sems.at[5 + r], recv_sem=recv_sems.at[5 + r],
                                             device_id=(x, y, c), device_id_type=MESH).wait_recv()
            for cp in sends():
                cp.wait_send()
            own_small.wait()

    return pl.pallas_call(
        body, name="grad_x_and_scatter", grid=(ni, nk),
        in_specs=[pl.BlockSpec((None, tm, D_MODEL), lambda i, k: (_slot_of_seg(k), i, 0)),
                  pl.BlockSpec((D_MODEL, D_MODEL), lambda i, k: (0, k)),
                  pl.BlockSpec((tm, D_MODEL), lambda i, k: (i, 0), pipeline_mode=pl.Buffered(1)), ANY, ANY, ANY],
        out_specs=(pl.BlockSpec((tm, D_MODEL), lambda i, k: (i, 0)), ANY, ANY, ANY),
        out_shape=(jax.ShapeDtypeStruct((t, D_MODEL), F32),
                   jax.ShapeDtypeStruct((N_CHIPS, hd, sc), BF16),
                   jax.ShapeDtypeStruct((N_CHIPS, 3, hr, D_MODEL), BF16),
                   jax.ShapeDtypeStruct((8,) + small.shape, small.dtype)),
        scratch_shapes=[pltpu.SemaphoreType.DMA((13,)), pltpu.SemaphoreType.DMA((13,)), pltpu.SemaphoreType.DMA],
        input_output_aliases={2: 0},
        compiler_params=pltpu.CompilerParams(dimension_semantics=("arbitrary", "arbitrary"),
                                             vmem_limit_bytes=VMEM_LIMIT, has_side_effects=True),
    )(dp, w_full, gx, s_in, s_o, small)


TAIL_SEGS = 2


def _input_grad_tail_and_adam(dp, w_full, partial, w, g, m, v, steps):
    _, t, _ = dp.shape
    tm = t // steps
    r, c = w.shape
    ra = r // steps
    segs = tuple(range(N_SEG - TAIL_SEGS, N_SEG))

    def body(*refs):
        dp_refs = refs[:TAIL_SEGS]
        w_refs = refs[TAIL_SEGS:2 * TAIL_SEGS]
        part_ref, aw_ref, ag_ref, am_ref, av_ref, o_ref, go_ref, d_ref, mo_ref, vo_ref = refs[2 * TAIL_SEGS:]
        acc = part_ref[...]
        for dp_ref, w_ref in zip(dp_refs, w_refs):
            acc = acc + _dot(dp_ref[...], w_ref[...], 1, 1)
        o_ref[...] = acc
        gv = ag_ref[...]
        d, mn, vn = _adam_math(aw_ref[...], gv, am_ref[...], av_ref[...])
        go_ref[...] = gv
        d_ref[...] = d
        mo_ref[...] = mn
        vo_ref[...] = vn

    row = pl.BlockSpec((tm, D_MODEL), lambda i: (i, 0))
    arow = pl.BlockSpec((ra, c), lambda i: (i, 0))
    ashape = jax.ShapeDtypeStruct((r, c), F32)
    in_specs = [pl.BlockSpec((None, tm, D_MODEL), functools.partial(lambda s, i: (SLOT_OF_SEG[s], i, 0), s))
                for s in segs]
    in_specs += [pl.BlockSpec((D_MODEL, D_MODEL), functools.partial(lambda s, i: (0, s), s),
                              pipeline_mode=pl.Buffered(1)) for s in segs]
    in_specs += [row, arow, arow, arow, arow]
    res = pl.pallas_call(
        body, name="grad_x_tail_and_adam_w_in", grid=(steps,),
        in_specs=in_specs,
        out_specs=(row, arow, arow, arow, arow),
        out_shape=(jax.ShapeDtypeStruct((t, D_MODEL), F32), ashape, ashape, ashape, ashape),
        compiler_params=_params("parallel"),
    )(*([dp] * TAIL_SEGS), *([w_full] * TAIL_SEGS), partial, w, g, m, v)
    return res[0], res[1:]


def _sgu_chunk_forward(u, v, z, wm, bs, lng, lnb):
    ug, dug = _gelu_and_grad(u)
    vg, dvg = _gelu_and_grad(v)
    mu = jnp.mean(vg, axis=-1, keepdims=True)
    xc = vg - mu
    var = jnp.mean(xc * xc, axis=-1, keepdims=True)
    rstd = lax.rsqrt(var + LN_EPS)
    vhat = xc * rstd
    vln = (vhat * lng + lnb).astype(BF16)
    mixed = _dot(wm, vln) + bs
    sig = _sigmoid(z)
    return ug, dug, dvg, rstd, vhat, vln, mixed, sig


def _mixer_a_forward(p_a, wm, bs_col, ln_v_g, ln_v_b, tm):
    t = p_a.shape[1]

    def body(p_ref, wm_ref, bs_ref, g_ref, b_ref, o_ref):
        wm_v, bs_v, lng, lnb = wm_ref[...], bs_ref[...], g_ref[...], b_ref[...]

        def chunk(ci, carry):
            rows = pl.ds(pl.multiple_of(ci * CHUNK, CHUNK), CHUNK)
            u = p_ref[0, rows, :].astype(F32)
            v = p_ref[1, rows, :].astype(F32)
            z = p_ref[2, rows, :].astype(F32)
            ug, _, _, _, _, _, mixed, sig = _sgu_chunk_forward(u, v, z, wm_v, bs_v, lng, lnb)
            o_ref[rows, :] = (ug * mixed * (z * sig)).astype(BF16)
            return carry

        lax.fori_loop(0, tm // CHUNK, chunk, 0, unroll=True)

    return pl.pallas_call(
        body, name="mixer_a_forward", grid=(t // tm, N_HEADS),
        in_specs=[pl.BlockSpec((3, tm, HEAD_DIM), lambda i, h: (0, i, h)),
                  pl.BlockSpec((None, CHUNK, CHUNK), lambda i, h: (h, 0, 0)),
                  pl.BlockSpec((None, CHUNK, 1), lambda i, h: (h, 0, 0)),
                  pl.BlockSpec((1, HEAD_DIM), lambda i, h: (0, h)),
                  pl.BlockSpec((1, HEAD_DIM), lambda i, h: (0, h))],
        out_specs=pl.BlockSpec((tm, HEAD_DIM), lambda i, h: (i, h)),
        out_shape=jax.ShapeDtypeStruct((t, D_MODEL), BF16),
        compiler_params=_params("parallel", "parallel"),
    )(p_a, wm, bs_col, ln_v_g, ln_v_b)


def _mixer_a_backward(p_a, dyab, wm, bs_col, ln_v_g, ln_v_b, dp, tm):
    t = p_a.shape[1]

    def body(p_ref, dy_ref, wm_ref, bs_ref, g_ref, b_ref, dp_in, dp_ref, dws_ref, dbs_ref, dg_ref, db_ref):
        @pl.when(pl.program_id(1) == 0)
        def _():
            dws_ref[...] = jnp.zeros_like(dws_ref)
            dbs_ref[...] = jnp.zeros_like(dbs_ref)
            dg_ref[...] = jnp.zeros_like(dg_ref)
            db_ref[...] = jnp.zeros_like(db_ref)

        wm_v, bs_v, lng, lnb = wm_ref[...], bs_ref[...], g_ref[...], b_ref[...]
        causal = (lax.broadcasted_iota(jnp.int32, (CHUNK, CHUNK), 1)
                  <= lax.broadcasted_iota(jnp.int32, (CHUNK, CHUNK), 0))

        def chunk(ci, carry):
            rows = pl.ds(pl.multiple_of(ci * CHUNK, CHUNK), CHUNK)
            u = p_ref[0, rows, :].astype(F32)
            v = p_ref[1, rows, :].astype(F32)
            z = p_ref[2, rows, :].astype(F32)
            dy = dy_ref[rows, :].astype(F32)
            ug, dug, dvg, rstd, vhat, vln, mixed, sig = _sgu_chunk_forward(u, v, z, wm_v, bs_v, lng, lnb)
            sz = z * sig
            dmixed = dy * ug * sz
            dp_ref[0, rows, :] = (dy * mixed * sz * dug).astype(BF16)
            dp_ref[2, rows, :] = (dy * ug * mixed * _silu_grad(sig, sz)).astype(BF16)
            dbs_ref[...] += jnp.sum(dmixed, axis=1, keepdims=True)
            dmb = dmixed.astype(BF16)
            dws_ref[...] += jnp.where(causal, _dot(dmb, vln, 1, 1), 0.0)
            dvln = _dot(wm_v, dmb, 0, 0)
            db_ref[...] += jnp.sum(dvln, axis=0, keepdims=True)
            dg_ref[...] += jnp.sum(dvln * vhat, axis=0, keepdims=True)
            dvh = dvln * lng
            m1 = jnp.mean(dvh, axis=-1, keepdims=True)
            m2 = jnp.mean(dvh * vhat, axis=-1, keepdims=True)
            dp_ref[1, rows, :] = (rstd * (dvh - m1 - vhat * m2) * dvg).astype(BF16)
            return carry

        lax.fori_loop(0, tm // CHUNK, chunk, 0, unroll=True)

    return pl.pallas_call(
        body, name="mixer_a_backward", grid=(N_HEADS, t // tm),
        in_specs=[pl.BlockSpec((3, tm, HEAD_DIM), lambda h, i: (0, i, h)),
                  pl.BlockSpec((None, tm, HEAD_DIM), lambda h, i: (0, i, h)),
                  pl.BlockSpec((None, CHUNK, CHUNK), lambda h, i: (h, 0, 0)),
                  pl.BlockSpec((None, CHUNK, 1), lambda h, i: (h, 0, 0)),
                  pl.BlockSpec((1, HEAD_DIM), lambda h, i: (0, h)),
                  pl.BlockSpec((1, HEAD_DIM), lambda h, i: (0, h)), ANY],
        out_specs=(pl.BlockSpec((3, tm, HEAD_DIM), lambda h, i: (BLOCK_A, i, h)),
                   pl.BlockSpec((None, CHUNK, CHUNK), lambda h, i: (h, 0, 0)),
                   pl.BlockSpec((None, CHUNK, 1), lambda h, i: (h, 0, 0)),
                   pl.BlockSpec((1, HEAD_DIM), lambda h, i: (0, h)),
                   pl.BlockSpec((1, HEAD_DIM), lambda h, i: (0, h))),
        out_shape=(jax.ShapeDtypeStruct(dp.shape, BF16),
                   jax.ShapeDtypeStruct((N_HEADS, CHUNK, CHUNK), F32),
                   jax.ShapeDtypeStruct((N_HEADS, CHUNK, 1), F32),
                   jax.ShapeDtypeStruct((1, D_MODEL), F32), jax.ShapeDtypeStruct((1, D_MODEL), F32)),
        input_output_aliases={6: 0},
        compiler_params=_params("parallel", "arbitrary"),
    )(p_a, dyab, wm, bs_col, ln_v_g, ln_v_b, dp)


HALO = 16


def _conv_taps(h, halo_h, tm):
    row = lax.broadcasted_iota(jnp.int32, h.shape, 0)
    last1 = halo_h[HALO - 1:HALO, :]
    last2 = halo_h[HALO - 2:HALO - 1, :]
    h1 = jnp.where(row == 0, last1, pltpu.roll(h, 1, 0))
    h2 = jnp.where(row == 0, last2, jnp.where(row == 1, last1, pltpu.roll(h, 2, 0)))
    return h1, h2


def _mixer_b_forward(p_b, conv_w, conv_b, tm, tc):
    t = p_b.shape[1]

    def body(p_ref, halo_ref, w_ref, b_ref, o_ref):
        valid = (pl.program_id(1) > 0).astype(F32)
        h = p_ref[1].astype(F32) * p_ref[0].astype(F32)
        halo_h = halo_ref[1].astype(F32) * halo_ref[0].astype(F32) * valid
        h1, h2 = _conv_taps(h, halo_h, tm)
        w = w_ref[...]
        conv = b_ref[...] + w[0:1, :] * h2 + w[1:2, :] * h1 + w[2:3, :] * h
        z = p_ref[3].astype(F32)
        o_ref[...] = (p_ref[2].astype(F32) * conv * (z * _sigmoid(z))).astype(BF16)

    steps = tm // HALO
    return pl.pallas_call(
        body, name="mixer_b_forward", grid=(D_MODEL // tc, t // tm),
        in_specs=[pl.BlockSpec((4, tm, tc), lambda j, i: (BLOCK_B, i, j)),
                  pl.BlockSpec((4, HALO, tc), lambda j, i: (BLOCK_B, jnp.maximum(i * steps - 1, 0), j)),
                  pl.BlockSpec((3, tc), lambda j, i: (0, j)),
                  pl.BlockSpec((1, tc), lambda j, i: (0, j))],
        out_specs=pl.BlockSpec((tm, tc), lambda j, i: (i, j)),
        out_shape=jax.ShapeDtypeStruct((t, D_MODEL), BF16),
        compiler_params=_params("parallel", "parallel"),
    )(p_b, p_b, conv_w, conv_b)


def _mixer_b_backward(p_b, dyab, conv_w, conv_b, dp, tm, tc):
    t = p_b.shape[1]
    n = t // tm

    def body(p_ref, halo_ref, dy_ref, w_ref, b_ref, dp_in, dp_ref, dw_ref, db_ref, next_ref):
        ii = pl.program_id(1)

        @pl.when(ii == 0)
        def _():
            dw_ref[...] = jnp.zeros_like(dw_ref)
            db_ref[...] = jnp.zeros_like(db_ref)
            next_ref[...] = jnp.zeros_like(next_ref)

        valid = (ii < n - 1).astype(F32)
        xb = p_ref[0].astype(F32)
        cb = p_ref[1].astype(F32)
        bb = p_ref[2].astype(F32)
        z = p_ref[3].astype(F32)
        h = cb * xb
        halo_h = halo_ref[1].astype(F32) * halo_ref[0].astype(F32) * valid
        h1, h2 = _conv_taps(h, halo_h, tm)
        w = w_ref[...]
        w0, w1, w2 = w[0:1, :], w[1:2, :], w[2:3, :]
        conv = b_ref[...] + w0 * h2 + w1 * h1 + w2 * h
        sig = _sigmoid(z)
        sz = z * sig
        dy = dy_ref[...].astype(F32)
        dconv = dy * bb * sz
        dp_ref[2] = (dy * conv * sz).astype(BF16)
        dp_ref[3] = (dy * bb * conv * _silu_grad(sig, sz)).astype(BF16)
        db_ref[...] += jnp.sum(dconv, axis=0, keepdims=True)
        dw_ref[0:1, :] += jnp.sum(dconv * h2, axis=0, keepdims=True)
        dw_ref[1:2, :] += jnp.sum(dconv * h1, axis=0, keepdims=True)
        dw_ref[2:3, :] += jnp.sum(dconv * h, axis=0, keepdims=True)
        row = lax.broadcasted_iota(jnp.int32, h.shape, 0)
        nxt = next_ref[...]
        n0, n1 = nxt[0:1, :], nxt[1:2, :]
        d1 = jnp.where(row == tm - 1, n0, pltpu.roll(dconv, tm - 1, 0))
        d2 = jnp.where(row == tm - 1, n1, jnp.where(row == tm - 2, n0, pltpu.roll(dconv, tm - 2, 0)))
        dh = w2 * dconv + w1 * d1 + w0 * d2
        dp_ref[0] = (dh * cb).astype(BF16)
        dp_ref[1] = (dh * xb).astype(BF16)
        next_ref[...] = dconv[0:8, :]

    steps = tm // HALO
    return pl.pallas_call(
        body, name="mixer_b_backward", grid=(D_MODEL // tc, n),
        in_specs=[pl.BlockSpec((4, tm, tc), lambda j, ii: (BLOCK_B, n - 1 - ii, j)),
                  pl.BlockSpec((4, HALO, tc), lambda j, ii: (BLOCK_B, jnp.maximum((n - 1 - ii) * steps - 1, 0), j)),
                  pl.BlockSpec((None, tm, tc), lambda j, ii: (1, n - 1 - ii, j)),
                  pl.BlockSpec((3, tc), lambda j, ii: (0, j)),
                  pl.BlockSpec((1, tc), lambda j, ii: (0, j)), ANY],
        out_specs=(pl.BlockSpec((4, tm, tc), lambda j, ii: (BLOCK_B, n - 1 - ii, j)),
                   pl.BlockSpec((3, tc), lambda j, ii: (0, j)),
                   pl.BlockSpec((1, tc), lambda j, ii: (0, j))),
        out_shape=(jax.ShapeDtypeStruct(dp.shape, BF16),
                   jax.ShapeDtypeStruct((3, D_MODEL), F32), jax.ShapeDtypeStruct((1, D_MODEL), F32)),
        scratch_shapes=[pltpu.VMEM((8, tc), F32)],
        input_output_aliases={5: 0},
        compiler_params=_params("parallel", "arbitrary"),
    )(p_b, p_b, dyab, conv_w, conv_b, dp)


def _adam_math(w, g, m, v):
    m = ADAM_B1 * m + (1.0 - ADAM_B1) * g
    v = ADAM_B2 * v + (1.0 - ADAM_B2) * (g * g)
    delta = -ADAM_LR * ((m * ADAM_C1) / (jnp.sqrt(v * ADAM_C2) + ADAM_EPS) + ADAM_WD * w)
    return delta, m, v


def _adam_rows(w, g, m, v, tm, name, g_sel=None):
    r, c = w.shape

    def body(w_ref, g_ref, m_ref, v_ref, go_ref, d_ref, mo_ref, vo_ref):
        g = g_ref[...]
        d, mn, vn = _adam_math(w_ref[...], g, m_ref[...], v_ref[...])
        go_ref[...] = g
        d_ref[...] = d
        mo_ref[...] = mn
        vo_ref[...] = vn

    spec = pl.BlockSpec((tm, c), lambda i: (i, 0))
    g_spec = spec if g_sel is None else pl.BlockSpec((None, tm, c), lambda i: (g_sel, i, 0))
    shape = jax.ShapeDtypeStruct((r, c), F32)
    return pl.pallas_call(
        body, name=name, grid=(r // tm,),
        in_specs=[spec, g_spec, spec, spec], out_specs=(spec,) * 4, out_shape=(shape,) * 4,
        compiler_params=_params("parallel"),
    )(w, g, m, v)


SMALL_ROW0 = {name: sum(r for _, r in SMALL_ROWS[:i]) for i, (name, _) in enumerate(SMALL_ROWS)}
LANE_MAJOR = ("ln_g", "ln_b", "b_gate", "ln_v_g", "ln_v_b", "conv_b")


def _lane_pieces(n):
    return [(q, slice(q * 128, (q + 1) * 128)) for q in range(n // 128)]


def _pack_small(d_ln_g, d_ln_b, d_bg, d_lnv_g, d_lnv_b, d_ws, d_bs, d_cw, d_cb, loss_part):
    def body(lg, lb, bg, vg, vb, ws, bs, cw, cb, loss, o_ref):
        def put(row0, vec):
            for q, cols in _lane_pieces(vec.shape[1]):
                o_ref[row0 + q:row0 + q + 1, :] = vec[:, cols]

        put(SMALL_ROW0["ln_g"], lg[...])
        put(SMALL_ROW0["ln_b"], lb[...])
        for n in range(2):
            put(SMALL_ROW0["b_gate"] + n * (D_MODEL // 128), bg[n])
        put(SMALL_ROW0["ln_v_g"], vg[...])
        put(SMALL_ROW0["ln_v_b"], vb[...])
        for h in range(N_HEADS):
            o_ref[SMALL_ROW0["w_s"] + h * CHUNK:SMALL_ROW0["w_s"] + (h + 1) * CHUNK, :] = ws[h]
        o_ref[SMALL_ROW0["b_s"]:SMALL_ROW0["b_s"] + N_HEADS, :] = bs[...]
        for c in range(3):
            put(SMALL_ROW0["conv_w"] + c * (D_MODEL // 128), cw[c:c + 1, :])
        put(SMALL_ROW0["conv_b"], cb[...])
        o_ref[SMALL_ROW0["loss"]:SMALL_ROW0["loss"] + 8, :] = jnp.broadcast_to(loss[...], (8, 128))

    return pl.pallas_call(
        body, name="pack_small", out_shape=jax.ShapeDtypeStruct((SMALL_TOTAL, 128), F32), compiler_params=_params(),
    )(d_ln_g, d_ln_b, d_bg, d_lnv_g, d_lnv_b, d_ws, d_bs.reshape(N_HEADS, CHUNK), d_cw, d_cb, loss_part)


def _adam_small(gathered, params):
    names = list(params)
    flat = [a for n in names for a in params[n]]

    def body(*refs):
        ga_ref = refs[0]
        ins = refs[1:1 + 3 * len(names)]
        outs = refs[1 + 3 * len(names):-1]
        gs_ref = refs[-1]
        g = ga_ref[0]
        for k in range(1, 8):
            g = g + ga_ref[k]
        gs_ref[...] = g
        for i, name in enumerate(names):
            w_ref, m_ref, v_ref = ins[3 * i:3 * i + 3]
            o_refs = outs[4 * i:4 * i + 4]
            row0 = SMALL_ROW0[name]
            if name in LANE_MAJOR:
                pieces = [((slice(None), cols), slice(row0 + q, row0 + q + 1))
                          for q, cols in _lane_pieces(w_ref.shape[1])]
            elif name == "w_s":
                pieces = [((0, h), slice(row0 + h * CHUNK, row0 + (h + 1) * CHUNK)) for h in range(N_HEADS)]
            else:
                pieces = [((0,), slice(row0, row0 + N_HEADS))]
            for idx, rows in pieces:
                gp = gs_ref[rows, :]
                res = (gp,) + _adam_math(w_ref[idx], gp, m_ref[idx], v_ref[idx])
                for o_ref, val in zip(o_refs, res):
                    o_ref[idx] = val
        gcw_ref, loss_ref = outs[-2:]
        for c in range(3):
            for q, cols in _lane_pieces(D_MODEL):
                r = SMALL_ROW0["conv_w"] + c * (D_MODEL // 128) + q
                gcw_ref[c:c + 1, cols] = gs_ref[r:r + 1, :]
        loss_ref[...] = gs_ref[SMALL_ROW0["loss"]:SMALL_ROW0["loss"] + 1, :]

    out_shape = [jax.ShapeDtypeStruct(params[n][0].shape, F32) for n in names for _ in range(4)]
    out_shape += [jax.ShapeDtypeStruct((3, D_MODEL), F32), jax.ShapeDtypeStruct((1, 128), F32)]
    res = pl.pallas_call(
        body, name="adam_small", out_shape=tuple(out_shape),
        scratch_shapes=[pltpu.VMEM((SMALL_TOTAL, 128), F32)], compiler_params=_params(),
    )(gathered, *flat)
    return {n: res[4 * i:4 * i + 4] for i, n in enumerate(names)}, res[-2], res[-1]


def _adam_conv_w(g_all, chip, w, m, v):
    cols = w.shape[2]

    def body(c_ref, g_ref, w_ref, m_ref, v_ref, go_ref, d_ref, mo_ref, vo_ref):
        g = g_ref[...]
        d, mn, vn = _adam_math(w_ref[...], g, m_ref[...], v_ref[...])
        go_ref[...] = g
        d_ref[...] = d
        mo_ref[...] = mn
        vo_ref[...] = vn

    own = pl.BlockSpec((None, 3, cols), lambda i, c_ref: (0, 0, 0))
    return pl.pallas_call(
        body, name="adam_conv_w",
        grid_spec=pltpu.PrefetchScalarGridSpec(
            num_scalar_prefetch=1, grid=(1,),
            in_specs=[pl.BlockSpec((3, cols), lambda i, c_ref: (0, c_ref[0])), own, own, own],
            out_specs=(own,) * 4),
        out_shape=(jax.ShapeDtypeStruct(w.shape, F32),) * 4,
        compiler_params=_params("arbitrary"),
    )(chip, g_all, w, m, v)


def kernel(x, w_in, b_gate, ln_v_g, ln_v_b, w_s, b_s, conv_w, conv_b, w_oa, w_ob, w_out, ln_g, ln_b, loss_target, m_w_in, m_b_gate, m_ln_v_g, m_ln_v_b, m_w_s, m_b_s, m_conv_w, m_conv_b, m_w_oa, m_w_ob, m_w_out, m_ln_g, m_ln_b, v_w_in, v_b_gate, v_ln_v_g, v_ln_v_b, v_w_s, v_b_s, v_conv_w, v_conv_b, v_w_oa, v_w_ob, v_w_out, v_ln_g, v_ln_b):
    t = x.shape[1]
    x2 = x[0]
    target = loss_target[0]
    chip = 2 * lax.axis_index("x") + lax.axis_index("y")
    conv_cols = conv_w.shape[2]

    chip1 = chip.astype(jnp.int32).reshape(1)
    w_pre = _cast_into_columns(w_in[0], chip1, N_CHIPS, 256, "cast_w_in")
    wo_b, wm = _prep_small_weights(w_oa[0], w_ob[0], w_out[0], w_s[0])
    conv_w8 = jnp.concatenate([conv_w[0], jnp.zeros((5, conv_cols), F32)], axis=0)
    bs_col = b_s[0].reshape(N_HEADS, CHUNK, 1)
    bg = b_gate.reshape(2, 1, D_MODEL)

    xb, xt = _cast_and_transpose(x2, min(512, t))
    p, w_full, wo_full, cw_full = _gather_and_project(xb, w_pre, wo_b, conv_w8, min(4096, t))
    wo3 = wo_full.reshape(3, D_MODEL, D_MODEL)
    conv_w_all = jnp.transpose(cw_full[:, :3, :], (1, 0, 2)).reshape(3, D_MODEL)
    tm_a = min(512, t)
    ya = _mixer_a_forward(p, wm, bs_col, ln_v_g, ln_v_b, tm_a)
    tm_b = min(512, t)
    yb = _mixer_b_forward(p, conv_w_all, conv_b, tm_b, 512)
    tm_m = min(1024, t)
    merged, oab = _merge_forward(ya, yb, wo3, p, bg, tm_m, 512)

    drb, gx, dmerged, d_ln_g, d_ln_b, loss_part = _head(merged, wo3, x2, target, ln_g, ln_b, min(512, t))
    doab, dp, d_bg, dyab = _gate_and_branch_backward(dmerged, oab, p, bg, wo3, min(256, t))
    dp, d_ws, d_bs, d_lnv_g, d_lnv_b = _mixer_a_backward(p, dyab, wm, bs_col, ln_v_g, ln_v_b, dp, tm_a)
    dp, d_cw, d_cb = _mixer_b_backward(p, dyab, conv_w_all, conv_b, dp, tm_b, 512)

    tk = min(2048, t)
    g_in = _win_grad(xt, dp, 1024, tk)
    g_oa = _weight_grad(ya, 0, doab, 0, 1024, 1024, tk, "grad_w_oa")
    g_ob = _weight_grad(yb, 0, doab, 1, 1024, 1024, tk, "grad_w_ob")
    g_out, r_in = _weight_grad(merged, 0, drb, 0, 1024, 1024, tk, "grad_w_out", exchange=g_in)
    r_o = _exchange_halves((g_oa, g_ob, g_out))
    s_in, s_o = _add_halves(g_in, (g_oa, g_ob, g_out), r_in, r_o)
    small_part = _pack_small(d_ln_g, d_ln_b, d_bg, d_lnv_g, d_lnv_b, d_ws, d_bs, d_cw, d_cb, loss_part)
    gx, q_in, q_o, gathered = _input_grad_and_scatter(dp, w_full, gx, s_in, s_o, small_part, min(1024, t))
    f_in, f_o = _sum_chips(q_in, q_o, s_in, s_o)
    gsum_in, gsum_o = _share_halves(f_in, f_o)

    big = {}
    grad_x, big["w_in"] = _input_grad_tail_and_adam(dp, w_full, gx, w_in[0], gsum_in, m_w_in[0], v_w_in[0],
                                                    min(32, t // 128))
    for n, (name, w, m, v) in enumerate((("w_oa", w_oa, m_w_oa, v_w_oa), ("w_ob", w_ob, m_w_ob, v_w_ob),
                                         ("w_out", w_out, m_w_out, v_w_out))):
        big[name] = _adam_rows(w[0], gsum_o, m[0], v[0], 256, "adam_" + name, g_sel=n)

    small, g_conv_w, loss_row = _adam_small(gathered, {
        "ln_g": (ln_g, m_ln_g, v_ln_g), "ln_b": (ln_b, m_ln_b, v_ln_b), "b_gate": (b_gate, m_b_gate, v_b_gate),
        "ln_v_g": (ln_v_g, m_ln_v_g, v_ln_v_g), "ln_v_b": (ln_v_b, m_ln_v_b, v_ln_v_b),
        "w_s": (w_s, m_w_s, v_w_s), "b_s": (b_s, m_b_s, v_b_s), "conv_b": (conv_b, m_conv_b, v_conv_b)})
    small["conv_w"] = _adam_conv_w(g_conv_w, chip1, conv_w, m_conv_w, v_conv_w)
    loss = loss_row[0, 0]

    order = ("w_in", "b_gate", "ln_v_g", "ln_v_b", "w_s", "b_s", "conv_w", "conv_b", "w_oa", "w_ob", "w_out",
             "ln_g", "ln_b")
    outs = [loss, grad_x[None]]
    for which in range(4):
        for name in order:
            outs.append(big[name][which][None] if name in big else small[name][which])
    return tuple(outs)
```

```python
import functools
import math

import jax
import jax.numpy as jnp
from jax import lax
from jax.experimental import pallas as pl
from jax.experimental.pallas import tpu as pltpu

F32 = jnp.float32
BF16 = jnp.bfloat16

D_MODEL = 2048
N_HEADS = 8
HEAD_DIM = D_MODEL // N_HEADS
CHUNK = 128
N_SEG = 9
N_CHIPS = 4
SHARD_COLS = N_SEG * D_MODEL // N_CHIPS
COL_BLOCK = 512
BLOCKS_PER_SHARD = SHARD_COLS // COL_BLOCK
BLOCKS_PER_SEG = D_MODEL // COL_BLOCK
SUBS = 3
SHARD_ROWS = D_MODEL // N_CHIPS
N_SLOTS = 12
BLOCK_A, BLOCK_G, BLOCK_B = 0, 2, 2
SLOT_OF_SEG = (0, 1, 2, 8, 9, 10, 11, 4, 5)
DN_ALPHA = 2.0 ** 0.25
LN_EPS = 1e-5
GELU_K = math.sqrt(2.0 / math.pi)
GELU_C = 0.044715

ADAM_LR = 0.001
ADAM_B1 = 0.9
ADAM_B2 = 0.999
ADAM_EPS = 1e-08
ADAM_WD = 0.01
ADAM_STEP = 10
ADAM_C1 = 1.0 / (1.0 - ADAM_B1 ** ADAM_STEP)
ADAM_C2 = 1.0 / (1.0 - ADAM_B2 ** ADAM_STEP)

VMEM_LIMIT = 60 * 1024 * 1024
MESH = pl.DeviceIdType.MESH
ANY = pl.BlockSpec(memory_space=pl.ANY)

SMALL_ROWS = (("ln_g", 16), ("ln_b", 16), ("b_gate", 32), ("ln_v_g", 16), ("ln_v_b", 16),
              ("w_s", 1024), ("b_s", 8), ("conv_w", 48), ("conv_b", 16), ("loss", 8))
SMALL_TOTAL = sum(r for _, r in SMALL_ROWS)


def _params(*sem):
    return pltpu.CompilerParams(dimension_semantics=sem, vmem_limit_bytes=VMEM_LIMIT)


def _sigmoid(x):
    return 1.0 / (1.0 + jnp.exp(-x))


def _gelu_gate(x, x2):
    return 1.0 / (1.0 + jnp.exp(x * ((-2.0 * GELU_K) + (-2.0 * GELU_K * GELU_C) * x2)))


def _gelu_and_grad(x):
    x2 = x * x
    s = _gelu_gate(x, x2)
    g = x * s
    dg = s + g * (1.0 - s) * ((2.0 * GELU_K) + (6.0 * GELU_K * GELU_C) * x2)
    return g, dg


def _silu_grad(sig, sz):
    return sig + sz * (1.0 - sig)


def _dot(a, b, ca=1, cb=0):
    return lax.dot_general(a, b, (((ca,), (cb,)), ((), ())), preferred_element_type=F32)


def _cast_and_transpose(x, tm):
    t, d = x.shape

    def body(x_ref, o_ref, ot_ref):
        v = x_ref[...]
        o_ref[...] = v.astype(BF16)
        ot_ref[...] = v.T.astype(BF16)

    return pl.pallas_call(
        body, name="cast_x", grid=(t // tm,),
        in_specs=[pl.BlockSpec((tm, d), lambda i: (i, 0))],
        out_specs=(pl.BlockSpec((tm, d), lambda i: (i, 0)), pl.BlockSpec((d, tm), lambda i: (0, i))),
        out_shape=(jax.ShapeDtypeStruct((t, d), BF16), jax.ShapeDtypeStruct((d, t), BF16)),
        compiler_params=_params("parallel"),
    )(x)


def _cast_into_columns(w, slot, n_slots, tm, name):
    r, c = w.shape

    def body(s_ref, w_ref, o_ref):
        o_ref[...] = w_ref[...].astype(BF16)

    return pl.pallas_call(
        body, name=name,
        grid_spec=pltpu.PrefetchScalarGridSpec(
            num_scalar_prefetch=1, grid=(r // tm,),
            in_specs=[pl.BlockSpec((tm, c), lambda i, s_ref: (i, 0))],
            out_specs=pl.BlockSpec((tm, c), lambda i, s_ref: (i, s_ref[0]))),
        out_shape=jax.ShapeDtypeStruct((r, n_slots * c), BF16),
        compiler_params=_params("parallel"),
    )(slot, w)


def _prep_small_weights(w_oa, w_ob, w_out, w_s):
    rows = w_oa.shape[0]

    def body(a_ref, b_ref, c_ref, ws_ref, wo_ref, wm_ref):
        wo_ref[0] = a_ref[...].astype(BF16)
        wo_ref[1] = b_ref[...].astype(BF16)
        wo_ref[2] = c_ref[...].astype(BF16)
        t = lax.broadcasted_iota(jnp.int32, (CHUNK, CHUNK), 0)
        s = lax.broadcasted_iota(jnp.int32, (CHUNK, CHUNK), 1)
        for h in range(N_HEADS):
            wm_ref[h] = jnp.where(s <= t, ws_ref[h], 0.0).astype(BF16)

    return pl.pallas_call(
        body, name="prep_small_weights",
        out_shape=(jax.ShapeDtypeStruct((3, rows, D_MODEL), BF16),
                   jax.ShapeDtypeStruct((N_HEADS, CHUNK, CHUNK), BF16)),
        compiler_params=_params(),
    )(w_oa, w_ob, w_out, w_s)


def _mesh_pos():
    x, y, c = lax.axis_index("x"), lax.axis_index("y"), lax.axis_index("c")
    chips = [(1 - x, y), (x, 1 - y), (1 - x, 1 - y)]
    return x, y, c, chips


def _slot_of_seg(seg):
    return jnp.where(seg < 3, seg, jnp.where(seg < 7, seg + 5, seg - 3))


def _gather_and_project(xb, w_pre, wo_b, conv_w8, tm):
    t = xb.shape[0]
    d, sc = w_pre.shape[0], w_pre.shape[1] // N_CHIPS
    rows = wo_b.shape[1]
    hd, hr = d // 2, rows // 2
    nj = BLOCKS_PER_SHARD // SUBS
    pc = nj * COL_BLOCK
    units = N_CHIPS * SUBS
    ni = t // tm
    total = units * ni * nj
    mx, my = lax.axis_index("x"), lax.axis_index("y")
    order = jnp.stack([2 * mx + my, 2 * (1 - mx) + my, 2 * mx + (1 - my),
                       2 * (1 - mx) + (1 - my)]).astype(jnp.int32)

    def body(order_ref, x_ref, wpre_ref, wo_ref, cw_ref, p_ref, wf_ref, wof_ref, cwf_ref,
             wbuf, wsem, xbuf, xsem, send_sems, recv_sems, local_sems):
        x, y, c, chips = _mesh_pos()
        me = 2 * x + y
        sibling = (x, y, 1 - c)
        u, i, j = pl.program_id(0), pl.program_id(1), pl.program_id(2)
        n = (u * ni + i) * nj + j
        m = u * ni + i

        def rows_start(m_):
            pltpu.make_async_copy(x_ref.at[pl.ds(pl.multiple_of(lax.rem(m_, ni) * tm, tm), tm), :],
                                  xbuf.at[lax.rem(m_, 2)], xsem.at[lax.rem(m_, 2)]).start()

        def chip_of(q):
            return 2 * chips[q][0] + chips[q][1]

        def piece(ref, k, cc, r):
            return ref.at[pl.ds(cc * hd, hd), pl.ds(pl.multiple_of(k * sc + r * pc, COL_BLOCK), pc)]

        def wo_half(ref4, k, cc):
            return ref4.at[:, k, pl.ds(cc * hr, hr), :]

        def rcopy(sem, src, dst, to):
            return pltpu.make_async_remote_copy(src_ref=src, dst_ref=dst, send_sem=send_sems.at[sem],
                                                recv_sem=recv_sems.at[sem], device_id=to, device_id_type=MESH)

        def w_send(q, r):
            return rcopy(q * SUBS + r, piece(wpre_ref, me, c, r), piece(wf_ref, me, c, r), (*chips[q], c))

        def w_landed(q, r):
            return rcopy(q * SUBS + r, piece(wpre_ref, me, c, r), piece(wf_ref, chip_of(q), c, r), sibling)

        def w_forward(q, r, cc):
            ref = piece(wf_ref, chip_of(q), cc, r)
            return rcopy(9 + q * SUBS + r, ref, ref, sibling)

        def wo_send(q):
            return rcopy(18 + q, wo_ref.at[:, pl.ds(c * hr, hr), :], wo_half(wof_ref, me, c), (*chips[q], c))

        def wo_landed(q):
            return rcopy(18 + q, wo_ref.at[:, pl.ds(c * hr, hr), :], wo_half(wof_ref, chip_of(q), c), sibling)

        def wo_forward(q, cc):
            ref = wo_half(wof_ref, chip_of(q), cc)
            return rcopy(21 + q, ref, ref, sibling)

        def conv_send(q):
            return rcopy(24 + q, cw_ref, cwf_ref.at[me], (*chips[q], c))

        def local_copies():
            return [pltpu.make_async_copy(wo_ref, wof_ref.at[:, me], local_sems.at[0]),
                    pltpu.make_async_copy(cw_ref, cwf_ref.at[me], local_sems.at[1])]

        def tile_start(u_, j_, slot):
            g = order_ref[u_ // SUBS] * BLOCKS_PER_SHARD + lax.rem(u_, SUBS) * nj + j_
            cols = pl.ds(pl.multiple_of(g * COL_BLOCK, COL_BLOCK), COL_BLOCK)

            @pl.when(u_ < SUBS)
            def _():
                pltpu.make_async_copy(wpre_ref.at[:, cols], wbuf.at[slot], wsem.at[slot]).start()

            @pl.when(u_ >= SUBS)
            def _():
                pltpu.make_async_copy(wf_ref.at[:, cols], wbuf.at[slot], wsem.at[slot]).start()

        def end_of(u_):
            return jnp.logical_and(u == u_, jnp.logical_and(i == ni - 1, j == nj - 1))

        @pl.when(n == 0)
        def _():
            for cp in local_copies():
                cp.start()
            for r in range(SUBS):
                for q in (0, 1):
                    w_send(q, r).start()
            for q in range(3):
                conv_send(q).start()
            tile_start(0, 0, 0)
            rows_start(0)

        def pass_on(q, r):
            w_landed(q, r).wait_recv()
            w_forward(q, r, c).start()

        for u_ in range(1, units - 1):
            @pl.when(end_of(u_))
            def _(u_=u_):
                if u_ <= SUBS:
                    for q in (0, 1):
                        pass_on(q, u_ - 1)
                if u_ == SUBS:
                    for r in range(SUBS):
                        for q in (0, 1):
                            w_send(q, r).wait_send()
                    for r in range(SUBS):
                        w_send(2, r).start()
                if 2 * SUBS - 1 <= u_ <= 3 * SUBS - 2:
                    pass_on(2, u_ - (2 * SUBS - 1))
                if u_ == 3 * SUBS - 2:
                    for r in range(SUBS):
                        w_send(2, r).wait_send()
                    for q in range(3):
                        wo_send(q).start()
                nxt = u_ + 1
                if nxt >= SUBS:
                    w_forward(nxt // SUBS - 1, nxt % SUBS, 1 - c).wait_recv()

        n1 = n + 1

        @pl.when(n1 < total)
        def _():
            tile_start(n1 // (ni * nj), lax.rem(n1, nj), lax.rem(n1, 2))

        xslot = lax.rem(m, 2)

        @pl.when(j == 0)
        def _():
            @pl.when(m + 1 < units * ni)
            def _():
                rows_start(m + 1)

            pltpu.make_async_copy(x_ref.at[pl.ds(0, tm), :], xbuf.at[xslot], xsem.at[xslot]).wait()

        slot = lax.rem(n, 2)
        pltpu.make_async_copy(wpre_ref.at[:, pl.ds(0, COL_BLOCK)], wbuf.at[slot], wsem.at[slot]).wait()
        p_ref[...] = _dot(xbuf[xslot], wbuf[slot]).astype(BF16)

        @pl.when(n == total - 1)
        def _():
            for q in range(3):
                wo_landed(q).wait_recv()
                wo_forward(q, c).start()
            for q in range(3):
                wo_forward(q, 1 - c).wait_recv()
                rcopy(24 + q, cw_ref, cwf_ref.at[chip_of(q)], sibling).wait_recv()
            for q in range(3):
                for r in range(SUBS):
                    w_forward(q, r, c).wait_send()
                wo_send(q).wait_send()
                wo_forward(q, c).wait_send()
                conv_send(q).wait_send()
            for cp in local_copies():
                cp.wait()

    def p_map(u, i, j, o):
        g = o[u // SUBS] * BLOCKS_PER_SHARD + lax.rem(u, SUBS) * nj + j
        return (_slot_of_seg(g // BLOCKS_PER_SEG), i, lax.rem(g, BLOCKS_PER_SEG))

    return pl.pallas_call(
        body, name="gather_and_project",
        grid_spec=pltpu.PrefetchScalarGridSpec(
            num_scalar_prefetch=1, grid=(units, ni, nj),
            in_specs=[ANY, ANY, ANY, ANY],
            out_specs=(pl.BlockSpec((None, tm, COL_BLOCK), p_map), ANY, ANY, ANY),
            scratch_shapes=[pltpu.VMEM((2, d, COL_BLOCK), BF16), pltpu.SemaphoreType.DMA((2,)),
                            pltpu.VMEM((2, tm, D_MODEL), BF16), pltpu.SemaphoreType.DMA((2,)),
                            pltpu.SemaphoreType.DMA((27,)), pltpu.SemaphoreType.DMA((27,)),
                            pltpu.SemaphoreType.DMA((2,))]),
        out_shape=(jax.ShapeDtypeStruct((N_SLOTS, t, D_MODEL), BF16),
                   jax.ShapeDtypeStruct((d, N_CHIPS * sc), BF16),
                   jax.ShapeDtypeStruct((3, N_CHIPS, rows, D_MODEL), BF16),
                   jax.ShapeDtypeStruct((N_CHIPS,) + conv_w8.shape, F32)),
        input_output_aliases={2: 1},
        compiler_params=pltpu.CompilerParams(dimension_semantics=("arbitrary",) * 3, vmem_limit_bytes=VMEM_LIMIT,
                                             has_side_effects=True),
    )(order, xb, w_pre, wo_b, conv_w8)


def _exchange_halves(g_o):
    hr = SHARD_ROWS // 2
    g_o4 = [g.reshape(N_CHIPS, 2, hr, D_MODEL) for g in g_o]

    def body(ga_ref, gb_ref, gc_ref, ra_ref, rb_ref, rc_ref, send_sems, recv_sems):
        x, y, c, _ = _mesh_pos()
        sibling = (x, y, 1 - c)
        cps = []
        for n, (g_ref, r_ref) in enumerate(((ga_ref, ra_ref), (gb_ref, rb_ref), (gc_ref, rc_ref))):
            cps.append(pltpu.make_async_remote_copy(src_ref=g_ref.at[:, 1 - c], dst_ref=r_ref,
                                                    send_sem=send_sems.at[n], recv_sem=recv_sems.at[n],
                                                    device_id=sibling, device_id_type=MESH))
        for cp in cps:
            cp.start()
        for cp in cps:
            cp.wait()

    o_shape = jax.ShapeDtypeStruct((N_CHIPS, hr, D_MODEL), BF16)
    return pl.pallas_call(
        body, name="rs_exchange_halves",
        in_specs=[ANY] * 3, out_specs=(ANY,) * 3,
        out_shape=(o_shape, o_shape, o_shape),
        scratch_shapes=[pltpu.SemaphoreType.DMA((3,)), pltpu.SemaphoreType.DMA((3,))],
        compiler_params=pltpu.CompilerParams(has_side_effects=True),
    )(*g_o4)


def _add_halves(g_in, g_o, r_in, r_o):
    d, c9 = g_in.shape
    hd = d // 2
    hr = SHARD_ROWS // 2
    core = lax.axis_index("c").astype(jnp.int32).reshape(1)
    tm = min(512, hd)
    nb = hd // tm

    def body_in(c_ref, g_ref, r_ref, o_ref):
        o_ref[...] = (g_ref[...].astype(F32) + r_ref[...].astype(F32)).astype(BF16)

    s_in = pl.pallas_call(
        body_in, name="rs_add_halves_in",
        grid_spec=pltpu.PrefetchScalarGridSpec(
            num_scalar_prefetch=1, grid=(N_CHIPS, nb),
            in_specs=[pl.BlockSpec((tm, SHARD_COLS), lambda k, i, c_ref: (c_ref[0] * nb + i, k)),
                      pl.BlockSpec((tm, SHARD_COLS), lambda k, i, c_ref: (i, k))],
            out_specs=pl.BlockSpec((None, tm, SHARD_COLS), lambda k, i, c_ref: (k, i, 0))),
        out_shape=jax.ShapeDtypeStruct((N_CHIPS, hd, SHARD_COLS), BF16),
        compiler_params=_params("parallel", "parallel"),
    )(core, g_in, r_in)

    g_o4 = [g.reshape(N_CHIPS, 2, hr, D_MODEL) for g in g_o]

    def body_o(c_ref, ga_ref, gb_ref, gc_ref, ra_ref, rb_ref, rc_ref, o_ref):
        for n, (g_ref, r_ref) in enumerate(((ga_ref, ra_ref), (gb_ref, rb_ref), (gc_ref, rc_ref))):
            o_ref[n] = (g_ref[...].astype(F32) + r_ref[...].astype(F32)).astype(BF16)

    gspec = pl.BlockSpec((None, None, hr, D_MODEL), lambda k, c_ref: (k, c_ref[0], 0, 0))
    rspec = pl.BlockSpec((None, hr, D_MODEL), lambda k, c_ref: (k, 0, 0))
    s_o = pl.pallas_call(
        body_o, name="rs_add_halves_o",
        grid_spec=pltpu.PrefetchScalarGridSpec(
            num_scalar_prefetch=1, grid=(N_CHIPS,),
            in_specs=[gspec] * 3 + [rspec] * 3,
            out_specs=pl.BlockSpec((3, None, hr, D_MODEL), lambda k, c_ref: (0, k, 0, 0))),
        out_shape=jax.ShapeDtypeStruct((3, N_CHIPS, hr, D_MODEL), BF16),
        compiler_params=_params("parallel"),
    )(core, *g_o4, *r_o)
    return s_in, s_o


def _sum_chips(r_in, r_o, s_in, s_o):
    _, hd, sc = r_in.shape
    hr = r_o.shape[2]
    tm = min(256, hd)
    nb = hd // tm
    pos = jnp.stack([2 * lax.axis_index("x") + lax.axis_index("y"), lax.axis_index("c")]).astype(jnp.int32)

    def chip_sum(pos_ref, r_ref, s_ref):
        acc = None
        for k in range(N_CHIPS):
            term = jnp.where(pos_ref[0] == k, s_ref[...], r_ref[k]).astype(F32)
            acc = term if acc is None else acc + term
        return acc

    def body_in(pos_ref, r_ref, s_ref, o_ref):
        o_ref[...] = chip_sum(pos_ref, r_ref, s_ref)

    f_in = pl.pallas_call(
        body_in, name="rs_sum_chips_in",
        grid_spec=pltpu.PrefetchScalarGridSpec(
            num_scalar_prefetch=1, grid=(nb,),
            in_specs=[pl.BlockSpec((N_CHIPS, tm, sc), lambda i, p: (0, i, 0)),
                      pl.BlockSpec((None, tm, sc), lambda i, p: (p[0], i, 0))],
            out_specs=pl.BlockSpec((tm, sc), lambda i, p: (p[1] * nb + i, 0))),
        out_shape=jax.ShapeDtypeStruct((2 * hd, sc), F32),
        compiler_params=_params("parallel"),
    )(pos, r_in, s_in)

    def body_o(pos_ref, r_ref, s_ref, o_ref):
        o_ref[...] = chip_sum(pos_ref, r_ref, s_ref)

    f_o = pl.pallas_call(
        body_o, name="rs_sum_chips_o",
        grid_spec=pltpu.PrefetchScalarGridSpec(
            num_scalar_prefetch=1, grid=(3,),
            in_specs=[pl.BlockSpec((N_CHIPS, None, hr, D_MODEL), lambda n, p: (0, n, 0, 0)),
                      pl.BlockSpec((None, None, hr, D_MODEL), lambda n, p: (n, p[0], 0, 0))],
            out_specs=pl.BlockSpec((None, hr, D_MODEL), lambda n, p: (n, p[1], 0))),
        out_shape=jax.ShapeDtypeStruct((3, 2 * hr, D_MODEL), F32),
        compiler_params=_params("parallel"),
    )(pos, r_o, s_o)
    return f_in, f_o


def _share_halves(f_in, f_o):
    hd, sc = f_in.shape[0] // 2, f_in.shape[1]
    hr = f_o.shape[1] // 2

    def body(fi_ref, fo_ref, gi_ref, go_ref, send_sems, recv_sems):
        x, y, c, _ = _mesh_pos()
        sibling = (x, y, 1 - c)

        def halves(cc):
            rows_i, rows_o = pl.ds(cc * hd, hd), pl.ds(cc * hr, hr)
            return (fi_ref.at[rows_i, :], gi_ref.at[rows_i, :]), (fo_ref.at[:, rows_o, :], go_ref.at[:, rows_o, :])

        def copies(cc):
            return [pltpu.make_async_remote_copy(src_ref=src, dst_ref=dst, send_sem=send_sems.at[n],
                                                 recv_sem=recv_sems.at[n], device_id=sibling, device_id_type=MESH)
                    for n, (src, dst) in enumerate(halves(cc))]

        sends = copies(c)
        for cp in sends:
            cp.start()
        for cp in copies(1 - c):
            cp.wait_recv()
        for cp in sends:
            cp.wait_send()

    return pl.pallas_call(
        body, name="rs_share_halves",
        in_specs=[ANY, ANY], out_specs=(ANY, ANY),
        out_shape=(jax.ShapeDtypeStruct(f_in.shape, F32), jax.ShapeDtypeStruct(f_o.shape, F32)),
        scratch_shapes=[pltpu.SemaphoreType.DMA((2,)), pltpu.SemaphoreType.DMA((2,))],
        input_output_aliases={0: 0, 1: 1},
        compiler_params=pltpu.CompilerParams(has_side_effects=True),
    )(f_in, f_o)


def _merge_forward(ya, yb, wo3, p, bg, tm, tn):
    t = ya.shape[0]

    def body(ya_ref, yb_ref, wa_ref, wb_ref, g_ref, bg_ref, m_ref, oab_ref):
        oa = _dot(ya_ref[...], wa_ref[...])
        ob = _dot(yb_ref[...], wb_ref[...])
        ga = _sigmoid(g_ref[0].astype(F32) + bg_ref[0])
        gb = _sigmoid(g_ref[1].astype(F32) + bg_ref[1])
        m_ref[...] = (ga * oa + gb * ob).astype(BF16)
        oab_ref[0] = oa.astype(BF16)
        oab_ref[1] = ob.astype(BF16)

    return pl.pallas_call(
        body, name="merge_forward", grid=(t // tm, D_MODEL // tn),
        in_specs=[pl.BlockSpec((tm, D_MODEL), lambda i, j: (i, 0)),
                  pl.BlockSpec((tm, D_MODEL), lambda i, j: (i, 0)),
                  pl.BlockSpec((None, D_MODEL, tn), lambda i, j: (0, 0, j)),
                  pl.BlockSpec((None, D_MODEL, tn), lambda i, j: (1, 0, j)),
                  pl.BlockSpec((2, tm, tn), lambda i, j: (BLOCK_G, i, j)),
                  pl.BlockSpec((2, 1, tn), lambda i, j: (0, 0, j))],
        out_specs=(pl.BlockSpec((tm, tn), lambda i, j: (i, j)),
                   pl.BlockSpec((2, tm, tn), lambda i, j: (0, i, j))),
        out_shape=(jax.ShapeDtypeStruct((t, D_MODEL), BF16), jax.ShapeDtypeStruct((2, t, D_MODEL), BF16)),
        compiler_params=_params("parallel", "parallel"),
    )(ya, yb, wo3, wo3, p, bg)


HEAD_ROWS = 256


def _head(merged, wo3, x, target, ln_g, ln_b, tm):
    t = x.shape[0]
    inv_d = 1.0 / D_MODEL

    def body(m_ref, w_ref, x_ref, t_ref, g_ref, b_ref, dr_ref, gx_ref, dm_ref, dg_ref, db_ref, loss_ref):
        i = pl.program_id(0)

        @pl.when(i == 0)
        def _():
            dg_ref[...] = jnp.zeros_like(dg_ref)
            db_ref[...] = jnp.zeros_like(db_ref)
            loss_ref[...] = jnp.zeros_like(loss_ref)

        w = w_ref[...]
        g = g_ref[...]
        tiles = [slice(r0, r0 + HEAD_ROWS) for r0 in range(0, tm, HEAD_ROWS)]
        firsts = [_dot(m_ref[rows, :], w) for rows in tiles]
        for rows, out in zip(tiles, firsts):
            r = DN_ALPHA * x_ref[rows, :] + out
            mu = jnp.mean(r, axis=-1, keepdims=True)
            xc = r - mu
            var = jnp.mean(xc * xc, axis=-1, keepdims=True)
            rstd = lax.rsqrt(var + LN_EPS)
            xhat = xc * rstd
            e = xhat * g + b_ref[...] - t_ref[rows, :]
            se = jnp.sum(jnp.sum(e * e, axis=1, keepdims=True), axis=0, keepdims=True)
            loss_ref[...] += jnp.broadcast_to((0.5 * inv_d) * se, loss_ref.shape)
            dy = e * inv_d
            db_ref[...] += jnp.sum(dy, axis=0, keepdims=True)
            dg_ref[...] += jnp.sum(dy * xhat, axis=0, keepdims=True)
            dxh = dy * g
            m1 = jnp.mean(dxh, axis=-1, keepdims=True)
            m2 = jnp.mean(dxh * xhat, axis=-1, keepdims=True)
            dr = rstd * (dxh - m1 - xhat * m2)
            gx_ref[rows, :] = DN_ALPHA * dr
            drb = dr.astype(BF16)
            dr_ref[rows, :] = drb
            dm_ref[rows, :] = _dot(drb, w, 1, 1).astype(BF16)

    row = pl.BlockSpec((tm, D_MODEL), lambda i: (i, 0))
    vec = pl.BlockSpec((1, D_MODEL), lambda i: (0, 0))
    return pl.pallas_call(
        body, name="head", grid=(t // tm,),
        in_specs=[row, pl.BlockSpec((None, D_MODEL, D_MODEL), lambda i: (2, 0, 0), pipeline_mode=pl.Buffered(1)),
                  row, row, vec, vec],
        out_specs=(row, row, row, vec, vec, pl.BlockSpec((1, 128), lambda i: (0, 0))),
        out_shape=(jax.ShapeDtypeStruct((t, D_MODEL), BF16), jax.ShapeDtypeStruct((t, D_MODEL), F32),
                   jax.ShapeDtypeStruct((t, D_MODEL), BF16), jax.ShapeDtypeStruct((1, D_MODEL), F32),
                   jax.ShapeDtypeStruct((1, D_MODEL), F32), jax.ShapeDtypeStruct((1, 128), F32)),
        compiler_params=_params("arbitrary"),
    )(merged, wo3, x, target, ln_g, ln_b)


def _gate_and_branch_backward(dmerged, oab, p, bg, wo3, tm):
    t = dmerged.shape[0]

    def body(dm_ref, oab_ref, g_ref, bg_ref, wa_ref, wb_ref, do_ref, dpg_ref, dbg_ref, dy_ref):
        @pl.when(pl.program_id(0) == 0)
        def _():
            dbg_ref[...] = jnp.zeros_like(dbg_ref)

        dm = dm_ref[...].astype(F32)
        for n, w_ref in enumerate((wa_ref, wb_ref)):
            gate = _sigmoid(g_ref[n].astype(F32) + bg_ref[n])
            d_o = (dm * gate).astype(BF16)
            do_ref[n] = d_o
            dgate = dm * oab_ref[n].astype(F32) * gate * (1.0 - gate)
            dpg_ref[n] = dgate.astype(BF16)
            dbg_ref[n] += jnp.sum(dgate, axis=0, keepdims=True)
            dy_ref[n] = _dot(d_o, w_ref[...], 1, 1).astype(BF16)

    pair = pl.BlockSpec((2, tm, D_MODEL), lambda i: (0, i, 0))
    gates = pl.BlockSpec((2, tm, D_MODEL), lambda i: (BLOCK_G, i, 0))
    vec = pl.BlockSpec((2, 1, D_MODEL), lambda i: (0, 0, 0))

    def weight(n):
        return pl.BlockSpec((None, D_MODEL, D_MODEL), lambda i: (n, 0, 0), pipeline_mode=pl.Buffered(1))

    pair_shape = jax.ShapeDtypeStruct((2, t, D_MODEL), BF16)
    return pl.pallas_call(
        body, name="gate_and_branch_backward", grid=(t // tm,),
        in_specs=[pl.BlockSpec((tm, D_MODEL), lambda i: (i, 0)), pair, gates, vec, weight(0), weight(1)],
        out_specs=(pair, gates, vec, pair),
        out_shape=(pair_shape, jax.ShapeDtypeStruct((N_SLOTS, t, D_MODEL), BF16),
                   jax.ShapeDtypeStruct((2, 1, D_MODEL), F32), pair_shape),
        compiler_params=_params("arbitrary"),
    )(dmerged, oab, p, bg, wo3, wo3)


def _weight_grad(a, a_sel, b, b_sel, tm, tn, tk, name, exchange=None):
    t = a.shape[-2]
    nk = t // tk
    ni, nj = D_MODEL // tm, D_MODEL // tn

    def body(a_ref, b_ref, *rest):
        if exchange is None:
            o_ref, acc_ref = rest
        else:
            g_ref, o_ref, r_ref, acc_ref, send_sem, recv_sem = rest
        i, j, k = pl.program_id(0), pl.program_id(1), pl.program_id(2)

        if exchange is not None:
            x, y, c, _ = _mesh_pos()
            hd = exchange.shape[0] // 2
            swap = pltpu.make_async_remote_copy(src_ref=g_ref.at[pl.ds((1 - c) * hd, hd), :], dst_ref=r_ref,
                                                send_sem=send_sem, recv_sem=recv_sem,
                                                device_id=(x, y, 1 - c), device_id_type=MESH)

            @pl.when(jnp.logical_and(i == 0, jnp.logical_and(j == 0, k == 0)))
            def _():
                swap.start()

        @pl.when(k == 0)
        def _():
            acc_ref[...] = jnp.zeros_like(acc_ref)

        acc_ref[...] += _dot(a_ref[...], b_ref[...], 0, 0)

        @pl.when(k == nk - 1)
        def _():
            o_ref[...] = acc_ref[...].astype(BF16)

        if exchange is not None:
            @pl.when(jnp.logical_and(i == ni - 1, jnp.logical_and(j == nj - 1, k == nk - 1)))
            def _():
                swap.wait()

    def spec(arr, sel, width, which):
        if arr.ndim == 2:
            return pl.BlockSpec((tk, width), lambda i, j, k: (k, (i, j)[which]))
        return pl.BlockSpec((None, tk, width), lambda i, j, k: (sel, k, (i, j)[which]))

    o_spec = pl.BlockSpec((tm, tn), lambda i, j, k: (i, j))
    o_shape = jax.ShapeDtypeStruct((D_MODEL, D_MODEL), BF16)
    if exchange is None:
        return pl.pallas_call(
            body, name=name, grid=(ni, nj, nk),
            in_specs=[spec(a, a_sel, tm, 0), spec(b, b_sel, tn, 1)],
            out_specs=o_spec, out_shape=o_shape,
            scratch_shapes=[pltpu.VMEM((tm, tn), F32)],
            compiler_params=_params("parallel", "parallel", "arbitrary"),
        )(a, b)
    return pl.pallas_call(
        body, name=name, grid=(ni, nj, nk),
        in_specs=[spec(a, a_sel, tm, 0), spec(b, b_sel, tn, 1), ANY],
        out_specs=(o_spec, ANY),
        out_shape=(o_shape, jax.ShapeDtypeStruct((exchange.shape[0] // 2, exchange.shape[1]), BF16)),
        scratch_shapes=[pltpu.VMEM((tm, tn), F32), pltpu.SemaphoreType.DMA, pltpu.SemaphoreType.DMA],
        compiler_params=pltpu.CompilerParams(dimension_semantics=("arbitrary",) * 3, vmem_limit_bytes=VMEM_LIMIT,
                                             has_side_effects=True),
    )(a, b, exchange)


def _win_grad(xt, dp, tn, tk):
    _, t, _ = dp.shape
    nk = t // tk
    per_seg = D_MODEL // tn
    nj = N_SEG * per_seg

    def body(x_ref, dp_ref, o_ref, acc_ref):
        k = pl.program_id(1)

        @pl.when(k == 0)
        def _():
            acc_ref[...] = jnp.zeros_like(acc_ref)

        acc_ref[...] += _dot(x_ref[...], dp_ref[...])

        @pl.when(k == nk - 1)
        def _():
            o_ref[...] = acc_ref[...].astype(BF16)

    return pl.pallas_call(
        body, name="grad_w_in", grid=(nj, nk),
        in_specs=[pl.BlockSpec((D_MODEL, tk), lambda j, k: (0, k)),
                  pl.BlockSpec((None, tk, tn), lambda j, k: (_slot_of_seg(j // per_seg), k, j % per_seg))],
        out_specs=pl.BlockSpec((D_MODEL, tn), lambda j, k: (0, j)),
        out_shape=jax.ShapeDtypeStruct((D_MODEL, N_SEG * D_MODEL), BF16),
        scratch_shapes=[pltpu.VMEM((D_MODEL, tn), F32)],
        compiler_params=_params("parallel", "arbitrary"),
    )(xt, dp)


def _input_grad_and_scatter(dp, w_full, gx, s_in, s_o, small, tm):
    _, t, _ = dp.shape
    ni, nk = t // tm, N_SEG - TAIL_SEGS
    _, hd, sc = s_in.shape
    hr = s_o.shape[2]

    def body(dp_ref, w_ref, gx_ref, si_ref, so_ref, sm_ref, o_ref, ri_ref, ro_ref, ga_ref,
             send_sems, recv_sems, local_sem):
        i, k = pl.program_id(0), pl.program_id(1)
        x, y, c, chips = _mesh_pos()
        me = 2 * x + y
        dev = 4 * x + 2 * y + c

        def peer(r):
            return (x ^ ((r >> 2) & 1), y ^ ((r >> 1) & 1), c ^ (r & 1))

        def sends():
            cps = []
            for q, (cx, cy) in enumerate(chips):
                dest = 2 * cx + cy
                cps.append(pltpu.make_async_remote_copy(src_ref=si_ref.at[dest], dst_ref=ri_ref.at[me],
                                                        send_sem=send_sems.at[q], recv_sem=recv_sems.at[q],
                                                        device_id=(cx, cy, c), device_id_type=MESH))
                cps.append(pltpu.make_async_remote_copy(src_ref=so_ref.at[:, dest], dst_ref=ro_ref.at[me],
                                                        send_sem=send_sems.at[3 + q], recv_sem=recv_sems.at[3 + q],
                                                        device_id=(cx, cy, c), device_id_type=MESH))
            for r in range(1, 8):
                cps.append(pltpu.make_async_remote_copy(src_ref=sm_ref, dst_ref=ga_ref.at[dev],
                                                        send_sem=send_sems.at[5 + r], recv_sem=recv_sems.at[5 + r],
                                                        device_id=peer(r), device_id_type=MESH))
            return cps

        own_small = pltpu.make_async_copy(sm_ref, ga_ref.at[dev], local_sem)

        @pl.when(jnp.logical_and(i == 0, k == 0))
        def _():
            for cp in sends():
                cp.start()
            own_small.start()

        @pl.when(k == 0)
        def _():
            o_ref[...] = gx_ref[...]

        o_ref[...] += _dot(dp_ref[...], w_ref[...], 1, 1)

        @pl.when(jnp.logical_and(i == ni - 1, k == nk - 1))
        def _():
            for q, (cx, cy) in enumerate(chips):
                frm = 2 * cx + cy
                pltpu.make_async_remote_copy(src_ref=si_ref.at[frm], dst_ref=ri_ref.at[frm], send_sem=send_sems.at[q],
                                             recv_sem=recv_sems.at[q], device_id=(x, y, c),
                                             device_id_type=MESH).wait_recv()
                pltpu.make_async_remote_copy(src_ref=so_ref.at[:, frm], dst_ref=ro_ref.at[frm],
                                             send_sem=send_sems.at[3 + q], recv_sem=recv_sems.at[3 + q],
                                             device_id=(x, y, c), device_id_type=MESH).wait_recv()
            for r in range(1, 8):
                px, py, pc = peer(r)
                pltpu.make_async_remote_copy(src_ref=sm_ref, dst_ref=ga_ref.at[4 * px + 2 * py + pc],
                                             send_sem=send_sems.at[5 + r], recv_sem=recv_sems.at[5 + r],
                                             device_id=(x, y, c), device_id_type=MESH).wait_recv()
            for cp in sends():
                cp.wait_send()
            own_small.wait()

    return pl.pallas_call(
        body, name="grad_x_and_scatter", grid=(ni, nk),
        in_specs=[pl.BlockSpec((None, tm, D_MODEL), lambda i, k: (_slot_of_seg(k), i, 0)),
                  pl.BlockSpec((D_MODEL, D_MODEL), lambda i, k: (0, k)),
                  pl.BlockSpec((tm, D_MODEL), lambda i, k: (i, 0), pipeline_mode=pl.Buffered(1)), ANY, ANY, ANY],
        out_specs=(pl.BlockSpec((tm, D_MODEL), lambda i, k: (i, 0)), ANY, ANY, ANY),
        out_shape=(jax.ShapeDtypeStruct((t, D_MODEL), F32),
                   jax.ShapeDtypeStruct((N_CHIPS, hd, sc), BF16),
                   jax.ShapeDtypeStruct((N_CHIPS, 3, hr, D_MODEL), BF16),
                   jax.ShapeDtypeStruct((8,) + small.shape, small.dtype)),
        scratch_shapes=[pltpu.SemaphoreType.DMA((13,)), pltpu.SemaphoreType.DMA((13,)), pltpu.SemaphoreType.DMA],
        input_output_aliases={2: 0},
        compiler_params=pltpu.CompilerParams(dimension_semantics=("arbitrary", "arbitrary"),
                                             vmem_limit_bytes=VMEM_LIMIT, has_side_effects=True),
    )(dp, w_full, gx, s_in, s_o, small)


TAIL_SEGS = 1


def _input_grad_tail_and_adam(dp, w_full, partial, w, g, m, v, steps):
    _, t, _ = dp.shape
    tm = t // steps
    r, c = w.shape
    ra = r // steps
    segs = tuple(range(N_SEG - TAIL_SEGS, N_SEG))

    def body(*refs):
        dp_refs = refs[:TAIL_SEGS]
        w_refs = refs[TAIL_SEGS:2 * TAIL_SEGS]
        part_ref, aw_ref, ag_ref, am_ref, av_ref, o_ref, go_ref, d_ref, mo_ref, vo_ref = refs[2 * TAIL_SEGS:]
        acc = part_ref[...]
        for dp_ref, w_ref in zip(dp_refs, w_refs):
            acc = acc + _dot(dp_ref[...], w_ref[...], 1, 1)
        o_ref[...] = acc
        gv = ag_ref[...]
        d, mn, vn = _adam_math(aw_ref[...], gv, am_ref[...], av_ref[...])
        go_ref[...] = gv
        d_ref[...] = d
        mo_ref[...] = mn
        vo_ref[...] = vn

    row = pl.BlockSpec((tm, D_MODEL), lambda i: (i, 0))
    arow = pl.BlockSpec((ra, c), lambda i: (i, 0))
    ashape = jax.ShapeDtypeStruct((r, c), F32)
    in_specs = [pl.BlockSpec((None, tm, D_MODEL), functools.partial(lambda s, i: (SLOT_OF_SEG[s], i, 0), s))
                for s in segs]
    in_specs += [pl.BlockSpec((D_MODEL, D_MODEL), functools.partial(lambda s, i: (0, s), s),
                              pipeline_mode=pl.Buffered(1)) for s in segs]
    in_specs += [row, arow, arow, arow, arow]
    res = pl.pallas_call(
        body, name="grad_x_tail_and_adam_w_in", grid=(steps,),
        in_specs=in_specs,
        out_specs=(row, arow, arow, arow, arow),
        out_shape=(jax.ShapeDtypeStruct((t, D_MODEL), F32), ashape, ashape, ashape, ashape),
        compiler_params=_params("parallel"),
    )(*([dp] * TAIL_SEGS), *([w_full] * TAIL_SEGS), partial, w, g, m, v)
    return res[0], res[1:]


def _sgu_chunk_forward(u, v, z, wm, bs, lng, lnb):
    ug, dug = _gelu_and_grad(u)
    vg, dvg = _gelu_and_grad(v)
    mu = jnp.mean(vg, axis=-1, keepdims=True)
    xc = vg - mu
    var = jnp.mean(xc * xc, axis=-1, keepdims=True)
    rstd = lax.rsqrt(var + LN_EPS)
    vhat = xc * rstd
    vln = (vhat * lng + lnb).astype(BF16)
    mixed = _dot(wm, vln) + bs
    sig = _sigmoid(z)
    return ug, dug, dvg, rstd, vhat, vln, mixed, sig


def _mixer_a_forward(p_a, wm, bs_col, ln_v_g, ln_v_b, tm):
    t = p_a.shape[1]

    def body(p_ref, wm_ref, bs_ref, g_ref, b_ref, o_ref):
        wm_v, bs_v, lng, lnb = wm_ref[...], bs_ref[...], g_ref[...], b_ref[...]

        def chunk(ci, carry):
            rows = pl.ds(pl.multiple_of(ci * CHUNK, CHUNK), CHUNK)
            u = p_ref[0, rows, :].astype(F32)
            v = p_ref[1, rows, :].astype(F32)
            z = p_ref[2, rows, :].astype(F32)
            ug, _, _, _, _, _, mixed, sig = _sgu_chunk_forward(u, v, z, wm_v, bs_v, lng, lnb)
            o_ref[rows, :] = (ug * mixed * (z * sig)).astype(BF16)
            return carry

        lax.fori_loop(0, tm // CHUNK, chunk, 0, unroll=True)

    return pl.pallas_call(
        body, name="mixer_a_forward", grid=(t // tm, N_HEADS),
        in_specs=[pl.BlockSpec((3, tm, HEAD_DIM), lambda i, h: (0, i, h)),
                  pl.BlockSpec((None, CHUNK, CHUNK), lambda i, h: (h, 0, 0)),
                  pl.BlockSpec((None, CHUNK, 1), lambda i, h: (h, 0, 0)),
                  pl.BlockSpec((1, HEAD_DIM), lambda i, h: (0, h)),
                  pl.BlockSpec((1, HEAD_DIM), lambda i, h: (0, h))],
        out_specs=pl.BlockSpec((tm, HEAD_DIM), lambda i, h: (i, h)),
        out_shape=jax.ShapeDtypeStruct((t, D_MODEL), BF16),
        compiler_params=_params("parallel", "parallel"),
    )(p_a, wm, bs_col, ln_v_g, ln_v_b)


def _mixer_a_backward(p_a, dyab, wm, bs_col, ln_v_g, ln_v_b, dp, tm):
    t = p_a.shape[1]

    def body(p_ref, dy_ref, wm_ref, bs_ref, g_ref, b_ref, dp_in, dp_ref, dws_ref, dbs_ref, dg_ref, db_ref):
        @pl.when(pl.program_id(1) == 0)
        def _():
            dws_ref[...] = jnp.zeros_like(dws_ref)
            dbs_ref[...] = jnp.zeros_like(dbs_ref)
            dg_ref[...] = jnp.zeros_like(dg_ref)
            db_ref[...] = jnp.zeros_like(db_ref)

        wm_v, bs_v, lng, lnb = wm_ref[...], bs_ref[...], g_ref[...], b_ref[...]
        causal = (lax.broadcasted_iota(jnp.int32, (CHUNK, CHUNK), 1)
                  <= lax.broadcasted_iota(jnp.int32, (CHUNK, CHUNK), 0))

        def chunk(ci, carry):
            rows = pl.ds(pl.multiple_of(ci * CHUNK, CHUNK), CHUNK)
            u = p_ref[0, rows, :].astype(F32)
            v = p_ref[1, rows, :].astype(F32)
            z = p_ref[2, rows, :].astype(F32)
            dy = dy_ref[rows, :].astype(F32)
            ug, dug, dvg, rstd, vhat, vln, mixed, sig = _sgu_chunk_forward(u, v, z, wm_v, bs_v, lng, lnb)
            sz = z * sig
            dmixed = dy * ug * sz
            dp_ref[0, rows, :] = (dy * mixed * sz * dug).astype(BF16)
            dp_ref[2, rows, :] = (dy * ug * mixed * _silu_grad(sig, sz)).astype(BF16)
            dbs_ref[...] += jnp.sum(dmixed, axis=1, keepdims=True)
            dmb = dmixed.astype(BF16)
            dws_ref[...] += jnp.where(causal, _dot(dmb, vln, 1, 1), 0.0)
            dvln = _dot(wm_v, dmb, 0, 0)
            db_ref[...] += jnp.sum(dvln, axis=0, keepdims=True)
            dg_ref[...] += jnp.sum(dvln * vhat, axis=0, keepdims=True)
            dvh = dvln * lng
            m1 = jnp.mean(dvh, axis=-1, keepdims=True)
            m2 = jnp.mean(dvh * vhat, axis=-1, keepdims=True)
            dp_ref[1, rows, :] = (rstd * (dvh - m1 - vhat * m2) * dvg).astype(BF16)
            return carry

        lax.fori_loop(0, tm // CHUNK, chunk, 0, unroll=True)

    return pl.pallas_call(
        body, name="mixer_a_backward", grid=(N_HEADS, t // tm),
        in_specs=[pl.BlockSpec((3, tm, HEAD_DIM), lambda h, i: (0, i, h)),
                  pl.BlockSpec((None, tm, HEAD_DIM), lambda h, i: (0, i, h)),
                  pl.BlockSpec((None, CHUNK, CHUNK), lambda h, i: (h, 0, 0)),
                  pl.BlockSpec((None, CHUNK, 1), lambda h, i: (h, 0, 0)),
                  pl.BlockSpec((1, HEAD_DIM), lambda h, i: (0, h)),
                  pl.BlockSpec((1, HEAD_DIM), lambda h, i: (0, h)), ANY],
        out_specs=(pl.BlockSpec((3, tm, HEAD_DIM), lambda h, i: (BLOCK_A, i, h)),
                   pl.BlockSpec((None, CHUNK, CHUNK), lambda h, i: (h, 0, 0)),
                   pl.BlockSpec((None, CHUNK, 1), lambda h, i: (h, 0, 0)),
                   pl.BlockSpec((1, HEAD_DIM), lambda h, i: (0, h)),
                   pl.BlockSpec((1, HEAD_DIM), lambda h, i: (0, h))),
        out_shape=(jax.ShapeDtypeStruct(dp.shape, BF16),
                   jax.ShapeDtypeStruct((N_HEADS, CHUNK, CHUNK), F32),
                   jax.ShapeDtypeStruct((N_HEADS, CHUNK, 1), F32),
                   jax.ShapeDtypeStruct((1, D_MODEL), F32), jax.ShapeDtypeStruct((1, D_MODEL), F32)),
        input_output_aliases={6: 0},
        compiler_params=_params("parallel", "arbitrary"),
    )(p_a, dyab, wm, bs_col, ln_v_g, ln_v_b, dp)


HALO = 16


def _conv_taps(h, halo_h, tm):
    row = lax.broadcasted_iota(jnp.int32, h.shape, 0)
    last1 = halo_h[HALO - 1:HALO, :]
    last2 = halo_h[HALO - 2:HALO - 1, :]
    h1 = jnp.where(row == 0, last1, pltpu.roll(h, 1, 0))
    h2 = jnp.where(row == 0, last2, jnp.where(row == 1, last1, pltpu.roll(h, 2, 0)))
    return h1, h2


def _mixer_b_forward(p_b, conv_w, conv_b, tm, tc):
    t = p_b.shape[1]

    def body(p_ref, halo_ref, w_ref, b_ref, o_ref):
        valid = (pl.program_id(1) > 0).astype(F32)
        h = p_ref[1].astype(F32) * p_ref[0].astype(F32)
        halo_h = halo_ref[1].astype(F32) * halo_ref[0].astype(F32) * valid
        h1, h2 = _conv_taps(h, halo_h, tm)
        w = w_ref[...]
        conv = b_ref[...] + w[0:1, :] * h2 + w[1:2, :] * h1 + w[2:3, :] * h
        z = p_ref[3].astype(F32)
        o_ref[...] = (p_ref[2].astype(F32) * conv * (z * _sigmoid(z))).astype(BF16)

    steps = tm // HALO
    return pl.pallas_call(
        body, name="mixer_b_forward", grid=(D_MODEL // tc, t // tm),
        in_specs=[pl.BlockSpec((4, tm, tc), lambda j, i: (BLOCK_B, i, j)),
                  pl.BlockSpec((4, HALO, tc), lambda j, i: (BLOCK_B, jnp.maximum(i * steps - 1, 0), j)),
                  pl.BlockSpec((3, tc), lambda j, i: (0, j)),
                  pl.BlockSpec((1, tc), lambda j, i: (0, j))],
        out_specs=pl.BlockSpec((tm, tc), lambda j, i: (i, j)),
        out_shape=jax.ShapeDtypeStruct((t, D_MODEL), BF16),
        compiler_params=_params("parallel", "parallel"),
    )(p_b, p_b, conv_w, conv_b)


def _mixer_b_backward(p_b, dyab, conv_w, conv_b, dp, tm, tc):
    t = p_b.shape[1]
    n = t // tm

    def body(p_ref, halo_ref, dy_ref, w_ref, b_ref, dp_in, dp_ref, dw_ref, db_ref, next_ref):
        ii = pl.program_id(1)

        @pl.when(ii == 0)
        def _():
            dw_ref[...] = jnp.zeros_like(dw_ref)
            db_ref[...] = jnp.zeros_like(db_ref)
            next_ref[...] = jnp.zeros_like(next_ref)

        valid = (ii < n - 1).astype(F32)
        xb = p_ref[0].astype(F32)
        cb = p_ref[1].astype(F32)
        bb = p_ref[2].astype(F32)
        z = p_ref[3].astype(F32)
        h = cb * xb
        halo_h = halo_ref[1].astype(F32) * halo_ref[0].astype(F32) * valid
        h1, h2 = _conv_taps(h, halo_h, tm)
        w = w_ref[...]
        w0, w1, w2 = w[0:1, :], w[1:2, :], w[2:3, :]
        conv = b_ref[...] + w0 * h2 + w1 * h1 + w2 * h
        sig = _sigmoid(z)
        sz = z * sig
        dy = dy_ref[...].astype(F32)
        dconv = dy * bb * sz
        dp_ref[2] = (dy * conv * sz).astype(BF16)
        dp_ref[3] = (dy * bb * conv * _silu_grad(sig, sz)).astype(BF16)
        db_ref[...] += jnp.sum(dconv, axis=0, keepdims=True)
        dw_ref[0:1, :] += jnp.sum(dconv * h2, axis=0, keepdims=True)
        dw_ref[1:2, :] += jnp.sum(dconv * h1, axis=0, keepdims=True)
        dw_ref[2:3, :] += jnp.sum(dconv * h, axis=0, keepdims=True)
        row = lax.broadcasted_iota(jnp.int32, h.shape, 0)
        nxt = next_ref[...]
        n0, n1 = nxt[0:1, :], nxt[1:2, :]
        d1 = jnp.where(row == tm - 1, n0, pltpu.roll(dconv, tm - 1, 0))
        d2 = jnp.where(row == tm - 1, n1, jnp.where(row == tm - 2, n0, pltpu.roll(dconv, tm - 2, 0)))
        dh = w2 * dconv + w1 * d1 + w0 * d2
        dp_ref[0] = (dh * cb).astype(BF16)
        dp_ref[1] = (dh * xb).astype(BF16)
        next_ref[...] = dconv[0:8, :]

    steps = tm // HALO
    return pl.pallas_call(
        body, name="mixer_b_backward", grid=(D_MODEL // tc, n),
        in_specs=[pl.BlockSpec((4, tm, tc), lambda j, ii: (BLOCK_B, n - 1 - ii, j)),
                  pl.BlockSpec((4, HALO, tc), lambda j, ii: (BLOCK_B, jnp.maximum((n - 1 - ii) * steps - 1, 0), j)),
                  pl.BlockSpec((None, tm, tc), lambda j, ii: (1, n - 1 - ii, j)),
                  pl.BlockSpec((3, tc), lambda j, ii: (0, j)),
                  pl.BlockSpec((1, tc), lambda j, ii: (0, j)), ANY],
        out_specs=(pl.BlockSpec((4, tm, tc), lambda j, ii: (BLOCK_B, n - 1 - ii, j)),
                   pl.BlockSpec((3, tc), lambda j, ii: (0, j)),
                   pl.BlockSpec((1, tc), lambda j, ii: (0, j))),
        out_shape=(jax.ShapeDtypeStruct(dp.shape, BF16),
                   jax.ShapeDtypeStruct((3, D_MODEL), F32), jax.ShapeDtypeStruct((1, D_MODEL), F32)),
        scratch_shapes=[pltpu.VMEM((8, tc), F32)],
        input_output_aliases={5: 0},
        compiler_params=_params("parallel", "arbitrary"),
    )(p_b, p_b, dyab, conv_w, conv_b, dp)


def _adam_math(w, g, m, v):
    m = ADAM_B1 * m + (1.0 - ADAM_B1) * g
    v = ADAM_B2 * v + (1.0 - ADAM_B2) * (g * g)
    delta = -ADAM_LR * ((m * ADAM_C1) / (jnp.sqrt(v * ADAM_C2) + ADAM_EPS) + ADAM_WD * w)
    return delta, m, v


def _adam_rows(w, g, m, v, tm, name, g_sel=None):
    r, c = w.shape

    def body(w_ref, g_ref, m_ref, v_ref, go_ref, d_ref, mo_ref, vo_ref):
        g = g_ref[...]
        d, mn, vn = _adam_math(w_ref[...], g, m_ref[...], v_ref[...])
        go_ref[...] = g
        d_ref[...] = d
        mo_ref[...] = mn
        vo_ref[...] = vn

    spec = pl.BlockSpec((tm, c), lambda i: (i, 0))
    g_spec = spec if g_sel is None else pl.BlockSpec((None, tm, c), lambda i: (g_sel, i, 0))
    shape = jax.ShapeDtypeStruct((r, c), F32)
    return pl.pallas_call(
        body, name=name, grid=(r // tm,),
        in_specs=[spec, g_spec, spec, spec], out_specs=(spec,) * 4, out_shape=(shape,) * 4,
        compiler_params=_params("parallel"),
    )(w, g, m, v)


SMALL_ROW0 = {name: sum(r for _, r in SMALL_ROWS[:i]) for i, (name, _) in enumerate(SMALL_ROWS)}
LANE_MAJOR = ("ln_g", "ln_b", "b_gate", "ln_v_g", "ln_v_b", "conv_b")


def _lane_pieces(n):
    return [(q, slice(q * 128, (q + 1) * 128)) for q in range(n // 128)]


def _pack_small(d_ln_g, d_ln_b, d_bg, d_lnv_g, d_lnv_b, d_ws, d_bs, d_cw, d_cb, loss_part):
    def body(lg, lb, bg, vg, vb, ws, bs, cw, cb, loss, o_ref):
        def put(row0, vec):
            for q, cols in _lane_pieces(vec.shape[1]):
                o_ref[row0 + q:row0 + q + 1, :] = vec[:, cols]

        put(SMALL_ROW0["ln_g"], lg[...])
        put(SMALL_ROW0["ln_b"], lb[...])
        for n in range(2):
            put(SMALL_ROW0["b_gate"] + n * (D_MODEL // 128), bg[n])
        put(SMALL_ROW0["ln_v_g"], vg[...])
        put(SMALL_ROW0["ln_v_b"], vb[...])
        for h in range(N_HEADS):
            o_ref[SMALL_ROW0["w_s"] + h * CHUNK:SMALL_ROW0["w_s"] + (h + 1) * CHUNK, :] = ws[h]
        o_ref[SMALL_ROW0["b_s"]:SMALL_ROW0["b_s"] + N_HEADS, :] = bs[...]
        for c in range(3):
            put(SMALL_ROW0["conv_w"] + c * (D_MODEL // 128), cw[c:c + 1, :])
        put(SMALL_ROW0["conv_b"], cb[...])
        o_ref[SMALL_ROW0["loss"]:SMALL_ROW0["loss"] + 8, :] = jnp.broadcast_to(loss[...], (8, 128))

    return pl.pallas_call(
        body, name="pack_small", out_shape=jax.ShapeDtypeStruct((SMALL_TOTAL, 128), F32), compiler_params=_params(),
    )(d_ln_g, d_ln_b, d_bg, d_lnv_g, d_lnv_b, d_ws, d_bs.reshape(N_HEADS, CHUNK), d_cw, d_cb, loss_part)


def _adam_small(gathered, params):
    names = list(params)
    flat = [a for n in names for a in params[n]]

    def body(*refs):
        ga_ref = refs[0]
        ins = refs[1:1 + 3 * len(names)]
        outs = refs[1 + 3 * len(names):-1]
        gs_ref = refs[-1]
        g = ga_ref[0]
        for k in range(1, 8):
            g = g + ga_ref[k]
        gs_ref[...] = g
        for i, name in enumerate(names):
            w_ref, m_ref, v_ref = ins[3 * i:3 * i + 3]
            o_refs = outs[4 * i:4 * i + 4]
            row0 = SMALL_ROW0[name]
            if name in LANE_MAJOR:
                pieces = [((slice(None), cols), slice(row0 + q, row0 + q + 1))
                          for q, cols in _lane_pieces(w_ref.shape[1])]
            elif name == "w_s":
                pieces = [((0, h), slice(row0 + h * CHUNK, row0 + (h + 1) * CHUNK)) for h in range(N_HEADS)]
            else:
                pieces = [((0,), slice(row0, row0 + N_HEADS))]
            for idx, rows in pieces:
                gp = gs_ref[rows, :]
                res = (gp,) + _adam_math(w_ref[idx], gp, m_ref[idx], v_ref[idx])
                for o_ref, val in zip(o_refs, res):
                    o_ref[idx] = val
        gcw_ref, loss_ref = outs[-2:]
        for c in range(3):
            for q, cols in _lane_pieces(D_MODEL):
                r = SMALL_ROW0["conv_w"] + c * (D_MODEL // 128) + q
                gcw_ref[c:c + 1, cols] = gs_ref[r:r + 1, :]
        loss_ref[...] = gs_ref[SMALL_ROW0["loss"]:SMALL_ROW0["loss"] + 1, :]

    out_shape = [jax.ShapeDtypeStruct(params[n][0].shape, F32) for n in names for _ in range(4)]
    out_shape += [jax.ShapeDtypeStruct((3, D_MODEL), F32), jax.ShapeDtypeStruct((1, 128), F32)]
    res = pl.pallas_call(
        body, name="adam_small", out_shape=tuple(out_shape),
        scratch_shapes=[pltpu.VMEM((SMALL_TOTAL, 128), F32)], compiler_params=_params(),
    )(gathered, *flat)
    return {n: res[4 * i:4 * i + 4] for i, n in enumerate(names)}, res[-2], res[-1]


def _adam_conv_w(g_all, chip, w, m, v):
    cols = w.shape[2]

    def body(c_ref, g_ref, w_ref, m_ref, v_ref, go_ref, d_ref, mo_ref, vo_ref):
        g = g_ref[...]
        d, mn, vn = _adam_math(w_ref[...], g, m_ref[...], v_ref[...])
        go_ref[...] = g
        d_ref[...] = d
        mo_ref[...] = mn
        vo_ref[...] = vn

    own = pl.BlockSpec((None, 3, cols), lambda i, c_ref: (0, 0, 0))
    return pl.pallas_call(
        body, name="adam_conv_w",
        grid_spec=pltpu.PrefetchScalarGridSpec(
            num_scalar_prefetch=1, grid=(1,),
            in_specs=[pl.BlockSpec((3, cols), lambda i, c_ref: (0, c_ref[0])), own, own, own],
            out_specs=(own,) * 4),
        out_shape=(jax.ShapeDtypeStruct(w.shape, F32),) * 4,
        compiler_params=_params("arbitrary"),
    )(chip, g_all, w, m, v)


def kernel(x, w_in, b_gate, ln_v_g, ln_v_b, w_s, b_s, conv_w, conv_b, w_oa, w_ob, w_out, ln_g, ln_b, loss_target, m_w_in, m_b_gate, m_ln_v_g, m_ln_v_b, m_w_s, m_b_s, m_conv_w, m_conv_b, m_w_oa, m_w_ob, m_w_out, m_ln_g, m_ln_b, v_w_in, v_b_gate, v_ln_v_g, v_ln_v_b, v_w_s, v_b_s, v_conv_w, v_conv_b, v_w_oa, v_w_ob, v_w_out, v_ln_g, v_ln_b):
    t = x.shape[1]
    x2 = x[0]
    target = loss_target[0]
    chip = 2 * lax.axis_index("x") + lax.axis_index("y")
    conv_cols = conv_w.shape[2]

    chip1 = chip.astype(jnp.int32).reshape(1)
    w_pre = _cast_into_columns(w_in[0], chip1, N_CHIPS, 256, "cast_w_in")
    wo_b, wm = _prep_small_weights(w_oa[0], w_ob[0], w_out[0], w_s[0])
    conv_w8 = jnp.concatenate([conv_w[0], jnp.zeros((5, conv_cols), F32)], axis=0)
    bs_col = b_s[0].reshape(N_HEADS, CHUNK, 1)
    bg = b_gate.reshape(2, 1, D_MODEL)

    xb, xt = _cast_and_transpose(x2, min(512, t))
    p, w_full, wo_full, cw_full = _gather_and_project(xb, w_pre, wo_b, conv_w8, min(4096, t))
    wo3 = wo_full.reshape(3, D_MODEL, D_MODEL)
    conv_w_all = jnp.transpose(cw_full[:, :3, :], (1, 0, 2)).reshape(3, D_MODEL)
    tm_a = min(512, t)
    ya = _mixer_a_forward(p, wm, bs_col, ln_v_g, ln_v_b, tm_a)
    tm_b = min(512, t)
    yb = _mixer_b_forward(p, conv_w_all, conv_b, tm_b, 512)
    tm_m = min(1024, t)
    merged, oab = _merge_forward(ya, yb, wo3, p, bg, tm_m, 512)

    drb, gx, dmerged, d_ln_g, d_ln_b, loss_part = _head(merged, wo3, x2, target, ln_g, ln_b, min(512, t))
    doab, dp, d_bg, dyab = _gate_and_branch_backward(dmerged, oab, p, bg, wo3, min(256, t))
    dp, d_ws, d_bs, d_lnv_g, d_lnv_b = _mixer_a_backward(p, dyab, wm, bs_col, ln_v_g, ln_v_b, dp, tm_a)
    dp, d_cw, d_cb = _mixer_b_backward(p, dyab, conv_w_all, conv_b, dp, tm_b, 512)

    tk = min(2048, t)
    g_in = _win_grad(xt, dp, 1024, tk)
    g_oa = _weight_grad(ya, 0, doab, 0, 1024, 1024, tk, "grad_w_oa")
    g_ob = _weight_grad(yb, 0, doab, 1, 1024, 1024, tk, "grad_w_ob")
    g_out, r_in = _weight_grad(merged, 0, drb, 0, 1024, 1024, tk, "grad_w_out", exchange=g_in)
    r_o = _exchange_halves((g_oa, g_ob, g_out))
    s_in, s_o = _add_halves(g_in, (g_oa, g_ob, g_out), r_in, r_o)
    small_part = _pack_small(d_ln_g, d_ln_b, d_bg, d_lnv_g, d_lnv_b, d_ws, d_bs, d_cw, d_cb, loss_part)
    gx, q_in, q_o, gathered = _input_grad_and_scatter(dp, w_full, gx, s_in, s_o, small_part, min(1024, t))
    f_in, f_o = _sum_chips(q_in, q_o, s_in, s_o)
    gsum_in, gsum_o = _share_halves(f_in, f_o)

    big = {}
    grad_x, big["w_in"] = _input_grad_tail_and_adam(dp, w_full, gx, w_in[0], gsum_in, m_w_in[0], v_w_in[0],
                                                    min(32, t // 128))
    for n, (name, w, m, v) in enumerate((("w_oa", w_oa, m_w_oa, v_w_oa), ("w_ob", w_ob, m_w_ob, v_w_ob),
                                         ("w_out", w_out, m_w_out, v_w_out))):
        big[name] = _adam_rows(w[0], gsum_o, m[0], v[0], 256, "adam_" + name, g_sel=n)

    small, g_conv_w, loss_row = _adam_small(gathered, {
        "ln_g": (ln_g, m_ln_g, v_ln_g), "ln_b": (ln_b, m_ln_b, v_ln_b), "b_gate": (b_gate, m_b_gate, v_b_gate),
        "ln_v_g": (ln_v_g, m_ln_v_g, v_ln_v_g), "ln_v_b": (ln_v_b, m_ln_v_b, v_ln_v_b),
        "w_s": (w_s, m_w_s, v_w_s), "b_s": (b_s, m_b_s, v_b_s), "conv_b": (conv_b, m_conv_b, v_conv_b)})
    small["conv_w"] = _adam_conv_w(g_conv_w, chip1, conv_w, m_conv_w, v_conv_w)
    loss = loss_row[0, 0]

    order = ("w_in", "b_gate", "ln_v_g", "ln_v_b", "w_s", "b_s", "conv_w", "conv_b", "w_oa", "w_ob", "w_out",
             "ln_g", "ln_b")
    outs = [loss, grad_x[None]]
    for which in range(4):
        for name in order:
            outs.append(big[name][which][None] if name in big else small[name][which])
    return tuple(outs)
```

```python
import functools
import math

import jax
import jax.numpy as jnp
from jax import lax
from jax.experimental import pallas as pl
from jax.experimental.pallas import tpu as pltpu

F32 = jnp.float32
BF16 = jnp.bfloat16

D_MODEL = 2048
N_HEADS = 8
HEAD_DIM = D_MODEL // N_HEADS
CHUNK = 128
N_SEG = 9
N_CHIPS = 4
SHARD_COLS = N_SEG * D_MODEL // N_CHIPS
COL_BLOCK = 512
BLOCKS_PER_SHARD = SHARD_COLS // COL_BLOCK
BLOCKS_PER_SEG = D_MODEL // COL_BLOCK
SUBS = 3
SHARD_ROWS = D_MODEL // N_CHIPS
N_SLOTS = 12
BLOCK_A, BLOCK_G, BLOCK_B = 0, 2, 2
SLOT_OF_SEG = (0, 1, 2, 8, 9, 10, 11, 4, 5)
DN_ALPHA = 2.0 ** 0.25
LN_EPS = 1e-5
GELU_K = math.sqrt(2.0 / math.pi)
GELU_C = 0.044715

ADAM_LR = 0.001
ADAM_B1 = 0.9
ADAM_B2 = 0.999
ADAM_EPS = 1e-08
ADAM_WD = 0.01
ADAM_STEP = 10
ADAM_C1 = 1.0 / (1.0 - ADAM_B1 ** ADAM_STEP)
ADAM_C2 = 1.0 / (1.0 - ADAM_B2 ** ADAM_STEP)

VMEM_LIMIT = 60 * 1024 * 1024
MESH = pl.DeviceIdType.MESH
ANY = pl.BlockSpec(memory_space=pl.ANY)

SMALL_ROWS = (("ln_g", 16), ("ln_b", 16), ("b_gate", 32), ("ln_v_g", 16), ("ln_v_b", 16),
              ("w_s", 1024), ("b_s", 8), ("conv_w", 48), ("conv_b", 16), ("loss", 8))
SMALL_TOTAL = sum(r for _, r in SMALL_ROWS)


def _params(*sem):
    return pltpu.CompilerParams(dimension_semantics=sem, vmem_limit_bytes=VMEM_LIMIT)


def _sigmoid(x):
    return 1.0 / (1.0 + jnp.exp(-x))


def _gelu_gate(x, x2):
    return 1.0 / (1.0 + jnp.exp(x * ((-2.0 * GELU_K) + (-2.0 * GELU_K * GELU_C) * x2)))


def _gelu_and_grad(x):
    x2 = x * x
    s = _gelu_gate(x, x2)
    g = x * s
    dg = s + g * (1.0 - s) * ((2.0 * GELU_K) + (6.0 * GELU_K * GELU_C) * x2)
    return g, dg


def _silu_grad(sig, sz):
    return sig + sz * (1.0 - sig)


def _dot(a, b, ca=1, cb=0):
    return lax.dot_general(a, b, (((ca,), (cb,)), ((), ())), preferred_element_type=F32)


def _cast_and_transpose(x, tm):
    t, d = x.shape

    def body(x_ref, o_ref, ot_ref):
        v = x_ref[...]
        o_ref[...] = v.astype(BF16)
        ot_ref[...] = v.T.astype(BF16)

    return pl.pallas_call(
        body, name="cast_x", grid=(t // tm,),
        in_specs=[pl.BlockSpec((tm, d), lambda i: (i, 0))],
        out_specs=(pl.BlockSpec((tm, d), lambda i: (i, 0)), pl.BlockSpec((d, tm), lambda i: (0, i))),
        out_shape=(jax.ShapeDtypeStruct((t, d), BF16), jax.ShapeDtypeStruct((d, t), BF16)),
        compiler_params=_params("parallel"),
    )(x)


def _cast_into_columns(w, slot, n_slots, tm, name):
    r, c = w.shape

    def body(s_ref, w_ref, o_ref):
        o_ref[...] = w_ref[...].astype(BF16)

    return pl.pallas_call(
        body, name=name,
        grid_spec=pltpu.PrefetchScalarGridSpec(
            num_scalar_prefetch=1, grid=(r // tm,),
            in_specs=[pl.BlockSpec((tm, c), lambda i, s_ref: (i, 0))],
            out_specs=pl.BlockSpec((tm, c), lambda i, s_ref: (i, s_ref[0]))),
        out_shape=jax.ShapeDtypeStruct((r, n_slots * c), BF16),
        compiler_params=_params("parallel"),
    )(slot, w)


def _prep_small_weights(w_oa, w_ob, w_out, w_s):
    rows = w_oa.shape[0]

    def body(a_ref, b_ref, c_ref, ws_ref, wo_ref, wm_ref):
        wo_ref[0] = a_ref[...].astype(BF16)
        wo_ref[1] = b_ref[...].astype(BF16)
        wo_ref[2] = c_ref[...].astype(BF16)
        t = lax.broadcasted_iota(jnp.int32, (CHUNK, CHUNK), 0)
        s = lax.broadcasted_iota(jnp.int32, (CHUNK, CHUNK), 1)
        for h in range(N_HEADS):
            wm_ref[h] = jnp.where(s <= t, ws_ref[h], 0.0).astype(BF16)

    return pl.pallas_call(
        body, name="prep_small_weights",
        out_shape=(jax.ShapeDtypeStruct((3, rows, D_MODEL), BF16),
                   jax.ShapeDtypeStruct((N_HEADS, CHUNK, CHUNK), BF16)),
        compiler_params=_params(),
    )(w_oa, w_ob, w_out, w_s)


def _mesh_pos():
    x, y, c = lax.axis_index("x"), lax.axis_index("y"), lax.axis_index("c")
    chips = [(1 - x, y), (x, 1 - y), (1 - x, 1 - y)]
    return x, y, c, chips


def _slot_of_seg(seg):
    return jnp.where(seg < 3, seg, jnp.where(seg < 7, seg + 5, seg - 3))


def _gather_and_project(xb, w_pre, wo_b, conv_w8, tm):
    t = xb.shape[0]
    d, sc = w_pre.shape[0], w_pre.shape[1] // N_CHIPS
    rows = wo_b.shape[1]
    hd, hr = d // 2, rows // 2
    nj = BLOCKS_PER_SHARD // SUBS
    pc = nj * COL_BLOCK
    units = N_CHIPS * SUBS
    ni = t // tm
    total = units * ni * nj
    mx, my = lax.axis_index("x"), lax.axis_index("y")
    order = jnp.stack([2 * mx + my, 2 * (1 - mx) + my, 2 * mx + (1 - my),
                       2 * (1 - mx) + (1 - my)]).astype(jnp.int32)

    def body(order_ref, x_ref, wpre_ref, wo_ref, cw_ref, p_ref, wf_ref, wof_ref, cwf_ref,
             wbuf, wsem, xbuf, xsem, send_sems, recv_sems, local_sems):
        x, y, c, chips = _mesh_pos()
        me = 2 * x + y
        sibling = (x, y, 1 - c)
        u, i, j = pl.program_id(0), pl.program_id(1), pl.program_id(2)
        n = (u * ni + i) * nj + j
        m = u * ni + i

        def rows_start(m_):
            pltpu.make_async_copy(x_ref.at[pl.ds(pl.multiple_of(lax.rem(m_, ni) * tm, tm), tm), :],
                                  xbuf.at[lax.rem(m_, 2)], xsem.at[lax.rem(m_, 2)]).start()

        def chip_of(q):
            return 2 * chips[q][0] + chips[q][1]

        def piece(ref, k, cc, r):
            return ref.at[pl.ds(cc * hd, hd), pl.ds(pl.multiple_of(k * sc + r * pc, COL_BLOCK), pc)]

        def wo_half(ref4, k, cc):
            return ref4.at[:, k, pl.ds(cc * hr, hr), :]

        def rcopy(sem, src, dst, to):
            return pltpu.make_async_remote_copy(src_ref=src, dst_ref=dst, send_sem=send_sems.at[sem],
                                                recv_sem=recv_sems.at[sem], device_id=to, device_id_type=MESH)

        def w_send(q, r):
            return rcopy(q * SUBS + r, piece(wpre_ref, me, c, r), piece(wf_ref, me, c, r), (*chips[q], c))

        def w_landed(q, r):
            return rcopy(q * SUBS + r, piece(wpre_ref, me, c, r), piece(wf_ref, chip_of(q), c, r), sibling)

        def w_forward(q, r, cc):
            ref = piece(wf_ref, chip_of(q), cc, r)
            return rcopy(9 + q * SUBS + r, ref, ref, sibling)

        def wo_send(q):
            return rcopy(18 + q, wo_ref.at[:, pl.ds(c * hr, hr), :], wo_half(wof_ref, me, c), (*chips[q], c))

        def wo_landed(q):
            return rcopy(18 + q, wo_ref.at[:, pl.ds(c * hr, hr), :], wo_half(wof_ref, chip_of(q), c), sibling)

        def wo_forward(q, cc):
            ref = wo_half(wof_ref, chip_of(q), cc)
            return rcopy(21 + q, ref, ref, sibling)

        def conv_send(q):
            return rcopy(24 + q, cw_ref, cwf_ref.at[me], (*chips[q], c))

        def local_copies():
            return [pltpu.make_async_copy(wo_ref, wof_ref.at[:, me], local_sems.at[0]),
                    pltpu.make_async_copy(cw_ref, cwf_ref.at[me], local_sems.at[1])]

        def tile_start(u_, j_, slot):
            g = order_ref[u_ // SUBS] * BLOCKS_PER_SHARD + lax.rem(u_, SUBS) * nj + j_
            cols = pl.ds(pl.multiple_of(g * COL_BLOCK, COL_BLOCK), COL_BLOCK)

            @pl.when(u_ < SUBS)
            def _():
                pltpu.make_async_copy(wpre_ref.at[:, cols], wbuf.at[slot], wsem.at[slot]).start()

            @pl.when(u_ >= SUBS)
            def _():
                pltpu.make_async_copy(wf_ref.at[:, cols], wbuf.at[slot], wsem.at[slot]).start()

        def end_of(u_):
            return jnp.logical_and(u == u_, jnp.logical_and(i == ni - 1, j == nj - 1))

        @pl.when(n == 0)
        def _():
            for cp in local_copies():
                cp.start()
            for r in range(SUBS):
                for q in (0, 1):
                    w_send(q, r).start()
            for q in range(3):
                conv_send(q).start()
            tile_start(0, 0, 0)
            rows_start(0)

        def pass_on(q, r):
            w_landed(q, r).wait_recv()
            w_forward(q, r, c).start()

        for u_ in range(1, units - 1):
            @pl.when(end_of(u_))
            def _(u_=u_):
                if u_ <= SUBS:
                    for q in (0, 1):
                        pass_on(q, u_ - 1)
                if u_ == SUBS:
                    for r in range(SUBS):
                        for q in (0, 1):
                            w_send(q, r).wait_send()
                    for r in range(SUBS):
                        w_send(2, r).start()
                if 2 * SUBS - 1 <= u_ <= 3 * SUBS - 2:
                    pass_on(2, u_ - (2 * SUBS - 1))
                if u_ == 3 * SUBS - 2:
                    for r in range(SUBS):
                        w_send(2, r).wait_send()
                    for q in range(3):
                        wo_send(q).start()
                nxt = u_ + 1
                if nxt >= SUBS:
                    w_forward(nxt // SUBS - 1, nxt % SUBS, 1 - c).wait_recv()

        n1 = n + 1

        @pl.when(n1 < total)
        def _():
            tile_start(n1 // (ni * nj), lax.rem(n1, nj), lax.rem(n1, 2))

        xslot = lax.rem(m, 2)

        @pl.when(j == 0)
        def _():
            @pl.when(m + 1 < units * ni)
            def _():
                rows_start(m + 1)

            pltpu.make_async_copy(x_ref.at[pl.ds(0, tm), :], xbuf.at[xslot], xsem.at[xslot]).wait()

        slot = lax.rem(n, 2)
        pltpu.make_async_copy(wpre_ref.at[:, pl.ds(0, COL_BLOCK)], wbuf.at[slot], wsem.at[slot]).wait()
        p_ref[...] = _dot(xbuf[xslot], wbuf[slot]).astype(BF16)

        @pl.when(n == total - 1)
        def _():
            for q in range(3):
                wo_landed(q).wait_recv()
                wo_forward(q, c).start()
            for q in range(3):
                wo_forward(q, 1 - c).wait_recv()
                rcopy(24 + q, cw_ref, cwf_ref.at[chip_of(q)], sibling).wait_recv()
            for q in range(3):
                for r in range(SUBS):
                    w_forward(q, r, c).wait_send()
                wo_send(q).wait_send()
                wo_forward(q, c).wait_send()
                conv_send(q).wait_send()
            for cp in local_copies():
                cp.wait()

    def p_map(u, i, j, o):
        g = o[u // SUBS] * BLOCKS_PER_SHARD + lax.rem(u, SUBS) * nj + j
        return (_slot_of_seg(g // BLOCKS_PER_SEG), i, lax.rem(g, BLOCKS_PER_SEG))

    return pl.pallas_call(
        body, name="gather_and_project",
        grid_spec=pltpu.PrefetchScalarGridSpec(
            num_scalar_prefetch=1, grid=(units, ni, nj),
            in_specs=[ANY, ANY, ANY, ANY],
            out_specs=(pl.BlockSpec((None, tm, COL_BLOCK), p_map), ANY, ANY, ANY),
            scratch_shapes=[pltpu.VMEM((2, d, COL_BLOCK), BF16), pltpu.SemaphoreType.DMA((2,)),
                            pltpu.VMEM((2, tm, D_MODEL), BF16), pltpu.SemaphoreType.DMA((2,)),
                            pltpu.SemaphoreType.DMA((27,)), pltpu.SemaphoreType.DMA((27,)),
                            pltpu.SemaphoreType.DMA((2,))]),
        out_shape=(jax.ShapeDtypeStruct((N_SLOTS, t, D_MODEL), BF16),
                   jax.ShapeDtypeStruct((d, N_CHIPS * sc), BF16),
                   jax.ShapeDtypeStruct((3, N_CHIPS, rows, D_MODEL), BF16),
                   jax.ShapeDtypeStruct((N_CHIPS,) + conv_w8.shape, F32)),
        input_output_aliases={2: 1},
        compiler_params=pltpu.CompilerParams(dimension_semantics=("arbitrary",) * 3, vmem_limit_bytes=VMEM_LIMIT,
                                             has_side_effects=True),
    )(order, xb, w_pre, wo_b, conv_w8)


def _exchange_halves(g_o):
    hr = SHARD_ROWS // 2
    g_o4 = [g.reshape(N_CHIPS, 2, hr, D_MODEL) for g in g_o]

    def body(ga_ref, gb_ref, gc_ref, ra_ref, rb_ref, rc_ref, send_sems, recv_sems):
        x, y, c, _ = _mesh_pos()
        sibling = (x, y, 1 - c)
        cps = []
        for n, (g_ref, r_ref) in enumerate(((ga_ref, ra_ref), (gb_ref, rb_ref), (gc_ref, rc_ref))):
            cps.append(pltpu.make_async_remote_copy(src_ref=g_ref.at[:, 1 - c], dst_ref=r_ref,
                                                    send_sem=send_sems.at[n], recv_sem=recv_sems.at[n],
                                                    device_id=sibling, device_id_type=MESH))
        for cp in cps:
            cp.start()
        for cp in cps:
            cp.wait()

    o_shape = jax.ShapeDtypeStruct((N_CHIPS, hr, D_MODEL), BF16)
    return pl.pallas_call(
        body, name="rs_exchange_halves",
        in_specs=[ANY] * 3, out_specs=(ANY,) * 3,
        out_shape=(o_shape, o_shape, o_shape),
        scratch_shapes=[pltpu.SemaphoreType.DMA((3,)), pltpu.SemaphoreType.DMA((3,))],
        compiler_params=pltpu.CompilerParams(has_side_effects=True),
    )(*g_o4)


def _add_halves(g_in, g_o, r_in, r_o):
    d, c9 = g_in.shape
    hd = d // 2
    hr = SHARD_ROWS // 2
    core = lax.axis_index("c").astype(jnp.int32).reshape(1)
    tm = min(512, hd)
    nb = hd // tm

    def body_in(c_ref, g_ref, r_ref, o_ref):
        o_ref[...] = (g_ref[...].astype(F32) + r_ref[...].astype(F32)).astype(BF16)

    s_in = pl.pallas_call(
        body_in, name="rs_add_halves_in",
        grid_spec=pltpu.PrefetchScalarGridSpec(
            num_scalar_prefetch=1, grid=(N_CHIPS, nb),
            in_specs=[pl.BlockSpec((tm, SHARD_COLS), lambda k, i, c_ref: (c_ref[0] * nb + i, k)),
                      pl.BlockSpec((tm, SHARD_COLS), lambda k, i, c_ref: (i, k))],
            out_specs=pl.BlockSpec((None, tm, SHARD_COLS), lambda k, i, c_ref: (k, i, 0))),
        out_shape=jax.ShapeDtypeStruct((N_CHIPS, hd, SHARD_COLS), BF16),
        compiler_params=_params("parallel", "parallel"),
    )(core, g_in, r_in)

    g_o4 = [g.reshape(N_CHIPS, 2, hr, D_MODEL) for g in g_o]

    def body_o(c_ref, ga_ref, gb_ref, gc_ref, ra_ref, rb_ref, rc_ref, o_ref):
        for n, (g_ref, r_ref) in enumerate(((ga_ref, ra_ref), (gb_ref, rb_ref), (gc_ref, rc_ref))):
            o_ref[n] = (g_ref[...].astype(F32) + r_ref[...].astype(F32)).astype(BF16)

    gspec = pl.BlockSpec((None, None, hr, D_MODEL), lambda k, c_ref: (k, c_ref[0], 0, 0))
    rspec = pl.BlockSpec((None, hr, D_MODEL), lambda k, c_ref: (k, 0, 0))
    s_o = pl.pallas_call(
        body_o, name="rs_add_halves_o",
        grid_spec=pltpu.PrefetchScalarGridSpec(
            num_scalar_prefetch=1, grid=(N_CHIPS,),
            in_specs=[gspec] * 3 + [rspec] * 3,
            out_specs=pl.BlockSpec((3, None, hr, D_MODEL), lambda k, c_ref: (0, k, 0, 0))),
        out_shape=jax.ShapeDtypeStruct((3, N_CHIPS, hr, D_MODEL), BF16),
        compiler_params=_params("parallel"),
    )(core, *g_o4, *r_o)
    return s_in, s_o


def _sum_chips(r_in, r_o, s_in, s_o):
    _, hd, sc = r_in.shape
    hr = r_o.shape[2]
    tm = min(256, hd)
    nb = hd // tm
    pos = jnp.stack([2 * lax.axis_index("x") + lax.axis_index("y"), lax.axis_index("c")]).astype(jnp.int32)

    def chip_sum(pos_ref, r_ref, s_ref):
        acc = None
        for k in range(N_CHIPS):
            term = jnp.where(pos_ref[0] == k, s_ref[...], r_ref[k]).astype(F32)
            acc = term if acc is None else acc + term
        return acc

    def body_in(pos_ref, r_ref, s_ref, o_ref):
        o_ref[...] = chip_sum(pos_ref, r_ref, s_ref)

    f_in = pl.pallas_call(
        body_in, name="rs_sum_chips_in",
        grid_spec=pltpu.PrefetchScalarGridSpec(
            num_scalar_prefetch=1, grid=(nb,),
            in_specs=[pl.BlockSpec((N_CHIPS, tm, sc), lambda i, p: (0, i, 0)),
                      pl.BlockSpec((None, tm, sc), lambda i, p: (p[0], i, 0))],
            out_specs=pl.BlockSpec((tm, sc), lambda i, p: (p[1] * nb + i, 0))),
        out_shape=jax.ShapeDtypeStruct((2 * hd, sc), F32),
        compiler_params=_params("parallel"),
    )(pos, r_in, s_in)

    def body_o(pos_ref, r_ref, s_ref, o_ref):
        o_ref[...] = chip_sum(pos_ref, r_ref, s_ref)

    f_o = pl.pallas_call(
        body_o, name="rs_sum_chips_o",
        grid_spec=pltpu.PrefetchScalarGridSpec(
            num_scalar_prefetch=1, grid=(3,),
            in_specs=[pl.BlockSpec((N_CHIPS, None, hr, D_MODEL), lambda n, p: (0, n, 0, 0)),
                      pl.BlockSpec((None, None, hr, D_MODEL), lambda n, p: (n, p[0], 0, 0))],
            out_specs=pl.BlockSpec((None, hr, D_MODEL), lambda n, p: (n, p[1], 0))),
        out_shape=jax.ShapeDtypeStruct((3, 2 * hr, D_MODEL), F32),
        compiler_params=_params("parallel"),
    )(pos, r_o, s_o)
    return f_in, f_o


def _share_halves(f_in, f_o):
    hd, sc = f_in.shape[0] // 2, f_in.shape[1]
    hr = f_o.shape[1] // 2

    def body(fi_ref, fo_ref, gi_ref, go_ref, send_sems, recv_sems):
        x, y, c, _ = _mesh_pos()
        sibling = (x, y, 1 - c)

        def halves(cc):
            rows_i, rows_o = pl.ds(cc * hd, hd), pl.ds(cc * hr, hr)
            return (fi_ref.at[rows_i, :], gi_ref.at[rows_i, :]), (fo_ref.at[:, rows_o, :], go_ref.at[:, rows_o, :])

        def copies(cc):
            return [pltpu.make_async_remote_copy(src_ref=src, dst_ref=dst, send_sem=send_sems.at[n],
                                                 recv_sem=recv_sems.at[n], device_id=sibling, device_id_type=MESH)
                    for n, (src, dst) in enumerate(halves(cc))]

        sends = copies(c)
        for cp in sends:
            cp.start()
        for cp in copies(1 - c):
            cp.wait_recv()
        for cp in sends:
            cp.wait_send()

    return pl.pallas_call(
        body, name="rs_share_halves",
        in_specs=[ANY, ANY], out_specs=(ANY, ANY),
        out_shape=(jax.ShapeDtypeStruct(f_in.shape, F32), jax.ShapeDtypeStruct(f_o.shape, F32)),
        scratch_shapes=[pltpu.SemaphoreType.DMA((2,)), pltpu.SemaphoreType.DMA((2,))],
        input_output_aliases={0: 0, 1: 1},
        compiler_params=pltpu.CompilerParams(has_side_effects=True),
    )(f_in, f_o)


def _merge_forward(ya, yb, wo3, p, bg, tm, tn):
    t = ya.shape[0]

    def body(ya_ref, yb_ref, wa_ref, wb_ref, g_ref, bg_ref, m_ref, oab_ref):
        oa = _dot(ya_ref[...], wa_ref[...])
        ob = _dot(yb_ref[...], wb_ref[...])
        ga = _sigmoid(g_ref[0].astype(F32) + bg_ref[0])
        gb = _sigmoid(g_ref[1].astype(F32) + bg_ref[1])
        m_ref[...] = (ga * oa + gb * ob).astype(BF16)
        oab_ref[0] = oa.astype(BF16)
        oab_ref[1] = ob.astype(BF16)

    return pl.pallas_call(
        body, name="merge_forward", grid=(t // tm, D_MODEL // tn),
        in_specs=[pl.BlockSpec((tm, D_MODEL), lambda i, j: (i, 0)),
                  pl.BlockSpec((tm, D_MODEL), lambda i, j: (i, 0)),
                  pl.BlockSpec((None, D_MODEL, tn), lambda i, j: (0, 0, j)),
                  pl.BlockSpec((None, D_MODEL, tn), lambda i, j: (1, 0, j)),
                  pl.BlockSpec((2, tm, tn), lambda i, j: (BLOCK_G, i, j)),
                  pl.BlockSpec((2, 1, tn), lambda i, j: (0, 0, j))],
        out_specs=(pl.BlockSpec((tm, tn), lambda i, j: (i, j)),
                   pl.BlockSpec((2, tm, tn), lambda i, j: (0, i, j))),
        out_shape=(jax.ShapeDtypeStruct((t, D_MODEL), BF16), jax.ShapeDtypeStruct((2, t, D_MODEL), BF16)),
        compiler_params=_params("parallel", "parallel"),
    )(ya, yb, wo3, wo3, p, bg)


HEAD_ROWS = 256


def _head(merged, wo3, x, target, ln_g, ln_b, tm):
    t = x.shape[0]
    inv_d = 1.0 / D_MODEL

    def body(m_ref, w_ref, x_ref, t_ref, g_ref, b_ref, dr_ref, gx_ref, dm_ref, dg_ref, db_ref, loss_ref):
        i = pl.program_id(0)

        @pl.when(i == 0)
        def _():
            dg_ref[...] = jnp.zeros_like(dg_ref)
            db_ref[...] = jnp.zeros_like(db_ref)
            loss_ref[...] = jnp.zeros_like(loss_ref)

        w = w_ref[...]
        g = g_ref[...]
        tiles = [slice(r0, r0 + HEAD_ROWS) for r0 in range(0, tm, HEAD_ROWS)]
        firsts = [_dot(m_ref[rows, :], w) for rows in tiles]
        for rows, out in zip(tiles, firsts):
            r = DN_ALPHA * x_ref[rows, :] + out
            mu = jnp.mean(r, axis=-1, keepdims=True)
            xc = r - mu
            var = jnp.mean(xc * xc, axis=-1, keepdims=True)
            rstd = lax.rsqrt(var + LN_EPS)
            xhat = xc * rstd
            e = xhat * g + b_ref[...] - t_ref[rows, :]
            se = jnp.sum(jnp.sum(e * e, axis=1, keepdims=True), axis=0, keepdims=True)
            loss_ref[...] += jnp.broadcast_to((0.5 * inv_d) * se, loss_ref.shape)
            dy = e * inv_d
            db_ref[...] += jnp.sum(dy, axis=0, keepdims=True)
            dg_ref[...] += jnp.sum(dy * xhat, axis=0, keepdims=True)
            dxh = dy * g
            m1 = jnp.mean(dxh, axis=-1, keepdims=True)
            m2 = jnp.mean(dxh * xhat, axis=-1, keepdims=True)
            dr = rstd * (dxh - m1 - xhat * m2)
            gx_ref[rows, :] = DN_ALPHA * dr
            drb = dr.astype(BF16)
            dr_ref[rows, :] = drb
            dm_ref[rows, :] = _dot(drb, w, 1, 1).astype(BF16)

    row = pl.BlockSpec((tm, D_MODEL), lambda i: (i, 0))
    vec = pl.BlockSpec((1, D_MODEL), lambda i: (0, 0))
    return pl.pallas_call(
        body, name="head", grid=(t // tm,),
        in_specs=[row, pl.BlockSpec((None, D_MODEL, D_MODEL), lambda i: (2, 0, 0), pipeline_mode=pl.Buffered(1)),
                  row, row, vec, vec],
        out_specs=(row, row, row, vec, vec, pl.BlockSpec((1, 128), lambda i: (0, 0))),
        out_shape=(jax.ShapeDtypeStruct((t, D_MODEL), BF16), jax.ShapeDtypeStruct((t, D_MODEL), F32),
                   jax.ShapeDtypeStruct((t, D_MODEL), BF16), jax.ShapeDtypeStruct((1, D_MODEL), F32),
                   jax.ShapeDtypeStruct((1, D_MODEL), F32), jax.ShapeDtypeStruct((1, 128), F32)),
        compiler_params=_params("arbitrary"),
    )(merged, wo3, x, target, ln_g, ln_b)


def _gate_and_branch_backward(dmerged, oab, p, bg, wo3, tm):
    t = dmerged.shape[0]

    def body(dm_ref, oab_ref, g_ref, bg_ref, wa_ref, wb_ref, do_ref, dpg_ref, dbg_ref, dy_ref):
        @pl.when(pl.program_id(0) == 0)
        def _():
            dbg_ref[...] = jnp.zeros_like(dbg_ref)

        dm = dm_ref[...].astype(F32)
        for n, w_ref in enumerate((wa_ref, wb_ref)):
            gate = _sigmoid(g_ref[n].astype(F32) + bg_ref[n])
            d_o = (dm * gate).astype(BF16)
            do_ref[n] = d_o
            dgate = dm * oab_ref[n].astype(F32) * gate * (1.0 - gate)
            dpg_ref[n] = dgate.astype(BF16)
            dbg_ref[n] += jnp.sum(dgate, axis=0, keepdims=True)
            dy_ref[n] = _dot(d_o, w_ref[...], 1, 1).astype(BF16)

    pair = pl.BlockSpec((2, tm, D_MODEL), lambda i: (0, i, 0))
    gates = pl.BlockSpec((2, tm, D_MODEL), lambda i: (BLOCK_G, i, 0))
    vec = pl.BlockSpec((2, 1, D_MODEL), lambda i: (0, 0, 0))

    def weight(n):
        return pl.BlockSpec((None, D_MODEL, D_MODEL), lambda i: (n, 0, 0), pipeline_mode=pl.Buffered(1))

    pair_shape = jax.ShapeDtypeStruct((2, t, D_MODEL), BF16)
    return pl.pallas_call(
        body, name="gate_and_branch_backward", grid=(t // tm,),
        in_specs=[pl.BlockSpec((tm, D_MODEL), lambda i: (i, 0)), pair, gates, vec, weight(0), weight(1)],
        out_specs=(pair, gates, vec, pair),
        out_shape=(pair_shape, jax.ShapeDtypeStruct((N_SLOTS, t, D_MODEL), BF16),
                   jax.ShapeDtypeStruct((2, 1, D_MODEL), F32), pair_shape),
        compiler_params=_params("arbitrary"),
    )(dmerged, oab, p, bg, wo3, wo3)


def _weight_grad(a, a_sel, b, b_sel, tm, tn, tk, name, exchange=None):
    t = a.shape[-2]
    nk = t // tk
    ni, nj = D_MODEL // tm, D_MODEL // tn

    def body(a_ref, b_ref, *rest):
        if exchange is None:
            o_ref, acc_ref = rest
        else:
            g_ref, o_ref, r_ref, acc_ref, send_sem, recv_sem = rest
        i, j, k = pl.program_id(0), pl.program_id(1), pl.program_id(2)

        if exchange is not None:
            x, y, c, _ = _mesh_pos()
            hd = exchange.shape[0] // 2
            swap = pltpu.make_async_remote_copy(src_ref=g_ref.at[pl.ds((1 - c) * hd, hd), :], dst_ref=r_ref,
                                                send_sem=send_sem, recv_sem=recv_sem,
                                                device_id=(x, y, 1 - c), device_id_type=MESH)

            @pl.when(jnp.logical_and(i == 0, jnp.logical_and(j == 0, k == 0)))
            def _():
                swap.start()

        @pl.when(k == 0)
        def _():
            acc_ref[...] = jnp.zeros_like(acc_ref)

        acc_ref[...] += _dot(a_ref[...], b_ref[...], 0, 0)

        @pl.when(k == nk - 1)
        def _():
            o_ref[...] = acc_ref[...].astype(BF16)

        if exchange is not None:
            @pl.when(jnp.logical_and(i == ni - 1, jnp.logical_and(j == nj - 1, k == nk - 1)))
            def _():
                swap.wait()

    def spec(arr, sel, width, which):
        if arr.ndim == 2:
            return pl.BlockSpec((tk, width), lambda i, j, k: (k, (i, j)[which]))
        return pl.BlockSpec((None, tk, width), lambda i, j, k: (sel, k, (i, j)[which]))

    o_spec = pl.BlockSpec((tm, tn), lambda i, j, k: (i, j))
    o_shape = jax.ShapeDtypeStruct((D_MODEL, D_MODEL), BF16)
    if exchange is None:
        return pl.pallas_call(
            body, name=name, grid=(ni, nj, nk),
            in_specs=[spec(a, a_sel, tm, 0), spec(b, b_sel, tn, 1)],
            out_specs=o_spec, out_shape=o_shape,
            scratch_shapes=[pltpu.VMEM((tm, tn), F32)],
            compiler_params=_params("parallel", "parallel", "arbitrary"),
        )(a, b)
    return pl.pallas_call(
        body, name=name, grid=(ni, nj, nk),
        in_specs=[spec(a, a_sel, tm, 0), spec(b, b_sel, tn, 1), ANY],
        out_specs=(o_spec, ANY),
        out_shape=(o_shape, jax.ShapeDtypeStruct((exchange.shape[0] // 2, exchange.shape[1]), BF16)),
        scratch_shapes=[pltpu.VMEM((tm, tn), F32), pltpu.SemaphoreType.DMA, pltpu.SemaphoreType.DMA],
        compiler_params=pltpu.CompilerParams(dimension_semantics=("arbitrary",) * 3, vmem_limit_bytes=VMEM_LIMIT,
                                             has_side_effects=True),
    )(a, b, exchange)


def _win_grad(xt, dp, tn, tk):
    _, t, _ = dp.shape
    nk = t // tk
    per_seg = D_MODEL // tn
    nj = N_SEG * per_seg

    def body(x_ref, dp_ref, o_ref, acc_ref):
        k = pl.program_id(1)

        @pl.when(k == 0)
        def _():
            acc_ref[...] = jnp.zeros_like(acc_ref)

        acc_ref[...] += _dot(x_ref[...], dp_ref[...])

        @pl.when(k == nk - 1)
        def _():
            o_ref[...] = acc_ref[...].astype(BF16)

    return pl.pallas_call(
        body, name="grad_w_in", grid=(nj, nk),
        in_specs=[pl.BlockSpec((D_MODEL, tk), lambda j, k: (0, k)),
                  pl.BlockSpec((None, tk, tn), lambda j, k: (_slot_of_seg(j // per_seg), k, j % per_seg))],
        out_specs=pl.BlockSpec((D_MODEL, tn), lambda j, k: (0, j)),
        out_shape=jax.ShapeDtypeStruct((D_MODEL, N_SEG * D_MODEL), BF16),
        scratch_shapes=[pltpu.VMEM((D_MODEL, tn), F32)],
        compiler_params=_params("parallel", "arbitrary"),
    )(xt, dp)


def _input_grad_and_scatter(dp, w_full, gx, s_in, s_o, small, tm):
    _, t, _ = dp.shape
    ni, nk = t // tm, N_SEG - TAIL_SEGS
    _, hd, sc = s_in.shape
    hr = s_o.shape[2]

    def body(dp_ref, w_ref, gx_ref, si_ref, so_ref, sm_ref, o_ref, ri_ref, ro_ref, ga_ref,
             send_sems, recv_sems, local_sem):
        i, k = pl.program_id(0), pl.program_id(1)
        x, y, c, chips = _mesh_pos()
        me = 2 * x + y
        dev = 4 * x + 2 * y + c

        def peer(r):
            return (x ^ ((r >> 2) & 1), y ^ ((r >> 1) & 1), c ^ (r & 1))

        def sends():
            cps = []
            for q, (cx, cy) in enumerate(chips):
                dest = 2 * cx + cy
                cps.append(pltpu.make_async_remote_copy(src_ref=si_ref.at[dest], dst_ref=ri_ref.at[me],
                                                        send_sem=send_sems.at[q], recv_sem=recv_sems.at[q],
                                                        device_id=(cx, cy, c), device_id_type=MESH))
                cps.append(pltpu.make_async_remote_copy(src_ref=so_ref.at[:, dest], dst_ref=ro_ref.at[me],
                                                        send_sem=send_sems.at[3 + q], recv_sem=recv_sems.at[3 + q],
                                                        device_id=(cx, cy, c), device_id_type=MESH))
            for r in range(1, 8):
                cps.append(pltpu.make_async_remote_copy(src_ref=sm_ref, dst_ref=ga_ref.at[dev],
                                                        send_sem=send_sems.at[5 + r], recv_sem=recv_sems.at[5 + r],
                                                        device_id=peer(r), device_id_type=MESH))
            return cps

        own_small = pltpu.make_async_copy(sm_ref, ga_ref.at[dev], local_sem)

        @pl.when(jnp.logical_and(i == 0, k == 0))
        def _():
            for cp in sends():
                cp.start()
            own_small.start()

        @pl.when(k == 0)
        def _():
            o_ref[...] = gx_ref[...]

        o_ref[...] += _dot(dp_ref[...], w_ref[...], 1, 1)

        @pl.when(jnp.logical_and(i == ni - 1, k == nk - 1))
        def _():
            for q, (cx, cy) in enumerate(chips):
                frm = 2 * cx + cy
                pltpu.make_async_remote_copy(src_ref=si_ref.at[frm], dst_ref=ri_ref.at[frm], send_sem=send_sems.at[q],
                                             recv_sem=recv_sems.at[q], device_id=(x, y, c),
                                             device_id_type=MESH).wait_recv()
                pltpu.make_async_remote_copy(src_ref=so_ref.at[:, frm], dst_ref=ro_ref.at[frm],
                                             send_sem=send_sems.at[3 + q], recv_sem=recv_sems.at[3 + q],
                                             device_id=(x, y, c), device_id_type=MESH).wait_recv()
            for r in range(1, 8):
                px, py, pc = peer(r)
                pltpu.make_async_remote_copy(src_ref=sm_ref, dst_ref=ga_ref.at[4 * px + 2 * py + pc],
                                             send_sem=send_sems.at[5 + r], recv_sem=recv_sems.at[5 + r],
                                             device_id=(x, y, c), device_id_type=MESH).wait_recv()
            for cp in sends():
                cp.wait_send()
            own_small.wait()

    return pl.pallas_call(
        body, name="grad_x_and_scatter", grid=(ni, nk),
        in_specs=[pl.BlockSpec((None, tm, D_MODEL), lambda i, k: (_slot_of_seg(k), i, 0)),
                  pl.BlockSpec((D_MODEL, D_MODEL), lambda i, k: (0, k)),
                  pl.BlockSpec((tm, D_MODEL), lambda i, k: (i, 0), pipeline_mode=pl.Buffered(1)), ANY, ANY, ANY],
        out_specs=(pl.BlockSpec((tm, D_MODEL), lambda i, k: (i, 0)), ANY, ANY, ANY),
        out_shape=(jax.ShapeDtypeStruct((t, D_MODEL), F32),
                   jax.ShapeDtypeStruct((N_CHIPS, hd, sc), BF16),
                   jax.ShapeDtypeStruct((N_CHIPS, 3, hr, D_MODEL), BF16),
                   jax.ShapeDtypeStruct((8,) + small.shape, small.dtype)),
        scratch_shapes=[pltpu.SemaphoreType.DMA((13,)), pltpu.SemaphoreType.DMA((13,)), pltpu.SemaphoreType.DMA],
        input_output_aliases={2: 0},
        compiler_params=pltpu.CompilerParams(dimension_semantics=("arbitrary", "arbitrary"),
                                             vmem_limit_bytes=VMEM_LIMIT, has_side_effects=True),
    )(dp, w_full, gx, s_in, s_o, small)


TAIL_SEGS = 1


def _input_grad_tail_and_adam(dp, w_full, partial, w, g, m, v, steps):
    _, t, _ = dp.shape
    tm = t // steps
    r, c = w.shape
    ra = r // steps
    segs = tuple(range(N_SEG - TAIL_SEGS, N_SEG))

    def body(*refs):
        dp_refs = refs[:TAIL_SEGS]
        w_refs = refs[TAIL_SEGS:2 * TAIL_SEGS]
        part_ref, aw_ref, ag_ref, am_ref, av_ref, o_ref, go_ref, d_ref, mo_ref, vo_ref = refs[2 * TAIL_SEGS:]
        acc = part_ref[...]
        for dp_ref, w_ref in zip(dp_refs, w_refs):
            acc = acc + _dot(dp_ref[...], w_ref[...], 1, 1)
        o_ref[...] = acc
        gv = ag_ref[...]
        d, mn, vn = _adam_math(aw_ref[...], gv, am_ref[...], av_ref[...])
        go_ref[...] = gv
        d_ref[...] = d
        mo_ref[...] = mn
        vo_ref[...] = vn

    row = pl.BlockSpec((tm, D_MODEL), lambda i: (i, 0))
    arow = pl.BlockSpec((ra, c), lambda i: (i, 0))
    ashape = jax.ShapeDtypeStruct((r, c), F32)
    in_specs = [pl.BlockSpec((None, tm, D_MODEL), functools.partial(lambda s, i: (SLOT_OF_SEG[s], i, 0), s))
                for s in segs]
    in_specs += [pl.BlockSpec((D_MODEL, D_MODEL), functools.partial(lambda s, i: (0, s), s),
                              pipeline_mode=pl.Buffered(1)) for s in segs]
    in_specs += [row, arow, arow, arow, arow]
    res = pl.pallas_call(
        body, name="grad_x_tail_and_adam_w_in", grid=(steps,),
        in_specs=in_specs,
        out_specs=(row, arow, arow, arow, arow),
        out_shape=(jax.ShapeDtypeStruct((t, D_MODEL), F32), ashape, ashape, ashape, ashape),
        compiler_params=_params("parallel"),
    )(*([dp] * TAIL_SEGS), *([w_full] * TAIL_SEGS), partial, w, g, m, v)
    return res[0], res[1:]


def _sgu_chunk_forward(u, v, z, wm, bs, lng, lnb):
    ug, dug = _gelu_and_grad(u)
    vg, dvg = _gelu_and_grad(v)
    mu = jnp.mean(vg, axis=-1, keepdims=True)
    xc = vg - mu
    var = jnp.mean(xc * xc, axis=-1, keepdims=True)
    rstd = lax.rsqrt(var + LN_EPS)
    vhat = xc * rstd
    vln = (vhat * lng + lnb).astype(BF16)
    mixed = _dot(wm, vln) + bs
    sig = _sigmoid(z)
    return ug, dug, dvg, rstd, vhat, vln, mixed, sig


def _mixer_a_forward(p_a, wm, bs_col, ln_v_g, ln_v_b, tm):
    t = p_a.shape[1]

    def body(p_ref, wm_ref, bs_ref, g_ref, b_ref, o_ref):
        wm_v, bs_v, lng, lnb = wm_ref[...], bs_ref[...], g_ref[...], b_ref[...]

        def chunk(ci, carry):
            rows = pl.ds(pl.multiple_of(ci * CHUNK, CHUNK), CHUNK)
            u = p_ref[0, rows, :].astype(F32)
            v = p_ref[1, rows, :].astype(F32)
            z = p_ref[2, rows, :].astype(F32)
            ug, _, _, _, _, _, mixed, sig = _sgu_chunk_forward(u, v, z, wm_v, bs_v, lng, lnb)
            o_ref[rows, :] = (ug * mixed * (z * sig)).astype(BF16)
            return carry

        lax.fori_loop(0, tm // CHUNK, chunk, 0, unroll=True)

    return pl.pallas_call(
        body, name="mixer_a_forward", grid=(t // tm, N_HEADS),
        in_specs=[pl.BlockSpec((3, tm, HEAD_DIM), lambda i, h: (0, i, h)),
                  pl.BlockSpec((None, CHUNK, CHUNK), lambda i, h: (h, 0, 0)),
                  pl.BlockSpec((None, CHUNK, 1), lambda i, h: (h, 0, 0)),
                  pl.BlockSpec((1, HEAD_DIM), lambda i, h: (0, h)),
                  pl.BlockSpec((1, HEAD_DIM), lambda i, h: (0, h))],
        out_specs=pl.BlockSpec((tm, HEAD_DIM), lambda i, h: (i, h)),
        out_shape=jax.ShapeDtypeStruct((t, D_MODEL), BF16),
        compiler_params=_params("parallel", "parallel"),
    )(p_a, wm, bs_col, ln_v_g, ln_v_b)


def _mixer_a_backward(p_a, dyab, wm, bs_col, ln_v_g, ln_v_b, dp, tm):
    t = p_a.shape[1]

    def body(p_ref, dy_ref, wm_ref, bs_ref, g_ref, b_ref, dp_in, dp_ref, dws_ref, dbs_ref, dg_ref, db_ref):
        @pl.when(pl.program_id(1) == 0)
        def _():
            dws_ref[...] = jnp.zeros_like(dws_ref)
            dbs_ref[...] = jnp.zeros_like(dbs_ref)
            dg_ref[...] = jnp.zeros_like(dg_ref)
            db_ref[...] = jnp.zeros_like(db_ref)

        wm_v, bs_v, lng, lnb = wm_ref[...], bs_ref[...], g_ref[...], b_ref[...]
        causal = (lax.broadcasted_iota(jnp.int32, (CHUNK, CHUNK), 1)
                  <= lax.broadcasted_iota(jnp.int32, (CHUNK, CHUNK), 0))

        def chunk(ci, carry):
            rows = pl.ds(pl.multiple_of(ci * CHUNK, CHUNK), CHUNK)
            u = p_ref[0, rows, :].astype(F32)
            v = p_ref[1, rows, :].astype(F32)
            z = p_ref[2, rows, :].astype(F32)
            dy = dy_ref[rows, :].astype(F32)
            ug, dug, dvg, rstd, vhat, vln, mixed, sig = _sgu_chunk_forward(u, v, z, wm_v, bs_v, lng, lnb)
            sz = z * sig
            dmixed = dy * ug * sz
            dp_ref[0, rows, :] = (dy * mixed * sz * dug).astype(BF16)
            dp_ref[2, rows, :] = (dy * ug * mixed * _silu_grad(sig, sz)).astype(BF16)
            dbs_ref[...] += jnp.sum(dmixed, axis=1, keepdims=True)
            dmb = dmixed.astype(BF16)
            dws_ref[...] += jnp.where(causal, _dot(dmb, vln, 1, 1), 0.0)
            dvln = _dot(wm_v, dmb, 0, 0)
            db_ref[...] += jnp.sum(dvln, axis=0, keepdims=True)
            dg_ref[...] += jnp.sum(dvln * vhat, axis=0, keepdims=True)
            dvh = dvln * lng
            m1 = jnp.mean(dvh, axis=-1, keepdims=True)
            m2 = jnp.mean(dvh * vhat, axis=-1, keepdims=True)
            dp_ref[1, rows, :] = (rstd * (dvh - m1 - vhat * m2) * dvg).astype(BF16)
            return carry

        lax.fori_loop(0, tm // CHUNK, chunk, 0, unroll=True)

    return pl.pallas_call(
        body, name="mixer_a_backward", grid=(N_HEADS, t // tm),
        in_specs=[pl.BlockSpec((3, tm, HEAD_DIM), lambda h, i: (0, i, h)),
                  pl.BlockSpec((None, tm, HEAD_DIM), lambda h, i: (0, i, h)),
                  pl.BlockSpec((None, CHUNK, CHUNK), lambda h, i: (h, 0, 0)),
                  pl.BlockSpec((None, CHUNK, 1), lambda h, i: (h, 0, 0)),
                  pl.BlockSpec((1, HEAD_DIM), lambda h, i: (0, h)),
                  pl.BlockSpec((1, HEAD_DIM), lambda h, i: (0, h)), ANY],
        out_specs=(pl.BlockSpec((3, tm, HEAD_DIM), lambda h, i: (BLOCK_A, i, h)),
                   pl.BlockSpec((None, CHUNK, CHUNK), lambda h, i: (h, 0, 0)),
                   pl.BlockSpec((None, CHUNK, 1), lambda h, i: (h, 0, 0)),
                   pl.BlockSpec((1, HEAD_DIM), lambda h, i: (0, h)),
                   pl.BlockSpec((1, HEAD_DIM), lambda h, i: (0, h))),
        out_shape=(jax.ShapeDtypeStruct(dp.shape, BF16),
                   jax.ShapeDtypeStruct((N_HEADS, CHUNK, CHUNK), F32),
                   jax.ShapeDtypeStruct((N_HEADS, CHUNK, 1), F32),
                   jax.ShapeDtypeStruct((1, D_MODEL), F32), jax.ShapeDtypeStruct((1, D_MODEL), F32)),
        input_output_aliases={6: 0},
        compiler_params=_params("parallel", "arbitrary"),
    )(p_a, dyab, wm, bs_col, ln_v_g, ln_v_b, dp)


HALO = 16


def _conv_taps(h, halo_h, tm):
    row = lax.broadcasted_iota(jnp.int32, h.shape, 0)
    last1 = halo_h[HALO - 1:HALO, :]
    last2 = halo_h[HALO - 2:HALO - 1, :]
    h1 = jnp.where(row == 0, last1, pltpu.roll(h, 1, 0))
    h2 = jnp.where(row == 0, last2, jnp.where(row == 1, last1, pltpu.roll(h, 2, 0)))
    return h1, h2


def _mixer_b_forward(p_b, conv_w, conv_b, tm, tc):
    t = p_b.shape[1]

    def body(p_ref, halo_ref, w_ref, b_ref, o_ref):
        valid = (pl.program_id(1) > 0).astype(F32)
        h = p_ref[1].astype(F32) * p_ref[0].astype(F32)
        halo_h = halo_ref[1].astype(F32) * halo_ref[0].astype(F32) * valid
        h1, h2 = _conv_taps(h, halo_h, tm)
        w = w_ref[...]
        conv = b_ref[...] + w[0:1, :] * h2 + w[1:2, :] * h1 + w[2:3, :] * h
        z = p_ref[3].astype(F32)
        o_ref[...] = (p_ref[2].astype(F32) * conv * (z * _sigmoid(z))).astype(BF16)

    steps = tm // HALO
    return pl.pallas_call(
        body, name="mixer_b_forward", grid=(D_MODEL // tc, t // tm),
        in_specs=[pl.BlockSpec((4, tm, tc), lambda j, i: (BLOCK_B, i, j)),
                  pl.BlockSpec((4, HALO, tc), lambda j, i: (BLOCK_B, jnp.maximum(i * steps - 1, 0), j)),
                  pl.BlockSpec((3, tc), lambda j, i: (0, j)),
                  pl.BlockSpec((1, tc), lambda j, i: (0, j))],
        out_specs=pl.BlockSpec((tm, tc), lambda j, i: (i, j)),
        out_shape=jax.ShapeDtypeStruct((t, D_MODEL), BF16),
        compiler_params=_params("parallel", "parallel"),
    )(p_b, p_b, conv_w, conv_b)


def _mixer_b_backward(p_b, dyab, conv_w, conv_b, dp, tm, tc):
    t = p_b.shape[1]
    n = t // tm

    def body(p_ref, halo_ref, dy_ref, w_ref, b_ref, dp_in, dp_ref, dw_ref, db_ref, next_ref):
        ii = pl.program_id(1)

        @pl.when(ii == 0)
        def _():
            dw_ref[...] = jnp.zeros_like(dw_ref)
            db_ref[...] = jnp.zeros_like(db_ref)
            next_ref[...] = jnp.zeros_like(next_ref)

        valid = (ii < n - 1).astype(F32)
        xb = p_ref[0].astype(F32)
        cb = p_ref[1].astype(F32)
        bb = p_ref[2].astype(F32)
        z = p_ref[3].astype(F32)
        h = cb * xb
        halo_h = halo_ref[1].astype(F32) * halo_ref[0].astype(F32) * valid
        h1, h2 = _conv_taps(h, halo_h, tm)
        w = w_ref[...]
        w0, w1, w2 = w[0:1, :], w[1:2, :], w[2:3, :]
        conv = b_ref[...] + w0 * h2 + w1 * h1 + w2 * h
        sig = _sigmoid(z)
        sz = z * sig
        dy = dy_ref[...].astype(F32)
        dconv = dy * bb * sz
        dp_ref[2] = (dy * conv * sz).astype(BF16)
        dp_ref[3] = (dy * bb * conv * _silu_grad(sig, sz)).astype(BF16)
        db_ref[...] += jnp.sum(dconv, axis=0, keepdims=True)
        dw_ref[0:1, :] += jnp.sum(dconv * h2, axis=0, keepdims=True)
        dw_ref[1:2, :] += jnp.sum(dconv * h1, axis=0, keepdims=True)
        dw_ref[2:3, :] += jnp.sum(dconv * h, axis=0, keepdims=True)
        row = lax.broadcasted_iota(jnp.int32, h.shape, 0)
        nxt = next_ref[...]
        n0, n1 = nxt[0:1, :], nxt[1:2, :]
        d1 = jnp.where(row == tm - 1, n0, pltpu.roll(dconv, tm - 1, 0))
        d2 = jnp.where(row == tm - 1, n1, jnp.where(row == tm - 2, n0, pltpu.roll(dconv, tm - 2, 0)))
        dh = w2 * dconv + w1 * d1 + w0 * d2
        dp_ref[0] = (dh * cb).astype(BF16)
        dp_ref[1] = (dh * xb).astype(BF16)
        next_ref[...] = dconv[0:8, :]

    steps = tm // HALO
    return pl.pallas_call(
        body, name="mixer_b_backward", grid=(D_MODEL // tc, n),
        in_specs=[pl.BlockSpec((4, tm, tc), lambda j, ii: (BLOCK_B, n - 1 - ii, j)),
                  pl.BlockSpec((4, HALO, tc), lambda j, ii: (BLOCK_B, jnp.maximum((n - 1 - ii) * steps - 1, 0), j)),
                  pl.BlockSpec((None, tm, tc), lambda j, ii: (1, n - 1 - ii, j)),
                  pl.BlockSpec((3, tc), lambda j, ii: (0, j)),
                  pl.BlockSpec((1, tc), lambda j, ii: (0, j)), ANY],
        out_specs=(pl.BlockSpec((4, tm, tc), lambda j, ii: (BLOCK_B, n - 1 - ii, j)),
                   pl.BlockSpec((3, tc), lambda j, ii: (0, j)),
                   pl.BlockSpec((1, tc), lambda j, ii: (0, j))),
        out_shape=(jax.ShapeDtypeStruct(dp.shape, BF16),
                   jax.ShapeDtypeStruct((3, D_MODEL), F32), jax.ShapeDtypeStruct((1, D_MODEL), F32)),
        scratch_shapes=[pltpu.VMEM((8, tc), F32)],
        input_output_aliases={5: 0},
        compiler_params=_params("parallel", "arbitrary"),
    )(p_b, p_b, dyab, conv_w, conv_b, dp)


def _adam_math(w, g, m, v):
    m = ADAM_B1 * m + (1.0 - ADAM_B1) * g
    v = ADAM_B2 * v + (1.0 - ADAM_B2) * (g * g)
    delta = -ADAM_LR * ((m * ADAM_C1) / (jnp.sqrt(v * ADAM_C2) + ADAM_EPS) + ADAM_WD * w)
    return delta, m, v


def _adam_rows(w, g, m, v, tm, name, g_sel=None):
    r, c = w.shape

    def body(w_ref, g_ref, m_ref, v_ref, go_ref, d_ref, mo_ref, vo_ref):
        g = g_ref[...]
        d, mn, vn = _adam_math(w_ref[...], g, m_ref[...], v_ref[...])
        go_ref[...] = g
        d_ref[...] = d
        mo_ref[...] = mn
        vo_ref[...] = vn

    spec = pl.BlockSpec((tm, c), lambda i: (i, 0))
    g_spec = spec if g_sel is None else pl.BlockSpec((None, tm, c), lambda i: (g_sel, i, 0))
    shape = jax.ShapeDtypeStruct((r, c), F32)
    return pl.pallas_call(
        body, name=name, grid=(r // tm,),
        in_specs=[spec, g_spec, spec, spec], out_specs=(spec,) * 4, out_shape=(shape,) * 4,
        compiler_params=_params("parallel"),
    )(w, g, m, v)


SMALL_ROW0 = {name: sum(r for _, r in SMALL_ROWS[:i]) for i, (name, _) in enumerate(SMALL_ROWS)}
LANE_MAJOR = ("ln_g", "ln_b", "b_gate", "ln_v_g", "ln_v_b", "conv_b")


def _lane_pieces(n):
    return [(q, slice(q * 128, (q + 1) * 128)) for q in range(n // 128)]


def _pack_small(d_ln_g, d_ln_b, d_bg, d_lnv_g, d_lnv_b, d_ws, d_bs, d_cw, d_cb, loss_part):
    def body(lg, lb, bg, vg, vb, ws, bs, cw, cb, loss, o_ref):
        def put(row0, vec):
            for q, cols in _lane_pieces(vec.shape[1]):
                o_ref[row0 + q:row0 + q + 1, :] = vec[:, cols]

        put(SMALL_ROW0["ln_g"], lg[...])
        put(SMALL_ROW0["ln_b"], lb[...])
        for n in range(2):
            put(SMALL_ROW0["b_gate"] + n * (D_MODEL // 128), bg[n])
        put(SMALL_ROW0["ln_v_g"], vg[...])
        put(SMALL_ROW0["ln_v_b"], vb[...])
        for h in range(N_HEADS):
            o_ref[SMALL_ROW0["w_s"] + h * CHUNK:SMALL_ROW0["w_s"] + (h + 1) * CHUNK, :] = ws[h]
        o_ref[SMALL_ROW0["b_s"]:SMALL_ROW0["b_s"] + N_HEADS, :] = bs[...]
        for c in range(3):
            put(SMALL_ROW0["conv_w"] + c * (D_MODEL // 128), cw[c:c + 1, :])
        put(SMALL_ROW0["conv_b"], cb[...])
        o_ref[SMALL_ROW0["loss"]:SMALL_ROW0["loss"] + 8, :] = jnp.broadcast_to(loss[...], (8, 128))

    return pl.pallas_call(
        body, name="pack_small", out_shape=jax.ShapeDtypeStruct((SMALL_TOTAL, 128), F32), compiler_params=_params(),
    )(d_ln_g, d_ln_b, d_bg, d_lnv_g, d_lnv_b, d_ws, d_bs.reshape(N_HEADS, CHUNK), d_cw, d_cb, loss_part)


def _adam_small(gathered, params):
    names = list(params)
    flat = [a for n in names for a in params[n]]

    def body(*refs):
        ga_ref = refs[0]
        ins = refs[1:1 + 3 * len(names)]
        outs = refs[1 + 3 * len(names):-1]
        gs_ref = refs[-1]
        g = ga_ref[0]
        for k in range(1, 8):
            g = g + ga_ref[k]
        gs_ref[...] = g
        for i, name in enumerate(names):
            w_ref, m_ref, v_ref = ins[3 * i:3 * i + 3]
            o_refs = outs[4 * i:4 * i + 4]
            row0 = SMALL_ROW0[name]
            if name in LANE_MAJOR:
                pieces = [((slice(None), cols), slice(row0 + q, row0 + q + 1))
                          for q, cols in _lane_pieces(w_ref.shape[1])]
            elif name == "w_s":
                pieces = [((0, h), slice(row0 + h * CHUNK, row0 + (h + 1) * CHUNK)) for h in range(N_HEADS)]
            else:
                pieces = [((0,), slice(row0, row0 + N_HEADS))]
            for idx, rows in pieces:
                gp = gs_ref[rows, :]
                res = (gp,) + _adam_math(w_ref[idx], gp, m_ref[idx], v_ref[idx])
                for o_ref, val in zip(o_refs, res):
                    o_ref[idx] = val
        gcw_ref, loss_ref = outs[-2:]
        for c in range(3):
            for q, cols in _lane_pieces(D_MODEL):
                r = SMALL_ROW0["conv_w"] + c * (D_MODEL // 128) + q
                gcw_ref[c:c + 1, cols] = gs_ref[r:r + 1, :]
        loss_ref[...] = gs_ref[SMALL_ROW0["loss"]:SMALL_ROW0["loss"] + 1, :]

    out_shape = [jax.ShapeDtypeStruct(params[n][0].shape, F32) for n in names for _ in range(4)]
    out_shape += [jax.ShapeDtypeStruct((3, D_MODEL), F32), jax.ShapeDtypeStruct((1, 128), F32)]
    res = pl.pallas_call(
        body, name="adam_small", out_shape=tuple(out_shape),
        scratch_shapes=[pltpu.VMEM((SMALL_TOTAL, 128), F32)], compiler_params=_params(),
    )(gathered, *flat)
    return {n: res[4 * i:4 * i + 4] for i, n in enumerate(names)}, res[-2], res[-1]


def _adam_conv_w(g_all, chip, w, m, v):
    cols = w.shape[2]

    def body(c_ref, g_ref, w_ref, m_ref, v_ref, go_ref, d_ref, mo_ref, vo_ref):
        g = g_ref[...]
        d, mn, vn = _adam_math(w_ref[...], g, m_ref[...], v_ref[...])
        go_ref[...] = g
        d_ref[...] = d
        mo_ref[...] = mn
        vo_ref[...] = vn

    own = pl.BlockSpec((None, 3, cols), lambda i, c_ref: (0, 0, 0))
    return pl.pallas_call(
        body, name="adam_conv_w",
        grid_spec=pltpu.PrefetchScalarGridSpec(
            num_scalar_prefetch=1, grid=(1,),
            in_specs=[pl.BlockSpec((3, cols), lambda i, c_ref: (0, c_ref[0])), own, own, own],
            out_specs=(own,) * 4),
        out_shape=(jax.ShapeDtypeStruct(w.shape, F32),) * 4,
        compiler_params=_params("arbitrary"),
    )(chip, g_all, w, m, v)


def kernel(x, w_in, b_gate, ln_v_g, ln_v_b, w_s, b_s, conv_w, conv_b, w_oa, w_ob, w_out, ln_g, ln_b, loss_target, m_w_in, m_b_gate, m_ln_v_g, m_ln_v_b, m_w_s, m_b_s, m_conv_w, m_conv_b, m_w_oa, m_w_ob, m_w_out, m_ln_g, m_ln_b, v_w_in, v_b_gate, v_ln_v_g, v_ln_v_b, v_w_s, v_b_s, v_conv_w, v_conv_b, v_w_oa, v_w_ob, v_w_out, v_ln_g, v_ln_b):
    t = x.shape[1]
    x2 = x[0]
    target = loss_target[0]
    chip = 2 * lax.axis_index("x") + lax.axis_index("y")
    conv_cols = conv_w.shape[2]

    chip1 = chip.astype(jnp.int32).reshape(1)
    w_pre = _cast_into_columns(w_in[0], chip1, N_CHIPS, 256, "cast_w_in")
    wo_b, wm = _prep_small_weights(w_oa[0], w_ob[0], w_out[0], w_s[0])
    conv_w8 = jnp.concatenate([conv_w[0], jnp.zeros((5, conv_cols), F32)], axis=0)
    bs_col = b_s[0].reshape(N_HEADS, CHUNK, 1)
    bg = b_gate.reshape(2, 1, D_MODEL)

    xb, xt = _cast_and_transpose(x2, min(512, t))
    p, w_full, wo_full, cw_full = _gather_and_project(xb, w_pre, wo_b, conv_w8, min(4096, t))
    wo3 = wo_full.reshape(3, D_MODEL, D_MODEL)
    conv_w_all = jnp.transpose(cw_full[:, :3, :], (1, 0, 2)).reshape(3, D_MODEL)
    tm_a = min(1024, t)
    ya = _mixer_a_forward(p, wm, bs_col, ln_v_g, ln_v_b, tm_a)
    tm_b = min(512, t)
    yb = _mixer_b_forward(p, conv_w_all, conv_b, tm_b, 512)
    tm_m = min(1024, t)
    merged, oab = _merge_forward(ya, yb, wo3, p, bg, tm_m, 512)

    drb, gx, dmerged, d_ln_g, d_ln_b, loss_part = _head(merged, wo3, x2, target, ln_g, ln_b, min(512, t))
    doab, dp, d_bg, dyab = _gate_and_branch_backward(dmerged, oab, p, bg, wo3, min(256, t))
    dp, d_ws, d_bs, d_lnv_g, d_lnv_b = _mixer_a_backward(p, dyab, wm, bs_col, ln_v_g, ln_v_b, dp, tm_a)
    dp, d_cw, d_cb = _mixer_b_backward(p, dyab, conv_w_all, conv_b, dp, tm_b, 512)

    tk = min(2048, t)
    g_in = _win_grad(xt, dp, 1024, tk)
    g_oa = _weight_grad(ya, 0, doab, 0, 1024, 1024, tk, "grad_w_oa")
    g_ob = _weight_grad(yb, 0, doab, 1, 1024, 1024, tk, "grad_w_ob")
    g_out, r_in = _weight_grad(merged, 0, drb, 0, 1024, 1024, tk, "grad_w_out", exchange=g_in)
    r_o = _exchange_halves((g_oa, g_ob, g_out))
    s_in, s_o = _add_halves(g_in, (g_oa, g_ob, g_out), r_in, r_o)
    small_part = _pack_small(d_ln_g, d_ln_b, d_bg, d_lnv_g, d_lnv_b, d_ws, d_bs, d_cw, d_cb, loss_part)
    gx, q_in, q_o, gathered = _input_grad_and_scatter(dp, w_full, gx, s_in, s_o, small_part, min(1024, t))
    f_in, f_o = _sum_chips(q_in, q_o, s_in, s_o)
    gsum_in, gsum_o = _share_halves(f_in, f_o)

    big = {}
    grad_x, big["w_in"] = _input_grad_tail_and_adam(dp, w_full, gx, w_in[0], gsum_in, m_w_in[0], v_w_in[0],
                                                    min(32, t // 128))
    for n, (name, w, m, v) in enumerate((("w_oa", w_oa, m_w_oa, v_w_oa), ("w_ob", w_ob, m_w_ob, v_w_ob),
                                         ("w_out", w_out, m_w_out, v_w_out))):
        big[name] = _adam_rows(w[0], gsum_o, m[0], v[0], 256, "adam_" + name, g_sel=n)

    small, g_conv_w, loss_row = _adam_small(gathered, {
        "ln_g": (ln_g, m_ln_g, v_ln_g), "ln_b": (ln_b, m_ln_b, v_ln_b), "b_gate": (b_gate, m_b_gate, v_b_gate),
        "ln_v_g": (ln_v_g, m_ln_v_g, v_ln_v_g), "ln_v_b": (ln_v_b, m_ln_v_b, v_ln_v_b),
        "w_s": (w_s, m_w_s, v_w_s), "b_s": (b_s, m_b_s, v_b_s), "conv_b": (conv_b, m_conv_b, v_conv_b)})
    small["conv_w"] = _adam_conv_w(g_conv_w, chip1, conv_w, m_conv_w, v_conv_w)
    loss = loss_row[0, 0]

    order = ("w_in", "b_gate", "ln_v_g", "ln_v_b", "w_s", "b_s", "conv_w", "conv_b", "w_oa", "w_ob", "w_out",
             "ln_g", "ln_b")
    outs = [loss, grad_x[None]]
    for which in range(4):
        for name in order:
            outs.append(big[name][which][None] if name in big else small[name][which])
    return tuple(outs)
```

```python
import functools
import math

import jax
import jax.numpy as jnp
from jax import lax
from jax.experimental import pallas as pl
from jax.experimental.pallas import tpu as pltpu

F32 = jnp.float32
BF16 = jnp.bfloat16

D_MODEL = 2048
N_HEADS = 8
HEAD_DIM = D_MODEL // N_HEADS
CHUNK = 128
N_SEG = 9
N_CHIPS = 4
SHARD_COLS = N_SEG * D_MODEL // N_CHIPS
COL_BLOCK = 512
BLOCKS_PER_SHARD = SHARD_COLS // COL_BLOCK
BLOCKS_PER_SEG = D_MODEL // COL_BLOCK
SUBS = 3
SHARD_ROWS = D_MODEL // N_CHIPS
N_SLOTS = 12
BLOCK_A, BLOCK_G, BLOCK_B = 0, 2, 2
SLOT_OF_SEG = (0, 1, 2, 8, 9, 10, 11, 4, 5)
DN_ALPHA = 2.0 ** 0.25
LN_EPS = 1e-5
GELU_K = math.sqrt(2.0 / math.pi)
GELU_C = 0.044715

ADAM_LR = 0.001
ADAM_B1 = 0.9
ADAM_B2 = 0.999
ADAM_EPS = 1e-08
ADAM_WD = 0.01
ADAM_STEP = 10
ADAM_C1 = 1.0 / (1.0 - ADAM_B1 ** ADAM_STEP)
ADAM_C2 = 1.0 / (1.0 - ADAM_B2 ** ADAM_STEP)

VMEM_LIMIT = 60 * 1024 * 1024
MESH = pl.DeviceIdType.MESH
ANY = pl.BlockSpec(memory_space=pl.ANY)

SMALL_ROWS = (("ln_g", 16), ("ln_b", 16), ("b_gate", 32), ("ln_v_g", 16), ("ln_v_b", 16),
              ("w_s", 1024), ("b_s", 8), ("conv_w", 48), ("conv_b", 16), ("loss", 8))
SMALL_TOTAL = sum(r for _, r in SMALL_ROWS)


def _params(*sem):
    return pltpu.CompilerParams(dimension_semantics=sem, vmem_limit_bytes=VMEM_LIMIT)


def _sigmoid(x):
    return 1.0 / (1.0 + jnp.exp(-x))


def _gelu_gate(x, x2):
    return 1.0 / (1.0 + jnp.exp(x * ((-2.0 * GELU_K) + (-2.0 * GELU_K * GELU_C) * x2)))


def _gelu_and_grad(x):
    x2 = x * x
    s = _gelu_gate(x, x2)
    g = x * s
    dg = s + g * (1.0 - s) * ((2.0 * GELU_K) + (6.0 * GELU_K * GELU_C) * x2)
    return g, dg


def _silu_grad(sig, sz):
    return sig + sz * (1.0 - sig)


def _dot(a, b, ca=1, cb=0):
    return lax.dot_general(a, b, (((ca,), (cb,)), ((), ())), preferred_element_type=F32)


def _cast_and_transpose(x, tm):
    t, d = x.shape

    def body(x_ref, o_ref, ot_ref):
        v = x_ref[...]
        o_ref[...] = v.astype(BF16)
        ot_ref[...] = v.T.astype(BF16)

    return pl.pallas_call(
        body, name="cast_x", grid=(t // tm,),
        in_specs=[pl.BlockSpec((tm, d), lambda i: (i, 0))],
        out_specs=(pl.BlockSpec((tm, d), lambda i: (i, 0)), pl.BlockSpec((d, tm), lambda i: (0, i))),
        out_shape=(jax.ShapeDtypeStruct((t, d), BF16), jax.ShapeDtypeStruct((d, t), BF16)),
        compiler_params=_params("parallel"),
    )(x)


def _cast_into_columns(w, slot, n_slots, tm, name):
    r, c = w.shape

    def body(s_ref, w_ref, o_ref):
        o_ref[...] = w_ref[...].astype(BF16)

    return pl.pallas_call(
        body, name=name,
        grid_spec=pltpu.PrefetchScalarGridSpec(
            num_scalar_prefetch=1, grid=(r // tm,),
            in_specs=[pl.BlockSpec((tm, c), lambda i, s_ref: (i, 0))],
            out_specs=pl.BlockSpec((tm, c), lambda i, s_ref: (i, s_ref[0]))),
        out_shape=jax.ShapeDtypeStruct((r, n_slots * c), BF16),
        compiler_params=_params("parallel"),
    )(slot, w)


def _prep_small_weights(w_oa, w_ob, w_out, w_s):
    rows = w_oa.shape[0]

    def body(a_ref, b_ref, c_ref, ws_ref, wo_ref, wm_ref):
        wo_ref[0] = a_ref[...].astype(BF16)
        wo_ref[1] = b_ref[...].astype(BF16)
        wo_ref[2] = c_ref[...].astype(BF16)
        t = lax.broadcasted_iota(jnp.int32, (CHUNK, CHUNK), 0)
        s = lax.broadcasted_iota(jnp.int32, (CHUNK, CHUNK), 1)
        for h in range(N_HEADS):
            wm_ref[h] = jnp.where(s <= t, ws_ref[h], 0.0).astype(BF16)

    return pl.pallas_call(
        body, name="prep_small_weights",
        out_shape=(jax.ShapeDtypeStruct((3, rows, D_MODEL), BF16),
                   jax.ShapeDtypeStruct((N_HEADS, CHUNK, CHUNK), BF16)),
        compiler_params=_params(),
    )(w_oa, w_ob, w_out, w_s)


def _mesh_pos():
    x, y, c = lax.axis_index("x"), lax.axis_index("y"), lax.axis_index("c")
    chips = [(1 - x, y), (x, 1 - y), (1 - x, 1 - y)]
    return x, y, c, chips


def _slot_of_seg(seg):
    return jnp.where(seg < 3, seg, jnp.where(seg < 7, seg + 5, seg - 3))


def _gather_and_project(xb, w_pre, wo_b, conv_w8, tm):
    t = xb.shape[0]
    d, sc = w_pre.shape[0], w_pre.shape[1] // N_CHIPS
    rows = wo_b.shape[1]
    hd, hr = d // 2, rows // 2
    nj = BLOCKS_PER_SHARD // SUBS
    pc = nj * COL_BLOCK
    units = N_CHIPS * SUBS
    ni = t // tm
    total = units * ni * nj
    mx, my = lax.axis_index("x"), lax.axis_index("y")
    order = jnp.stack([2 * mx + my, 2 * (1 - mx) + my, 2 * mx + (1 - my),
                       2 * (1 - mx) + (1 - my)]).astype(jnp.int32)

    def body(order_ref, x_ref, wpre_ref, wo_ref, cw_ref, p_ref, wf_ref, wof_ref, cwf_ref,
             wbuf, wsem, xbuf, xsem, send_sems, recv_sems, local_sems):
        x, y, c, chips = _mesh_pos()
        me = 2 * x + y
        sibling = (x, y, 1 - c)
        u, i, j = pl.program_id(0), pl.program_id(1), pl.program_id(2)
        n = (u * ni + i) * nj + j
        m = u * ni + i

        def rows_start(m_):
            pltpu.make_async_copy(x_ref.at[pl.ds(pl.multiple_of(lax.rem(m_, ni) * tm, tm), tm), :],
                                  xbuf.at[lax.rem(m_, 2)], xsem.at[lax.rem(m_, 2)]).start()

        def chip_of(q):
            return 2 * chips[q][0] + chips[q][1]

        def piece(ref, k, cc, r):
            return ref.at[pl.ds(cc * hd, hd), pl.ds(pl.multiple_of(k * sc + r * pc, COL_BLOCK), pc)]

        def wo_half(ref4, k, cc):
            return ref4.at[:, k, pl.ds(cc * hr, hr), :]

        def rcopy(sem, src, dst, to):
            return pltpu.make_async_remote_copy(src_ref=src, dst_ref=dst, send_sem=send_sems.at[sem],
                                                recv_sem=recv_sems.at[sem], device_id=to, device_id_type=MESH)

        def w_send(q, r):
            return rcopy(q * SUBS + r, piece(wpre_ref, me, c, r), piece(wf_ref, me, c, r), (*chips[q], c))

        def w_landed(q, r):
            return rcopy(q * SUBS + r, piece(wpre_ref, me, c, r), piece(wf_ref, chip_of(q), c, r), sibling)

        def w_forward(q, r, cc):
            ref = piece(wf_ref, chip_of(q), cc, r)
            return rcopy(9 + q * SUBS + r, ref, ref, sibling)

        def wo_send(q):
            return rcopy(18 + q, wo_ref.at[:, pl.ds(c * hr, hr), :], wo_half(wof_ref, me, c), (*chips[q], c))

        def wo_landed(q):
            return rcopy(18 + q, wo_ref.at[:, pl.ds(c * hr, hr), :], wo_half(wof_ref, chip_of(q), c), sibling)

        def wo_forward(q, cc):
            ref = wo_half(wof_ref, chip_of(q), cc)
            return rcopy(21 + q, ref, ref, sibling)

        def conv_send(q):
            return rcopy(24 + q, cw_ref, cwf_ref.at[me], (*chips[q], c))

        def local_copies():
            return [pltpu.make_async_copy(wo_ref, wof_ref.at[:, me], local_sems.at[0]),
                    pltpu.make_async_copy(cw_ref, cwf_ref.at[me], local_sems.at[1])]

        def tile_start(u_, j_, slot):
            g = order_ref[u_ // SUBS] * BLOCKS_PER_SHARD + lax.rem(u_, SUBS) * nj + j_
            cols = pl.ds(pl.multiple_of(g * COL_BLOCK, COL_BLOCK), COL_BLOCK)

            @pl.when(u_ < SUBS)
            def _():
                pltpu.make_async_copy(wpre_ref.at[:, cols], wbuf.at[slot], wsem.at[slot]).start()

            @pl.when(u_ >= SUBS)
            def _():
                pltpu.make_async_copy(wf_ref.at[:, cols], wbuf.at[slot], wsem.at[slot]).start()

        def end_of(u_):
            return jnp.logical_and(u == u_, jnp.logical_and(i == ni - 1, j == nj - 1))

        @pl.when(n == 0)
        def _():
            for cp in local_copies():
                cp.start()
            for r in range(SUBS):
                for q in (0, 1):
                    w_send(q, r).start()
            for q in range(3):
                conv_send(q).start()
            tile_start(0, 0, 0)
            rows_start(0)

        def pass_on(q, r):
            w_landed(q, r).wait_recv()
            w_forward(q, r, c).start()

        for u_ in range(1, units - 1):
            @pl.when(end_of(u_))
            def _(u_=u_):
                if u_ <= SUBS:
                    for q in (0, 1):
                        pass_on(q, u_ - 1)
                if u_ == SUBS:
                    for r in range(SUBS):
                        for q in (0, 1):
                            w_send(q, r).wait_send()
                    for r in range(SUBS):
                        w_send(2, r).start()
                if 2 * SUBS - 1 <= u_ <= 3 * SUBS - 2:
                    pass_on(2, u_ - (2 * SUBS - 1))
                if u_ == 3 * SUBS - 2:
                    for r in range(SUBS):
                        w_send(2, r).wait_send()
                    for q in range(3):
                        wo_send(q).start()
                nxt = u_ + 1
                if nxt >= SUBS:
                    w_forward(nxt // SUBS - 1, nxt % SUBS, 1 - c).wait_recv()

        n1 = n + 1

        @pl.when(n1 < total)
        def _():
            tile_start(n1 // (ni * nj), lax.rem(n1, nj), lax.rem(n1, 2))

        xslot = lax.rem(m, 2)

        @pl.when(j == 0)
        def _():
            @pl.when(m + 1 < units * ni)
            def _():
                rows_start(m + 1)

            pltpu.make_async_copy(x_ref.at[pl.ds(0, tm), :], xbuf.at[xslot], xsem.at[xslot]).wait()

        slot = lax.rem(n, 2)
        pltpu.make_async_copy(wpre_ref.at[:, pl.ds(0, COL_BLOCK)], wbuf.at[slot], wsem.at[slot]).wait()
        p_ref[...] = _dot(xbuf[xslot], wbuf[slot]).astype(BF16)

        @pl.when(n == total - 1)
        def _():
            for q in range(3):
                wo_landed(q).wait_recv()
                wo_forward(q, c).start()
            for q in range(3):
                wo_forward(q, 1 - c).wait_recv()
                rcopy(24 + q, cw_ref, cwf_ref.at[chip_of(q)], sibling).wait_recv()
            for q in range(3):
                for r in range(SUBS):
                    w_forward(q, r, c).wait_send()
                wo_send(q).wait_send()
                wo_forward(q, c).wait_send()
                conv_send(q).wait_send()
            for cp in local_copies():
                cp.wait()

    def p_map(u, i, j, o):
        g = o[u // SUBS] * BLOCKS_PER_SHARD + lax.rem(u, SUBS) * nj + j
        return (_slot_of_seg(g // BLOCKS_PER_SEG), i, lax.rem(g, BLOCKS_PER_SEG))

    return pl.pallas_call(
        body, name="gather_and_project",
        grid_spec=pltpu.PrefetchScalarGridSpec(
            num_scalar_prefetch=1, grid=(units, ni, nj),
            in_specs=[ANY, ANY, ANY, ANY],
            out_specs=(pl.BlockSpec((None, tm, COL_BLOCK), p_map), ANY, ANY, ANY),
            scratch_shapes=[pltpu.VMEM((2, d, COL_BLOCK), BF16), pltpu.SemaphoreType.DMA((2,)),
                            pltpu.VMEM((2, tm, D_MODEL), BF16), pltpu.SemaphoreType.DMA((2,)),
                            pltpu.SemaphoreType.DMA((27,)), pltpu.SemaphoreType.DMA((27,)),
                            pltpu.SemaphoreType.DMA((2,))]),
        out_shape=(jax.ShapeDtypeStruct((N_SLOTS, t, D_MODEL), BF16),
                   jax.ShapeDtypeStruct((d, N_CHIPS * sc), BF16),
                   jax.ShapeDtypeStruct((3, N_CHIPS, rows, D_MODEL), BF16),
                   jax.ShapeDtypeStruct((N_CHIPS,) + conv_w8.shape, F32)),
        input_output_aliases={2: 1},
        compiler_params=pltpu.CompilerParams(dimension_semantics=("arbitrary",) * 3, vmem_limit_bytes=VMEM_LIMIT,
                                             has_side_effects=True),
    )(order, xb, w_pre, wo_b, conv_w8)


def _exchange_halves(g_o):
    hr = SHARD_ROWS // 2
    g_o4 = [g.reshape(N_CHIPS, 2, hr, D_MODEL) for g in g_o]

    def body(ga_ref, gb_ref, gc_ref, ra_ref, rb_ref, rc_ref, send_sems, recv_sems):
        x, y, c, _ = _mesh_pos()
        sibling = (x, y, 1 - c)
        cps = []
        for n, (g_ref, r_ref) in enumerate(((ga_ref, ra_ref), (gb_ref, rb_ref), (gc_ref, rc_ref))):
            cps.append(pltpu.make_async_remote_copy(src_ref=g_ref.at[:, 1 - c], dst_ref=r_ref,
                                                    send_sem=send_sems.at[n], recv_sem=recv_sems.at[n],
                                                    device_id=sibling, device_id_type=MESH))
        for cp in cps:
            cp.start()
        for cp in cps:
            cp.wait()

    o_shape = jax.ShapeDtypeStruct((N_CHIPS, hr, D_MODEL), BF16)
    return pl.pallas_call(
        body, name="rs_exchange_halves",
        in_specs=[ANY] * 3, out_specs=(ANY,) * 3,
        out_shape=(o_shape, o_shape, o_shape),
        scratch_shapes=[pltpu.SemaphoreType.DMA((3,)), pltpu.SemaphoreType.DMA((3,))],
        compiler_params=pltpu.CompilerParams(has_side_effects=True),
    )(*g_o4)


def _add_halves(g_in, g_o, r_in, r_o):
    d, c9 = g_in.shape
    hd = d // 2
    hr = SHARD_ROWS // 2
    core = lax.axis_index("c").astype(jnp.int32).reshape(1)
    tm = min(512, hd)
    nb = hd // tm

    def body_in(c_ref, g_ref, r_ref, o_ref):
        o_ref[...] = (g_ref[...].astype(F32) + r_ref[...].astype(F32)).astype(BF16)

    s_in = pl.pallas_call(
        body_in, name="rs_add_halves_in",
        grid_spec=pltpu.PrefetchScalarGridSpec(
            num_scalar_prefetch=1, grid=(N_CHIPS, nb),
            in_specs=[pl.BlockSpec((tm, SHARD_COLS), lambda k, i, c_ref: (c_ref[0] * nb + i, k)),
                      pl.BlockSpec((tm, SHARD_COLS), lambda k, i, c_ref: (i, k))],
            out_specs=pl.BlockSpec((None, tm, SHARD_COLS), lambda k, i, c_ref: (k, i, 0))),
        out_shape=jax.ShapeDtypeStruct((N_CHIPS, hd, SHARD_COLS), BF16),
        compiler_params=_params("parallel", "parallel"),
    )(core, g_in, r_in)

    g_o4 = [g.reshape(N_CHIPS, 2, hr, D_MODEL) for g in g_o]

    def body_o(c_ref, ga_ref, gb_ref, gc_ref, ra_ref, rb_ref, rc_ref, o_ref):
        for n, (g_ref, r_ref) in enumerate(((ga_ref, ra_ref), (gb_ref, rb_ref), (gc_ref, rc_ref))):
            o_ref[n] = (g_ref[...].astype(F32) + r_ref[...].astype(F32)).astype(BF16)

    gspec = pl.BlockSpec((None, None, hr, D_MODEL), lambda k, c_ref: (k, c_ref[0], 0, 0))
    rspec = pl.BlockSpec((None, hr, D_MODEL), lambda k, c_ref: (k, 0, 0))
    s_o = pl.pallas_call(
        body_o, name="rs_add_halves_o",
        grid_spec=pltpu.PrefetchScalarGridSpec(
            num_scalar_prefetch=1, grid=(N_CHIPS,),
            in_specs=[gspec] * 3 + [rspec] * 3,
            out_specs=pl.BlockSpec((3, None, hr, D_MODEL), lambda k, c_ref: (0, k, 0, 0))),
        out_shape=jax.ShapeDtypeStruct((3, N_CHIPS, hr, D_MODEL), BF16),
        compiler_params=_params("parallel"),
    )(core, *g_o4, *r_o)
    return s_in, s_o


def _sum_chips(r_in, r_o, s_in, s_o):
    _, hd, sc = r_in.shape
    hr = r_o.shape[2]
    tm = min(256, hd)
    nb = hd // tm
    pos = jnp.stack([2 * lax.axis_index("x") + lax.axis_index("y"), lax.axis_index("c")]).astype(jnp.int32)

    def chip_sum(pos_ref, r_ref, s_ref):
        acc = None
        for k in range(N_CHIPS):
            term = jnp.where(pos_ref[0] == k, s_ref[...], r_ref[k]).astype(F32)
            acc = term if acc is None else acc + term
        return acc

    def body_in(pos_ref, r_ref, s_ref, o_ref):
        o_ref[...] = chip_sum(pos_ref, r_ref, s_ref)

    f_in = pl.pallas_call(
        body_in, name="rs_sum_chips_in",
        grid_spec=pltpu.PrefetchScalarGridSpec(
            num_scalar_prefetch=1, grid=(nb,),
            in_specs=[pl.BlockSpec((N_CHIPS, tm, sc), lambda i, p: (0, i, 0)),
                      pl.BlockSpec((None, tm, sc), lambda i, p: (p[0], i, 0))],
            out_specs=pl.BlockSpec((tm, sc), lambda i, p: (p[1] * nb + i, 0))),
        out_shape=jax.ShapeDtypeStruct((2 * hd, sc), F32),
        compiler_params=_params("parallel"),
    )(pos, r_in, s_in)

    def body_o(pos_ref, r_ref, s_ref, o_ref):
        o_ref[...] = chip_sum(pos_ref, r_ref, s_ref)

    f_o = pl.pallas_call(
        body_o, name="rs_sum_chips_o",
        grid_spec=pltpu.PrefetchScalarGridSpec(
            num_scalar_prefetch=1, grid=(3,),
            in_specs=[pl.BlockSpec((N_CHIPS, None, hr, D_MODEL), lambda n, p: (0, n, 0, 0)),
                      pl.BlockSpec((None, None, hr, D_MODEL), lambda n, p: (n, p[0], 0, 0))],
            out_specs=pl.BlockSpec((None, hr, D_MODEL), lambda n, p: (n, p[1], 0))),
        out_shape=jax.ShapeDtypeStruct((3, 2 * hr, D_MODEL), F32),
        compiler_params=_params("parallel"),
    )(pos, r_o, s_o)
    return f_in, f_o


def _share_halves(f_in, f_o):
    hd, sc = f_in.shape[0] // 2, f_in.shape[1]
    hr = f_o.shape[1] // 2

    def body(fi_ref, fo_ref, gi_ref, go_ref, send_sems, recv_sems):
        x, y, c, _ = _mesh_pos()
        sibling = (x, y, 1 - c)

        def halves(cc):
            rows_i, rows_o = pl.ds(cc * hd, hd), pl.ds(cc * hr, hr)
            return (fi_ref.at[rows_i, :], gi_ref.at[rows_i, :]), (fo_ref.at[:, rows_o, :], go_ref.at[:, rows_o, :])

        def copies(cc):
            return [pltpu.make_async_remote_copy(src_ref=src, dst_ref=dst, send_sem=send_sems.at[n],
                                                 recv_sem=recv_sems.at[n], device_id=sibling, device_id_type=MESH)
                    for n, (src, dst) in enumerate(halves(cc))]

        sends = copies(c)
        for cp in sends:
            cp.start()
        for cp in copies(1 - c):
            cp.wait_recv()
        for cp in sends:
            cp.wait_send()

    return pl.pallas_call(
        body, name="rs_share_halves",
        in_specs=[ANY, ANY], out_specs=(ANY, ANY),
        out_shape=(jax.ShapeDtypeStruct(f_in.shape, F32), jax.ShapeDtypeStruct(f_o.shape, F32)),
        scratch_shapes=[pltpu.SemaphoreType.DMA((2,)), pltpu.SemaphoreType.DMA((2,))],
        input_output_aliases={0: 0, 1: 1},
        compiler_params=pltpu.CompilerParams(has_side_effects=True),
    )(f_in, f_o)


def _merge_forward(ya, yb, wo3, p, bg, tm, tn):
    t = ya.shape[0]

    def body(ya_ref, yb_ref, wa_ref, wb_ref, g_ref, bg_ref, m_ref, oab_ref):
        oa = _dot(ya_ref[...], wa_ref[...])
        ob = _dot(yb_ref[...], wb_ref[...])
        ga = _sigmoid(g_ref[0].astype(F32) + bg_ref[0])
        gb = _sigmoid(g_ref[1].astype(F32) + bg_ref[1])
        m_ref[...] = (ga * oa + gb * ob).astype(BF16)
        oab_ref[0] = oa.astype(BF16)
        oab_ref[1] = ob.astype(BF16)

    return pl.pallas_call(
        body, name="merge_forward", grid=(t // tm, D_MODEL // tn),
        in_specs=[pl.BlockSpec((tm, D_MODEL), lambda i, j: (i, 0)),
                  pl.BlockSpec((tm, D_MODEL), lambda i, j: (i, 0)),
                  pl.BlockSpec((None, D_MODEL, tn), lambda i, j: (0, 0, j)),
                  pl.BlockSpec((None, D_MODEL, tn), lambda i, j: (1, 0, j)),
                  pl.BlockSpec((2, tm, tn), lambda i, j: (BLOCK_G, i, j)),
                  pl.BlockSpec((2, 1, tn), lambda i, j: (0, 0, j))],
        out_specs=(pl.BlockSpec((tm, tn), lambda i, j: (i, j)),
                   pl.BlockSpec((2, tm, tn), lambda i, j: (0, i, j))),
        out_shape=(jax.ShapeDtypeStruct((t, D_MODEL), BF16), jax.ShapeDtypeStruct((2, t, D_MODEL), BF16)),
        compiler_params=_params("parallel", "parallel"),
    )(ya, yb, wo3, wo3, p, bg)


HEAD_ROWS = 256


def _head(merged, wo3, x, target, ln_g, ln_b, tm):
    t = x.shape[0]
    inv_d = 1.0 / D_MODEL

    def body(m_ref, w_ref, x_ref, t_ref, g_ref, b_ref, dr_ref, gx_ref, dm_ref, dg_ref, db_ref, loss_ref):
        i = pl.program_id(0)

        @pl.when(i == 0)
        def _():
            dg_ref[...] = jnp.zeros_like(dg_ref)
            db_ref[...] = jnp.zeros_like(db_ref)
            loss_ref[...] = jnp.zeros_like(loss_ref)

        w = w_ref[...]
        g = g_ref[...]
        tiles = [slice(r0, r0 + HEAD_ROWS) for r0 in range(0, tm, HEAD_ROWS)]
        firsts = [_dot(m_ref[rows, :], w) for rows in tiles]
        for rows, out in zip(tiles, firsts):
            r = DN_ALPHA * x_ref[rows, :] + out
            mu = jnp.mean(r, axis=-1, keepdims=True)
            xc = r - mu
            var = jnp.mean(xc * xc, axis=-1, keepdims=True)
            rstd = lax.rsqrt(var + LN_EPS)
            xhat = xc * rstd
            e = xhat * g + b_ref[...] - t_ref[rows, :]
            se = jnp.sum(jnp.sum(e * e, axis=1, keepdims=True), axis=0, keepdims=True)
            loss_ref[...] += jnp.broadcast_to((0.5 * inv_d) * se, loss_ref.shape)
            dy = e * inv_d
            db_ref[...] += jnp.sum(dy, axis=0, keepdims=True)
            dg_ref[...] += jnp.sum(dy * xhat, axis=0, keepdims=True)
            dxh = dy * g
            m1 = jnp.mean(dxh, axis=-1, keepdims=True)
            m2 = jnp.mean(dxh * xhat, axis=-1, keepdims=True)
            dr = rstd * (dxh - m1 - xhat * m2)
            gx_ref[rows, :] = DN_ALPHA * dr
            drb = dr.astype(BF16)
            dr_ref[rows, :] = drb
            dm_ref[rows, :] = _dot(drb, w, 1, 1).astype(BF16)

    row = pl.BlockSpec((tm, D_MODEL), lambda i: (i, 0))
    vec = pl.BlockSpec((1, D_MODEL), lambda i: (0, 0))
    return pl.pallas_call(
        body, name="head", grid=(t // tm,),
        in_specs=[row, pl.BlockSpec((None, D_MODEL, D_MODEL), lambda i: (2, 0, 0), pipeline_mode=pl.Buffered(1)),
                  row, row, vec, vec],
        out_specs=(row, row, row, vec, vec, pl.BlockSpec((1, 128), lambda i: (0, 0))),
        out_shape=(jax.ShapeDtypeStruct((t, D_MODEL), BF16), jax.ShapeDtypeStruct((t, D_MODEL), F32),
                   jax.ShapeDtypeStruct((t, D_MODEL), BF16), jax.ShapeDtypeStruct((1, D_MODEL), F32),
                   jax.ShapeDtypeStruct((1, D_MODEL), F32), jax.ShapeDtypeStruct((1, 128), F32)),
        compiler_params=_params("arbitrary"),
    )(merged, wo3, x, target, ln_g, ln_b)


def _gate_and_branch_backward(dmerged, oab, p, bg, wo3, tm):
    t = dmerged.shape[0]

    def body(dm_ref, oab_ref, g_ref, bg_ref, wa_ref, wb_ref, do_ref, dpg_ref, dbg_ref, dy_ref):
        @pl.when(pl.program_id(0) == 0)
        def _():
            dbg_ref[...] = jnp.zeros_like(dbg_ref)

        dm = dm_ref[...].astype(F32)
        for n, w_ref in enumerate((wa_ref, wb_ref)):
            gate = _sigmoid(g_ref[n].astype(F32) + bg_ref[n])
            d_o = (dm * gate).astype(BF16)
            do_ref[n] = d_o
            dgate = dm * oab_ref[n].astype(F32) * gate * (1.0 - gate)
            dpg_ref[n] = dgate.astype(BF16)
            dbg_ref[n] += jnp.sum(dgate, axis=0, keepdims=True)
            dy_ref[n] = _dot(d_o, w_ref[...], 1, 1).astype(BF16)

    pair = pl.BlockSpec((2, tm, D_MODEL), lambda i: (0, i, 0))
    gates = pl.BlockSpec((2, tm, D_MODEL), lambda i: (BLOCK_G, i, 0))
    vec = pl.BlockSpec((2, 1, D_MODEL), lambda i: (0, 0, 0))

    def weight(n):
        return pl.BlockSpec((None, D_MODEL, D_MODEL), lambda i: (n, 0, 0), pipeline_mode=pl.Buffered(1))

    pair_shape = jax.ShapeDtypeStruct((2, t, D_MODEL), BF16)
    return pl.pallas_call(
        body, name="gate_and_branch_backward", grid=(t // tm,),
        in_specs=[pl.BlockSpec((tm, D_MODEL), lambda i: (i, 0)), pair, gates, vec, weight(0), weight(1)],
        out_specs=(pair, gates, vec, pair),
        out_shape=(pair_shape, jax.ShapeDtypeStruct((N_SLOTS, t, D_MODEL), BF16),
                   jax.ShapeDtypeStruct((2, 1, D_MODEL), F32), pair_shape),
        compiler_params=_params("arbitrary"),
    )(dmerged, oab, p, bg, wo3, wo3)


def _weight_grad(a, a_sel, b, b_sel, tm, tn, tk, name, exchange=None):
    t = a.shape[-2]
    nk = t // tk
    ni, nj = D_MODEL // tm, D_MODEL // tn

    def body(a_ref, b_ref, *rest):
        if exchange is None:
            o_ref, acc_ref = rest
        else:
            g_ref, o_ref, r_ref, acc_ref, send_sem, recv_sem = rest
        i, j, k = pl.program_id(0), pl.program_id(1), pl.program_id(2)

        if exchange is not None:
            x, y, c, _ = _mesh_pos()
            hd = exchange.shape[0] // 2
            swap = pltpu.make_async_remote_copy(src_ref=g_ref.at[pl.ds((1 - c) * hd, hd), :], dst_ref=r_ref,
                                                send_sem=send_sem, recv_sem=recv_sem,
                                                device_id=(x, y, 1 - c), device_id_type=MESH)

            @pl.when(jnp.logical_and(i == 0, jnp.logical_and(j == 0, k == 0)))
            def _():
                swap.start()

        @pl.when(k == 0)
        def _():
            acc_ref[...] = jnp.zeros_like(acc_ref)

        acc_ref[...] += _dot(a_ref[...], b_ref[...], 0, 0)

        @pl.when(k == nk - 1)
        def _():
            o_ref[...] = acc_ref[...].astype(BF16)

        if exchange is not None:
            @pl.when(jnp.logical_and(i == ni - 1, jnp.logical_and(j == nj - 1, k == nk - 1)))
            def _():
                swap.wait()

    def spec(arr, sel, width, which):
        if arr.ndim == 2:
            return pl.BlockSpec((tk, width), lambda i, j, k: (k, (i, j)[which]))
        return pl.BlockSpec((None, tk, width), lambda i, j, k: (sel, k, (i, j)[which]))

    o_spec = pl.BlockSpec((tm, tn), lambda i, j, k: (i, j))
    o_shape = jax.ShapeDtypeStruct((D_MODEL, D_MODEL), BF16)
    if exchange is None:
        return pl.pallas_call(
            body, name=name, grid=(ni, nj, nk),
            in_specs=[spec(a, a_sel, tm, 0), spec(b, b_sel, tn, 1)],
            out_specs=o_spec, out_shape=o_shape,
            scratch_shapes=[pltpu.VMEM((tm, tn), F32)],
            compiler_params=_params("parallel", "parallel", "arbitrary"),
        )(a, b)
    return pl.pallas_call(
        body, name=name, grid=(ni, nj, nk),
        in_specs=[spec(a, a_sel, tm, 0), spec(b, b_sel, tn, 1), ANY],
        out_specs=(o_spec, ANY),
        out_shape=(o_shape, jax.ShapeDtypeStruct((exchange.shape[0] // 2, exchange.shape[1]), BF16)),
        scratch_shapes=[pltpu.VMEM((tm, tn), F32), pltpu.SemaphoreType.DMA, pltpu.SemaphoreType.DMA],
        compiler_params=pltpu.CompilerParams(dimension_semantics=("arbitrary",) * 3, vmem_limit_bytes=VMEM_LIMIT,
                                             has_side_effects=True),
    )(a, b, exchange)


def _win_grad(xt, dp, tn, tk):
    _, t, _ = dp.shape
    nk = t // tk
    per_seg = D_MODEL // tn
    nj = N_SEG * per_seg

    def body(x_ref, dp_ref, o_ref, acc_ref):
        k = pl.program_id(1)

        @pl.when(k == 0)
        def _():
            acc_ref[...] = jnp.zeros_like(acc_ref)

        acc_ref[...] += _dot(x_ref[...], dp_ref[...])

        @pl.when(k == nk - 1)
        def _():
            o_ref[...] = acc_ref[...].astype(BF16)

    return pl.pallas_call(
        body, name="grad_w_in", grid=(nj, nk),
        in_specs=[pl.BlockSpec((D_MODEL, tk), lambda j, k: (0, k)),
                  pl.BlockSpec((None, tk, tn), lambda j, k: (_slot_of_seg(j // per_seg), k, j % per_seg))],
        out_specs=pl.BlockSpec((D_MODEL, tn), lambda j, k: (0, j)),
        out_shape=jax.ShapeDtypeStruct((D_MODEL, N_SEG * D_MODEL), BF16),
        scratch_shapes=[pltpu.VMEM((D_MODEL, tn), F32)],
        compiler_params=_params("parallel", "arbitrary"),
    )(xt, dp)


def _input_grad_and_scatter(dp, w_full, gx, s_in, s_o, small, tm):
    _, t, _ = dp.shape
    ni, nk = t // tm, N_SEG - TAIL_SEGS
    _, hd, sc = s_in.shape
    hr = s_o.shape[2]

    def body(dp_ref, w_ref, gx_ref, si_ref, so_ref, sm_ref, o_ref, ri_ref, ro_ref, ga_ref,
             send_sems, recv_sems, local_sem):
        i, k = pl.program_id(0), pl.program_id(1)
        x, y, c, chips = _mesh_pos()
        me = 2 * x + y
        dev = 4 * x + 2 * y + c

        def peer(r):
            return (x ^ ((r >> 2) & 1), y ^ ((r >> 1) & 1), c ^ (r & 1))

        def sends():
            cps = []
            for q, (cx, cy) in enumerate(chips):
                dest = 2 * cx + cy
                cps.append(pltpu.make_async_remote_copy(src_ref=si_ref.at[dest], dst_ref=ri_ref.at[me],
                                                        send_sem=send_sems.at[q], recv_sem=recv_sems.at[q],
                                                        device_id=(cx, cy, c), device_id_type=MESH))
                cps.append(pltpu.make_async_remote_copy(src_ref=so_ref.at[:, dest], dst_ref=ro_ref.at[me],
                                                        send_sem=send_sems.at[3 + q], recv_sem=recv_sems.at[3 + q],
                                                        device_id=(cx, cy, c), device_id_type=MESH))
            for r in range(1, 8):
                cps.append(pltpu.make_async_remote_copy(src_ref=sm_ref, dst_ref=ga_ref.at[dev],
                                                        send_sem=send_sems.at[5 + r], recv_sem=recv_sems.at[5 + r],
                                                        device_id=peer(r), device_id_type=MESH))
            return cps

        own_small = pltpu.make_async_copy(sm_ref, ga_ref.at[dev], local_sem)

        @pl.when(jnp.logical_and(i == 0, k == 0))
        def _():
            for cp in sends():
                cp.start()
            own_small.start()

        @pl.when(k == 0)
        def _():
            o_ref[...] = gx_ref[...]

        o_ref[...] += _dot(dp_ref[...], w_ref[...], 1, 1)

        @pl.when(jnp.logical_and(i == ni - 1, k == nk - 1))
        def _():
            for q, (cx, cy) in enumerate(chips):
                frm = 2 * cx + cy
                pltpu.make_async_remote_copy(src_ref=si_ref.at[frm], dst_ref=ri_ref.at[frm], send_sem=send_sems.at[q],
                                             recv_sem=recv_sems.at[q], device_id=(x, y, c),
                                             device_id_type=MESH).wait_recv()
                pltpu.make_async_remote_copy(src_ref=so_ref.at[:, frm], dst_ref=ro_ref.at[frm],
                                             send_sem=send_sems.at[3 + q], recv_sem=recv_sems.at[3 + q],
                                             device_id=(x, y, c), device_id_type=MESH).wait_recv()
            for r in range(1, 8):
                px, py, pc = peer(r)
                pltpu.make_async_remote_copy(src_ref=sm_ref, dst_ref=ga_ref.at[4 * px + 2 * py + pc],
                                             send_sem=send_sems.at[5 + r], recv_sem=recv_sems.at[5 + r],
                                             device_id=(x, y, c), device_id_type=MESH).wait_recv()
            for cp in sends():
                cp.wait_send()
            own_small.wait()

    return pl.pallas_call(
        body, name="grad_x_and_scatter", grid=(ni, nk),
        in_specs=[pl.BlockSpec((None, tm, D_MODEL), lambda i, k: (_slot_of_seg(k), i, 0)),
                  pl.BlockSpec((D_MODEL, D_MODEL), lambda i, k: (0, k)),
                  pl.BlockSpec((tm, D_MODEL), lambda i, k: (i, 0), pipeline_mode=pl.Buffered(1)), ANY, ANY, ANY],
        out_specs=(pl.BlockSpec((tm, D_MODEL), lambda i, k: (i, 0)), ANY, ANY, ANY),
        out_shape=(jax.ShapeDtypeStruct((t, D_MODEL), F32),
                   jax.ShapeDtypeStruct((N_CHIPS, hd, sc), BF16),
                   jax.ShapeDtypeStruct((N_CHIPS, 3, hr, D_MODEL), BF16),
                   jax.ShapeDtypeStruct((8,) + small.shape, small.dtype)),
        scratch_shapes=[pltpu.SemaphoreType.DMA((13,)), pltpu.SemaphoreType.DMA((13,)), pltpu.SemaphoreType.DMA],
        input_output_aliases={2: 0},
        compiler_params=pltpu.CompilerParams(dimension_semantics=("arbitrary", "arbitrary"),
                                             vmem_limit_bytes=VMEM_LIMIT, has_side_effects=True),
    )(dp, w_full, gx, s_in, s_o, small)


TAIL_SEGS = 1


def _input_grad_tail_and_adam(dp, w_full, partial, w, g, m, v, steps):
    _, t, _ = dp.shape
    tm = t // steps
    r, c = w.shape
    ra = r // steps
    segs = tuple(range(N_SEG - TAIL_SEGS, N_SEG))

    def body(*refs):
        dp_refs = refs[:TAIL_SEGS]
        w_refs = refs[TAIL_SEGS:2 * TAIL_SEGS]
        part_ref, aw_ref, ag_ref, am_ref, av_ref, o_ref, go_ref, d_ref, mo_ref, vo_ref = refs[2 * TAIL_SEGS:]
        acc = part_ref[...]
        for dp_ref, w_ref in zip(dp_refs, w_refs):
            acc = acc + _dot(dp_ref[...], w_ref[...], 1, 1)
        o_ref[...] = acc
        gv = ag_ref[...]
        d, mn, vn = _adam_math(aw_ref[...], gv, am_ref[...], av_ref[...])
        go_ref[...] = gv
        d_ref[...] = d
        mo_ref[...] = mn
        vo_ref[...] = vn

    row = pl.BlockSpec((tm, D_MODEL), lambda i: (i, 0))
    arow = pl.BlockSpec((ra, c), lambda i: (i, 0))
    ashape = jax.ShapeDtypeStruct((r, c), F32)
    in_specs = [pl.BlockSpec((None, tm, D_MODEL), functools.partial(lambda s, i: (SLOT_OF_SEG[s], i, 0), s))
                for s in segs]
    in_specs += [pl.BlockSpec((D_MODEL, D_MODEL), functools.partial(lambda s, i: (0, s), s),
                              pipeline_mode=pl.Buffered(1)) for s in segs]
    in_specs += [row, arow, arow, arow, arow]
    res = pl.pallas_call(
        body, name="grad_x_tail_and_adam_w_in", grid=(steps,),
        in_specs=in_specs,
        out_specs=(row, arow, arow, arow, arow),
        out_shape=(jax.ShapeDtypeStruct((t, D_MODEL), F32), ashape, ashape, ashape, ashape),
        compiler_params=_params("parallel"),
    )(*([dp] * TAIL_SEGS), *([w_full] * TAIL_SEGS), partial, w, g, m, v)
    return res[0], res[1:]


def _sgu_chunk_forward(u, v, z, wm, bs, lng, lnb):
    ug, dug = _gelu_and_grad(u)
    vg, dvg = _gelu_and_grad(v)
    mu = jnp.mean(vg, axis=-1, keepdims=True)
    xc = vg - mu
    var = jnp.mean(xc * xc, axis=-1, keepdims=True)
    rstd = lax.rsqrt(var + LN_EPS)
    vhat = xc * rstd
    vln = (vhat * lng + lnb).astype(BF16)
    mixed = _dot(wm, vln) + bs
    sig = _sigmoid(z)
    return ug, dug, dvg, rstd, vhat, vln, mixed, sig


def _mixer_a_forward(p_a, wm, bs_col, ln_v_g, ln_v_b, tm):
    t = p_a.shape[1]

    def body(p_ref, wm_ref, bs_ref, g_ref, b_ref, o_ref):
        wm_v, bs_v, lng, lnb = wm_ref[...], bs_ref[...], g_ref[...], b_ref[...]

        def chunk(ci, carry):
            rows = pl.ds(pl.multiple_of(ci * CHUNK, CHUNK), CHUNK)
            u = p_ref[0, rows, :].astype(F32)
            v = p_ref[1, rows, :].astype(F32)
            z = p_ref[2, rows, :].astype(F32)
            ug, _, _, _, _, _, mixed, sig = _sgu_chunk_forward(u, v, z, wm_v, bs_v, lng, lnb)
            o_ref[rows, :] = (ug * mixed * (z * sig)).astype(BF16)
            return carry

        lax.fori_loop(0, tm // CHUNK, chunk, 0, unroll=True)

    return pl.pallas_call(
        body, name="mixer_a_forward", grid=(t // tm, N_HEADS),
        in_specs=[pl.BlockSpec((3, tm, HEAD_DIM), lambda i, h: (0, i, h)),
                  pl.BlockSpec((None, CHUNK, CHUNK), lambda i, h: (h, 0, 0)),
                  pl.BlockSpec((None, CHUNK, 1), lambda i, h: (h, 0, 0)),
                  pl.BlockSpec((1, HEAD_DIM), lambda i, h: (0, h)),
                  pl.BlockSpec((1, HEAD_DIM), lambda i, h: (0, h))],
        out_specs=pl.BlockSpec((tm, HEAD_DIM), lambda i, h: (i, h)),
        out_shape=jax.ShapeDtypeStruct((t, D_MODEL), BF16),
        compiler_params=_params("parallel", "parallel"),
    )(p_a, wm, bs_col, ln_v_g, ln_v_b)


def _mixer_a_backward(p_a, dyab, wm, bs_col, ln_v_g, ln_v_b, dp, tm):
    t = p_a.shape[1]

    def body(p_ref, dy_ref, wm_ref, bs_ref, g_ref, b_ref, dp_in, dp_ref, dws_ref, dbs_ref, dg_ref, db_ref):
        @pl.when(pl.program_id(1) == 0)
        def _():
            dws_ref[...] = jnp.zeros_like(dws_ref)
            dbs_ref[...] = jnp.zeros_like(dbs_ref)
            dg_ref[...] = jnp.zeros_like(dg_ref)
            db_ref[...] = jnp.zeros_like(db_ref)

        wm_v, bs_v, lng, lnb = wm_ref[...], bs_ref[...], g_ref[...], b_ref[...]
        causal = (lax.broadcasted_iota(jnp.int32, (CHUNK, CHUNK), 1)
                  <= lax.broadcasted_iota(jnp.int32, (CHUNK, CHUNK), 0))

        def chunk(ci, carry):
            rows = pl.ds(pl.multiple_of(ci * CHUNK, CHUNK), CHUNK)
            u = p_ref[0, rows, :].astype(F32)
            v = p_ref[1, rows, :].astype(F32)
            z = p_ref[2, rows, :].astype(F32)
            dy = dy_ref[rows, :].astype(F32)
            ug, dug, dvg, rstd, vhat, vln, mixed, sig = _sgu_chunk_forward(u, v, z, wm_v, bs_v, lng, lnb)
            sz = z * sig
            dmixed = dy * ug * sz
            dp_ref[0, rows, :] = (dy * mixed * sz * dug).astype(BF16)
            dp_ref[2, rows, :] = (dy * ug * mixed * _silu_grad(sig, sz)).astype(BF16)
            dbs_ref[...] += jnp.sum(dmixed, axis=1, keepdims=True)
            dmb = dmixed.astype(BF16)
            dws_ref[...] += jnp.where(causal, _dot(dmb, vln, 1, 1), 0.0)
            dvln = _dot(wm_v, dmb, 0, 0)
            db_ref[...] += jnp.sum(dvln, axis=0, keepdims=True)
            dg_ref[...] += jnp.sum(dvln * vhat, axis=0, keepdims=True)
            dvh = dvln * lng
            m1 = jnp.mean(dvh, axis=-1, keepdims=True)
            m2 = jnp.mean(dvh * vhat, axis=-1, keepdims=True)
            dp_ref[1, rows, :] = (rstd * (dvh - m1 - vhat * m2) * dvg).astype(BF16)
            return carry

        lax.fori_loop(0, tm // CHUNK, chunk, 0, unroll=True)

    return pl.pallas_call(
        body, name="mixer_a_backward", grid=(N_HEADS, t // tm),
        in_specs=[pl.BlockSpec((3, tm, HEAD_DIM), lambda h, i: (0, i, h)),
                  pl.BlockSpec((None, tm, HEAD_DIM), lambda h, i: (0, i, h)),
                  pl.BlockSpec((None, CHUNK, CHUNK), lambda h, i: (h, 0, 0)),
                  pl.BlockSpec((None, CHUNK, 1), lambda h, i: (h, 0, 0)),
                  pl.BlockSpec((1, HEAD_DIM), lambda h, i: (0, h)),
                  pl.BlockSpec((1, HEAD_DIM), lambda h, i: (0, h)), ANY],
        out_specs=(pl.BlockSpec((3, tm, HEAD_DIM), lambda h, i: (BLOCK_A, i, h)),
                   pl.BlockSpec((None, CHUNK, CHUNK), lambda h, i: (h, 0, 0)),
                   pl.BlockSpec((None, CHUNK, 1), lambda h, i: (h, 0, 0)),
                   pl.BlockSpec((1, HEAD_DIM), lambda h, i: (0, h)),
                   pl.BlockSpec((1, HEAD_DIM), lambda h, i: (0, h))),
        out_shape=(jax.ShapeDtypeStruct(dp.shape, BF16),
                   jax.ShapeDtypeStruct((N_HEADS, CHUNK, CHUNK), F32),
                   jax.ShapeDtypeStruct((N_HEADS, CHUNK, 1), F32),
                   jax.ShapeDtypeStruct((1, D_MODEL), F32), jax.ShapeDtypeStruct((1, D_MODEL), F32)),
        input_output_aliases={6: 0},
        compiler_params=_params("parallel", "arbitrary"),
    )(p_a, dyab, wm, bs_col, ln_v_g, ln_v_b, dp)


HALO = 16


def _conv_taps(h, halo_h, tm):
    row = lax.broadcasted_iota(jnp.int32, h.shape, 0)
    last1 = halo_h[HALO - 1:HALO, :]
    last2 = halo_h[HALO - 2:HALO - 1, :]
    h1 = jnp.where(row == 0, last1, pltpu.roll(h, 1, 0))
    h2 = jnp.where(row == 0, last2, jnp.where(row == 1, last1, pltpu.roll(h, 2, 0)))
    return h1, h2


def _mixer_b_forward(p_b, conv_w, conv_b, tm, tc):
    t = p_b.shape[1]

    def body(p_ref, halo_ref, w_ref, b_ref, o_ref):
        valid = (pl.program_id(1) > 0).astype(F32)
        h = p_ref[1].astype(F32) * p_ref[0].astype(F32)
        halo_h = halo_ref[1].astype(F32) * halo_ref[0].astype(F32) * valid
        h1, h2 = _conv_taps(h, halo_h, tm)
        w = w_ref[...]
        conv = b_ref[...] + w[0:1, :] * h2 + w[1:2, :] * h1 + w[2:3, :] * h
        z = p_ref[3].astype(F32)
        o_ref[...] = (p_ref[2].astype(F32) * conv * (z * _sigmoid(z))).astype(BF16)

    steps = tm // HALO
    return pl.pallas_call(
        body, name="mixer_b_forward", grid=(D_MODEL // tc, t // tm),
        in_specs=[pl.BlockSpec((4, tm, tc), lambda j, i: (BLOCK_B, i, j)),
                  pl.BlockSpec((4, HALO, tc), lambda j, i: (BLOCK_B, jnp.maximum(i * steps - 1, 0), j)),
                  pl.BlockSpec((3, tc), lambda j, i: (0, j)),
                  pl.BlockSpec((1, tc), lambda j, i: (0, j))],
        out_specs=pl.BlockSpec((tm, tc), lambda j, i: (i, j)),
        out_shape=jax.ShapeDtypeStruct((t, D_MODEL), BF16),
        compiler_params=_params("parallel", "parallel"),
    )(p_b, p_b, conv_w, conv_b)


def _mixer_b_backward(p_b, dyab, conv_w, conv_b, dp, tm, tc):
    t = p_b.shape[1]
    n = t // tm

    def body(p_ref, halo_ref, dy_ref, w_ref, b_ref, dp_in, dp_ref, dw_ref, db_ref, next_ref):
        ii = pl.program_id(1)

        @pl.when(ii == 0)
        def _():
            dw_ref[...] = jnp.zeros_like(dw_ref)
            db_ref[...] = jnp.zeros_like(db_ref)
            next_ref[...] = jnp.zeros_like(next_ref)

        valid = (ii < n - 1).astype(F32)
        xb = p_ref[0].astype(F32)
        cb = p_ref[1].astype(F32)
        bb = p_ref[2].astype(F32)
        z = p_ref[3].astype(F32)
        h = cb * xb
        halo_h = halo_ref[1].astype(F32) * halo_ref[0].astype(F32) * valid
        h1, h2 = _conv_taps(h, halo_h, tm)
        w = w_ref[...]
        w0, w1, w2 = w[0:1, :], w[1:2, :], w[2:3, :]
        conv = b_ref[...] + w0 * h2 + w1 * h1 + w2 * h
        sig = _sigmoid(z)
        sz = z * sig
        dy = dy_ref[...].astype(F32)
        dconv = dy * bb * sz
        dp_ref[2] = (dy * conv * sz).astype(BF16)
        dp_ref[3] = (dy * bb * conv * _silu_grad(sig, sz)).astype(BF16)
        db_ref[...] += jnp.sum(dconv, axis=0, keepdims=True)
        dw_ref[0:1, :] += jnp.sum(dconv * h2, axis=0, keepdims=True)
        dw_ref[1:2, :] += jnp.sum(dconv * h1, axis=0, keepdims=True)
        dw_ref[2:3, :] += jnp.sum(dconv * h, axis=0, keepdims=True)
        row = lax.broadcasted_iota(jnp.int32, h.shape, 0)
        nxt = next_ref[...]
        n0, n1 = nxt[0:1, :], nxt[1:2, :]
        d1 = jnp.where(row == tm - 1, n0, pltpu.roll(dconv, tm - 1, 0))
        d2 = jnp.where(row == tm - 1, n1, jnp.where(row == tm - 2, n0, pltpu.roll(dconv, tm - 2, 0)))
        dh = w2 * dconv + w1 * d1 + w0 * d2
        dp_ref[0] = (dh * cb).astype(BF16)
        dp_ref[1] = (dh * xb).astype(BF16)
        next_ref[...] = dconv[0:8, :]

    steps = tm // HALO
    return pl.pallas_call(
        body, name="mixer_b_backward", grid=(D_MODEL // tc, n),
        in_specs=[pl.BlockSpec((4, tm, tc), lambda j, ii: (BLOCK_B, n - 1 - ii, j)),
                  pl.BlockSpec((4, HALO, tc), lambda j, ii: (BLOCK_B, jnp.maximum((n - 1 - ii) * steps - 1, 0), j)),
                  pl.BlockSpec((None, tm, tc), lambda j, ii: (1, n - 1 - ii, j)),
                  pl.BlockSpec((3, tc), lambda j, ii: (0, j)),
                  pl.BlockSpec((1, tc), lambda j, ii: (0, j)), ANY],
        out_specs=(pl.BlockSpec((4, tm, tc), lambda j, ii: (BLOCK_B, n - 1 - ii, j)),
                   pl.BlockSpec((3, tc), lambda j, ii: (0, j)),
                   pl.BlockSpec((1, tc), lambda j, ii: (0, j))),
        out_shape=(jax.ShapeDtypeStruct(dp.shape, BF16),
                   jax.ShapeDtypeStruct((3, D_MODEL), F32), jax.ShapeDtypeStruct((1, D_MODEL), F32)),
        scratch_shapes=[pltpu.VMEM((8, tc), F32)],
        input_output_aliases={5: 0},
        compiler_params=_params("parallel", "arbitrary"),
    )(p_b, p_b, dyab, conv_w, conv_b, dp)


def _adam_math(w, g, m, v):
    m = ADAM_B1 * m + (1.0 - ADAM_B1) * g
    v = ADAM_B2 * v + (1.0 - ADAM_B2) * (g * g)
    delta = -ADAM_LR * ((m * ADAM_C1) / (jnp.sqrt(v * ADAM_C2) + ADAM_EPS) + ADAM_WD * w)
    return delta, m, v


def _adam_rows(w, g, m, v, tm, name, g_sel=None):
    r, c = w.shape

    def body(w_ref, g_ref, m_ref, v_ref, go_ref, d_ref, mo_ref, vo_ref):
        g = g_ref[...]
        d, mn, vn = _adam_math(w_ref[...], g, m_ref[...], v_ref[...])
        go_ref[...] = g
        d_ref[...] = d
        mo_ref[...] = mn
        vo_ref[...] = vn

    spec = pl.BlockSpec((tm, c), lambda i: (i, 0))
    g_spec = spec if g_sel is None else pl.BlockSpec((None, tm, c), lambda i: (g_sel, i, 0))
    shape = jax.ShapeDtypeStruct((r, c), F32)
    return pl.pallas_call(
        body, name=name, grid=(r // tm,),
        in_specs=[spec, g_spec, spec, spec], out_specs=(spec,) * 4, out_shape=(shape,) * 4,
        compiler_params=_params("parallel"),
    )(w, g, m, v)


SMALL_ROW0 = {name: sum(r for _, r in SMALL_ROWS[:i]) for i, (name, _) in enumerate(SMALL_ROWS)}
LANE_MAJOR = ("ln_g", "ln_b", "b_gate", "ln_v_g", "ln_v_b", "conv_b")


def _lane_pieces(n):
    return [(q, slice(q * 128, (q + 1) * 128)) for q in range(n // 128)]


def _pack_small(d_ln_g, d_ln_b, d_bg, d_lnv_g, d_lnv_b, d_ws, d_bs, d_cw, d_cb, loss_part):
    def body(lg, lb, bg, vg, vb, ws, bs, cw, cb, loss, o_ref):
        def put(row0, vec):
            for q, cols in _lane_pieces(vec.shape[1]):
                o_ref[row0 + q:row0 + q + 1, :] = vec[:, cols]

        put(SMALL_ROW0["ln_g"], lg[...])
        put(SMALL_ROW0["ln_b"], lb[...])
        for n in range(2):
            put(SMALL_ROW0["b_gate"] + n * (D_MODEL // 128), bg[n])
        put(SMALL_ROW0["ln_v_g"], vg[...])
        put(SMALL_ROW0["ln_v_b"], vb[...])
        for h in range(N_HEADS):
            o_ref[SMALL_ROW0["w_s"] + h * CHUNK:SMALL_ROW0["w_s"] + (h + 1) * CHUNK, :] = ws[h]
        o_ref[SMALL_ROW0["b_s"]:SMALL_ROW0["b_s"] + N_HEADS, :] = bs[...]
        for c in range(3):
            put(SMALL_ROW0["conv_w"] + c * (D_MODEL // 128), cw[c:c + 1, :])
        put(SMALL_ROW0["conv_b"], cb[...])
        o_ref[SMALL_ROW0["loss"]:SMALL_ROW0["loss"] + 8, :] = jnp.broadcast_to(loss[...], (8, 128))

    return pl.pallas_call(
        body, name="pack_small", out_shape=jax.ShapeDtypeStruct((SMALL_TOTAL, 128), F32), compiler_params=_params(),
    )(d_ln_g, d_ln_b, d_bg, d_lnv_g, d_lnv_b, d_ws, d_bs.reshape(N_HEADS, CHUNK), d_cw, d_cb, loss_part)


def _adam_small(gathered, params):
    names = list(params)
    flat = [a for n in names for a in params[n]]

    def body(*refs):
        ga_ref = refs[0]
        ins = refs[1:1 + 3 * len(names)]
        outs = refs[1 + 3 * len(names):-1]
        gs_ref = refs[-1]
        g = ga_ref[0]
        for k in range(1, 8):
            g = g + ga_ref[k]
        gs_ref[...] = g
        for i, name in enumerate(names):
            w_ref, m_ref, v_ref = ins[3 * i:3 * i + 3]
            o_refs = outs[4 * i:4 * i + 4]
            row0 = SMALL_ROW0[name]
            if name in LANE_MAJOR:
                pieces = [((slice(None), cols), slice(row0 + q, row0 + q + 1))
                          for q, cols in _lane_pieces(w_ref.shape[1])]
            elif name == "w_s":
                pieces = [((0, h), slice(row0 + h * CHUNK, row0 + (h + 1) * CHUNK)) for h in range(N_HEADS)]
            else:
                pieces = [((0,), slice(row0, row0 + N_HEADS))]
            for idx, rows in pieces:
                gp = gs_ref[rows, :]
                res = (gp,) + _adam_math(w_ref[idx], gp, m_ref[idx], v_ref[idx])
                for o_ref, val in zip(o_refs, res):
                    o_ref[idx] = val
        gcw_ref, loss_ref = outs[-2:]
        for c in range(3):
            for q, cols in _lane_pieces(D_MODEL):
                r = SMALL_ROW0["conv_w"] + c * (D_MODEL // 128) + q
                gcw_ref[c:c + 1, cols] = gs_ref[r:r + 1, :]
        loss_ref[...] = gs_ref[SMALL_ROW0["loss"]:SMALL_ROW0["loss"] + 1, :]

    out_shape = [jax.ShapeDtypeStruct(params[n][0].shape, F32) for n in names for _ in range(4)]
    out_shape += [jax.ShapeDtypeStruct((3, D_MODEL), F32), jax.ShapeDtypeStruct((1, 128), F32)]
    res = pl.pallas_call(
        body, name="adam_small", out_shape=tuple(out_shape),
        scratch_shapes=[pltpu.VMEM((SMALL_TOTAL, 128), F32)], compiler_params=_params(),
    )(gathered, *flat)
    return {n: res[4 * i:4 * i + 4] for i, n in enumerate(names)}, res[-2], res[-1]


def _adam_conv_w(g_all, chip, w, m, v):
    cols = w.shape[2]

    def body(c_ref, g_ref, w_ref, m_ref, v_ref, go_ref, d_ref, mo_ref, vo_ref):
        g = g_ref[...]
        d, mn, vn = _adam_math(w_ref[...], g, m_ref[...], v_ref[...])
        go_ref[...] = g
        d_ref[...] = d
        mo_ref[...] = mn
        vo_ref[...] = vn

    own = pl.BlockSpec((None, 3, cols), lambda i, c_ref: (0, 0, 0))
    return pl.pallas_call(
        body, name="adam_conv_w",
        grid_spec=pltpu.PrefetchScalarGridSpec(
            num_scalar_prefetch=1, grid=(1,),
            in_specs=[pl.BlockSpec((3, cols), lambda i, c_ref: (0, c_ref[0])), own, own, own],
            out_specs=(own,) * 4),
        out_shape=(jax.ShapeDtypeStruct(w.shape, F32),) * 4,
        compiler_params=_params("arbitrary"),
    )(chip, g_all, w, m, v)


def kernel(x, w_in, b_gate, ln_v_g, ln_v_b, w_s, b_s, conv_w, conv_b, w_oa, w_ob, w_out, ln_g, ln_b, loss_target, m_w_in, m_b_gate, m_ln_v_g, m_ln_v_b, m_w_s, m_b_s, m_conv_w, m_conv_b, m_w_oa, m_w_ob, m_w_out, m_ln_g, m_ln_b, v_w_in, v_b_gate, v_ln_v_g, v_ln_v_b, v_w_s, v_b_s, v_conv_w, v_conv_b, v_w_oa, v_w_ob, v_w_out, v_ln_g, v_ln_b):
    t = x.shape[1]
    x2 = x[0]
    target = loss_target[0]
    chip = 2 * lax.axis_index("x") + lax.axis_index("y")
    conv_cols = conv_w.shape[2]

    chip1 = chip.astype(jnp.int32).reshape(1)
    w_pre = _cast_into_columns(w_in[0], chip1, N_CHIPS, 256, "cast_w_in")
    wo_b, wm = _prep_small_weights(w_oa[0], w_ob[0], w_out[0], w_s[0])
    conv_w8 = jnp.concatenate([conv_w[0], jnp.zeros((5, conv_cols), F32)], axis=0)
    bs_col = b_s[0].reshape(N_HEADS, CHUNK, 1)
    bg = b_gate.reshape(2, 1, D_MODEL)

    xb, xt = _cast_and_transpose(x2, min(512, t))
    p, w_full, wo_full, cw_full = _gather_and_project(xb, w_pre, wo_b, conv_w8, min(4096, t))
    wo3 = wo_full.reshape(3, D_MODEL, D_MODEL)
    conv_w_all = jnp.transpose(cw_full[:, :3, :], (1, 0, 2)).reshape(3, D_MODEL)
    tm_a = min(2048, t)
    ya = _mixer_a_forward(p, wm, bs_col, ln_v_g, ln_v_b, tm_a)
    tm_b = min(1024, t)
    yb = _mixer_b_forward(p, conv_w_all, conv_b, tm_b, 512)
    tm_m = min(1024, t)
    merged, oab = _merge_forward(ya, yb, wo3, p, bg, tm_m, 512)

    drb, gx, dmerged, d_ln_g, d_ln_b, loss_part = _head(merged, wo3, x2, target, ln_g, ln_b, min(512, t))
    doab, dp, d_bg, dyab = _gate_and_branch_backward(dmerged, oab, p, bg, wo3, min(256, t))
    dp, d_ws, d_bs, d_lnv_g, d_lnv_b = _mixer_a_backward(p, dyab, wm, bs_col, ln_v_g, ln_v_b, dp, tm_a)
    dp, d_cw, d_cb = _mixer_b_backward(p, dyab, conv_w_all, conv_b, dp, tm_b, 512)

    tk = min(2048, t)
    g_in = _win_grad(xt, dp, 1024, tk)
    g_oa = _weight_grad(ya, 0, doab, 0, 1024, 1024, tk, "grad_w_oa")
    g_ob = _weight_grad(yb, 0, doab, 1, 1024, 1024, tk, "grad_w_ob")
    g_out, r_in = _weight_grad(merged, 0, drb, 0, 1024, 1024, tk, "grad_w_out", exchange=g_in)
    r_o = _exchange_halves((g_oa, g_ob, g_out))
    s_in, s_o = _add_halves(g_in, (g_oa, g_ob, g_out), r_in, r_o)
    small_part = _pack_small(d_ln_g, d_ln_b, d_bg, d_lnv_g, d_lnv_b, d_ws, d_bs, d_cw, d_cb, loss_part)
    gx, q_in, q_o, gathered = _input_grad_and_scatter(dp, w_full, gx, s_in, s_o, small_part, min(1024, t))
    f_in, f_o = _sum_chips(q_in, q_o, s_in, s_o)
    gsum_in, gsum_o = _share_halves(f_in, f_o)

    big = {}
    grad_x, big["w_in"] = _input_grad_tail_and_adam(dp, w_full, gx, w_in[0], gsum_in, m_w_in[0], v_w_in[0],
                                                    min(32, t // 128))
    for n, (name, w, m, v) in enumerate((("w_oa", w_oa, m_w_oa, v_w_oa), ("w_ob", w_ob, m_w_ob, v_w_ob),
                                         ("w_out", w_out, m_w_out, v_w_out))):
        big[name] = _adam_rows(w[0], gsum_o, m[0], v[0], 256, "adam_" + name, g_sel=n)

    small, g_conv_w, loss_row = _adam_small(gathered, {
        "ln_g": (ln_g, m_ln_g, v_ln_g), "ln_b": (ln_b, m_ln_b, v_ln_b), "b_gate": (b_gate, m_b_gate, v_b_gate),
        "ln_v_g": (ln_v_g, m_ln_v_g, v_ln_v_g), "ln_v_b": (ln_v_b, m_ln_v_b, v_ln_v_b),
        "w_s": (w_s, m_w_s, v_w_s), "b_s": (b_s, m_b_s, v_b_s), "conv_b": (conv_b, m_conv_b, v_conv_b)})
    small["conv_w"] = _adam_conv_w(g_conv_w, chip1, conv_w, m_conv_w, v_conv_w)
    loss = loss_row[0, 0]

    order = ("w_in", "b_gate", "ln_v_g", "ln_v_b", "w_s", "b_s", "conv_w", "conv_b", "w_oa", "w_ob", "w_out",
             "ln_g", "ln_b")
    outs = [loss, grad_x[None]]
    for which in range(4):
        for name in order:
            outs.append(big[name][which][None] if name in big else small[name][which])
    return tuple(outs)
```

```python
import functools
import math

import jax
import jax.numpy as jnp
from jax import lax
from jax.experimental import pallas as pl
from jax.experimental.pallas import tpu as pltpu

F32 = jnp.float32
BF16 = jnp.bfloat16

D_MODEL = 2048
N_HEADS = 8
HEAD_DIM = D_MODEL // N_HEADS
CHUNK = 128
N_SEG = 9
N_CHIPS = 4
SHARD_COLS = N_SEG * D_MODEL // N_CHIPS
COL_BLOCK = 512
BLOCKS_PER_SHARD = SHARD_COLS // COL_BLOCK
BLOCKS_PER_SEG = D_MODEL // COL_BLOCK
SUBS = 3
SHARD_ROWS = D_MODEL // N_CHIPS
N_SLOTS = 12
BLOCK_A, BLOCK_G, BLOCK_B = 0, 2, 2
SLOT_OF_SEG = (0, 1, 2, 8, 9, 10, 11, 4, 5)
DN_ALPHA = 2.0 ** 0.25
LN_EPS = 1e-5
GELU_K = math.sqrt(2.0 / math.pi)
GELU_C = 0.044715

ADAM_LR = 0.001
ADAM_B1 = 0.9
ADAM_B2 = 0.999
ADAM_EPS = 1e-08
ADAM_WD = 0.01
ADAM_STEP = 10
ADAM_C1 = 1.0 / (1.0 - ADAM_B1 ** ADAM_STEP)
ADAM_C2 = 1.0 / (1.0 - ADAM_B2 ** ADAM_STEP)

VMEM_LIMIT = 60 * 1024 * 1024
MESH = pl.DeviceIdType.MESH
ANY = pl.BlockSpec(memory_space=pl.ANY)

SMALL_ROWS = (("ln_g", 16), ("ln_b", 16), ("b_gate", 32), ("ln_v_g", 16), ("ln_v_b", 16),
              ("w_s", 1024), ("b_s", 8), ("conv_w", 48), ("conv_b", 16), ("loss", 8))
SMALL_TOTAL = sum(r for _, r in SMALL_ROWS)


def _params(*sem):
    return pltpu.CompilerParams(dimension_semantics=sem, vmem_limit_bytes=VMEM_LIMIT)


def _sigmoid(x):
    return 1.0 / (1.0 + jnp.exp(-x))


def _gelu_gate(x, x2):
    return 1.0 / (1.0 + jnp.exp(x * ((-2.0 * GELU_K) + (-2.0 * GELU_K * GELU_C) * x2)))


def _gelu_and_grad(x):
    x2 = x * x
    s = _gelu_gate(x, x2)
    g = x * s
    dg = s + g * (1.0 - s) * ((2.0 * GELU_K) + (6.0 * GELU_K * GELU_C) * x2)
    return g, dg


def _silu_grad(sig, sz):
    return sig + sz * (1.0 - sig)


def _dot(a, b, ca=1, cb=0):
    return lax.dot_general(a, b, (((ca,), (cb,)), ((), ())), preferred_element_type=F32)


def _cast_and_transpose(x, tm):
    t, d = x.shape

    def body(x_ref, o_ref, ot_ref):
        v = x_ref[...]
        o_ref[...] = v.astype(BF16)
        ot_ref[...] = v.T.astype(BF16)

    return pl.pallas_call(
        body, name="cast_x", grid=(t // tm,),
        in_specs=[pl.BlockSpec((tm, d), lambda i: (i, 0))],
        out_specs=(pl.BlockSpec((tm, d), lambda i: (i, 0)), pl.BlockSpec((d, tm), lambda i: (0, i))),
        out_shape=(jax.ShapeDtypeStruct((t, d), BF16), jax.ShapeDtypeStruct((d, t), BF16)),
        compiler_params=_params("parallel"),
    )(x)


def _cast_into_columns(w, slot, n_slots, tm, name):
    r, c = w.shape

    def body(s_ref, w_ref, o_ref):
        o_ref[...] = w_ref[...].astype(BF16)

    return pl.pallas_call(
        body, name=name,
        grid_spec=pltpu.PrefetchScalarGridSpec(
            num_scalar_prefetch=1, grid=(r // tm,),
            in_specs=[pl.BlockSpec((tm, c), lambda i, s_ref: (i, 0))],
            out_specs=pl.BlockSpec((tm, c), lambda i, s_ref: (i, s_ref[0]))),
        out_shape=jax.ShapeDtypeStruct((r, n_slots * c), BF16),
        compiler_params=_params("parallel"),
    )(slot, w)


def _prep_small_weights(w_oa, w_ob, w_out, w_s):
    rows = w_oa.shape[0]

    def body(a_ref, b_ref, c_ref, ws_ref, wo_ref, wm_ref):
        wo_ref[0] = a_ref[...].astype(BF16)
        wo_ref[1] = b_ref[...].astype(BF16)
        wo_ref[2] = c_ref[...].astype(BF16)
        t = lax.broadcasted_iota(jnp.int32, (CHUNK, CHUNK), 0)
        s = lax.broadcasted_iota(jnp.int32, (CHUNK, CHUNK), 1)
        for h in range(N_HEADS):
            wm_ref[h] = jnp.where(s <= t, ws_ref[h], 0.0).astype(BF16)

    return pl.pallas_call(
        body, name="prep_small_weights",
        out_shape=(jax.ShapeDtypeStruct((3, rows, D_MODEL), BF16),
                   jax.ShapeDtypeStruct((N_HEADS, CHUNK, CHUNK), BF16)),
        compiler_params=_params(),
    )(w_oa, w_ob, w_out, w_s)


def _mesh_pos():
    x, y, c = lax.axis_index("x"), lax.axis_index("y"), lax.axis_index("c")
    chips = [(1 - x, y), (x, 1 - y), (1 - x, 1 - y)]
    return x, y, c, chips


def _slot_of_seg(seg):
    return jnp.where(seg < 3, seg, jnp.where(seg < 7, seg + 5, seg - 3))


def _gather_and_project(xb, w_pre, wo_b, conv_w8, tm):
    t = xb.shape[0]
    d, sc = w_pre.shape[0], w_pre.shape[1] // N_CHIPS
    rows = wo_b.shape[1]
    hd, hr = d // 2, rows // 2
    nj = BLOCKS_PER_SHARD // SUBS
    pc = nj * COL_BLOCK
    units = N_CHIPS * SUBS
    ni = t // tm
    total = units * ni * nj
    mx, my = lax.axis_index("x"), lax.axis_index("y")
    order = jnp.stack([2 * mx + my, 2 * (1 - mx) + my, 2 * mx + (1 - my),
                       2 * (1 - mx) + (1 - my)]).astype(jnp.int32)

    def body(order_ref, x_ref, wpre_ref, wo_ref, cw_ref, p_ref, wf_ref, wof_ref, cwf_ref,
             wbuf, wsem, xbuf, xsem, send_sems, recv_sems, local_sems):
        x, y, c, chips = _mesh_pos()
        me = 2 * x + y
        sibling = (x, y, 1 - c)
        u, i, j = pl.program_id(0), pl.program_id(1), pl.program_id(2)
        n = (u * ni + i) * nj + j
        m = u * ni + i

        def rows_start(m_):
            pltpu.make_async_copy(x_ref.at[pl.ds(pl.multiple_of(lax.rem(m_, ni) * tm, tm), tm), :],
                                  xbuf.at[lax.rem(m_, 2)], xsem.at[lax.rem(m_, 2)]).start()

        def chip_of(q):
            return 2 * chips[q][0] + chips[q][1]

        def piece(ref, k, cc, r):
            return ref.at[pl.ds(cc * hd, hd), pl.ds(pl.multiple_of(k * sc + r * pc, COL_BLOCK), pc)]

        def wo_half(ref4, k, cc):
            return ref4.at[:, k, pl.ds(cc * hr, hr), :]

        def rcopy(sem, src, dst, to):
            return pltpu.make_async_remote_copy(src_ref=src, dst_ref=dst, send_sem=send_sems.at[sem],
                                                recv_sem=recv_sems.at[sem], device_id=to, device_id_type=MESH)

        def w_send(q, r):
            return rcopy(q * SUBS + r, piece(wpre_ref, me, c, r), piece(wf_ref, me, c, r), (*chips[q], c))

        def w_landed(q, r):
            return rcopy(q * SUBS + r, piece(wpre_ref, me, c, r), piece(wf_ref, chip_of(q), c, r), sibling)

        def w_forward(q, r, cc):
            ref = piece(wf_ref, chip_of(q), cc, r)
            return rcopy(9 + q * SUBS + r, ref, ref, sibling)

        def wo_send(q):
            return rcopy(18 + q, wo_ref.at[:, pl.ds(c * hr, hr), :], wo_half(wof_ref, me, c), (*chips[q], c))

        def wo_landed(q):
            return rcopy(18 + q, wo_ref.at[:, pl.ds(c * hr, hr), :], wo_half(wof_ref, chip_of(q), c), sibling)

        def wo_forward(q, cc):
            ref = wo_half(wof_ref, chip_of(q), cc)
            return rcopy(21 + q, ref, ref, sibling)

        def conv_send(q):
            return rcopy(24 + q, cw_ref, cwf_ref.at[me], (*chips[q], c))

        def local_copies():
            return [pltpu.make_async_copy(wo_ref, wof_ref.at[:, me], local_sems.at[0]),
                    pltpu.make_async_copy(cw_ref, cwf_ref.at[me], local_sems.at[1])]

        def tile_start(u_, j_, slot):
            g = order_ref[u_ // SUBS] * BLOCKS_PER_SHARD + lax.rem(u_, SUBS) * nj + j_
            cols = pl.ds(pl.multiple_of(g * COL_BLOCK, COL_BLOCK), COL_BLOCK)

            @pl.when(u_ < SUBS)
            def _():
                pltpu.make_async_copy(wpre_ref.at[:, cols], wbuf.at[slot], wsem.at[slot]).start()

            @pl.when(u_ >= SUBS)
            def _():
                pltpu.make_async_copy(wf_ref.at[:, cols], wbuf.at[slot], wsem.at[slot]).start()

        def end_of(u_):
            return jnp.logical_and(u == u_, jnp.logical_and(i == ni - 1, j == nj - 1))

        @pl.when(n == 0)
        def _():
            for cp in local_copies():
                cp.start()
            for r in range(SUBS):
                for q in (0, 1):
                    w_send(q, r).start()
            for q in range(3):
                conv_send(q).start()
            tile_start(0, 0, 0)
            rows_start(0)

        def pass_on(q, r):
            w_landed(q, r).wait_recv()
            w_forward(q, r, c).start()

        for u_ in range(1, units - 1):
            @pl.when(end_of(u_))
            def _(u_=u_):
                if u_ <= SUBS:
                    for q in (0, 1):
                        pass_on(q, u_ - 1)
                if u_ == SUBS:
                    for r in range(SUBS):
                        for q in (0, 1):
                            w_send(q, r).wait_send()
                    for r in range(SUBS):
                        w_send(2, r).start()
                if 2 * SUBS - 1 <= u_ <= 3 * SUBS - 2:
                    pass_on(2, u_ - (2 * SUBS - 1))
                if u_ == 3 * SUBS - 2:
                    for r in range(SUBS):
                        w_send(2, r).wait_send()
                    for q in range(3):
                        wo_send(q).start()
                nxt = u_ + 1
                if nxt >= SUBS:
                    w_forward(nxt // SUBS - 1, nxt % SUBS, 1 - c).wait_recv()

        n1 = n + 1

        @pl.when(n1 < total)
        def _():
            tile_start(n1 // (ni * nj), lax.rem(n1, nj), lax.rem(n1, 2))

        xslot = lax.rem(m, 2)

        @pl.when(j == 0)
        def _():
            @pl.when(m + 1 < units * ni)
            def _():
                rows_start(m + 1)

            pltpu.make_async_copy(x_ref.at[pl.ds(0, tm), :], xbuf.at[xslot], xsem.at[xslot]).wait()

        slot = lax.rem(n, 2)
        pltpu.make_async_copy(wpre_ref.at[:, pl.ds(0, COL_BLOCK)], wbuf.at[slot], wsem.at[slot]).wait()
        p_ref[...] = _dot(xbuf[xslot], wbuf[slot]).astype(BF16)

        @pl.when(n == total - 1)
        def _():
            for q in range(3):
                wo_landed(q).wait_recv()
                wo_forward(q, c).start()
            for q in range(3):
                wo_forward(q, 1 - c).wait_recv()
                rcopy(24 + q, cw_ref, cwf_ref.at[chip_of(q)], sibling).wait_recv()
            for q in range(3):
                for r in range(SUBS):
                    w_forward(q, r, c).wait_send()
                wo_send(q).wait_send()
                wo_forward(q, c).wait_send()
                conv_send(q).wait_send()
            for cp in local_copies():
                cp.wait()

    def p_map(u, i, j, o):
        g = o[u // SUBS] * BLOCKS_PER_SHARD + lax.rem(u, SUBS) * nj + j
        return (_slot_of_seg(g // BLOCKS_PER_SEG), i, lax.rem(g, BLOCKS_PER_SEG))

    return pl.pallas_call(
        body, name="gather_and_project",
        grid_spec=pltpu.PrefetchScalarGridSpec(
            num_scalar_prefetch=1, grid=(units, ni, nj),
            in_specs=[ANY, ANY, ANY, ANY],
            out_specs=(pl.BlockSpec((None, tm, COL_BLOCK), p_map), ANY, ANY, ANY),
            scratch_shapes=[pltpu.VMEM((2, d, COL_BLOCK), BF16), pltpu.SemaphoreType.DMA((2,)),
                            pltpu.VMEM((2, tm, D_MODEL), BF16), pltpu.SemaphoreType.DMA((2,)),
                            pltpu.SemaphoreType.DMA((27,)), pltpu.SemaphoreType.DMA((27,)),
                            pltpu.SemaphoreType.DMA((2,))]),
        out_shape=(jax.ShapeDtypeStruct((N_SLOTS, t, D_MODEL), BF16),
                   jax.ShapeDtypeStruct((d, N_CHIPS * sc), BF16),
                   jax.ShapeDtypeStruct((3, N_CHIPS, rows, D_MODEL), BF16),
                   jax.ShapeDtypeStruct((N_CHIPS,) + conv_w8.shape, F32)),
        input_output_aliases={2: 1},
        compiler_params=pltpu.CompilerParams(dimension_semantics=("arbitrary",) * 3, vmem_limit_bytes=VMEM_LIMIT,
                                             has_side_effects=True),
    )(order, xb, w_pre, wo_b, conv_w8)


def _exchange_halves(g_o):
    hr = SHARD_ROWS // 2
    g_o4 = [g.reshape(N_CHIPS, 2, hr, D_MODEL) for g in g_o]

    def body(ga_ref, gb_ref, gc_ref, ra_ref, rb_ref, rc_ref, send_sems, recv_sems):
        x, y, c, _ = _mesh_pos()
        sibling = (x, y, 1 - c)
        cps = []
        for n, (g_ref, r_ref) in enumerate(((ga_ref, ra_ref), (gb_ref, rb_ref), (gc_ref, rc_ref))):
            cps.append(pltpu.make_async_remote_copy(src_ref=g_ref.at[:, 1 - c], dst_ref=r_ref,
                                                    send_sem=send_sems.at[n], recv_sem=recv_sems.at[n],
                                                    device_id=sibling, device_id_type=MESH))
        for cp in cps:
            cp.start()
        for cp in cps:
            cp.wait()

    o_shape = jax.ShapeDtypeStruct((N_CHIPS, hr, D_MODEL), BF16)
    return pl.pallas_call(
        body, name="rs_exchange_halves",
        in_specs=[ANY] * 3, out_specs=(ANY,) * 3,
        out_shape=(o_shape, o_shape, o_shape),
        scratch_shapes=[pltpu.SemaphoreType.DMA((3,)), pltpu.SemaphoreType.DMA((3,))],
        compiler_params=pltpu.CompilerParams(has_side_effects=True),
    )(*g_o4)


def _add_halves(g_in, g_o, r_in, r_o):
    d, c9 = g_in.shape
    hd = d // 2
    hr = SHARD_ROWS // 2
    core = lax.axis_index("c").astype(jnp.int32).reshape(1)
    tm = min(512, hd)
    nb = hd // tm

    def body_in(c_ref, g_ref, r_ref, o_ref):
        o_ref[...] = (g_ref[...].astype(F32) + r_ref[...].astype(F32)).astype(BF16)

    s_in = pl.pallas_call(
        body_in, name="rs_add_halves_in",
        grid_spec=pltpu.PrefetchScalarGridSpec(
            num_scalar_prefetch=1, grid=(N_CHIPS, nb),
            in_specs=[pl.BlockSpec((tm, SHARD_COLS), lambda k, i, c_ref: (c_ref[0] * nb + i, k)),
                      pl.BlockSpec((tm, SHARD_COLS), lambda k, i, c_ref: (i, k))],
            out_specs=pl.BlockSpec((None, tm, SHARD_COLS), lambda k, i, c_ref: (k, i, 0))),
        out_shape=jax.ShapeDtypeStruct((N_CHIPS, hd, SHARD_COLS), BF16),
        compiler_params=_params("parallel", "parallel"),
    )(core, g_in, r_in)

    g_o4 = [g.reshape(N_CHIPS, 2, hr, D_MODEL) for g in g_o]

    def body_o(c_ref, ga_ref, gb_ref, gc_ref, ra_ref, rb_ref, rc_ref, o_ref):
        for n, (g_ref, r_ref) in enumerate(((ga_ref, ra_ref), (gb_ref, rb_ref), (gc_ref, rc_ref))):
            o_ref[n] = (g_ref[...].astype(F32) + r_ref[...].astype(F32)).astype(BF16)

    gspec = pl.BlockSpec((None, None, hr, D_MODEL), lambda k, c_ref: (k, c_ref[0], 0, 0))
    rspec = pl.BlockSpec((None, hr, D_MODEL), lambda k, c_ref: (k, 0, 0))
    s_o = pl.pallas_call(
        body_o, name="rs_add_halves_o",
        grid_spec=pltpu.PrefetchScalarGridSpec(
            num_scalar_prefetch=1, grid=(N_CHIPS,),
            in_specs=[gspec] * 3 + [rspec] * 3,
            out_specs=pl.BlockSpec((3, None, hr, D_MODEL), lambda k, c_ref: (0, k, 0, 0))),
        out_shape=jax.ShapeDtypeStruct((3, N_CHIPS, hr, D_MODEL), BF16),
        compiler_params=_params("parallel"),
    )(core, *g_o4, *r_o)
    return s_in, s_o


def _sum_chips(r_in, r_o, s_in, s_o):
    _, hd, sc = r_in.shape
    hr = r_o.shape[2]
    tm = min(256, hd)
    nb = hd // tm
    pos = jnp.stack([2 * lax.axis_index("x") + lax.axis_index("y"), lax.axis_index("c")]).astype(jnp.int32)

    def chip_sum(pos_ref, r_ref, s_ref):
        acc = None
        for k in range(N_CHIPS):
            term = jnp.where(pos_ref[0] == k, s_ref[...], r_ref[k]).astype(F32)
            acc = term if acc is None else acc + term
        return acc

    def body_in(pos_ref, r_ref, s_ref, o_ref):
        o_ref[...] = chip_sum(pos_ref, r_ref, s_ref)

    f_in = pl.pallas_call(
        body_in, name="rs_sum_chips_in",
        grid_spec=pltpu.PrefetchScalarGridSpec(
            num_scalar_prefetch=1, grid=(nb,),
            in_specs=[pl.BlockSpec((N_CHIPS, tm, sc), lambda i, p: (0, i, 0)),
                      pl.BlockSpec((None, tm, sc), lambda i, p: (p[0], i, 0))],
            out_specs=pl.BlockSpec((tm, sc), lambda i, p: (p[1] * nb + i, 0))),
        out_shape=jax.ShapeDtypeStruct((2 * hd, sc), F32),
        compiler_params=_params("parallel"),
    )(pos, r_in, s_in)

    def body_o(pos_ref, r_ref, s_ref, o_ref):
        o_ref[...] = chip_sum(pos_ref, r_ref, s_ref)

    f_o = pl.pallas_call(
        body_o, name="rs_sum_chips_o",
        grid_spec=pltpu.PrefetchScalarGridSpec(
            num_scalar_prefetch=1, grid=(3,),
            in_specs=[pl.BlockSpec((N_CHIPS, None, hr, D_MODEL), lambda n, p: (0, n, 0, 0)),
                      pl.BlockSpec((None, None, hr, D_MODEL), lambda n, p: (n, p[0], 0, 0))],
            out_specs=pl.BlockSpec((None, hr, D_MODEL), lambda n, p: (n, p[1], 0))),
        out_shape=jax.ShapeDtypeStruct((3, 2 * hr, D_MODEL), F32),
        compiler_params=_params("parallel"),
    )(pos, r_o, s_o)
    return f_in, f_o


def _share_halves(f_in, f_o):
    hd, sc = f_in.shape[0] // 2, f_in.shape[1]
    hr = f_o.shape[1] // 2

    def body(fi_ref, fo_ref, gi_ref, go_ref, send_sems, recv_sems):
        x, y, c, _ = _mesh_pos()
        sibling = (x, y, 1 - c)

        def halves(cc):
            rows_i, rows_o = pl.ds(cc * hd, hd), pl.ds(cc * hr, hr)
            return (fi_ref.at[rows_i, :], gi_ref.at[rows_i, :]), (fo_ref.at[:, rows_o, :], go_ref.at[:, rows_o, :])

        def copies(cc):
            return [pltpu.make_async_remote_copy(src_ref=src, dst_ref=dst, send_sem=send_sems.at[n],
                                                 recv_sem=recv_sems.at[n], device_id=sibling, device_id_type=MESH)
                    for n, (src, dst) in enumerate(halves(cc))]

        sends = copies(c)
        for cp in sends:
            cp.start()
        for cp in copies(1 - c):
            cp.wait_recv()
        for cp in sends:
            cp.wait_send()

    return pl.pallas_call(
        body, name="rs_share_halves",
        in_specs=[ANY, ANY], out_specs=(ANY, ANY),
        out_shape=(jax.ShapeDtypeStruct(f_in.shape, F32), jax.ShapeDtypeStruct(f_o.shape, F32)),
        scratch_shapes=[pltpu.SemaphoreType.DMA((2,)), pltpu.SemaphoreType.DMA((2,))],
        input_output_aliases={0: 0, 1: 1},
        compiler_params=pltpu.CompilerParams(has_side_effects=True),
    )(f_in, f_o)


def _merge_forward(ya, yb, wo3, p, bg, tm, tn):
    t = ya.shape[0]

    def body(ya_ref, yb_ref, wa_ref, wb_ref, g_ref, bg_ref, m_ref, oab_ref):
        oa = _dot(ya_ref[...], wa_ref[...])
        ob = _dot(yb_ref[...], wb_ref[...])
        ga = _sigmoid(g_ref[0].astype(F32) + bg_ref[0])
        gb = _sigmoid(g_ref[1].astype(F32) + bg_ref[1])
        m_ref[...] = (ga * oa + gb * ob).astype(BF16)
        oab_ref[0] = oa.astype(BF16)
        oab_ref[1] = ob.astype(BF16)

    return pl.pallas_call(
        body, name="merge_forward", grid=(t // tm, D_MODEL // tn),
        in_specs=[pl.BlockSpec((tm, D_MODEL), lambda i, j: (i, 0)),
                  pl.BlockSpec((tm, D_MODEL), lambda i, j: (i, 0)),
                  pl.BlockSpec((None, D_MODEL, tn), lambda i, j: (0, 0, j)),
                  pl.BlockSpec((None, D_MODEL, tn), lambda i, j: (1, 0, j)),
                  pl.BlockSpec((2, tm, tn), lambda i, j: (BLOCK_G, i, j)),
                  pl.BlockSpec((2, 1, tn), lambda i, j: (0, 0, j))],
        out_specs=(pl.BlockSpec((tm, tn), lambda i, j: (i, j)),
                   pl.BlockSpec((2, tm, tn), lambda i, j: (0, i, j))),
        out_shape=(jax.ShapeDtypeStruct((t, D_MODEL), BF16), jax.ShapeDtypeStruct((2, t, D_MODEL), BF16)),
        compiler_params=_params("parallel", "parallel"),
    )(ya, yb, wo3, wo3, p, bg)


HEAD_ROWS = 256


def _head(merged, wo3, x, target, ln_g, ln_b, tm):
    t = x.shape[0]
    inv_d = 1.0 / D_MODEL

    def body(m_ref, w_ref, x_ref, t_ref, g_ref, b_ref, dr_ref, gx_ref, dm_ref, dg_ref, db_ref, loss_ref):
        i = pl.program_id(0)

        @pl.when(i == 0)
        def _():
            dg_ref[...] = jnp.zeros_like(dg_ref)
            db_ref[...] = jnp.zeros_like(db_ref)
            loss_ref[...] = jnp.zeros_like(loss_ref)

        w = w_ref[...]
        g = g_ref[...]
        tiles = [slice(r0, r0 + HEAD_ROWS) for r0 in range(0, tm, HEAD_ROWS)]
        firsts = [_dot(m_ref[rows, :], w) for rows in tiles]
        for rows, out in zip(tiles, firsts):
            r = DN_ALPHA * x_ref[rows, :] + out
            mu = jnp.mean(r, axis=-1, keepdims=True)
            xc = r - mu
            var = jnp.mean(xc * xc, axis=-1, keepdims=True)
            rstd = lax.rsqrt(var + LN_EPS)
            xhat = xc * rstd
            e = xhat * g + b_ref[...] - t_ref[rows, :]
            se = jnp.sum(jnp.sum(e * e, axis=1, keepdims=True), axis=0, keepdims=True)
            loss_ref[...] += jnp.broadcast_to((0.5 * inv_d) * se, loss_ref.shape)
            dy = e * inv_d
            db_ref[...] += jnp.sum(dy, axis=0, keepdims=True)
            dg_ref[...] += jnp.sum(dy * xhat, axis=0, keepdims=True)
            dxh = dy * g
            m1 = jnp.mean(dxh, axis=-1, keepdims=True)
            m2 = jnp.mean(dxh * xhat, axis=-1, keepdims=True)
            dr = rstd * (dxh - m1 - xhat * m2)
            gx_ref[rows, :] = DN_ALPHA * dr
            drb = dr.astype(BF16)
            dr_ref[rows, :] = drb
            dm_ref[rows, :] = _dot(drb, w, 1, 1).astype(BF16)

    row = pl.BlockSpec((tm, D_MODEL), lambda i: (i, 0))
    vec = pl.BlockSpec((1, D_MODEL), lambda i: (0, 0))
    return pl.pallas_call(
        body, name="head", grid=(t // tm,),
        in_specs=[row, pl.BlockSpec((None, D_MODEL, D_MODEL), lambda i: (2, 0, 0), pipeline_mode=pl.Buffered(1)),
                  row, row, vec, vec],
        out_specs=(row, row, row, vec, vec, pl.BlockSpec((1, 128), lambda i: (0, 0))),
        out_shape=(jax.ShapeDtypeStruct((t, D_MODEL), BF16), jax.ShapeDtypeStruct((t, D_MODEL), F32),
                   jax.ShapeDtypeStruct((t, D_MODEL), BF16), jax.ShapeDtypeStruct((1, D_MODEL), F32),
                   jax.ShapeDtypeStruct((1, D_MODEL), F32), jax.ShapeDtypeStruct((1, 128), F32)),
        compiler_params=_params("arbitrary"),
    )(merged, wo3, x, target, ln_g, ln_b)


def _gate_and_branch_backward(dmerged, oab, p, bg, wo3, tm):
    t = dmerged.shape[0]

    def body(dm_ref, oab_ref, g_ref, bg_ref, wa_ref, wb_ref, do_ref, dpg_ref, dbg_ref, dy_ref):
        @pl.when(pl.program_id(0) == 0)
        def _():
            dbg_ref[...] = jnp.zeros_like(dbg_ref)

        dm = dm_ref[...].astype(F32)
        for n, w_ref in enumerate((wa_ref, wb_ref)):
            gate = _sigmoid(g_ref[n].astype(F32) + bg_ref[n])
            d_o = (dm * gate).astype(BF16)
            do_ref[n] = d_o
            dgate = dm * oab_ref[n].astype(F32) * gate * (1.0 - gate)
            dpg_ref[n] = dgate.astype(BF16)
            dbg_ref[n] += jnp.sum(dgate, axis=0, keepdims=True)
            dy_ref[n] = _dot(d_o, w_ref[...], 1, 1).astype(BF16)

    pair = pl.BlockSpec((2, tm, D_MODEL), lambda i: (0, i, 0))
    gates = pl.BlockSpec((2, tm, D_MODEL), lambda i: (BLOCK_G, i, 0))
    vec = pl.BlockSpec((2, 1, D_MODEL), lambda i: (0, 0, 0))

    def weight(n):
        return pl.BlockSpec((None, D_MODEL, D_MODEL), lambda i: (n, 0, 0), pipeline_mode=pl.Buffered(1))

    pair_shape = jax.ShapeDtypeStruct((2, t, D_MODEL), BF16)
    return pl.pallas_call(
        body, name="gate_and_branch_backward", grid=(t // tm,),
        in_specs=[pl.BlockSpec((tm, D_MODEL), lambda i: (i, 0)), pair, gates, vec, weight(0), weight(1)],
        out_specs=(pair, gates, vec, pair),
        out_shape=(pair_shape, jax.ShapeDtypeStruct((N_SLOTS, t, D_MODEL), BF16),
                   jax.ShapeDtypeStruct((2, 1, D_MODEL), F32), pair_shape),
        compiler_params=_params("arbitrary"),
    )(dmerged, oab, p, bg, wo3, wo3)


def _weight_grad(a, a_sel, b, b_sel, tm, tn, tk, name, exchange=None):
    t = a.shape[-2]
    nk = t // tk
    ni, nj = D_MODEL // tm, D_MODEL // tn

    def body(a_ref, b_ref, *rest):
        if exchange is None:
            o_ref, acc_ref = rest
        else:
            g_ref, o_ref, r_ref, acc_ref, send_sem, recv_sem = rest
        i, j, k = pl.program_id(0), pl.program_id(1), pl.program_id(2)

        if exchange is not None:
            x, y, c, _ = _mesh_pos()
            hd = exchange.shape[0] // 2
            swap = pltpu.make_async_remote_copy(src_ref=g_ref.at[pl.ds((1 - c) * hd, hd), :], dst_ref=r_ref,
                                                send_sem=send_sem, recv_sem=recv_sem,
                                                device_id=(x, y, 1 - c), device_id_type=MESH)

            @pl.when(jnp.logical_and(i == 0, jnp.logical_and(j == 0, k == 0)))
            def _():
                swap.start()

        @pl.when(k == 0)
        def _():
            acc_ref[...] = jnp.zeros_like(acc_ref)

        acc_ref[...] += _dot(a_ref[...], b_ref[...], 0, 0)

        @pl.when(k == nk - 1)
        def _():
            o_ref[...] = acc_ref[...].astype(BF16)

        if exchange is not None:
            @pl.when(jnp.logical_and(i == ni - 1, jnp.logical_and(j == nj - 1, k == nk - 1)))
            def _():
                swap.wait()

    def spec(arr, sel, width, which):
        if arr.ndim == 2:
            return pl.BlockSpec((tk, width), lambda i, j, k: (k, (i, j)[which]))
        return pl.BlockSpec((None, tk, width), lambda i, j, k: (sel, k, (i, j)[which]))

    o_spec = pl.BlockSpec((tm, tn), lambda i, j, k: (i, j))
    o_shape = jax.ShapeDtypeStruct((D_MODEL, D_MODEL), BF16)
    if exchange is None:
        return pl.pallas_call(
            body, name=name, grid=(ni, nj, nk),
            in_specs=[spec(a, a_sel, tm, 0), spec(b, b_sel, tn, 1)],
            out_specs=o_spec, out_shape=o_shape,
            scratch_shapes=[pltpu.VMEM((tm, tn), F32)],
            compiler_params=_params("parallel", "parallel", "arbitrary"),
        )(a, b)
    return pl.pallas_call(
        body, name=name, grid=(ni, nj, nk),
        in_specs=[spec(a, a_sel, tm, 0), spec(b, b_sel, tn, 1), ANY],
        out_specs=(o_spec, ANY),
        out_shape=(o_shape, jax.ShapeDtypeStruct((exchange.shape[0] // 2, exchange.shape[1]), BF16)),
        scratch_shapes=[pltpu.VMEM((tm, tn), F32), pltpu.SemaphoreType.DMA, pltpu.SemaphoreType.DMA],
        compiler_params=pltpu.CompilerParams(dimension_semantics=("arbitrary",) * 3, vmem_limit_bytes=VMEM_LIMIT,
                                             has_side_effects=True),
    )(a, b, exchange)


def _win_grad(xt, dp, tn, tk):
    _, t, _ = dp.shape
    nk = t // tk
    per_seg = D_MODEL // tn
    nj = N_SEG * per_seg

    def body(x_ref, dp_ref, o_ref, acc_ref):
        k = pl.program_id(1)

        @pl.when(k == 0)
        def _():
            acc_ref[...] = jnp.zeros_like(acc_ref)

        acc_ref[...] += _dot(x_ref[...], dp_ref[...])

        @pl.when(k == nk - 1)
        def _():
            o_ref[...] = acc_ref[...].astype(BF16)

    return pl.pallas_call(
        body, name="grad_w_in", grid=(nj, nk),
        in_specs=[pl.BlockSpec((D_MODEL, tk), lambda j, k: (0, k)),
                  pl.BlockSpec((None, tk, tn), lambda j, k: (_slot_of_seg(j // per_seg), k, j % per_seg))],
        out_specs=pl.BlockSpec((D_MODEL, tn), lambda j, k: (0, j)),
        out_shape=jax.ShapeDtypeStruct((D_MODEL, N_SEG * D_MODEL), BF16),
        scratch_shapes=[pltpu.VMEM((D_MODEL, tn), F32)],
        compiler_params=_params("parallel", "arbitrary"),
    )(xt, dp)


def _input_grad_and_scatter(dp, w_full, gx, s_in, s_o, small, tm):
    _, t, _ = dp.shape
    ni, nk = t // tm, N_SEG - TAIL_SEGS
    _, hd, sc = s_in.shape
    hr = s_o.shape[2]

    def body(dp_ref, w_ref, gx_ref, si_ref, so_ref, sm_ref, o_ref, ri_ref, ro_ref, ga_ref,
             send_sems, recv_sems, local_sem):
        i, k = pl.program_id(0), pl.program_id(1)
        x, y, c, chips = _mesh_pos()
        me = 2 * x + y
        dev = 4 * x + 2 * y + c

        def peer(r):
            return (x ^ ((r >> 2) & 1), y ^ ((r >> 1) & 1), c ^ (r & 1))

        def sends():
            cps = []
            for q, (cx, cy) in enumerate(chips):
                dest = 2 * cx + cy
                cps.append(pltpu.make_async_remote_copy(src_ref=si_ref.at[dest], dst_ref=ri_ref.at[me],
                                                        send_sem=send_sems.at[q], recv_sem=recv_sems.at[q],
                                                        device_id=(cx, cy, c), device_id_type=MESH))
                cps.append(pltpu.make_async_remote_copy(src_ref=so_ref.at[:, dest], dst_ref=ro_ref.at[me],
                                                        send_sem=send_sems.at[3 + q], recv_sem=recv_sems.at[3 + q],
                                                        device_id=(cx, cy, c), device_id_type=MESH))
            for r in range(1, 8):
                cps.append(pltpu.make_async_remote_copy(src_ref=sm_ref, dst_ref=ga_ref.at[dev],
                                                        send_sem=send_sems.at[5 + r], recv_sem=recv_sems.at[5 + r],
                                                        device_id=peer(r), device_id_type=MESH))
            return cps

        own_small = pltpu.make_async_copy(sm_ref, ga_ref.at[dev], local_sem)

        @pl.when(jnp.logical_and(i == 0, k == 0))
        def _():
            for cp in sends():
                cp.start()
            own_small.start()

        @pl.when(k == 0)
        def _():
            o_ref[...] = gx_ref[...]

        o_ref[...] += _dot(dp_ref[...], w_ref[...], 1, 1)

        @pl.when(jnp.logical_and(i == ni - 1, k == nk - 1))
        def _():
            for q, (cx, cy) in enumerate(chips):
                frm = 2 * cx + cy
                pltpu.make_async_remote_copy(src_ref=si_ref.at[frm], dst_ref=ri_ref.at[frm], send_sem=send_sems.at[q],
                                             recv_sem=recv_sems.at[q], device_id=(x, y, c),
                                             device_id_type=MESH).wait_recv()
                pltpu.make_async_remote_copy(src_ref=so_ref.at[:, frm], dst_ref=ro_ref.at[frm],
                                             send_sem=send_sems.at[3 + q], recv_sem=recv_sems.at[3 + q],
                                             device_id=(x, y, c), device_id_type=MESH).wait_recv()
            for r in range(1, 8):
                px, py, pc = peer(r)
                pltpu.make_async_remote_copy(src_ref=sm_ref, dst_ref=ga_ref.at[4 * px + 2 * py + pc],
                                             send_sem=send_sems.at[5 + r], recv_sem=recv_sems.at[5 + r],
                                             device_id=(x, y, c), device_id_type=MESH).wait_recv()
            for cp in sends():
                cp.wait_send()
            own_small.wait()

    return pl.pallas_call(
        body, name="grad_x_and_scatter", grid=(ni, nk),
        in_specs=[pl.BlockSpec((None, tm, D_MODEL), lambda i, k: (_slot_of_seg(k), i, 0)),
                  pl.BlockSpec((D_MODEL, D_MODEL), lambda i, k: (0, k)),
                  pl.BlockSpec((tm, D_MODEL), lambda i, k: (i, 0), pipeline_mode=pl.Buffered(1)), ANY, ANY, ANY],
        out_specs=(pl.BlockSpec((tm, D_MODEL), lambda i, k: (i, 0)), ANY, ANY, ANY),
        out_shape=(jax.ShapeDtypeStruct((t, D_MODEL), F32),
                   jax.ShapeDtypeStruct((N_CHIPS, hd, sc), BF16),
                   jax.ShapeDtypeStruct((N_CHIPS, 3, hr, D_MODEL), BF16),
                   jax.ShapeDtypeStruct((8,) + small.shape, small.dtype)),
        scratch_shapes=[pltpu.SemaphoreType.DMA((13,)), pltpu.SemaphoreType.DMA((13,)), pltpu.SemaphoreType.DMA],
        input_output_aliases={2: 0},
        compiler_params=pltpu.CompilerParams(dimension_semantics=("arbitrary", "arbitrary"),
                                             vmem_limit_bytes=VMEM_LIMIT, has_side_effects=True),
    )(dp, w_full, gx, s_in, s_o, small)


TAIL_SEGS = 1


def _input_grad_tail_and_adam(dp, w_full, partial, w, g, m, v, steps):
    _, t, _ = dp.shape
    tm = t // steps
    r, c = w.shape
    ra = r // steps
    segs = tuple(range(N_SEG - TAIL_SEGS, N_SEG))

    def body(*refs):
        dp_refs = refs[:TAIL_SEGS]
        w_refs = refs[TAIL_SEGS:2 * TAIL_SEGS]
        part_ref, aw_ref, ag_ref, am_ref, av_ref, o_ref, go_ref, d_ref, mo_ref, vo_ref = refs[2 * TAIL_SEGS:]
        acc = part_ref[...]
        for dp_ref, w_ref in zip(dp_refs, w_refs):
            acc = acc + _dot(dp_ref[...], w_ref[...], 1, 1)
        o_ref[...] = acc
        gv = ag_ref[...]
        d, mn, vn = _adam_math(aw_ref[...], gv, am_ref[...], av_ref[...])
        go_ref[...] = gv
        d_ref[...] = d
        mo_ref[...] = mn
        vo_ref[...] = vn

    row = pl.BlockSpec((tm, D_MODEL), lambda i: (i, 0))
    arow = pl.BlockSpec((ra, c), lambda i: (i, 0))
    ashape = jax.ShapeDtypeStruct((r, c), F32)
    in_specs = [pl.BlockSpec((None, tm, D_MODEL), functools.partial(lambda s, i: (SLOT_OF_SEG[s], i, 0), s))
                for s in segs]
    in_specs += [pl.BlockSpec((D_MODEL, D_MODEL), functools.partial(lambda s, i: (0, s), s),
                              pipeline_mode=pl.Buffered(1)) for s in segs]
    in_specs += [row, arow, arow, arow, arow]
    res = pl.pallas_call(
        body, name="grad_x_tail_and_adam_w_in", grid=(steps,),
        in_specs=in_specs,
        out_specs=(row, arow, arow, arow, arow),
        out_shape=(jax.ShapeDtypeStruct((t, D_MODEL), F32), ashape, ashape, ashape, ashape),
        compiler_params=_params("parallel"),
    )(*([dp] * TAIL_SEGS), *([w_full] * TAIL_SEGS), partial, w, g, m, v)
    return res[0], res[1:]


def _sgu_chunk_forward(u, v, z, wm, bs, lng, lnb):
    ug, dug = _gelu_and_grad(u)
    vg, dvg = _gelu_and_grad(v)
    mu = jnp.mean(vg, axis=-1, keepdims=True)
    xc = vg - mu
    var = jnp.mean(xc * xc, axis=-1, keepdims=True)
    rstd = lax.rsqrt(var + LN_EPS)
    vhat = xc * rstd
    vln = (vhat * lng + lnb).astype(BF16)
    mixed = _dot(wm, vln) + bs
    sig = _sigmoid(z)
    return ug, dug, dvg, rstd, vhat, vln, mixed, sig


def _mixer_a_forward(p_a, wm, bs_col, ln_v_g, ln_v_b, tm):
    t = p_a.shape[1]

    def body(p_ref, wm_ref, bs_ref, g_ref, b_ref, o_ref):
        wm_v, bs_v, lng, lnb = wm_ref[...], bs_ref[...], g_ref[...], b_ref[...]

        def chunk(ci, carry):
            rows = pl.ds(pl.multiple_of(ci * CHUNK, CHUNK), CHUNK)
            u = p_ref[0, rows, :].astype(F32)
            v = p_ref[1, rows, :].astype(F32)
            z = p_ref[2, rows, :].astype(F32)
            ug, _, _, _, _, _, mixed, sig = _sgu_chunk_forward(u, v, z, wm_v, bs_v, lng, lnb)
            o_ref[rows, :] = (ug * mixed * (z * sig)).astype(BF16)
            return carry

        lax.fori_loop(0, tm // CHUNK, chunk, 0, unroll=True)

    return pl.pallas_call(
        body, name="mixer_a_forward", grid=(t // tm, N_HEADS),
        in_specs=[pl.BlockSpec((3, tm, HEAD_DIM), lambda i, h: (0, i, h)),
                  pl.BlockSpec((None, CHUNK, CHUNK), lambda i, h: (h, 0, 0)),
                  pl.BlockSpec((None, CHUNK, 1), lambda i, h: (h, 0, 0)),
                  pl.BlockSpec((1, HEAD_DIM), lambda i, h: (0, h)),
                  pl.BlockSpec((1, HEAD_DIM), lambda i, h: (0, h))],
        out_specs=pl.BlockSpec((tm, HEAD_DIM), lambda i, h: (i, h)),
        out_shape=jax.ShapeDtypeStruct((t, D_MODEL), BF16),
        compiler_params=_params("parallel", "parallel"),
    )(p_a, wm, bs_col, ln_v_g, ln_v_b)


def _mixer_a_backward(p_a, dyab, wm, bs_col, ln_v_g, ln_v_b, dp, tm):
    t = p_a.shape[1]

    def body(p_ref, dy_ref, wm_ref, bs_ref, g_ref, b_ref, dp_in, dp_ref, dws_ref, dbs_ref, dg_ref, db_ref):
        @pl.when(pl.program_id(1) == 0)
        def _():
            dws_ref[...] = jnp.zeros_like(dws_ref)
            dbs_ref[...] = jnp.zeros_like(dbs_ref)
            dg_ref[...] = jnp.zeros_like(dg_ref)
            db_ref[...] = jnp.zeros_like(db_ref)

        wm_v, bs_v, lng, lnb = wm_ref[...], bs_ref[...], g_ref[...], b_ref[...]
        causal = (lax.broadcasted_iota(jnp.int32, (CHUNK, CHUNK), 1)
                  <= lax.broadcasted_iota(jnp.int32, (CHUNK, CHUNK), 0))

        def chunk(ci, carry):
            rows = pl.ds(pl.multiple_of(ci * CHUNK, CHUNK), CHUNK)
            u = p_ref[0, rows, :].astype(F32)
            v = p_ref[1, rows, :].astype(F32)
            z = p_ref[2, rows, :].astype(F32)
            dy = dy_ref[rows, :].astype(F32)
            ug, dug, dvg, rstd, vhat, vln, mixed, sig = _sgu_chunk_forward(u, v, z, wm_v, bs_v, lng, lnb)
            sz = z * sig
            dmixed = dy * ug * sz
            dp_ref[0, rows, :] = (dy * mixed * sz * dug).astype(BF16)
            dp_ref[2, rows, :] = (dy * ug * mixed * _silu_grad(sig, sz)).astype(BF16)
            dbs_ref[...] += jnp.sum(dmixed, axis=1, keepdims=True)
            dmb = dmixed.astype(BF16)
            dws_ref[...] += jnp.where(causal, _dot(dmb, vln, 1, 1), 0.0)
            dvln = _dot(wm_v, dmb, 0, 0)
            db_ref[...] += jnp.sum(dvln, axis=0, keepdims=True)
            dg_ref[...] += jnp.sum(dvln * vhat, axis=0, keepdims=True)
            dvh = dvln * lng
            m1 = jnp.mean(dvh, axis=-1, keepdims=True)
            m2 = jnp.mean(dvh * vhat, axis=-1, keepdims=True)
            dp_ref[1, rows, :] = (rstd * (dvh - m1 - vhat * m2) * dvg).astype(BF16)
            return carry

        lax.fori_loop(0, tm // CHUNK, chunk, 0, unroll=True)

    return pl.pallas_call(
        body, name="mixer_a_backward", grid=(N_HEADS, t // tm),
        in_specs=[pl.BlockSpec((3, tm, HEAD_DIM), lambda h, i: (0, i, h)),
                  pl.BlockSpec((None, tm, HEAD_DIM), lambda h, i: (0, i, h)),
                  pl.BlockSpec((None, CHUNK, CHUNK), lambda h, i: (h, 0, 0)),
                  pl.BlockSpec((None, CHUNK, 1), lambda h, i: (h, 0, 0)),
                  pl.BlockSpec((1, HEAD_DIM), lambda h, i: (0, h)),
                  pl.BlockSpec((1, HEAD_DIM), lambda h, i: (0, h)), ANY],
        out_specs=(pl.BlockSpec((3, tm, HEAD_DIM), lambda h, i: (BLOCK_A, i, h)),
                   pl.BlockSpec((None, CHUNK, CHUNK), lambda h, i: (h, 0, 0)),
                   pl.BlockSpec((None, CHUNK, 1), lambda h, i: (h, 0, 0)),
                   pl.BlockSpec((1, HEAD_DIM), lambda h, i: (0, h)),
                   pl.BlockSpec((1, HEAD_DIM), lambda h, i: (0, h))),
        out_shape=(jax.ShapeDtypeStruct(dp.shape, BF16),
                   jax.ShapeDtypeStruct((N_HEADS, CHUNK, CHUNK), F32),
                   jax.ShapeDtypeStruct((N_HEADS, CHUNK, 1), F32),
                   jax.ShapeDtypeStruct((1, D_MODEL), F32), jax.ShapeDtypeStruct((1, D_MODEL), F32)),
        input_output_aliases={6: 0},
        compiler_params=_params("parallel", "arbitrary"),
    )(p_a, dyab, wm, bs_col, ln_v_g, ln_v_b, dp)


HALO = 16


def _conv_taps(h, halo_h, tm):
    row = lax.broadcasted_iota(jnp.int32, h.shape, 0)
    last1 = halo_h[HALO - 1:HALO, :]
    last2 = halo_h[HALO - 2:HALO - 1, :]
    h1 = jnp.where(row == 0, last1, pltpu.roll(h, 1, 0))
    h2 = jnp.where(row == 0, last2, jnp.where(row == 1, last1, pltpu.roll(h, 2, 0)))
    return h1, h2


def _mixer_b_forward(p_b, conv_w, conv_b, tm, tc):
    t = p_b.shape[1]

    def body(p_ref, halo_ref, w_ref, b_ref, o_ref):
        valid = (pl.program_id(1) > 0).astype(F32)
        h = p_ref[1].astype(F32) * p_ref[0].astype(F32)
        halo_h = halo_ref[1].astype(F32) * halo_ref[0].astype(F32) * valid
        h1, h2 = _conv_taps(h, halo_h, tm)
        w = w_ref[...]
        conv = b_ref[...] + w[0:1, :] * h2 + w[1:2, :] * h1 + w[2:3, :] * h
        z = p_ref[3].astype(F32)
        o_ref[...] = (p_ref[2].astype(F32) * conv * (z * _sigmoid(z))).astype(BF16)

    steps = tm // HALO
    return pl.pallas_call(
        body, name="mixer_b_forward", grid=(D_MODEL // tc, t // tm),
        in_specs=[pl.BlockSpec((4, tm, tc), lambda j, i: (BLOCK_B, i, j)),
                  pl.BlockSpec((4, HALO, tc), lambda j, i: (BLOCK_B, jnp.maximum(i * steps - 1, 0), j)),
                  pl.BlockSpec((3, tc), lambda j, i: (0, j)),
                  pl.BlockSpec((1, tc), lambda j, i: (0, j))],
        out_specs=pl.BlockSpec((tm, tc), lambda j, i: (i, j)),
        out_shape=jax.ShapeDtypeStruct((t, D_MODEL), BF16),
        compiler_params=_params("parallel", "parallel"),
    )(p_b, p_b, conv_w, conv_b)


def _mixer_b_backward(p_b, dyab, conv_w, conv_b, dp, tm, tc):
    t = p_b.shape[1]
    n = t // tm

    def body(p_ref, halo_ref, dy_ref, w_ref, b_ref, dp_in, dp_ref, dw_ref, db_ref, next_ref):
        ii = pl.program_id(1)

        @pl.when(ii == 0)
        def _():
            dw_ref[...] = jnp.zeros_like(dw_ref)
            db_ref[...] = jnp.zeros_like(db_ref)
            next_ref[...] = jnp.zeros_like(next_ref)

        valid = (ii < n - 1).astype(F32)
        xb = p_ref[0].astype(F32)
        cb = p_ref[1].astype(F32)
        bb = p_ref[2].astype(F32)
        z = p_ref[3].astype(F32)
        h = cb * xb
        halo_h = halo_ref[1].astype(F32) * halo_ref[0].astype(F32) * valid
        h1, h2 = _conv_taps(h, halo_h, tm)
        w = w_ref[...]
        w0, w1, w2 = w[0:1, :], w[1:2, :], w[2:3, :]
        conv = b_ref[...] + w0 * h2 + w1 * h1 + w2 * h
        sig = _sigmoid(z)
        sz = z * sig
        dy = dy_ref[...].astype(F32)
        dconv = dy * bb * sz
        dp_ref[2] = (dy * conv * sz).astype(BF16)
        dp_ref[3] = (dy * bb * conv * _silu_grad(sig, sz)).astype(BF16)
        db_ref[...] += jnp.sum(dconv, axis=0, keepdims=True)
        dw_ref[0:1, :] += jnp.sum(dconv * h2, axis=0, keepdims=True)
        dw_ref[1:2, :] += jnp.sum(dconv * h1, axis=0, keepdims=True)
        dw_ref[2:3, :] += jnp.sum(dconv * h, axis=0, keepdims=True)
        row = lax.broadcasted_iota(jnp.int32, h.shape, 0)
        nxt = next_ref[...]
        n0, n1 = nxt[0:1, :], nxt[1:2, :]
        d1 = jnp.where(row == tm - 1, n0, pltpu.roll(dconv, tm - 1, 0))
        d2 = jnp.where(row == tm - 1, n1, jnp.where(row == tm - 2, n0, pltpu.roll(dconv, tm - 2, 0)))
        dh = w2 * dconv + w1 * d1 + w0 * d2
        dp_ref[0] = (dh * cb).astype(BF16)
        dp_ref[1] = (dh * xb).astype(BF16)
        next_ref[...] = dconv[0:8, :]

    steps = tm // HALO
    return pl.pallas_call(
        body, name="mixer_b_backward", grid=(D_MODEL // tc, n),
        in_specs=[pl.BlockSpec((4, tm, tc), lambda j, ii: (BLOCK_B, n - 1 - ii, j)),
                  pl.BlockSpec((4, HALO, tc), lambda j, ii: (BLOCK_B, jnp.maximum((n - 1 - ii) * steps - 1, 0), j)),
                  pl.BlockSpec((None, tm, tc), lambda j, ii: (1, n - 1 - ii, j)),
                  pl.BlockSpec((3, tc), lambda j, ii: (0, j)),
                  pl.BlockSpec((1, tc), lambda j, ii: (0, j)), ANY],
        out_specs=(pl.BlockSpec((4, tm, tc), lambda j, ii: (BLOCK_B, n - 1 - ii, j)),
                   pl.BlockSpec((3, tc), lambda j, ii: (0, j)),
                   pl.BlockSpec((1, tc), lambda j, ii: (0, j))),
        out_shape=(jax.ShapeDtypeStruct(dp.shape, BF16),
                   jax.ShapeDtypeStruct((3, D_MODEL), F32), jax.ShapeDtypeStruct((1, D_MODEL), F32)),
        scratch_shapes=[pltpu.VMEM((8, tc), F32)],
        input_output_aliases={5: 0},
        compiler_params=_params("parallel", "arbitrary"),
    )(p_b, p_b, dyab, conv_w, conv_b, dp)


def _adam_math(w, g, m, v):
    m = ADAM_B1 * m + (1.0 - ADAM_B1) * g
    v = ADAM_B2 * v + (1.0 - ADAM_B2) * (g * g)
    delta = -ADAM_LR * ((m * ADAM_C1) / (jnp.sqrt(v * ADAM_C2) + ADAM_EPS) + ADAM_WD * w)
    return delta, m, v


def _adam_rows(w, g, m, v, tm, name, g_sel=None):
    r, c = w.shape

    def body(w_ref, g_ref, m_ref, v_ref, go_ref, d_ref, mo_ref, vo_ref):
        g = g_ref[...]
        d, mn, vn = _adam_math(w_ref[...], g, m_ref[...], v_ref[...])
        go_ref[...] = g
        d_ref[...] = d
        mo_ref[...] = mn
        vo_ref[...] = vn

    spec = pl.BlockSpec((tm, c), lambda i: (i, 0))
    g_spec = spec if g_sel is None else pl.BlockSpec((None, tm, c), lambda i: (g_sel, i, 0))
    shape = jax.ShapeDtypeStruct((r, c), F32)
    return pl.pallas_call(
        body, name=name, grid=(r // tm,),
        in_specs=[spec, g_spec, spec, spec], out_specs=(spec,) * 4, out_shape=(shape,) * 4,
        compiler_params=_params("parallel"),
    )(w, g, m, v)


SMALL_ROW0 = {name: sum(r for _, r in SMALL_ROWS[:i]) for i, (name, _) in enumerate(SMALL_ROWS)}
LANE_MAJOR = ("ln_g", "ln_b", "b_gate", "ln_v_g", "ln_v_b", "conv_b")


def _lane_pieces(n):
    return [(q, slice(q * 128, (q + 1) * 128)) for q in range(n // 128)]


def _pack_small(d_ln_g, d_ln_b, d_bg, d_lnv_g, d_lnv_b, d_ws, d_bs, d_cw, d_cb, loss_part):
    def body(lg, lb, bg, vg, vb, ws, bs, cw, cb, loss, o_ref):
        def put(row0, vec):
            for q, cols in _lane_pieces(vec.shape[1]):
                o_ref[row0 + q:row0 + q + 1, :] = vec[:, cols]

        put(SMALL_ROW0["ln_g"], lg[...])
        put(SMALL_ROW0["ln_b"], lb[...])
        for n in range(2):
            put(SMALL_ROW0["b_gate"] + n * (D_MODEL // 128), bg[n])
        put(SMALL_ROW0["ln_v_g"], vg[...])
        put(SMALL_ROW0["ln_v_b"], vb[...])
        for h in range(N_HEADS):
            o_ref[SMALL_ROW0["w_s"] + h * CHUNK:SMALL_ROW0["w_s"] + (h + 1) * CHUNK, :] = ws[h]
        o_ref[SMALL_ROW0["b_s"]:SMALL_ROW0["b_s"] + N_HEADS, :] = bs[...]
        for c in range(3):
            put(SMALL_ROW0["conv_w"] + c * (D_MODEL // 128), cw[c:c + 1, :])
        put(SMALL_ROW0["conv_b"], cb[...])
        o_ref[SMALL_ROW0["loss"]:SMALL_ROW0["loss"] + 8, :] = jnp.broadcast_to(loss[...], (8, 128))

    return pl.pallas_call(
        body, name="pack_small", out_shape=jax.ShapeDtypeStruct((SMALL_TOTAL, 128), F32), compiler_params=_params(),
    )(d_ln_g, d_ln_b, d_bg, d_lnv_g, d_lnv_b, d_ws, d_bs.reshape(N_HEADS, CHUNK), d_cw, d_cb, loss_part)


def _adam_small(gathered, params):
    names = list(params)
    flat = [a for n in names for a in params[n]]

    def body(*refs):
        ga_ref = refs[0]
        ins = refs[1:1 + 3 * len(names)]
        outs = refs[1 + 3 * len(names):-1]
        gs_ref = refs[-1]
        g = ga_ref[0]
        for k in range(1, 8):
            g = g + ga_ref[k]
        gs_ref[...] = g
        for i, name in enumerate(names):
            w_ref, m_ref, v_ref = ins[3 * i:3 * i + 3]
            o_refs = outs[4 * i:4 * i + 4]
            row0 = SMALL_ROW0[name]
            if name in LANE_MAJOR:
                pieces = [((slice(None), cols), slice(row0 + q, row0 + q + 1))
                          for q, cols in _lane_pieces(w_ref.shape[1])]
            elif name == "w_s":
                pieces = [((0, h), slice(row0 + h * CHUNK, row0 + (h + 1) * CHUNK)) for h in range(N_HEADS)]
            else:
                pieces = [((0,), slice(row0, row0 + N_HEADS))]
            for idx, rows in pieces:
                gp = gs_ref[rows, :]
                res = (gp,) + _adam_math(w_ref[idx], gp, m_ref[idx], v_ref[idx])
                for o_ref, val in zip(o_refs, res):
                    o_ref[idx] = val
        gcw_ref, loss_ref = outs[-2:]
        for c in range(3):
            for q, cols in _lane_pieces(D_MODEL):
                r = SMALL_ROW0["conv_w"] + c * (D_MODEL // 128) + q
                gcw_ref[c:c + 1, cols] = gs_ref[r:r + 1, :]
        loss_ref[...] = gs_ref[SMALL_ROW0["loss"]:SMALL_ROW0["loss"] + 1, :]

    out_shape = [jax.ShapeDtypeStruct(params[n][0].shape, F32) for n in names for _ in range(4)]
    out_shape += [jax.ShapeDtypeStruct((3, D_MODEL), F32), jax.ShapeDtypeStruct((1, 128), F32)]
    res = pl.pallas_call(
        body, name="adam_small", out_shape=tuple(out_shape),
        scratch_shapes=[pltpu.VMEM((SMALL_TOTAL, 128), F32)], compiler_params=_params(),
    )(gathered, *flat)
    return {n: res[4 * i:4 * i + 4] for i, n in enumerate(names)}, res[-2], res[-1]


def _adam_conv_w(g_all, chip, w, m, v):
    cols = w.shape[2]

    def body(c_ref, g_ref, w_ref, m_ref, v_ref, go_ref, d_ref, mo_ref, vo_ref):
        g = g_ref[...]
        d, mn, vn = _adam_math(w_ref[...], g, m_ref[...], v_ref[...])
        go_ref[...] = g
        d_ref[...] = d
        mo_ref[...] = mn
        vo_ref[...] = vn

    own = pl.BlockSpec((None, 3, cols), lambda i, c_ref: (0, 0, 0))
    return pl.pallas_call(
        body, name="adam_conv_w",
        grid_spec=pltpu.PrefetchScalarGridSpec(
            num_scalar_prefetch=1, grid=(1,),
            in_specs=[pl.BlockSpec((3, cols), lambda i, c_ref: (0, c_ref[0])), own, own, own],
            out_specs=(own,) * 4),
        out_shape=(jax.ShapeDtypeStruct(w.shape, F32),) * 4,
        compiler_params=_params("arbitrary"),
    )(chip, g_all, w, m, v)


def kernel(x, w_in, b_gate, ln_v_g, ln_v_b, w_s, b_s, conv_w, conv_b, w_oa, w_ob, w_out, ln_g, ln_b, loss_target, m_w_in, m_b_gate, m_ln_v_g, m_ln_v_b, m_w_s, m_b_s, m_conv_w, m_conv_b, m_w_oa, m_w_ob, m_w_out, m_ln_g, m_ln_b, v_w_in, v_b_gate, v_ln_v_g, v_ln_v_b, v_w_s, v_b_s, v_conv_w, v_conv_b, v_w_oa, v_w_ob, v_w_out, v_ln_g, v_ln_b):
    t = x.shape[1]
    x2 = x[0]
    target = loss_target[0]
    chip = 2 * lax.axis_index("x") + lax.axis_index("y")
    conv_cols = conv_w.shape[2]

    chip1 = chip.astype(jnp.int32).reshape(1)
    w_pre = _cast_into_columns(w_in[0], chip1, N_CHIPS, 256, "cast_w_in")
    wo_b, wm = _prep_small_weights(w_oa[0], w_ob[0], w_out[0], w_s[0])
    conv_w8 = jnp.concatenate([conv_w[0], jnp.zeros((5, conv_cols), F32)], axis=0)
    bs_col = b_s[0].reshape(N_HEADS, CHUNK, 1)
    bg = b_gate.reshape(2, 1, D_MODEL)

    xb, xt = _cast_and_transpose(x2, min(512, t))
    p, w_full, wo_full, cw_full = _gather_and_project(xb, w_pre, wo_b, conv_w8, min(4096, t))
    wo3 = wo_full.reshape(3, D_MODEL, D_MODEL)
    conv_w_all = jnp.transpose(cw_full[:, :3, :], (1, 0, 2)).reshape(3, D_MODEL)
    tm_a = min(2048, t)
    ya = _mixer_a_forward(p, wm, bs_col, ln_v_g, ln_v_b, tm_a)
    tm_b = min(1024, t)
    yb = _mixer_b_forward(p, conv_w_all, conv_b, tm_b, 512)
    tm_m = min(1024, t)
    merged, oab = _merge_forward(ya, yb, wo3, p, bg, tm_m, 512)

    drb, gx, dmerged, d_ln_g, d_ln_b, loss_part = _head(merged, wo3, x2, target, ln_g, ln_b, min(512, t))
    doab, dp, d_bg, dyab = _gate_and_branch_backward(dmerged, oab, p, bg, wo3, min(256, t))
    dp, d_ws, d_bs, d_lnv_g, d_lnv_b = _mixer_a_backward(p, dyab, wm, bs_col, ln_v_g, ln_v_b, dp, min(1024, t))
    dp, d_cw, d_cb = _mixer_b_backward(p, dyab, conv_w_all, conv_b, dp, tm_b, 512)

    tk = min(2048, t)
    g_in = _win_grad(xt, dp, 1024, tk)
    g_oa = _weight_grad(ya, 0, doab, 0, 1024, 1024, tk, "grad_w_oa")
    g_ob = _weight_grad(yb, 0, doab, 1, 1024, 1024, tk, "grad_w_ob")
    g_out, r_in = _weight_grad(merged, 0, drb, 0, 1024, 1024, tk, "grad_w_out", exchange=g_in)
    r_o = _exchange_halves((g_oa, g_ob, g_out))
    s_in, s_o = _add_halves(g_in, (g_oa, g_ob, g_out), r_in, r_o)
    small_part = _pack_small(d_ln_g, d_ln_b, d_bg, d_lnv_g, d_lnv_b, d_ws, d_bs, d_cw, d_cb, loss_part)
    gx, q_in, q_o, gathered = _input_grad_and_scatter(dp, w_full, gx, s_in, s_o, small_part, min(1024, t))
    f_in, f_o = _sum_chips(q_in, q_o, s_in, s_o)
    gsum_in, gsum_o = _share_halves(f_in, f_o)

    big = {}
    grad_x, big["w_in"] = _input_grad_tail_and_adam(dp, w_full, gx, w_in[0], gsum_in, m_w_in[0], v_w_in[0],
                                                    min(32, t // 128))
    for n, (name, w, m, v) in enumerate((("w_oa", w_oa, m_w_oa, v_w_oa), ("w_ob", w_ob, m_w_ob, v_w_ob),
                                         ("w_out", w_out, m_w_out, v_w_out))):
        big[name] = _adam_rows(w[0], gsum_o, m[0], v[0], 256, "adam_" + name, g_sel=n)

    small, g_conv_w, loss_row = _adam_small(gathered, {
        "ln_g": (ln_g, m_ln_g, v_ln_g), "ln_b": (ln_b, m_ln_b, v_ln_b), "b_gate": (b_gate, m_b_gate, v_b_gate),
        "ln_v_g": (ln_v_g, m_ln_v_g, v_ln_v_g), "ln_v_b": (ln_v_b, m_ln_v_b, v_ln_v_b),
        "w_s": (w_s, m_w_s, v_w_s), "b_s": (b_s, m_b_s, v_b_s), "conv_b": (conv_b, m_conv_b, v_conv_b)})
    small["conv_w"] = _adam_conv_w(g_conv_w, chip1, conv_w, m_conv_w, v_conv_w)
    loss = loss_row[0, 0]

    order = ("w_in", "b_gate", "ln_v_g", "ln_v_b", "w_s", "b_s", "conv_w", "conv_b", "w_oa", "w_ob", "w_out",
             "ln_g", "ln_b")
    outs = [loss, grad_x[None]]
    for which in range(4):
        for name in order:
            outs.append(big[name][which][None] if name in big else small[name][which])
    return tuple(outs)
```

```python
import functools
import math

import jax
import jax.numpy as jnp
from jax import lax
from jax.experimental import pallas as pl
from jax.experimental.pallas import tpu as pltpu

F32 = jnp.float32
BF16 = jnp.bfloat16

D_MODEL = 2048
N_HEADS = 8
HEAD_DIM = D_MODEL // N_HEADS
CHUNK = 128
N_SEG = 9
N_CHIPS = 4
SHARD_COLS = N_SEG * D_MODEL // N_CHIPS
COL_BLOCK = 512
BLOCKS_PER_SHARD = SHARD_COLS // COL_BLOCK
BLOCKS_PER_SEG = D_MODEL // COL_BLOCK
SUBS = 3
SHARD_ROWS = D_MODEL // N_CHIPS
N_SLOTS = 12
BLOCK_A, BLOCK_G, BLOCK_B = 0, 2, 2
SLOT_OF_SEG = (0, 1, 2, 8, 9, 10, 11, 4, 5)
DN_ALPHA = 2.0 ** 0.25
LN_EPS = 1e-5
GELU_K = math.sqrt(2.0 / math.pi)
GELU_C = 0.044715

ADAM_LR = 0.001
ADAM_B1 = 0.9
ADAM_B2 = 0.999
ADAM_EPS = 1e-08
ADAM_WD = 0.01
ADAM_STEP = 10
ADAM_C1 = 1.0 / (1.0 - ADAM_B1 ** ADAM_STEP)
ADAM_C2 = 1.0 / (1.0 - ADAM_B2 ** ADAM_STEP)

VMEM_LIMIT = 60 * 1024 * 1024
MESH = pl.DeviceIdType.MESH
ANY = pl.BlockSpec(memory_space=pl.ANY)

SMALL_ROWS = (("ln_g", 16), ("ln_b", 16), ("b_gate", 32), ("ln_v_g", 16), ("ln_v_b", 16),
              ("w_s", 1024), ("b_s", 8), ("conv_w", 48), ("conv_b", 16), ("loss", 8))
SMALL_TOTAL = sum(r for _, r in SMALL_ROWS)


def _params(*sem):
    return pltpu.CompilerParams(dimension_semantics=sem, vmem_limit_bytes=VMEM_LIMIT)


def _sigmoid(x):
    return 1.0 / (1.0 + jnp.exp(-x))


def _gelu_gate(x, x2):
    return 1.0 / (1.0 + jnp.exp(x * ((-2.0 * GELU_K) + (-2.0 * GELU_K * GELU_C) * x2)))


def _gelu_and_grad(x):
    x2 = x * x
    s = _gelu_gate(x, x2)
    g = x * s
    dg = s + g * (1.0 - s) * ((2.0 * GELU_K) + (6.0 * GELU_K * GELU_C) * x2)
    return g, dg


def _silu_grad(sig, sz):
    return sig + sz * (1.0 - sig)


def _dot(a, b, ca=1, cb=0):
    return lax.dot_general(a, b, (((ca,), (cb,)), ((), ())), preferred_element_type=F32)


def _cast_and_transpose(x, tm):
    t, d = x.shape

    def body(x_ref, o_ref, ot_ref):
        v = x_ref[...]
        o_ref[...] = v.astype(BF16)
        ot_ref[...] = v.T.astype(BF16)

    return pl.pallas_call(
        body, name="cast_x", grid=(t // tm,),
        in_specs=[pl.BlockSpec((tm, d), lambda i: (i, 0))],
        out_specs=(pl.BlockSpec((tm, d), lambda i: (i, 0)), pl.BlockSpec((d, tm), lambda i: (0, i))),
        out_shape=(jax.ShapeDtypeStruct((t, d), BF16), jax.ShapeDtypeStruct((d, t), BF16)),
        compiler_params=_params("parallel"),
    )(x)


def _cast_into_columns(w, slot, n_slots, tm, name):
    r, c = w.shape

    def body(s_ref, w_ref, o_ref):
        o_ref[...] = w_ref[...].astype(BF16)

    return pl.pallas_call(
        body, name=name,
        grid_spec=pltpu.PrefetchScalarGridSpec(
            num_scalar_prefetch=1, grid=(r // tm,),
            in_specs=[pl.BlockSpec((tm, c), lambda i, s_ref: (i, 0))],
            out_specs=pl.BlockSpec((tm, c), lambda i, s_ref: (i, s_ref[0]))),
        out_shape=jax.ShapeDtypeStruct((r, n_slots * c), BF16),
        compiler_params=_params("parallel"),
    )(slot, w)


def _prep_small_weights(w_oa, w_ob, w_out, w_s):
    rows = w_oa.shape[0]

    def body(a_ref, b_ref, c_ref, ws_ref, wo_ref, wm_ref):
        wo_ref[0] = a_ref[...].astype(BF16)
        wo_ref[1] = b_ref[...].astype(BF16)
        wo_ref[2] = c_ref[...].astype(BF16)
        t = lax.broadcasted_iota(jnp.int32, (CHUNK, CHUNK), 0)
        s = lax.broadcasted_iota(jnp.int32, (CHUNK, CHUNK), 1)
        for h in range(N_HEADS):
            wm_ref[h] = jnp.where(s <= t, ws_ref[h], 0.0).astype(BF16)

    return pl.pallas_call(
        body, name="prep_small_weights",
        out_shape=(jax.ShapeDtypeStruct((3, rows, D_MODEL), BF16),
                   jax.ShapeDtypeStruct((N_HEADS, CHUNK, CHUNK), BF16)),
        compiler_params=_params(),
    )(w_oa, w_ob, w_out, w_s)


def _mesh_pos():
    x, y, c = lax.axis_index("x"), lax.axis_index("y"), lax.axis_index("c")
    chips = [(1 - x, y), (x, 1 - y), (1 - x, 1 - y)]
    return x, y, c, chips


def _slot_of_seg(seg):
    return jnp.where(seg < 3, seg, jnp.where(seg < 7, seg + 5, seg - 3))


def _gather_and_project(xb, w_pre, wo_b, conv_w8, tm):
    t = xb.shape[0]
    d, sc = w_pre.shape[0], w_pre.shape[1] // N_CHIPS
    rows = wo_b.shape[1]
    hd, hr = d // 2, rows // 2
    nj = BLOCKS_PER_SHARD // SUBS
    pc = nj * COL_BLOCK
    units = N_CHIPS * SUBS
    ni = t // tm
    total = units * ni * nj
    mx, my = lax.axis_index("x"), lax.axis_index("y")
    order = jnp.stack([2 * mx + my, 2 * (1 - mx) + my, 2 * mx + (1 - my),
                       2 * (1 - mx) + (1 - my)]).astype(jnp.int32)

    def body(order_ref, x_ref, wpre_ref, wo_ref, cw_ref, p_ref, wf_ref, wof_ref, cwf_ref,
             wbuf, wsem, xbuf, xsem, send_sems, recv_sems, local_sems):
        x, y, c, chips = _mesh_pos()
        me = 2 * x + y
        sibling = (x, y, 1 - c)
        u, i, j = pl.program_id(0), pl.program_id(1), pl.program_id(2)
        n = (u * ni + i) * nj + j
        m = u * ni + i

        def rows_start(m_):
            pltpu.make_async_copy(x_ref.at[pl.ds(pl.multiple_of(lax.rem(m_, ni) * tm, tm), tm), :],
                                  xbuf.at[lax.rem(m_, 2)], xsem.at[lax.rem(m_, 2)]).start()

        def chip_of(q):
            return 2 * chips[q][0] + chips[q][1]

        def piece(ref, k, cc, r):
            return ref.at[pl.ds(cc * hd, hd), pl.ds(pl.multiple_of(k * sc + r * pc, COL_BLOCK), pc)]

        def wo_half(ref4, k, cc):
            return ref4.at[:, k, pl.ds(cc * hr, hr), :]

        def rcopy(sem, src, dst, to):
            return pltpu.make_async_remote_copy(src_ref=src, dst_ref=dst, send_sem=send_sems.at[sem],
                                                recv_sem=recv_sems.at[sem], device_id=to, device_id_type=MESH)

        def w_send(q, r):
            return rcopy(q * SUBS + r, piece(wpre_ref, me, c, r), piece(wf_ref, me, c, r), (*chips[q], c))

        def w_landed(q, r):
            return rcopy(q * SUBS + r, piece(wpre_ref, me, c, r), piece(wf_ref, chip_of(q), c, r), sibling)

        def w_forward(q, r, cc):
            ref = piece(wf_ref, chip_of(q), cc, r)
            return rcopy(9 + q * SUBS + r, ref, ref, sibling)

        def wo_send(q):
            return rcopy(18 + q, wo_ref.at[:, pl.ds(c * hr, hr), :], wo_half(wof_ref, me, c), (*chips[q], c))

        def wo_landed(q):
            return rcopy(18 + q, wo_ref.at[:, pl.ds(c * hr, hr), :], wo_half(wof_ref, chip_of(q), c), sibling)

        def wo_forward(q, cc):
            ref = wo_half(wof_ref, chip_of(q), cc)
            return rcopy(21 + q, ref, ref, sibling)

        def conv_send(q):
            return rcopy(24 + q, cw_ref, cwf_ref.at[me], (*chips[q], c))

        def local_copies():
            return [pltpu.make_async_copy(wo_ref, wof_ref.at[:, me], local_sems.at[0]),
                    pltpu.make_async_copy(cw_ref, cwf_ref.at[me], local_sems.at[1])]

        def tile_start(u_, j_, slot):
            g = order_ref[u_ // SUBS] * BLOCKS_PER_SHARD + lax.rem(u_, SUBS) * nj + j_
            cols = pl.ds(pl.multiple_of(g * COL_BLOCK, COL_BLOCK), COL_BLOCK)

            @pl.when(u_ < SUBS)
            def _():
                pltpu.make_async_copy(wpre_ref.at[:, cols], wbuf.at[slot], wsem.at[slot]).start()

            @pl.when(u_ >= SUBS)
            def _():
                pltpu.make_async_copy(wf_ref.at[:, cols], wbuf.at[slot], wsem.at[slot]).start()

        def end_of(u_):
            return jnp.logical_and(u == u_, jnp.logical_and(i == ni - 1, j == nj - 1))

        @pl.when(n == 0)
        def _():
            for cp in local_copies():
                cp.start()
            for r in range(SUBS):
                for q in (0, 1):
                    w_send(q, r).start()
            for q in range(3):
                conv_send(q).start()
            tile_start(0, 0, 0)
            rows_start(0)

        def pass_on(q, r):
            w_landed(q, r).wait_recv()
            w_forward(q, r, c).start()

        for u_ in range(1, units - 1):
            @pl.when(end_of(u_))
            def _(u_=u_):
                if u_ <= SUBS:
                    for q in (0, 1):
                        pass_on(q, u_ - 1)
                if u_ == SUBS:
                    for r in range(SUBS):
                        for q in (0, 1):
                            w_send(q, r).wait_send()
                    for r in range(SUBS):
                        w_send(2, r).start()
                if 2 * SUBS - 1 <= u_ <= 3 * SUBS - 2:
                    pass_on(2, u_ - (2 * SUBS - 1))
                if u_ == 3 * SUBS - 2:
                    for r in range(SUBS):
                        w_send(2, r).wait_send()
                    for q in range(3):
                        wo_send(q).start()
                nxt = u_ + 1
                if nxt >= SUBS:
                    w_forward(nxt // SUBS - 1, nxt % SUBS, 1 - c).wait_recv()

        n1 = n + 1

        @pl.when(n1 < total)
        def _():
            tile_start(n1 // (ni * nj), lax.rem(n1, nj), lax.rem(n1, 2))

        xslot = lax.rem(m, 2)

        @pl.when(j == 0)
        def _():
            @pl.when(m + 1 < units * ni)
            def _():
                rows_start(m + 1)

            pltpu.make_async_copy(x_ref.at[pl.ds(0, tm), :], xbuf.at[xslot], xsem.at[xslot]).wait()

        slot = lax.rem(n, 2)
        pltpu.make_async_copy(wpre_ref.at[:, pl.ds(0, COL_BLOCK)], wbuf.at[slot], wsem.at[slot]).wait()
        p_ref[...] = _dot(xbuf[xslot], wbuf[slot]).astype(BF16)

        @pl.when(n == total - 1)
        def _():
            for q in range(3):
                wo_landed(q).wait_recv()
                wo_forward(q, c).start()
            for q in range(3):
                wo_forward(q, 1 - c).wait_recv()
                rcopy(24 + q, cw_ref, cwf_ref.at[chip_of(q)], sibling).wait_recv()
            for q in range(3):
                for r in range(SUBS):
                    w_forward(q, r, c).wait_send()
                wo_send(q).wait_send()
                wo_forward(q, c).wait_send()
                conv_send(q).wait_send()
            for cp in local_copies():
                cp.wait()

    def p_map(u, i, j, o):
        g = o[u // SUBS] * BLOCKS_PER_SHARD + lax.rem(u, SUBS) * nj + j
        return (_slot_of_seg(g // BLOCKS_PER_SEG), i, lax.rem(g, BLOCKS_PER_SEG))

    return pl.pallas_call(
        body, name="gather_and_project",
        grid_spec=pltpu.PrefetchScalarGridSpec(
            num_scalar_prefetch=1, grid=(units, ni, nj),
            in_specs=[ANY, ANY, ANY, ANY],
            out_specs=(pl.BlockSpec((None, tm, COL_BLOCK), p_map), ANY, ANY, ANY),
            scratch_shapes=[pltpu.VMEM((2, d, COL_BLOCK), BF16), pltpu.SemaphoreType.DMA((2,)),
                            pltpu.VMEM((2, tm, D_MODEL), BF16), pltpu.SemaphoreType.DMA((2,)),
                            pltpu.SemaphoreType.DMA((27,)), pltpu.SemaphoreType.DMA((27,)),
                            pltpu.SemaphoreType.DMA((2,))]),
        out_shape=(jax.ShapeDtypeStruct((N_SLOTS, t, D_MODEL), BF16),
                   jax.ShapeDtypeStruct((d, N_CHIPS * sc), BF16),
                   jax.ShapeDtypeStruct((3, N_CHIPS, rows, D_MODEL), BF16),
                   jax.ShapeDtypeStruct((N_CHIPS,) + conv_w8.shape, F32)),
        input_output_aliases={2: 1},
        compiler_params=pltpu.CompilerParams(dimension_semantics=("arbitrary",) * 3, vmem_limit_bytes=VMEM_LIMIT,
                                             has_side_effects=True),
    )(order, xb, w_pre, wo_b, conv_w8)


def _exchange_halves(g_o):
    hr = SHARD_ROWS // 2
    g_o4 = [g.reshape(N_CHIPS, 2, hr, D_MODEL) for g in g_o]

    def body(ga_ref, gb_ref, gc_ref, ra_ref, rb_ref, rc_ref, send_sems, recv_sems):
        x, y, c, _ = _mesh_pos()
        sibling = (x, y, 1 - c)
        cps = []
        for n, (g_ref, r_ref) in enumerate(((ga_ref, ra_ref), (gb_ref, rb_ref), (gc_ref, rc_ref))):
            cps.append(pltpu.make_async_remote_copy(src_ref=g_ref.at[:, 1 - c], dst_ref=r_ref,
                                                    send_sem=send_sems.at[n], recv_sem=recv_sems.at[n],
                                                    device_id=sibling, device_id_type=MESH))
        for cp in cps:
            cp.start()
        for cp in cps:
            cp.wait()

    o_shape = jax.ShapeDtypeStruct((N_CHIPS, hr, D_MODEL), BF16)
    return pl.pallas_call(
        body, name="rs_exchange_halves",
        in_specs=[ANY] * 3, out_specs=(ANY,) * 3,
        out_shape=(o_shape, o_shape, o_shape),
        scratch_shapes=[pltpu.SemaphoreType.DMA((3,)), pltpu.SemaphoreType.DMA((3,))],
        compiler_params=pltpu.CompilerParams(has_side_effects=True),
    )(*g_o4)


def _add_halves(g_in, g_o, r_in, r_o):
    d, c9 = g_in.shape
    hd = d // 2
    hr = SHARD_ROWS // 2
    core = lax.axis_index("c").astype(jnp.int32).reshape(1)
    tm = min(512, hd)
    nb = hd // tm

    def body_in(c_ref, g_ref, r_ref, o_ref):
        o_ref[...] = (g_ref[...].astype(F32) + r_ref[...].astype(F32)).astype(BF16)

    s_in = pl.pallas_call(
        body_in, name="rs_add_halves_in",
        grid_spec=pltpu.PrefetchScalarGridSpec(
            num_scalar_prefetch=1, grid=(N_CHIPS, nb),
            in_specs=[pl.BlockSpec((tm, SHARD_COLS), lambda k, i, c_ref: (c_ref[0] * nb + i, k)),
                      pl.BlockSpec((tm, SHARD_COLS), lambda k, i, c_ref: (i, k))],
            out_specs=pl.BlockSpec((None, tm, SHARD_COLS), lambda k, i, c_ref: (k, i, 0))),
        out_shape=jax.ShapeDtypeStruct((N_CHIPS, hd, SHARD_COLS), BF16),
        compiler_params=_params("parallel", "parallel"),
    )(core, g_in, r_in)

    g_o4 = [g.reshape(N_CHIPS, 2, hr, D_MODEL) for g in g_o]

    def body_o(c_ref, ga_ref, gb_ref, gc_ref, ra_ref, rb_ref, rc_ref, o_ref):
        for n, (g_ref, r_ref) in enumerate(((ga_ref, ra_ref), (gb_ref, rb_ref), (gc_ref, rc_ref))):
            o_ref[n] = (g_ref[...].astype(F32) + r_ref[...].astype(F32)).astype(BF16)

    gspec = pl.BlockSpec((None, None, hr, D_MODEL), lambda k, c_ref: (k, c_ref[0], 0, 0))
    rspec = pl.BlockSpec((None, hr, D_MODEL), lambda k, c_ref: (k, 0, 0))
    s_o = pl.pallas_call(
        body_o, name="rs_add_halves_o",
        grid_spec=pltpu.PrefetchScalarGridSpec(
            num_scalar_prefetch=1, grid=(N_CHIPS,),
            in_specs=[gspec] * 3 + [rspec] * 3,
            out_specs=pl.BlockSpec((3, None, hr, D_MODEL), lambda k, c_ref: (0, k, 0, 0))),
        out_shape=jax.ShapeDtypeStruct((3, N_CHIPS, hr, D_MODEL), BF16),
        compiler_params=_params("parallel"),
    )(core, *g_o4, *r_o)
    return s_in, s_o


def _sum_chips(r_in, r_o, s_in, s_o):
    _, hd, sc = r_in.shape
    hr = r_o.shape[2]
    tm = min(256, hd)
    nb = hd // tm
    pos = jnp.stack([2 * lax.axis_index("x") + lax.axis_index("y"), lax.axis_index("c")]).astype(jnp.int32)

    def chip_sum(pos_ref, r_ref, s_ref):
        acc = None
        for k in range(N_CHIPS):
            term = jnp.where(pos_ref[0] == k, s_ref[...], r_ref[k]).astype(F32)
            acc = term if acc is None else acc + term
        return acc

    def body_in(pos_ref, r_ref, s_ref, o_ref):
        o_ref[...] = chip_sum(pos_ref, r_ref, s_ref)

    f_in = pl.pallas_call(
        body_in, name="rs_sum_chips_in",
        grid_spec=pltpu.PrefetchScalarGridSpec(
            num_scalar_prefetch=1, grid=(nb,),
            in_specs=[pl.BlockSpec((N_CHIPS, tm, sc), lambda i, p: (0, i, 0)),
                      pl.BlockSpec((None, tm, sc), lambda i, p: (p[0], i, 0))],
            out_specs=pl.BlockSpec((tm, sc), lambda i, p: (p[1] * nb + i, 0))),
        out_shape=jax.ShapeDtypeStruct((2 * hd, sc), F32),
        compiler_params=_params("parallel"),
    )(pos, r_in, s_in)

    def body_o(pos_ref, r_ref, s_ref, o_ref):
        o_ref[...] = chip_sum(pos_ref, r_ref, s_ref)

    f_o = pl.pallas_call(
        body_o, name="rs_sum_chips_o",
        grid_spec=pltpu.PrefetchScalarGridSpec(
            num_scalar_prefetch=1, grid=(3,),
            in_specs=[pl.BlockSpec((N_CHIPS, None, hr, D_MODEL), lambda n, p: (0, n, 0, 0)),
                      pl.BlockSpec((None, None, hr, D_MODEL), lambda n, p: (n, p[0], 0, 0))],
            out_specs=pl.BlockSpec((None, hr, D_MODEL), lambda n, p: (n, p[1], 0))),
        out_shape=jax.ShapeDtypeStruct((3, 2 * hr, D_MODEL), F32),
        compiler_params=_params("parallel"),
    )(pos, r_o, s_o)
    return f_in, f_o


def _share_halves(f_in, f_o):
    hd, sc = f_in.shape[0] // 2, f_in.shape[1]
    hr = f_o.shape[1] // 2

    def body(fi_ref, fo_ref, gi_ref, go_ref, send_sems, recv_sems):
        x, y, c, _ = _mesh_pos()
        sibling = (x, y, 1 - c)

        def halves(cc):
            rows_i, rows_o = pl.ds(cc * hd, hd), pl.ds(cc * hr, hr)
            return (fi_ref.at[rows_i, :], gi_ref.at[rows_i, :]), (fo_ref.at[:, rows_o, :], go_ref.at[:, rows_o, :])

        def copies(cc):
            return [pltpu.make_async_remote_copy(src_ref=src, dst_ref=dst, send_sem=send_sems.at[n],
                                                 recv_sem=recv_sems.at[n], device_id=sibling, device_id_type=MESH)
                    for n, (src, dst) in enumerate(halves(cc))]

        sends = copies(c)
        for cp in sends:
            cp.start()
        for cp in copies(1 - c):
            cp.wait_recv()
        for cp in sends:
            cp.wait_send()

    return pl.pallas_call(
        body, name="rs_share_halves",
        in_specs=[ANY, ANY], out_specs=(ANY, ANY),
        out_shape=(jax.ShapeDtypeStruct(f_in.shape, F32), jax.ShapeDtypeStruct(f_o.shape, F32)),
        scratch_shapes=[pltpu.SemaphoreType.DMA((2,)), pltpu.SemaphoreType.DMA((2,))],
        input_output_aliases={0: 0, 1: 1},
        compiler_params=pltpu.CompilerParams(has_side_effects=True),
    )(f_in, f_o)


def _merge_forward(ya, yb, wo3, p, bg, tm, tn):
    t = ya.shape[0]

    def body(ya_ref, yb_ref, wa_ref, wb_ref, g_ref, bg_ref, m_ref, oab_ref):
        oa = _dot(ya_ref[...], wa_ref[...])
        ob = _dot(yb_ref[...], wb_ref[...])
        ga = _sigmoid(g_ref[0].astype(F32) + bg_ref[0])
        gb = _sigmoid(g_ref[1].astype(F32) + bg_ref[1])
        m_ref[...] = (ga * oa + gb * ob).astype(BF16)
        oab_ref[0] = oa.astype(BF16)
        oab_ref[1] = ob.astype(BF16)

    return pl.pallas_call(
        body, name="merge_forward", grid=(t // tm, D_MODEL // tn),
        in_specs=[pl.BlockSpec((tm, D_MODEL), lambda i, j: (i, 0)),
                  pl.BlockSpec((tm, D_MODEL), lambda i, j: (i, 0)),
                  pl.BlockSpec((None, D_MODEL, tn), lambda i, j: (0, 0, j)),
                  pl.BlockSpec((None, D_MODEL, tn), lambda i, j: (1, 0, j)),
                  pl.BlockSpec((2, tm, tn), lambda i, j: (BLOCK_G, i, j)),
                  pl.BlockSpec((2, 1, tn), lambda i, j: (0, 0, j))],
        out_specs=(pl.BlockSpec((tm, tn), lambda i, j: (i, j)),
                   pl.BlockSpec((2, tm, tn), lambda i, j: (0, i, j))),
        out_shape=(jax.ShapeDtypeStruct((t, D_MODEL), BF16), jax.ShapeDtypeStruct((2, t, D_MODEL), BF16)),
        compiler_params=_params("parallel", "parallel"),
    )(ya, yb, wo3, wo3, p, bg)


HEAD_ROWS = 256


def _head(merged, wo3, x, target, ln_g, ln_b, tm):
    t = x.shape[0]
    inv_d = 1.0 / D_MODEL

    def body(m_ref, w_ref, x_ref, t_ref, g_ref, b_ref, dr_ref, gx_ref, dm_ref, dg_ref, db_ref, loss_ref):
        i = pl.program_id(0)

        @pl.when(i == 0)
        def _():
            dg_ref[...] = jnp.zeros_like(dg_ref)
            db_ref[...] = jnp.zeros_like(db_ref)
            loss_ref[...] = jnp.zeros_like(loss_ref)

        w = w_ref[...]
        g = g_ref[...]
        tiles = [slice(r0, r0 + HEAD_ROWS) for r0 in range(0, tm, HEAD_ROWS)]
        firsts = [_dot(m_ref[rows, :], w) for rows in tiles]
        for rows, out in zip(tiles, firsts):
            r = DN_ALPHA * x_ref[rows, :] + out
            mu = jnp.mean(r, axis=-1, keepdims=True)
            xc = r - mu
            var = jnp.mean(xc * xc, axis=-1, keepdims=True)
            rstd = lax.rsqrt(var + LN_EPS)
            xhat = xc * rstd
            e = xhat * g + b_ref[...] - t_ref[rows, :]
            se = jnp.sum(jnp.sum(e * e, axis=1, keepdims=True), axis=0, keepdims=True)
            loss_ref[...] += jnp.broadcast_to((0.5 * inv_d) * se, loss_ref.shape)
            dy = e * inv_d
            db_ref[...] += jnp.sum(dy, axis=0, keepdims=True)
            dg_ref[...] += jnp.sum(dy * xhat, axis=0, keepdims=True)
            dxh = dy * g
            m1 = jnp.mean(dxh, axis=-1, keepdims=True)
            m2 = jnp.mean(dxh * xhat, axis=-1, keepdims=True)
            dr = rstd * (dxh - m1 - xhat * m2)
            gx_ref[rows, :] = DN_ALPHA * dr
            drb = dr.astype(BF16)
            dr_ref[rows, :] = drb
            dm_ref[rows, :] = _dot(drb, w, 1, 1).astype(BF16)

    row = pl.BlockSpec((tm, D_MODEL), lambda i: (i, 0))
    vec = pl.BlockSpec((1, D_MODEL), lambda i: (0, 0))
    return pl.pallas_call(
        body, name="head", grid=(t // tm,),
        in_specs=[row, pl.BlockSpec((None, D_MODEL, D_MODEL), lambda i: (2, 0, 0), pipeline_mode=pl.Buffered(1)),
                  row, row, vec, vec],
        out_specs=(row, row, row, vec, vec, pl.BlockSpec((1, 128), lambda i: (0, 0))),
        out_shape=(jax.ShapeDtypeStruct((t, D_MODEL), BF16), jax.ShapeDtypeStruct((t, D_MODEL), F32),
                   jax.ShapeDtypeStruct((t, D_MODEL), BF16), jax.ShapeDtypeStruct((1, D_MODEL), F32),
                   jax.ShapeDtypeStruct((1, D_MODEL), F32), jax.ShapeDtypeStruct((1, 128), F32)),
        compiler_params=_params("arbitrary"),
    )(merged, wo3, x, target, ln_g, ln_b)


def _gate_and_branch_backward(dmerged, oab, p, bg, wo3, tm):
    t = dmerged.shape[0]

    def body(dm_ref, oab_ref, g_ref, bg_ref, wa_ref, wb_ref, do_ref, dpg_ref, dbg_ref, dy_ref):
        @pl.when(pl.program_id(0) == 0)
        def _():
            dbg_ref[...] = jnp.zeros_like(dbg_ref)

        dm = dm_ref[...].astype(F32)
        for n, w_ref in enumerate((wa_ref, wb_ref)):
            gate = _sigmoid(g_ref[n].astype(F32) + bg_ref[n])
            d_o = (dm * gate).astype(BF16)
            do_ref[n] = d_o
            dgate = dm * oab_ref[n].astype(F32) * gate * (1.0 - gate)
            dpg_ref[n] = dgate.astype(BF16)
            dbg_ref[n] += jnp.sum(dgate, axis=0, keepdims=True)
            dy_ref[n] = _dot(d_o, w_ref[...], 1, 1).astype(BF16)

    pair = pl.BlockSpec((2, tm, D_MODEL), lambda i: (0, i, 0))
    gates = pl.BlockSpec((2, tm, D_MODEL), lambda i: (BLOCK_G, i, 0))
    vec = pl.BlockSpec((2, 1, D_MODEL), lambda i: (0, 0, 0))

    def weight(n):
        return pl.BlockSpec((None, D_MODEL, D_MODEL), lambda i: (n, 0, 0), pipeline_mode=pl.Buffered(1))

    pair_shape = jax.ShapeDtypeStruct((2, t, D_MODEL), BF16)
    return pl.pallas_call(
        body, name="gate_and_branch_backward", grid=(t // tm,),
        in_specs=[pl.BlockSpec((tm, D_MODEL), lambda i: (i, 0)), pair, gates, vec, weight(0), weight(1)],
        out_specs=(pair, gates, vec, pair),
        out_shape=(pair_shape, jax.ShapeDtypeStruct((N_SLOTS, t, D_MODEL), BF16),
                   jax.ShapeDtypeStruct((2, 1, D_MODEL), F32), pair_shape),
        compiler_params=_params("arbitrary"),
    )(dmerged, oab, p, bg, wo3, wo3)


def _weight_grad(a, a_sel, b, b_sel, tm, tn, tk, name, exchange=None):
    t = a.shape[-2]
    nk = t // tk
    ni, nj = D_MODEL // tm, D_MODEL // tn

    def body(a_ref, b_ref, *rest):
        if exchange is None:
            o_ref, acc_ref = rest
        else:
            g_ref, o_ref, r_ref, acc_ref, send_sem, recv_sem = rest
        i, j, k = pl.program_id(0), pl.program_id(1), pl.program_id(2)

        if exchange is not None:
            x, y, c, _ = _mesh_pos()
            hd = exchange.shape[0] // 2
            swap = pltpu.make_async_remote_copy(src_ref=g_ref.at[pl.ds((1 - c) * hd, hd), :], dst_ref=r_ref,
                                                send_sem=send_sem, recv_sem=recv_sem,
                                                device_id=(x, y, 1 - c), device_id_type=MESH)

            @pl.when(jnp.logical_and(i == 0, jnp.logical_and(j == 0, k == 0)))
            def _():
                swap.start()

        @pl.when(k == 0)
        def _():
            acc_ref[...] = jnp.zeros_like(acc_ref)

        acc_ref[...] += _dot(a_ref[...], b_ref[...], 0, 0)

        @pl.when(k == nk - 1)
        def _():
            o_ref[...] = acc_ref[...].astype(BF16)

        if exchange is not None:
            @pl.when(jnp.logical_and(i == ni - 1, jnp.logical_and(j == nj - 1, k == nk - 1)))
            def _():
                swap.wait()

    def spec(arr, sel, width, which):
        if arr.ndim == 2:
            return pl.BlockSpec((tk, width), lambda i, j, k: (k, (i, j)[which]))
        return pl.BlockSpec((None, tk, width), lambda i, j, k: (sel, k, (i, j)[which]))

    o_spec = pl.BlockSpec((tm, tn), lambda i, j, k: (i, j))
    o_shape = jax.ShapeDtypeStruct((D_MODEL, D_MODEL), BF16)
    if exchange is None:
        return pl.pallas_call(
            body, name=name, grid=(ni, nj, nk),
            in_specs=[spec(a, a_sel, tm, 0), spec(b, b_sel, tn, 1)],
            out_specs=o_spec, out_shape=o_shape,
            scratch_shapes=[pltpu.VMEM((tm, tn), F32)],
            compiler_params=_params("parallel", "parallel", "arbitrary"),
        )(a, b)
    return pl.pallas_call(
        body, name=name, grid=(ni, nj, nk),
        in_specs=[spec(a, a_sel, tm, 0), spec(b, b_sel, tn, 1), ANY],
        out_specs=(o_spec, ANY),
        out_shape=(o_shape, jax.ShapeDtypeStruct((exchange.shape[0] // 2, exchange.shape[1]), BF16)),
        scratch_shapes=[pltpu.VMEM((tm, tn), F32), pltpu.SemaphoreType.DMA, pltpu.SemaphoreType.DMA],
        compiler_params=pltpu.CompilerParams(dimension_semantics=("arbitrary",) * 3, vmem_limit_bytes=VMEM_LIMIT,
                                             has_side_effects=True),
    )(a, b, exchange)


def _win_grad(xt, dp, tn, tk):
    _, t, _ = dp.shape
    nk = t // tk
    per_seg = D_MODEL // tn
    nj = N_SEG * per_seg

    def body(x_ref, dp_ref, o_ref, acc_ref):
        k = pl.program_id(1)

        @pl.when(k == 0)
        def _():
            acc_ref[...] = jnp.zeros_like(acc_ref)

        acc_ref[...] += _dot(x_ref[...], dp_ref[...])

        @pl.when(k == nk - 1)
        def _():
            o_ref[...] = acc_ref[...].astype(BF16)

    return pl.pallas_call(
        body, name="grad_w_in", grid=(nj, nk),
        in_specs=[pl.BlockSpec((D_MODEL, tk), lambda j, k: (0, k)),
                  pl.BlockSpec((None, tk, tn), lambda j, k: (_slot_of_seg(j // per_seg), k, j % per_seg))],
        out_specs=pl.BlockSpec((D_MODEL, tn), lambda j, k: (0, j)),
        out_shape=jax.ShapeDtypeStruct((D_MODEL, N_SEG * D_MODEL), BF16),
        scratch_shapes=[pltpu.VMEM((D_MODEL, tn), F32)],
        compiler_params=_params("parallel", "arbitrary"),
    )(xt, dp)


def _input_grad_and_scatter(dp, w_full, gx, s_in, s_o, small, tm):
    _, t, _ = dp.shape
    ni, nk = t // tm, N_SEG - TAIL_SEGS
    _, hd, sc = s_in.shape
    hr = s_o.shape[2]

    def body(dp_ref, w_ref, gx_ref, si_ref, so_ref, sm_ref, o_ref, ri_ref, ro_ref, ga_ref,
             send_sems, recv_sems, local_sem):
        i, k = pl.program_id(0), pl.program_id(1)
        x, y, c, chips = _mesh_pos()
        me = 2 * x + y
        dev = 4 * x + 2 * y + c

        def peer(r):
            return (x ^ ((r >> 2) & 1), y ^ ((r >> 1) & 1), c ^ (r & 1))

        def sends():
            cps = []
            for q, (cx, cy) in enumerate(chips):
                dest = 2 * cx + cy
                cps.append(pltpu.make_async_remote_copy(src_ref=si_ref.at[dest], dst_ref=ri_ref.at[me],
                                                        send_sem=send_sems.at[q], recv_sem=recv_sems.at[q],
                                                        device_id=(cx, cy, c), device_id_type=MESH))
                cps.append(pltpu.make_async_remote_copy(src_ref=so_ref.at[:, dest], dst_ref=ro_ref.at[me],
                                                        send_sem=send_sems.at[3 + q], recv_sem=recv_sems.at[3 + q],
                                                        device_id=(cx, cy, c), device_id_type=MESH))
            for r in range(1, 8):
                cps.append(pltpu.make_async_remote_copy(src_ref=sm_ref, dst_ref=ga_ref.at[dev],
                                                        send_sem=send_sems.at[5 + r], recv_sem=recv_sems.at[5 + r],
                                                        device_id=peer(r), device_id_type=MESH))
            return cps

        own_small = pltpu.make_async_copy(sm_ref, ga_ref.at[dev], local_sem)

        @pl.when(jnp.logical_and(i == 0, k == 0))
        def _():
            for cp in sends():
                cp.start()
            own_small.start()

        @pl.when(k == 0)
        def _():
            o_ref[...] = gx_ref[...]

        o_ref[...] += _dot(dp_ref[...], w_ref[...], 1, 1)

        @pl.when(jnp.logical_and(i == ni - 1, k == nk - 1))
        def _():
            for q, (cx, cy) in enumerate(chips):
                frm = 2 * cx + cy
                pltpu.make_async_remote_copy(src_ref=si_ref.at[frm], dst_ref=ri_ref.at[frm], send_sem=send_sems.at[q],
                                             recv_sem=recv_sems.at[q], device_id=(x, y, c),
                                             device_id_type=MESH).wait_recv()
                pltpu.make_async_remote_copy(src_ref=so_ref.at[:, frm], dst_ref=ro_ref.at[frm],
                                             send_sem=send_sems.at[3 + q], recv_sem=recv_sems.at[3 + q],
                                             device_id=(x, y, c), device_id_type=MESH).wait_recv()
            for r in range(1, 8):
                px, py, pc = peer(r)
                pltpu.make_async_remote_copy(src_ref=sm_ref, dst_ref=ga_ref.at[4 * px + 2 * py + pc],
                                             send_sem=send_sems.at[5 + r], recv_sem=recv_sems.at[5 + r],
                                             device_id=(x, y, c), device_id_type=MESH).wait_recv()
            for cp in sends():
                cp.wait_send()
            own_small.wait()

    return pl.pallas_call(
        body, name="grad_x_and_scatter", grid=(ni, nk),
        in_specs=[pl.BlockSpec((None, tm, D_MODEL), lambda i, k: (_slot_of_seg(k), i, 0)),
                  pl.BlockSpec((D_MODEL, D_MODEL), lambda i, k: (0, k)),
                  pl.BlockSpec((tm, D_MODEL), lambda i, k: (i, 0), pipeline_mode=pl.Buffered(1)), ANY, ANY, ANY],
        out_specs=(pl.BlockSpec((tm, D_MODEL), lambda i, k: (i, 0)), ANY, ANY, ANY),
        out_shape=(jax.ShapeDtypeStruct((t, D_MODEL), F32),
                   jax.ShapeDtypeStruct((N_CHIPS, hd, sc), BF16),
                   jax.ShapeDtypeStruct((N_CHIPS, 3, hr, D_MODEL), BF16),
                   jax.ShapeDtypeStruct((8,) + small.shape, small.dtype)),
        scratch_shapes=[pltpu.SemaphoreType.DMA((13,)), pltpu.SemaphoreType.DMA((13,)), pltpu.SemaphoreType.DMA],
        input_output_aliases={2: 0},
        compiler_params=pltpu.CompilerParams(dimension_semantics=("arbitrary", "arbitrary"),
                                             vmem_limit_bytes=VMEM_LIMIT, has_side_effects=True),
    )(dp, w_full, gx, s_in, s_o, small)


TAIL_SEGS = 1


def _input_grad_tail_and_adam(dp, w_full, partial, w, g, m, v, steps):
    _, t, _ = dp.shape
    tm = t // steps
    r, c = w.shape
    ra = r // steps
    segs = tuple(range(N_SEG - TAIL_SEGS, N_SEG))

    def body(*refs):
        dp_refs = refs[:TAIL_SEGS]
        w_refs = refs[TAIL_SEGS:2 * TAIL_SEGS]
        part_ref, aw_ref, ag_ref, am_ref, av_ref, o_ref, go_ref, d_ref, mo_ref, vo_ref = refs[2 * TAIL_SEGS:]
        acc = part_ref[...]
        for dp_ref, w_ref in zip(dp_refs, w_refs):
            acc = acc + _dot(dp_ref[...], w_ref[...], 1, 1)
        o_ref[...] = acc
        gv = ag_ref[...]
        d, mn, vn = _adam_math(aw_ref[...], gv, am_ref[...], av_ref[...])
        go_ref[...] = gv
        d_ref[...] = d
        mo_ref[...] = mn
        vo_ref[...] = vn

    row = pl.BlockSpec((tm, D_MODEL), lambda i: (i, 0))
    arow = pl.BlockSpec((ra, c), lambda i: (i, 0))
    ashape = jax.ShapeDtypeStruct((r, c), F32)
    in_specs = [pl.BlockSpec((None, tm, D_MODEL), functools.partial(lambda s, i: (SLOT_OF_SEG[s], i, 0), s))
                for s in segs]
    in_specs += [pl.BlockSpec((D_MODEL, D_MODEL), functools.partial(lambda s, i: (0, s), s),
                              pipeline_mode=pl.Buffered(1)) for s in segs]
    in_specs += [row, arow, arow, arow, arow]
    res = pl.pallas_call(
        body, name="grad_x_tail_and_adam_w_in", grid=(steps,),
        in_specs=in_specs,
        out_specs=(row, arow, arow, arow, arow),
        out_shape=(jax.ShapeDtypeStruct((t, D_MODEL), F32), ashape, ashape, ashape, ashape),
        compiler_params=_params("parallel"),
    )(*([dp] * TAIL_SEGS), *([w_full] * TAIL_SEGS), partial, w, g, m, v)
    return res[0], res[1:]


def _sgu_chunk_forward(u, v, z, wm, bs, lng, lnb):
    ug, dug = _gelu_and_grad(u)
    vg, dvg = _gelu_and_grad(v)
    mu = jnp.mean(vg, axis=-1, keepdims=True)
    xc = vg - mu
    var = jnp.mean(xc * xc, axis=-1, keepdims=True)
    rstd = lax.rsqrt(var + LN_EPS)
    vhat = xc * rstd
    vln = (vhat * lng + lnb).astype(BF16)
    mixed = _dot(wm, vln) + bs
    sig = _sigmoid(z)
    return ug, dug, dvg, rstd, vhat, vln, mixed, sig


def _mixer_a_forward(p_a, wm, bs_col, ln_v_g, ln_v_b, tm):
    t = p_a.shape[1]

    def body(p_ref, wm_ref, bs_ref, g_ref, b_ref, o_ref):
        wm_v, bs_v, lng, lnb = wm_ref[...], bs_ref[...], g_ref[...], b_ref[...]

        def chunk(ci, carry):
            rows = pl.ds(pl.multiple_of(ci * CHUNK, CHUNK), CHUNK)
            u = p_ref[0, rows, :].astype(F32)
            v = p_ref[1, rows, :].astype(F32)
            z = p_ref[2, rows, :].astype(F32)
            ug, _, _, _, _, _, mixed, sig = _sgu_chunk_forward(u, v, z, wm_v, bs_v, lng, lnb)
            o_ref[rows, :] = (ug * mixed * (z * sig)).astype(BF16)
            return carry

        lax.fori_loop(0, tm // CHUNK, chunk, 0, unroll=True)

    return pl.pallas_call(
        body, name="mixer_a_forward", grid=(t // tm, N_HEADS),
        in_specs=[pl.BlockSpec((3, tm, HEAD_DIM), lambda i, h: (0, i, h)),
                  pl.BlockSpec((None, CHUNK, CHUNK), lambda i, h: (h, 0, 0)),
                  pl.BlockSpec((None, CHUNK, 1), lambda i, h: (h, 0, 0)),
                  pl.BlockSpec((1, HEAD_DIM), lambda i, h: (0, h)),
                  pl.BlockSpec((1, HEAD_DIM), lambda i, h: (0, h))],
        out_specs=pl.BlockSpec((tm, HEAD_DIM), lambda i, h: (i, h)),
        out_shape=jax.ShapeDtypeStruct((t, D_MODEL), BF16),
        compiler_params=_params("parallel", "parallel"),
    )(p_a, wm, bs_col, ln_v_g, ln_v_b)


def _mixer_a_backward(p_a, dyab, wm, bs_col, ln_v_g, ln_v_b, dp, tm):
    t = p_a.shape[1]

    def body(p_ref, dy_ref, wm_ref, bs_ref, g_ref, b_ref, dp_in, dp_ref, dws_ref, dbs_ref, dg_ref, db_ref):
        @pl.when(pl.program_id(1) == 0)
        def _():
            dws_ref[...] = jnp.zeros_like(dws_ref)
            dbs_ref[...] = jnp.zeros_like(dbs_ref)
            dg_ref[...] = jnp.zeros_like(dg_ref)
            db_ref[...] = jnp.zeros_like(db_ref)

        wm_v, bs_v, lng, lnb = wm_ref[...], bs_ref[...], g_ref[...], b_ref[...]
        causal = (lax.broadcasted_iota(jnp.int32, (CHUNK, CHUNK), 1)
                  <= lax.broadcasted_iota(jnp.int32, (CHUNK, CHUNK), 0))

        def chunk(ci, carry):
            rows = pl.ds(pl.multiple_of(ci * CHUNK, CHUNK), CHUNK)
            u = p_ref[0, rows, :].astype(F32)
            v = p_ref[1, rows, :].astype(F32)
            z = p_ref[2, rows, :].astype(F32)
            dy = dy_ref[rows, :].astype(F32)
            ug, dug, dvg, rstd, vhat, vln, mixed, sig = _sgu_chunk_forward(u, v, z, wm_v, bs_v, lng, lnb)
            sz = z * sig
            dmixed = dy * ug * sz
            dp_ref[0, rows, :] = (dy * mixed * sz * dug).astype(BF16)
            dp_ref[2, rows, :] = (dy * ug * mixed * _silu_grad(sig, sz)).astype(BF16)
            dbs_ref[...] += jnp.sum(dmixed, axis=1, keepdims=True)
            dmb = dmixed.astype(BF16)
            dws_ref[...] += jnp.where(causal, _dot(dmb, vln, 1, 1), 0.0)
            dvln = _dot(wm_v, dmb, 0, 0)
            db_ref[...] += jnp.sum(dvln, axis=0, keepdims=True)
            dg_ref[...] += jnp.sum(dvln * vhat, axis=0, keepdims=True)
            dvh = dvln * lng
            m1 = jnp.mean(dvh, axis=-1, keepdims=True)
            m2 = jnp.mean(dvh * vhat, axis=-1, keepdims=True)
            dp_ref[1, rows, :] = (rstd * (dvh - m1 - vhat * m2) * dvg).astype(BF16)
            return carry

        lax.fori_loop(0, tm // CHUNK, chunk, 0, unroll=True)

    return pl.pallas_call(
        body, name="mixer_a_backward", grid=(N_HEADS, t // tm),
        in_specs=[pl.BlockSpec((3, tm, HEAD_DIM), lambda h, i: (0, i, h)),
                  pl.BlockSpec((None, tm, HEAD_DIM), lambda h, i: (0, i, h)),
                  pl.BlockSpec((None, CHUNK, CHUNK), lambda h, i: (h, 0, 0)),
                  pl.BlockSpec((None, CHUNK, 1), lambda h, i: (h, 0, 0)),
                  pl.BlockSpec((1, HEAD_DIM), lambda h, i: (0, h)),
                  pl.BlockSpec((1, HEAD_DIM), lambda h, i: (0, h)), ANY],
        out_specs=(pl.BlockSpec((3, tm, HEAD_DIM), lambda h, i: (BLOCK_A, i, h)),
                   pl.BlockSpec((None, CHUNK, CHUNK), lambda h, i: (h, 0, 0)),
                   pl.BlockSpec((None, CHUNK, 1), lambda h, i: (h, 0, 0)),
                   pl.BlockSpec((1, HEAD_DIM), lambda h, i: (0, h)),
                   pl.BlockSpec((1, HEAD_DIM), lambda h, i: (0, h))),
        out_shape=(jax.ShapeDtypeStruct(dp.shape, BF16),
                   jax.ShapeDtypeStruct((N_HEADS, CHUNK, CHUNK), F32),
                   jax.ShapeDtypeStruct((N_HEADS, CHUNK, 1), F32),
                   jax.ShapeDtypeStruct((1, D_MODEL), F32), jax.ShapeDtypeStruct((1, D_MODEL), F32)),
        input_output_aliases={6: 0},
        compiler_params=_params("parallel", "arbitrary"),
    )(p_a, dyab, wm, bs_col, ln_v_g, ln_v_b, dp)


HALO = 16


def _conv_taps(h, halo_h, tm):
    row = lax.broadcasted_iota(jnp.int32, h.shape, 0)
    last1 = halo_h[HALO - 1:HALO, :]
    last2 = halo_h[HALO - 2:HALO - 1, :]
    h1 = jnp.where(row == 0, last1, pltpu.roll(h, 1, 0))
    h2 = jnp.where(row == 0, last2, jnp.where(row == 1, last1, pltpu.roll(h, 2, 0)))
    return h1, h2


def _mixer_b_forward(p_b, conv_w, conv_b, tm, tc):
    t = p_b.shape[1]

    def body(p_ref, halo_ref, w_ref, b_ref, o_ref):
        valid = (pl.program_id(1) > 0).astype(F32)
        h = p_ref[1].astype(F32) * p_ref[0].astype(F32)
        halo_h = halo_ref[1].astype(F32) * halo_ref[0].astype(F32) * valid
        h1, h2 = _conv_taps(h, halo_h, tm)
        w = w_ref[...]
        conv = b_ref[...] + w[0:1, :] * h2 + w[1:2, :] * h1 + w[2:3, :] * h
        z = p_ref[3].astype(F32)
        o_ref[...] = (p_ref[2].astype(F32) * conv * (z * _sigmoid(z))).astype(BF16)

    steps = tm // HALO
    return pl.pallas_call(
        body, name="mixer_b_forward", grid=(D_MODEL // tc, t // tm),
        in_specs=[pl.BlockSpec((4, tm, tc), lambda j, i: (BLOCK_B, i, j)),
                  pl.BlockSpec((4, HALO, tc), lambda j, i: (BLOCK_B, jnp.maximum(i * steps - 1, 0), j)),
                  pl.BlockSpec((3, tc), lambda j, i: (0, j)),
                  pl.BlockSpec((1, tc), lambda j, i: (0, j))],
        out_specs=pl.BlockSpec((tm, tc), lambda j, i: (i, j)),
        out_shape=jax.ShapeDtypeStruct((t, D_MODEL), BF16),
        compiler_params=_params("parallel", "parallel"),
    )(p_b, p_b, conv_w, conv_b)


def _mixer_b_backward(p_b, dyab, conv_w, conv_b, dp, tm, tc):
    t = p_b.shape[1]
    n = t // tm

    def body(p_ref, halo_ref, dy_ref, w_ref, b_ref, dp_in, dp_ref, dw_ref, db_ref, next_ref):
        ii = pl.program_id(1)

        @pl.when(ii == 0)
        def _():
            dw_ref[...] = jnp.zeros_like(dw_ref)
            db_ref[...] = jnp.zeros_like(db_ref)
            next_ref[...] = jnp.zeros_like(next_ref)

        valid = (ii < n - 1).astype(F32)
        xb = p_ref[0].astype(F32)
        cb = p_ref[1].astype(F32)
        bb = p_ref[2].astype(F32)
        z = p_ref[3].astype(F32)
        h = cb * xb
        halo_h = halo_ref[1].astype(F32) * halo_ref[0].astype(F32) * valid
        h1, h2 = _conv_taps(h, halo_h, tm)
        w = w_ref[...]
        w0, w1, w2 = w[0:1, :], w[1:2, :], w[2:3, :]
        conv = b_ref[...] + w0 * h2 + w1 * h1 + w2 * h
        sig = _sigmoid(z)
        sz = z * sig
        dy = dy_ref[...].astype(F32)
        dconv = dy * bb * sz
        dp_ref[2] = (dy * conv * sz).astype(BF16)
        dp_ref[3] = (dy * bb * conv * _silu_grad(sig, sz)).astype(BF16)
        db_ref[...] += jnp.sum(dconv, axis=0, keepdims=True)
        dw_ref[0:1, :] += jnp.sum(dconv * h2, axis=0, keepdims=True)
        dw_ref[1:2, :] += jnp.sum(dconv * h1, axis=0, keepdims=True)
        dw_ref[2:3, :] += jnp.sum(dconv * h, axis=0, keepdims=True)
        row = lax.broadcasted_iota(jnp.int32, h.shape, 0)
        nxt = next_ref[...]
        n0, n1 = nxt[0:1, :], nxt[1:2, :]
        d1 = jnp.where(row == tm - 1, n0, pltpu.roll(dconv, tm - 1, 0))
        d2 = jnp.where(row == tm - 1, n1, jnp.where(row == tm - 2, n0, pltpu.roll(dconv, tm - 2, 0)))
        dh = w2 * dconv + w1 * d1 + w0 * d2
        dp_ref[0] = (dh * cb).astype(BF16)
        dp_ref[1] = (dh * xb).astype(BF16)
        next_ref[...] = dconv[0:8, :]

    steps = tm // HALO
    return pl.pallas_call(
        body, name="mixer_b_backward", grid=(D_MODEL // tc, n),
        in_specs=[pl.BlockSpec((4, tm, tc), lambda j, ii: (BLOCK_B, n - 1 - ii, j)),
                  pl.BlockSpec((4, HALO, tc), lambda j, ii: (BLOCK_B, jnp.maximum((n - 1 - ii) * steps - 1, 0), j)),
                  pl.BlockSpec((None, tm, tc), lambda j, ii: (1, n - 1 - ii, j)),
                  pl.BlockSpec((3, tc), lambda j, ii: (0, j)),
                  pl.BlockSpec((1, tc), lambda j, ii: (0, j)), ANY],
        out_specs=(pl.BlockSpec((4, tm, tc), lambda j, ii: (BLOCK_B, n - 1 - ii, j)),
                   pl.BlockSpec((3, tc), lambda j, ii: (0, j)),
                   pl.BlockSpec((1, tc), lambda j, ii: (0, j))),
        out_shape=(jax.ShapeDtypeStruct(dp.shape, BF16),
                   jax.ShapeDtypeStruct((3, D_MODEL), F32), jax.ShapeDtypeStruct((1, D_MODEL), F32)),
        scratch_shapes=[pltpu.VMEM((8, tc), F32)],
        input_output_aliases={5: 0},
        compiler_params=_params("parallel", "arbitrary"),
    )(p_b, p_b, dyab, conv_w, conv_b, dp)


def _adam_math(w, g, m, v):
    m = ADAM_B1 * m + (1.0 - ADAM_B1) * g
    v = ADAM_B2 * v + (1.0 - ADAM_B2) * (g * g)
    delta = -ADAM_LR * ((m * ADAM_C1) / (jnp.sqrt(v * ADAM_C2) + ADAM_EPS) + ADAM_WD * w)
    return delta, m, v


def _adam_rows(w, g, m, v, tm, name, g_sel=None):
    r, c = w.shape

    def body(w_ref, g_ref, m_ref, v_ref, go_ref, d_ref, mo_ref, vo_ref):
        g = g_ref[...]
        d, mn, vn = _adam_math(w_ref[...], g, m_ref[...], v_ref[...])
        go_ref[...] = g
        d_ref[...] = d
        mo_ref[...] = mn
        vo_ref[...] = vn

    spec = pl.BlockSpec((tm, c), lambda i: (i, 0))
    g_spec = spec if g_sel is None else pl.BlockSpec((None, tm, c), lambda i: (g_sel, i, 0))
    shape = jax.ShapeDtypeStruct((r, c), F32)
    return pl.pallas_call(
        body, name=name, grid=(r // tm,),
        in_specs=[spec, g_spec, spec, spec], out_specs=(spec,) * 4, out_shape=(shape,) * 4,
        compiler_params=_params("parallel"),
    )(w, g, m, v)


SMALL_ROW0 = {name: sum(r for _, r in SMALL_ROWS[:i]) for i, (name, _) in enumerate(SMALL_ROWS)}
LANE_MAJOR = ("ln_g", "ln_b", "b_gate", "ln_v_g", "ln_v_b", "conv_b")


def _lane_pieces(n):
    return [(q, slice(q * 128, (q + 1) * 128)) for q in range(n // 128)]


def _pack_small(d_ln_g, d_ln_b, d_bg, d_lnv_g, d_lnv_b, d_ws, d_bs, d_cw, d_cb, loss_part):
    def body(lg, lb, bg, vg, vb, ws, bs, cw, cb, loss, o_ref):
        def put(row0, vec):
            for q, cols in _lane_pieces(vec.shape[1]):
                o_ref[row0 + q:row0 + q + 1, :] = vec[:, cols]

        put(SMALL_ROW0["ln_g"], lg[...])
        put(SMALL_ROW0["ln_b"], lb[...])
        for n in range(2):
            put(SMALL_ROW0["b_gate"] + n * (D_MODEL // 128), bg[n])
        put(SMALL_ROW0["ln_v_g"], vg[...])
        put(SMALL_ROW0["ln_v_b"], vb[...])
        for h in range(N_HEADS):
            o_ref[SMALL_ROW0["w_s"] + h * CHUNK:SMALL_ROW0["w_s"] + (h + 1) * CHUNK, :] = ws[h]
        o_ref[SMALL_ROW0["b_s"]:SMALL_ROW0["b_s"] + N_HEADS, :] = bs[...]
        for c in range(3):
            put(SMALL_ROW0["conv_w"] + c * (D_MODEL // 128), cw[c:c + 1, :])
        put(SMALL_ROW0["conv_b"], cb[...])
        o_ref[SMALL_ROW0["loss"]:SMALL_ROW0["loss"] + 8, :] = jnp.broadcast_to(loss[...], (8, 128))

    return pl.pallas_call(
        body, name="pack_small", out_shape=jax.ShapeDtypeStruct((SMALL_TOTAL, 128), F32), compiler_params=_params(),
    )(d_ln_g, d_ln_b, d_bg, d_lnv_g, d_lnv_b, d_ws, d_bs.reshape(N_HEADS, CHUNK), d_cw, d_cb, loss_part)


def _adam_small(gathered, params):
    names = list(params)
    flat = [a for n in names for a in params[n]]

    def body(*refs):
        ga_ref = refs[0]
        ins = refs[1:1 + 3 * len(names)]
        outs = refs[1 + 3 * len(names):-1]
        gs_ref = refs[-1]
        g = ga_ref[0]
        for k in range(1, 8):
            g = g + ga_ref[k]
        gs_ref[...] = g
        for i, name in enumerate(names):
            w_ref, m_ref, v_ref = ins[3 * i:3 * i + 3]
            o_refs = outs[4 * i:4 * i + 4]
            row0 = SMALL_ROW0[name]
            if name in LANE_MAJOR:
                pieces = [((slice(None), cols), slice(row0 + q, row0 + q + 1))
                          for q, cols in _lane_pieces(w_ref.shape[1])]
            elif name == "w_s":
                pieces = [((0, h), slice(row0 + h * CHUNK, row0 + (h + 1) * CHUNK)) for h in range(N_HEADS)]
            else:
                pieces = [((0,), slice(row0, row0 + N_HEADS))]
            for idx, rows in pieces:
                gp = gs_ref[rows, :]
                res = (gp,) + _adam_math(w_ref[idx], gp, m_ref[idx], v_ref[idx])
                for o_ref, val in zip(o_refs, res):
                    o_ref[idx] = val
        gcw_ref, loss_ref = outs[-2:]
        for c in range(3):
            for q, cols in _lane_pieces(D_MODEL):
                r = SMALL_ROW0["conv_w"] + c * (D_MODEL // 128) + q
                gcw_ref[c:c + 1, cols] = gs_ref[r:r + 1, :]
        loss_ref[...] = gs_ref[SMALL_ROW0["loss"]:SMALL_ROW0["loss"] + 1, :]

    out_shape = [jax.ShapeDtypeStruct(params[n][0].shape, F32) for n in names for _ in range(4)]
    out_shape += [jax.ShapeDtypeStruct((3, D_MODEL), F32), jax.ShapeDtypeStruct((1, 128), F32)]
    res = pl.pallas_call(
        body, name="adam_small", out_shape=tuple(out_shape),
        scratch_shapes=[pltpu.VMEM((SMALL_TOTAL, 128), F32)], compiler_params=_params(),
    )(gathered, *flat)
    return {n: res[4 * i:4 * i + 4] for i, n in enumerate(names)}, res[-2], res[-1]


def _adam_conv_w(g_all, chip, w, m, v):
    cols = w.shape[2]

    def body(c_ref, g_ref, w_ref, m_ref, v_ref, go_ref, d_ref, mo_ref, vo_ref):
        g = g_ref[...]
        d, mn, vn = _adam_math(w_ref[...], g, m_ref[...], v_ref[...])
        go_ref[...] = g
        d_ref[...] = d
        mo_ref[...] = mn
        vo_ref[...] = vn

    own = pl.BlockSpec((None, 3, cols), lambda i, c_ref: (0, 0, 0))
    return pl.pallas_call(
        body, name="adam_conv_w",
        grid_spec=pltpu.PrefetchScalarGridSpec(
            num_scalar_prefetch=1, grid=(1,),
            in_specs=[pl.BlockSpec((3, cols), lambda i, c_ref: (0, c_ref[0])), own, own, own],
            out_specs=(own,) * 4),
        out_shape=(jax.ShapeDtypeStruct(w.shape, F32),) * 4,
        compiler_params=_params("arbitrary"),
    )(chip, g_all, w, m, v)


def kernel(x, w_in, b_gate, ln_v_g, ln_v_b, w_s, b_s, conv_w, conv_b, w_oa, w_ob, w_out, ln_g, ln_b, loss_target, m_w_in, m_b_gate, m_ln_v_g, m_ln_v_b, m_w_s, m_b_s, m_conv_w, m_conv_b, m_w_oa, m_w_ob, m_w_out, m_ln_g, m_ln_b, v_w_in, v_b_gate, v_ln_v_g, v_ln_v_b, v_w_s, v_b_s, v_conv_w, v_conv_b, v_w_oa, v_w_ob, v_w_out, v_ln_g, v_ln_b):
    t = x.shape[1]
    x2 = x[0]
    target = loss_target[0]
    chip = 2 * lax.axis_index("x") + lax.axis_index("y")
    conv_cols = conv_w.shape[2]

    chip1 = chip.astype(jnp.int32).reshape(1)
    w_pre = _cast_into_columns(w_in[0], chip1, N_CHIPS, 256, "cast_w_in")
    wo_b, wm = _prep_small_weights(w_oa[0], w_ob[0], w_out[0], w_s[0])
    conv_w8 = jnp.concatenate([conv_w[0], jnp.zeros((5, conv_cols), F32)], axis=0)
    bs_col = b_s[0].reshape(N_HEADS, CHUNK, 1)
    bg = b_gate.reshape(2, 1, D_MODEL)

    xb, xt = _cast_and_transpose(x2, min(512, t))
    p, w_full, wo_full, cw_full = _gather_and_project(xb, w_pre, wo_b, conv_w8, min(4096, t))
    wo3 = wo_full.reshape(3, D_MODEL, D_MODEL)
    conv_w_all = jnp.transpose(cw_full[:, :3, :], (1, 0, 2)).reshape(3, D_MODEL)
    tm_a = min(2048, t)
    ya = _mixer_a_forward(p, wm, bs_col, ln_v_g, ln_v_b, tm_a)
    tm_b = min(1024, t)
    yb = _mixer_b_forward(p, conv_w_all, conv_b, min(2048, t), 512)
    tm_m = min(1024, t)
    merged, oab = _merge_forward(ya, yb, wo3, p, bg, tm_m, 512)

    drb, gx, dmerged, d_ln_g, d_ln_b, loss_part = _head(merged, wo3, x2, target, ln_g, ln_b, min(512, t))
    doab, dp, d_bg, dyab = _gate_and_branch_backward(dmerged, oab, p, bg, wo3, min(256, t))
    dp, d_ws, d_bs, d_lnv_g, d_lnv_b = _mixer_a_backward(p, dyab, wm, bs_col, ln_v_g, ln_v_b, dp, min(1024, t))
    dp, d_cw, d_cb = _mixer_b_backward(p, dyab, conv_w_all, conv_b, dp, tm_b, 512)

    tk = min(2048, t)
    g_in = _win_grad(xt, dp, 1024, tk)
    g_oa = _weight_grad(ya, 0, doab, 0, 1024, 1024, tk, "grad_w_oa")
    g_ob = _weight_grad(yb, 0, doab, 1, 1024, 1024, tk, "grad_w_ob")
    g_out, r_in = _weight_grad(merged, 0, drb, 0, 1024, 1024, tk, "grad_w_out", exchange=g_in)
    r_o = _exchange_halves((g_oa, g_ob, g_out))
    s_in, s_o = _add_halves(g_in, (g_oa, g_ob, g_out), r_in, r_o)
    small_part = _pack_small(d_ln_g, d_ln_b, d_bg, d_lnv_g, d_lnv_b, d_ws, d_bs, d_cw, d_cb, loss_part)
    gx, q_in, q_o, gathered = _input_grad_and_scatter(dp, w_full, gx, s_in, s_o, small_part, min(1024, t))
    f_in, f_o = _sum_chips(q_in, q_o, s_in, s_o)
    gsum_in, gsum_o = _share_halves(f_in, f_o)

    big = {}
    grad_x, big["w_in"] = _input_grad_tail_and_adam(dp, w_full, gx, w_in[0], gsum_in, m_w_in[0], v_w_in[0],
                                                    min(32, t // 128))
    for n, (name, w, m, v) in enumerate((("w_oa", w_oa, m_w_oa, v_w_oa), ("w_ob", w_ob, m_w_ob, v_w_ob),
                                         ("w_out", w_out, m_w_out, v_w_out))):
        big[name] = _adam_rows(w[0], gsum_o, m[0], v[0], 256, "adam_" + name, g_sel=n)

    small, g_conv_w, loss_row = _adam_small(gathered, {
        "ln_g": (ln_g, m_ln_g, v_ln_g), "ln_b": (ln_b, m_ln_b, v_ln_b), "b_gate": (b_gate, m_b_gate, v_b_gate),
        "ln_v_g": (ln_v_g, m_ln_v_g, v_ln_v_g), "ln_v_b": (ln_v_b, m_ln_v_b, v_ln_v_b),
        "w_s": (w_s, m_w_s, v_w_s), "b_s": (b_s, m_b_s, v_b_s), "conv_b": (conv_b, m_conv_b, v_conv_b)})
    small["conv_w"] = _adam_conv_w(g_conv_w, chip1, conv_w, m_conv_w, v_conv_w)
    loss = loss_row[0, 0]

    order = ("w_in", "b_gate", "ln_v_g", "ln_v_b", "w_s", "b_s", "conv_w", "conv_b", "w_oa", "w_ob", "w_out",
             "ln_g", "ln_b")
    outs = [loss, grad_x[None]]
    for which in range(4):
        for name in order:
            outs.append(big[name][which][None] if name in big else small[name][which])
    return tuple(outs)
```

```python
import functools
import math

import jax
import jax.numpy as jnp
from jax import lax
from jax.experimental import pallas as pl
from jax.experimental.pallas import tpu as pltpu

F32 = jnp.float32
BF16 = jnp.bfloat16

D_MODEL = 2048
N_HEADS = 8
HEAD_DIM = D_MODEL // N_HEADS
CHUNK = 128
N_SEG = 9
N_CHIPS = 4
SHARD_COLS = N_SEG * D_MODEL // N_CHIPS
COL_BLOCK = 512
BLOCKS_PER_SHARD = SHARD_COLS // COL_BLOCK
BLOCKS_PER_SEG = D_MODEL // COL_BLOCK
SUBS = 3
SHARD_ROWS = D_MODEL // N_CHIPS
N_SLOTS = 12
BLOCK_A, BLOCK_G, BLOCK_B = 0, 2, 2
SLOT_OF_SEG = (0, 1, 2, 8, 9, 10, 11, 4, 5)
DN_ALPHA = 2.0 ** 0.25
LN_EPS = 1e-5
GELU_K = math.sqrt(2.0 / math.pi)
GELU_C = 0.044715

ADAM_LR = 0.001
ADAM_B1 = 0.9
ADAM_B2 = 0.999
ADAM_EPS = 1e-08
ADAM_WD = 0.01
ADAM_STEP = 10
ADAM_C1 = 1.0 / (1.0 - ADAM_B1 ** ADAM_STEP)
ADAM_C2 = 1.0 / (1.0 - ADAM_B2 ** ADAM_STEP)

VMEM_LIMIT = 60 * 1024 * 1024
MESH = pl.DeviceIdType.MESH
ANY = pl.BlockSpec(memory_space=pl.ANY)

SMALL_ROWS = (("ln_g", 16), ("ln_b", 16), ("b_gate", 32), ("ln_v_g", 16), ("ln_v_b", 16),
              ("w_s", 1024), ("b_s", 8), ("conv_w", 48), ("conv_b", 16), ("loss", 8))
SMALL_TOTAL = sum(r for _, r in SMALL_ROWS)


def _params(*sem):
    return pltpu.CompilerParams(dimension_semantics=sem, vmem_limit_bytes=VMEM_LIMIT)


def _sigmoid(x):
    return 1.0 / (1.0 + jnp.exp(-x))


def _gelu_gate(x, x2):
    return 1.0 / (1.0 + jnp.exp(x * ((-2.0 * GELU_K) + (-2.0 * GELU_K * GELU_C) * x2)))


def _gelu_and_grad(x):
    x2 = x * x
    s = _gelu_gate(x, x2)
    g = x * s
    dg = s + g * (1.0 - s) * ((2.0 * GELU_K) + (6.0 * GELU_K * GELU_C) * x2)
    return g, dg


def _silu_grad(sig, sz):
    return sig + sz * (1.0 - sig)


def _dot(a, b, ca=1, cb=0):
    return lax.dot_general(a, b, (((ca,), (cb,)), ((), ())), preferred_element_type=F32)


def _cast_and_transpose(x, tm):
    t, d = x.shape

    def body(x_ref, o_ref, ot_ref):
        v = x_ref[...]
        o_ref[...] = v.astype(BF16)
        ot_ref[...] = v.T.astype(BF16)

    return pl.pallas_call(
        body, name="cast_x", grid=(t // tm,),
        in_specs=[pl.BlockSpec((tm, d), lambda i: (i, 0))],
        out_specs=(pl.BlockSpec((tm, d), lambda i: (i, 0)), pl.BlockSpec((d, tm), lambda i: (0, i))),
        out_shape=(jax.ShapeDtypeStruct((t, d), BF16), jax.ShapeDtypeStruct((d, t), BF16)),
        compiler_params=_params("parallel"),
    )(x)


def _cast_into_columns(w, slot, n_slots, tm, name):
    r, c = w.shape

    def body(s_ref, w_ref, o_ref):
        o_ref[...] = w_ref[...].astype(BF16)

    return pl.pallas_call(
        body, name=name,
        grid_spec=pltpu.PrefetchScalarGridSpec(
            num_scalar_prefetch=1, grid=(r // tm,),
            in_specs=[pl.BlockSpec((tm, c), lambda i, s_ref: (i, 0))],
            out_specs=pl.BlockSpec((tm, c), lambda i, s_ref: (i, s_ref[0]))),
        out_shape=jax.ShapeDtypeStruct((r, n_slots * c), BF16),
        compiler_params=_params("parallel"),
    )(slot, w)


def _prep_small_weights(w_oa, w_ob, w_out, w_s):
    rows = w_oa.shape[0]

    def body(a_ref, b_ref, c_ref, ws_ref, wo_ref, wm_ref):
        wo_ref[0] = a_ref[...].astype(BF16)
        wo_ref[1] = b_ref[...].astype(BF16)
        wo_ref[2] = c_ref[...].astype(BF16)
        t = lax.broadcasted_iota(jnp.int32, (CHUNK, CHUNK), 0)
        s = lax.broadcasted_iota(jnp.int32, (CHUNK, CHUNK), 1)
        for h in range(N_HEADS):
            wm_ref[h] = jnp.where(s <= t, ws_ref[h], 0.0).astype(BF16)

    return pl.pallas_call(
        body, name="prep_small_weights",
        out_shape=(jax.ShapeDtypeStruct((3, rows, D_MODEL), BF16),
                   jax.ShapeDtypeStruct((N_HEADS, CHUNK, CHUNK), BF16)),
        compiler_params=_params(),
    )(w_oa, w_ob, w_out, w_s)


def _mesh_pos():
    x, y, c = lax.axis_index("x"), lax.axis_index("y"), lax.axis_index("c")
    chips = [(1 - x, y), (x, 1 - y), (1 - x, 1 - y)]
    return x, y, c, chips


def _slot_of_seg(seg):
    return jnp.where(seg < 3, seg, jnp.where(seg < 7, seg + 5, seg - 3))


def _gather_and_project(xb, w_pre, wo_b, conv_w8, tm):
    t = xb.shape[0]
    d, sc = w_pre.shape[0], w_pre.shape[1] // N_CHIPS
    rows = wo_b.shape[1]
    hd, hr = d // 2, rows // 2
    nj = BLOCKS_PER_SHARD // SUBS
    pc = nj * COL_BLOCK
    units = N_CHIPS * SUBS
    ni = t // tm
    total = units * ni * nj
    mx, my = lax.axis_index("x"), lax.axis_index("y")
    order = jnp.stack([2 * mx + my, 2 * (1 - mx) + my, 2 * mx + (1 - my),
                       2 * (1 - mx) + (1 - my)]).astype(jnp.int32)

    def body(order_ref, x_ref, wpre_ref, wo_ref, cw_ref, p_ref, wf_ref, wof_ref, cwf_ref,
             wbuf, wsem, xbuf, xsem, send_sems, recv_sems, local_sems):
        x, y, c, chips = _mesh_pos()
        me = 2 * x + y
        sibling = (x, y, 1 - c)
        u, i, j = pl.program_id(0), pl.program_id(1), pl.program_id(2)
        n = (u * ni + i) * nj + j
        m = u * ni + i

        def rows_start(m_):
            pltpu.make_async_copy(x_ref.at[pl.ds(pl.multiple_of(lax.rem(m_, ni) * tm, tm), tm), :],
                                  xbuf.at[lax.rem(m_, 2)], xsem.at[lax.rem(m_, 2)]).start()

        def chip_of(q):
            return 2 * chips[q][0] + chips[q][1]

        def piece(ref, k, cc, r):
            return ref.at[pl.ds(cc * hd, hd), pl.ds(pl.multiple_of(k * sc + r * pc, COL_BLOCK), pc)]

        def wo_half(ref4, k, cc):
            return ref4.at[:, k, pl.ds(cc * hr, hr), :]

        def rcopy(sem, src, dst, to):
            return pltpu.make_async_remote_copy(src_ref=src, dst_ref=dst, send_sem=send_sems.at[sem],
                                                recv_sem=recv_sems.at[sem], device_id=to, device_id_type=MESH)

        def w_send(q, r):
            return rcopy(q * SUBS + r, piece(wpre_ref, me, c, r), piece(wf_ref, me, c, r), (*chips[q], c))

        def w_landed(q, r):
            return rcopy(q * SUBS + r, piece(wpre_ref, me, c, r), piece(wf_ref, chip_of(q), c, r), sibling)

        def w_forward(q, r, cc):
            ref = piece(wf_ref, chip_of(q), cc, r)
            return rcopy(9 + q * SUBS + r, ref, ref, sibling)

        def wo_send(q):
            return rcopy(18 + q, wo_ref.at[:, pl.ds(c * hr, hr), :], wo_half(wof_ref, me, c), (*chips[q], c))

        def wo_landed(q):
            return rcopy(18 + q, wo_ref.at[:, pl.ds(c * hr, hr), :], wo_half(wof_ref, chip_of(q), c), sibling)

        def wo_forward(q, cc):
            ref = wo_half(wof_ref, chip_of(q), cc)
            return rcopy(21 + q, ref, ref, sibling)

        def conv_send(q):
            return rcopy(24 + q, cw_ref, cwf_ref.at[me], (*chips[q], c))

        def local_copies():
            return [pltpu.make_async_copy(wo_ref, wof_ref.at[:, me], local_sems.at[0]),
                    pltpu.make_async_copy(cw_ref, cwf_ref.at[me], local_sems.at[1])]

        def tile_start(u_, j_, slot):
            g = order_ref[u_ // SUBS] * BLOCKS_PER_SHARD + lax.rem(u_, SUBS) * nj + j_
            cols = pl.ds(pl.multiple_of(g * COL_BLOCK, COL_BLOCK), COL_BLOCK)

            @pl.when(u_ < SUBS)
            def _():
                pltpu.make_async_copy(wpre_ref.at[:, cols], wbuf.at[slot], wsem.at[slot]).start()

            @pl.when(u_ >= SUBS)
            def _():
                pltpu.make_async_copy(wf_ref.at[:, cols], wbuf.at[slot], wsem.at[slot]).start()

        def end_of(u_):
            return jnp.logical_and(u == u_, jnp.logical_and(i == ni - 1, j == nj - 1))

        @pl.when(n == 0)
        def _():
            for cp in local_copies():
                cp.start()
            for r in range(SUBS):
                for q in (0, 1):
                    w_send(q, r).start()
            for q in range(3):
                conv_send(q).start()
            tile_start(0, 0, 0)
            rows_start(0)

        def pass_on(q, r):
            w_landed(q, r).wait_recv()
            w_forward(q, r, c).start()

        for u_ in range(1, units - 1):
            @pl.when(end_of(u_))
            def _(u_=u_):
                if u_ <= SUBS:
                    for q in (0, 1):
                        pass_on(q, u_ - 1)
                if u_ == SUBS:
                    for r in range(SUBS):
                        for q in (0, 1):
                            w_send(q, r).wait_send()
                    for r in range(SUBS):
                        w_send(2, r).start()
                if 2 * SUBS - 1 <= u_ <= 3 * SUBS - 2:
                    pass_on(2, u_ - (2 * SUBS - 1))
                if u_ == 3 * SUBS - 2:
                    for r in range(SUBS):
                        w_send(2, r).wait_send()
                    for q in range(3):
                        wo_send(q).start()
                nxt = u_ + 1
                if nxt >= SUBS:
                    w_forward(nxt // SUBS - 1, nxt % SUBS, 1 - c).wait_recv()

        n1 = n + 1

        @pl.when(n1 < total)
        def _():
            tile_start(n1 // (ni * nj), lax.rem(n1, nj), lax.rem(n1, 2))

        xslot = lax.rem(m, 2)

        @pl.when(j == 0)
        def _():
            @pl.when(m + 1 < units * ni)
            def _():
                rows_start(m + 1)

            pltpu.make_async_copy(x_ref.at[pl.ds(0, tm), :], xbuf.at[xslot], xsem.at[xslot]).wait()

        slot = lax.rem(n, 2)
        pltpu.make_async_copy(wpre_ref.at[:, pl.ds(0, COL_BLOCK)], wbuf.at[slot], wsem.at[slot]).wait()
        p_ref[...] = _dot(xbuf[xslot], wbuf[slot]).astype(BF16)

        @pl.when(n == total - 1)
        def _():
            for q in range(3):
                wo_landed(q).wait_recv()
                wo_forward(q, c).start()
            for q in range(3):
                wo_forward(q, 1 - c).wait_recv()
                rcopy(24 + q, cw_ref, cwf_ref.at[chip_of(q)], sibling).wait_recv()
            for q in range(3):
                for r in range(SUBS):
                    w_forward(q, r, c).wait_send()
                wo_send(q).wait_send()
                wo_forward(q, c).wait_send()
                conv_send(q).wait_send()
            for cp in local_copies():
                cp.wait()

    def p_map(u, i, j, o):
        g = o[u // SUBS] * BLOCKS_PER_SHARD + lax.rem(u, SUBS) * nj + j
        return (_slot_of_seg(g // BLOCKS_PER_SEG), i, lax.rem(g, BLOCKS_PER_SEG))

    return pl.pallas_call(
        body, name="gather_and_project",
        grid_spec=pltpu.PrefetchScalarGridSpec(
            num_scalar_prefetch=1, grid=(units, ni, nj),
            in_specs=[ANY, ANY, ANY, ANY],
            out_specs=(pl.BlockSpec((None, tm, COL_BLOCK), p_map), ANY, ANY, ANY),
            scratch_shapes=[pltpu.VMEM((2, d, COL_BLOCK), BF16), pltpu.SemaphoreType.DMA((2,)),
                            pltpu.VMEM((2, tm, D_MODEL), BF16), pltpu.SemaphoreType.DMA((2,)),
                            pltpu.SemaphoreType.DMA((27,)), pltpu.SemaphoreType.DMA((27,)),
                            pltpu.SemaphoreType.DMA((2,))]),
        out_shape=(jax.ShapeDtypeStruct((N_SLOTS, t, D_MODEL), BF16),
                   jax.ShapeDtypeStruct((d, N_CHIPS * sc), BF16),
                   jax.ShapeDtypeStruct((3, N_CHIPS, rows, D_MODEL), BF16),
                   jax.ShapeDtypeStruct((N_CHIPS,) + conv_w8.shape, F32)),
        input_output_aliases={2: 1},
        compiler_params=pltpu.CompilerParams(dimension_semantics=("arbitrary",) * 3, vmem_limit_bytes=VMEM_LIMIT,
                                             has_side_effects=True),
    )(order, xb, w_pre, wo_b, conv_w8)


def _exchange_halves(g_out):
    hr = SHARD_ROWS // 2

    def body(g_ref, r_ref, send_sem, recv_sem):
        x, y, c, _ = _mesh_pos()
        cp = pltpu.make_async_remote_copy(src_ref=g_ref.at[:, 1 - c], dst_ref=r_ref, send_sem=send_sem,
                                          recv_sem=recv_sem, device_id=(x, y, 1 - c), device_id_type=MESH)
        cp.start()
        cp.wait()

    return pl.pallas_call(
        body, name="rs_exchange_halves",
        in_specs=[ANY], out_specs=ANY,
        out_shape=jax.ShapeDtypeStruct((N_CHIPS, hr, D_MODEL), BF16),
        scratch_shapes=[pltpu.SemaphoreType.DMA, pltpu.SemaphoreType.DMA],
        compiler_params=pltpu.CompilerParams(has_side_effects=True),
    )(g_out.reshape(N_CHIPS, 2, hr, D_MODEL))


def _add_halves(g_in, g_o, r_in, r_o):
    d, c9 = g_in.shape
    hd = d // 2
    hr = SHARD_ROWS // 2
    core = lax.axis_index("c").astype(jnp.int32).reshape(1)
    tm = min(512, hd)
    nb = hd // tm

    def body_in(c_ref, g_ref, r_ref, o_ref):
        o_ref[...] = (g_ref[...].astype(F32) + r_ref[...].astype(F32)).astype(BF16)

    s_in = pl.pallas_call(
        body_in, name="rs_add_halves_in",
        grid_spec=pltpu.PrefetchScalarGridSpec(
            num_scalar_prefetch=1, grid=(N_CHIPS, nb),
            in_specs=[pl.BlockSpec((tm, SHARD_COLS), lambda k, i, c_ref: (c_ref[0] * nb + i, k)),
                      pl.BlockSpec((tm, SHARD_COLS), lambda k, i, c_ref: (i, k))],
            out_specs=pl.BlockSpec((None, tm, SHARD_COLS), lambda k, i, c_ref: (k, i, 0))),
        out_shape=jax.ShapeDtypeStruct((N_CHIPS, hd, SHARD_COLS), BF16),
        compiler_params=_params("parallel", "parallel"),
    )(core, g_in, r_in)

    g_o4 = [g.reshape(N_CHIPS, 2, hr, D_MODEL) for g in g_o]

    def body_o(c_ref, ga_ref, gb_ref, gc_ref, ra_ref, rb_ref, rc_ref, o_ref):
        for n, (g_ref, r_ref) in enumerate(((ga_ref, ra_ref), (gb_ref, rb_ref), (gc_ref, rc_ref))):
            o_ref[n] = (g_ref[...].astype(F32) + r_ref[...].astype(F32)).astype(BF16)

    gspec = pl.BlockSpec((None, None, hr, D_MODEL), lambda k, c_ref: (k, c_ref[0], 0, 0))
    rspec = pl.BlockSpec((None, hr, D_MODEL), lambda k, c_ref: (k, 0, 0))
    s_o = pl.pallas_call(
        body_o, name="rs_add_halves_o",
        grid_spec=pltpu.PrefetchScalarGridSpec(
            num_scalar_prefetch=1, grid=(N_CHIPS,),
            in_specs=[gspec] * 3 + [rspec] * 3,
            out_specs=pl.BlockSpec((3, None, hr, D_MODEL), lambda k, c_ref: (0, k, 0, 0))),
        out_shape=jax.ShapeDtypeStruct((3, N_CHIPS, hr, D_MODEL), BF16),
        compiler_params=_params("parallel"),
    )(core, *g_o4, *r_o)
    return s_in, s_o


def _sum_chips(r_in, r_o, s_in, s_o):
    _, hd, sc = r_in.shape
    hr = r_o.shape[2]
    tm = min(256, hd)
    nb = hd // tm
    pos = jnp.stack([2 * lax.axis_index("x") + lax.axis_index("y"), lax.axis_index("c")]).astype(jnp.int32)

    def chip_sum(pos_ref, r_ref, s_ref):
        acc = None
        for k in range(N_CHIPS):
            term = jnp.where(pos_ref[0] == k, s_ref[...], r_ref[k]).astype(F32)
            acc = term if acc is None else acc + term
        return acc

    def body_in(pos_ref, r_ref, s_ref, o_ref):
        o_ref[...] = chip_sum(pos_ref, r_ref, s_ref)

    f_in = pl.pallas_call(
        body_in, name="rs_sum_chips_in",
        grid_spec=pltpu.PrefetchScalarGridSpec(
            num_scalar_prefetch=1, grid=(nb,),
            in_specs=[pl.BlockSpec((N_CHIPS, tm, sc), lambda i, p: (0, i, 0)),
                      pl.BlockSpec((None, tm, sc), lambda i, p: (p[0], i, 0))],
            out_specs=pl.BlockSpec((tm, sc), lambda i, p: (p[1] * nb + i, 0))),
        out_shape=jax.ShapeDtypeStruct((2 * hd, sc), F32),
        compiler_params=_params("parallel"),
    )(pos, r_in, s_in)

    def body_o(pos_ref, r_ref, s_ref, o_ref):
        o_ref[...] = chip_sum(pos_ref, r_ref, s_ref)

    f_o = pl.pallas_call(
        body_o, name="rs_sum_chips_o",
        grid_spec=pltpu.PrefetchScalarGridSpec(
            num_scalar_prefetch=1, grid=(3,),
            in_specs=[pl.BlockSpec((N_CHIPS, None, hr, D_MODEL), lambda n, p: (0, n, 0, 0)),
                      pl.BlockSpec((None, None, hr, D_MODEL), lambda n, p: (n, p[0], 0, 0))],
            out_specs=pl.BlockSpec((None, hr, D_MODEL), lambda n, p: (n, p[1], 0))),
        out_shape=jax.ShapeDtypeStruct((3, 2 * hr, D_MODEL), F32),
        compiler_params=_params("parallel"),
    )(pos, r_o, s_o)
    return f_in, f_o


def _share_halves(f_in, f_o):
    hd, sc = f_in.shape[0] // 2, f_in.shape[1]
    hr = f_o.shape[1] // 2

    def body(fi_ref, fo_ref, gi_ref, go_ref, send_sems, recv_sems):
        x, y, c, _ = _mesh_pos()
        sibling = (x, y, 1 - c)

        def halves(cc):
            rows_i, rows_o = pl.ds(cc * hd, hd), pl.ds(cc * hr, hr)
            return (fi_ref.at[rows_i, :], gi_ref.at[rows_i, :]), (fo_ref.at[:, rows_o, :], go_ref.at[:, rows_o, :])

        def copies(cc):
            return [pltpu.make_async_remote_copy(src_ref=src, dst_ref=dst, send_sem=send_sems.at[n],
                                                 recv_sem=recv_sems.at[n], device_id=sibling, device_id_type=MESH)
                    for n, (src, dst) in enumerate(halves(cc))]

        sends = copies(c)
        for cp in sends:
            cp.start()
        for cp in copies(1 - c):
            cp.wait_recv()
        for cp in sends:
            cp.wait_send()

    return pl.pallas_call(
        body, name="rs_share_halves",
        in_specs=[ANY, ANY], out_specs=(ANY, ANY),
        out_shape=(jax.ShapeDtypeStruct(f_in.shape, F32), jax.ShapeDtypeStruct(f_o.shape, F32)),
        scratch_shapes=[pltpu.SemaphoreType.DMA((2,)), pltpu.SemaphoreType.DMA((2,))],
        input_output_aliases={0: 0, 1: 1},
        compiler_params=pltpu.CompilerParams(has_side_effects=True),
    )(f_in, f_o)


def _merge_forward(ya, yb, wo3, p, bg, tm, tn):
    t = ya.shape[0]

    def body(ya_ref, yb_ref, wa_ref, wb_ref, g_ref, bg_ref, m_ref, oab_ref):
        oa = _dot(ya_ref[...], wa_ref[...])
        ob = _dot(yb_ref[...], wb_ref[...])
        ga = _sigmoid(g_ref[0].astype(F32) + bg_ref[0])
        gb = _sigmoid(g_ref[1].astype(F32) + bg_ref[1])
        m_ref[...] = (ga * oa + gb * ob).astype(BF16)
        oab_ref[0] = oa.astype(BF16)
        oab_ref[1] = ob.astype(BF16)

    return pl.pallas_call(
        body, name="merge_forward", grid=(t // tm, D_MODEL // tn),
        in_specs=[pl.BlockSpec((tm, D_MODEL), lambda i, j: (i, 0)),
                  pl.BlockSpec((tm, D_MODEL), lambda i, j: (i, 0)),
                  pl.BlockSpec((None, D_MODEL, tn), lambda i, j: (0, 0, j)),
                  pl.BlockSpec((None, D_MODEL, tn), lambda i, j: (1, 0, j)),
                  pl.BlockSpec((2, tm, tn), lambda i, j: (BLOCK_G, i, j)),
                  pl.BlockSpec((2, 1, tn), lambda i, j: (0, 0, j))],
        out_specs=(pl.BlockSpec((tm, tn), lambda i, j: (i, j)),
                   pl.BlockSpec((2, tm, tn), lambda i, j: (0, i, j))),
        out_shape=(jax.ShapeDtypeStruct((t, D_MODEL), BF16), jax.ShapeDtypeStruct((2, t, D_MODEL), BF16)),
        compiler_params=_params("parallel", "parallel"),
    )(ya, yb, wo3, wo3, p, bg)


HEAD_ROWS = 256


def _head(merged, wo3, x, target, ln_g, ln_b, tm):
    t = x.shape[0]
    inv_d = 1.0 / D_MODEL

    def body(m_ref, w_ref, x_ref, t_ref, g_ref, b_ref, dr_ref, gx_ref, dm_ref, dg_ref, db_ref, loss_ref):
        i = pl.program_id(0)

        @pl.when(i == 0)
        def _():
            dg_ref[...] = jnp.zeros_like(dg_ref)
            db_ref[...] = jnp.zeros_like(db_ref)
            loss_ref[...] = jnp.zeros_like(loss_ref)

        w = w_ref[...]
        g = g_ref[...]
        tiles = [slice(r0, r0 + HEAD_ROWS) for r0 in range(0, tm, HEAD_ROWS)]
        firsts = [_dot(m_ref[rows, :], w) for rows in tiles]
        for rows, out in zip(tiles, firsts):
            r = DN_ALPHA * x_ref[rows, :] + out
            mu = jnp.mean(r, axis=-1, keepdims=True)
            xc = r - mu
            var = jnp.mean(xc * xc, axis=-1, keepdims=True)
            rstd = lax.rsqrt(var + LN_EPS)
            xhat = xc * rstd
            e = xhat * g + b_ref[...] - t_ref[rows, :]
            se = jnp.sum(jnp.sum(e * e, axis=1, keepdims=True), axis=0, keepdims=True)
            loss_ref[...] += jnp.broadcast_to((0.5 * inv_d) * se, loss_ref.shape)
            dy = e * inv_d
            db_ref[...] += jnp.sum(dy, axis=0, keepdims=True)
            dg_ref[...] += jnp.sum(dy * xhat, axis=0, keepdims=True)
            dxh = dy * g
            m1 = jnp.mean(dxh, axis=-1, keepdims=True)
            m2 = jnp.mean(dxh * xhat, axis=-1, keepdims=True)
            dr = rstd * (dxh - m1 - xhat * m2)
            gx_ref[rows, :] = DN_ALPHA * dr
            drb = dr.astype(BF16)
            dr_ref[rows, :] = drb
            dm_ref[rows, :] = _dot(drb, w, 1, 1).astype(BF16)

    row = pl.BlockSpec((tm, D_MODEL), lambda i: (i, 0))
    vec = pl.BlockSpec((1, D_MODEL), lambda i: (0, 0))
    return pl.pallas_call(
        body, name="head", grid=(t // tm,),
        in_specs=[row, pl.BlockSpec((None, D_MODEL, D_MODEL), lambda i: (2, 0, 0), pipeline_mode=pl.Buffered(1)),
                  row, row, vec, vec],
        out_specs=(row, row, row, vec, vec, pl.BlockSpec((1, 128), lambda i: (0, 0))),
        out_shape=(jax.ShapeDtypeStruct((t, D_MODEL), BF16), jax.ShapeDtypeStruct((t, D_MODEL), F32),
                   jax.ShapeDtypeStruct((t, D_MODEL), BF16), jax.ShapeDtypeStruct((1, D_MODEL), F32),
                   jax.ShapeDtypeStruct((1, D_MODEL), F32), jax.ShapeDtypeStruct((1, 128), F32)),
        compiler_params=_params("arbitrary"),
    )(merged, wo3, x, target, ln_g, ln_b)


def _gate_and_branch_backward(dmerged, oab, p, bg, wo3, tm):
    t = dmerged.shape[0]

    def body(dm_ref, oab_ref, g_ref, bg_ref, wa_ref, wb_ref, do_ref, dpg_ref, dbg_ref, dy_ref):
        @pl.when(pl.program_id(0) == 0)
        def _():
            dbg_ref[...] = jnp.zeros_like(dbg_ref)

        dm = dm_ref[...].astype(F32)
        for n, w_ref in enumerate((wa_ref, wb_ref)):
            gate = _sigmoid(g_ref[n].astype(F32) + bg_ref[n])
            d_o = (dm * gate).astype(BF16)
            do_ref[n] = d_o
            dgate = dm * oab_ref[n].astype(F32) * gate * (1.0 - gate)
            dpg_ref[n] = dgate.astype(BF16)
            dbg_ref[n] += jnp.sum(dgate, axis=0, keepdims=True)
            dy_ref[n] = _dot(d_o, w_ref[...], 1, 1).astype(BF16)

    pair = pl.BlockSpec((2, tm, D_MODEL), lambda i: (0, i, 0))
    gates = pl.BlockSpec((2, tm, D_MODEL), lambda i: (BLOCK_G, i, 0))
    vec = pl.BlockSpec((2, 1, D_MODEL), lambda i: (0, 0, 0))

    def weight(n):
        return pl.BlockSpec((None, D_MODEL, D_MODEL), lambda i: (n, 0, 0), pipeline_mode=pl.Buffered(1))

    pair_shape = jax.ShapeDtypeStruct((2, t, D_MODEL), BF16)
    return pl.pallas_call(
        body, name="gate_and_branch_backward", grid=(t // tm,),
        in_specs=[pl.BlockSpec((tm, D_MODEL), lambda i: (i, 0)), pair, gates, vec, weight(0), weight(1)],
        out_specs=(pair, gates, vec, pair),
        out_shape=(pair_shape, jax.ShapeDtypeStruct((N_SLOTS, t, D_MODEL), BF16),
                   jax.ShapeDtypeStruct((2, 1, D_MODEL), F32), pair_shape),
        compiler_params=_params("arbitrary"),
    )(dmerged, oab, p, bg, wo3, wo3)


def _weight_grad(a, a_sel, b, b_sel, tm, tn, tk, name, exchange=None):
    t = a.shape[-2]
    nk = t // tk
    ni, nj = D_MODEL // tm, D_MODEL // tn
    hr = SHARD_ROWS // 2
    if exchange is not None:
        g_in, *g_sq = exchange
        g_sq4 = [g.reshape(N_CHIPS, 2, hr, D_MODEL) for g in g_sq]

    def body(a_ref, b_ref, *rest):
        if exchange is None:
            o_ref, acc_ref = rest
        else:
            n_ex = 1 + len(g_sq4)
            g_refs = rest[:n_ex]
            o_ref = rest[n_ex]
            r_refs = rest[n_ex + 1:2 * n_ex + 1]
            acc_ref, send_sems, recv_sems = rest[2 * n_ex + 1:]
        i, j, k = pl.program_id(0), pl.program_id(1), pl.program_id(2)

        if exchange is not None:
            x, y, c, _ = _mesh_pos()
            hd = g_in.shape[0] // 2
            srcs = [g_refs[0].at[pl.ds((1 - c) * hd, hd), :]] + [g.at[:, 1 - c] for g in g_refs[1:]]
            swaps = [pltpu.make_async_remote_copy(src_ref=src, dst_ref=dst, send_sem=send_sems.at[n],
                                                  recv_sem=recv_sems.at[n], device_id=(x, y, 1 - c),
                                                  device_id_type=MESH)
                     for n, (src, dst) in enumerate(zip(srcs, r_refs))]

            @pl.when(jnp.logical_and(i == 0, jnp.logical_and(j == 0, k == 0)))
            def _():
                for swap in swaps:
                    swap.start()

        @pl.when(k == 0)
        def _():
            acc_ref[...] = jnp.zeros_like(acc_ref)

        acc_ref[...] += _dot(a_ref[...], b_ref[...], 0, 0)

        @pl.when(k == nk - 1)
        def _():
            o_ref[...] = acc_ref[...].astype(BF16)

        if exchange is not None:
            @pl.when(jnp.logical_and(i == ni - 1, jnp.logical_and(j == nj - 1, k == nk - 1)))
            def _():
                for swap in swaps:
                    swap.wait()

    def spec(arr, sel, width, which):
        if arr.ndim == 2:
            return pl.BlockSpec((tk, width), lambda i, j, k: (k, (i, j)[which]))
        return pl.BlockSpec((None, tk, width), lambda i, j, k: (sel, k, (i, j)[which]))

    o_spec = pl.BlockSpec((tm, tn), lambda i, j, k: (i, j))
    o_shape = jax.ShapeDtypeStruct((D_MODEL, D_MODEL), BF16)
    if exchange is None:
        return pl.pallas_call(
            body, name=name, grid=(ni, nj, nk),
            in_specs=[spec(a, a_sel, tm, 0), spec(b, b_sel, tn, 1)],
            out_specs=o_spec, out_shape=o_shape,
            scratch_shapes=[pltpu.VMEM((tm, tn), F32)],
            compiler_params=_params("parallel", "parallel", "arbitrary"),
        )(a, b)
    n_ex = 1 + len(g_sq4)
    r_shapes = [jax.ShapeDtypeStruct((g_in.shape[0] // 2, g_in.shape[1]), BF16)]
    r_shapes += [jax.ShapeDtypeStruct((N_CHIPS, hr, D_MODEL), BF16)] * len(g_sq4)
    return pl.pallas_call(
        body, name=name, grid=(ni, nj, nk),
        in_specs=[spec(a, a_sel, tm, 0), spec(b, b_sel, tn, 1)] + [ANY] * n_ex,
        out_specs=(o_spec,) + (ANY,) * n_ex,
        out_shape=(o_shape, *r_shapes),
        scratch_shapes=[pltpu.VMEM((tm, tn), F32), pltpu.SemaphoreType.DMA((n_ex,)),
                        pltpu.SemaphoreType.DMA((n_ex,))],
        compiler_params=pltpu.CompilerParams(dimension_semantics=("arbitrary",) * 3, vmem_limit_bytes=VMEM_LIMIT,
                                             has_side_effects=True),
    )(a, b, g_in, *g_sq4)


def _win_grad(xt, dp, tn, tk):
    _, t, _ = dp.shape
    nk = t // tk
    per_seg = D_MODEL // tn
    nj = N_SEG * per_seg

    def body(x_ref, dp_ref, o_ref, acc_ref):
        k = pl.program_id(1)

        @pl.when(k == 0)
        def _():
            acc_ref[...] = jnp.zeros_like(acc_ref)

        acc_ref[...] += _dot(x_ref[...], dp_ref[...])

        @pl.when(k == nk - 1)
        def _():
            o_ref[...] = acc_ref[...].astype(BF16)

    return pl.pallas_call(
        body, name="grad_w_in", grid=(nj, nk),
        in_specs=[pl.BlockSpec((D_MODEL, tk), lambda j, k: (0, k)),
                  pl.BlockSpec((None, tk, tn), lambda j, k: (_slot_of_seg(j // per_seg), k, j % per_seg))],
        out_specs=pl.BlockSpec((D_MODEL, tn), lambda j, k: (0, j)),
        out_shape=jax.ShapeDtypeStruct((D_MODEL, N_SEG * D_MODEL), BF16),
        scratch_shapes=[pltpu.VMEM((D_MODEL, tn), F32)],
        compiler_params=_params("parallel", "arbitrary"),
    )(xt, dp)


def _input_grad_and_scatter(dp, w_full, gx, s_in, s_o, small, tm):
    _, t, _ = dp.shape
    ni, nk = t // tm, N_SEG - TAIL_SEGS
    _, hd, sc = s_in.shape
    hr = s_o.shape[2]

    def body(dp_ref, w_ref, gx_ref, si_ref, so_ref, sm_ref, o_ref, ri_ref, ro_ref, ga_ref,
             send_sems, recv_sems, local_sem):
        i, k = pl.program_id(0), pl.program_id(1)
        x, y, c, chips = _mesh_pos()
        me = 2 * x + y
        dev = 4 * x + 2 * y + c

        def peer(r):
            return (x ^ ((r >> 2) & 1), y ^ ((r >> 1) & 1), c ^ (r & 1))

        def sends():
            cps = []
            for q, (cx, cy) in enumerate(chips):
                dest = 2 * cx + cy
                cps.append(pltpu.make_async_remote_copy(src_ref=si_ref.at[dest], dst_ref=ri_ref.at[me],
                                                        send_sem=send_sems.at[q], recv_sem=recv_sems.at[q],
                                                        device_id=(cx, cy, c), device_id_type=MESH))
                cps.append(pltpu.make_async_remote_copy(src_ref=so_ref.at[:, dest], dst_ref=ro_ref.at[me],
                                                        send_sem=send_sems.at[3 + q], recv_sem=recv_sems.at[3 + q],
                                                        device_id=(cx, cy, c), device_id_type=MESH))
            for r in range(1, 8):
                cps.append(pltpu.make_async_remote_copy(src_ref=sm_ref, dst_ref=ga_ref.at[dev],
                                                        send_sem=send_sems.at[5 + r], recv_sem=recv_sems.at[5 + r],
                                                        device_id=peer(r), device_id_type=MESH))
            return cps

        own_small = pltpu.make_async_copy(sm_ref, ga_ref.at[dev], local_sem)

        @pl.when(jnp.logical_and(i == 0, k == 0))
        def _():
            for cp in sends():
                cp.start()
            own_small.start()

        @pl.when(k == 0)
        def _():
            o_ref[...] = gx_ref[...]

        o_ref[...] += _dot(dp_ref[...], w_ref[...], 1, 1)

        @pl.when(jnp.logical_and(i == ni - 1, k == nk - 1))
        def _():
            for q, (cx, cy) in enumerate(chips):
                frm = 2 * cx + cy
                pltpu.make_async_remote_copy(src_ref=si_ref.at[frm], dst_ref=ri_ref.at[frm], send_sem=send_sems.at[q],
                                             recv_sem=recv_sems.at[q], device_id=(x, y, c),
                                             device_id_type=MESH).wait_recv()
                pltpu.make_async_remote_copy(src_ref=so_ref.at[:, frm], dst_ref=ro_ref.at[frm],
                                             send_sem=send_sems.at[3 + q], recv_sem=recv_sems.at[3 + q],
                                             device_id=(x, y, c), device_id_type=MESH).wait_recv()
            for r in range(1, 8):
                px, py, pc = peer(r)
                pltpu.make_async_remote_copy(src_ref=sm_ref, dst_ref=ga_ref.at[4 * px + 2 * py + pc],
                                             send_sem=send_sems.at[5 + r], recv_sem=recv_sems.at[5 + r],
                                             device_id=(x, y, c), device_id_type=MESH).wait_recv()
            for cp in sends():
                cp.wait_send()
            own_small.wait()

    return pl.pallas_call(
        body, name="grad_x_and_scatter", grid=(ni, nk),
        in_specs=[pl.BlockSpec((None, tm, D_MODEL), lambda i, k: (_slot_of_seg(k), i, 0)),
                  pl.BlockSpec((D_MODEL, D_MODEL), lambda i, k: (0, k)),
                  pl.BlockSpec((tm, D_MODEL), lambda i, k: (i, 0), pipeline_mode=pl.Buffered(1)), ANY, ANY, ANY],
        out_specs=(pl.BlockSpec((tm, D_MODEL), lambda i, k: (i, 0)), ANY, ANY, ANY),
        out_shape=(jax.ShapeDtypeStruct((t, D_MODEL), F32),
                   jax.ShapeDtypeStruct((N_CHIPS, hd, sc), BF16),
                   jax.ShapeDtypeStruct((N_CHIPS, 3, hr, D_MODEL), BF16),
                   jax.ShapeDtypeStruct((8,) + small.shape, small.dtype)),
        scratch_shapes=[pltpu.SemaphoreType.DMA((13,)), pltpu.SemaphoreType.DMA((13,)), pltpu.SemaphoreType.DMA],
        input_output_aliases={2: 0},
        compiler_params=pltpu.CompilerParams(dimension_semantics=("arbitrary", "arbitrary"),
                                             vmem_limit_bytes=VMEM_LIMIT, has_side_effects=True),
    )(dp, w_full, gx, s_in, s_o, small)


TAIL_SEGS = 1


def _input_grad_tail_and_adam(dp, w_full, partial, w, g, m, v, steps):
    _, t, _ = dp.shape
    tm = t // steps
    r, c = w.shape
    ra = r // steps
    segs = tuple(range(N_SEG - TAIL_SEGS, N_SEG))

    def body(*refs):
        dp_refs = refs[:TAIL_SEGS]
        w_refs = refs[TAIL_SEGS:2 * TAIL_SEGS]
        part_ref, aw_ref, ag_ref, am_ref, av_ref, o_ref, go_ref, d_ref, mo_ref, vo_ref = refs[2 * TAIL_SEGS:]
        acc = part_ref[...]
        for dp_ref, w_ref in zip(dp_refs, w_refs):
            acc = acc + _dot(dp_ref[...], w_ref[...], 1, 1)
        o_ref[...] = acc
        gv = ag_ref[...]
        d, mn, vn = _adam_math(aw_ref[...], gv, am_ref[...], av_ref[...])
        go_ref[...] = gv
        d_ref[...] = d
        mo_ref[...] = mn
        vo_ref[...] = vn

    row = pl.BlockSpec((tm, D_MODEL), lambda i: (i, 0))
    arow = pl.BlockSpec((ra, c), lambda i: (i, 0))
    ashape = jax.ShapeDtypeStruct((r, c), F32)
    in_specs = [pl.BlockSpec((None, tm, D_MODEL), functools.partial(lambda s, i: (SLOT_OF_SEG[s], i, 0), s))
                for s in segs]
    in_specs += [pl.BlockSpec((D_MODEL, D_MODEL), functools.partial(lambda s, i: (0, s), s),
                              pipeline_mode=pl.Buffered(1)) for s in segs]
    in_specs += [row, arow, arow, arow, arow]
    res = pl.pallas_call(
        body, name="grad_x_tail_and_adam_w_in", grid=(steps,),
        in_specs=in_specs,
        out_specs=(row, arow, arow, arow, arow),
        out_shape=(jax.ShapeDtypeStruct((t, D_MODEL), F32), ashape, ashape, ashape, ashape),
        compiler_params=_params("parallel"),
    )(*([dp] * TAIL_SEGS), *([w_full] * TAIL_SEGS), partial, w, g, m, v)
    return res[0], res[1:]


def _sgu_chunk_forward(u, v, z, wm, bs, lng, lnb):
    ug, dug = _gelu_and_grad(u)
    vg, dvg = _gelu_and_grad(v)
    mu = jnp.mean(vg, axis=-1, keepdims=True)
    xc = vg - mu
    var = jnp.mean(xc * xc, axis=-1, keepdims=True)
    rstd = lax.rsqrt(var + LN_EPS)
    vhat = xc * rstd
    vln = (vhat * lng + lnb).astype(BF16)
    mixed = _dot(wm, vln) + bs
    sig = _sigmoid(z)
    return ug, dug, dvg, rstd, vhat, vln, mixed, sig


def _mixer_a_forward(p_a, wm, bs_col, ln_v_g, ln_v_b, tm):
    t = p_a.shape[1]

    def body(p_ref, wm_ref, bs_ref, g_ref, b_ref, o_ref):
        wm_v, bs_v, lng, lnb = wm_ref[...], bs_ref[...], g_ref[...], b_ref[...]

        def chunk(ci, carry):
            rows = pl.ds(pl.multiple_of(ci * CHUNK, CHUNK), CHUNK)
            u = p_ref[0, rows, :].astype(F32)
            v = p_ref[1, rows, :].astype(F32)
            z = p_ref[2, rows, :].astype(F32)
            ug, _, _, _, _, _, mixed, sig = _sgu_chunk_forward(u, v, z, wm_v, bs_v, lng, lnb)
            o_ref[rows, :] = (ug * mixed * (z * sig)).astype(BF16)
            return carry

        lax.fori_loop(0, tm // CHUNK, chunk, 0, unroll=True)

    return pl.pallas_call(
        body, name="mixer_a_forward", grid=(t // tm, N_HEADS),
        in_specs=[pl.BlockSpec((3, tm, HEAD_DIM), lambda i, h: (0, i, h)),
                  pl.BlockSpec((None, CHUNK, CHUNK), lambda i, h: (h, 0, 0)),
                  pl.BlockSpec((None, CHUNK, 1), lambda i, h: (h, 0, 0)),
                  pl.BlockSpec((1, HEAD_DIM), lambda i, h: (0, h)),
                  pl.BlockSpec((1, HEAD_DIM), lambda i, h: (0, h))],
        out_specs=pl.BlockSpec((tm, HEAD_DIM), lambda i, h: (i, h)),
        out_shape=jax.ShapeDtypeStruct((t, D_MODEL), BF16),
        compiler_params=_params("parallel", "parallel"),
    )(p_a, wm, bs_col, ln_v_g, ln_v_b)


def _mixer_a_backward(p_a, dyab, wm, bs_col, ln_v_g, ln_v_b, dp, tm):
    t = p_a.shape[1]

    def body(p_ref, dy_ref, wm_ref, bs_ref, g_ref, b_ref, dp_in, dp_ref, dws_ref, dbs_ref, dg_ref, db_ref):
        @pl.when(pl.program_id(1) == 0)
        def _():
            dws_ref[...] = jnp.zeros_like(dws_ref)
            dbs_ref[...] = jnp.zeros_like(dbs_ref)
            dg_ref[...] = jnp.zeros_like(dg_ref)
            db_ref[...] = jnp.zeros_like(db_ref)

        wm_v, bs_v, lng, lnb = wm_ref[...], bs_ref[...], g_ref[...], b_ref[...]
        causal = (lax.broadcasted_iota(jnp.int32, (CHUNK, CHUNK), 1)
                  <= lax.broadcasted_iota(jnp.int32, (CHUNK, CHUNK), 0))

        def chunk(ci, carry):
            rows = pl.ds(pl.multiple_of(ci * CHUNK, CHUNK), CHUNK)
            u = p_ref[0, rows, :].astype(F32)
            v = p_ref[1, rows, :].astype(F32)
            z = p_ref[2, rows, :].astype(F32)
            dy = dy_ref[rows, :].astype(F32)
            ug, dug, dvg, rstd, vhat, vln, mixed, sig = _sgu_chunk_forward(u, v, z, wm_v, bs_v, lng, lnb)
            sz = z * sig
            dmixed = dy * ug * sz
            dp_ref[0, rows, :] = (dy * mixed * sz * dug).astype(BF16)
            dp_ref[2, rows, :] = (dy * ug * mixed * _silu_grad(sig, sz)).astype(BF16)
            dbs_ref[...] += jnp.sum(dmixed, axis=1, keepdims=True)
            dmb = dmixed.astype(BF16)
            dws_ref[...] += jnp.where(causal, _dot(dmb, vln, 1, 1), 0.0)
            dvln = _dot(wm_v, dmb, 0, 0)
            db_ref[...] += jnp.sum(dvln, axis=0, keepdims=True)
            dg_ref[...] += jnp.sum(dvln * vhat, axis=0, keepdims=True)
            dvh = dvln * lng
            m1 = jnp.mean(dvh, axis=-1, keepdims=True)
            m2 = jnp.mean(dvh * vhat, axis=-1, keepdims=True)
            dp_ref[1, rows, :] = (rstd * (dvh - m1 - vhat * m2) * dvg).astype(BF16)
            return carry

        lax.fori_loop(0, tm // CHUNK, chunk, 0, unroll=True)

    return pl.pallas_call(
        body, name="mixer_a_backward", grid=(N_HEADS, t // tm),
        in_specs=[pl.BlockSpec((3, tm, HEAD_DIM), lambda h, i: (0, i, h)),
                  pl.BlockSpec((None, tm, HEAD_DIM), lambda h, i: (0, i, h)),
                  pl.BlockSpec((None, CHUNK, CHUNK), lambda h, i: (h, 0, 0)),
                  pl.BlockSpec((None, CHUNK, 1), lambda h, i: (h, 0, 0)),
                  pl.BlockSpec((1, HEAD_DIM), lambda h, i: (0, h)),
                  pl.BlockSpec((1, HEAD_DIM), lambda h, i: (0, h)), ANY],
        out_specs=(pl.BlockSpec((3, tm, HEAD_DIM), lambda h, i: (BLOCK_A, i, h)),
                   pl.BlockSpec((None, CHUNK, CHUNK), lambda h, i: (h, 0, 0)),
                   pl.BlockSpec((None, CHUNK, 1), lambda h, i: (h, 0, 0)),
                   pl.BlockSpec((1, HEAD_DIM), lambda h, i: (0, h)),
                   pl.BlockSpec((1, HEAD_DIM), lambda h, i: (0, h))),
        out_shape=(jax.ShapeDtypeStruct(dp.shape, BF16),
                   jax.ShapeDtypeStruct((N_HEADS, CHUNK, CHUNK), F32),
                   jax.ShapeDtypeStruct((N_HEADS, CHUNK, 1), F32),
                   jax.ShapeDtypeStruct((1, D_MODEL), F32), jax.ShapeDtypeStruct((1, D_MODEL), F32)),
        input_output_aliases={6: 0},
        compiler_params=_params("parallel", "arbitrary"),
    )(p_a, dyab, wm, bs_col, ln_v_g, ln_v_b, dp)


HALO = 16


def _conv_taps(h, halo_h, tm):
    row = lax.broadcasted_iota(jnp.int32, h.shape, 0)
    last1 = halo_h[HALO - 1:HALO, :]
    last2 = halo_h[HALO - 2:HALO - 1, :]
    h1 = jnp.where(row == 0, last1, pltpu.roll(h, 1, 0))
    h2 = jnp.where(row == 0, last2, jnp.where(row == 1, last1, pltpu.roll(h, 2, 0)))
    return h1, h2


def _mixer_b_forward(p_b, conv_w, conv_b, tm, tc):
    t = p_b.shape[1]

    def body(p_ref, halo_ref, w_ref, b_ref, o_ref):
        valid = (pl.program_id(1) > 0).astype(F32)
        h = p_ref[1].astype(F32) * p_ref[0].astype(F32)
        halo_h = halo_ref[1].astype(F32) * halo_ref[0].astype(F32) * valid
        h1, h2 = _conv_taps(h, halo_h, tm)
        w = w_ref[...]
        conv = b_ref[...] + w[0:1, :] * h2 + w[1:2, :] * h1 + w[2:3, :] * h
        z = p_ref[3].astype(F32)
        o_ref[...] = (p_ref[2].astype(F32) * conv * (z * _sigmoid(z))).astype(BF16)

    steps = tm // HALO
    return pl.pallas_call(
        body, name="mixer_b_forward", grid=(D_MODEL // tc, t // tm),
        in_specs=[pl.BlockSpec((4, tm, tc), lambda j, i: (BLOCK_B, i, j)),
                  pl.BlockSpec((4, HALO, tc), lambda j, i: (BLOCK_B, jnp.maximum(i * steps - 1, 0), j)),
                  pl.BlockSpec((3, tc), lambda j, i: (0, j)),
                  pl.BlockSpec((1, tc), lambda j, i: (0, j))],
        out_specs=pl.BlockSpec((tm, tc), lambda j, i: (i, j)),
        out_shape=jax.ShapeDtypeStruct((t, D_MODEL), BF16),
        compiler_params=_params("parallel", "parallel"),
    )(p_b, p_b, conv_w, conv_b)


def _mixer_b_backward(p_b, dyab, conv_w, conv_b, dp, tm, tc):
    t = p_b.shape[1]
    n = t // tm

    def body(p_ref, halo_ref, dy_ref, w_ref, b_ref, dp_in, dp_ref, dw_ref, db_ref, next_ref):
        ii = pl.program_id(1)

        @pl.when(ii == 0)
        def _():
            dw_ref[...] = jnp.zeros_like(dw_ref)
            db_ref[...] = jnp.zeros_like(db_ref)
            next_ref[...] = jnp.zeros_like(next_ref)

        valid = (ii < n - 1).astype(F32)
        xb = p_ref[0].astype(F32)
        cb = p_ref[1].astype(F32)
        bb = p_ref[2].astype(F32)
        z = p_ref[3].astype(F32)
        h = cb * xb
        halo_h = halo_ref[1].astype(F32) * halo_ref[0].astype(F32) * valid
        h1, h2 = _conv_taps(h, halo_h, tm)
        w = w_ref[...]
        w0, w1, w2 = w[0:1, :], w[1:2, :], w[2:3, :]
        conv = b_ref[...] + w0 * h2 + w1 * h1 + w2 * h
        sig = _sigmoid(z)
        sz = z * sig
        dy = dy_ref[...].astype(F32)
        dconv = dy * bb * sz
        dp_ref[2] = (dy * conv * sz).astype(BF16)
        dp_ref[3] = (dy * bb * conv * _silu_grad(sig, sz)).astype(BF16)
        db_ref[...] += jnp.sum(dconv, axis=0, keepdims=True)
        dw_ref[0:1, :] += jnp.sum(dconv * h2, axis=0, keepdims=True)
        dw_ref[1:2, :] += jnp.sum(dconv * h1, axis=0, keepdims=True)
        dw_ref[2:3, :] += jnp.sum(dconv * h, axis=0, keepdims=True)
        row = lax.broadcasted_iota(jnp.int32, h.shape, 0)
        nxt = next_ref[...]
        n0, n1 = nxt[0:1, :], nxt[1:2, :]
        d1 = jnp.where(row == tm - 1, n0, pltpu.roll(dconv, tm - 1, 0))
        d2 = jnp.where(row == tm - 1, n1, jnp.where(row == tm - 2, n0, pltpu.roll(dconv, tm - 2, 0)))
        dh = w2 * dconv + w1 * d1 + w0 * d2
        dp_ref[0] = (dh * cb).astype(BF16)
        dp_ref[1] = (dh * xb).astype(BF16)
        next_ref[...] = dconv[0:8, :]

    steps = tm // HALO
    return pl.pallas_call(
        body, name="mixer_b_backward", grid=(D_MODEL // tc, n),
        in_specs=[pl.BlockSpec((4, tm, tc), lambda j, ii: (BLOCK_B, n - 1 - ii, j)),
                  pl.BlockSpec((4, HALO, tc), lambda j, ii: (BLOCK_B, jnp.maximum((n - 1 - ii) * steps - 1, 0), j)),
                  pl.BlockSpec((None, tm, tc), lambda j, ii: (1, n - 1 - ii, j)),
                  pl.BlockSpec((3, tc), lambda j, ii: (0, j)),
                  pl.BlockSpec((1, tc), lambda j, ii: (0, j)), ANY],
        out_specs=(pl.BlockSpec((4, tm, tc), lambda j, ii: (BLOCK_B, n - 1 - ii, j)),
                   pl.BlockSpec((3, tc), lambda j, ii: (0, j)),
                   pl.BlockSpec((1, tc), lambda j, ii: (0, j))),
        out_shape=(jax.ShapeDtypeStruct(dp.shape, BF16),
                   jax.ShapeDtypeStruct((3, D_MODEL), F32), jax.ShapeDtypeStruct((1, D_MODEL), F32)),
        scratch_shapes=[pltpu.VMEM((8, tc), F32)],
        input_output_aliases={5: 0},
        compiler_params=_params("parallel", "arbitrary"),
    )(p_b, p_b, dyab, conv_w, conv_b, dp)


def _adam_math(w, g, m, v):
    m = ADAM_B1 * m + (1.0 - ADAM_B1) * g
    v = ADAM_B2 * v + (1.0 - ADAM_B2) * (g * g)
    delta = -ADAM_LR * ((m * ADAM_C1) / (jnp.sqrt(v * ADAM_C2) + ADAM_EPS) + ADAM_WD * w)
    return delta, m, v


def _adam_rows(w, g, m, v, tm, name, g_sel=None):
    r, c = w.shape

    def body(w_ref, g_ref, m_ref, v_ref, go_ref, d_ref, mo_ref, vo_ref):
        g = g_ref[...]
        d, mn, vn = _adam_math(w_ref[...], g, m_ref[...], v_ref[...])
        go_ref[...] = g
        d_ref[...] = d
        mo_ref[...] = mn
        vo_ref[...] = vn

    spec = pl.BlockSpec((tm, c), lambda i: (i, 0))
    g_spec = spec if g_sel is None else pl.BlockSpec((None, tm, c), lambda i: (g_sel, i, 0))
    shape = jax.ShapeDtypeStruct((r, c), F32)
    return pl.pallas_call(
        body, name=name, grid=(r // tm,),
        in_specs=[spec, g_spec, spec, spec], out_specs=(spec,) * 4, out_shape=(shape,) * 4,
        compiler_params=_params("parallel"),
    )(w, g, m, v)


SMALL_ROW0 = {name: sum(r for _, r in SMALL_ROWS[:i]) for i, (name, _) in enumerate(SMALL_ROWS)}
LANE_MAJOR = ("ln_g", "ln_b", "b_gate", "ln_v_g", "ln_v_b", "conv_b")


def _lane_pieces(n):
    return [(q, slice(q * 128, (q + 1) * 128)) for q in range(n // 128)]


def _pack_small(d_ln_g, d_ln_b, d_bg, d_lnv_g, d_lnv_b, d_ws, d_bs, d_cw, d_cb, loss_part):
    def body(lg, lb, bg, vg, vb, ws, bs, cw, cb, loss, o_ref):
        def put(row0, vec):
            for q, cols in _lane_pieces(vec.shape[1]):
                o_ref[row0 + q:row0 + q + 1, :] = vec[:, cols]

        put(SMALL_ROW0["ln_g"], lg[...])
        put(SMALL_ROW0["ln_b"], lb[...])
        for n in range(2):
            put(SMALL_ROW0["b_gate"] + n * (D_MODEL // 128), bg[n])
        put(SMALL_ROW0["ln_v_g"], vg[...])
        put(SMALL_ROW0["ln_v_b"], vb[...])
        for h in range(N_HEADS):
            o_ref[SMALL_ROW0["w_s"] + h * CHUNK:SMALL_ROW0["w_s"] + (h + 1) * CHUNK, :] = ws[h]
        o_ref[SMALL_ROW0["b_s"]:SMALL_ROW0["b_s"] + N_HEADS, :] = bs[...]
        for c in range(3):
            put(SMALL_ROW0["conv_w"] + c * (D_MODEL // 128), cw[c:c + 1, :])
        put(SMALL_ROW0["conv_b"], cb[...])
        o_ref[SMALL_ROW0["loss"]:SMALL_ROW0["loss"] + 8, :] = jnp.broadcast_to(loss[...], (8, 128))

    return pl.pallas_call(
        body, name="pack_small", out_shape=jax.ShapeDtypeStruct((SMALL_TOTAL, 128), F32), compiler_params=_params(),
    )(d_ln_g, d_ln_b, d_bg, d_lnv_g, d_lnv_b, d_ws, d_bs.reshape(N_HEADS, CHUNK), d_cw, d_cb, loss_part)


def _adam_small(gathered, params):
    names = list(params)
    flat = [a for n in names for a in params[n]]

    def body(*refs):
        ga_ref = refs[0]
        ins = refs[1:1 + 3 * len(names)]
        outs = refs[1 + 3 * len(names):-1]
        gs_ref = refs[-1]
        g = ga_ref[0]
        for k in range(1, 8):
            g = g + ga_ref[k]
        gs_ref[...] = g
        for i, name in enumerate(names):
            w_ref, m_ref, v_ref = ins[3 * i:3 * i + 3]
            o_refs = outs[4 * i:4 * i + 4]
            row0 = SMALL_ROW0[name]
            if name in LANE_MAJOR:
                pieces = [((slice(None), cols), slice(row0 + q, row0 + q + 1))
                          for q, cols in _lane_pieces(w_ref.shape[1])]
            elif name == "w_s":
                pieces = [((0, h), slice(row0 + h * CHUNK, row0 + (h + 1) * CHUNK)) for h in range(N_HEADS)]
            else:
                pieces = [((0,), slice(row0, row0 + N_HEADS))]
            for idx, rows in pieces:
                gp = gs_ref[rows, :]
                res = (gp,) + _adam_math(w_ref[idx], gp, m_ref[idx], v_ref[idx])
                for o_ref, val in zip(o_refs, res):
                    o_ref[idx] = val
        gcw_ref, loss_ref = outs[-2:]
        for c in range(3):
            for q, cols in _lane_pieces(D_MODEL):
                r = SMALL_ROW0["conv_w"] + c * (D_MODEL // 128) + q
                gcw_ref[c:c + 1, cols] = gs_ref[r:r + 1, :]
        loss_ref[...] = gs_ref[SMALL_ROW0["loss"]:SMALL_ROW0["loss"] + 1, :]

    out_shape = [jax.ShapeDtypeStruct(params[n][0].shape, F32) for n in names for _ in range(4)]
    out_shape += [jax.ShapeDtypeStruct((3, D_MODEL), F32), jax.ShapeDtypeStruct((1, 128), F32)]
    res = pl.pallas_call(
        body, name="adam_small", out_shape=tuple(out_shape),
        scratch_shapes=[pltpu.VMEM((SMALL_TOTAL, 128), F32)], compiler_params=_params(),
    )(gathered, *flat)
    return {n: res[4 * i:4 * i + 4] for i, n in enumerate(names)}, res[-2], res[-1]


def _adam_conv_w(g_all, chip, w, m, v):
    cols = w.shape[2]

    def body(c_ref, g_ref, w_ref, m_ref, v_ref, go_ref, d_ref, mo_ref, vo_ref):
        g = g_ref[...]
        d, mn, vn = _adam_math(w_ref[...], g, m_ref[...], v_ref[...])
        go_ref[...] = g
        d_ref[...] = d
        mo_ref[...] = mn
        vo_ref[...] = vn

    own = pl.BlockSpec((None, 3, cols), lambda i, c_ref: (0, 0, 0))
    return pl.pallas_call(
        body, name="adam_conv_w",
        grid_spec=pltpu.PrefetchScalarGridSpec(
            num_scalar_prefetch=1, grid=(1,),
            in_specs=[pl.BlockSpec((3, cols), lambda i, c_ref: (0, c_ref[0])), own, own, own],
            out_specs=(own,) * 4),
        out_shape=(jax.ShapeDtypeStruct(w.shape, F32),) * 4,
        compiler_params=_params("arbitrary"),
    )(chip, g_all, w, m, v)


def kernel(x, w_in, b_gate, ln_v_g, ln_v_b, w_s, b_s, conv_w, conv_b, w_oa, w_ob, w_out, ln_g, ln_b, loss_target, m_w_in, m_b_gate, m_ln_v_g, m_ln_v_b, m_w_s, m_b_s, m_conv_w, m_conv_b, m_w_oa, m_w_ob, m_w_out, m_ln_g, m_ln_b, v_w_in, v_b_gate, v_ln_v_g, v_ln_v_b, v_w_s, v_b_s, v_conv_w, v_conv_b, v_w_oa, v_w_ob, v_w_out, v_ln_g, v_ln_b):
    t = x.shape[1]
    x2 = x[0]
    target = loss_target[0]
    chip = 2 * lax.axis_index("x") + lax.axis_index("y")
    conv_cols = conv_w.shape[2]

    chip1 = chip.astype(jnp.int32).reshape(1)
    w_pre = _cast_into_columns(w_in[0], chip1, N_CHIPS, 256, "cast_w_in")
    wo_b, wm = _prep_small_weights(w_oa[0], w_ob[0], w_out[0], w_s[0])
    conv_w8 = jnp.concatenate([conv_w[0], jnp.zeros((5, conv_cols), F32)], axis=0)
    bs_col = b_s[0].reshape(N_HEADS, CHUNK, 1)
    bg = b_gate.reshape(2, 1, D_MODEL)

    xb, xt = _cast_and_transpose(x2, min(512, t))
    p, w_full, wo_full, cw_full = _gather_and_project(xb, w_pre, wo_b, conv_w8, min(4096, t))
    wo3 = wo_full.reshape(3, D_MODEL, D_MODEL)
    conv_w_all = jnp.transpose(cw_full[:, :3, :], (1, 0, 2)).reshape(3, D_MODEL)
    tm_a = min(2048, t)
    ya = _mixer_a_forward(p, wm, bs_col, ln_v_g, ln_v_b, tm_a)
    tm_b = min(1024, t)
    yb = _mixer_b_forward(p, conv_w_all, conv_b, min(2048, t), 512)
    tm_m = min(1024, t)
    merged, oab = _merge_forward(ya, yb, wo3, p, bg, tm_m, 512)

    drb, gx, dmerged, d_ln_g, d_ln_b, loss_part = _head(merged, wo3, x2, target, ln_g, ln_b, min(512, t))
    doab, dp, d_bg, dyab = _gate_and_branch_backward(dmerged, oab, p, bg, wo3, min(256, t))
    dp, d_ws, d_bs, d_lnv_g, d_lnv_b = _mixer_a_backward(p, dyab, wm, bs_col, ln_v_g, ln_v_b, dp, min(1024, t))
    dp, d_cw, d_cb = _mixer_b_backward(p, dyab, conv_w_all, conv_b, dp, tm_b, 512)

    tk = min(2048, t)
    g_in = _win_grad(xt, dp, 1024, tk)
    g_oa = _weight_grad(ya, 0, doab, 0, 1024, 1024, tk, "grad_w_oa")
    g_ob = _weight_grad(yb, 0, doab, 1, 1024, 1024, tk, "grad_w_ob")
    g_out, r_in, r_oa, r_ob = _weight_grad(merged, 0, drb, 0, 1024, 1024, tk, "grad_w_out",
                                           exchange=(g_in, g_oa, g_ob))
    r_o = (r_oa, r_ob, _exchange_halves(g_out))
    s_in, s_o = _add_halves(g_in, (g_oa, g_ob, g_out), r_in, r_o)
    small_part = _pack_small(d_ln_g, d_ln_b, d_bg, d_lnv_g, d_lnv_b, d_ws, d_bs, d_cw, d_cb, loss_part)
    gx, q_in, q_o, gathered = _input_grad_and_scatter(dp, w_full, gx, s_in, s_o, small_part, min(1024, t))
    f_in, f_o = _sum_chips(q_in, q_o, s_in, s_o)
    gsum_in, gsum_o = _share_halves(f_in, f_o)

    big = {}
    grad_x, big["w_in"] = _input_grad_tail_and_adam(dp, w_full, gx, w_in[0], gsum_in, m_w_in[0], v_w_in[0],
                                                    min(32, t // 128))
    for n, (name, w, m, v) in enumerate((("w_oa", w_oa, m_w_oa, v_w_oa), ("w_ob", w_ob, m_w_ob, v_w_ob),
                                         ("w_out", w_out, m_w_out, v_w_out))):
        big[name] = _adam_rows(w[0], gsum_o, m[0], v[0], 256, "adam_" + name, g_sel=n)

    small, g_conv_w, loss_row = _adam_small(gathered, {
        "ln_g": (ln_g, m_ln_g, v_ln_g), "ln_b": (ln_b, m_ln_b, v_ln_b), "b_gate": (b_gate, m_b_gate, v_b_gate),
        "ln_v_g": (ln_v_g, m_ln_v_g, v_ln_v_g), "ln_v_b": (ln_v_b, m_ln_v_b, v_ln_v_b),
        "w_s": (w_s, m_w_s, v_w_s), "b_s": (b_s, m_b_s, v_b_s), "conv_b": (conv_b, m_conv_b, v_conv_b)})
    small["conv_w"] = _adam_conv_w(g_conv_w, chip1, conv_w, m_conv_w, v_conv_w)
    loss = loss_row[0, 0]

    order = ("w_in", "b_gate", "ln_v_g", "ln_v_b", "w_s", "b_s", "conv_w", "conv_b", "w_oa", "w_ob", "w_out",
             "ln_g", "ln_b")
    outs = [loss, grad_x[None]]
    for which in range(4):
        for name in order:
            outs.append(big[name][which][None] if name in big else small[name][which])
    return tuple(outs)
```

```python
import functools
import math

import jax
import jax.numpy as jnp
from jax import lax
from jax.experimental import pallas as pl
from jax.experimental.pallas import tpu as pltpu

F32 = jnp.float32
BF16 = jnp.bfloat16

D_MODEL = 2048
N_HEADS = 8
HEAD_DIM = D_MODEL // N_HEADS
CHUNK = 128
N_SEG = 9
N_CHIPS = 4
SHARD_COLS = N_SEG * D_MODEL // N_CHIPS
COL_BLOCK = 512
BLOCKS_PER_SHARD = SHARD_COLS // COL_BLOCK
BLOCKS_PER_SEG = D_MODEL // COL_BLOCK
SUBS = 3
SHARD_ROWS = D_MODEL // N_CHIPS
N_SLOTS = 12
BLOCK_A, BLOCK_G, BLOCK_B = 0, 2, 2
SLOT_OF_SEG = (0, 1, 2, 8, 9, 10, 11, 4, 5)
DN_ALPHA = 2.0 ** 0.25
LN_EPS = 1e-5
GELU_K = math.sqrt(2.0 / math.pi)
GELU_C = 0.044715

ADAM_LR = 0.001
ADAM_B1 = 0.9
ADAM_B2 = 0.999
ADAM_EPS = 1e-08
ADAM_WD = 0.01
ADAM_STEP = 10
ADAM_C1 = 1.0 / (1.0 - ADAM_B1 ** ADAM_STEP)
ADAM_C2 = 1.0 / (1.0 - ADAM_B2 ** ADAM_STEP)

VMEM_LIMIT = 60 * 1024 * 1024
MESH = pl.DeviceIdType.MESH
ANY = pl.BlockSpec(memory_space=pl.ANY)

SMALL_ROWS = (("ln_g", 16), ("ln_b", 16), ("b_gate", 32), ("ln_v_g", 16), ("ln_v_b", 16),
              ("w_s", 1024), ("b_s", 8), ("conv_w", 48), ("conv_b", 16), ("loss", 8))
SMALL_TOTAL = sum(r for _, r in SMALL_ROWS)


def _params(*sem):
    return pltpu.CompilerParams(dimension_semantics=sem, vmem_limit_bytes=VMEM_LIMIT)


def _sigmoid(x):
    return 1.0 / (1.0 + jnp.exp(-x))


def _gelu_gate(x, x2):
    return 1.0 / (1.0 + jnp.exp(x * ((-2.0 * GELU_K) + (-2.0 * GELU_K * GELU_C) * x2)))


def _gelu_and_grad(x):
    x2 = x * x
    s = _gelu_gate(x, x2)
    g = x * s
    dg = s + g * (1.0 - s) * ((2.0 * GELU_K) + (6.0 * GELU_K * GELU_C) * x2)
    return g, dg


def _silu_grad(sig, sz):
    return sig + sz * (1.0 - sig)


def _dot(a, b, ca=1, cb=0):
    return lax.dot_general(a, b, (((ca,), (cb,)), ((), ())), preferred_element_type=F32)


def _cast_and_transpose(x, tm):
    t, d = x.shape

    def body(x_ref, o_ref, ot_ref):
        v = x_ref[...]
        o_ref[...] = v.astype(BF16)
        ot_ref[...] = v.T.astype(BF16)

    return pl.pallas_call(
        body, name="cast_x", grid=(t // tm,),
        in_specs=[pl.BlockSpec((tm, d), lambda i: (i, 0))],
        out_specs=(pl.BlockSpec((tm, d), lambda i: (i, 0)), pl.BlockSpec((d, tm), lambda i: (0, i))),
        out_shape=(jax.ShapeDtypeStruct((t, d), BF16), jax.ShapeDtypeStruct((d, t), BF16)),
        compiler_params=_params("parallel"),
    )(x)


def _cast_into_columns(w, slot, n_slots, tm, name):
    r, c = w.shape

    def body(s_ref, w_ref, o_ref):
        o_ref[...] = w_ref[...].astype(BF16)

    return pl.pallas_call(
        body, name=name,
        grid_spec=pltpu.PrefetchScalarGridSpec(
            num_scalar_prefetch=1, grid=(r // tm,),
            in_specs=[pl.BlockSpec((tm, c), lambda i, s_ref: (i, 0))],
            out_specs=pl.BlockSpec((tm, c), lambda i, s_ref: (i, s_ref[0]))),
        out_shape=jax.ShapeDtypeStruct((r, n_slots * c), BF16),
        compiler_params=_params("parallel"),
    )(slot, w)


def _prep_small_weights(w_oa, w_ob, w_out, w_s):
    rows = w_oa.shape[0]

    def body(a_ref, b_ref, c_ref, ws_ref, wo_ref, wm_ref):
        wo_ref[0] = a_ref[...].astype(BF16)
        wo_ref[1] = b_ref[...].astype(BF16)
        wo_ref[2] = c_ref[...].astype(BF16)
        t = lax.broadcasted_iota(jnp.int32, (CHUNK, CHUNK), 0)
        s = lax.broadcasted_iota(jnp.int32, (CHUNK, CHUNK), 1)
        for h in range(N_HEADS):
            wm_ref[h] = jnp.where(s <= t, ws_ref[h], 0.0).astype(BF16)

    return pl.pallas_call(
        body, name="prep_small_weights",
        out_shape=(jax.ShapeDtypeStruct((3, rows, D_MODEL), BF16),
                   jax.ShapeDtypeStruct((N_HEADS, CHUNK, CHUNK), BF16)),
        compiler_params=_params(),
    )(w_oa, w_ob, w_out, w_s)


def _mesh_pos():
    x, y, c = lax.axis_index("x"), lax.axis_index("y"), lax.axis_index("c")
    chips = [(1 - x, y), (x, 1 - y), (1 - x, 1 - y)]
    return x, y, c, chips


def _slot_of_seg(seg):
    return jnp.where(seg < 3, seg, jnp.where(seg < 7, seg + 5, seg - 3))


def _gather_and_project(xb, w_pre, wo_b, conv_w8, tm):
    t = xb.shape[0]
    d, sc = w_pre.shape[0], w_pre.shape[1] // N_CHIPS
    rows = wo_b.shape[1]
    hd, hr = d // 2, rows // 2
    nj = BLOCKS_PER_SHARD // SUBS
    pc = nj * COL_BLOCK
    units = N_CHIPS * SUBS
    ni = t // tm
    total = units * ni * nj
    mx, my = lax.axis_index("x"), lax.axis_index("y")
    order = jnp.stack([2 * mx + my, 2 * (1 - mx) + my, 2 * mx + (1 - my),
                       2 * (1 - mx) + (1 - my)]).astype(jnp.int32)

    def body(order_ref, x_ref, wpre_ref, wo_ref, cw_ref, p_ref, wf_ref, wof_ref, cwf_ref,
             wbuf, wsem, xbuf, xsem, send_sems, recv_sems, local_sems):
        x, y, c, chips = _mesh_pos()
        me = 2 * x + y
        sibling = (x, y, 1 - c)
        u, i, j = pl.program_id(0), pl.program_id(1), pl.program_id(2)
        n = (u * ni + i) * nj + j
        m = u * ni + i

        def rows_start(m_):
            pltpu.make_async_copy(x_ref.at[pl.ds(pl.multiple_of(lax.rem(m_, ni) * tm, tm), tm), :],
                                  xbuf.at[lax.rem(m_, 2)], xsem.at[lax.rem(m_, 2)]).start()

        def chip_of(q):
            return 2 * chips[q][0] + chips[q][1]

        def piece(ref, k, cc, r):
            return ref.at[pl.ds(cc * hd, hd), pl.ds(pl.multiple_of(k * sc + r * pc, COL_BLOCK), pc)]

        def wo_half(ref4, k, cc):
            return ref4.at[:, k, pl.ds(cc * hr, hr), :]

        def rcopy(sem, src, dst, to):
            return pltpu.make_async_remote_copy(src_ref=src, dst_ref=dst, send_sem=send_sems.at[sem],
                                                recv_sem=recv_sems.at[sem], device_id=to, device_id_type=MESH)

        def w_send(q, r):
            return rcopy(q * SUBS + r, piece(wpre_ref, me, c, r), piece(wf_ref, me, c, r), (*chips[q], c))

        def w_landed(q, r):
            return rcopy(q * SUBS + r, piece(wpre_ref, me, c, r), piece(wf_ref, chip_of(q), c, r), sibling)

        def w_forward(q, r, cc):
            ref = piece(wf_ref, chip_of(q), cc, r)
            return rcopy(9 + q * SUBS + r, ref, ref, sibling)

        def wo_send(q):
            return rcopy(18 + q, wo_ref.at[:, pl.ds(c * hr, hr), :], wo_half(wof_ref, me, c), (*chips[q], c))

        def wo_landed(q):
            return rcopy(18 + q, wo_ref.at[:, pl.ds(c * hr, hr), :], wo_half(wof_ref, chip_of(q), c), sibling)

        def wo_forward(q, cc):
            ref = wo_half(wof_ref, chip_of(q), cc)
            return rcopy(21 + q, ref, ref, sibling)

        def conv_send(q):
            return rcopy(24 + q, cw_ref, cwf_ref.at[me], (*chips[q], c))

        def local_copies():
            return [pltpu.make_async_copy(wo_ref, wof_ref.at[:, me], local_sems.at[0]),
                    pltpu.make_async_copy(cw_ref, cwf_ref.at[me], local_sems.at[1])]

        def tile_start(u_, j_, slot):
            g = order_ref[u_ // SUBS] * BLOCKS_PER_SHARD + lax.rem(u_, SUBS) * nj + j_
            cols = pl.ds(pl.multiple_of(g * COL_BLOCK, COL_BLOCK), COL_BLOCK)

            @pl.when(u_ < SUBS)
            def _():
                pltpu.make_async_copy(wpre_ref.at[:, cols], wbuf.at[slot], wsem.at[slot]).start()

            @pl.when(u_ >= SUBS)
            def _():
                pltpu.make_async_copy(wf_ref.at[:, cols], wbuf.at[slot], wsem.at[slot]).start()

        def end_of(u_):
            return jnp.logical_and(u == u_, jnp.logical_and(i == ni - 1, j == nj - 1))

        @pl.when(n == 0)
        def _():
            for cp in local_copies():
                cp.start()
            for r in range(SUBS):
                for q in (0, 1):
                    w_send(q, r).start()
            for q in range(3):
                conv_send(q).start()
            tile_start(0, 0, 0)
            rows_start(0)

        def pass_on(q, r):
            w_landed(q, r).wait_recv()
            w_forward(q, r, c).start()

        for u_ in range(1, units - 1):
            @pl.when(end_of(u_))
            def _(u_=u_):
                if u_ <= SUBS:
                    for q in (0, 1):
                        pass_on(q, u_ - 1)
                if u_ == SUBS:
                    for r in range(SUBS):
                        for q in (0, 1):
                            w_send(q, r).wait_send()
                    for r in range(SUBS):
                        w_send(2, r).start()
                if 2 * SUBS - 1 <= u_ <= 3 * SUBS - 2:
                    pass_on(2, u_ - (2 * SUBS - 1))
                if u_ == 3 * SUBS - 2:
                    for r in range(SUBS):
                        w_send(2, r).wait_send()
                    for q in range(3):
                        wo_send(q).start()
                nxt = u_ + 1
                if nxt >= SUBS:
                    w_forward(nxt // SUBS - 1, nxt % SUBS, 1 - c).wait_recv()

        n1 = n + 1

        @pl.when(n1 < total)
        def _():
            tile_start(n1 // (ni * nj), lax.rem(n1, nj), lax.rem(n1, 2))

        xslot = lax.rem(m, 2)

        @pl.when(j == 0)
        def _():
            @pl.when(m + 1 < units * ni)
            def _():
                rows_start(m + 1)

            pltpu.make_async_copy(x_ref.at[pl.ds(0, tm), :], xbuf.at[xslot], xsem.at[xslot]).wait()

        slot = lax.rem(n, 2)
        pltpu.make_async_copy(wpre_ref.at[:, pl.ds(0, COL_BLOCK)], wbuf.at[slot], wsem.at[slot]).wait()
        p_ref[...] = _dot(xbuf[xslot], wbuf[slot]).astype(BF16)

        @pl.when(n == total - 1)
        def _():
            for q in range(3):
                wo_landed(q).wait_recv()
                wo_forward(q, c).start()
            for q in range(3):
                wo_forward(q, 1 - c).wait_recv()
                rcopy(24 + q, cw_ref, cwf_ref.at[chip_of(q)], sibling).wait_recv()
            for q in range(3):
                for r in range(SUBS):
                    w_forward(q, r, c).wait_send()
                wo_send(q).wait_send()
                wo_forward(q, c).wait_send()
                conv_send(q).wait_send()
            for cp in local_copies():
                cp.wait()

    def p_map(u, i, j, o):
        g = o[u // SUBS] * BLOCKS_PER_SHARD + lax.rem(u, SUBS) * nj + j
        return (_slot_of_seg(g // BLOCKS_PER_SEG), i, lax.rem(g, BLOCKS_PER_SEG))

    return pl.pallas_call(
        body, name="gather_and_project",
        grid_spec=pltpu.PrefetchScalarGridSpec(
            num_scalar_prefetch=1, grid=(units, ni, nj),
            in_specs=[ANY, ANY, ANY, ANY],
            out_specs=(pl.BlockSpec((None, tm, COL_BLOCK), p_map), ANY, ANY, ANY),
            scratch_shapes=[pltpu.VMEM((2, d, COL_BLOCK), BF16), pltpu.SemaphoreType.DMA((2,)),
                            pltpu.VMEM((2, tm, D_MODEL), BF16), pltpu.SemaphoreType.DMA((2,)),
                            pltpu.SemaphoreType.DMA((27,)), pltpu.SemaphoreType.DMA((27,)),
                            pltpu.SemaphoreType.DMA((2,))]),
        out_shape=(jax.ShapeDtypeStruct((N_SLOTS, t, D_MODEL), BF16),
                   jax.ShapeDtypeStruct((d, N_CHIPS * sc), BF16),
                   jax.ShapeDtypeStruct((3, N_CHIPS, rows, D_MODEL), BF16),
                   jax.ShapeDtypeStruct((N_CHIPS,) + conv_w8.shape, F32)),
        input_output_aliases={2: 1},
        compiler_params=pltpu.CompilerParams(dimension_semantics=("arbitrary",) * 3, vmem_limit_bytes=VMEM_LIMIT,
                                             has_side_effects=True),
    )(order, xb, w_pre, wo_b, conv_w8)


def _add_halves(g_in, g_o, r_in, r_ab):
    d, c9 = g_in.shape
    hd = d // 2
    hr = SHARD_ROWS // 2
    core = lax.axis_index("c").astype(jnp.int32).reshape(1)
    tm = min(512, hd)
    nb = hd // tm
    g_o4 = [g.reshape(N_CHIPS, 2, hr, D_MODEL) for g in g_o]

    def body_in(c_ref, g_ref, r_ref, gout_ref, o_ref, rout_ref, send_sem, recv_sem):
        k, i = pl.program_id(0), pl.program_id(1)
        x, y, c, _ = _mesh_pos()
        swap = pltpu.make_async_remote_copy(src_ref=gout_ref.at[:, 1 - c], dst_ref=rout_ref, send_sem=send_sem,
                                            recv_sem=recv_sem, device_id=(x, y, 1 - c), device_id_type=MESH)

        @pl.when(jnp.logical_and(k == 0, i == 0))
        def _():
            swap.start()

        o_ref[...] = (g_ref[...].astype(F32) + r_ref[...].astype(F32)).astype(BF16)

        @pl.when(jnp.logical_and(k == N_CHIPS - 1, i == nb - 1))
        def _():
            swap.wait()

    s_in, r_out = pl.pallas_call(
        body_in, name="rs_add_halves_in",
        grid_spec=pltpu.PrefetchScalarGridSpec(
            num_scalar_prefetch=1, grid=(N_CHIPS, nb),
            in_specs=[pl.BlockSpec((tm, SHARD_COLS), lambda k, i, c_ref: (c_ref[0] * nb + i, k)),
                      pl.BlockSpec((tm, SHARD_COLS), lambda k, i, c_ref: (i, k)), ANY],
            out_specs=(pl.BlockSpec((None, tm, SHARD_COLS), lambda k, i, c_ref: (k, i, 0)), ANY),
            scratch_shapes=[pltpu.SemaphoreType.DMA, pltpu.SemaphoreType.DMA]),
        out_shape=(jax.ShapeDtypeStruct((N_CHIPS, hd, SHARD_COLS), BF16),
                   jax.ShapeDtypeStruct((N_CHIPS, hr, D_MODEL), BF16)),
        compiler_params=pltpu.CompilerParams(dimension_semantics=("arbitrary", "arbitrary"),
                                             vmem_limit_bytes=VMEM_LIMIT, has_side_effects=True),
    )(core, g_in, r_in, g_o4[2])
    r_o = (*r_ab, r_out)

    def body_o(c_ref, ga_ref, gb_ref, gc_ref, ra_ref, rb_ref, rc_ref, o_ref):
        for n, (g_ref, r_ref) in enumerate(((ga_ref, ra_ref), (gb_ref, rb_ref), (gc_ref, rc_ref))):
            o_ref[n] = (g_ref[...].astype(F32) + r_ref[...].astype(F32)).astype(BF16)

    gspec = pl.BlockSpec((None, None, hr, D_MODEL), lambda k, c_ref: (k, c_ref[0], 0, 0))
    rspec = pl.BlockSpec((None, hr, D_MODEL), lambda k, c_ref: (k, 0, 0))
    s_o = pl.pallas_call(
        body_o, name="rs_add_halves_o",
        grid_spec=pltpu.PrefetchScalarGridSpec(
            num_scalar_prefetch=1, grid=(N_CHIPS,),
            in_specs=[gspec] * 3 + [rspec] * 3,
            out_specs=pl.BlockSpec((3, None, hr, D_MODEL), lambda k, c_ref: (0, k, 0, 0))),
        out_shape=jax.ShapeDtypeStruct((3, N_CHIPS, hr, D_MODEL), BF16),
        compiler_params=_params("parallel"),
    )(core, *g_o4, *r_o)
    return s_in, s_o


def _sum_chips(r_in, r_o, s_in, s_o):
    _, hd, sc = r_in.shape
    hr = r_o.shape[2]
    tm = min(256, hd)
    nb = hd // tm
    pos = jnp.stack([2 * lax.axis_index("x") + lax.axis_index("y"), lax.axis_index("c")]).astype(jnp.int32)

    def chip_sum(pos_ref, r_ref, s_ref):
        acc = None
        for k in range(N_CHIPS):
            term = jnp.where(pos_ref[0] == k, s_ref[...], r_ref[k]).astype(F32)
            acc = term if acc is None else acc + term
        return acc

    def body_in(pos_ref, r_ref, s_ref, o_ref):
        o_ref[...] = chip_sum(pos_ref, r_ref, s_ref)

    f_in = pl.pallas_call(
        body_in, name="rs_sum_chips_in",
        grid_spec=pltpu.PrefetchScalarGridSpec(
            num_scalar_prefetch=1, grid=(nb,),
            in_specs=[pl.BlockSpec((N_CHIPS, tm, sc), lambda i, p: (0, i, 0)),
                      pl.BlockSpec((None, tm, sc), lambda i, p: (p[0], i, 0))],
            out_specs=pl.BlockSpec((tm, sc), lambda i, p: (p[1] * nb + i, 0))),
        out_shape=jax.ShapeDtypeStruct((2 * hd, sc), F32),
        compiler_params=_params("parallel"),
    )(pos, r_in, s_in)

    def body_o(pos_ref, r_ref, s_ref, o_ref):
        o_ref[...] = chip_sum(pos_ref, r_ref, s_ref)

    f_o = pl.pallas_call(
        body_o, name="rs_sum_chips_o",
        grid_spec=pltpu.PrefetchScalarGridSpec(
            num_scalar_prefetch=1, grid=(3,),
            in_specs=[pl.BlockSpec((N_CHIPS, None, hr, D_MODEL), lambda n, p: (0, n, 0, 0)),
                      pl.BlockSpec((None, None, hr, D_MODEL), lambda n, p: (n, p[0], 0, 0))],
            out_specs=pl.BlockSpec((None, hr, D_MODEL), lambda n, p: (n, p[1], 0))),
        out_shape=jax.ShapeDtypeStruct((3, 2 * hr, D_MODEL), F32),
        compiler_params=_params("parallel"),
    )(pos, r_o, s_o)
    return f_in, f_o


def _share_halves(f_in, f_o):
    hd, sc = f_in.shape[0] // 2, f_in.shape[1]
    hr = f_o.shape[1] // 2

    def body(fi_ref, fo_ref, gi_ref, go_ref, send_sems, recv_sems):
        x, y, c, _ = _mesh_pos()
        sibling = (x, y, 1 - c)

        def halves(cc):
            rows_i, rows_o = pl.ds(cc * hd, hd), pl.ds(cc * hr, hr)
            return (fi_ref.at[rows_i, :], gi_ref.at[rows_i, :]), (fo_ref.at[:, rows_o, :], go_ref.at[:, rows_o, :])

        def copies(cc):
            return [pltpu.make_async_remote_copy(src_ref=src, dst_ref=dst, send_sem=send_sems.at[n],
                                                 recv_sem=recv_sems.at[n], device_id=sibling, device_id_type=MESH)
                    for n, (src, dst) in enumerate(halves(cc))]

        sends = copies(c)
        for cp in sends:
            cp.start()
        for cp in copies(1 - c):
            cp.wait_recv()
        for cp in sends:
            cp.wait_send()

    return pl.pallas_call(
        body, name="rs_share_halves",
        in_specs=[ANY, ANY], out_specs=(ANY, ANY),
        out_shape=(jax.ShapeDtypeStruct(f_in.shape, F32), jax.ShapeDtypeStruct(f_o.shape, F32)),
        scratch_shapes=[pltpu.SemaphoreType.DMA((2,)), pltpu.SemaphoreType.DMA((2,))],
        input_output_aliases={0: 0, 1: 1},
        compiler_params=pltpu.CompilerParams(has_side_effects=True),
    )(f_in, f_o)


def _merge_forward(ya, yb, wo3, p, bg, tm, tn):
    t = ya.shape[0]

    def body(ya_ref, yb_ref, wa_ref, wb_ref, g_ref, bg_ref, m_ref, oab_ref):
        oa = _dot(ya_ref[...], wa_ref[...])
        ob = _dot(yb_ref[...], wb_ref[...])
        ga = _sigmoid(g_ref[0].astype(F32) + bg_ref[0])
        gb = _sigmoid(g_ref[1].astype(F32) + bg_ref[1])
        m_ref[...] = (ga * oa + gb * ob).astype(BF16)
        oab_ref[0] = oa.astype(BF16)
        oab_ref[1] = ob.astype(BF16)

    return pl.pallas_call(
        body, name="merge_forward", grid=(t // tm, D_MODEL // tn),
        in_specs=[pl.BlockSpec((tm, D_MODEL), lambda i, j: (i, 0)),
                  pl.BlockSpec((tm, D_MODEL), lambda i, j: (i, 0)),
                  pl.BlockSpec((None, D_MODEL, tn), lambda i, j: (0, 0, j)),
                  pl.BlockSpec((None, D_MODEL, tn), lambda i, j: (1, 0, j)),
                  pl.BlockSpec((2, tm, tn), lambda i, j: (BLOCK_G, i, j)),
                  pl.BlockSpec((2, 1, tn), lambda i, j: (0, 0, j))],
        out_specs=(pl.BlockSpec((tm, tn), lambda i, j: (i, j)),
                   pl.BlockSpec((2, tm, tn), lambda i, j: (0, i, j))),
        out_shape=(jax.ShapeDtypeStruct((t, D_MODEL), BF16), jax.ShapeDtypeStruct((2, t, D_MODEL), BF16)),
        compiler_params=_params("parallel", "parallel"),
    )(ya, yb, wo3, wo3, p, bg)


HEAD_ROWS = 256


def _head(merged, wo3, x, target, ln_g, ln_b, tm):
    t = x.shape[0]
    inv_d = 1.0 / D_MODEL

    def body(m_ref, w_ref, x_ref, t_ref, g_ref, b_ref, dr_ref, gx_ref, dm_ref, dg_ref, db_ref, loss_ref):
        i = pl.program_id(0)

        @pl.when(i == 0)
        def _():
            dg_ref[...] = jnp.zeros_like(dg_ref)
            db_ref[...] = jnp.zeros_like(db_ref)
            loss_ref[...] = jnp.zeros_like(loss_ref)

        w = w_ref[...]
        g = g_ref[...]
        tiles = [slice(r0, r0 + HEAD_ROWS) for r0 in range(0, tm, HEAD_ROWS)]
        firsts = [_dot(m_ref[rows, :], w) for rows in tiles]
        for rows, out in zip(tiles, firsts):
            r = DN_ALPHA * x_ref[rows, :] + out
            mu = jnp.mean(r, axis=-1, keepdims=True)
            xc = r - mu
            var = jnp.mean(xc * xc, axis=-1, keepdims=True)
            rstd = lax.rsqrt(var + LN_EPS)
            xhat = xc * rstd
            e = xhat * g + b_ref[...] - t_ref[rows, :]
            se = jnp.sum(jnp.sum(e * e, axis=1, keepdims=True), axis=0, keepdims=True)
            loss_ref[...] += jnp.broadcast_to((0.5 * inv_d) * se, loss_ref.shape)
            dy = e * inv_d
            db_ref[...] += jnp.sum(dy, axis=0, keepdims=True)
            dg_ref[...] += jnp.sum(dy * xhat, axis=0, keepdims=True)
            dxh = dy * g
            m1 = jnp.mean(dxh, axis=-1, keepdims=True)
            m2 = jnp.mean(dxh * xhat, axis=-1, keepdims=True)
            dr = rstd * (dxh - m1 - xhat * m2)
            gx_ref[rows, :] = DN_ALPHA * dr
            drb = dr.astype(BF16)
            dr_ref[rows, :] = drb
            dm_ref[rows, :] = _dot(drb, w, 1, 1).astype(BF16)

    row = pl.BlockSpec((tm, D_MODEL), lambda i: (i, 0))
    vec = pl.BlockSpec((1, D_MODEL), lambda i: (0, 0))
    return pl.pallas_call(
        body, name="head", grid=(t // tm,),
        in_specs=[row, pl.BlockSpec((None, D_MODEL, D_MODEL), lambda i: (2, 0, 0), pipeline_mode=pl.Buffered(1)),
                  row, row, vec, vec],
        out_specs=(row, row, row, vec, vec, pl.BlockSpec((1, 128), lambda i: (0, 0))),
        out_shape=(jax.ShapeDtypeStruct((t, D_MODEL), BF16), jax.ShapeDtypeStruct((t, D_MODEL), F32),
                   jax.ShapeDtypeStruct((t, D_MODEL), BF16), jax.ShapeDtypeStruct((1, D_MODEL), F32),
                   jax.ShapeDtypeStruct((1, D_MODEL), F32), jax.ShapeDtypeStruct((1, 128), F32)),
        compiler_params=_params("arbitrary"),
    )(merged, wo3, x, target, ln_g, ln_b)


def _gate_and_branch_backward(dmerged, oab, p, bg, wo3, tm):
    t = dmerged.shape[0]

    def body(dm_ref, oab_ref, g_ref, bg_ref, wa_ref, wb_ref, do_ref, dpg_ref, dbg_ref, dy_ref):
        @pl.when(pl.program_id(0) == 0)
        def _():
            dbg_ref[...] = jnp.zeros_like(dbg_ref)

        dm = dm_ref[...].astype(F32)
        for n, w_ref in enumerate((wa_ref, wb_ref)):
            gate = _sigmoid(g_ref[n].astype(F32) + bg_ref[n])
            d_o = (dm * gate).astype(BF16)
            do_ref[n] = d_o
            dgate = dm * oab_ref[n].astype(F32) * gate * (1.0 - gate)
            dpg_ref[n] = dgate.astype(BF16)
            dbg_ref[n] += jnp.sum(dgate, axis=0, keepdims=True)
            dy_ref[n] = _dot(d_o, w_ref[...], 1, 1).astype(BF16)

    pair = pl.BlockSpec((2, tm, D_MODEL), lambda i: (0, i, 0))
    gates = pl.BlockSpec((2, tm, D_MODEL), lambda i: (BLOCK_G, i, 0))
    vec = pl.BlockSpec((2, 1, D_MODEL), lambda i: (0, 0, 0))

    def weight(n):
        return pl.BlockSpec((None, D_MODEL, D_MODEL), lambda i: (n, 0, 0), pipeline_mode=pl.Buffered(1))

    pair_shape = jax.ShapeDtypeStruct((2, t, D_MODEL), BF16)
    return pl.pallas_call(
        body, name="gate_and_branch_backward", grid=(t // tm,),
        in_specs=[pl.BlockSpec((tm, D_MODEL), lambda i: (i, 0)), pair, gates, vec, weight(0), weight(1)],
        out_specs=(pair, gates, vec, pair),
        out_shape=(pair_shape, jax.ShapeDtypeStruct((N_SLOTS, t, D_MODEL), BF16),
                   jax.ShapeDtypeStruct((2, 1, D_MODEL), F32), pair_shape),
        compiler_params=_params("arbitrary"),
    )(dmerged, oab, p, bg, wo3, wo3)


def _weight_grad(a, a_sel, b, b_sel, tm, tn, tk, name, exchange=None):
    t = a.shape[-2]
    nk = t // tk
    ni, nj = D_MODEL // tm, D_MODEL // tn
    hr = SHARD_ROWS // 2
    if exchange is not None:
        g_in, *g_sq = exchange
        g_sq4 = [g.reshape(N_CHIPS, 2, hr, D_MODEL) for g in g_sq]

    def body(a_ref, b_ref, *rest):
        if exchange is None:
            o_ref, acc_ref = rest
        else:
            n_ex = 1 + len(g_sq4)
            g_refs = rest[:n_ex]
            o_ref = rest[n_ex]
            r_refs = rest[n_ex + 1:2 * n_ex + 1]
            acc_ref, send_sems, recv_sems = rest[2 * n_ex + 1:]
        i, j, k = pl.program_id(0), pl.program_id(1), pl.program_id(2)

        if exchange is not None:
            x, y, c, _ = _mesh_pos()
            hd = g_in.shape[0] // 2
            srcs = [g_refs[0].at[pl.ds((1 - c) * hd, hd), :]] + [g.at[:, 1 - c] for g in g_refs[1:]]
            swaps = [pltpu.make_async_remote_copy(src_ref=src, dst_ref=dst, send_sem=send_sems.at[n],
                                                  recv_sem=recv_sems.at[n], device_id=(x, y, 1 - c),
                                                  device_id_type=MESH)
                     for n, (src, dst) in enumerate(zip(srcs, r_refs))]

            @pl.when(jnp.logical_and(i == 0, jnp.logical_and(j == 0, k == 0)))
            def _():
                for swap in swaps:
                    swap.start()

        @pl.when(k == 0)
        def _():
            acc_ref[...] = jnp.zeros_like(acc_ref)

        acc_ref[...] += _dot(a_ref[...], b_ref[...], 0, 0)

        @pl.when(k == nk - 1)
        def _():
            o_ref[...] = acc_ref[...].astype(BF16)

        if exchange is not None:
            @pl.when(jnp.logical_and(i == ni - 1, jnp.logical_and(j == nj - 1, k == nk - 1)))
            def _():
                for swap in swaps:
                    swap.wait()

    def spec(arr, sel, width, which):
        if arr.ndim == 2:
            return pl.BlockSpec((tk, width), lambda i, j, k: (k, (i, j)[which]))
        return pl.BlockSpec((None, tk, width), lambda i, j, k: (sel, k, (i, j)[which]))

    o_spec = pl.BlockSpec((tm, tn), lambda i, j, k: (i, j))
    o_shape = jax.ShapeDtypeStruct((D_MODEL, D_MODEL), BF16)
    if exchange is None:
        return pl.pallas_call(
            body, name=name, grid=(ni, nj, nk),
            in_specs=[spec(a, a_sel, tm, 0), spec(b, b_sel, tn, 1)],
            out_specs=o_spec, out_shape=o_shape,
            scratch_shapes=[pltpu.VMEM((tm, tn), F32)],
            compiler_params=_params("parallel", "parallel", "arbitrary"),
        )(a, b)
    n_ex = 1 + len(g_sq4)
    r_shapes = [jax.ShapeDtypeStruct((g_in.shape[0] // 2, g_in.shape[1]), BF16)]
    r_shapes += [jax.ShapeDtypeStruct((N_CHIPS, hr, D_MODEL), BF16)] * len(g_sq4)
    return pl.pallas_call(
        body, name=name, grid=(ni, nj, nk),
        in_specs=[spec(a, a_sel, tm, 0), spec(b, b_sel, tn, 1)] + [ANY] * n_ex,
        out_specs=(o_spec,) + (ANY,) * n_ex,
        out_shape=(o_shape, *r_shapes),
        scratch_shapes=[pltpu.VMEM((tm, tn), F32), pltpu.SemaphoreType.DMA((n_ex,)),
                        pltpu.SemaphoreType.DMA((n_ex,))],
        compiler_params=pltpu.CompilerParams(dimension_semantics=("arbitrary",) * 3, vmem_limit_bytes=VMEM_LIMIT,
                                             has_side_effects=True),
    )(a, b, g_in, *g_sq4)


def _win_grad(xt, dp, tn, tk):
    _, t, _ = dp.shape
    nk = t // tk
    per_seg = D_MODEL // tn
    nj = N_SEG * per_seg

    def body(x_ref, dp_ref, o_ref, acc_ref):
        k = pl.program_id(1)

        @pl.when(k == 0)
        def _():
            acc_ref[...] = jnp.zeros_like(acc_ref)

        acc_ref[...] += _dot(x_ref[...], dp_ref[...])

        @pl.when(k == nk - 1)
        def _():
            o_ref[...] = acc_ref[...].astype(BF16)

    return pl.pallas_call(
        body, name="grad_w_in", grid=(nj, nk),
        in_specs=[pl.BlockSpec((D_MODEL, tk), lambda j, k: (0, k)),
                  pl.BlockSpec((None, tk, tn), lambda j, k: (_slot_of_seg(j // per_seg), k, j % per_seg))],
        out_specs=pl.BlockSpec((D_MODEL, tn), lambda j, k: (0, j)),
        out_shape=jax.ShapeDtypeStruct((D_MODEL, N_SEG * D_MODEL), BF16),
        scratch_shapes=[pltpu.VMEM((D_MODEL, tn), F32)],
        compiler_params=_params("parallel", "arbitrary"),
    )(xt, dp)


def _input_grad_and_scatter(dp, w_full, gx, s_in, s_o, small, tm):
    _, t, _ = dp.shape
    ni, nk = t // tm, N_SEG - TAIL_SEGS
    _, hd, sc = s_in.shape
    hr = s_o.shape[2]

    def body(dp_ref, w_ref, gx_ref, si_ref, so_ref, sm_ref, o_ref, ri_ref, ro_ref, ga_ref,
             send_sems, recv_sems, local_sem):
        i, k = pl.program_id(0), pl.program_id(1)
        x, y, c, chips = _mesh_pos()
        me = 2 * x + y
        dev = 4 * x + 2 * y + c

        def peer(r):
            return (x ^ ((r >> 2) & 1), y ^ ((r >> 1) & 1), c ^ (r & 1))

        def sends():
            cps = []
            for q, (cx, cy) in enumerate(chips):
                dest = 2 * cx + cy
                cps.append(pltpu.make_async_remote_copy(src_ref=si_ref.at[dest], dst_ref=ri_ref.at[me],
                                                        send_sem=send_sems.at[q], recv_sem=recv_sems.at[q],
                                                        device_id=(cx, cy, c), device_id_type=MESH))
                cps.append(pltpu.make_async_remote_copy(src_ref=so_ref.at[:, dest], dst_ref=ro_ref.at[me],
                                                        send_sem=send_sems.at[3 + q], recv_sem=recv_sems.at[3 + q],
                                                        device_id=(cx, cy, c), device_id_type=MESH))
            for r in range(1, 8):
                cps.append(pltpu.make_async_remote_copy(src_ref=sm_ref, dst_ref=ga_ref.at[dev],
                                                        send_sem=send_sems.at[5 + r], recv_sem=recv_sems.at[5 + r],
                                                        device_id=peer(r), device_id_type=MESH))
            return cps

        own_small = pltpu.make_async_copy(sm_ref, ga_ref.at[dev], local_sem)

        @pl.when(jnp.logical_and(i == 0, k == 0))
        def _():
            for cp in sends():
                cp.start()
            own_small.start()

        @pl.when(k == 0)
        def _():
            o_ref[...] = gx_ref[...]

        o_ref[...] += _dot(dp_ref[...], w_ref[...], 1, 1)

        @pl.when(jnp.logical_and(i == ni - 1, k == nk - 1))
        def _():
            for q, (cx, cy) in enumerate(chips):
                frm = 2 * cx + cy
                pltpu.make_async_remote_copy(src_ref=si_ref.at[frm], dst_ref=ri_ref.at[frm], send_sem=send_sems.at[q],
                                             recv_sem=recv_sems.at[q], device_id=(x, y, c),
                                             device_id_type=MESH).wait_recv()
                pltpu.make_async_remote_copy(src_ref=so_ref.at[:, frm], dst_ref=ro_ref.at[frm],
                                             send_sem=send_sems.at[3 + q], recv_sem=recv_sems.at[3 + q],
                                             device_id=(x, y, c), device_id_type=MESH).wait_recv()
            for r in range(1, 8):
                px, py, pc = peer(r)
                pltpu.make_async_remote_copy(src_ref=sm_ref, dst_ref=ga_ref.at[4 * px + 2 * py + pc],
                                             send_sem=send_sems.at[5 + r], recv_sem=recv_sems.at[5 + r],
                                             device_id=(x, y, c), device_id_type=MESH).wait_recv()
            for cp in sends():
                cp.wait_send()
            own_small.wait()

    return pl.pallas_call(
        body, name="grad_x_and_scatter", grid=(ni, nk),
        in_specs=[pl.BlockSpec((None, tm, D_MODEL), lambda i, k: (_slot_of_seg(k), i, 0)),
                  pl.BlockSpec((D_MODEL, D_MODEL), lambda i, k: (0, k)),
                  pl.BlockSpec((tm, D_MODEL), lambda i, k: (i, 0), pipeline_mode=pl.Buffered(1)), ANY, ANY, ANY],
        out_specs=(pl.BlockSpec((tm, D_MODEL), lambda i, k: (i, 0)), ANY, ANY, ANY),
        out_shape=(jax.ShapeDtypeStruct((t, D_MODEL), F32),
                   jax.ShapeDtypeStruct((N_CHIPS, hd, sc), BF16),
                   jax.ShapeDtypeStruct((N_CHIPS, 3, hr, D_MODEL), BF16),
                   jax.ShapeDtypeStruct((8,) + small.shape, small.dtype)),
        scratch_shapes=[pltpu.SemaphoreType.DMA((13,)), pltpu.SemaphoreType.DMA((13,)), pltpu.SemaphoreType.DMA],
        input_output_aliases={2: 0},
        compiler_params=pltpu.CompilerParams(dimension_semantics=("arbitrary", "arbitrary"),
                                             vmem_limit_bytes=VMEM_LIMIT, has_side_effects=True),
    )(dp, w_full, gx, s_in, s_o, small)


TAIL_SEGS = 1


def _input_grad_tail_and_adam(dp, w_full, partial, w, g, m, v, steps):
    _, t, _ = dp.shape
    tm = t // steps
    r, c = w.shape
    ra = r // steps
    segs = tuple(range(N_SEG - TAIL_SEGS, N_SEG))

    def body(*refs):
        dp_refs = refs[:TAIL_SEGS]
        w_refs = refs[TAIL_SEGS:2 * TAIL_SEGS]
        part_ref, aw_ref, ag_ref, am_ref, av_ref, o_ref, go_ref, d_ref, mo_ref, vo_ref = refs[2 * TAIL_SEGS:]
        acc = part_ref[...]
        for dp_ref, w_ref in zip(dp_refs, w_refs):
            acc = acc + _dot(dp_ref[...], w_ref[...], 1, 1)
        o_ref[...] = acc
        gv = ag_ref[...]
        d, mn, vn = _adam_math(aw_ref[...], gv, am_ref[...], av_ref[...])
        go_ref[...] = gv
        d_ref[...] = d
        mo_ref[...] = mn
        vo_ref[...] = vn

    row = pl.BlockSpec((tm, D_MODEL), lambda i: (i, 0))
    arow = pl.BlockSpec((ra, c), lambda i: (i, 0))
    ashape = jax.ShapeDtypeStruct((r, c), F32)
    in_specs = [pl.BlockSpec((None, tm, D_MODEL), functools.partial(lambda s, i: (SLOT_OF_SEG[s], i, 0), s))
                for s in segs]
    in_specs += [pl.BlockSpec((D_MODEL, D_MODEL), functools.partial(lambda s, i: (0, s), s),
                              pipeline_mode=pl.Buffered(1)) for s in segs]
    in_specs += [row, arow, arow, arow, arow]
    res = pl.pallas_call(
        body, name="grad_x_tail_and_adam_w_in", grid=(steps,),
        in_specs=in_specs,
        out_specs=(row, arow, arow, arow, arow),
        out_shape=(jax.ShapeDtypeStruct((t, D_MODEL), F32), ashape, ashape, ashape, ashape),
        compiler_params=_params("parallel"),
    )(*([dp] * TAIL_SEGS), *([w_full] * TAIL_SEGS), partial, w, g, m, v)
    return res[0], res[1:]


def _sgu_chunk_forward(u, v, z, wm, bs, lng, lnb):
    ug, dug = _gelu_and_grad(u)
    vg, dvg = _gelu_and_grad(v)
    mu = jnp.mean(vg, axis=-1, keepdims=True)
    xc = vg - mu
    var = jnp.mean(xc * xc, axis=-1, keepdims=True)
    rstd = lax.rsqrt(var + LN_EPS)
    vhat = xc * rstd
    vln = (vhat * lng + lnb).astype(BF16)
    mixed = _dot(wm, vln) + bs
    sig = _sigmoid(z)
    return ug, dug, dvg, rstd, vhat, vln, mixed, sig


def _mixer_a_forward(p_a, wm, bs_col, ln_v_g, ln_v_b, tm):
    t = p_a.shape[1]

    def body(p_ref, wm_ref, bs_ref, g_ref, b_ref, o_ref):
        wm_v, bs_v, lng, lnb = wm_ref[...], bs_ref[...], g_ref[...], b_ref[...]

        def chunk(ci, carry):
            rows = pl.ds(pl.multiple_of(ci * CHUNK, CHUNK), CHUNK)
            u = p_ref[0, rows, :].astype(F32)
            v = p_ref[1, rows, :].astype(F32)
            z = p_ref[2, rows, :].astype(F32)
            ug, _, _, _, _, _, mixed, sig = _sgu_chunk_forward(u, v, z, wm_v, bs_v, lng, lnb)
            o_ref[rows, :] = (ug * mixed * (z * sig)).astype(BF16)
            return carry

        lax.fori_loop(0, tm // CHUNK, chunk, 0, unroll=True)

    return pl.pallas_call(
        body, name="mixer_a_forward", grid=(t // tm, N_HEADS),
        in_specs=[pl.BlockSpec((3, tm, HEAD_DIM), lambda i, h: (0, i, h)),
                  pl.BlockSpec((None, CHUNK, CHUNK), lambda i, h: (h, 0, 0)),
                  pl.BlockSpec((None, CHUNK, 1), lambda i, h: (h, 0, 0)),
                  pl.BlockSpec((1, HEAD_DIM), lambda i, h: (0, h)),
                  pl.BlockSpec((1, HEAD_DIM), lambda i, h: (0, h))],
        out_specs=pl.BlockSpec((tm, HEAD_DIM), lambda i, h: (i, h)),
        out_shape=jax.ShapeDtypeStruct((t, D_MODEL), BF16),
        compiler_params=_params("parallel", "parallel"),
    )(p_a, wm, bs_col, ln_v_g, ln_v_b)


def _mixer_a_backward(p_a, dyab, wm, bs_col, ln_v_g, ln_v_b, dp, tm):
    t = p_a.shape[1]

    def body(p_ref, dy_ref, wm_ref, bs_ref, g_ref, b_ref, dp_in, dp_ref, dws_ref, dbs_ref, dg_ref, db_ref):
        @pl.when(pl.program_id(1) == 0)
        def _():
            dws_ref[...] = jnp.zeros_like(dws_ref)
            dbs_ref[...] = jnp.zeros_like(dbs_ref)
            dg_ref[...] = jnp.zeros_like(dg_ref)
            db_ref[...] = jnp.zeros_like(db_ref)

        wm_v, bs_v, lng, lnb = wm_ref[...], bs_ref[...], g_ref[...], b_ref[...]
        causal = (lax.broadcasted_iota(jnp.int32, (CHUNK, CHUNK), 1)
                  <= lax.broadcasted_iota(jnp.int32, (CHUNK, CHUNK), 0))

        def chunk(ci, carry):
            rows = pl.ds(pl.multiple_of(ci * CHUNK, CHUNK), CHUNK)
            u = p_ref[0, rows, :].astype(F32)
            v = p_ref[1, rows, :].astype(F32)
            z = p_ref[2, rows, :].astype(F32)
            dy = dy_ref[rows, :].astype(F32)
            ug, dug, dvg, rstd, vhat, vln, mixed, sig = _sgu_chunk_forward(u, v, z, wm_v, bs_v, lng, lnb)
            sz = z * sig
            dmixed = dy * ug * sz
            dp_ref[0, rows, :] = (dy * mixed * sz * dug).astype(BF16)
            dp_ref[2, rows, :] = (dy * ug * mixed * _silu_grad(sig, sz)).astype(BF16)
            dbs_ref[...] += jnp.sum(dmixed, axis=1, keepdims=True)
            dmb = dmixed.astype(BF16)
            dws_ref[...] += jnp.where(causal, _dot(dmb, vln, 1, 1), 0.0)
            dvln = _dot(wm_v, dmb, 0, 0)
            db_ref[...] += jnp.sum(dvln, axis=0, keepdims=True)
            dg_ref[...] += jnp.sum(dvln * vhat, axis=0, keepdims=True)
            dvh = dvln * lng
            m1 = jnp.mean(dvh, axis=-1, keepdims=True)
            m2 = jnp.mean(dvh * vhat, axis=-1, keepdims=True)
            dp_ref[1, rows, :] = (rstd * (dvh - m1 - vhat * m2) * dvg).astype(BF16)
            return carry

        lax.fori_loop(0, tm // CHUNK, chunk, 0, unroll=True)

    return pl.pallas_call(
        body, name="mixer_a_backward", grid=(N_HEADS, t // tm),
        in_specs=[pl.BlockSpec((3, tm, HEAD_DIM), lambda h, i: (0, i, h)),
                  pl.BlockSpec((None, tm, HEAD_DIM), lambda h, i: (0, i, h)),
                  pl.BlockSpec((None, CHUNK, CHUNK), lambda h, i: (h, 0, 0)),
                  pl.BlockSpec((None, CHUNK, 1), lambda h, i: (h, 0, 0)),
                  pl.BlockSpec((1, HEAD_DIM), lambda h, i: (0, h)),
                  pl.BlockSpec((1, HEAD_DIM), lambda h, i: (0, h)), ANY],
        out_specs=(pl.BlockSpec((3, tm, HEAD_DIM), lambda h, i: (BLOCK_A, i, h)),
                   pl.BlockSpec((None, CHUNK, CHUNK), lambda h, i: (h, 0, 0)),
                   pl.BlockSpec((None, CHUNK, 1), lambda h, i: (h, 0, 0)),
                   pl.BlockSpec((1, HEAD_DIM), lambda h, i: (0, h)),
                   pl.BlockSpec((1, HEAD_DIM), lambda h, i: (0, h))),
        out_shape=(jax.ShapeDtypeStruct(dp.shape, BF16),
                   jax.ShapeDtypeStruct((N_HEADS, CHUNK, CHUNK), F32),
                   jax.ShapeDtypeStruct((N_HEADS, CHUNK, 1), F32),
                   jax.ShapeDtypeStruct((1, D_MODEL), F32), jax.ShapeDtypeStruct((1, D_MODEL), F32)),
        input_output_aliases={6: 0},
        compiler_params=_params("parallel", "arbitrary"),
    )(p_a, dyab, wm, bs_col, ln_v_g, ln_v_b, dp)


HALO = 16


def _conv_taps(h, halo_h, tm):
    row = lax.broadcasted_iota(jnp.int32, h.shape, 0)
    last1 = halo_h[HALO - 1:HALO, :]
    last2 = halo_h[HALO - 2:HALO - 1, :]
    h1 = jnp.where(row == 0, last1, pltpu.roll(h, 1, 0))
    h2 = jnp.where(row == 0, last2, jnp.where(row == 1, last1, pltpu.roll(h, 2, 0)))
    return h1, h2


def _mixer_b_forward(p_b, conv_w, conv_b, tm, tc):
    t = p_b.shape[1]

    def body(p_ref, halo_ref, w_ref, b_ref, o_ref):
        valid = (pl.program_id(1) > 0).astype(F32)
        h = p_ref[1].astype(F32) * p_ref[0].astype(F32)
        halo_h = halo_ref[1].astype(F32) * halo_ref[0].astype(F32) * valid
        h1, h2 = _conv_taps(h, halo_h, tm)
        w = w_ref[...]
        conv = b_ref[...] + w[0:1, :] * h2 + w[1:2, :] * h1 + w[2:3, :] * h
        z = p_ref[3].astype(F32)
        o_ref[...] = (p_ref[2].astype(F32) * conv * (z * _sigmoid(z))).astype(BF16)

    steps = tm // HALO
    return pl.pallas_call(
        body, name="mixer_b_forward", grid=(D_MODEL // tc, t // tm),
        in_specs=[pl.BlockSpec((4, tm, tc), lambda j, i: (BLOCK_B, i, j)),
                  pl.BlockSpec((4, HALO, tc), lambda j, i: (BLOCK_B, jnp.maximum(i * steps - 1, 0), j)),
                  pl.BlockSpec((3, tc), lambda j, i: (0, j)),
                  pl.BlockSpec((1, tc), lambda j, i: (0, j))],
        out_specs=pl.BlockSpec((tm, tc), lambda j, i: (i, j)),
        out_shape=jax.ShapeDtypeStruct((t, D_MODEL), BF16),
        compiler_params=_params("parallel", "parallel"),
    )(p_b, p_b, conv_w, conv_b)


def _mixer_b_backward(p_b, dyab, conv_w, conv_b, dp, tm, tc):
    t = p_b.shape[1]
    n = t // tm

    def body(p_ref, halo_ref, dy_ref, w_ref, b_ref, dp_in, dp_ref, dw_ref, db_ref, next_ref):
        ii = pl.program_id(1)

        @pl.when(ii == 0)
        def _():
            dw_ref[...] = jnp.zeros_like(dw_ref)
            db_ref[...] = jnp.zeros_like(db_ref)
            next_ref[...] = jnp.zeros_like(next_ref)

        valid = (ii < n - 1).astype(F32)
        xb = p_ref[0].astype(F32)
        cb = p_ref[1].astype(F32)
        bb = p_ref[2].astype(F32)
        z = p_ref[3].astype(F32)
        h = cb * xb
        halo_h = halo_ref[1].astype(F32) * halo_ref[0].astype(F32) * valid
        h1, h2 = _conv_taps(h, halo_h, tm)
        w = w_ref[...]
        w0, w1, w2 = w[0:1, :], w[1:2, :], w[2:3, :]
        conv = b_ref[...] + w0 * h2 + w1 * h1 + w2 * h
        sig = _sigmoid(z)
        sz = z * sig
        dy = dy_ref[...].astype(F32)
        dconv = dy * bb * sz
        dp_ref[2] = (dy * conv * sz).astype(BF16)
        dp_ref[3] = (dy * bb * conv * _silu_grad(sig, sz)).astype(BF16)
        db_ref[...] += jnp.sum(dconv, axis=0, keepdims=True)
        dw_ref[0:1, :] += jnp.sum(dconv * h2, axis=0, keepdims=True)
        dw_ref[1:2, :] += jnp.sum(dconv * h1, axis=0, keepdims=True)
        dw_ref[2:3, :] += jnp.sum(dconv * h, axis=0, keepdims=True)
        row = lax.broadcasted_iota(jnp.int32, h.shape, 0)
        nxt = next_ref[...]
        n0, n1 = nxt[0:1, :], nxt[1:2, :]
        d1 = jnp.where(row == tm - 1, n0, pltpu.roll(dconv, tm - 1, 0))
        d2 = jnp.where(row == tm - 1, n1, jnp.where(row == tm - 2, n0, pltpu.roll(dconv, tm - 2, 0)))
        dh = w2 * dconv + w1 * d1 + w0 * d2
        dp_ref[0] = (dh * cb).astype(BF16)
        dp_ref[1] = (dh * xb).astype(BF16)
        next_ref[...] = dconv[0:8, :]

    steps = tm // HALO
    return pl.pallas_call(
        body, name="mixer_b_backward", grid=(D_MODEL // tc, n),
        in_specs=[pl.BlockSpec((4, tm, tc), lambda j, ii: (BLOCK_B, n - 1 - ii, j)),
                  pl.BlockSpec((4, HALO, tc), lambda j, ii: (BLOCK_B, jnp.maximum((n - 1 - ii) * steps - 1, 0), j)),
                  pl.BlockSpec((None, tm, tc), lambda j, ii: (1, n - 1 - ii, j)),
                  pl.BlockSpec((3, tc), lambda j, ii: (0, j)),
                  pl.BlockSpec((1, tc), lambda j, ii: (0, j)), ANY],
        out_specs=(pl.BlockSpec((4, tm, tc), lambda j, ii: (BLOCK_B, n - 1 - ii, j)),
                   pl.BlockSpec((3, tc), lambda j, ii: (0, j)),
                   pl.BlockSpec((1, tc), lambda j, ii: (0, j))),
        out_shape=(jax.ShapeDtypeStruct(dp.shape, BF16),
                   jax.ShapeDtypeStruct((3, D_MODEL), F32), jax.ShapeDtypeStruct((1, D_MODEL), F32)),
        scratch_shapes=[pltpu.VMEM((8, tc), F32)],
        input_output_aliases={5: 0},
        compiler_params=_params("parallel", "arbitrary"),
    )(p_b, p_b, dyab, conv_w, conv_b, dp)


def _adam_math(w, g, m, v):
    m = ADAM_B1 * m + (1.0 - ADAM_B1) * g
    v = ADAM_B2 * v + (1.0 - ADAM_B2) * (g * g)
    delta = -ADAM_LR * ((m * ADAM_C1) / (jnp.sqrt(v * ADAM_C2) + ADAM_EPS) + ADAM_WD * w)
    return delta, m, v


def _adam_rows(w, g, m, v, tm, name, g_sel=None):
    r, c = w.shape

    def body(w_ref, g_ref, m_ref, v_ref, go_ref, d_ref, mo_ref, vo_ref):
        g = g_ref[...]
        d, mn, vn = _adam_math(w_ref[...], g, m_ref[...], v_ref[...])
        go_ref[...] = g
        d_ref[...] = d
        mo_ref[...] = mn
        vo_ref[...] = vn

    spec = pl.BlockSpec((tm, c), lambda i: (i, 0))
    g_spec = spec if g_sel is None else pl.BlockSpec((None, tm, c), lambda i: (g_sel, i, 0))
    shape = jax.ShapeDtypeStruct((r, c), F32)
    return pl.pallas_call(
        body, name=name, grid=(r // tm,),
        in_specs=[spec, g_spec, spec, spec], out_specs=(spec,) * 4, out_shape=(shape,) * 4,
        compiler_params=_params("parallel"),
    )(w, g, m, v)


SMALL_ROW0 = {name: sum(r for _, r in SMALL_ROWS[:i]) for i, (name, _) in enumerate(SMALL_ROWS)}
LANE_MAJOR = ("ln_g", "ln_b", "b_gate", "ln_v_g", "ln_v_b", "conv_b")


def _lane_pieces(n):
    return [(q, slice(q * 128, (q + 1) * 128)) for q in range(n // 128)]


def _pack_small(d_ln_g, d_ln_b, d_bg, d_lnv_g, d_lnv_b, d_ws, d_bs, d_cw, d_cb, loss_part):
    def body(lg, lb, bg, vg, vb, ws, bs, cw, cb, loss, o_ref):
        def put(row0, vec):
            for q, cols in _lane_pieces(vec.shape[1]):
                o_ref[row0 + q:row0 + q + 1, :] = vec[:, cols]

        put(SMALL_ROW0["ln_g"], lg[...])
        put(SMALL_ROW0["ln_b"], lb[...])
        for n in range(2):
            put(SMALL_ROW0["b_gate"] + n * (D_MODEL // 128), bg[n])
        put(SMALL_ROW0["ln_v_g"], vg[...])
        put(SMALL_ROW0["ln_v_b"], vb[...])
        for h in range(N_HEADS):
            o_ref[SMALL_ROW0["w_s"] + h * CHUNK:SMALL_ROW0["w_s"] + (h + 1) * CHUNK, :] = ws[h]
        o_ref[SMALL_ROW0["b_s"]:SMALL_ROW0["b_s"] + N_HEADS, :] = bs[...]
        for c in range(3):
            put(SMALL_ROW0["conv_w"] + c * (D_MODEL // 128), cw[c:c + 1, :])
        put(SMALL_ROW0["conv_b"], cb[...])
        o_ref[SMALL_ROW0["loss"]:SMALL_ROW0["loss"] + 8, :] = jnp.broadcast_to(loss[...], (8, 128))

    return pl.pallas_call(
        body, name="pack_small", out_shape=jax.ShapeDtypeStruct((SMALL_TOTAL, 128), F32), compiler_params=_params(),
    )(d_ln_g, d_ln_b, d_bg, d_lnv_g, d_lnv_b, d_ws, d_bs.reshape(N_HEADS, CHUNK), d_cw, d_cb, loss_part)


def _adam_small(gathered, params):
    names = list(params)
    flat = [a for n in names for a in params[n]]

    def body(*refs):
        ga_ref = refs[0]
        ins = refs[1:1 + 3 * len(names)]
        outs = refs[1 + 3 * len(names):-1]
        gs_ref = refs[-1]
        g = ga_ref[0]
        for k in range(1, 8):
            g = g + ga_ref[k]
        gs_ref[...] = g
        for i, name in enumerate(names):
            w_ref, m_ref, v_ref = ins[3 * i:3 * i + 3]
            o_refs = outs[4 * i:4 * i + 4]
            row0 = SMALL_ROW0[name]
            if name in LANE_MAJOR:
                pieces = [((slice(None), cols), slice(row0 + q, row0 + q + 1))
                          for q, cols in _lane_pieces(w_ref.shape[1])]
            elif name == "w_s":
                pieces = [((0, h), slice(row0 + h * CHUNK, row0 + (h + 1) * CHUNK)) for h in range(N_HEADS)]
            else:
                pieces = [((0,), slice(row0, row0 + N_HEADS))]
            for idx, rows in pieces:
                gp = gs_ref[rows, :]
                res = (gp,) + _adam_math(w_ref[idx], gp, m_ref[idx], v_ref[idx])
                for o_ref, val in zip(o_refs, res):
                    o_ref[idx] = val
        gcw_ref, loss_ref = outs[-2:]
        for c in range(3):
            for q, cols in _lane_pieces(D_MODEL):
                r = SMALL_ROW0["conv_w"] + c * (D_MODEL // 128) + q
                gcw_ref[c:c + 1, cols] = gs_ref[r:r + 1, :]
        loss_ref[...] = gs_ref[SMALL_ROW0["loss"]:SMALL_ROW0["loss"] + 1, :]

    out_shape = [jax.ShapeDtypeStruct(params[n][0].shape, F32) for n in names for _ in range(4)]
    out_shape += [jax.ShapeDtypeStruct((3, D_MODEL), F32), jax.ShapeDtypeStruct((1, 128), F32)]
    res = pl.pallas_call(
        body, name="adam_small", out_shape=tuple(out_shape),
        scratch_shapes=[pltpu.VMEM((SMALL_TOTAL, 128), F32)], compiler_params=_params(),
    )(gathered, *flat)
    return {n: res[4 * i:4 * i + 4] for i, n in enumerate(names)}, res[-2], res[-1]


def _adam_conv_w(g_all, chip, w, m, v):
    cols = w.shape[2]

    def body(c_ref, g_ref, w_ref, m_ref, v_ref, go_ref, d_ref, mo_ref, vo_ref):
        g = g_ref[...]
        d, mn, vn = _adam_math(w_ref[...], g, m_ref[...], v_ref[...])
        go_ref[...] = g
        d_ref[...] = d
        mo_ref[...] = mn
        vo_ref[...] = vn

    own = pl.BlockSpec((None, 3, cols), lambda i, c_ref: (0, 0, 0))
    return pl.pallas_call(
        body, name="adam_conv_w",
        grid_spec=pltpu.PrefetchScalarGridSpec(
            num_scalar_prefetch=1, grid=(1,),
            in_specs=[pl.BlockSpec((3, cols), lambda i, c_ref: (0, c_ref[0])), own, own, own],
            out_specs=(own,) * 4),
        out_shape=(jax.ShapeDtypeStruct(w.shape, F32),) * 4,
        compiler_params=_params("arbitrary"),
    )(chip, g_all, w, m, v)


def kernel(x, w_in, b_gate, ln_v_g, ln_v_b, w_s, b_s, conv_w, conv_b, w_oa, w_ob, w_out, ln_g, ln_b, loss_target, m_w_in, m_b_gate, m_ln_v_g, m_ln_v_b, m_w_s, m_b_s, m_conv_w, m_conv_b, m_w_oa, m_w_ob, m_w_out, m_ln_g, m_ln_b, v_w_in, v_b_gate, v_ln_v_g, v_ln_v_b, v_w_s, v_b_s, v_conv_w, v_conv_b, v_w_oa, v_w_ob, v_w_out, v_ln_g, v_ln_b):
    t = x.shape[1]
    x2 = x[0]
    target = loss_target[0]
    chip = 2 * lax.axis_index("x") + lax.axis_index("y")
    conv_cols = conv_w.shape[2]

    chip1 = chip.astype(jnp.int32).reshape(1)
    w_pre = _cast_into_columns(w_in[0], chip1, N_CHIPS, 256, "cast_w_in")
    wo_b, wm = _prep_small_weights(w_oa[0], w_ob[0], w_out[0], w_s[0])
    conv_w8 = jnp.concatenate([conv_w[0], jnp.zeros((5, conv_cols), F32)], axis=0)
    bs_col = b_s[0].reshape(N_HEADS, CHUNK, 1)
    bg = b_gate.reshape(2, 1, D_MODEL)

    xb, xt = _cast_and_transpose(x2, min(512, t))
    p, w_full, wo_full, cw_full = _gather_and_project(xb, w_pre, wo_b, conv_w8, min(4096, t))
    wo3 = wo_full.reshape(3, D_MODEL, D_MODEL)
    conv_w_all = jnp.transpose(cw_full[:, :3, :], (1, 0, 2)).reshape(3, D_MODEL)
    tm_a = min(2048, t)
    ya = _mixer_a_forward(p, wm, bs_col, ln_v_g, ln_v_b, tm_a)
    tm_b = min(1024, t)
    yb = _mixer_b_forward(p, conv_w_all, conv_b, min(2048, t), 512)
    tm_m = min(1024, t)
    merged, oab = _merge_forward(ya, yb, wo3, p, bg, tm_m, 512)

    drb, gx, dmerged, d_ln_g, d_ln_b, loss_part = _head(merged, wo3, x2, target, ln_g, ln_b, min(512, t))
    doab, dp, d_bg, dyab = _gate_and_branch_backward(dmerged, oab, p, bg, wo3, min(256, t))
    dp, d_ws, d_bs, d_lnv_g, d_lnv_b = _mixer_a_backward(p, dyab, wm, bs_col, ln_v_g, ln_v_b, dp, min(1024, t))
    dp, d_cw, d_cb = _mixer_b_backward(p, dyab, conv_w_all, conv_b, dp, tm_b, 512)

    tk = min(2048, t)
    g_in = _win_grad(xt, dp, 1024, tk)
    g_oa = _weight_grad(ya, 0, doab, 0, 1024, 1024, tk, "grad_w_oa")
    g_ob = _weight_grad(yb, 0, doab, 1, 1024, 1024, tk, "grad_w_ob")
    g_out, r_in, r_oa, r_ob = _weight_grad(merged, 0, drb, 0, 1024, 1024, tk, "grad_w_out",
                                           exchange=(g_in, g_oa, g_ob))
    s_in, s_o = _add_halves(g_in, (g_oa, g_ob, g_out), r_in, (r_oa, r_ob))
    small_part = _pack_small(d_ln_g, d_ln_b, d_bg, d_lnv_g, d_lnv_b, d_ws, d_bs, d_cw, d_cb, loss_part)
    gx, q_in, q_o, gathered = _input_grad_and_scatter(dp, w_full, gx, s_in, s_o, small_part, min(1024, t))
    f_in, f_o = _sum_chips(q_in, q_o, s_in, s_o)
    gsum_in, gsum_o = _share_halves(f_in, f_o)

    big = {}
    grad_x, big["w_in"] = _input_grad_tail_and_adam(dp, w_full, gx, w_in[0], gsum_in, m_w_in[0], v_w_in[0],
                                                    min(32, t // 128))
    for n, (name, w, m, v) in enumerate((("w_oa", w_oa, m_w_oa, v_w_oa), ("w_ob", w_ob, m_w_ob, v_w_ob),
                                         ("w_out", w_out, m_w_out, v_w_out))):
        big[name] = _adam_rows(w[0], gsum_o, m[0], v[0], 256, "adam_" + name, g_sel=n)

    small, g_conv_w, loss_row = _adam_small(gathered, {
        "ln_g": (ln_g, m_ln_g, v_ln_g), "ln_b": (ln_b, m_ln_b, v_ln_b), "b_gate": (b_gate, m_b_gate, v_b_gate),
        "ln_v_g": (ln_v_g, m_ln_v_g, v_ln_v_g), "ln_v_b": (ln_v_b, m_ln_v_b, v_ln_v_b),
        "w_s": (w_s, m_w_s, v_w_s), "b_s": (b_s, m_b_s, v_b_s), "conv_b": (conv_b, m_conv_b, v_conv_b)})
    small["conv_w"] = _adam_conv_w(g_conv_w, chip1, conv_w, m_conv_w, v_conv_w)
    loss = loss_row[0, 0]

    order = ("w_in", "b_gate", "ln_v_g", "ln_v_b", "w_s", "b_s", "conv_w", "conv_b", "w_oa", "w_ob", "w_out",
             "ln_g", "ln_b")
    outs = [loss, grad_x[None]]
    for which in range(4):
        for name in order:
            outs.append(big[name][which][None] if name in big else small[name][which])
    return tuple(outs)
```

```python
import functools
import math

import jax
import jax.numpy as jnp
from jax import lax
from jax.experimental import pallas as pl
from jax.experimental.pallas import tpu as pltpu

F32 = jnp.float32
BF16 = jnp.bfloat16

D_MODEL = 2048
N_HEADS = 8
HEAD_DIM = D_MODEL // N_HEADS
CHUNK = 128
N_SEG = 9
N_CHIPS = 4
SHARD_COLS = N_SEG * D_MODEL // N_CHIPS
COL_BLOCK = 512
BLOCKS_PER_SHARD = SHARD_COLS // COL_BLOCK
BLOCKS_PER_SEG = D_MODEL // COL_BLOCK
SUBS = 3
SHARD_ROWS = D_MODEL // N_CHIPS
N_SLOTS = 12
BLOCK_A, BLOCK_G, BLOCK_B = 0, 2, 2
SLOT_OF_SEG = (0, 1, 2, 8, 9, 10, 11, 4, 5)
DN_ALPHA = 2.0 ** 0.25
LN_EPS = 1e-5
GELU_K = math.sqrt(2.0 / math.pi)
GELU_C = 0.044715

ADAM_LR = 0.001
ADAM_B1 = 0.9
ADAM_B2 = 0.999
ADAM_EPS = 1e-08
ADAM_WD = 0.01
ADAM_STEP = 10
ADAM_C1 = 1.0 / (1.0 - ADAM_B1 ** ADAM_STEP)
ADAM_C2 = 1.0 / (1.0 - ADAM_B2 ** ADAM_STEP)

VMEM_LIMIT = 60 * 1024 * 1024
MESH = pl.DeviceIdType.MESH
ANY = pl.BlockSpec(memory_space=pl.ANY)

SMALL_ROWS = (("ln_g", 16), ("ln_b", 16), ("b_gate", 32), ("ln_v_g", 16), ("ln_v_b", 16),
              ("w_s", 1024), ("b_s", 8), ("conv_w", 48), ("conv_b", 16), ("loss", 8))
SMALL_TOTAL = sum(r for _, r in SMALL_ROWS)


def _params(*sem):
    return pltpu.CompilerParams(dimension_semantics=sem, vmem_limit_bytes=VMEM_LIMIT)


def _sigmoid(x):
    return 1.0 / (1.0 + jnp.exp(-x))


def _gelu_gate(x, x2):
    return 1.0 / (1.0 + jnp.exp(x * ((-2.0 * GELU_K) + (-2.0 * GELU_K * GELU_C) * x2)))


def _gelu_and_grad(x):
    x2 = x * x
    s = _gelu_gate(x, x2)
    g = x * s
    dg = s + g * (1.0 - s) * ((2.0 * GELU_K) + (6.0 * GELU_K * GELU_C) * x2)
    return g, dg


def _silu_grad(sig, sz):
    return sig + sz * (1.0 - sig)


def _dot(a, b, ca=1, cb=0):
    return lax.dot_general(a, b, (((ca,), (cb,)), ((), ())), preferred_element_type=F32)


def _cast_and_transpose(x, tm):
    t, d = x.shape

    def body(x_ref, o_ref, ot_ref):
        v = x_ref[...]
        o_ref[...] = v.astype(BF16)
        ot_ref[...] = v.T.astype(BF16)

    return pl.pallas_call(
        body, name="cast_x", grid=(t // tm,),
        in_specs=[pl.BlockSpec((tm, d), lambda i: (i, 0))],
        out_specs=(pl.BlockSpec((tm, d), lambda i: (i, 0)), pl.BlockSpec((d, tm), lambda i: (0, i))),
        out_shape=(jax.ShapeDtypeStruct((t, d), BF16), jax.ShapeDtypeStruct((d, t), BF16)),
        compiler_params=_params("parallel"),
    )(x)


def _cast_into_columns(w, slot, n_slots, tm, name):
    r, c = w.shape

    def body(s_ref, w_ref, o_ref):
        o_ref[...] = w_ref[...].astype(BF16)

    return pl.pallas_call(
        body, name=name,
        grid_spec=pltpu.PrefetchScalarGridSpec(
            num_scalar_prefetch=1, grid=(r // tm,),
            in_specs=[pl.BlockSpec((tm, c), lambda i, s_ref: (i, 0))],
            out_specs=pl.BlockSpec((tm, c), lambda i, s_ref: (i, s_ref[0]))),
        out_shape=jax.ShapeDtypeStruct((r, n_slots * c), BF16),
        compiler_params=_params("parallel"),
    )(slot, w)


def _prep_small_weights(w_oa, w_ob, w_out, w_s):
    rows = w_oa.shape[0]

    def body(a_ref, b_ref, c_ref, ws_ref, wo_ref, wm_ref):
        wo_ref[0] = a_ref[...].astype(BF16)
        wo_ref[1] = b_ref[...].astype(BF16)
        wo_ref[2] = c_ref[...].astype(BF16)
        t = lax.broadcasted_iota(jnp.int32, (CHUNK, CHUNK), 0)
        s = lax.broadcasted_iota(jnp.int32, (CHUNK, CHUNK), 1)
        for h in range(N_HEADS):
            wm_ref[h] = jnp.where(s <= t, ws_ref[h], 0.0).astype(BF16)

    return pl.pallas_call(
        body, name="prep_small_weights",
        out_shape=(jax.ShapeDtypeStruct((3, rows, D_MODEL), BF16),
                   jax.ShapeDtypeStruct((N_HEADS, CHUNK, CHUNK), BF16)),
        compiler_params=_params(),
    )(w_oa, w_ob, w_out, w_s)


def _mesh_pos():
    x, y, c = lax.axis_index("x"), lax.axis_index("y"), lax.axis_index("c")
    chips = [(1 - x, y), (x, 1 - y), (1 - x, 1 - y)]
    return x, y, c, chips


def _slot_of_seg(seg):
    return jnp.where(seg < 3, seg, jnp.where(seg < 7, seg + 5, seg - 3))


def _gather_and_project(xb, w_pre, wo_b, conv_w8, tm):
    t = xb.shape[0]
    d, sc = w_pre.shape[0], w_pre.shape[1] // N_CHIPS
    rows = wo_b.shape[1]
    hd, hr = d // 2, rows // 2
    nj = BLOCKS_PER_SHARD // SUBS
    pc = nj * COL_BLOCK
    units = N_CHIPS * SUBS
    ni = t // tm
    total = units * ni * nj
    mx, my = lax.axis_index("x"), lax.axis_index("y")
    order = jnp.stack([2 * mx + my, 2 * (1 - mx) + my, 2 * mx + (1 - my),
                       2 * (1 - mx) + (1 - my)]).astype(jnp.int32)

    def body(order_ref, x_ref, wpre_ref, wo_ref, cw_ref, p_ref, wf_ref, wof_ref, cwf_ref,
             wbuf, wsem, xbuf, xsem, send_sems, recv_sems, local_sems):
        x, y, c, chips = _mesh_pos()
        me = 2 * x + y
        sibling = (x, y, 1 - c)
        u, i, j = pl.program_id(0), pl.program_id(1), pl.program_id(2)
        n = (u * ni + i) * nj + j
        m = u * ni + i

        def rows_start(m_):
            pltpu.make_async_copy(x_ref.at[pl.ds(pl.multiple_of(lax.rem(m_, ni) * tm, tm), tm), :],
                                  xbuf.at[lax.rem(m_, 2)], xsem.at[lax.rem(m_, 2)]).start()

        def chip_of(q):
            return 2 * chips[q][0] + chips[q][1]

        def piece(ref, k, cc, r):
            return ref.at[pl.ds(cc * hd, hd), pl.ds(pl.multiple_of(k * sc + r * pc, COL_BLOCK), pc)]

        def wo_half(ref4, k, cc):
            return ref4.at[:, k, pl.ds(cc * hr, hr), :]

        def rcopy(sem, src, dst, to):
            return pltpu.make_async_remote_copy(src_ref=src, dst_ref=dst, send_sem=send_sems.at[sem],
                                                recv_sem=recv_sems.at[sem], device_id=to, device_id_type=MESH)

        def w_send(q, r):
            return rcopy(q * SUBS + r, piece(wpre_ref, me, c, r), piece(wf_ref, me, c, r), (*chips[q], c))

        def w_landed(q, r):
            return rcopy(q * SUBS + r, piece(wpre_ref, me, c, r), piece(wf_ref, chip_of(q), c, r), sibling)

        def w_forward(q, r, cc):
            ref = piece(wf_ref, chip_of(q), cc, r)
            return rcopy(9 + q * SUBS + r, ref, ref, sibling)

        def wo_send(q):
            return rcopy(18 + q, wo_ref.at[:, pl.ds(c * hr, hr), :], wo_half(wof_ref, me, c), (*chips[q], c))

        def wo_landed(q):
            return rcopy(18 + q, wo_ref.at[:, pl.ds(c * hr, hr), :], wo_half(wof_ref, chip_of(q), c), sibling)

        def wo_forward(q, cc):
            ref = wo_half(wof_ref, chip_of(q), cc)
            return rcopy(21 + q, ref, ref, sibling)

        def conv_send(q):
            return rcopy(24 + q, cw_ref, cwf_ref.at[me], (*chips[q], c))

        def local_copies():
            return [pltpu.make_async_copy(wo_ref, wof_ref.at[:, me], local_sems.at[0]),
                    pltpu.make_async_copy(cw_ref, cwf_ref.at[me], local_sems.at[1])]

        def tile_start(u_, j_, slot):
            g = order_ref[u_ // SUBS] * BLOCKS_PER_SHARD + lax.rem(u_, SUBS) * nj + j_
            cols = pl.ds(pl.multiple_of(g * COL_BLOCK, COL_BLOCK), COL_BLOCK)

            @pl.when(u_ < SUBS)
            def _():
                pltpu.make_async_copy(wpre_ref.at[:, cols], wbuf.at[slot], wsem.at[slot]).start()

            @pl.when(u_ >= SUBS)
            def _():
                pltpu.make_async_copy(wf_ref.at[:, cols], wbuf.at[slot], wsem.at[slot]).start()

        def end_of(u_):
            return jnp.logical_and(u == u_, jnp.logical_and(i == ni - 1, j == nj - 1))

        @pl.when(n == 0)
        def _():
            for cp in local_copies():
                cp.start()
            for r in range(SUBS):
                for q in (0, 1):
                    w_send(q, r).start()
            for q in range(3):
                conv_send(q).start()
            tile_start(0, 0, 0)
            rows_start(0)

        def pass_on(q, r):
            w_landed(q, r).wait_recv()
            w_forward(q, r, c).start()

        for u_ in range(1, units - 1):
            @pl.when(end_of(u_))
            def _(u_=u_):
                if u_ <= SUBS:
                    for q in (0, 1):
                        pass_on(q, u_ - 1)
                if u_ == SUBS:
                    for r in range(SUBS):
                        for q in (0, 1):
                            w_send(q, r).wait_send()
                    for r in range(SUBS):
                        w_send(2, r).start()
                if 2 * SUBS - 1 <= u_ <= 3 * SUBS - 2:
                    pass_on(2, u_ - (2 * SUBS - 1))
                if u_ == 3 * SUBS - 2:
                    for r in range(SUBS):
                        w_send(2, r).wait_send()
                    for q in range(3):
                        wo_send(q).start()
                nxt = u_ + 1
                if nxt >= SUBS:
                    w_forward(nxt // SUBS - 1, nxt % SUBS, 1 - c).wait_recv()

        n1 = n + 1

        @pl.when(n1 < total)
        def _():
            tile_start(n1 // (ni * nj), lax.rem(n1, nj), lax.rem(n1, 2))

        xslot = lax.rem(m, 2)

        @pl.when(j == 0)
        def _():
            @pl.when(m + 1 < units * ni)
            def _():
                rows_start(m + 1)

            pltpu.make_async_copy(x_ref.at[pl.ds(0, tm), :], xbuf.at[xslot], xsem.at[xslot]).wait()

        slot = lax.rem(n, 2)
        pltpu.make_async_copy(wpre_ref.at[:, pl.ds(0, COL_BLOCK)], wbuf.at[slot], wsem.at[slot]).wait()
        p_ref[...] = _dot(xbuf[xslot], wbuf[slot]).astype(BF16)

        @pl.when(n == total - 1)
        def _():
            for q in range(3):
                wo_landed(q).wait_recv()
                wo_forward(q, c).start()
            for q in range(3):
                wo_forward(q, 1 - c).wait_recv()
                rcopy(24 + q, cw_ref, cwf_ref.at[chip_of(q)], sibling).wait_recv()
            for q in range(3):
                for r in range(SUBS):
                    w_forward(q, r, c).wait_send()
                wo_send(q).wait_send()
                wo_forward(q, c).wait_send()
                conv_send(q).wait_send()
            for cp in local_copies():
                cp.wait()

    def p_map(u, i, j, o):
        g = o[u // SUBS] * BLOCKS_PER_SHARD + lax.rem(u, SUBS) * nj + j
        return (_slot_of_seg(g // BLOCKS_PER_SEG), i, lax.rem(g, BLOCKS_PER_SEG))

    return pl.pallas_call(
        body, name="gather_and_project",
        grid_spec=pltpu.PrefetchScalarGridSpec(
            num_scalar_prefetch=1, grid=(units, ni, nj),
            in_specs=[ANY, ANY, ANY, ANY],
            out_specs=(pl.BlockSpec((None, tm, COL_BLOCK), p_map), ANY, ANY, ANY),
            scratch_shapes=[pltpu.VMEM((2, d, COL_BLOCK), BF16), pltpu.SemaphoreType.DMA((2,)),
                            pltpu.VMEM((2, tm, D_MODEL), BF16), pltpu.SemaphoreType.DMA((2,)),
                            pltpu.SemaphoreType.DMA((27,)), pltpu.SemaphoreType.DMA((27,)),
                            pltpu.SemaphoreType.DMA((2,))]),
        out_shape=(jax.ShapeDtypeStruct((N_SLOTS, t, D_MODEL), BF16),
                   jax.ShapeDtypeStruct((d, N_CHIPS * sc), BF16),
                   jax.ShapeDtypeStruct((3, N_CHIPS, rows, D_MODEL), BF16),
                   jax.ShapeDtypeStruct((N_CHIPS,) + conv_w8.shape, F32)),
        input_output_aliases={2: 1},
        compiler_params=pltpu.CompilerParams(dimension_semantics=("arbitrary",) * 3, vmem_limit_bytes=VMEM_LIMIT,
                                             has_side_effects=True),
    )(order, xb, w_pre, wo_b, conv_w8)


def _add_halves(g_in, g_o, r_in, r_ab):
    d, c9 = g_in.shape
    hd = d // 2
    hr = SHARD_ROWS // 2
    core = lax.axis_index("c").astype(jnp.int32).reshape(1)
    tm = min(512, hd)
    nb = hd // tm
    g_o4 = [g.reshape(N_CHIPS, 2, hr, D_MODEL) for g in g_o]

    def body_in(c_ref, g_ref, r_ref, gout_ref, o_ref, rout_ref, send_sem, recv_sem):
        k, i = pl.program_id(0), pl.program_id(1)
        x, y, c, _ = _mesh_pos()
        swap = pltpu.make_async_remote_copy(src_ref=gout_ref.at[:, 1 - c], dst_ref=rout_ref, send_sem=send_sem,
                                            recv_sem=recv_sem, device_id=(x, y, 1 - c), device_id_type=MESH)

        @pl.when(jnp.logical_and(k == 0, i == 0))
        def _():
            swap.start()

        o_ref[...] = (g_ref[...].astype(F32) + r_ref[...].astype(F32)).astype(BF16)

        @pl.when(jnp.logical_and(k == N_CHIPS - 1, i == nb - 1))
        def _():
            swap.wait()

    s_in, r_out = pl.pallas_call(
        body_in, name="rs_add_halves_in",
        grid_spec=pltpu.PrefetchScalarGridSpec(
            num_scalar_prefetch=1, grid=(N_CHIPS, nb),
            in_specs=[pl.BlockSpec((tm, SHARD_COLS), lambda k, i, c_ref: (c_ref[0] * nb + i, k)),
                      pl.BlockSpec((tm, SHARD_COLS), lambda k, i, c_ref: (i, k)), ANY],
            out_specs=(pl.BlockSpec((None, tm, SHARD_COLS), lambda k, i, c_ref: (k, i, 0)), ANY),
            scratch_shapes=[pltpu.SemaphoreType.DMA, pltpu.SemaphoreType.DMA]),
        out_shape=(jax.ShapeDtypeStruct((N_CHIPS, hd, SHARD_COLS), BF16),
                   jax.ShapeDtypeStruct((N_CHIPS, hr, D_MODEL), BF16)),
        compiler_params=pltpu.CompilerParams(dimension_semantics=("arbitrary", "arbitrary"),
                                             vmem_limit_bytes=VMEM_LIMIT, has_side_effects=True),
    )(core, g_in, r_in, g_o4[2])
    r_o = (*r_ab, r_out)

    def body_o(c_ref, ga_ref, gb_ref, gc_ref, ra_ref, rb_ref, rc_ref, o_ref):
        for n, (g_ref, r_ref) in enumerate(((ga_ref, ra_ref), (gb_ref, rb_ref), (gc_ref, rc_ref))):
            o_ref[n] = (g_ref[...].astype(F32) + r_ref[...].astype(F32)).astype(BF16)

    gspec = pl.BlockSpec((None, None, hr, D_MODEL), lambda k, c_ref: (k, c_ref[0], 0, 0))
    rspec = pl.BlockSpec((None, hr, D_MODEL), lambda k, c_ref: (k, 0, 0))
    s_o = pl.pallas_call(
        body_o, name="rs_add_halves_o",
        grid_spec=pltpu.PrefetchScalarGridSpec(
            num_scalar_prefetch=1, grid=(N_CHIPS,),
            in_specs=[gspec] * 3 + [rspec] * 3,
            out_specs=pl.BlockSpec((3, None, hr, D_MODEL), lambda k, c_ref: (0, k, 0, 0))),
        out_shape=jax.ShapeDtypeStruct((3, N_CHIPS, hr, D_MODEL), BF16),
        compiler_params=_params("parallel"),
    )(core, *g_o4, *r_o)
    return s_in, s_o


def _sum_chips(r_in, r_o, s_in, s_o):
    _, hd, sc = r_in.shape
    hr = r_o.shape[2]
    tm = min(256, hd)
    nb = hd // tm
    pos = jnp.stack([2 * lax.axis_index("x") + lax.axis_index("y"), lax.axis_index("c")]).astype(jnp.int32)

    def chip_sum(pos_ref, r_ref, s_ref):
        acc = None
        for k in range(N_CHIPS):
            term = jnp.where(pos_ref[0] == k, s_ref[...], r_ref[k]).astype(F32)
            acc = term if acc is None else acc + term
        return acc

    def body_in(pos_ref, r_ref, s_ref, o_ref):
        o_ref[...] = chip_sum(pos_ref, r_ref, s_ref)

    f_in = pl.pallas_call(
        body_in, name="rs_sum_chips_in",
        grid_spec=pltpu.PrefetchScalarGridSpec(
            num_scalar_prefetch=1, grid=(nb,),
            in_specs=[pl.BlockSpec((N_CHIPS, tm, sc), lambda i, p: (0, i, 0)),
                      pl.BlockSpec((None, tm, sc), lambda i, p: (p[0], i, 0))],
            out_specs=pl.BlockSpec((tm, sc), lambda i, p: (p[1] * nb + i, 0))),
        out_shape=jax.ShapeDtypeStruct((2 * hd, sc), F32),
        compiler_params=_params("parallel"),
    )(pos, r_in, s_in)

    def body_o(pos_ref, r_ref, s_ref, o_ref):
        o_ref[...] = chip_sum(pos_ref, r_ref, s_ref)

    f_o = pl.pallas_call(
        body_o, name="rs_sum_chips_o",
        grid_spec=pltpu.PrefetchScalarGridSpec(
            num_scalar_prefetch=1, grid=(3,),
            in_specs=[pl.BlockSpec((N_CHIPS, None, hr, D_MODEL), lambda n, p: (0, n, 0, 0)),
                      pl.BlockSpec((None, None, hr, D_MODEL), lambda n, p: (n, p[0], 0, 0))],
            out_specs=pl.BlockSpec((None, hr, D_MODEL), lambda n, p: (n, p[1], 0))),
        out_shape=jax.ShapeDtypeStruct((3, 2 * hr, D_MODEL), F32),
        compiler_params=_params("parallel"),
    )(pos, r_o, s_o)
    return f_in, f_o


def _share_halves(f_in, f_o):
    hd, sc = f_in.shape[0] // 2, f_in.shape[1]
    hr = f_o.shape[1] // 2

    def body(fi_ref, fo_ref, gi_ref, go_ref, send_sems, recv_sems):
        x, y, c, _ = _mesh_pos()
        sibling = (x, y, 1 - c)

        def halves(cc):
            rows_i, rows_o = pl.ds(cc * hd, hd), pl.ds(cc * hr, hr)
            return (fi_ref.at[rows_i, :], gi_ref.at[rows_i, :]), (fo_ref.at[:, rows_o, :], go_ref.at[:, rows_o, :])

        def copies(cc):
            return [pltpu.make_async_remote_copy(src_ref=src, dst_ref=dst, send_sem=send_sems.at[n],
                                                 recv_sem=recv_sems.at[n], device_id=sibling, device_id_type=MESH)
                    for n, (src, dst) in enumerate(halves(cc))]

        sends = copies(c)
        for cp in sends:
            cp.start()
        for cp in copies(1 - c):
            cp.wait_recv()
        for cp in sends:
            cp.wait_send()

    return pl.pallas_call(
        body, name="rs_share_halves",
        in_specs=[ANY, ANY], out_specs=(ANY, ANY),
        out_shape=(jax.ShapeDtypeStruct(f_in.shape, F32), jax.ShapeDtypeStruct(f_o.shape, F32)),
        scratch_shapes=[pltpu.SemaphoreType.DMA((2,)), pltpu.SemaphoreType.DMA((2,))],
        input_output_aliases={0: 0, 1: 1},
        compiler_params=pltpu.CompilerParams(has_side_effects=True),
    )(f_in, f_o)


def _merge_forward(ya, yb, wo3, p, bg, tm, tn):
    t = ya.shape[0]

    def body(ya_ref, yb_ref, wa_ref, wb_ref, g_ref, bg_ref, m_ref, oab_ref):
        oa = _dot(ya_ref[...], wa_ref[...])
        ob = _dot(yb_ref[...], wb_ref[...])
        ga = _sigmoid(g_ref[0].astype(F32) + bg_ref[0])
        gb = _sigmoid(g_ref[1].astype(F32) + bg_ref[1])
        m_ref[...] = (ga * oa + gb * ob).astype(BF16)
        oab_ref[0] = oa.astype(BF16)
        oab_ref[1] = ob.astype(BF16)

    return pl.pallas_call(
        body, name="merge_forward", grid=(t // tm, D_MODEL // tn),
        in_specs=[pl.BlockSpec((tm, D_MODEL), lambda i, j: (i, 0)),
                  pl.BlockSpec((tm, D_MODEL), lambda i, j: (i, 0)),
                  pl.BlockSpec((None, D_MODEL, tn), lambda i, j: (0, 0, j)),
                  pl.BlockSpec((None, D_MODEL, tn), lambda i, j: (1, 0, j)),
                  pl.BlockSpec((2, tm, tn), lambda i, j: (BLOCK_G, i, j)),
                  pl.BlockSpec((2, 1, tn), lambda i, j: (0, 0, j))],
        out_specs=(pl.BlockSpec((tm, tn), lambda i, j: (i, j)),
                   pl.BlockSpec((2, tm, tn), lambda i, j: (0, i, j))),
        out_shape=(jax.ShapeDtypeStruct((t, D_MODEL), BF16), jax.ShapeDtypeStruct((2, t, D_MODEL), BF16)),
        compiler_params=_params("parallel", "parallel"),
    )(ya, yb, wo3, wo3, p, bg)


HEAD_ROWS = 256


def _head(merged, wo3, x, target, ln_g, ln_b, tm):
    t = x.shape[0]
    inv_d = 1.0 / D_MODEL

    def body(m_ref, w_ref, x_ref, t_ref, g_ref, b_ref, dr_ref, gx_ref, dm_ref, dg_ref, db_ref, loss_ref):
        i = pl.program_id(0)

        @pl.when(i == 0)
        def _():
            dg_ref[...] = jnp.zeros_like(dg_ref)
            db_ref[...] = jnp.zeros_like(db_ref)
            loss_ref[...] = jnp.zeros_like(loss_ref)

        w = w_ref[...]
        g = g_ref[...]
        tiles = [slice(r0, r0 + HEAD_ROWS) for r0 in range(0, tm, HEAD_ROWS)]
        firsts = [_dot(m_ref[rows, :], w) for rows in tiles]
        for rows, out in zip(tiles, firsts):
            r = DN_ALPHA * x_ref[rows, :] + out
            mu = jnp.mean(r, axis=-1, keepdims=True)
            xc = r - mu
            var = jnp.mean(xc * xc, axis=-1, keepdims=True)
            rstd = lax.rsqrt(var + LN_EPS)
            xhat = xc * rstd
            e = xhat * g + b_ref[...] - t_ref[rows, :]
            se = jnp.sum(jnp.sum(e * e, axis=1, keepdims=True), axis=0, keepdims=True)
            loss_ref[...] += jnp.broadcast_to((0.5 * inv_d) * se, loss_ref.shape)
            dy = e * inv_d
            db_ref[...] += jnp.sum(dy, axis=0, keepdims=True)
            dg_ref[...] += jnp.sum(dy * xhat, axis=0, keepdims=True)
            dxh = dy * g
            m1 = jnp.mean(dxh, axis=-1, keepdims=True)
            m2 = jnp.mean(dxh * xhat, axis=-1, keepdims=True)
            dr = rstd * (dxh - m1 - xhat * m2)
            gx_ref[rows, :] = DN_ALPHA * dr
            drb = dr.astype(BF16)
            dr_ref[rows, :] = drb
            dm_ref[rows, :] = _dot(drb, w, 1, 1).astype(BF16)

    row = pl.BlockSpec((tm, D_MODEL), lambda i: (i, 0))
    vec = pl.BlockSpec((1, D_MODEL), lambda i: (0, 0))
    return pl.pallas_call(
        body, name="head", grid=(t // tm,),
        in_specs=[row, pl.BlockSpec((None, D_MODEL, D_MODEL), lambda i: (2, 0, 0), pipeline_mode=pl.Buffered(1)),
                  row, row, vec, vec],
        out_specs=(row, row, row, vec, vec, pl.BlockSpec((1, 128), lambda i: (0, 0))),
        out_shape=(jax.ShapeDtypeStruct((t, D_MODEL), BF16), jax.ShapeDtypeStruct((t, D_MODEL), F32),
                   jax.ShapeDtypeStruct((t, D_MODEL), BF16), jax.ShapeDtypeStruct((1, D_MODEL), F32),
                   jax.ShapeDtypeStruct((1, D_MODEL), F32), jax.ShapeDtypeStruct((1, 128), F32)),
        compiler_params=_params("arbitrary"),
    )(merged, wo3, x, target, ln_g, ln_b)


def _gate_and_branch_backward(dmerged, oab, p, bg, wo3, tm):
    t = dmerged.shape[0]

    def body(dm_ref, oab_ref, g_ref, bg_ref, wa_ref, wb_ref, do_ref, dpg_ref, dbg_ref, dy_ref):
        @pl.when(pl.program_id(0) == 0)
        def _():
            dbg_ref[...] = jnp.zeros_like(dbg_ref)

        dm = dm_ref[...].astype(F32)
        for n, w_ref in enumerate((wa_ref, wb_ref)):
            gate = _sigmoid(g_ref[n].astype(F32) + bg_ref[n])
            d_o = (dm * gate).astype(BF16)
            do_ref[n] = d_o
            dgate = dm * oab_ref[n].astype(F32) * gate * (1.0 - gate)
            dpg_ref[n] = dgate.astype(BF16)
            dbg_ref[n] += jnp.sum(dgate, axis=0, keepdims=True)
            dy_ref[n] = _dot(d_o, w_ref[...], 1, 1).astype(BF16)

    pair = pl.BlockSpec((2, tm, D_MODEL), lambda i: (0, i, 0))
    gates = pl.BlockSpec((2, tm, D_MODEL), lambda i: (BLOCK_G, i, 0))
    vec = pl.BlockSpec((2, 1, D_MODEL), lambda i: (0, 0, 0))

    def weight(n):
        return pl.BlockSpec((None, D_MODEL, D_MODEL), lambda i: (n, 0, 0), pipeline_mode=pl.Buffered(1))

    pair_shape = jax.ShapeDtypeStruct((2, t, D_MODEL), BF16)
    return pl.pallas_call(
        body, name="gate_and_branch_backward", grid=(t // tm,),
        in_specs=[pl.BlockSpec((tm, D_MODEL), lambda i: (i, 0)), pair, gates, vec, weight(0), weight(1)],
        out_specs=(pair, gates, vec, pair),
        out_shape=(pair_shape, jax.ShapeDtypeStruct((N_SLOTS, t, D_MODEL), BF16),
                   jax.ShapeDtypeStruct((2, 1, D_MODEL), F32), pair_shape),
        compiler_params=_params("arbitrary"),
    )(dmerged, oab, p, bg, wo3, wo3)


def _weight_grad(a, a_sel, b, b_sel, tm, tn, tk, name, exchange=None):
    t = a.shape[-2]
    nk = t // tk
    ni, nj = D_MODEL // tm, D_MODEL // tn
    hr = SHARD_ROWS // 2
    if exchange is not None:
        g_in, *g_sq = exchange
        g_sq4 = [g.reshape(N_CHIPS, 2, hr, D_MODEL) for g in g_sq]

    def body(a_ref, b_ref, *rest):
        if exchange is None:
            o_ref, acc_ref = rest
        else:
            n_ex = 1 + len(g_sq4)
            g_refs = rest[:n_ex]
            o_ref = rest[n_ex]
            r_refs = rest[n_ex + 1:2 * n_ex + 1]
            acc_ref, send_sems, recv_sems = rest[2 * n_ex + 1:]
        i, j, k = pl.program_id(0), pl.program_id(1), pl.program_id(2)

        if exchange is not None:
            x, y, c, _ = _mesh_pos()
            hd = g_in.shape[0] // 2
            srcs = [g_refs[0].at[pl.ds((1 - c) * hd, hd), :]] + [g.at[:, 1 - c] for g in g_refs[1:]]
            swaps = [pltpu.make_async_remote_copy(src_ref=src, dst_ref=dst, send_sem=send_sems.at[n],
                                                  recv_sem=recv_sems.at[n], device_id=(x, y, 1 - c),
                                                  device_id_type=MESH)
                     for n, (src, dst) in enumerate(zip(srcs, r_refs))]

            @pl.when(jnp.logical_and(i == 0, jnp.logical_and(j == 0, k == 0)))
            def _():
                for swap in swaps:
                    swap.start()

        @pl.when(k == 0)
        def _():
            acc_ref[...] = jnp.zeros_like(acc_ref)

        acc_ref[...] += _dot(a_ref[...], b_ref[...], 0, 0)

        @pl.when(k == nk - 1)
        def _():
            o_ref[...] = acc_ref[...].astype(BF16)

        if exchange is not None:
            @pl.when(jnp.logical_and(i == ni - 1, jnp.logical_and(j == nj - 1, k == nk - 1)))
            def _():
                for swap in swaps:
                    swap.wait()

    def spec(arr, sel, width, which):
        if arr.ndim == 2:
            return pl.BlockSpec((tk, width), lambda i, j, k: (k, (i, j)[which]))
        return pl.BlockSpec((None, tk, width), lambda i, j, k: (sel, k, (i, j)[which]))

    o_spec = pl.BlockSpec((tm, tn), lambda i, j, k: (i, j))
    o_shape = jax.ShapeDtypeStruct((D_MODEL, D_MODEL), BF16)
    if exchange is None:
        return pl.pallas_call(
            body, name=name, grid=(ni, nj, nk),
            in_specs=[spec(a, a_sel, tm, 0), spec(b, b_sel, tn, 1)],
            out_specs=o_spec, out_shape=o_shape,
            scratch_shapes=[pltpu.VMEM((tm, tn), F32)],
            compiler_params=_params("parallel", "parallel", "arbitrary"),
        )(a, b)
    n_ex = 1 + len(g_sq4)
    r_shapes = [jax.ShapeDtypeStruct((g_in.shape[0] // 2, g_in.shape[1]), BF16)]
    r_shapes += [jax.ShapeDtypeStruct((N_CHIPS, hr, D_MODEL), BF16)] * len(g_sq4)
    return pl.pallas_call(
        body, name=name, grid=(ni, nj, nk),
        in_specs=[spec(a, a_sel, tm, 0), spec(b, b_sel, tn, 1)] + [ANY] * n_ex,
        out_specs=(o_spec,) + (ANY,) * n_ex,
        out_shape=(o_shape, *r_shapes),
        scratch_shapes=[pltpu.VMEM((tm, tn), F32), pltpu.SemaphoreType.DMA((n_ex,)),
                        pltpu.SemaphoreType.DMA((n_ex,))],
        compiler_params=pltpu.CompilerParams(dimension_semantics=("arbitrary",) * 3, vmem_limit_bytes=VMEM_LIMIT,
                                             has_side_effects=True),
    )(a, b, g_in, *g_sq4)


def _win_grad(xt, dp, tn, tk):
    _, t, _ = dp.shape
    nk = t // tk
    per_seg = D_MODEL // tn
    nj = N_SEG * per_seg

    def body(x_ref, dp_ref, o_ref, acc_ref):
        k = pl.program_id(1)

        @pl.when(k == 0)
        def _():
            acc_ref[...] = jnp.zeros_like(acc_ref)

        acc_ref[...] += _dot(x_ref[...], dp_ref[...])

        @pl.when(k == nk - 1)
        def _():
            o_ref[...] = acc_ref[...].astype(BF16)

    return pl.pallas_call(
        body, name="grad_w_in", grid=(nj, nk),
        in_specs=[pl.BlockSpec((D_MODEL, tk), lambda j, k: (0, k)),
                  pl.BlockSpec((None, tk, tn), lambda j, k: (_slot_of_seg(j // per_seg), k, j % per_seg))],
        out_specs=pl.BlockSpec((D_MODEL, tn), lambda j, k: (0, j)),
        out_shape=jax.ShapeDtypeStruct((D_MODEL, N_SEG * D_MODEL), BF16),
        scratch_shapes=[pltpu.VMEM((D_MODEL, tn), F32)],
        compiler_params=_params("parallel", "arbitrary"),
    )(xt, dp)


def _input_grad_and_scatter(dp, w_full, gx, s_in, s_o, small, tm):
    _, t, _ = dp.shape
    ni, nk = t // tm, N_SEG - TAIL_SEGS
    _, hd, sc = s_in.shape
    hr = s_o.shape[2]

    def body(dp_ref, w_ref, gx_ref, si_ref, so_ref, sm_ref, o_ref, ri_ref, ro_ref, ga_ref,
             send_sems, recv_sems, local_sem):
        i, k = pl.program_id(0), pl.program_id(1)
        x, y, c, chips = _mesh_pos()
        me = 2 * x + y
        dev = 4 * x + 2 * y + c

        def peer(r):
            return (x ^ ((r >> 2) & 1), y ^ ((r >> 1) & 1), c ^ (r & 1))

        def sends():
            cps = []
            for q, (cx, cy) in enumerate(chips):
                dest = 2 * cx + cy
                cps.append(pltpu.make_async_remote_copy(src_ref=si_ref.at[dest], dst_ref=ri_ref.at[me],
                                                        send_sem=send_sems.at[q], recv_sem=recv_sems.at[q],
                                                        device_id=(cx, cy, c), device_id_type=MESH))
                cps.append(pltpu.make_async_remote_copy(src_ref=so_ref.at[:, dest], dst_ref=ro_ref.at[me],
                                                        send_sem=send_sems.at[3 + q], recv_sem=recv_sems.at[3 + q],
                                                        device_id=(cx, cy, c), device_id_type=MESH))
            for r in range(1, 8):
                cps.append(pltpu.make_async_remote_copy(src_ref=sm_ref, dst_ref=ga_ref.at[dev],
                                                        send_sem=send_sems.at[5 + r], recv_sem=recv_sems.at[5 + r],
                                                        device_id=peer(r), device_id_type=MESH))
            return cps

        own_small = pltpu.make_async_copy(sm_ref, ga_ref.at[dev], local_sem)

        @pl.when(jnp.logical_and(i == 0, k == 0))
        def _():
            for cp in sends():
                cp.start()
            own_small.start()

        @pl.when(k == 0)
        def _():
            o_ref[...] = gx_ref[...]

        o_ref[...] += _dot(dp_ref[...], w_ref[...], 1, 1)

        @pl.when(jnp.logical_and(i == ni - 1, k == nk - 1))
        def _():
            for q, (cx, cy) in enumerate(chips):
                frm = 2 * cx + cy
                pltpu.make_async_remote_copy(src_ref=si_ref.at[frm], dst_ref=ri_ref.at[frm], send_sem=send_sems.at[q],
                                             recv_sem=recv_sems.at[q], device_id=(x, y, c),
                                             device_id_type=MESH).wait_recv()
                pltpu.make_async_remote_copy(src_ref=so_ref.at[:, frm], dst_ref=ro_ref.at[frm],
                                             send_sem=send_sems.at[3 + q], recv_sem=recv_sems.at[3 + q],
                                             device_id=(x, y, c), device_id_type=MESH).wait_recv()
            for r in range(1, 8):
                px, py, pc = peer(r)
                pltpu.make_async_remote_copy(src_ref=sm_ref, dst_ref=ga_ref.at[4 * px + 2 * py + pc],
                                             send_sem=send_sems.at[5 + r], recv_sem=recv_sems.at[5 + r],
                                             device_id=(x, y, c), device_id_type=MESH).wait_recv()
            for cp in sends():
                cp.wait_send()
            own_small.wait()

    return pl.pallas_call(
        body, name="grad_x_and_scatter", grid=(ni, nk),
        in_specs=[pl.BlockSpec((None, tm, D_MODEL), lambda i, k: (_slot_of_seg(k), i, 0)),
                  pl.BlockSpec((D_MODEL, D_MODEL), lambda i, k: (0, k)),
                  pl.BlockSpec((tm, D_MODEL), lambda i, k: (i, 0), pipeline_mode=pl.Buffered(1)), ANY, ANY, ANY],
        out_specs=(pl.BlockSpec((tm, D_MODEL), lambda i, k: (i, 0)), ANY, ANY, ANY),
        out_shape=(jax.ShapeDtypeStruct((t, D_MODEL), F32),
                   jax.ShapeDtypeStruct((N_CHIPS, hd, sc), BF16),
                   jax.ShapeDtypeStruct((N_CHIPS, 3, hr, D_MODEL), BF16),
                   jax.ShapeDtypeStruct((8,) + small.shape, small.dtype)),
        scratch_shapes=[pltpu.SemaphoreType.DMA((13,)), pltpu.SemaphoreType.DMA((13,)), pltpu.SemaphoreType.DMA],
        input_output_aliases={2: 0},
        compiler_params=pltpu.CompilerParams(dimension_semantics=("arbitrary", "arbitrary"),
                                             vmem_limit_bytes=VMEM_LIMIT, has_side_effects=True),
    )(dp, w_full, gx, s_in, s_o, small)


TAIL_SEGS = 1


def _input_grad_tail_and_adam(dp, w_full, partial, w, g, m, v, steps):
    _, t, _ = dp.shape
    tm = t // steps
    r, c = w.shape
    ra = r // steps
    segs = tuple(range(N_SEG - TAIL_SEGS, N_SEG))

    def body(*refs):
        dp_refs = refs[:TAIL_SEGS]
        w_refs = refs[TAIL_SEGS:2 * TAIL_SEGS]
        part_ref, aw_ref, ag_ref, am_ref, av_ref, o_ref, go_ref, d_ref, mo_ref, vo_ref = refs[2 * TAIL_SEGS:]
        acc = part_ref[...]
        for dp_ref, w_ref in zip(dp_refs, w_refs):
            acc = acc + _dot(dp_ref[...], w_ref[...], 1, 1)
        o_ref[...] = acc
        gv = ag_ref[...]
        d, mn, vn = _adam_math(aw_ref[...], gv, am_ref[...], av_ref[...])
        go_ref[...] = gv
        d_ref[...] = d
        mo_ref[...] = mn
        vo_ref[...] = vn

    row = pl.BlockSpec((tm, D_MODEL), lambda i: (i, 0))
    arow = pl.BlockSpec((ra, c), lambda i: (i, 0))
    ashape = jax.ShapeDtypeStruct((r, c), F32)
    in_specs = [pl.BlockSpec((None, tm, D_MODEL), functools.partial(lambda s, i: (SLOT_OF_SEG[s], i, 0), s))
                for s in segs]
    in_specs += [pl.BlockSpec((D_MODEL, D_MODEL), functools.partial(lambda s, i: (0, s), s),
                              pipeline_mode=pl.Buffered(1)) for s in segs]
    in_specs += [row, arow, arow, arow, arow]
    res = pl.pallas_call(
        body, name="grad_x_tail_and_adam_w_in", grid=(steps,),
        in_specs=in_specs,
        out_specs=(row, arow, arow, arow, arow),
        out_shape=(jax.ShapeDtypeStruct((t, D_MODEL), F32), ashape, ashape, ashape, ashape),
        compiler_params=_params("parallel"),
    )(*([dp] * TAIL_SEGS), *([w_full] * TAIL_SEGS), partial, w, g, m, v)
    return res[0], res[1:]


def _sgu_chunk_forward(u, v, z, wm, bs, lng, lnb):
    ug, dug = _gelu_and_grad(u)
    vg, dvg = _gelu_and_grad(v)
    mu = jnp.mean(vg, axis=-1, keepdims=True)
    xc = vg - mu
    var = jnp.mean(xc * xc, axis=-1, keepdims=True)
    rstd = lax.rsqrt(var + LN_EPS)
    vhat = xc * rstd
    vln = (vhat * lng + lnb).astype(BF16)
    mixed = _dot(wm, vln) + bs
    sig = _sigmoid(z)
    return ug, dug, dvg, rstd, vhat, vln, mixed, sig


def _mixer_a_forward(p_a, wm, bs_col, ln_v_g, ln_v_b, tm):
    t = p_a.shape[1]

    def body(p_ref, wm_ref, bs_ref, g_ref, b_ref, o_ref):
        wm_v, bs_v, lng, lnb = wm_ref[...], bs_ref[...], g_ref[...], b_ref[...]

        def chunk(ci, carry):
            rows = pl.ds(pl.multiple_of(ci * CHUNK, CHUNK), CHUNK)
            u = p_ref[0, rows, :].astype(F32)
            v = p_ref[1, rows, :].astype(F32)
            z = p_ref[2, rows, :].astype(F32)
            ug, _, _, _, _, _, mixed, sig = _sgu_chunk_forward(u, v, z, wm_v, bs_v, lng, lnb)
            o_ref[rows, :] = (ug * mixed * (z * sig)).astype(BF16)
            return carry

        lax.fori_loop(0, tm // CHUNK, chunk, 0, unroll=True)

    return pl.pallas_call(
        body, name="mixer_a_forward", grid=(t // tm, N_HEADS),
        in_specs=[pl.BlockSpec((3, tm, HEAD_DIM), lambda i, h: (0, i, h)),
                  pl.BlockSpec((None, CHUNK, CHUNK), lambda i, h: (h, 0, 0)),
                  pl.BlockSpec((None, CHUNK, 1), lambda i, h: (h, 0, 0)),
                  pl.BlockSpec((1, HEAD_DIM), lambda i, h: (0, h)),
                  pl.BlockSpec((1, HEAD_DIM), lambda i, h: (0, h))],
        out_specs=pl.BlockSpec((tm, HEAD_DIM), lambda i, h: (i, h)),
        out_shape=jax.ShapeDtypeStruct((t, D_MODEL), BF16),
        compiler_params=_params("parallel", "parallel"),
    )(p_a, wm, bs_col, ln_v_g, ln_v_b)


def _mixer_a_backward(p_a, dyab, wm, bs_col, ln_v_g, ln_v_b, dp, tm):
    t = p_a.shape[1]

    def body(p_ref, dy_ref, wm_ref, bs_ref, g_ref, b_ref, dp_in, dp_ref, dws_ref, dbs_ref, dg_ref, db_ref):
        @pl.when(pl.program_id(1) == 0)
        def _():
            dws_ref[...] = jnp.zeros_like(dws_ref)
            dbs_ref[...] = jnp.zeros_like(dbs_ref)
            dg_ref[...] = jnp.zeros_like(dg_ref)
            db_ref[...] = jnp.zeros_like(db_ref)

        wm_v, bs_v, lng, lnb = wm_ref[...], bs_ref[...], g_ref[...], b_ref[...]
        causal = (lax.broadcasted_iota(jnp.int32, (CHUNK, CHUNK), 1)
                  <= lax.broadcasted_iota(jnp.int32, (CHUNK, CHUNK), 0))

        def chunk(ci, carry):
            rows = pl.ds(pl.multiple_of(ci * CHUNK, CHUNK), CHUNK)
            u = p_ref[0, rows, :].astype(F32)
            v = p_ref[1, rows, :].astype(F32)
            z = p_ref[2, rows, :].astype(F32)
            dy = dy_ref[rows, :].astype(F32)
            ug, dug, dvg, rstd, vhat, vln, mixed, sig = _sgu_chunk_forward(u, v, z, wm_v, bs_v, lng, lnb)
            sz = z * sig
            dmixed = dy * ug * sz
            dp_ref[0, rows, :] = (dy * mixed * sz * dug).astype(BF16)
            dp_ref[2, rows, :] = (dy * ug * mixed * _silu_grad(sig, sz)).astype(BF16)
            dbs_ref[...] += jnp.sum(dmixed, axis=1, keepdims=True)
            dmb = dmixed.astype(BF16)
            dws_ref[...] += jnp.where(causal, _dot(dmb, vln, 1, 1), 0.0)
            dvln = _dot(wm_v, dmb, 0, 0)
            db_ref[...] += jnp.sum(dvln, axis=0, keepdims=True)
            dg_ref[...] += jnp.sum(dvln * vhat, axis=0, keepdims=True)
            dvh = dvln * lng
            m1 = jnp.mean(dvh, axis=-1, keepdims=True)
            m2 = jnp.mean(dvh * vhat, axis=-1, keepdims=True)
            dp_ref[1, rows, :] = (rstd * (dvh - m1 - vhat * m2) * dvg).astype(BF16)
            return carry

        lax.fori_loop(0, tm // CHUNK, chunk, 0, unroll=True)

    return pl.pallas_call(
        body, name="mixer_a_backward", grid=(N_HEADS, t // tm),
        in_specs=[pl.BlockSpec((3, tm, HEAD_DIM), lambda h, i: (0, i, h)),
                  pl.BlockSpec((None, tm, HEAD_DIM), lambda h, i: (0, i, h)),
                  pl.BlockSpec((None, CHUNK, CHUNK), lambda h, i: (h, 0, 0)),
                  pl.BlockSpec((None, CHUNK, 1), lambda h, i: (h, 0, 0)),
                  pl.BlockSpec((1, HEAD_DIM), lambda h, i: (0, h)),
                  pl.BlockSpec((1, HEAD_DIM), lambda h, i: (0, h)), ANY],
        out_specs=(pl.BlockSpec((3, tm, HEAD_DIM), lambda h, i: (BLOCK_A, i, h)),
                   pl.BlockSpec((None, CHUNK, CHUNK), lambda h, i: (h, 0, 0)),
                   pl.BlockSpec((None, CHUNK, 1), lambda h, i: (h, 0, 0)),
                   pl.BlockSpec((1, HEAD_DIM), lambda h, i: (0, h)),
                   pl.BlockSpec((1, HEAD_DIM), lambda h, i: (0, h))),
        out_shape=(jax.ShapeDtypeStruct(dp.shape, BF16),
                   jax.ShapeDtypeStruct((N_HEADS, CHUNK, CHUNK), F32),
                   jax.ShapeDtypeStruct((N_HEADS, CHUNK, 1), F32),
                   jax.ShapeDtypeStruct((1, D_MODEL), F32), jax.ShapeDtypeStruct((1, D_MODEL), F32)),
        input_output_aliases={6: 0},
        compiler_params=_params("parallel", "arbitrary"),
    )(p_a, dyab, wm, bs_col, ln_v_g, ln_v_b, dp)


HALO = 16


def _conv_taps(h, halo_h, tm):
    row = lax.broadcasted_iota(jnp.int32, h.shape, 0)
    last1 = halo_h[HALO - 1:HALO, :]
    last2 = halo_h[HALO - 2:HALO - 1, :]
    h1 = jnp.where(row == 0, last1, pltpu.roll(h, 1, 0))
    h2 = jnp.where(row == 0, last2, jnp.where(row == 1, last1, pltpu.roll(h, 2, 0)))
    return h1, h2


def _mixer_b_forward(p_b, conv_w, conv_b, tm, tc):
    t = p_b.shape[1]

    def body(p_ref, halo_ref, w_ref, b_ref, o_ref):
        valid = (pl.program_id(1) > 0).astype(F32)
        h = p_ref[1].astype(F32) * p_ref[0].astype(F32)
        halo_h = halo_ref[1].astype(F32) * halo_ref[0].astype(F32) * valid
        h1, h2 = _conv_taps(h, halo_h, tm)
        w = w_ref[...]
        conv = b_ref[...] + w[0:1, :] * h2 + w[1:2, :] * h1 + w[2:3, :] * h
        z = p_ref[3].astype(F32)
        o_ref[...] = (p_ref[2].astype(F32) * conv * (z * _sigmoid(z))).astype(BF16)

    steps = tm // HALO
    return pl.pallas_call(
        body, name="mixer_b_forward", grid=(D_MODEL // tc, t // tm),
        in_specs=[pl.BlockSpec((4, tm, tc), lambda j, i: (BLOCK_B, i, j)),
                  pl.BlockSpec((4, HALO, tc), lambda j, i: (BLOCK_B, jnp.maximum(i * steps - 1, 0), j)),
                  pl.BlockSpec((3, tc), lambda j, i: (0, j)),
                  pl.BlockSpec((1, tc), lambda j, i: (0, j))],
        out_specs=pl.BlockSpec((tm, tc), lambda j, i: (i, j)),
        out_shape=jax.ShapeDtypeStruct((t, D_MODEL), BF16),
        compiler_params=_params("parallel", "parallel"),
    )(p_b, p_b, conv_w, conv_b)


def _mixer_b_backward(p_b, dyab, conv_w, conv_b, dp, tm, tc):
    t = p_b.shape[1]
    n = t // tm

    def body(p_ref, halo_ref, dy_ref, w_ref, b_ref, dp_in, dp_ref, dw_ref, db_ref, next_ref):
        ii = pl.program_id(1)

        @pl.when(ii == 0)
        def _():
            dw_ref[...] = jnp.zeros_like(dw_ref)
            db_ref[...] = jnp.zeros_like(db_ref)
            next_ref[...] = jnp.zeros_like(next_ref)

        valid = (ii < n - 1).astype(F32)
        xb = p_ref[0].astype(F32)
        cb = p_ref[1].astype(F32)
        bb = p_ref[2].astype(F32)
        z = p_ref[3].astype(F32)
        h = cb * xb
        halo_h = halo_ref[1].astype(F32) * halo_ref[0].astype(F32) * valid
        h1, h2 = _conv_taps(h, halo_h, tm)
        w = w_ref[...]
        w0, w1, w2 = w[0:1, :], w[1:2, :], w[2:3, :]
        conv = b_ref[...] + w0 * h2 + w1 * h1 + w2 * h
        sig = _sigmoid(z)
        sz = z * sig
        dy = dy_ref[...].astype(F32)
        dconv = dy * bb * sz
        dp_ref[2] = (dy * conv * sz).astype(BF16)
        dp_ref[3] = (dy * bb * conv * _silu_grad(sig, sz)).astype(BF16)
        db_ref[...] += jnp.sum(dconv, axis=0, keepdims=True)
        dw_ref[0:1, :] += jnp.sum(dconv * h2, axis=0, keepdims=True)
        dw_ref[1:2, :] += jnp.sum(dconv * h1, axis=0, keepdims=True)
        dw_ref[2:3, :] += jnp.sum(dconv * h, axis=0, keepdims=True)
        row = lax.broadcasted_iota(jnp.int32, h.shape, 0)
        nxt = next_ref[...]
        n0, n1 = nxt[0:1, :], nxt[1:2, :]
        d1 = jnp.where(row == tm - 1, n0, pltpu.roll(dconv, tm - 1, 0))
        d2 = jnp.where(row == tm - 1, n1, jnp.where(row == tm - 2, n0, pltpu.roll(dconv, tm - 2, 0)))
        dh = w2 * dconv + w1 * d1 + w0 * d2
        dp_ref[0] = (dh * cb).astype(BF16)
        dp_ref[1] = (dh * xb).astype(BF16)
        next_ref[...] = dconv[0:8, :]

    steps = tm // HALO
    return pl.pallas_call(
        body, name="mixer_b_backward", grid=(D_MODEL // tc, n),
        in_specs=[pl.BlockSpec((4, tm, tc), lambda j, ii: (BLOCK_B, n - 1 - ii, j)),
                  pl.BlockSpec((4, HALO, tc), lambda j, ii: (BLOCK_B, jnp.maximum((n - 1 - ii) * steps - 1, 0), j)),
                  pl.BlockSpec((None, tm, tc), lambda j, ii: (1, n - 1 - ii, j)),
                  pl.BlockSpec((3, tc), lambda j, ii: (0, j)),
                  pl.BlockSpec((1, tc), lambda j, ii: (0, j)), ANY],
        out_specs=(pl.BlockSpec((4, tm, tc), lambda j, ii: (BLOCK_B, n - 1 - ii, j)),
                   pl.BlockSpec((3, tc), lambda j, ii: (0, j)),
                   pl.BlockSpec((1, tc), lambda j, ii: (0, j))),
        out_shape=(jax.ShapeDtypeStruct(dp.shape, BF16),
                   jax.ShapeDtypeStruct((3, D_MODEL), F32), jax.ShapeDtypeStruct((1, D_MODEL), F32)),
        scratch_shapes=[pltpu.VMEM((8, tc), F32)],
        input_output_aliases={5: 0},
        compiler_params=_params("parallel", "arbitrary"),
    )(p_b, p_b, dyab, conv_w, conv_b, dp)


def _adam_math(w, g, m, v):
    m = ADAM_B1 * m + (1.0 - ADAM_B1) * g
    v = ADAM_B2 * v + (1.0 - ADAM_B2) * (g * g)
    delta = -ADAM_LR * ((m * ADAM_C1) / (jnp.sqrt(v * ADAM_C2) + ADAM_EPS) + ADAM_WD * w)
    return delta, m, v


def _adam_rows(w, g, m, v, tm, name, g_sel=None):
    r, c = w.shape

    def body(w_ref, g_ref, m_ref, v_ref, go_ref, d_ref, mo_ref, vo_ref):
        g = g_ref[...]
        d, mn, vn = _adam_math(w_ref[...], g, m_ref[...], v_ref[...])
        go_ref[...] = g
        d_ref[...] = d
        mo_ref[...] = mn
        vo_ref[...] = vn

    spec = pl.BlockSpec((tm, c), lambda i: (i, 0))
    g_spec = spec if g_sel is None else pl.BlockSpec((None, tm, c), lambda i: (g_sel, i, 0))
    shape = jax.ShapeDtypeStruct((r, c), F32)
    return pl.pallas_call(
        body, name=name, grid=(r // tm,),
        in_specs=[spec, g_spec, spec, spec], out_specs=(spec,) * 4, out_shape=(shape,) * 4,
        compiler_params=_params("parallel"),
    )(w, g, m, v)


SMALL_ROW0 = {name: sum(r for _, r in SMALL_ROWS[:i]) for i, (name, _) in enumerate(SMALL_ROWS)}
LANE_MAJOR = ("ln_g", "ln_b", "b_gate", "ln_v_g", "ln_v_b", "conv_b")


def _lane_pieces(n):
    return [(q, slice(q * 128, (q + 1) * 128)) for q in range(n // 128)]


def _pack_small(d_ln_g, d_ln_b, d_bg, d_lnv_g, d_lnv_b, d_ws, d_bs, d_cw, d_cb, loss_part):
    def body(lg, lb, bg, vg, vb, ws, bs, cw, cb, loss, o_ref):
        def put(row0, vec):
            for q, cols in _lane_pieces(vec.shape[1]):
                o_ref[row0 + q:row0 + q + 1, :] = vec[:, cols]

        put(SMALL_ROW0["ln_g"], lg[...])
        put(SMALL_ROW0["ln_b"], lb[...])
        for n in range(2):
            put(SMALL_ROW0["b_gate"] + n * (D_MODEL // 128), bg[n])
        put(SMALL_ROW0["ln_v_g"], vg[...])
        put(SMALL_ROW0["ln_v_b"], vb[...])
        for h in range(N_HEADS):
            o_ref[SMALL_ROW0["w_s"] + h * CHUNK:SMALL_ROW0["w_s"] + (h + 1) * CHUNK, :] = ws[h]
        o_ref[SMALL_ROW0["b_s"]:SMALL_ROW0["b_s"] + N_HEADS, :] = bs[...]
        for c in range(3):
            put(SMALL_ROW0["conv_w"] + c * (D_MODEL // 128), cw[c:c + 1, :])
        put(SMALL_ROW0["conv_b"], cb[...])
        o_ref[SMALL_ROW0["loss"]:SMALL_ROW0["loss"] + 8, :] = jnp.broadcast_to(loss[...], (8, 128))

    return pl.pallas_call(
        body, name="pack_small", out_shape=jax.ShapeDtypeStruct((SMALL_TOTAL, 128), F32), compiler_params=_params(),
    )(d_ln_g, d_ln_b, d_bg, d_lnv_g, d_lnv_b, d_ws, d_bs.reshape(N_HEADS, CHUNK), d_cw, d_cb, loss_part)


def _adam_small(gathered, params):
    names = list(params)
    flat = [a for n in names for a in params[n]]

    def body(*refs):
        ga_ref = refs[0]
        ins = refs[1:1 + 3 * len(names)]
        outs = refs[1 + 3 * len(names):-1]
        gs_ref = refs[-1]
        g = ga_ref[0]
        for k in range(1, 8):
            g = g + ga_ref[k]
        gs_ref[...] = g
        for i, name in enumerate(names):
            w_ref, m_ref, v_ref = ins[3 * i:3 * i + 3]
            o_refs = outs[4 * i:4 * i + 4]
            row0 = SMALL_ROW0[name]
            if name in LANE_MAJOR:
                pieces = [((slice(None), cols), slice(row0 + q, row0 + q + 1))
                          for q, cols in _lane_pieces(w_ref.shape[1])]
            elif name == "w_s":
                pieces = [((0, h), slice(row0 + h * CHUNK, row0 + (h + 1) * CHUNK)) for h in range(N_HEADS)]
            else:
                pieces = [((0,), slice(row0, row0 + N_HEADS))]
            for idx, rows in pieces:
                gp = gs_ref[rows, :]
                res = (gp,) + _adam_math(w_ref[idx], gp, m_ref[idx], v_ref[idx])
                for o_ref, val in zip(o_refs, res):
                    o_ref[idx] = val
        gcw_ref, loss_ref = outs[-2:]
        for c in range(3):
            for q, cols in _lane_pieces(D_MODEL):
                r = SMALL_ROW0["conv_w"] + c * (D_MODEL // 128) + q
                gcw_ref[c:c + 1, cols] = gs_ref[r:r + 1, :]
        loss_ref[...] = gs_ref[SMALL_ROW0["loss"]:SMALL_ROW0["loss"] + 1, :]

    out_shape = [jax.ShapeDtypeStruct(params[n][0].shape, F32) for n in names for _ in range(4)]
    out_shape += [jax.ShapeDtypeStruct((3, D_MODEL), F32), jax.ShapeDtypeStruct((1, 128), F32)]
    res = pl.pallas_call(
        body, name="adam_small", out_shape=tuple(out_shape),
        scratch_shapes=[pltpu.VMEM((SMALL_TOTAL, 128), F32)], compiler_params=_params(),
    )(gathered, *flat)
    return {n: res[4 * i:4 * i + 4] for i, n in enumerate(names)}, res[-2], res[-1]


def _adam_conv_w(g_all, chip, w, m, v):
    cols = w.shape[2]

    def body(c_ref, g_ref, w_ref, m_ref, v_ref, go_ref, d_ref, mo_ref, vo_ref):
        g = g_ref[...]
        d, mn, vn = _adam_math(w_ref[...], g, m_ref[...], v_ref[...])
        go_ref[...] = g
        d_ref[...] = d
        mo_ref[...] = mn
        vo_ref[...] = vn

    own = pl.BlockSpec((None, 3, cols), lambda i, c_ref: (0, 0, 0))
    return pl.pallas_call(
        body, name="adam_conv_w",
        grid_spec=pltpu.PrefetchScalarGridSpec(
            num_scalar_prefetch=1, grid=(1,),
            in_specs=[pl.BlockSpec((3, cols), lambda i, c_ref: (0, c_ref[0])), own, own, own],
            out_specs=(own,) * 4),
        out_shape=(jax.ShapeDtypeStruct(w.shape, F32),) * 4,
        compiler_params=_params("arbitrary"),
    )(chip, g_all, w, m, v)


def kernel(x, w_in, b_gate, ln_v_g, ln_v_b, w_s, b_s, conv_w, conv_b, w_oa, w_ob, w_out, ln_g, ln_b, loss_target, m_w_in, m_b_gate, m_ln_v_g, m_ln_v_b, m_w_s, m_b_s, m_conv_w, m_conv_b, m_w_oa, m_w_ob, m_w_out, m_ln_g, m_ln_b, v_w_in, v_b_gate, v_ln_v_g, v_ln_v_b, v_w_s, v_b_s, v_conv_w, v_conv_b, v_w_oa, v_w_ob, v_w_out, v_ln_g, v_ln_b):
    t = x.shape[1]
    x2 = x[0]
    target = loss_target[0]
    chip = 2 * lax.axis_index("x") + lax.axis_index("y")
    conv_cols = conv_w.shape[2]

    chip1 = chip.astype(jnp.int32).reshape(1)
    w_pre = _cast_into_columns(w_in[0], chip1, N_CHIPS, 512, "cast_w_in")
    wo_b, wm = _prep_small_weights(w_oa[0], w_ob[0], w_out[0], w_s[0])
    conv_w8 = jnp.concatenate([conv_w[0], jnp.zeros((5, conv_cols), F32)], axis=0)
    bs_col = b_s[0].reshape(N_HEADS, CHUNK, 1)
    bg = b_gate.reshape(2, 1, D_MODEL)

    xb, xt = _cast_and_transpose(x2, min(1024, t))
    p, w_full, wo_full, cw_full = _gather_and_project(xb, w_pre, wo_b, conv_w8, min(4096, t))
    wo3 = wo_full.reshape(3, D_MODEL, D_MODEL)
    conv_w_all = jnp.transpose(cw_full[:, :3, :], (1, 0, 2)).reshape(3, D_MODEL)
    tm_a = min(2048, t)
    ya = _mixer_a_forward(p, wm, bs_col, ln_v_g, ln_v_b, tm_a)
    tm_b = min(1024, t)
    yb = _mixer_b_forward(p, conv_w_all, conv_b, min(2048, t), 512)
    tm_m = min(1024, t)
    merged, oab = _merge_forward(ya, yb, wo3, p, bg, tm_m, 512)

    drb, gx, dmerged, d_ln_g, d_ln_b, loss_part = _head(merged, wo3, x2, target, ln_g, ln_b, min(512, t))
    doab, dp, d_bg, dyab = _gate_and_branch_backward(dmerged, oab, p, bg, wo3, min(256, t))
    dp, d_ws, d_bs, d_lnv_g, d_lnv_b = _mixer_a_backward(p, dyab, wm, bs_col, ln_v_g, ln_v_b, dp, min(1024, t))
    dp, d_cw, d_cb = _mixer_b_backward(p, dyab, conv_w_all, conv_b, dp, tm_b, 512)

    tk = min(2048, t)
    g_in = _win_grad(xt, dp, 1024, tk)
    g_oa = _weight_grad(ya, 0, doab, 0, 1024, 1024, tk, "grad_w_oa")
    g_ob = _weight_grad(yb, 0, doab, 1, 1024, 1024, tk, "grad_w_ob")
    g_out, r_in, r_oa, r_ob = _weight_grad(merged, 0, drb, 0, 1024, 1024, tk, "grad_w_out",
                                           exchange=(g_in, g_oa, g_ob))
    s_in, s_o = _add_halves(g_in, (g_oa, g_ob, g_out), r_in, (r_oa, r_ob))
    small_part = _pack_small(d_ln_g, d_ln_b, d_bg, d_lnv_g, d_lnv_b, d_ws, d_bs, d_cw, d_cb, loss_part)
    gx, q_in, q_o, gathered = _input_grad_and_scatter(dp, w_full, gx, s_in, s_o, small_part, min(1024, t))
    f_in, f_o = _sum_chips(q_in, q_o, s_in, s_o)
    gsum_in, gsum_o = _share_halves(f_in, f_o)

    big = {}
    grad_x, big["w_in"] = _input_grad_tail_and_adam(dp, w_full, gx, w_in[0], gsum_in, m_w_in[0], v_w_in[0],
                                                    min(32, t // 128))
    for n, (name, w, m, v) in enumerate((("w_oa", w_oa, m_w_oa, v_w_oa), ("w_ob", w_ob, m_w_ob, v_w_ob),
                                         ("w_out", w_out, m_w_out, v_w_out))):
        big[name] = _adam_rows(w[0], gsum_o, m[0], v[0], 256, "adam_" + name, g_sel=n)

    small, g_conv_w, loss_row = _adam_small(gathered, {
        "ln_g": (ln_g, m_ln_g, v_ln_g), "ln_b": (ln_b, m_ln_b, v_ln_b), "b_gate": (b_gate, m_b_gate, v_b_gate),
        "ln_v_g": (ln_v_g, m_ln_v_g, v_ln_v_g), "ln_v_b": (ln_v_b, m_ln_v_b, v_ln_v_b),
        "w_s": (w_s, m_w_s, v_w_s), "b_s": (b_s, m_b_s, v_b_s), "conv_b": (conv_b, m_conv_b, v_conv_b)})
    small["conv_w"] = _adam_conv_w(g_conv_w, chip1, conv_w, m_conv_w, v_conv_w)
    loss = loss_row[0, 0]

    order = ("w_in", "b_gate", "ln_v_g", "ln_v_b", "w_s", "b_s", "conv_w", "conv_b", "w_oa", "w_ob", "w_out",
             "ln_g", "ln_b")
    outs = [loss, grad_x[None]]
    for which in range(4):
        for name in order:
            outs.append(big[name][which][None] if name in big else small[name][which])
    return tuple(outs)
```
